```python
import math
import jax
import jax.numpy as jnp
from jax import lax
import numpy as np

D_MODEL = 2048
BATCH = 8
SEQ = 4096
DEPTH = 4

N_MIXERS = 2
N_LAYERS_A = (DEPTH + 1) // 2
N_LAYERS_B = DEPTH // 2
CHUNK = 128
SGU_WIDTH = D_MODEL
SGU_GROUP = 128
SGU_HEADS = SGU_WIDTH // SGU_GROUP
SSM_WIDTH = D_MODEL
SSM_GROUP = 16
SSM_HEADS = SSM_WIDTH // SSM_GROUP
SSM_STATE = 64
DT_MIN = 1e-3
DT_MAX = 1e-1
FFN_HIDDEN = 5632
CONV_WIDTH = 3
EPS = 1e-6

kernel_name = 'hybrid_sgu_s5_convffn'


def rms_norm(x, g):
    xf = x.astype(jnp.float32)
    y = xf * lax.rsqrt(jnp.mean(xf * xf, axis=-1, keepdims=True) + EPS)
    return (y * g.astype(jnp.float32)).astype(x.dtype)


def chunked_sgu_mixer(h, w_in, g_v, w_s, b_s, w_out):
    bsz, seq, _ = h.shape
    z = jax.nn.gelu(h @ w_in)
    u, v = jnp.split(z, 2, axis=-1)
    v = rms_norm(v, g_v).reshape(bsz, seq // CHUNK, CHUNK, SGU_HEADS, SGU_GROUP)
    causal = jnp.tril(jnp.ones((CHUNK, CHUNK), dtype=bool))
    w = jnp.where(causal[None], w_s, jnp.zeros((), w_s.dtype))
    s = jnp.einsum('hts,bcshd->bcthd', w, v) + b_s.T[:, :, None]
    s = s.reshape(bsz, seq, SGU_WIDTH)
    return (u * s) @ w_out


def _cmul(ar, ai, br, bi):
    return ar * br - ai * bi, ar * bi + ai * br


def _scan_combine(earlier, later):
    a1r, a1i, b1r, b1i = earlier
    a2r, a2i, b2r, b2i = later
    ar, ai = _cmul(a2r, a2i, a1r, a1i)
    br, bi = _cmul(a2r, a2i, b1r, b1i)
    return ar, ai, br + b2r, bi + b2i


def s5_mixer(h, w_in, a_re, a_im, log_dt, b_re, b_im, c_re, c_im, d_skip, w_glu):
    f32 = jnp.float32
    bsz, seq, _ = h.shape
    n_chunks = seq // CHUNK
    u = (h @ w_in).astype(f32).reshape(bsz, n_chunks, CHUNK, SSM_HEADS, SSM_GROUP)
    u = u.transpose(1, 0, 2, 3, 4)
    dt = jnp.exp(log_dt.astype(f32))[:, None]
    lr, li = a_re.astype(f32), a_im.astype(f32)
    mag = jnp.exp(dt * lr)
    abar_r, abar_i = mag * jnp.cos(dt * li), mag * jnp.sin(dt * li)
    den = lr * lr + li * li
    qr = ((abar_r - 1.0) * lr + abar_i * li) / den
    qi = (abar_i * lr - (abar_r - 1.0) * li) / den
    bbar_r, bbar_i = _cmul(qr[..., None], qi[..., None], b_re.astype(f32), b_im.astype(f32))
    cr, ci = c_re.astype(f32), c_im.astype(f32)
    dd = d_skip.astype(f32).reshape(SSM_HEADS, SSM_GROUP)
    a_seq_r = jnp.broadcast_to(abar_r, (bsz, CHUNK, SSM_HEADS, SSM_STATE))
    a_seq_i = jnp.broadcast_to(abar_i, (bsz, CHUNK, SSM_HEADS, SSM_STATE))

    def chunk_step(carry, u_c):
        h0r, h0i = carry
        bur = jnp.einsum('gpc,btgc->btgp', bbar_r, u_c)
        bui = jnp.einsum('gpc,btgc->btgp', bbar_i, u_c)
        pr, pim, hr, hi = lax.associative_scan(
            _scan_combine, (a_seq_r, a_seq_i, bur, bui), axis=1)
        sr, si = _cmul(pr, pim, h0r[:, None], h0i[:, None])
        hr = hr + sr
        hi = hi + si
        y = (jnp.einsum('gcp,btgp->btgc', cr, hr)
             - jnp.einsum('gcp,btgp->btgc', ci, hi)
             + dd * u_c)
        return (hr[:, -1], hi[:, -1]), y

    init = (jnp.zeros((bsz, SSM_HEADS, SSM_STATE), f32),
            jnp.zeros((bsz, SSM_HEADS, SSM_STATE), f32))
    _, y = lax.scan(chunk_step, init, u)
    y = y.transpose(1, 0, 2, 3, 4).reshape(bsz, seq, SSM_WIDTH).astype(h.dtype)
    ga, gb = jnp.split(jax.nn.gelu(y) @ w_glu, 2, axis=-1)
    return ga * jax.nn.sigmoid(gb)


def conv_glu_ffn(h, w_up, conv_w, conv_b, w_down):
    seq = h.shape[1]
    z = h @ w_up
    zp = jnp.pad(z, ((0, 0), (CONV_WIDTH - 1, 0), (0, 0)))
    acc = conv_b + conv_w[CONV_WIDTH - 1] * zp[:, CONV_WIDTH - 1:CONV_WIDTH - 1 + seq]
    for k in range(CONV_WIDTH - 1):
        acc = acc + conv_w[k] * zp[:, k:k + seq]
    gate, val = jnp.split(acc, 2, axis=-1)
    return (jax.nn.silu(gate) * val) @ w_down


def _fwd_setup_inputs(seed: int = 0) -> dict:
    key = jax.random.key(seed)
    ks = jax.random.split(key, 24)
    f32 = jnp.float32
    d = D_MODEL
    na, nb = N_LAYERS_A, N_LAYERS_B

    def nrm(k, shape, scale):
        return jax.random.normal(k, shape, f32) * scale

    def gain(k, shape):
        return 1.0 + 0.01 * jax.random.normal(k, shape, f32)

    n_idx = jnp.arange(SSM_STATE, dtype=f32)
    return {
        'x': nrm(ks[0], (BATCH, SEQ, d), 1.0),
        'norm_mix_g': gain(ks[1], (DEPTH, d)),
        'norm_ffn_g': gain(ks[2], (DEPTH, d)),
        'a_w_in': nrm(ks[3], (na, d, 2 * SGU_WIDTH), d ** -0.5),
        'a_g_v': gain(ks[4], (na, SGU_WIDTH)),
        'a_w_s': nrm(ks[5], (na, SGU_HEADS, CHUNK, CHUNK), 0.5 * CHUNK ** -0.5),
        'a_b_s': gain(ks[6], (na, SGU_HEADS, CHUNK)),
        'a_w_out': nrm(ks[7], (na, SGU_WIDTH, d), SGU_WIDTH ** -0.5),
        'b_w_in': nrm(ks[8], (nb, d, SSM_WIDTH), d ** -0.5),
        'b_a_re': -0.5 + nrm(ks[9], (nb, SSM_HEADS, SSM_STATE), 0.01),
        'b_a_im': math.pi * n_idx + nrm(ks[10], (nb, SSM_HEADS, SSM_STATE), 0.01),
        'b_log_dt': jax.random.uniform(ks[11], (nb, SSM_HEADS), f32,
                                       minval=math.log(DT_MIN), maxval=math.log(DT_MAX)),
        'b_b_re': nrm(ks[12], (nb, SSM_HEADS, SSM_STATE, SSM_GROUP), (2 * SSM_GROUP) ** -0.5),
        'b_b_im': nrm(ks[13], (nb, SSM_HEADS, SSM_STATE, SSM_GROUP), (2 * SSM_GROUP) ** -0.5),
        'b_c_re': nrm(ks[14], (nb, SSM_HEADS, SSM_GROUP, SSM_STATE), (2 * SSM_STATE) ** -0.5),
        'b_c_im': nrm(ks[15], (nb, SSM_HEADS, SSM_GROUP, SSM_STATE), (2 * SSM_STATE) ** -0.5),
        'b_d': nrm(ks[16], (nb, SSM_WIDTH), 1.0),
        'b_w_glu': nrm(ks[17], (nb, SSM_WIDTH, 2 * d), SSM_WIDTH ** -0.5),
        'f_w_up': nrm(ks[18], (DEPTH, d, 2 * FFN_HIDDEN), d ** -0.5),
        'f_conv_w': nrm(ks[19], (DEPTH, CONV_WIDTH, 2 * FFN_HIDDEN), CONV_WIDTH ** -0.5),
        'f_conv_b': nrm(ks[20], (DEPTH, 2 * FFN_HIDDEN), 0.01),
        'f_w_down': nrm(ks[21], (DEPTH, FFN_HIDDEN, d), FFN_HIDDEN ** -0.5),
        'final_g': gain(ks[22], (d,)),
    }


def _fwd_reference(x, norm_mix_g, norm_ffn_g, a_w_in, a_g_v, a_w_s, a_b_s, a_w_out,
              b_w_in, b_a_re, b_a_im, b_log_dt, b_b_re, b_b_im, b_c_re, b_c_im, b_d, b_w_glu,
              f_w_up, f_conv_w, f_conv_b, f_w_down, final_g):
    h = x
    for i in range(DEPTH):
        j = i // N_MIXERS
        hn = rms_norm(h, norm_mix_g[i])
        if i % N_MIXERS == 0:
            h = h + chunked_sgu_mixer(hn, a_w_in[j], a_g_v[j], a_w_s[j], a_b_s[j], a_w_out[j])
        else:
            h = h + s5_mixer(hn, b_w_in[j], b_a_re[j], b_a_im[j], b_log_dt[j],
                             b_b_re[j], b_b_im[j], b_c_re[j], b_c_im[j], b_d[j], b_w_glu[j])
        h = h + conv_glu_ffn(rms_norm(h, norm_ffn_g[i]), f_w_up[i], f_conv_w[i],
                             f_conv_b[i], f_w_down[i])
    return rms_norm(h, final_g)


import jax as _jax
import jax.numpy as _jnp

TWIN_FORMAT = 'train_step'
FWD_PARAMS = ['x', 'norm_mix_g', 'norm_ffn_g', 'a_w_in', 'a_g_v', 'a_w_s', 'a_b_s', 'a_w_out', 'b_w_in', 'b_a_re', 'b_a_im', 'b_log_dt', 'b_b_re', 'b_b_im', 'b_c_re', 'b_c_im', 'b_d', 'b_w_glu', 'f_w_up', 'f_conv_w', 'f_conv_b', 'f_w_down', 'final_g']
TWIN_WEIGHTS = ['norm_mix_g', 'norm_ffn_g', 'a_w_in', 'a_g_v', 'a_w_s', 'a_b_s', 'a_w_out', 'b_w_in', 'b_a_re', 'b_a_im', 'b_log_dt', 'b_b_re', 'b_b_im', 'b_c_re', 'b_c_im', 'b_d', 'b_w_glu', 'f_w_up', 'f_conv_w', 'f_conv_b', 'f_w_down', 'final_g']
TWIN_DIFF_INPUT = 'x'
TWIN_INPUTS = ['x', 'norm_mix_g', 'norm_ffn_g', 'a_w_in', 'a_g_v', 'a_w_s', 'a_b_s', 'a_w_out', 'b_w_in', 'b_a_re', 'b_a_im', 'b_log_dt', 'b_b_re', 'b_b_im', 'b_c_re', 'b_c_im', 'b_d', 'b_w_glu', 'f_w_up', 'f_conv_w', 'f_conv_b', 'f_w_down', 'final_g', 'loss_target', 'm_norm_mix_g', 'm_norm_ffn_g', 'm_a_w_in', 'm_a_g_v', 'm_a_w_s', 'm_a_b_s', 'm_a_w_out', 'm_b_w_in', 'm_b_a_re', 'm_b_a_im', 'm_b_log_dt', 'm_b_b_re', 'm_b_b_im', 'm_b_c_re', 'm_b_c_im', 'm_b_d', 'm_b_w_glu', 'm_f_w_up', 'm_f_conv_w', 'm_f_conv_b', 'm_f_w_down', 'm_final_g', 'v_norm_mix_g', 'v_norm_ffn_g', 'v_a_w_in', 'v_a_g_v', 'v_a_w_s', 'v_a_b_s', 'v_a_w_out', 'v_b_w_in', 'v_b_a_re', 'v_b_a_im', 'v_b_log_dt', 'v_b_b_re', 'v_b_b_im', 'v_b_c_re', 'v_b_c_im', 'v_b_d', 'v_b_w_glu', 'v_f_w_up', 'v_f_conv_w', 'v_f_conv_b', 'v_f_w_down', 'v_final_g']
TWIN_OUTPUTS = ['loss', 'grad_x', 'grad_norm_mix_g', 'grad_norm_ffn_g', 'grad_a_w_in', 'grad_a_g_v', 'grad_a_w_s', 'grad_a_b_s', 'grad_a_w_out', 'grad_b_w_in', 'grad_b_a_re', 'grad_b_a_im', 'grad_b_log_dt', 'grad_b_b_re', 'grad_b_b_im', 'grad_b_c_re', 'grad_b_c_im', 'grad_b_d', 'grad_b_w_glu', 'grad_f_w_up', 'grad_f_conv_w', 'grad_f_conv_b', 'grad_f_w_down', 'grad_final_g', 'delta_norm_mix_g', 'delta_norm_ffn_g', 'delta_a_w_in', 'delta_a_g_v', 'delta_a_w_s', 'delta_a_b_s', 'delta_a_w_out', 'delta_b_w_in', 'delta_b_a_re', 'delta_b_a_im', 'delta_b_log_dt', 'delta_b_b_re', 'delta_b_b_im', 'delta_b_c_re', 'delta_b_c_im', 'delta_b_d', 'delta_b_w_glu', 'delta_f_w_up', 'delta_f_conv_w', 'delta_f_conv_b', 'delta_f_w_down', 'delta_final_g', 'new_m_norm_mix_g', 'new_m_norm_ffn_g', 'new_m_a_w_in', 'new_m_a_g_v', 'new_m_a_w_s', 'new_m_a_b_s', 'new_m_a_w_out', 'new_m_b_w_in', 'new_m_b_a_re', 'new_m_b_a_im', 'new_m_b_log_dt', 'new_m_b_b_re', 'new_m_b_b_im', 'new_m_b_c_re', 'new_m_b_c_im', 'new_m_b_d', 'new_m_b_w_glu', 'new_m_f_w_up', 'new_m_f_conv_w', 'new_m_f_conv_b', 'new_m_f_w_down', 'new_m_final_g', 'new_v_norm_mix_g', 'new_v_norm_ffn_g', 'new_v_a_w_in', 'new_v_a_g_v', 'new_v_a_w_s', 'new_v_a_b_s', 'new_v_a_w_out', 'new_v_b_w_in', 'new_v_b_a_re', 'new_v_b_a_im', 'new_v_b_log_dt', 'new_v_b_b_re', 'new_v_b_b_im', 'new_v_b_c_re', 'new_v_b_c_im', 'new_v_b_d', 'new_v_b_w_glu', 'new_v_f_w_up', 'new_v_f_conv_w', 'new_v_f_conv_b', 'new_v_f_w_down', 'new_v_final_g']
TWIN_LEAF_KINDS = {'loss': 'loss', 'grad_x': 'grad_x', 'grad_norm_mix_g': 'grad_w', 'grad_norm_ffn_g': 'grad_w', 'grad_a_w_in': 'grad_w', 'grad_a_g_v': 'grad_w', 'grad_a_w_s': 'grad_w', 'grad_a_b_s': 'grad_w', 'grad_a_w_out': 'grad_w', 'grad_b_w_in': 'grad_w', 'grad_b_a_re': 'grad_w', 'grad_b_a_im': 'grad_w', 'grad_b_log_dt': 'grad_w', 'grad_b_b_re': 'grad_w', 'grad_b_b_im': 'grad_w', 'grad_b_c_re': 'grad_w', 'grad_b_c_im': 'grad_w', 'grad_b_d': 'grad_w', 'grad_b_w_glu': 'grad_w', 'grad_f_w_up': 'grad_w', 'grad_f_conv_w': 'grad_w', 'grad_f_conv_b': 'grad_w', 'grad_f_w_down': 'grad_w', 'grad_final_g': 'grad_w', 'delta_norm_mix_g': 'delta_w', 'delta_norm_ffn_g': 'delta_w', 'delta_a_w_in': 'delta_w', 'delta_a_g_v': 'delta_w', 'delta_a_w_s': 'delta_w', 'delta_a_b_s': 'delta_w', 'delta_a_w_out': 'delta_w', 'delta_b_w_in': 'delta_w', 'delta_b_a_re': 'delta_w', 'delta_b_a_im': 'delta_w', 'delta_b_log_dt': 'delta_w', 'delta_b_b_re': 'delta_w', 'delta_b_b_im': 'delta_w', 'delta_b_c_re': 'delta_w', 'delta_b_c_im': 'delta_w', 'delta_b_d': 'delta_w', 'delta_b_w_glu': 'delta_w', 'delta_f_w_up': 'delta_w', 'delta_f_conv_w': 'delta_w', 'delta_f_conv_b': 'delta_w', 'delta_f_w_down': 'delta_w', 'delta_final_g': 'delta_w', 'new_m_norm_mix_g': 'new_m', 'new_m_norm_ffn_g': 'new_m', 'new_m_a_w_in': 'new_m', 'new_m_a_g_v': 'new_m', 'new_m_a_w_s': 'new_m', 'new_m_a_b_s': 'new_m', 'new_m_a_w_out': 'new_m', 'new_m_b_w_in': 'new_m', 'new_m_b_a_re': 'new_m', 'new_m_b_a_im': 'new_m', 'new_m_b_log_dt': 'new_m', 'new_m_b_b_re': 'new_m', 'new_m_b_b_im': 'new_m', 'new_m_b_c_re': 'new_m', 'new_m_b_c_im': 'new_m', 'new_m_b_d': 'new_m', 'new_m_b_w_glu': 'new_m', 'new_m_f_w_up': 'new_m', 'new_m_f_conv_w': 'new_m', 'new_m_f_conv_b': 'new_m', 'new_m_f_w_down': 'new_m', 'new_m_final_g': 'new_m', 'new_v_norm_mix_g': 'new_v', 'new_v_norm_ffn_g': 'new_v', 'new_v_a_w_in': 'new_v', 'new_v_a_g_v': 'new_v', 'new_v_a_w_s': 'new_v', 'new_v_a_b_s': 'new_v', 'new_v_a_w_out': 'new_v', 'new_v_b_w_in': 'new_v', 'new_v_b_a_re': 'new_v', 'new_v_b_a_im': 'new_v', 'new_v_b_log_dt': 'new_v', 'new_v_b_b_re': 'new_v', 'new_v_b_b_im': 'new_v', 'new_v_b_c_re': 'new_v', 'new_v_b_c_im': 'new_v', 'new_v_b_d': 'new_v', 'new_v_b_w_glu': 'new_v', 'new_v_f_w_up': 'new_v', 'new_v_f_conv_w': 'new_v', 'new_v_f_conv_b': 'new_v', 'new_v_f_w_down': 'new_v', 'new_v_final_g': 'new_v'}


def _forward(args):
    return _fwd_reference(*[args[k] for k in FWD_PARAMS])


def _output_shape():
    out = _jax.eval_shape(lambda: _forward(_fwd_setup_inputs(0)))
    return out.shape, out.dtype

N_MICROBATCH = 1
ADAM_LR = 0.001
ADAM_B1 = 0.9
ADAM_B2 = 0.999
ADAM_EPS = 1e-08
ADAM_WD = 0.01
ADAM_STEP = 10
PER_EXAMPLE_BATCH_AXIS = {'x': 0, 'loss_target': 0}
SHARED_INPUTS = []
_WEIGHT_DTYPES = {'norm_mix_g': _jnp.float32, 'norm_ffn_g': _jnp.float32, 'a_w_in': _jnp.float32, 'a_g_v': _jnp.float32, 'a_w_s': _jnp.float32, 'a_b_s': _jnp.float32, 'a_w_out': _jnp.float32, 'b_w_in': _jnp.float32, 'b_a_re': _jnp.float32, 'b_a_im': _jnp.float32, 'b_log_dt': _jnp.float32, 'b_b_re': _jnp.float32, 'b_b_im': _jnp.float32, 'b_c_re': _jnp.float32, 'b_c_im': _jnp.float32, 'b_d': _jnp.float32, 'b_w_glu': _jnp.float32, 'f_w_up': _jnp.float32, 'f_conv_w': _jnp.float32, 'f_conv_b': _jnp.float32, 'f_w_down': _jnp.float32, 'final_g': _jnp.float32}
MOMENT_SCALE = {'norm_mix_g': 4.698420e-02, 'norm_ffn_g': 5.549996e-02, 'a_w_in': 4.342958e-02, 'a_g_v': 1.890157e-02, 'a_w_s': 3.774187e-02, 'a_b_s': 5.389821e-02, 'a_w_out': 5.695080e-02, 'b_w_in': 2.470564e-02, 'b_a_re': 1.819754e-03, 'b_a_im': 1.963377e-03, 'b_log_dt': 1.026322e+00, 'b_b_re': 1.005123e-03, 'b_b_im': 9.875556e-04, 'b_c_re': 1.959415e-03, 'b_c_im': 1.927702e-03, 'b_d': 2.623672e-02, 'b_w_glu': 1.737483e-02, 'f_w_up': 2.357328e-02, 'f_conv_w': 2.367802e-02, 'f_conv_b': 2.302437e-02, 'f_w_down': 3.848277e-02, 'final_g': 1.606061e+01}


def _to_microbatches(a, axis):
    t = _jnp.moveaxis(a, axis, 0)
    t = t.reshape((N_MICROBATCH, t.shape[0] // N_MICROBATCH) + t.shape[1:])
    return _jnp.moveaxis(t, 1, axis + 1)


def setup_inputs(seed: int = 0) -> dict:
    inp = _fwd_setup_inputs(seed)
    key = _jax.random.fold_in(_jax.random.key(seed), 7919)
    shape, _ = _output_shape()
    out = dict(inp)
    out["loss_target"] = _jax.random.normal(_jax.random.fold_in(key, 0), shape, _jnp.float32)
    for i, name in enumerate(TWIN_WEIGHTS):
        w = inp[name].astype(_jnp.float32)
        if MOMENT_SCALE is None:
            s = _jnp.sqrt(_jnp.mean(_jnp.square(w)) + 1e-30)
        else:
            s = MOMENT_SCALE[name]
        km, kv = _jax.random.split(_jax.random.fold_in(key, i + 1))
        out[name] = w
        out["m_" + name] = s * _jax.random.normal(km, w.shape, _jnp.float32)
        out["v_" + name] = (s * s) * _jax.random.uniform(kv, w.shape, _jnp.float32, 0.5, 1.5)
    if N_MICROBATCH > 1:
        for name, axis in PER_EXAMPLE_BATCH_AXIS.items():
            out[name] = _to_microbatches(out[name], axis)
    return {'x': out['x'], 'norm_mix_g': out['norm_mix_g'], 'norm_ffn_g': out['norm_ffn_g'], 'a_w_in': out['a_w_in'], 'a_g_v': out['a_g_v'], 'a_w_s': out['a_w_s'], 'a_b_s': out['a_b_s'], 'a_w_out': out['a_w_out'], 'b_w_in': out['b_w_in'], 'b_a_re': out['b_a_re'], 'b_a_im': out['b_a_im'], 'b_log_dt': out['b_log_dt'], 'b_b_re': out['b_b_re'], 'b_b_im': out['b_b_im'], 'b_c_re': out['b_c_re'], 'b_c_im': out['b_c_im'], 'b_d': out['b_d'], 'b_w_glu': out['b_w_glu'], 'f_w_up': out['f_w_up'], 'f_conv_w': out['f_conv_w'], 'f_conv_b': out['f_conv_b'], 'f_w_down': out['f_w_down'], 'final_g': out['final_g'], 'loss_target': out['loss_target'], 'm_norm_mix_g': out['m_norm_mix_g'], 'm_norm_ffn_g': out['m_norm_ffn_g'], 'm_a_w_in': out['m_a_w_in'], 'm_a_g_v': out['m_a_g_v'], 'm_a_w_s': out['m_a_w_s'], 'm_a_b_s': out['m_a_b_s'], 'm_a_w_out': out['m_a_w_out'], 'm_b_w_in': out['m_b_w_in'], 'm_b_a_re': out['m_b_a_re'], 'm_b_a_im': out['m_b_a_im'], 'm_b_log_dt': out['m_b_log_dt'], 'm_b_b_re': out['m_b_b_re'], 'm_b_b_im': out['m_b_b_im'], 'm_b_c_re': out['m_b_c_re'], 'm_b_c_im': out['m_b_c_im'], 'm_b_d': out['m_b_d'], 'm_b_w_glu': out['m_b_w_glu'], 'm_f_w_up': out['m_f_w_up'], 'm_f_conv_w': out['m_f_conv_w'], 'm_f_conv_b': out['m_f_conv_b'], 'm_f_w_down': out['m_f_w_down'], 'm_final_g': out['m_final_g'], 'v_norm_mix_g': out['v_norm_mix_g'], 'v_norm_ffn_g': out['v_norm_ffn_g'], 'v_a_w_in': out['v_a_w_in'], 'v_a_g_v': out['v_a_g_v'], 'v_a_w_s': out['v_a_w_s'], 'v_a_b_s': out['v_a_b_s'], 'v_a_w_out': out['v_a_w_out'], 'v_b_w_in': out['v_b_w_in'], 'v_b_a_re': out['v_b_a_re'], 'v_b_a_im': out['v_b_a_im'], 'v_b_log_dt': out['v_b_log_dt'], 'v_b_b_re': out['v_b_b_re'], 'v_b_b_im': out['v_b_b_im'], 'v_b_c_re': out['v_b_c_re'], 'v_b_c_im': out['v_b_c_im'], 'v_b_d': out['v_b_d'], 'v_b_w_glu': out['v_b_w_glu'], 'v_f_w_up': out['v_f_w_up'], 'v_f_conv_w': out['v_f_conv_w'], 'v_f_conv_b': out['v_f_conv_b'], 'v_f_w_down': out['v_f_w_down'], 'v_final_g': out['v_final_g']}


def _loss(weights, diff, rest, loss_target):
    with _jax.named_scope("forward"):
        args = {**rest, TWIN_DIFF_INPUT: diff, **{k: w.astype(_WEIGHT_DTYPES[k]) for k, w in weights.items()}}
        y = _forward(args)
    with _jax.named_scope("loss_head"):
        err = _jnp.square(y.astype(_jnp.float32) - loss_target)
        return 0.5 * _jnp.sum(_jnp.mean(err, axis=-1)) if err.ndim else 0.5 * err


def _adamw(w, g, m, v):
    m = ADAM_B1 * m + (1.0 - ADAM_B1) * g
    v = ADAM_B2 * v + (1.0 - ADAM_B2) * _jnp.square(g)
    m_hat = m / (1.0 - ADAM_B1 ** ADAM_STEP)
    v_hat = v / (1.0 - ADAM_B2 ** ADAM_STEP)
    delta = -ADAM_LR * (m_hat / (_jnp.sqrt(v_hat) + ADAM_EPS) + ADAM_WD * w)
    return delta, m, v


def reference(x, norm_mix_g, norm_ffn_g, a_w_in, a_g_v, a_w_s, a_b_s, a_w_out, b_w_in, b_a_re, b_a_im, b_log_dt, b_b_re, b_b_im, b_c_re, b_c_im, b_d, b_w_glu, f_w_up, f_conv_w, f_conv_b, f_w_down, final_g, loss_target, m_norm_mix_g, m_norm_ffn_g, m_a_w_in, m_a_g_v, m_a_w_s, m_a_b_s, m_a_w_out, m_b_w_in, m_b_a_re, m_b_a_im, m_b_log_dt, m_b_b_re, m_b_b_im, m_b_c_re, m_b_c_im, m_b_d, m_b_w_glu, m_f_w_up, m_f_conv_w, m_f_conv_b, m_f_w_down, m_final_g, v_norm_mix_g, v_norm_ffn_g, v_a_w_in, v_a_g_v, v_a_w_s, v_a_b_s, v_a_w_out, v_b_w_in, v_b_a_re, v_b_a_im, v_b_log_dt, v_b_b_re, v_b_b_im, v_b_c_re, v_b_c_im, v_b_d, v_b_w_glu, v_f_w_up, v_f_conv_w, v_f_conv_b, v_f_w_down, v_final_g):
    given = dict(x=x, norm_mix_g=norm_mix_g, norm_ffn_g=norm_ffn_g, a_w_in=a_w_in, a_g_v=a_g_v, a_w_s=a_w_s, a_b_s=a_b_s, a_w_out=a_w_out, b_w_in=b_w_in, b_a_re=b_a_re, b_a_im=b_a_im, b_log_dt=b_log_dt, b_b_re=b_b_re, b_b_im=b_b_im, b_c_re=b_c_re, b_c_im=b_c_im, b_d=b_d, b_w_glu=b_w_glu, f_w_up=f_w_up, f_conv_w=f_conv_w, f_conv_b=f_conv_b, f_w_down=f_w_down, final_g=final_g, loss_target=loss_target, m_norm_mix_g=m_norm_mix_g, m_norm_ffn_g=m_norm_ffn_g, m_a_w_in=m_a_w_in, m_a_g_v=m_a_g_v, m_a_w_s=m_a_w_s, m_a_b_s=m_a_b_s, m_a_w_out=m_a_w_out, m_b_w_in=m_b_w_in, m_b_a_re=m_b_a_re, m_b_a_im=m_b_a_im, m_b_log_dt=m_b_log_dt, m_b_b_re=m_b_b_re, m_b_b_im=m_b_b_im, m_b_c_re=m_b_c_re, m_b_c_im=m_b_c_im, m_b_d=m_b_d, m_b_w_glu=m_b_w_glu, m_f_w_up=m_f_w_up, m_f_conv_w=m_f_conv_w, m_f_conv_b=m_f_conv_b, m_f_w_down=m_f_w_down, m_final_g=m_final_g, v_norm_mix_g=v_norm_mix_g, v_norm_ffn_g=v_norm_ffn_g, v_a_w_in=v_a_w_in, v_a_g_v=v_a_g_v, v_a_w_s=v_a_w_s, v_a_b_s=v_a_b_s, v_a_w_out=v_a_w_out, v_b_w_in=v_b_w_in, v_b_a_re=v_b_a_re, v_b_a_im=v_b_a_im, v_b_log_dt=v_b_log_dt, v_b_b_re=v_b_b_re, v_b_b_im=v_b_b_im, v_b_c_re=v_b_c_re, v_b_c_im=v_b_c_im, v_b_d=v_b_d, v_b_w_glu=v_b_w_glu, v_f_w_up=v_f_w_up, v_f_conv_w=v_f_conv_w, v_f_conv_b=v_f_conv_b, v_f_w_down=v_f_w_down, v_final_g=v_final_g)
    weights = {n: given[n] for n in TWIN_WEIGHTS}
    shared = {n: given[n] for n in SHARED_INPUTS}
    per_example = {n: given[n] for n in ['x']}
    grad_fn = _jax.value_and_grad(_loss, argnums=(0, 1))

    def one_microbatch(ex, loss_target):
        ex = dict(ex)
        diff = ex.pop(TWIN_DIFF_INPUT)
        return grad_fn(weights, diff, {**shared, **ex}, loss_target)

    if N_MICROBATCH == 1:
        loss, (grad_w, grad_x) = one_microbatch(per_example, given["loss_target"])
    else:
        def body(carry, xs):
            loss_sum, grad_sum = carry
            l_k, (gw_k, gx_k) = one_microbatch(xs[0], xs[1])
            with _jax.named_scope("update"):
                return (loss_sum + l_k, _jax.tree.map(_jnp.add, grad_sum, gw_k)), gx_k

        init = (_jnp.zeros((), _jnp.float32), _jax.tree.map(_jnp.zeros_like, weights))
        (loss, grad_w), grad_x = _jax.lax.scan(body, init, (per_example, given["loss_target"]))
    with _jax.named_scope("update"):
        delta_w, new_m, new_v = {}, {}, {}
        for n in TWIN_WEIGHTS:
            delta_w[n], new_m[n], new_v[n] = _adamw(weights[n], grad_w[n], given["m_" + n], given["v_" + n])
    return (loss, grad_x, *[grad_w[n] for n in TWIN_WEIGHTS], *[delta_w[n] for n in TWIN_WEIGHTS],
            *[new_m[n] for n in TWIN_WEIGHTS], *[new_v[n] for n in TWIN_WEIGHTS])
```

```python
import functools
import math

import jax
import jax.numpy as jnp
from jax import lax
from jax.experimental import pallas as pl
from jax.experimental.pallas import tpu as pltpu

F32, BF16 = jnp.float32, jnp.bfloat16
MESH = pl.DeviceIdType.MESH

CHUNK = 128
SGU_GROUP = 128
SSM_GROUP = 16
SSM_STATE = 64
EPS = 1e-6
LANES = 128
GROUPS_PER_BLOCK = LANES // SSM_GROUP
STATE_BLOCKS = SSM_STATE // SSM_GROUP
VMEM_LIMIT = 52 * 1024 * 1024

ADAM_LR, ADAM_B1, ADAM_B2, ADAM_EPS, ADAM_WD, ADAM_STEP = 0.001, 0.9, 0.999, 1e-08, 0.01, 10

W_NAMES = ['norm_mix_g', 'norm_ffn_g', 'a_w_in', 'a_g_v', 'a_w_s', 'a_b_s', 'a_w_out', 'b_w_in', 'b_a_re', 'b_a_im',
           'b_log_dt', 'b_b_re', 'b_b_im', 'b_c_re', 'b_c_im', 'b_d', 'b_w_glu', 'f_w_up', 'f_conv_w', 'f_conv_b',
           'f_w_down', 'final_g']
BIG = {'a_w_in': 'col', 'a_w_out': 'row', 'b_w_in': 'row', 'b_w_glu': 'col', 'f_w_up': 'col', 'f_w_down': 'row'}
SMALL = [n for n in W_NAMES if n not in BIG]
CHIP_SHARDED_SMALL = {'b_d': 1, 'f_conv_w': 2}


def _tile(n, pref, align):
    t = min(n, pref)
    t -= t % align
    while t >= align:
        if n % t == 0:
            return t
        t -= align
    return n


def _params(*sem):
    return pltpu.CompilerParams(dimension_semantics=sem, vmem_limit_bytes=VMEM_LIMIT)


def _gelu(x):
    c = math.sqrt(2.0 / math.pi)
    return 0.5 * x * (1.0 + jnp.tanh(c * (x + 0.044715 * x * x * x)))


def _gelu_grad(x):
    c = math.sqrt(2.0 / math.pi)
    t = jnp.tanh(c * (x + 0.044715 * x * x * x))
    return 0.5 * (1.0 + t) + 0.5 * x * (1.0 - t * t) * c * (1.0 + 3.0 * 0.044715 * x * x)


def _half_shape(kind, shard_shape):
    _, r, c = shard_shape
    return (r // 2, c) if kind == 'col' else (r, c // 2)


def _full_dims(kind, shard_shape):
    _, r, c = shard_shape
    return (r, 4 * c) if kind == 'col' else (4 * r, c)


def _gspec(kind, kdim, ndim, tr, tc, layer, rc):
    if kind == 'col':
        nr, nc = (kdim // 2) // tr, (ndim // 4) // tc

        def imap(*g):
            rb, cb = rc(*g)
            return (cb // nc, rb // nr, layer, rb % nr, cb % nc)
    else:
        nr, nc = (kdim // 4) // tr, (ndim // 2) // tc

        def imap(*g):
            rb, cb = rc(*g)
            return (rb // nr, cb // nc, layer, rb % nr, cb % nc)
    return pl.BlockSpec((None, None, None, tr, tc), imap)


def _wtiles(kind, kdim, ndim):
    if kind == 'col':
        return _tile(kdim // 2, 1024, LANES), _tile(ndim // 4, 1408, LANES)
    return _tile(kdim // 4, 1408, LANES), _tile(ndim // 2, 1024, LANES)


_DIMS = {'nn': (((1,), (0,)), ((), ())), 'nt': (((1,), (1,)), ((), ())), 'tn': (((0,), (0,)), ((), ()))}


def _matmul(name, mode, a, b, grid, a_spec, b_spec, out_shape, out_spec, acc_shape, extras=(), extra_specs=(),
            epilogue=None, aliases=None):
    nk = grid[2]
    dims = _DIMS[mode]
    n_extra = len(extras)

    def body(a_ref, b_ref, *rest):
        extra_refs, o_ref = rest[:n_extra], rest[n_extra]
        prod = lax.dot_general(a_ref[...].astype(BF16), b_ref[...].astype(BF16), dims, preferred_element_type=F32)

        def finish(r):
            if epilogue is not None:
                r = epilogue(r, *[e[...] for e in extra_refs])
            o_ref[...] = r.astype(o_ref.dtype)

        if nk == 1:
            finish(prod)
            return
        acc_ref = rest[n_extra + 1]
        kk = pl.program_id(2)

        @pl.when(kk == 0)
        def _():
            acc_ref[...] = prod

        @pl.when(kk > 0)
        def _():
            acc_ref[...] += prod

        @pl.when(kk == nk - 1)
        def _():
            finish(acc_ref[...])

    scratch = [pltpu.VMEM(acc_shape, F32)] if nk > 1 else []
    return pl.pallas_call(
        body, name=name, grid=grid, in_specs=[a_spec, b_spec, *extra_specs], out_specs=out_spec, out_shape=out_shape,
        scratch_shapes=scratch, input_output_aliases=aliases or {},
        compiler_params=_params("parallel", "parallel", "arbitrary"))(a, b, *extras)


def _mm_x_w(name, a, wg, kind, kdim, ndim, layer, out_dtype, residual=None):
    rows = a.shape[0]
    tk, tn = _wtiles(kind, kdim, ndim)
    tm = _tile(rows, 1024, 16)
    grid = (rows // tm, ndim // tn, kdim // tk)
    extras, especs, epi = (), (), None
    if residual is not None:
        extras, especs = (residual,), (pl.BlockSpec((tm, tn), lambda i, j, k: (i, j)),)
        epi = lambda r, res: r + res
    return _matmul(name, 'nn', a, wg, grid, pl.BlockSpec((tm, tk), lambda i, j, k: (i, k)),
                   _gspec(kind, kdim, ndim, tk, tn, layer, lambda i, j, k: (k, j)),
                   jax.ShapeDtypeStruct((rows, ndim), out_dtype), pl.BlockSpec((tm, tn), lambda i, j, k: (i, j)),
                   (tm, tn), extras, especs, epi)


def _mm_dy_wt(name, dy, wg, kind, kdim, ndim, layer, out_dtype):
    rows = dy.shape[0]
    tn, tk = _wtiles(kind, kdim, ndim)
    tm = _tile(rows, 1024, 16)
    grid = (rows // tm, kdim // tn, ndim // tk)
    return _matmul(name, 'nt', dy, wg, grid, pl.BlockSpec((tm, tk), lambda i, j, k: (i, k)),
                   _gspec(kind, kdim, ndim, tn, tk, layer, lambda i, j, k: (j, k)),
                   jax.ShapeDtypeStruct((rows, kdim), out_dtype), pl.BlockSpec((tm, tn), lambda i, j, k: (i, j)),
                   (tm, tn))


def _mm_xt_dy(name, xa, dy, pg, kind, kdim, ndim, layer):
    rows = xa.shape[0]
    tm, tn = _wtiles(kind, kdim, ndim)
    tl = _tile(rows, 1024, 16)
    grid = (kdim // tm, ndim // tn, rows // tl)
    return _matmul(name, 'tn', xa, dy, grid, pl.BlockSpec((tl, tm), lambda i, j, k: (k, i)),
                   pl.BlockSpec((tl, tn), lambda i, j, k: (k, j)),
                   jax.ShapeDtypeStruct(pg.shape, pg.dtype),
                   _gspec(kind, kdim, ndim, tm, tn, layer, lambda i, j, k: (i, j)), (tm, tn),
                   extras=(pg,), extra_specs=(pl.BlockSpec(memory_space=pl.ANY),), aliases={2: 0})


def _rms_fwd(name, h, g):
    rows, d = h.shape
    tm = _tile(rows, 256, 16)

    def body(h_ref, g_ref, o_ref):
        x = h_ref[...]
        r = lax.rsqrt(jnp.mean(x * x, axis=-1, keepdims=True) + EPS)
        o_ref[...] = (x * r * g_ref[...]).astype(o_ref.dtype)

    return pl.pallas_call(
        body, name=name, grid=(rows // tm,),
        in_specs=[pl.BlockSpec((tm, d), lambda i: (i, 0)), pl.BlockSpec((1, d), lambda i: (0, 0))],
        out_specs=pl.BlockSpec((tm, d), lambda i: (i, 0)), out_shape=jax.ShapeDtypeStruct((rows, d), BF16),
        compiler_params=_params("parallel"))(h, g)


def _rms_bwd(name, h, g, dhn, dres):
    rows, d = h.shape
    tm = _tile(rows, 256, 16)

    def body(h_ref, g_ref, dy_ref, dres_ref, dh_ref, dg_ref):
        x = h_ref[...]
        r = lax.rsqrt(jnp.mean(x * x, axis=-1, keepdims=True) + EPS)
        xh = x * r
        dy = dy_ref[...].astype(F32)
        gy = dy * g_ref[...]
        dh_ref[...] = dres_ref[...] + r * (gy - xh * jnp.mean(gy * xh, axis=-1, keepdims=True))
        part = jnp.sum(dy * xh, axis=0, keepdims=True)

        @pl.when(pl.program_id(0) == 0)
        def _():
            dg_ref[...] = part

        @pl.when(pl.program_id(0) > 0)
        def _():
            dg_ref[...] += part

    row = pl.BlockSpec((tm, d), lambda i: (i, 0))
    vec = pl.BlockSpec((1, d), lambda i: (0, 0))
    return pl.pallas_call(
        body, name=name, grid=(rows // tm,), in_specs=[row, vec, row, row], out_specs=[row, vec],
        out_shape=[jax.ShapeDtypeStruct((rows, d), F32), jax.ShapeDtypeStruct((1, d), F32)],
        compiler_params=_params("arbitrary"))(h, g, dhn, dres)


def _loss_head(h, g, target):
    rows, d = h.shape
    tm = _tile(rows, 256, 16)

    def body(h_ref, g_ref, t_ref, dh_ref, dg_ref, loss_ref):
        x = h_ref[...]
        r = lax.rsqrt(jnp.mean(x * x, axis=-1, keepdims=True) + EPS)
        xh = x * r
        err = xh * g_ref[...] - t_ref[...]
        dy = err * (1.0 / d)
        gy = dy * g_ref[...]
        dh_ref[...] = r * (gy - xh * jnp.mean(gy * xh, axis=-1, keepdims=True))
        part = jnp.sum(dy * xh, axis=0, keepdims=True)
        sq = jnp.sum(err * err, axis=0, keepdims=True) * (0.5 / d)

        @pl.when(pl.program_id(0) == 0)
        def _():
            dg_ref[...] = part
            loss_ref[...] = sq

        @pl.when(pl.program_id(0) > 0)
        def _():
            dg_ref[...] += part
            loss_ref[...] += sq

    row = pl.BlockSpec((tm, d), lambda i: (i, 0))
    vec = pl.BlockSpec((1, d), lambda i: (0, 0))
    return pl.pallas_call(
        body, name="loss_head", grid=(rows // tm,), in_specs=[row, vec, row], out_specs=[row, vec, vec],
        out_shape=[jax.ShapeDtypeStruct((rows, d), F32), jax.ShapeDtypeStruct((1, d), F32),
                   jax.ShapeDtypeStruct((1, d), F32)],
        compiler_params=_params("arbitrary"))(h, g, target)


def _shift_down(cur, prev8, first, k):
    rows = cur.shape[0]
    rolled = pltpu.roll(cur, k, axis=0)
    idx = lax.broadcasted_iota(jnp.int32, cur.shape, 0)
    prev8 = jnp.where(first, 0.0, prev8)
    out = rolled
    for r in range(k):
        out = jnp.where(idx == r, prev8[8 - k + r:8 - k + r + 1, :], out)
    del rows
    return out


def _shift_up(cur, next8, last, k):
    rows = cur.shape[0]
    rolled = pltpu.roll(cur, rows - k, axis=0)
    idx = lax.broadcasted_iota(jnp.int32, cur.shape, 0)
    next8 = jnp.where(last, 0.0, next8)
    out = rolled
    for r in range(k):
        out = jnp.where(idx == rows - k + r, next8[r:r + 1, :], out)
    return out


def _conv_acc(z, zprev, first, w_ref, b_ref):
    z1 = _shift_down(z, zprev, first, 1)
    z2 = _shift_down(z, zprev, first, 2)
    return b_ref[...] + w_ref[2:3, :] * z + w_ref[1:2, :] * z1 + w_ref[0:1, :] * z2, z1, z2


def _ffn_tiles(rows, f):
    return _tile(rows, 512, 16), _tile(f, 512, LANES)


def _ffn_act_fwd(name, z, cw, cb):
    rows, f2 = z.shape
    f = f2 // 2
    tm, tc = _ffn_tiles(rows, f)
    nf = f // tc
    hb = tm // 8

    def body(zg_ref, zgp_ref, zv_ref, zvp_ref, wg_ref, wv_ref, bg_ref, bv_ref, o_ref):
        first = pl.program_id(0) == 0
        gate, _, _ = _conv_acc(zg_ref[...].astype(F32), zgp_ref[...].astype(F32), first, wg_ref, bg_ref)
        val, _, _ = _conv_acc(zv_ref[...].astype(F32), zvp_ref[...].astype(F32), first, wv_ref, bv_ref)
        o_ref[...] = (gate * jax.nn.sigmoid(gate) * val).astype(o_ref.dtype)

    cur = lambda off: pl.BlockSpec((tm, tc), lambda i, j: (i, j + off))
    prev = lambda off: pl.BlockSpec((8, tc), lambda i, j: (jnp.maximum(i * hb - 1, 0), j + off))
    wsp = lambda off: pl.BlockSpec((3, tc), lambda i, j: (0, j + off))
    bsp = lambda off: pl.BlockSpec((1, tc), lambda i, j: (0, j + off))
    return pl.pallas_call(
        body, name=name, grid=(rows // tm, nf),
        in_specs=[cur(0), prev(0), cur(nf), prev(nf), wsp(0), wsp(nf), bsp(0), bsp(nf)],
        out_specs=pl.BlockSpec((tm, tc), lambda i, j: (i, j)), out_shape=jax.ShapeDtypeStruct((rows, f), BF16),
        compiler_params=_params("parallel", "parallel"))(z, z, z, z, cw, cw, cb, cb)


def _ffn_act_bwd(name, z, da, cw, cb):
    rows, f2 = z.shape
    f = f2 // 2
    tm, tc = _ffn_tiles(rows, f)
    nf = f // tc
    hb = tm // 8

    def body(zs_ref, zsp_ref, zo_ref, zop_ref, ws_ref, wo_ref, bs_ref, bo_ref, da_ref, dacc_ref, dcw_ref, dcb_ref):
        i = pl.program_id(1)
        first = i == 0
        is_gate = pl.program_id(0) < nf
        acc_s, z1, z2 = _conv_acc(zs_ref[...].astype(F32), zsp_ref[...].astype(F32), first, ws_ref, bs_ref)
        acc_o, _, _ = _conv_acc(zo_ref[...].astype(F32), zop_ref[...].astype(F32), first, wo_ref, bo_ref)
        d_a = da_ref[...].astype(F32)
        sig_s = jax.nn.sigmoid(acc_s)
        sig_o = jax.nn.sigmoid(acc_o)
        as_gate = d_a * acc_o * sig_s * (1.0 + acc_s * (1.0 - sig_s))
        as_val = d_a * acc_o * sig_o
        dacc = jnp.where(is_gate, as_gate, as_val)
        dacc_ref[...] = dacc.astype(dacc_ref.dtype)
        zc = zs_ref[...].astype(F32)
        taps = [jnp.sum(dacc * t, axis=0, keepdims=True) for t in (z2, z1, zc)]
        pb = jnp.sum(dacc, axis=0, keepdims=True)

        @pl.when(first)
        def _():
            for k in range(3):
                dcw_ref[k:k + 1, :] = taps[k]
            dcb_ref[...] = pb

        @pl.when(i > 0)
        def _():
            for k in range(3):
                dcw_ref[k:k + 1, :] += taps[k]
            dcb_ref[...] += pb

    n2 = 2 * nf
    other = lambda j: (j + nf) % n2
    cur = lambda col: pl.BlockSpec((tm, tc), lambda j, i: (i, col(j)))
    prev = lambda col: pl.BlockSpec((8, tc), lambda j, i: (jnp.maximum(i * hb - 1, 0), col(j)))
    wsp = lambda col: pl.BlockSpec((3, tc), lambda j, i: (0, col(j)))
    bsp = lambda col: pl.BlockSpec((1, tc), lambda j, i: (0, col(j)))
    ident = lambda j: j
    return pl.pallas_call(
        body, name=name, grid=(n2, rows // tm),
        in_specs=[cur(ident), prev(ident), cur(other), prev(other), wsp(ident), wsp(other), bsp(ident), bsp(other),
                  pl.BlockSpec((tm, tc), lambda j, i: (i, j % nf))],
        out_specs=[cur(ident), wsp(ident), bsp(ident)],
        out_shape=[jax.ShapeDtypeStruct((rows, f2), BF16), jax.ShapeDtypeStruct((3, f2), F32),
                   jax.ShapeDtypeStruct((1, f2), F32)],
        compiler_params=_params("parallel", "arbitrary"))(z, z, z, z, cw, cw, cb, cb, da)


def _conv_bwd(name, dacc, cw):
    rows, f2 = dacc.shape
    tm, tc = _ffn_tiles(rows, f2 // 2)
    hb = tm // 8
    nrow = rows // tm

    def body(d_ref, dn_ref, w_ref, o_ref):
        last = pl.program_id(0) == nrow - 1
        d = d_ref[...].astype(F32)
        nxt = dn_ref[...].astype(F32)
        d1 = _shift_up(d, nxt, last, 1)
        d2 = _shift_up(d, nxt, last, 2)
        o_ref[...] = (w_ref[2:3, :] * d + w_ref[1:2, :] * d1 + w_ref[0:1, :] * d2).astype(o_ref.dtype)

    return pl.pallas_call(
        body, name=name, grid=(nrow, f2 // tc),
        in_specs=[pl.BlockSpec((tm, tc), lambda i, j: (i, j)),
                  pl.BlockSpec((8, tc), lambda i, j: (jnp.minimum((i + 1) * hb, rows // 8 - 1), j)),
                  pl.BlockSpec((3, tc), lambda i, j: (0, j))],
        out_specs=pl.BlockSpec((tm, tc), lambda i, j: (i, j)), out_shape=jax.ShapeDtypeStruct((rows, f2), BF16),
        compiler_params=_params("parallel", "parallel"))(dacc, dacc, cw)


def _glu_fwd(name, gg, h):
    rows, d2 = gg.shape
    d = d2 // 2
    tm, tc = _tile(rows, 512, 16), _tile(d, 1024, LANES)
    nd = d // tc

    def body(a_ref, b_ref, h_ref, o_ref):
        o_ref[...] = h_ref[...] + a_ref[...].astype(F32) * jax.nn.sigmoid(b_ref[...].astype(F32))

    return pl.pallas_call(
        body, name=name, grid=(rows // tm, nd),
        in_specs=[pl.BlockSpec((tm, tc), lambda i, j: (i, j)), pl.BlockSpec((tm, tc), lambda i, j: (i, j + nd)),
                  pl.BlockSpec((tm, tc), lambda i, j: (i, j))],
        out_specs=pl.BlockSpec((tm, tc), lambda i, j: (i, j)), out_shape=jax.ShapeDtypeStruct((rows, d), F32),
        compiler_params=_params("parallel", "parallel"))(gg, gg, h)


def _glu_bwd(name, gg, dh):
    rows, d2 = gg.shape
    d = d2 // 2
    tm, tc = _tile(rows, 512, 16), _tile(d, 1024, LANES)
    nd = d // tc

    def body(s_ref, o_ref, dh_ref, out_ref):
        is_a = pl.program_id(1) < nd
        me = s_ref[...].astype(F32)
        other = o_ref[...].astype(F32)
        g = dh_ref[...]
        sig_o = jax.nn.sigmoid(other)
        sig_m = jax.nn.sigmoid(me)
        out_ref[...] = jnp.where(is_a, g * sig_o, g * other * sig_m * (1.0 - sig_m)).astype(out_ref.dtype)

    return pl.pallas_call(
        body, name=name, grid=(rows // tm, 2 * nd),
        in_specs=[pl.BlockSpec((tm, tc), lambda i, j: (i, j)),
                  pl.BlockSpec((tm, tc), lambda i, j: (i, (j + nd) % (2 * nd))),
                  pl.BlockSpec((tm, tc), lambda i, j: (i, j % nd))],
        out_specs=pl.BlockSpec((tm, tc), lambda i, j: (i, j)), out_shape=jax.ShapeDtypeStruct((rows, d2), BF16),
        compiler_params=_params("parallel", "parallel"))(gg, gg, dh)


def _sgu_common(pre_ref, gv_ref, e):
    u = _gelu(pre_ref[:, :e].astype(F32))
    v = _gelu(pre_ref[:, e:].astype(F32))
    r = lax.rsqrt(jnp.mean(v * v, axis=-1, keepdims=True) + EPS)
    vh = v * r
    return u, vh, r, (vh * gv_ref[...]).astype(BF16)


def _tril_bf16(ws_ref, hd):
    t = lax.broadcasted_iota(jnp.int32, (CHUNK, CHUNK), 0)
    s = lax.broadcasted_iota(jnp.int32, (CHUNK, CHUNK), 1)
    return jnp.where(s <= t, ws_ref[hd], 0.0).astype(BF16)


def _sgu_mix_fwd(name, pre, gv, ws, bsx):
    rows, e2 = pre.shape
    e = e2 // 2
    heads = e // SGU_GROUP
    tr = _tile(rows, 256, CHUNK)

    def body(pre_ref, gv_ref, ws_ref, bs_ref, o_ref):
        u, _, _, vn = _sgu_common(pre_ref, gv_ref, e)
        for hd in range(heads):
            wm = _tril_bf16(ws_ref, hd)
            cols = slice(hd * SGU_GROUP, (hd + 1) * SGU_GROUP)
            for ck in range(tr // CHUNK):
                rws = slice(ck * CHUNK, (ck + 1) * CHUNK)
                s = jnp.dot(wm, vn[rws, cols], preferred_element_type=F32) + bs_ref[hd]
                o_ref[rws, cols] = (u[rws, cols] * s).astype(o_ref.dtype)

    whole3 = pl.BlockSpec((heads, CHUNK, CHUNK), lambda i: (0, 0, 0))
    return pl.pallas_call(
        body, name=name, grid=(rows // tr,),
        in_specs=[pl.BlockSpec((tr, e2), lambda i: (i, 0)), pl.BlockSpec((1, e), lambda i: (0, 0)), whole3, whole3],
        out_specs=pl.BlockSpec((tr, e), lambda i: (i, 0)), out_shape=jax.ShapeDtypeStruct((rows, e), BF16),
        compiler_params=_params("parallel"))(pre, gv, ws, bsx)


def _sgu_mix_bwd(name, pre, dus, gv, ws, bsx):
    rows, e2 = pre.shape
    e = e2 // 2
    heads = e // SGU_GROUP
    tr = _tile(rows, 256, CHUNK)

    def body(pre_ref, dus_ref, gv_ref, ws_ref, bs_ref, dpre_ref, dws_ref, dbs_ref, dgv_ref, dvn_ref, du_ref):
        first = pl.program_id(0) == 0
        u, vh, r, vn = _sgu_common(pre_ref, gv_ref, e)
        ones = jnp.ones((SGU_GROUP, LANES), BF16)
        tt = lax.broadcasted_iota(jnp.int32, (CHUNK, CHUNK), 0)
        ss = lax.broadcasted_iota(jnp.int32, (CHUNK, CHUNK), 1)
        for hd in range(heads):
            wm = _tril_bf16(ws_ref, hd)
            cols = slice(hd * SGU_GROUP, (hd + 1) * SGU_GROUP)
            dw = jnp.zeros((CHUNK, CHUNK), F32)
            db = jnp.zeros((CHUNK, LANES), F32)
            for ck in range(tr // CHUNK):
                rws = slice(ck * CHUNK, (ck + 1) * CHUNK)
                vblk = vn[rws, cols]
                s = jnp.dot(wm, vblk, preferred_element_type=F32) + bs_ref[hd]
                d_us = dus_ref[rws, cols].astype(F32)
                du_ref[rws, cols] = d_us * s
                ds = (d_us * u[rws, cols]).astype(BF16)
                dvn_ref[rws, cols] = lax.dot_general(wm, ds, _DIMS['tn'], preferred_element_type=F32)
                dw = dw + lax.dot_general(ds, vblk, _DIMS['nt'], preferred_element_type=F32)
                db = db + jnp.dot(ds, ones, preferred_element_type=F32)
            dw = jnp.where(ss <= tt, dw, 0.0)

            @pl.when(first)
            def _():
                dws_ref[hd] = dw
                dbs_ref[hd] = db

            @pl.when(jnp.logical_not(first))
            def _():
                dws_ref[hd] += dw
                dbs_ref[hd] += db

        dvn = dvn_ref[...]
        part = jnp.sum(dvn * vh, axis=0, keepdims=True)

        @pl.when(first)
        def _():
            dgv_ref[...] = part

        @pl.when(jnp.logical_not(first))
        def _():
            dgv_ref[...] += part

        gy = dvn * gv_ref[...]
        dv = r * (gy - vh * jnp.mean(gy * vh, axis=-1, keepdims=True))
        dpre_ref[:, :e] = (du_ref[...] * _gelu_grad(pre_ref[:, :e].astype(F32))).astype(dpre_ref.dtype)
        dpre_ref[:, e:] = (dv * _gelu_grad(pre_ref[:, e:].astype(F32))).astype(dpre_ref.dtype)

    whole3 = pl.BlockSpec((heads, CHUNK, CHUNK), lambda i: (0, 0, 0))
    vec = pl.BlockSpec((1, e), lambda i: (0, 0))
    return pl.pallas_call(
        body, name=name, grid=(rows // tr,),
        in_specs=[pl.BlockSpec((tr, e2), lambda i: (i, 0)), pl.BlockSpec((tr, e), lambda i: (i, 0)), vec, whole3, whole3],
        out_specs=[pl.BlockSpec((tr, e2), lambda i: (i, 0)), whole3, whole3, vec],
        out_shape=[jax.ShapeDtypeStruct((rows, e2), BF16), jax.ShapeDtypeStruct((heads, CHUNK, CHUNK), F32),
                   jax.ShapeDtypeStruct((heads, CHUNK, LANES), F32), jax.ShapeDtypeStruct((1, e), F32)],
        scratch_shapes=[pltpu.VMEM((tr, e), F32), pltpu.VMEM((tr, e), F32)],
        compiler_params=_params("arbitrary"))(pre, dus, gv, ws, bsx)


def _disc_a(lr, li, ldt):
    dt = jnp.exp(ldt)
    mag = jnp.exp(dt * lr)
    return mag * jnp.cos(dt * li), mag * jnp.sin(dt * li)


def _disc_b(lr, li, ldt, br, bi):
    ar, ai = _disc_a(lr, li, ldt)
    den = lr * lr + li * li
    qr = ((ar - 1.0) * lr + ai * li) / den
    qi = (ai * lr - (ar - 1.0) * li) / den
    return qr * br - qi * bi, qr * bi + qi * br


def _s5_disc(name, lr, li, ldt, lrx, lix, ldtx, br, bi):
    def body(lr_ref, li_ref, ldt_ref, lrx_ref, lix_ref, ldtx_ref, br_ref, bi_ref, ar_ref, ai_ref, bbr_ref, bbi_ref):
        ar_ref[...], ai_ref[...] = _disc_a(lr_ref[...], li_ref[...], ldt_ref[...])
        bbr_ref[...], bbi_ref[...] = _disc_b(lrx_ref[...], lix_ref[...], ldtx_ref[...], br_ref[...], bi_ref[...])

    small = jax.ShapeDtypeStruct(lr.shape, F32)
    wide = jax.ShapeDtypeStruct(br.shape, F32)
    return pl.pallas_call(body, name=name, out_shape=[small, small, wide, wide],
                          compiler_params=pltpu.CompilerParams(vmem_limit_bytes=VMEM_LIMIT))(
        lr, li, ldt, lrx, lix, ldtx, br, bi)


def _s5_disc_bwd(name, lr, li, ldt, lrx, lix, ldtx, br, bi, dar, dai, dbbr, dbbi, sel):
    def body(lr_ref, li_ref, ldt_ref, lrx_ref, lix_ref, ldtx_ref, br_ref, bi_ref, dar_ref, dai_ref, dbbr_ref,
             dbbi_ref, sel_ref, dlr_ref, dli_ref, dldt_ref, dbr_ref, dbi_ref):
        _, vjp_a = jax.vjp(_disc_a, lr_ref[...], li_ref[...], ldt_ref[...])
        g_lr, g_li, g_ldt = vjp_a((dar_ref[...], dai_ref[...]))
        _, vjp_b = jax.vjp(_disc_b, lrx_ref[...], lix_ref[...], ldtx_ref[...], br_ref[...], bi_ref[...])
        x_lr, x_li, x_ldt, g_br, g_bi = vjp_b((dbbr_ref[...], dbbi_ref[...]))
        fold = lambda t: jnp.dot(t, sel_ref[...], precision=lax.Precision.HIGHEST, preferred_element_type=F32)
        dlr_ref[...] = g_lr + fold(x_lr)
        dli_ref[...] = g_li + fold(x_li)
        dldt_ref[...] = jnp.sum(g_ldt + fold(x_ldt), axis=1, keepdims=True)
        dbr_ref[...] = g_br
        dbi_ref[...] = g_bi

    small = jax.ShapeDtypeStruct(lr.shape, F32)
    wide = jax.ShapeDtypeStruct(br.shape, F32)
    return pl.pallas_call(body, name=name,
                          out_shape=[small, small, jax.ShapeDtypeStruct((lr.shape[0], 1), F32), wide, wide],
                          compiler_params=pltpu.CompilerParams(vmem_limit_bytes=VMEM_LIMIT))(
        lr, li, ldt, lrx, lix, ldtx, br, bi, dar, dai, dbbr, dbbi, sel)


def _cmul(ar, ai, br, bi):
    return ar * br - ai * bi, ar * bi + ai * br


def _pow_chunk(ar, ai):
    pr, pi = ar, ai
    for _ in range(int(math.log2(CHUNK))):
        pr, pi = _cmul(pr, pi, pr, pi)
    return pr, pi


def _scan_forward(hr_ref, hi_ref, er_ref, ei_ref, sr_ref, si_ref, ar, ai, nck):
    arb, aib = jnp.broadcast_to(ar, (nck, LANES)), jnp.broadcast_to(ai, (nck, LANES))

    def intra(t, carry):
        sr, si = carry
        slab = pl.ds(t, nck, stride=CHUNK)
        nr = arb * sr - aib * si + hr_ref[slab, :]
        ni = arb * si + aib * sr + hi_ref[slab, :]
        hr_ref[slab, :] = nr
        hi_ref[slab, :] = ni
        return nr, ni

    zero = jnp.zeros((nck, LANES), F32)
    er_ref[...], ei_ref[...] = lax.fori_loop(0, CHUNK, intra, (zero, zero), unroll=8)
    pcr, pci = _pow_chunk(ar, ai)
    sr_ref[0:1, :] = jnp.zeros((1, LANES), F32)
    si_ref[0:1, :] = jnp.zeros((1, LANES), F32)
    for ck in range(nck - 1):
        pr, pi = sr_ref[ck:ck + 1, :], si_ref[ck:ck + 1, :]
        sr_ref[ck + 1:ck + 2, :] = pcr * pr - pci * pi + er_ref[ck:ck + 1, :]
        si_ref[ck + 1:ck + 2, :] = pcr * pi + pci * pr + ei_ref[ck:ck + 1, :]
    s_r, s_i = sr_ref[...], si_ref[...]

    def fix(t, carry):
        pr, pi = carry
        slab = pl.ds(t, nck, stride=CHUNK)
        hr_ref[slab, :] = hr_ref[slab, :] + (pr * s_r - pi * s_i)
        hi_ref[slab, :] = hi_ref[slab, :] + (pr * s_i + pi * s_r)
        return _cmul(pr, pi, arb, aib)

    lax.fori_loop(0, CHUNK, fix, (arb, aib), unroll=8)


def _scan_backward(gr_ref, gi_ref, hr_ref, hi_ref, er_ref, ei_ref, sr_ref, si_ref, ar, ai, nck):
    arb, aib = jnp.broadcast_to(ar, (nck, LANES)), jnp.broadcast_to(-ai, (nck, LANES))

    def intra(k, carry):
        sr, si = carry
        slab = pl.ds(CHUNK - 1 - k, nck, stride=CHUNK)
        nr = arb * sr - aib * si + gr_ref[slab, :]
        ni = arb * si + aib * sr + gi_ref[slab, :]
        gr_ref[slab, :] = nr
        gi_ref[slab, :] = ni
        return nr, ni

    zero = jnp.zeros((nck, LANES), F32)
    er_ref[...], ei_ref[...] = lax.fori_loop(0, CHUNK, intra, (zero, zero), unroll=8)
    pcr, pci = _pow_chunk(ar, -ai)
    sr_ref[nck - 1:nck, :] = jnp.zeros((1, LANES), F32)
    si_ref[nck - 1:nck, :] = jnp.zeros((1, LANES), F32)
    for ck in range(nck - 1, 0, -1):
        pr, pi = sr_ref[ck:ck + 1, :], si_ref[ck:ck + 1, :]
        sr_ref[ck - 1:ck, :] = pcr * pr - pci * pi + er_ref[ck:ck + 1, :]
        si_ref[ck - 1:ck, :] = pcr * pi + pci * pr + ei_ref[ck:ck + 1, :]
    s_r, s_i = sr_ref[...], si_ref[...]
    last = pl.ds(CHUNK - 1, nck, stride=CHUNK)
    row = lax.broadcasted_iota(jnp.int32, (nck, LANES), 0)
    hp_r = jnp.where(row == 0, 0.0, pltpu.roll(hr_ref[last, :], 1, axis=0)) if nck > 1 else zero
    hp_i = jnp.where(row == 0, 0.0, pltpu.roll(hi_ref[last, :], 1, axis=0)) if nck > 1 else zero

    def fix(k, carry):
        pr, pi, acr, aci = carry
        t = CHUNK - 1 - k
        slab = pl.ds(t, nck, stride=CHUNK)
        g_r = gr_ref[slab, :] + (pr * s_r - pi * s_i)
        g_i = gi_ref[slab, :] + (pr * s_i + pi * s_r)
        gr_ref[slab, :] = g_r
        gi_ref[slab, :] = g_i
        prev = pl.ds(jnp.maximum(t - 1, 0), nck, stride=CHUNK)
        h_r = jnp.where(t == 0, hp_r, hr_ref[prev, :])
        h_i = jnp.where(t == 0, hp_i, hi_ref[prev, :])
        acr = acr + g_r * h_r + g_i * h_i
        aci = aci + g_i * h_r - g_r * h_i
        nr, ni = _cmul(pr, pi, arb, aib)
        return nr, ni, acr, aci

    _, _, acr, aci = lax.fori_loop(0, CHUNK, fix, (arb, aib, zero, zero), unroll=8)
    return jnp.sum(acr, axis=0, keepdims=True), jnp.sum(aci, axis=0, keepdims=True)


def _s5_specs(rows, e):
    sb = STATE_BLOCKS
    chan = pl.BlockSpec((rows, LANES), lambda j: (0, j // sb))
    bmat = pl.BlockSpec((None, LANES, LANES), lambda j: (j // sb, 0, j % sb))
    cmat = pl.BlockSpec((None, LANES, LANES), lambda j: (j // sb, j % sb, 0))
    avec = pl.BlockSpec((1, LANES), lambda j: (0, j))
    dvec = pl.BlockSpec((1, LANES), lambda j: (0, j // sb))
    return chan, bmat, cmat, avec, dvec


def _s5_fwd(name, u, bre, bim, crt, cit, ar, ai, dd):
    rows, e = u.shape
    nck = rows // CHUNK
    nsteps = (e // LANES) * STATE_BLOCKS
    chan, bmat, cmat, avec, dvec = _s5_specs(rows, e)

    def body(u_ref, br_ref, bi_ref, cr_ref, ci_ref, ar_ref, ai_ref, dd_ref, y_ref, gy_ref,
             hr_ref, hi_ref, er_ref, ei_ref, sr_ref, si_ref, acc_ref):
        j = pl.program_id(0) % STATE_BLOCKS
        ub = u_ref[...].astype(BF16)
        hr_ref[...] = jnp.dot(ub, br_ref[...], preferred_element_type=F32)
        hi_ref[...] = jnp.dot(ub, bi_ref[...], preferred_element_type=F32)
        _scan_forward(hr_ref, hi_ref, er_ref, ei_ref, sr_ref, si_ref, ar_ref[...], ai_ref[...], nck)
        contrib = (jnp.dot(hr_ref[...].astype(BF16), cr_ref[...], preferred_element_type=F32)
                   - jnp.dot(hi_ref[...].astype(BF16), ci_ref[...], preferred_element_type=F32))

        @pl.when(j == 0)
        def _():
            acc_ref[...] = dd_ref[...] * u_ref[...] + contrib

        @pl.when(j > 0)
        def _():
            acc_ref[...] += contrib

        @pl.when(j == STATE_BLOCKS - 1)
        def _():
            y = acc_ref[...]
            y_ref[...] = y.astype(y_ref.dtype)
            gy_ref[...] = _gelu(y).astype(gy_ref.dtype)

    big = pltpu.VMEM((rows, LANES), F32)
    small = pltpu.VMEM((nck, LANES), F32)
    out = jax.ShapeDtypeStruct((rows, e), BF16)
    return pl.pallas_call(
        body, name=name, grid=(nsteps,), in_specs=[chan, bmat, bmat, cmat, cmat, avec, avec, dvec],
        out_specs=[chan, chan], out_shape=[out, out], scratch_shapes=[big, big, small, small, small, small, big],
        compiler_params=_params("arbitrary"))(u, bre, bim, crt, cit, ar, ai, dd)


def _s5_bwd(name, u, y, dgy, bre, bim, crt, cit, ar, ai, dd):
    rows, e = u.shape
    nb = e // LANES
    nck = rows // CHUNK
    nsteps = nb * STATE_BLOCKS
    chan, bmat, cmat, avec, dvec = _s5_specs(rows, e)

    def body(u_ref, y_ref, dgy_ref, br_ref, bi_ref, cr_ref, ci_ref, ar_ref, ai_ref, dd_ref,
             du_ref, dbr_ref, dbi_ref, dcr_ref, dci_ref, dar_ref, dai_ref, ddd_ref,
             hr_ref, hi_ref, gr_ref, gi_ref, er_ref, ei_ref, sr_ref, si_ref, acc_ref):
        j = pl.program_id(0) % STATE_BLOCKS
        uf = u_ref[...]
        ub = uf.astype(BF16)
        hr_ref[...] = jnp.dot(ub, br_ref[...], preferred_element_type=F32)
        hi_ref[...] = jnp.dot(ub, bi_ref[...], preferred_element_type=F32)
        _scan_forward(hr_ref, hi_ref, er_ref, ei_ref, sr_ref, si_ref, ar_ref[...], ai_ref[...], nck)
        dy = dgy_ref[...].astype(F32) * _gelu_grad(y_ref[...].astype(F32))
        dyb = dy.astype(BF16)
        gr_ref[...] = lax.dot_general(dyb, cr_ref[...], _DIMS['nt'], preferred_element_type=F32)
        gi_ref[...] = -lax.dot_general(dyb, ci_ref[...], _DIMS['nt'], preferred_element_type=F32)
        dcr_ref[...] = lax.dot_general(hr_ref[...].astype(BF16), dyb, _DIMS['tn'], preferred_element_type=F32)
        dci_ref[...] = -lax.dot_general(hi_ref[...].astype(BF16), dyb, _DIMS['tn'], preferred_element_type=F32)
        dar_ref[...], dai_ref[...] = _scan_backward(gr_ref, gi_ref, hr_ref, hi_ref, er_ref, ei_ref, sr_ref, si_ref,
                                                    ar_ref[...], ai_ref[...], nck)
        grb, gib = gr_ref[...].astype(BF16), gi_ref[...].astype(BF16)
        dbr_ref[...] = lax.dot_general(ub, grb, _DIMS['tn'], preferred_element_type=F32)
        dbi_ref[...] = lax.dot_general(ub, gib, _DIMS['tn'], preferred_element_type=F32)
        contrib = (lax.dot_general(grb, br_ref[...], _DIMS['nt'], preferred_element_type=F32)
                   + lax.dot_general(gib, bi_ref[...], _DIMS['nt'], preferred_element_type=F32))

        @pl.when(j == 0)
        def _():
            acc_ref[...] = dd_ref[...] * dy + contrib
            ddd_ref[...] = jnp.sum(dy * uf, axis=0, keepdims=True)

        @pl.when(j > 0)
        def _():
            acc_ref[...] += contrib

        @pl.when(j == STATE_BLOCKS - 1)
        def _():
            du_ref[...] = acc_ref[...]

    big = pltpu.VMEM((rows, LANES), F32)
    small = pltpu.VMEM((nck, LANES), F32)
    bshape = jax.ShapeDtypeStruct((nb, LANES, LANES * STATE_BLOCKS), F32)
    cshape = jax.ShapeDtypeStruct((nb, LANES * STATE_BLOCKS, LANES), F32)
    ashape = jax.ShapeDtypeStruct((1, nb * LANES * STATE_BLOCKS), F32)
    return pl.pallas_call(
        body, name=name, grid=(nsteps,),
        in_specs=[chan, chan, chan, bmat, bmat, cmat, cmat, avec, avec, dvec],
        out_specs=[chan, bmat, bmat, cmat, cmat, avec, avec, dvec],
        out_shape=[jax.ShapeDtypeStruct((rows, e), F32), bshape, bshape, cshape, cshape, ashape, ashape,
                   jax.ShapeDtypeStruct((1, e), F32)],
        scratch_shapes=[big, big, big, big, small, small, small, small, big],
        compiler_params=_params("arbitrary"))(u, y, dgy, bre, bim, crt, cit, ar, ai, dd)


def _to_blockdiag_b(bbar, nb):
    eye = jnp.eye(GROUPS_PER_BLOCK, dtype=bbar.dtype)
    t = jnp.einsum('bgpc,gh->bgchp', bbar.reshape(nb, GROUPS_PER_BLOCK, SSM_STATE, SSM_GROUP), eye)
    return t.reshape(nb, LANES, GROUPS_PER_BLOCK * SSM_STATE)


def _from_blockdiag_b(dmat, nb):
    eye = jnp.eye(GROUPS_PER_BLOCK, dtype=dmat.dtype)
    t = dmat.reshape(nb, GROUPS_PER_BLOCK, SSM_GROUP, GROUPS_PER_BLOCK, SSM_STATE)
    return jnp.einsum('bgchp,gh->bgpc', t, eye).reshape(nb * GROUPS_PER_BLOCK, SSM_STATE, SSM_GROUP)


def _to_blockdiag_ct(c, nb):
    eye = jnp.eye(GROUPS_PER_BLOCK, dtype=c.dtype)
    t = jnp.einsum('bgop,gh->bgpho', c.reshape(nb, GROUPS_PER_BLOCK, SSM_GROUP, SSM_STATE), eye)
    return t.reshape(nb, GROUPS_PER_BLOCK * SSM_STATE, LANES)


def _from_blockdiag_ct(dmat, nb):
    eye = jnp.eye(GROUPS_PER_BLOCK, dtype=dmat.dtype)
    t = dmat.reshape(nb, GROUPS_PER_BLOCK, SSM_STATE, GROUPS_PER_BLOCK, SSM_GROUP)
    return jnp.einsum('bgpho,gh->bgop', t, eye).reshape(nb * GROUPS_PER_BLOCK, SSM_GROUP, SSM_STATE)


def _half_specs(kind, rdim, cdim, tr, tc):
    nr, nc = rdim // tr, cdim // tc
    if kind == 'col':
        nat = pl.BlockSpec((None, tr, tc), lambda c, l, rb, cb: (l, c * nr + rb, cb))
    else:
        nat = pl.BlockSpec((None, tr, tc), lambda c, l, rb, cb: (l, rb, c * nc + cb))
    half = pl.BlockSpec((None, None, tr, tc), lambda c, l, rb, cb: (c, l, rb, cb))
    return nat, half


def _cast_halves(name, w, kind):
    layers = w.shape[0]
    rdim, cdim = _half_shape(kind, w.shape)
    tr, tc = _tile(rdim, 512, 16), _tile(cdim, 1408, LANES)
    nat, half = _half_specs(kind, rdim, cdim, tr, tc)

    def body(w_ref, o_ref):
        o_ref[...] = w_ref[...].astype(o_ref.dtype)

    return pl.pallas_call(
        body, name=name, grid=(2, layers, rdim // tr, cdim // tc), in_specs=[nat], out_specs=half,
        out_shape=jax.ShapeDtypeStruct((2, layers, rdim, cdim), BF16),
        compiler_params=_params("parallel", "parallel", "parallel", "parallel"))(w)


def _adam_math(w, g, m, v):
    m = ADAM_B1 * m + (1.0 - ADAM_B1) * g
    v = ADAM_B2 * v + (1.0 - ADAM_B2) * (g * g)
    m_hat = m / (1.0 - ADAM_B1 ** ADAM_STEP)
    v_hat = v / (1.0 - ADAM_B2 ** ADAM_STEP)
    delta = -ADAM_LR * (m_hat / (jnp.sqrt(v_hat) + ADAM_EPS) + ADAM_WD * w)
    return delta, m, v


def _adam_big(name, w, m, v, gfull, kind):
    layers = w.shape[0]
    rdim, cdim = _half_shape(kind, w.shape)
    tr, tc = _tile(rdim, 256, 8), _tile(cdim, 1408, LANES)
    nat, half = _half_specs(kind, rdim, cdim, tr, tc)

    def body(w_ref, m_ref, v_ref, g_ref, go_ref, d_ref, mo_ref, vo_ref):
        g = g_ref[...]
        go_ref[...] = g
        d_ref[...], mo_ref[...], vo_ref[...] = _adam_math(w_ref[...], g, m_ref[...], v_ref[...])

    shape = jax.ShapeDtypeStruct(w.shape, F32)
    return pl.pallas_call(
        body, name=name, grid=(2, layers, rdim // tr, cdim // tc), in_specs=[nat, nat, nat, half],
        out_specs=[nat, nat, nat, nat], out_shape=[shape, shape, shape, shape],
        compiler_params=_params("parallel", "parallel", "parallel", "parallel"))(w, m, v, gfull)


def _adam_small(w, m, v, g):
    rows = w.shape[0]
    tr = _tile(rows, 512, 8)
    spec = pl.BlockSpec((tr, LANES), lambda i: (i, 0))

    def body(w_ref, m_ref, v_ref, g_ref, d_ref, mo_ref, vo_ref):
        d_ref[...], mo_ref[...], vo_ref[...] = _adam_math(w_ref[...], g_ref[...], m_ref[...], v_ref[...])

    shape = jax.ShapeDtypeStruct(w.shape, F32)
    return pl.pallas_call(body, name="adam_small", grid=(rows // tr,), in_specs=[spec] * 4, out_specs=[spec] * 3,
                          out_shape=[shape] * 3, compiler_params=_params("parallel"))(w, m, v, g)


def _add2(name, a, b):
    cdim = a.shape[-1]
    a2, b2 = a.reshape(-1, cdim), b.reshape(-1, cdim)
    rows = a2.shape[0]
    tr, tc = _tile(rows, 512, 16), _tile(cdim, 1408, LANES)
    spec = pl.BlockSpec((tr, tc), lambda i, j: (i, j))

    def body(a_ref, b_ref, o_ref):
        o_ref[...] = (a_ref[...].astype(F32) + b_ref[...].astype(F32)).astype(o_ref.dtype)

    out = pl.pallas_call(body, name=name, grid=(rows // tr, cdim // tc), in_specs=[spec, spec], out_specs=spec,
                         out_shape=jax.ShapeDtypeStruct(a2.shape, BF16),
                         compiler_params=_params("parallel", "parallel"))(a2, b2)
    return out.reshape(a.shape)


def _add4(name, own, recv):
    cdim = own.shape[-1]
    o2 = own.reshape(-1, cdim)
    r3 = recv.reshape(3, -1, cdim)
    rows = o2.shape[0]
    tr, tc = _tile(rows, 512, 16), _tile(cdim, 1408, LANES)
    spec = pl.BlockSpec((tr, tc), lambda i, j: (i, j))
    rspec = lambda k: pl.BlockSpec((None, tr, tc), lambda i, j: (k, i, j))

    def body(o_ref, x_ref, y_ref, d_ref, out_ref):
        out_ref[...] = ((o_ref[...].astype(F32) + d_ref[...].astype(F32))
                        + (x_ref[...].astype(F32) + y_ref[...].astype(F32)))

    out = pl.pallas_call(body, name=name, grid=(rows // tr, cdim // tc), in_specs=[spec, rspec(0), rspec(1), rspec(2)],
                         out_specs=spec, out_shape=jax.ShapeDtypeStruct(o2.shape, F32),
                         compiler_params=_params("parallel", "parallel"))(o2, r3, r3, r3)
    return out.reshape(own.shape)


def _place():
    x, y, c = lax.axis_index("x"), lax.axis_index("y"), lax.axis_index("c")
    chips = [(1 - x, y), (x, 1 - y), (1 - x, 1 - y)]
    return x, y, c, chips


ANY = pl.BlockSpec(memory_space=pl.ANY)


def _remote(src, dst, send, recv, to):
    return pltpu.make_async_remote_copy(src_ref=src, dst_ref=dst, send_sem=send, recv_sem=recv, device_id=to,
                                        device_id_type=MESH)


def _allgather_big(halves):
    n = len(halves)

    def body(*refs):
        ins, outs = refs[:n], refs[n:2 * n]
        send, recv, loc = refs[2 * n:]
        x, y, c, chips = _place()
        kme = 2 * x + y
        sib = (x, y, 1 - c)
        local = [pltpu.make_async_copy(ins[a], outs[a].at[kme], loc.at[a]) for a in range(n)]
        for cp in local:
            cp.start()
        first = [_remote(ins[a].at[c], outs[a].at[kme, c], send.at[6 * a + r], recv.at[6 * a + r], (*chip, c))
                 for a in range(n) for r, chip in enumerate(chips)]
        for cp in first:
            cp.start()
        passed = []
        for a in range(n):
            for r, chip in enumerate(chips):
                kp = 2 * chip[0] + chip[1]
                _remote(ins[a].at[c], outs[a].at[kp, c], send.at[6 * a + r], recv.at[6 * a + r], (*chip, c)).wait_recv()
                fw = _remote(outs[a].at[kp, c], outs[a].at[kp, c], send.at[6 * a + 3 + r], recv.at[6 * a + 3 + r], sib)
                fw.start()
                passed.append(fw)
        for a in range(n):
            for r, chip in enumerate(chips):
                kp = 2 * chip[0] + chip[1]
                _remote(outs[a].at[kp, 1 - c], outs[a].at[kp, 1 - c], send.at[6 * a + 3 + r], recv.at[6 * a + 3 + r],
                        sib).wait_recv()
        for cp in first + passed:
            cp.wait_send()
        for cp in local:
            cp.wait()

    return pl.pallas_call(
        body, name="allgather_weights", in_specs=[ANY] * n, out_specs=[ANY] * n,
        out_shape=[jax.ShapeDtypeStruct((4, *h.shape), h.dtype) for h in halves],
        scratch_shapes=[pltpu.SemaphoreType.DMA((6 * n,)), pltpu.SemaphoreType.DMA((6 * n,)),
                        pltpu.SemaphoreType.DMA((n,))])(*halves)


def _allgather_small(shards):
    n = len(shards)

    def body(*refs):
        ins, outs = refs[:n], refs[n:2 * n]
        send, recv, loc = refs[2 * n:]
        x, y, c, chips = _place()
        kme = 2 * x + y
        local = [pltpu.make_async_copy(ins[a], outs[a].at[kme], loc.at[a]) for a in range(n)]
        for cp in local:
            cp.start()
        cps = [_remote(ins[a], outs[a].at[kme], send.at[3 * a + r], recv.at[3 * a + r], (*chip, c))
               for a in range(n) for r, chip in enumerate(chips)]
        for cp in cps:
            cp.start()
        for a in range(n):
            for r, chip in enumerate(chips):
                kp = 2 * chip[0] + chip[1]
                _remote(ins[a], outs[a].at[kp], send.at[3 * a + r], recv.at[3 * a + r], (*chip, c)).wait_recv()
        for cp in cps:
            cp.wait_send()
        for cp in local:
            cp.wait()

    return pl.pallas_call(
        body, name="allgather_small", in_specs=[ANY] * n, out_specs=[ANY] * n,
        out_shape=[jax.ShapeDtypeStruct((4, *s.shape), s.dtype) for s in shards],
        scratch_shapes=[pltpu.SemaphoreType.DMA((3 * n,)), pltpu.SemaphoreType.DMA((3 * n,)),
                        pltpu.SemaphoreType.DMA((n,))])(*shards)


def _swap_halves(parts):
    n = len(parts)

    def body(*refs):
        ins, own, got = refs[:n], refs[n:2 * n], refs[2 * n:3 * n]
        send, recv, loc = refs[3 * n:]
        x, y, c, _ = _place()
        sib = (x, y, 1 - c)
        local = [pltpu.make_async_copy(ins[a].at[:, c], own[a], loc.at[a]) for a in range(n)]
        cps = [_remote(ins[a].at[:, 1 - c], got[a], send.at[a], recv.at[a], sib) for a in range(n)]
        for cp in local + cps:
            cp.start()
        for cp in cps:
            cp.wait()
        for cp in local:
            cp.wait()

    shapes = [jax.ShapeDtypeStruct((4, *p.shape[2:]), p.dtype) for p in parts]
    res = pl.pallas_call(
        body, name="grad_swap_halves", in_specs=[ANY] * n, out_specs=[ANY] * (2 * n), out_shape=shapes + shapes,
        scratch_shapes=[pltpu.SemaphoreType.DMA((n,)), pltpu.SemaphoreType.DMA((n,)),
                        pltpu.SemaphoreType.DMA((n,))])(*parts)
    return res[:n], res[n:]


def _scatter_chips(sums):
    n = len(sums)

    def body(*refs):
        ins, own, got = refs[:n], refs[n:2 * n], refs[2 * n:3 * n]
        send, recv, loc = refs[3 * n:]
        x, y, c, chips = _place()
        kme = 2 * x + y
        local = [pltpu.make_async_copy(ins[a].at[kme], own[a], loc.at[a]) for a in range(n)]
        cps = [_remote(ins[a].at[2 * chip[0] + chip[1]], got[a].at[r], send.at[3 * a + r], recv.at[3 * a + r], (*chip, c))
               for a in range(n) for r, chip in enumerate(chips)]
        for cp in local + cps:
            cp.start()
        for cp in cps:
            cp.wait()
        for cp in local:
            cp.wait()

    own_shapes = [jax.ShapeDtypeStruct(s.shape[1:], s.dtype) for s in sums]
    got_shapes = [jax.ShapeDtypeStruct((3, *s.shape[1:]), s.dtype) for s in sums]
    res = pl.pallas_call(
        body, name="grad_scatter_chips", in_specs=[ANY] * n, out_specs=[ANY] * (2 * n),
        out_shape=own_shapes + got_shapes,
        scratch_shapes=[pltpu.SemaphoreType.DMA((3 * n,)), pltpu.SemaphoreType.DMA((3 * n,)),
                        pltpu.SemaphoreType.DMA((n,))])(*sums)
    return res[:n], res[n:]


def _join_halves(halves):
    n = len(halves)

    def body(*refs):
        ins, outs = refs[:n], refs[n:2 * n]
        send, recv, loc = refs[2 * n:]
        x, y, c, _ = _place()
        sib = (x, y, 1 - c)
        local = [pltpu.make_async_copy(ins[a], outs[a].at[c], loc.at[a]) for a in range(n)]
        cps = [_remote(ins[a], outs[a].at[c], send.at[a], recv.at[a], sib) for a in range(n)]
        for cp in local + cps:
            cp.start()
        for a in range(n):
            _remote(ins[a], outs[a].at[1 - c], send.at[a], recv.at[a], sib).wait_recv()
        for cp in cps:
            cp.wait_send()
        for cp in local:
            cp.wait()

    return pl.pallas_call(
        body, name="grad_join_halves", in_specs=[ANY] * n, out_specs=[ANY] * n,
        out_shape=[jax.ShapeDtypeStruct((2, *h.shape), h.dtype) for h in halves],
        scratch_shapes=[pltpu.SemaphoreType.DMA((n,)), pltpu.SemaphoreType.DMA((n,)),
                        pltpu.SemaphoreType.DMA((n,))])(*halves)


def _allreduce_small(packed):
    rows = packed.shape[0]
    half = rows // 2

    def body(in_ref, out_ref, q_ref, s_ref, t_ref, send, recv):
        x, y, c, chips = _place()
        sib = (x, y, 1 - c)
        mine = pl.ds(pl.multiple_of(c * half, 8), half)
        theirs = pl.ds(pl.multiple_of((1 - c) * half, 8), half)
        first = _remote(in_ref.at[theirs], q_ref, send.at[0], recv.at[0], sib)
        first.start()
        first.wait()
        s_ref[...] = in_ref[mine, :] + q_ref[...]
        cps = [_remote(s_ref, t_ref.at[r], send.at[1 + r], recv.at[1 + r], (*chip, c)) for r, chip in enumerate(chips)]
        for cp in cps:
            cp.start()
        for cp in cps:
            cp.wait()
        out_ref[mine, :] = (s_ref[...] + t_ref[2]) + (t_ref[0] + t_ref[1])
        last = _remote(out_ref.at[mine], out_ref.at[mine], send.at[4], recv.at[4], sib)
        last.start()
        _remote(out_ref.at[theirs], out_ref.at[theirs], send.at[4], recv.at[4], sib).wait_recv()
        last.wait_send()

    vm = pl.BlockSpec(memory_space=pltpu.VMEM)
    return pl.pallas_call(
        body, name="allreduce_small", in_specs=[vm], out_specs=vm, out_shape=jax.ShapeDtypeStruct(packed.shape, F32),
        scratch_shapes=[pltpu.VMEM((half, LANES), F32), pltpu.VMEM((half, LANES), F32),
                        pltpu.VMEM((3, half, LANES), F32), pltpu.SemaphoreType.DMA((5,)), pltpu.SemaphoreType.DMA((5,))],
        compiler_params=pltpu.CompilerParams(vmem_limit_bytes=VMEM_LIMIT))(packed)


PACK_ROWS = 16


def _pack(arrs):
    parts, total = [], 0
    for a in arrs:
        flat = a.reshape(-1)
        rows = -(-flat.shape[0] // (LANES * PACK_ROWS)) * PACK_ROWS
        parts.append(jnp.pad(flat, (0, rows * LANES - flat.shape[0])).reshape(rows, LANES))
        total += rows
    return jnp.concatenate(parts, axis=0)


def _unpack(packed, shapes):
    out, row = [], 0
    for shp in shapes:
        size = math.prod(shp)
        rows = -(-size // (LANES * PACK_ROWS)) * PACK_ROWS
        out.append(packed[row:row + rows].reshape(-1)[:size].reshape(shp))
        row += rows
    return out


def kernel(x, norm_mix_g, norm_ffn_g, a_w_in, a_g_v, a_w_s, a_b_s, a_w_out, b_w_in, b_a_re, b_a_im, b_log_dt, b_b_re, b_b_im, b_c_re, b_c_im, b_d, b_w_glu, f_w_up, f_conv_w, f_conv_b, f_w_down, final_g, loss_target, m_norm_mix_g, m_norm_ffn_g, m_a_w_in, m_a_g_v, m_a_w_s, m_a_b_s, m_a_w_out, m_b_w_in, m_b_a_re, m_b_a_im, m_b_log_dt, m_b_b_re, m_b_b_im, m_b_c_re, m_b_c_im, m_b_d, m_b_w_glu, m_f_w_up, m_f_conv_w, m_f_conv_b, m_f_w_down, m_final_g, v_norm_mix_g, v_norm_ffn_g, v_a_w_in, v_a_g_v, v_a_w_s, v_a_b_s, v_a_w_out, v_b_w_in, v_b_a_re, v_b_a_im, v_b_log_dt, v_b_b_re, v_b_b_im, v_b_c_re, v_b_c_im, v_b_d, v_b_w_glu, v_f_w_up, v_f_conv_w, v_f_conv_b, v_f_w_down, v_final_g):
    w = dict(norm_mix_g=norm_mix_g, norm_ffn_g=norm_ffn_g, a_w_in=a_w_in, a_g_v=a_g_v, a_w_s=a_w_s, a_b_s=a_b_s,
             a_w_out=a_w_out, b_w_in=b_w_in, b_a_re=b_a_re, b_a_im=b_a_im, b_log_dt=b_log_dt, b_b_re=b_b_re,
             b_b_im=b_b_im, b_c_re=b_c_re, b_c_im=b_c_im, b_d=b_d, b_w_glu=b_w_glu, f_w_up=f_w_up, f_conv_w=f_conv_w,
             f_conv_b=f_conv_b, f_w_down=f_w_down, final_g=final_g)
    mom = dict(norm_mix_g=m_norm_mix_g, norm_ffn_g=m_norm_ffn_g, a_w_in=m_a_w_in, a_g_v=m_a_g_v, a_w_s=m_a_w_s,
               a_b_s=m_a_b_s, a_w_out=m_a_w_out, b_w_in=m_b_w_in, b_a_re=m_b_a_re, b_a_im=m_b_a_im,
               b_log_dt=m_b_log_dt, b_b_re=m_b_b_re, b_b_im=m_b_b_im, b_c_re=m_b_c_re, b_c_im=m_b_c_im, b_d=m_b_d,
               b_w_glu=m_b_w_glu, f_w_up=m_f_w_up, f_conv_w=m_f_conv_w, f_conv_b=m_f_conv_b, f_w_down=m_f_w_down,
               final_g=m_final_g)
    var = dict(norm_mix_g=v_norm_mix_g, norm_ffn_g=v_norm_ffn_g, a_w_in=v_a_w_in, a_g_v=v_a_g_v, a_w_s=v_a_w_s,
               a_b_s=v_a_b_s, a_w_out=v_a_w_out, b_w_in=v_b_w_in, b_a_re=v_b_a_re, b_a_im=v_b_a_im,
               b_log_dt=v_b_log_dt, b_b_re=v_b_b_re, b_b_im=v_b_b_im, b_c_re=v_b_c_re, b_c_im=v_b_c_im, b_d=v_b_d,
               b_w_glu=v_b_w_glu, f_w_up=v_f_w_up, f_conv_w=v_f_conv_w, f_conv_b=v_f_conv_b, f_w_down=v_f_w_down,
               final_g=v_final_g)

    rows, d = x.shape[1], x.shape[2]
    depth = norm_mix_g.shape[0]
    kchip = 2 * lax.axis_index("x") + lax.axis_index("y")
    big_names = list(BIG)
    dims = {n: _full_dims(BIG[n], w[n].shape) for n in big_names}

    halves = [_cast_halves("cast_" + n, w[n], BIG[n]) for n in big_names]
    gathered = dict(zip(big_names, _allgather_big(halves)))
    bd_all, cw_all = _allgather_small([b_d, f_conv_w.reshape(-1, f_conv_w.shape[-1])])
    bd_full = jnp.swapaxes(bd_all, 0, 1).reshape(b_d.shape[0], -1)
    cw_full = jnp.transpose(cw_all.reshape(4, *f_conv_w.shape), (1, 2, 0, 3)).reshape(depth, f_conv_w.shape[1], -1)

    pgrad = {n: lax.empty((4, 2, *h.shape[1:]), BF16) for n, h in zip(big_names, halves)}
    sgrad = {}

    def mm(name, a, wn, layer, out_dtype, residual=None):
        return _mm_x_w(name, a, gathered[wn], BIG[wn], *dims[wn], layer, out_dtype, residual)

    def mm_t(name, dy, wn, layer, out_dtype):
        return _mm_dy_wt(name, dy, gathered[wn], BIG[wn], *dims[wn], layer, out_dtype)

    def mm_g(name, xa, dy, wn, layer):
        pgrad[wn] = _mm_xt_dy(name, xa, dy, pgrad[wn], BIG[wn], *dims[wn], layer)

    e = d
    nb = e // LANES
    heads = e // SGU_GROUP
    h = x[0]
    saved = []
    for i in range(depth):
        j = i // 2
        gm = norm_mix_g[i:i + 1]
        hn = _rms_fwd("rms_mix_fwd", h, gm)
        if i % 2 == 0:
            pre = mm("sgu_in", hn, 'a_w_in', j, BF16)
            bsx = jnp.broadcast_to(a_b_s[j][:, :, None], (heads, CHUNK, LANES))
            us = _sgu_mix_fwd("sgu_mix_fwd", pre, a_g_v[j:j + 1], a_w_s[j], bsx)
            h_mid = mm("sgu_out", us, 'a_w_out', j, F32, residual=h)
            mix = dict(h=h, hn=hn, pre=pre, us=us, bsx=bsx)
        else:
            groups = b_a_re.shape[1]
            rep = lambda t: jnp.repeat(t, SSM_GROUP, axis=1)
            lr, li = b_a_re[j], b_a_im[j]
            ldt = jnp.broadcast_to(b_log_dt[j][:, None], lr.shape)
            bflat = lambda t: t.reshape(groups, SSM_STATE * SSM_GROUP)
            disc_in = (lr, li, ldt, rep(lr), rep(li), rep(ldt), bflat(b_b_re[j]), bflat(b_b_im[j]))
            abr, abi, bbr, bbi = _s5_disc("s5_disc", *disc_in)
            shape_b = (groups, SSM_STATE, SSM_GROUP)
            bre = _to_blockdiag_b(bbr.reshape(shape_b), nb).astype(BF16)
            bim = _to_blockdiag_b(bbi.reshape(shape_b), nb).astype(BF16)
            crt = _to_blockdiag_ct(b_c_re[j], nb).astype(BF16)
            cit = _to_blockdiag_ct(b_c_im[j], nb).astype(BF16)
            ar_row, ai_row = abr.reshape(1, -1), abi.reshape(1, -1)
            dd = bd_full[j:j + 1]
            u = mm("s5_in", hn, 'b_w_in', j, F32)
            yv, gy = _s5_fwd("s5_fwd", u, bre, bim, crt, cit, ar_row, ai_row, dd)
            gg = mm("s5_glu", gy, 'b_w_glu', j, BF16)
            h_mid = _glu_fwd("glu_fwd", gg, h)
            mix = dict(h=h, hn=hn, u=u, y=yv, gy=gy, gg=gg, disc_in=disc_in, mats=(bre, bim, crt, cit, ar_row, ai_row, dd))
        gf = norm_ffn_g[i:i + 1]
        hn2 = _rms_fwd("rms_ffn_fwd", h_mid, gf)
        z = mm("ffn_up", hn2, 'f_w_up', i, BF16)
        cw, cb = cw_full[i], f_conv_b[i:i + 1]
        act = _ffn_act_fwd("ffn_act_fwd", z, cw, cb)
        h_out = mm("ffn_down", act, 'f_w_down', i, F32, residual=h_mid)
        saved.append((mix, dict(h=h_mid, hn=hn2, z=z, act=act, cw=cw, cb=cb)))
        h = h_out

    dh, g_final, loss_vec = _loss_head(h, final_g.reshape(1, d), loss_target[0])
    loss = lax.psum(jnp.sum(loss_vec), ("x", "y", "c"))
    sgrad['final_g'] = g_final.reshape(d)

    g_mix, g_ffn = [None] * depth, [None] * depth
    g_cw, g_cb = [None] * depth, [None] * depth
    sg = {k: [None] * (depth // 2) for k in ('a_g_v', 'a_w_s', 'a_b_s')}
    bg = {k: [None] * (depth // 2) for k in ('b_a_re', 'b_a_im', 'b_log_dt', 'b_b_re', 'b_b_im', 'b_c_re', 'b_c_im', 'b_d')}
    for i in reversed(range(depth)):
        j = i // 2
        mix, ffn = saved[i]
        d_act = mm_t("ffn_down_dx", dh, 'f_w_down', i, BF16)
        mm_g("ffn_down_dw", ffn['act'], dh, 'f_w_down', i)
        dacc, g_cw[i], g_cb[i] = _ffn_act_bwd("ffn_act_bwd", ffn['z'], d_act, ffn['cw'], ffn['cb'])
        dz = _conv_bwd("ffn_conv_bwd", dacc, ffn['cw'])
        mm_g("ffn_up_dw", ffn['hn'], dz, 'f_w_up', i)
        dhn = mm_t("ffn_up_dx", dz, 'f_w_up', i, F32)
        dh, g_ffn[i] = _rms_bwd("rms_ffn_bwd", ffn['h'], norm_ffn_g[i:i + 1], dhn, dh)
        if i % 2 == 0:
            dus = mm_t("sgu_out_dx", dh, 'a_w_out', j, BF16)
            mm_g("sgu_out_dw", mix['us'], dh, 'a_w_out', j)
            dpre, dws, dbs, dgv = _sgu_mix_bwd("sgu_mix_bwd", mix['pre'], dus, a_g_v[j:j + 1], a_w_s[j], mix['bsx'])
            sg['a_w_s'][j], sg['a_b_s'][j], sg['a_g_v'][j] = dws, dbs[:, :, 0], dgv[0]
            mm_g("sgu_in_dw", mix['hn'], dpre, 'a_w_in', j)
            dhn = mm_t("sgu_in_dx", dpre, 'a_w_in', j, F32)
        else:
            dgg = _glu_bwd("glu_bwd", mix['gg'], dh)
            mm_g("s5_glu_dw", mix['gy'], dgg, 'b_w_glu', j)
            dgy = mm_t("s5_glu_dx", dgg, 'b_w_glu', j, BF16)
            du, dbr, dbi, dcr, dci, dar, dai, ddd = _s5_bwd("s5_bwd", mix['u'], mix['y'], dgy, *mix['mats'])
            groups = b_a_re.shape[1]
            flat = lambda t: _from_blockdiag_b(t, nb).reshape(groups, SSM_STATE * SSM_GROUP)
            sel = jnp.repeat(jnp.eye(SSM_STATE, dtype=F32), SSM_GROUP, axis=0)
            dlr, dli, dldt, dbre, dbim = _s5_disc_bwd(
                "s5_disc_bwd", *mix['disc_in'], dar.reshape(groups, SSM_STATE), dai.reshape(groups, SSM_STATE),
                flat(dbr), flat(dbi), sel)
            bg['b_a_re'][j], bg['b_a_im'][j], bg['b_log_dt'][j] = dlr, dli, dldt[:, 0]
            bg['b_b_re'][j] = dbre.reshape(groups, SSM_STATE, SSM_GROUP)
            bg['b_b_im'][j] = dbim.reshape(groups, SSM_STATE, SSM_GROUP)
            bg['b_c_re'][j], bg['b_c_im'][j] = _from_blockdiag_ct(dcr, nb), _from_blockdiag_ct(dci, nb)
            bg['b_d'][j] = ddd[0]
            mm_g("s5_in_dw", mix['hn'], du, 'b_w_in', j)
            dhn = mm_t("s5_in_dx", du, 'b_w_in', j, F32)
        dh, g_mix[i] = _rms_bwd("rms_mix_bwd", mix['h'], norm_mix_g[i:i + 1], dhn, dh)
    grad_x = dh[None]

    sgrad['norm_mix_g'] = jnp.concatenate(g_mix, axis=0)
    sgrad['norm_ffn_g'] = jnp.concatenate(g_ffn, axis=0)
    sgrad['f_conv_w'] = jnp.stack(g_cw)
    sgrad['f_conv_b'] = jnp.concatenate(g_cb, axis=0)
    for k, v_ in list(sg.items()) + list(bg.items()):
        sgrad[k] = jnp.stack(v_)

    total = _allreduce_small(_pack([sgrad[n] for n in SMALL]))
    full_shapes = [sgrad[n].shape for n in SMALL]
    gsmall = dict(zip(SMALL, _unpack(total, full_shapes)))
    for n, axis in CHIP_SHARDED_SMALL.items():
        width = w[n].shape[axis]
        gsmall[n] = lax.dynamic_slice_in_dim(gsmall[n], kchip * width, width, axis=axis)
    pk = lambda t: _pack([t[n] for n in SMALL])
    gpacked = pk(gsmall)
    dpk, mpk, vpk = _adam_small(pk(w), pk(mom), pk(var), gpacked)
    shard_shapes = [w[n].shape for n in SMALL]
    out_g = dict(gsmall)
    out_d = dict(zip(SMALL, _unpack(dpk, shard_shapes)))
    out_m = dict(zip(SMALL, _unpack(mpk, shard_shapes)))
    out_v = dict(zip(SMALL, _unpack(vpk, shard_shapes)))

    own, got = _swap_halves([pgrad[n] for n in big_names])
    sums = [_add2("grad_chip_sum", o, g) for o, g in zip(own, got)]
    mine, recvd = _scatter_chips(sums)
    hsum = [_add4("grad_total", o, r) for o, r in zip(mine, recvd)]
    gfull = _join_halves(hsum)
    for n, gf_ in zip(big_names, gfull):
        out_g[n], out_d[n], out_m[n], out_v[n] = _adam_big("adam_" + n, w[n], mom[n], var[n], gf_, BIG[n])

    return (loss, grad_x, *[out_g[n] for n in W_NAMES], *[out_d[n] for n in W_NAMES],
            *[out_m[n] for n in W_NAMES], *[out_v[n] for n in W_NAMES])
```

```python
import functools
import itertools
import math

import jax
import jax.numpy as jnp
from jax import lax
from jax.experimental import pallas as pl
from jax.experimental.pallas import tpu as pltpu

F32, BF16 = jnp.float32, jnp.bfloat16
MESH = pl.DeviceIdType.MESH

CHUNK = 128
PITCH = CHUNK + 8
SGU_GROUP = 128
SSM_GROUP = 16
SSM_STATE = 64
EPS = 1e-6
LANES = 128
GROUPS_PER_BLOCK = LANES // SSM_GROUP
STATE_BLOCKS = SSM_STATE // SSM_GROUP
VMEM_LIMIT = 52 * 1024 * 1024

ADAM_LR, ADAM_B1, ADAM_B2, ADAM_EPS, ADAM_WD, ADAM_STEP = 0.001, 0.9, 0.999, 1e-08, 0.01, 10

W_NAMES = ['norm_mix_g', 'norm_ffn_g', 'a_w_in', 'a_g_v', 'a_w_s', 'a_b_s', 'a_w_out', 'b_w_in', 'b_a_re', 'b_a_im',
           'b_log_dt', 'b_b_re', 'b_b_im', 'b_c_re', 'b_c_im', 'b_d', 'b_w_glu', 'f_w_up', 'f_conv_w', 'f_conv_b',
           'f_w_down', 'final_g']
BIG = {'a_w_in': 'col', 'a_w_out': 'row', 'b_w_in': 'row', 'b_w_glu': 'col', 'f_w_up': 'col', 'f_w_down': 'row'}
SMALL = [n for n in W_NAMES if n not in BIG]
CHIP_SHARDED_SMALL = {'b_d': 1, 'f_conv_w': 2}


def _tile(n, pref, align):
    t = min(n, pref)
    t -= t % align
    while t >= align:
        if n % t == 0:
            return t
        t -= align
    return n


def _params(*sem):
    return pltpu.CompilerParams(dimension_semantics=sem, vmem_limit_bytes=VMEM_LIMIT)


def _gelu(x):
    c = math.sqrt(2.0 / math.pi)
    return 0.5 * x * (1.0 + jnp.tanh(c * (x + 0.044715 * x * x * x)))


def _gelu_grad(x):
    c = math.sqrt(2.0 / math.pi)
    t = jnp.tanh(c * (x + 0.044715 * x * x * x))
    return 0.5 * (1.0 + t) + 0.5 * x * (1.0 - t * t) * c * (1.0 + 3.0 * 0.044715 * x * x)


def _half_shape(kind, shard_shape):
    _, r, c = shard_shape
    return (r // 2, c) if kind == 'col' else (r, c // 2)


def _full_dims(kind, shard_shape):
    _, r, c = shard_shape
    return (r, 4 * c) if kind == 'col' else (4 * r, c)


def _gspec(kind, kdim, ndim, tr, tc, layer, rc):
    if kind == 'col':
        nr, nc = (kdim // 2) // tr, (ndim // 4) // tc

        def imap(*g):
            rb, cb = rc(*g)
            return (cb // nc, rb // nr, layer, rb % nr, cb % nc)
    else:
        nr, nc = (kdim // 4) // tr, (ndim // 2) // tc

        def imap(*g):
            rb, cb = rc(*g)
            return (rb // nr, cb // nc, layer, rb % nr, cb % nc)
    return pl.BlockSpec((None, None, None, tr, tc), imap)


def _wtiles(kind, kdim, ndim):
    if kind == 'col':
        return _tile(kdim // 2, 1024, LANES), _tile(ndim // 4, 1408, LANES)
    return _tile(kdim // 4, 1408, LANES), _tile(ndim // 2, 1024, LANES)


_DIMS = {'nn': (((1,), (0,)), ((), ())), 'nt': (((1,), (1,)), ((), ())), 'tn': (((0,), (0,)), ((), ()))}


def _matmul(name, mode, a, b, grid, a_spec, b_spec, out_shape, out_spec, acc_shape, extras=(), extra_specs=(),
            epilogue=None, aliases=None):
    nk = grid[2]
    dims = _DIMS[mode]
    n_extra = len(extras)

    def body(a_ref, b_ref, *rest):
        extra_refs, o_ref = rest[:n_extra], rest[n_extra]
        prod = lax.dot_general(a_ref[...].astype(BF16), b_ref[...].astype(BF16), dims, preferred_element_type=F32)

        def finish(r):
            if epilogue is not None:
                r = epilogue(r, *[e[...] for e in extra_refs])
            o_ref[...] = r.astype(o_ref.dtype)

        if nk == 1:
            finish(prod)
            return
        acc_ref = rest[n_extra + 1]
        kk = pl.program_id(2)

        @pl.when(kk == 0)
        def _():
            acc_ref[...] = prod

        @pl.when(kk > 0)
        def _():
            acc_ref[...] += prod

        @pl.when(kk == nk - 1)
        def _():
            finish(acc_ref[...])

    scratch = [pltpu.VMEM(acc_shape, F32)] if nk > 1 else []
    return pl.pallas_call(
        body, name=name, grid=grid, in_specs=[a_spec, b_spec, *extra_specs], out_specs=out_spec, out_shape=out_shape,
        scratch_shapes=scratch, input_output_aliases=aliases or {},
        compiler_params=_params("parallel", "parallel", "arbitrary"))(a, b, *extras)


def _mm_x_w(name, a, wg, kind, kdim, ndim, layer, out_dtype, residual=None):
    rows = a.shape[0]
    tk, tn = _wtiles(kind, kdim, ndim)
    tm = _tile(rows, 1024, 16)
    grid = (rows // tm, ndim // tn, kdim // tk)
    extras, especs, epi = (), (), None
    if residual is not None:
        extras, especs = (residual,), (pl.BlockSpec((tm, tn), lambda i, j, k: (i, j)),)
        epi = lambda r, res: r + res
    return _matmul(name, 'nn', a, wg, grid, pl.BlockSpec((tm, tk), lambda i, j, k: (i, k)),
                   _gspec(kind, kdim, ndim, tk, tn, layer, lambda i, j, k: (k, j)),
                   jax.ShapeDtypeStruct((rows, ndim), out_dtype), pl.BlockSpec((tm, tn), lambda i, j, k: (i, j)),
                   (tm, tn), extras, especs, epi)


def _mm_dy_wt(name, dy, wg, kind, kdim, ndim, layer, out_dtype):
    rows = dy.shape[0]
    tn, tk = _wtiles(kind, kdim, ndim)
    tm = _tile(rows, 1024, 16)
    grid = (rows // tm, kdim // tn, ndim // tk)
    return _matmul(name, 'nt', dy, wg, grid, pl.BlockSpec((tm, tk), lambda i, j, k: (i, k)),
                   _gspec(kind, kdim, ndim, tn, tk, layer, lambda i, j, k: (j, k)),
                   jax.ShapeDtypeStruct((rows, kdim), out_dtype), pl.BlockSpec((tm, tn), lambda i, j, k: (i, j)),
                   (tm, tn))


def _mm_xt_dy(name, xa, dy, pg, kind, kdim, ndim, layer):
    rows = xa.shape[0]
    tm, tn = _wtiles(kind, kdim, ndim)
    tl = _tile(rows, 1024, 16)
    grid = (kdim // tm, ndim // tn, rows // tl)
    return _matmul(name, 'tn', xa, dy, grid, pl.BlockSpec((tl, tm), lambda i, j, k: (k, i)),
                   pl.BlockSpec((tl, tn), lambda i, j, k: (k, j)),
                   jax.ShapeDtypeStruct(pg.shape, pg.dtype),
                   _gspec(kind, kdim, ndim, tm, tn, layer, lambda i, j, k: (i, j)), (tm, tn),
                   extras=(pg,), extra_specs=(pl.BlockSpec(memory_space=pl.ANY),), aliases={2: 0})


def _rms_fwd(name, h, g):
    rows, d = h.shape
    tm = _tile(rows, 256, 16)

    def body(h_ref, g_ref, o_ref):
        x = h_ref[...]
        r = lax.rsqrt(jnp.mean(x * x, axis=-1, keepdims=True) + EPS)
        o_ref[...] = (x * r * g_ref[...]).astype(o_ref.dtype)

    return pl.pallas_call(
        body, name=name, grid=(rows // tm,),
        in_specs=[pl.BlockSpec((tm, d), lambda i: (i, 0)), pl.BlockSpec((1, d), lambda i: (0, 0))],
        out_specs=pl.BlockSpec((tm, d), lambda i: (i, 0)), out_shape=jax.ShapeDtypeStruct((rows, d), BF16),
        compiler_params=_params("parallel"))(h, g)


def _rms_bwd(name, h, g, dhn, dres):
    rows, d = h.shape
    tm = _tile(rows, 256, 16)

    def body(h_ref, g_ref, dy_ref, dres_ref, dh_ref, dg_ref):
        x = h_ref[...]
        r = lax.rsqrt(jnp.mean(x * x, axis=-1, keepdims=True) + EPS)
        xh = x * r
        dy = dy_ref[...].astype(F32)
        gy = dy * g_ref[...]
        dh_ref[...] = dres_ref[...] + r * (gy - xh * jnp.mean(gy * xh, axis=-1, keepdims=True))
        part = jnp.sum(dy * xh, axis=0, keepdims=True)

        @pl.when(pl.program_id(0) == 0)
        def _():
            dg_ref[...] = part

        @pl.when(pl.program_id(0) > 0)
        def _():
            dg_ref[...] += part

    row = pl.BlockSpec((tm, d), lambda i: (i, 0))
    vec = pl.BlockSpec((1, d), lambda i: (0, 0))
    return pl.pallas_call(
        body, name=name, grid=(rows // tm,), in_specs=[row, vec, row, row], out_specs=[row, vec],
        out_shape=[jax.ShapeDtypeStruct((rows, d), F32), jax.ShapeDtypeStruct((1, d), F32)],
        compiler_params=_params("arbitrary"))(h, g, dhn, dres)


def _loss_head(h, g, target):
    rows, d = h.shape
    tm = _tile(rows, 256, 16)

    def body(h_ref, g_ref, t_ref, dh_ref, dg_ref, loss_ref):
        x = h_ref[...]
        r = lax.rsqrt(jnp.mean(x * x, axis=-1, keepdims=True) + EPS)
        xh = x * r
        err = xh * g_ref[...] - t_ref[...]
        dy = err * (1.0 / d)
        gy = dy * g_ref[...]
        dh_ref[...] = r * (gy - xh * jnp.mean(gy * xh, axis=-1, keepdims=True))
        part = jnp.sum(dy * xh, axis=0, keepdims=True)
        sq = jnp.sum(err * err, axis=0, keepdims=True) * (0.5 / d)

        @pl.when(pl.program_id(0) == 0)
        def _():
            dg_ref[...] = part
            loss_ref[...] = sq

        @pl.when(pl.program_id(0) > 0)
        def _():
            dg_ref[...] += part
            loss_ref[...] += sq

    row = pl.BlockSpec((tm, d), lambda i: (i, 0))
    vec = pl.BlockSpec((1, d), lambda i: (0, 0))
    return pl.pallas_call(
        body, name="loss_head", grid=(rows // tm,), in_specs=[row, vec, row], out_specs=[row, vec, vec],
        out_shape=[jax.ShapeDtypeStruct((rows, d), F32), jax.ShapeDtypeStruct((1, d), F32),
                   jax.ShapeDtypeStruct((1, d), F32)],
        compiler_params=_params("arbitrary"))(h, g, target)


def _shift_down(cur, prev8, first, k):
    rows = cur.shape[0]
    rolled = pltpu.roll(cur, k, axis=0)
    idx = lax.broadcasted_iota(jnp.int32, cur.shape, 0)
    prev8 = jnp.where(first, 0.0, prev8)
    out = rolled
    for r in range(k):
        out = jnp.where(idx == r, prev8[8 - k + r:8 - k + r + 1, :], out)
    del rows
    return out


def _shift_up(cur, next8, last, k):
    rows = cur.shape[0]
    rolled = pltpu.roll(cur, rows - k, axis=0)
    idx = lax.broadcasted_iota(jnp.int32, cur.shape, 0)
    next8 = jnp.where(last, 0.0, next8)
    out = rolled
    for r in range(k):
        out = jnp.where(idx == rows - k + r, next8[r:r + 1, :], out)
    return out


def _conv_acc(z, zprev, first, w_ref, b_ref):
    z1 = _shift_down(z, zprev, first, 1)
    z2 = _shift_down(z, zprev, first, 2)
    return b_ref[...] + w_ref[2:3, :] * z + w_ref[1:2, :] * z1 + w_ref[0:1, :] * z2, z1, z2


def _ffn_tiles(rows, f):
    return _tile(rows, 512, 16), _tile(f, 512, LANES)


def _ffn_act_fwd(name, z, cw, cb):
    rows, f2 = z.shape
    f = f2 // 2
    tm, tc = _ffn_tiles(rows, f)
    nf = f // tc
    hb = tm // 8

    def body(zg_ref, zgp_ref, zv_ref, zvp_ref, wg_ref, wv_ref, bg_ref, bv_ref, o_ref):
        first = pl.program_id(0) == 0
        gate, _, _ = _conv_acc(zg_ref[...].astype(F32), zgp_ref[...].astype(F32), first, wg_ref, bg_ref)
        val, _, _ = _conv_acc(zv_ref[...].astype(F32), zvp_ref[...].astype(F32), first, wv_ref, bv_ref)
        o_ref[...] = (gate * jax.nn.sigmoid(gate) * val).astype(o_ref.dtype)

    cur = lambda off: pl.BlockSpec((tm, tc), lambda i, j: (i, j + off))
    prev = lambda off: pl.BlockSpec((8, tc), lambda i, j: (jnp.maximum(i * hb - 1, 0), j + off))
    wsp = lambda off: pl.BlockSpec((3, tc), lambda i, j: (0, j + off))
    bsp = lambda off: pl.BlockSpec((1, tc), lambda i, j: (0, j + off))
    return pl.pallas_call(
        body, name=name, grid=(rows // tm, nf),
        in_specs=[cur(0), prev(0), cur(nf), prev(nf), wsp(0), wsp(nf), bsp(0), bsp(nf)],
        out_specs=pl.BlockSpec((tm, tc), lambda i, j: (i, j)), out_shape=jax.ShapeDtypeStruct((rows, f), BF16),
        compiler_params=_params("parallel", "parallel"))(z, z, z, z, cw, cw, cb, cb)


def _ffn_act_bwd(name, z, da, cw, cb):
    rows, f2 = z.shape
    f = f2 // 2
    tm, tc = _ffn_tiles(rows, f)
    nf = f // tc
    hb = tm // 8

    def body(zs_ref, zsp_ref, zo_ref, zop_ref, ws_ref, wo_ref, bs_ref, bo_ref, da_ref, dacc_ref, dcw_ref, dcb_ref):
        i = pl.program_id(1)
        first = i == 0
        is_gate = pl.program_id(0) < nf
        acc_s, z1, z2 = _conv_acc(zs_ref[...].astype(F32), zsp_ref[...].astype(F32), first, ws_ref, bs_ref)
        acc_o, _, _ = _conv_acc(zo_ref[...].astype(F32), zop_ref[...].astype(F32), first, wo_ref, bo_ref)
        d_a = da_ref[...].astype(F32)
        sig_s = jax.nn.sigmoid(acc_s)
        sig_o = jax.nn.sigmoid(acc_o)
        as_gate = d_a * acc_o * sig_s * (1.0 + acc_s * (1.0 - sig_s))
        as_val = d_a * acc_o * sig_o
        dacc = jnp.where(is_gate, as_gate, as_val)
        dacc_ref[...] = dacc.astype(dacc_ref.dtype)
        zc = zs_ref[...].astype(F32)
        taps = [jnp.sum(dacc * t, axis=0, keepdims=True) for t in (z2, z1, zc)]
        pb = jnp.sum(dacc, axis=0, keepdims=True)

        @pl.when(first)
        def _():
            for k in range(3):
                dcw_ref[k:k + 1, :] = taps[k]
            dcb_ref[...] = pb

        @pl.when(i > 0)
        def _():
            for k in range(3):
                dcw_ref[k:k + 1, :] += taps[k]
            dcb_ref[...] += pb

    n2 = 2 * nf
    other = lambda j: (j + nf) % n2
    cur = lambda col: pl.BlockSpec((tm, tc), lambda j, i: (i, col(j)))
    prev = lambda col: pl.BlockSpec((8, tc), lambda j, i: (jnp.maximum(i * hb - 1, 0), col(j)))
    wsp = lambda col: pl.BlockSpec((3, tc), lambda j, i: (0, col(j)))
    bsp = lambda col: pl.BlockSpec((1, tc), lambda j, i: (0, col(j)))
    ident = lambda j: j
    return pl.pallas_call(
        body, name=name, grid=(n2, rows // tm),
        in_specs=[cur(ident), prev(ident), cur(other), prev(other), wsp(ident), wsp(other), bsp(ident), bsp(other),
                  pl.BlockSpec((tm, tc), lambda j, i: (i, j % nf))],
        out_specs=[cur(ident), wsp(ident), bsp(ident)],
        out_shape=[jax.ShapeDtypeStruct((rows, f2), BF16), jax.ShapeDtypeStruct((3, f2), F32),
                   jax.ShapeDtypeStruct((1, f2), F32)],
        compiler_params=_params("parallel", "arbitrary"))(z, z, z, z, cw, cw, cb, cb, da)


def _conv_bwd(name, dacc, cw):
    rows, f2 = dacc.shape
    tm, tc = _ffn_tiles(rows, f2 // 2)
    hb = tm // 8
    nrow = rows // tm

    def body(d_ref, dn_ref, w_ref, o_ref):
        last = pl.program_id(0) == nrow - 1
        d = d_ref[...].astype(F32)
        nxt = dn_ref[...].astype(F32)
        d1 = _shift_up(d, nxt, last, 1)
        d2 = _shift_up(d, nxt, last, 2)
        o_ref[...] = (w_ref[2:3, :] * d + w_ref[1:2, :] * d1 + w_ref[0:1, :] * d2).astype(o_ref.dtype)

    return pl.pallas_call(
        body, name=name, grid=(nrow, f2 // tc),
        in_specs=[pl.BlockSpec((tm, tc), lambda i, j: (i, j)),
                  pl.BlockSpec((8, tc), lambda i, j: (jnp.minimum((i + 1) * hb, rows // 8 - 1), j)),
                  pl.BlockSpec((3, tc), lambda i, j: (0, j))],
        out_specs=pl.BlockSpec((tm, tc), lambda i, j: (i, j)), out_shape=jax.ShapeDtypeStruct((rows, f2), BF16),
        compiler_params=_params("parallel", "parallel"))(dacc, dacc, cw)


def _glu_fwd(name, gg, h):
    rows, d2 = gg.shape
    d = d2 // 2
    tm, tc = _tile(rows, 512, 16), _tile(d, 1024, LANES)
    nd = d // tc

    def body(a_ref, b_ref, h_ref, o_ref):
        o_ref[...] = h_ref[...] + a_ref[...].astype(F32) * jax.nn.sigmoid(b_ref[...].astype(F32))

    return pl.pallas_call(
        body, name=name, grid=(rows // tm, nd),
        in_specs=[pl.BlockSpec((tm, tc), lambda i, j: (i, j)), pl.BlockSpec((tm, tc), lambda i, j: (i, j + nd)),
                  pl.BlockSpec((tm, tc), lambda i, j: (i, j))],
        out_specs=pl.BlockSpec((tm, tc), lambda i, j: (i, j)), out_shape=jax.ShapeDtypeStruct((rows, d), F32),
        compiler_params=_params("parallel", "parallel"))(gg, gg, h)


def _glu_bwd(name, gg, dh):
    rows, d2 = gg.shape
    d = d2 // 2
    tm, tc = _tile(rows, 512, 16), _tile(d, 1024, LANES)
    nd = d // tc

    def body(s_ref, o_ref, dh_ref, out_ref):
        is_a = pl.program_id(1) < nd
        me = s_ref[...].astype(F32)
        other = o_ref[...].astype(F32)
        g = dh_ref[...]
        sig_o = jax.nn.sigmoid(other)
        sig_m = jax.nn.sigmoid(me)
        out_ref[...] = jnp.where(is_a, g * sig_o, g * other * sig_m * (1.0 - sig_m)).astype(out_ref.dtype)

    return pl.pallas_call(
        body, name=name, grid=(rows // tm, 2 * nd),
        in_specs=[pl.BlockSpec((tm, tc), lambda i, j: (i, j)),
                  pl.BlockSpec((tm, tc), lambda i, j: (i, (j + nd) % (2 * nd))),
                  pl.BlockSpec((tm, tc), lambda i, j: (i, j % nd))],
        out_specs=pl.BlockSpec((tm, tc), lambda i, j: (i, j)), out_shape=jax.ShapeDtypeStruct((rows, d2), BF16),
        compiler_params=_params("parallel", "parallel"))(gg, gg, dh)


def _sgu_common(pre_ref, gv_ref, e):
    u = _gelu(pre_ref[:, :e].astype(F32))
    v = _gelu(pre_ref[:, e:].astype(F32))
    r = lax.rsqrt(jnp.mean(v * v, axis=-1, keepdims=True) + EPS)
    vh = v * r
    return u, vh, r, (vh * gv_ref[...]).astype(BF16)


def _tril_bf16(ws_ref, hd):
    t = lax.broadcasted_iota(jnp.int32, (CHUNK, CHUNK), 0)
    s = lax.broadcasted_iota(jnp.int32, (CHUNK, CHUNK), 1)
    return jnp.where(s <= t, ws_ref[hd], 0.0).astype(BF16)


def _sgu_mix_fwd(name, pre, gv, ws, bsx):
    rows, e2 = pre.shape
    e = e2 // 2
    heads = e // SGU_GROUP
    tr = _tile(rows, 256, CHUNK)

    def body(pre_ref, gv_ref, ws_ref, bs_ref, o_ref):
        u, _, _, vn = _sgu_common(pre_ref, gv_ref, e)
        for hd in range(heads):
            wm = _tril_bf16(ws_ref, hd)
            cols = slice(hd * SGU_GROUP, (hd + 1) * SGU_GROUP)
            for ck in range(tr // CHUNK):
                rws = slice(ck * CHUNK, (ck + 1) * CHUNK)
                s = jnp.dot(wm, vn[rws, cols], preferred_element_type=F32) + bs_ref[hd]
                o_ref[rws, cols] = (u[rws, cols] * s).astype(o_ref.dtype)

    whole3 = pl.BlockSpec((heads, CHUNK, CHUNK), lambda i: (0, 0, 0))
    return pl.pallas_call(
        body, name=name, grid=(rows // tr,),
        in_specs=[pl.BlockSpec((tr, e2), lambda i: (i, 0)), pl.BlockSpec((1, e), lambda i: (0, 0)), whole3, whole3],
        out_specs=pl.BlockSpec((tr, e), lambda i: (i, 0)), out_shape=jax.ShapeDtypeStruct((rows, e), BF16),
        compiler_params=_params("parallel"))(pre, gv, ws, bsx)


def _sgu_mix_bwd(name, pre, dus, gv, ws, bsx):
    rows, e2 = pre.shape
    e = e2 // 2
    heads = e // SGU_GROUP
    tr = _tile(rows, 256, CHUNK)

    def body(pre_ref, dus_ref, gv_ref, ws_ref, bs_ref, dpre_ref, dws_ref, dbs_ref, dgv_ref, dvn_ref, du_ref):
        first = pl.program_id(0) == 0
        u, vh, r, vn = _sgu_common(pre_ref, gv_ref, e)
        ones = jnp.ones((SGU_GROUP, LANES), BF16)
        tt = lax.broadcasted_iota(jnp.int32, (CHUNK, CHUNK), 0)
        ss = lax.broadcasted_iota(jnp.int32, (CHUNK, CHUNK), 1)
        for hd in range(heads):
            wm = _tril_bf16(ws_ref, hd)
            cols = slice(hd * SGU_GROUP, (hd + 1) * SGU_GROUP)
            dw = jnp.zeros((CHUNK, CHUNK), F32)
            db = jnp.zeros((CHUNK, LANES), F32)
            for ck in range(tr // CHUNK):
                rws = slice(ck * CHUNK, (ck + 1) * CHUNK)
                vblk = vn[rws, cols]
                s = jnp.dot(wm, vblk, preferred_element_type=F32) + bs_ref[hd]
                d_us = dus_ref[rws, cols].astype(F32)
                du_ref[rws, cols] = d_us * s
                ds = (d_us * u[rws, cols]).astype(BF16)
                dvn_ref[rws, cols] = lax.dot_general(wm, ds, _DIMS['tn'], preferred_element_type=F32)
                dw = dw + lax.dot_general(ds, vblk, _DIMS['nt'], preferred_element_type=F32)
                db = db + jnp.dot(ds, ones, preferred_element_type=F32)
            dw = jnp.where(ss <= tt, dw, 0.0)

            @pl.when(first)
            def _():
                dws_ref[hd] = dw
                dbs_ref[hd] = db

            @pl.when(jnp.logical_not(first))
            def _():
                dws_ref[hd] += dw
                dbs_ref[hd] += db

        dvn = dvn_ref[...]
        part = jnp.sum(dvn * vh, axis=0, keepdims=True)

        @pl.when(first)
        def _():
            dgv_ref[...] = part

        @pl.when(jnp.logical_not(first))
        def _():
            dgv_ref[...] += part

        gy = dvn * gv_ref[...]
        dv = r * (gy - vh * jnp.mean(gy * vh, axis=-1, keepdims=True))
        dpre_ref[:, :e] = (du_ref[...] * _gelu_grad(pre_ref[:, :e].astype(F32))).astype(dpre_ref.dtype)
        dpre_ref[:, e:] = (dv * _gelu_grad(pre_ref[:, e:].astype(F32))).astype(dpre_ref.dtype)

    whole3 = pl.BlockSpec((heads, CHUNK, CHUNK), lambda i: (0, 0, 0))
    vec = pl.BlockSpec((1, e), lambda i: (0, 0))
    return pl.pallas_call(
        body, name=name, grid=(rows // tr,),
        in_specs=[pl.BlockSpec((tr, e2), lambda i: (i, 0)), pl.BlockSpec((tr, e), lambda i: (i, 0)), vec, whole3, whole3],
        out_specs=[pl.BlockSpec((tr, e2), lambda i: (i, 0)), whole3, whole3, vec],
        out_shape=[jax.ShapeDtypeStruct((rows, e2), BF16), jax.ShapeDtypeStruct((heads, CHUNK, CHUNK), F32),
                   jax.ShapeDtypeStruct((heads, CHUNK, LANES), F32), jax.ShapeDtypeStruct((1, e), F32)],
        scratch_shapes=[pltpu.VMEM((tr, e), F32), pltpu.VMEM((tr, e), F32)],
        compiler_params=_params("arbitrary"))(pre, dus, gv, ws, bsx)


def _disc_a(lr, li, ldt):
    dt = jnp.exp(ldt)
    mag = jnp.exp(dt * lr)
    return mag * jnp.cos(dt * li), mag * jnp.sin(dt * li)


def _disc_b(lr, li, ldt, br, bi):
    ar, ai = _disc_a(lr, li, ldt)
    den = lr * lr + li * li
    qr = ((ar - 1.0) * lr + ai * li) / den
    qi = (ai * lr - (ar - 1.0) * li) / den
    return qr * br - qi * bi, qr * bi + qi * br


def _s5_disc(name, lr, li, ldt, lrx, lix, ldtx, br, bi):
    def body(lr_ref, li_ref, ldt_ref, lrx_ref, lix_ref, ldtx_ref, br_ref, bi_ref, ar_ref, ai_ref, bbr_ref, bbi_ref):
        ar_ref[...], ai_ref[...] = _disc_a(lr_ref[...], li_ref[...], ldt_ref[...])
        bbr_ref[...], bbi_ref[...] = _disc_b(lrx_ref[...], lix_ref[...], ldtx_ref[...], br_ref[...], bi_ref[...])

    small = jax.ShapeDtypeStruct(lr.shape, F32)
    wide = jax.ShapeDtypeStruct(br.shape, F32)
    return pl.pallas_call(body, name=name, out_shape=[small, small, wide, wide],
                          compiler_params=pltpu.CompilerParams(vmem_limit_bytes=VMEM_LIMIT))(
        lr, li, ldt, lrx, lix, ldtx, br, bi)


def _s5_disc_bwd(name, lr, li, ldt, lrx, lix, ldtx, br, bi, dar, dai, dbbr, dbbi, sel):
    def body(lr_ref, li_ref, ldt_ref, lrx_ref, lix_ref, ldtx_ref, br_ref, bi_ref, dar_ref, dai_ref, dbbr_ref,
             dbbi_ref, sel_ref, dlr_ref, dli_ref, dldt_ref, dbr_ref, dbi_ref):
        _, vjp_a = jax.vjp(_disc_a, lr_ref[...], li_ref[...], ldt_ref[...])
        g_lr, g_li, g_ldt = vjp_a((dar_ref[...], dai_ref[...]))
        _, vjp_b = jax.vjp(_disc_b, lrx_ref[...], lix_ref[...], ldtx_ref[...], br_ref[...], bi_ref[...])
        x_lr, x_li, x_ldt, g_br, g_bi = vjp_b((dbbr_ref[...], dbbi_ref[...]))
        fold = lambda t: jnp.dot(t, sel_ref[...], precision=lax.Precision.HIGHEST, preferred_element_type=F32)
        dlr_ref[...] = g_lr + fold(x_lr)
        dli_ref[...] = g_li + fold(x_li)
        dldt_ref[...] = jnp.sum(g_ldt + fold(x_ldt), axis=1, keepdims=True)
        dbr_ref[...] = g_br
        dbi_ref[...] = g_bi

    small = jax.ShapeDtypeStruct(lr.shape, F32)
    wide = jax.ShapeDtypeStruct(br.shape, F32)
    return pl.pallas_call(body, name=name,
                          out_shape=[small, small, jax.ShapeDtypeStruct((lr.shape[0], 1), F32), wide, wide],
                          compiler_params=pltpu.CompilerParams(vmem_limit_bytes=VMEM_LIMIT))(
        lr, li, ldt, lrx, lix, ldtx, br, bi, dar, dai, dbbr, dbbi, sel)


def _cmul(ar, ai, br, bi):
    return ar * br - ai * bi, ar * bi + ai * br


def _pow_chunk(ar, ai):
    pr, pi = ar, ai
    for _ in range(int(math.log2(CHUNK))):
        pr, pi = _cmul(pr, pi, pr, pi)
    return pr, pi


def _scan_forward(hr_ref, hi_ref, er_ref, ei_ref, sr_ref, si_ref, ar, ai, nck):
    arb, aib = jnp.broadcast_to(ar, (nck, LANES)), jnp.broadcast_to(ai, (nck, LANES))

    def intra(t, carry):
        sr, si = carry
        slab = pl.ds(t, nck, stride=PITCH)
        nr = arb * sr - aib * si + hr_ref[slab, :]
        ni = arb * si + aib * sr + hi_ref[slab, :]
        hr_ref[slab, :] = nr
        hi_ref[slab, :] = ni
        return nr, ni

    zero = jnp.zeros((nck, LANES), F32)
    er_ref[...], ei_ref[...] = lax.fori_loop(0, CHUNK, intra, (zero, zero), unroll=8)
    pcr, pci = _pow_chunk(ar, ai)
    sr_ref[0:1, :] = jnp.zeros((1, LANES), F32)
    si_ref[0:1, :] = jnp.zeros((1, LANES), F32)
    for ck in range(nck - 1):
        pr, pi = sr_ref[ck:ck + 1, :], si_ref[ck:ck + 1, :]
        sr_ref[ck + 1:ck + 2, :] = pcr * pr - pci * pi + er_ref[ck:ck + 1, :]
        si_ref[ck + 1:ck + 2, :] = pcr * pi + pci * pr + ei_ref[ck:ck + 1, :]
    s_r, s_i = sr_ref[...], si_ref[...]

    def fix(t, carry):
        pr, pi = carry
        slab = pl.ds(t, nck, stride=PITCH)
        hr_ref[slab, :] = hr_ref[slab, :] + (pr * s_r - pi * s_i)
        hi_ref[slab, :] = hi_ref[slab, :] + (pr * s_i + pi * s_r)
        return _cmul(pr, pi, arb, aib)

    lax.fori_loop(0, CHUNK, fix, (arb, aib), unroll=8)


def _scan_backward(gr_ref, gi_ref, hr_ref, hi_ref, er_ref, ei_ref, sr_ref, si_ref, ar, ai, nck):
    arb, aib = jnp.broadcast_to(ar, (nck, LANES)), jnp.broadcast_to(-ai, (nck, LANES))

    def intra(k, carry):
        sr, si = carry
        slab = pl.ds(CHUNK - 1 - k, nck, stride=PITCH)
        nr = arb * sr - aib * si + gr_ref[slab, :]
        ni = arb * si + aib * sr + gi_ref[slab, :]
        gr_ref[slab, :] = nr
        gi_ref[slab, :] = ni
        return nr, ni

    zero = jnp.zeros((nck, LANES), F32)
    er_ref[...], ei_ref[...] = lax.fori_loop(0, CHUNK, intra, (zero, zero), unroll=8)
    pcr, pci = _pow_chunk(ar, -ai)
    sr_ref[nck - 1:nck, :] = jnp.zeros((1, LANES), F32)
    si_ref[nck - 1:nck, :] = jnp.zeros((1, LANES), F32)
    for ck in range(nck - 1, 0, -1):
        pr, pi = sr_ref[ck:ck + 1, :], si_ref[ck:ck + 1, :]
        sr_ref[ck - 1:ck, :] = pcr * pr - pci * pi + er_ref[ck:ck + 1, :]
        si_ref[ck - 1:ck, :] = pcr * pi + pci * pr + ei_ref[ck:ck + 1, :]
    s_r, s_i = sr_ref[...], si_ref[...]
    last = pl.ds(CHUNK - 1, nck, stride=PITCH)
    row = lax.broadcasted_iota(jnp.int32, (nck, LANES), 0)
    hp_r = jnp.where(row == 0, 0.0, pltpu.roll(hr_ref[last, :], 1, axis=0)) if nck > 1 else zero
    hp_i = jnp.where(row == 0, 0.0, pltpu.roll(hi_ref[last, :], 1, axis=0)) if nck > 1 else zero

    def fix(k, carry):
        pr, pi, acr, aci = carry
        t = CHUNK - 1 - k
        slab = pl.ds(t, nck, stride=PITCH)
        g_r = gr_ref[slab, :] + (pr * s_r - pi * s_i)
        g_i = gi_ref[slab, :] + (pr * s_i + pi * s_r)
        gr_ref[slab, :] = g_r
        gi_ref[slab, :] = g_i
        prev = pl.ds(jnp.maximum(t - 1, 0), nck, stride=PITCH)
        h_r = jnp.where(t == 0, hp_r, hr_ref[prev, :])
        h_i = jnp.where(t == 0, hp_i, hi_ref[prev, :])
        acr = acr + g_r * h_r + g_i * h_i
        aci = aci + g_i * h_r - g_r * h_i
        nr, ni = _cmul(pr, pi, arb, aib)
        return nr, ni, acr, aci

    _, _, acr, aci = lax.fori_loop(0, CHUNK, fix, (arb, aib, zero, zero), unroll=8)
    return jnp.sum(acr, axis=0, keepdims=True), jnp.sum(aci, axis=0, keepdims=True)


def _chunk_rows(ck):
    return (pl.ds(pl.multiple_of(ck * CHUNK, CHUNK), CHUNK), pl.ds(pl.multiple_of(ck * PITCH, 8), CHUNK))


def _s5_fill_states(u_ref, br_ref, bi_ref, hr_ref, hi_ref, nck):
    brm, bim = br_ref[...], bi_ref[...]

    def fill(ck, carry):
        src, dst = _chunk_rows(ck)
        ub = u_ref[src, :].astype(BF16)
        hr_ref[dst, :] = jnp.dot(ub, brm, preferred_element_type=F32)
        hi_ref[dst, :] = jnp.dot(ub, bim, preferred_element_type=F32)
        return carry

    lax.fori_loop(0, nck, fill, 0)


def _s5_specs(rows, e):
    sb = STATE_BLOCKS
    chan = pl.BlockSpec((rows, LANES), lambda j: (0, j // sb))
    bmat = pl.BlockSpec((None, LANES, LANES), lambda j: (j // sb, 0, j % sb))
    cmat = pl.BlockSpec((None, LANES, LANES), lambda j: (j // sb, j % sb, 0))
    avec = pl.BlockSpec((1, LANES), lambda j: (0, j))
    dvec = pl.BlockSpec((1, LANES), lambda j: (0, j // sb))
    return chan, bmat, cmat, avec, dvec


def _s5_fwd(name, u, bre, bim, crt, cit, ar, ai, dd):
    rows, e = u.shape
    nck = rows // CHUNK
    nsteps = (e // LANES) * STATE_BLOCKS
    chan, bmat, cmat, avec, dvec = _s5_specs(rows, e)

    def body(u_ref, br_ref, bi_ref, cr_ref, ci_ref, ar_ref, ai_ref, dd_ref, y_ref, gy_ref,
             hr_ref, hi_ref, er_ref, ei_ref, sr_ref, si_ref, acc_ref):
        j = pl.program_id(0) % STATE_BLOCKS
        _s5_fill_states(u_ref, br_ref, bi_ref, hr_ref, hi_ref, nck)
        _scan_forward(hr_ref, hi_ref, er_ref, ei_ref, sr_ref, si_ref, ar_ref[...], ai_ref[...], nck)
        crm, cim, ddv = cr_ref[...], ci_ref[...], dd_ref[...]

        def drain(ck, carry):
            src, dst = _chunk_rows(ck)
            contrib = (jnp.dot(hr_ref[dst, :].astype(BF16), crm, preferred_element_type=F32)
                       - jnp.dot(hi_ref[dst, :].astype(BF16), cim, preferred_element_type=F32))

            @pl.when(j == 0)
            def _():
                acc_ref[src, :] = ddv * u_ref[src, :] + contrib

            @pl.when(j > 0)
            def _():
                acc_ref[src, :] += contrib

            return carry

        lax.fori_loop(0, nck, drain, 0)

        @pl.when(j == STATE_BLOCKS - 1)
        def _():
            y = acc_ref[...]
            y_ref[...] = y.astype(y_ref.dtype)
            gy_ref[...] = _gelu(y).astype(gy_ref.dtype)

    flat = pltpu.VMEM((rows, LANES), F32)
    big = pltpu.VMEM((nck * PITCH, LANES), F32)
    small = pltpu.VMEM((nck, LANES), F32)
    out = jax.ShapeDtypeStruct((rows, e), BF16)
    return pl.pallas_call(
        body, name=name, grid=(nsteps,), in_specs=[chan, bmat, bmat, cmat, cmat, avec, avec, dvec],
        out_specs=[chan, chan], out_shape=[out, out], scratch_shapes=[big, big, small, small, small, small, flat],
        compiler_params=_params("arbitrary"))(u, bre, bim, crt, cit, ar, ai, dd)


def _s5_bwd(name, u, y, dgy, bre, bim, crt, cit, ar, ai, dd):
    rows, e = u.shape
    nb = e // LANES
    nck = rows // CHUNK
    nsteps = nb * STATE_BLOCKS
    chan, bmat, cmat, avec, dvec = _s5_specs(rows, e)

    def body(u_ref, y_ref, dgy_ref, br_ref, bi_ref, cr_ref, ci_ref, ar_ref, ai_ref, dd_ref,
             du_ref, dbr_ref, dbi_ref, dcr_ref, dci_ref, dar_ref, dai_ref, ddd_ref,
             hr_ref, hi_ref, gr_ref, gi_ref, er_ref, ei_ref, sr_ref, si_ref, acc_ref, dy_ref):
        j = pl.program_id(0) % STATE_BLOCKS
        _s5_fill_states(u_ref, br_ref, bi_ref, hr_ref, hi_ref, nck)
        _scan_forward(hr_ref, hi_ref, er_ref, ei_ref, sr_ref, si_ref, ar_ref[...], ai_ref[...], nck)
        brm, bim, crm, cim, ddv = br_ref[...], bi_ref[...], cr_ref[...], ci_ref[...], dd_ref[...]
        zero = jnp.zeros((LANES, LANES), F32)

        def seed(ck, carry):
            dcr, dci, ddd = carry
            src, dst = _chunk_rows(ck)
            dy = dgy_ref[src, :].astype(F32) * _gelu_grad(y_ref[src, :].astype(F32))
            dy_ref[src, :] = dy
            dyb = dy.astype(BF16)
            gr_ref[dst, :] = lax.dot_general(dyb, crm, _DIMS['nt'], preferred_element_type=F32)
            gi_ref[dst, :] = -lax.dot_general(dyb, cim, _DIMS['nt'], preferred_element_type=F32)
            dcr = dcr + lax.dot_general(hr_ref[dst, :].astype(BF16), dyb, _DIMS['tn'], preferred_element_type=F32)
            dci = dci - lax.dot_general(hi_ref[dst, :].astype(BF16), dyb, _DIMS['tn'], preferred_element_type=F32)
            ddd = ddd + jnp.sum(dy * u_ref[src, :], axis=0, keepdims=True)
            return dcr, dci, ddd

        dcr_ref[...], dci_ref[...], ddd = lax.fori_loop(0, nck, seed, (zero, zero, jnp.zeros((1, LANES), F32)))

        @pl.when(j == 0)
        def _():
            ddd_ref[...] = ddd

        dar_ref[...], dai_ref[...] = _scan_backward(gr_ref, gi_ref, hr_ref, hi_ref, er_ref, ei_ref, sr_ref, si_ref,
                                                    ar_ref[...], ai_ref[...], nck)

        def drain(ck, carry):
            dbr, dbi = carry
            src, dst = _chunk_rows(ck)
            ub = u_ref[src, :].astype(BF16)
            grb, gib = gr_ref[dst, :].astype(BF16), gi_ref[dst, :].astype(BF16)
            dbr = dbr + lax.dot_general(ub, grb, _DIMS['tn'], preferred_element_type=F32)
            dbi = dbi + lax.dot_general(ub, gib, _DIMS['tn'], preferred_element_type=F32)
            contrib = (lax.dot_general(grb, brm, _DIMS['nt'], preferred_element_type=F32)
                       + lax.dot_general(gib, bim, _DIMS['nt'], preferred_element_type=F32))

            @pl.when(j == 0)
            def _():
                acc_ref[src, :] = ddv * dy_ref[src, :] + contrib

            @pl.when(j > 0)
            def _():
                acc_ref[src, :] += contrib

            return dbr, dbi

        dbr_ref[...], dbi_ref[...] = lax.fori_loop(0, nck, drain, (zero, zero))

        @pl.when(j == STATE_BLOCKS - 1)
        def _():
            du_ref[...] = acc_ref[...]

    flat = pltpu.VMEM((rows, LANES), F32)
    big = pltpu.VMEM((nck * PITCH, LANES), F32)
    small = pltpu.VMEM((nck, LANES), F32)
    bshape = jax.ShapeDtypeStruct((nb, LANES, LANES * STATE_BLOCKS), F32)
    cshape = jax.ShapeDtypeStruct((nb, LANES * STATE_BLOCKS, LANES), F32)
    ashape = jax.ShapeDtypeStruct((1, nb * LANES * STATE_BLOCKS), F32)
    return pl.pallas_call(
        body, name=name, grid=(nsteps,),
        in_specs=[chan, chan, chan, bmat, bmat, cmat, cmat, avec, avec, dvec],
        out_specs=[chan, bmat, bmat, cmat, cmat, avec, avec, dvec],
        out_shape=[jax.ShapeDtypeStruct((rows, e), F32), bshape, bshape, cshape, cshape, ashape, ashape,
                   jax.ShapeDtypeStruct((1, e), F32)],
        scratch_shapes=[big, big, big, big, small, small, small, small, flat, flat],
        compiler_params=_params("arbitrary"))(u, y, dgy, bre, bim, crt, cit, ar, ai, dd)


def _to_blockdiag_b(bbar, nb):
    eye = jnp.eye(GROUPS_PER_BLOCK, dtype=bbar.dtype)
    t = jnp.einsum('bgpc,gh->bgchp', bbar.reshape(nb, GROUPS_PER_BLOCK, SSM_STATE, SSM_GROUP), eye)
    return t.reshape(nb, LANES, GROUPS_PER_BLOCK * SSM_STATE)


def _from_blockdiag_b(dmat, nb):
    eye = jnp.eye(GROUPS_PER_BLOCK, dtype=dmat.dtype)
    t = dmat.reshape(nb, GROUPS_PER_BLOCK, SSM_GROUP, GROUPS_PER_BLOCK, SSM_STATE)
    return jnp.einsum('bgchp,gh->bgpc', t, eye).reshape(nb * GROUPS_PER_BLOCK, SSM_STATE, SSM_GROUP)


def _to_blockdiag_ct(c, nb):
    eye = jnp.eye(GROUPS_PER_BLOCK, dtype=c.dtype)
    t = jnp.einsum('bgop,gh->bgpho', c.reshape(nb, GROUPS_PER_BLOCK, SSM_GROUP, SSM_STATE), eye)
    return t.reshape(nb, GROUPS_PER_BLOCK * SSM_STATE, LANES)


def _from_blockdiag_ct(dmat, nb):
    eye = jnp.eye(GROUPS_PER_BLOCK, dtype=dmat.dtype)
    t = dmat.reshape(nb, GROUPS_PER_BLOCK, SSM_STATE, GROUPS_PER_BLOCK, SSM_GROUP)
    return jnp.einsum('bgpho,gh->bgop', t, eye).reshape(nb * GROUPS_PER_BLOCK, SSM_GROUP, SSM_STATE)


def _half_specs(kind, rdim, cdim, tr, tc):
    nr, nc = rdim // tr, cdim // tc
    if kind == 'col':
        nat = pl.BlockSpec((None, tr, tc), lambda c, l, rb, cb: (l, c * nr + rb, cb))
    else:
        nat = pl.BlockSpec((None, tr, tc), lambda c, l, rb, cb: (l, rb, c * nc + cb))
    half = pl.BlockSpec((None, None, tr, tc), lambda c, l, rb, cb: (c, l, rb, cb))
    return nat, half


def _cast_halves(name, w, kind):
    layers = w.shape[0]
    rdim, cdim = _half_shape(kind, w.shape)
    tr, tc = _tile(rdim, 512, 16), _tile(cdim, 1408, LANES)
    nat, half = _half_specs(kind, rdim, cdim, tr, tc)

    def body(w_ref, o_ref):
        o_ref[...] = w_ref[...].astype(o_ref.dtype)

    return pl.pallas_call(
        body, name=name, grid=(2, layers, rdim // tr, cdim // tc), in_specs=[nat], out_specs=half,
        out_shape=jax.ShapeDtypeStruct((2, layers, rdim, cdim), BF16),
        compiler_params=_params("parallel", "parallel", "parallel", "parallel"))(w)


def _adam_math(w, g, m, v):
    m = ADAM_B1 * m + (1.0 - ADAM_B1) * g
    v = ADAM_B2 * v + (1.0 - ADAM_B2) * (g * g)
    m_hat = m / (1.0 - ADAM_B1 ** ADAM_STEP)
    v_hat = v / (1.0 - ADAM_B2 ** ADAM_STEP)
    delta = -ADAM_LR * (m_hat / (jnp.sqrt(v_hat) + ADAM_EPS) + ADAM_WD * w)
    return delta, m, v


def _adam_big(name, w, m, v, gfull, kind):
    layers = w.shape[0]
    rdim, cdim = _half_shape(kind, w.shape)
    tr, tc = _tile(rdim, 256, 8), _tile(cdim, 1408, LANES)
    nat, half = _half_specs(kind, rdim, cdim, tr, tc)

    def body(w_ref, m_ref, v_ref, g_ref, go_ref, d_ref, mo_ref, vo_ref):
        g = g_ref[...]
        go_ref[...] = g
        d_ref[...], mo_ref[...], vo_ref[...] = _adam_math(w_ref[...], g, m_ref[...], v_ref[...])

    shape = jax.ShapeDtypeStruct(w.shape, F32)
    return pl.pallas_call(
        body, name=name, grid=(2, layers, rdim // tr, cdim // tc), in_specs=[nat, nat, nat, half],
        out_specs=[nat, nat, nat, nat], out_shape=[shape, shape, shape, shape],
        compiler_params=_params("parallel", "parallel", "parallel", "parallel"))(w, m, v, gfull)


def _adam_small(w, m, v, g):
    rows = w.shape[0]
    tr = _tile(rows, 512, 8)
    spec = pl.BlockSpec((tr, LANES), lambda i: (i, 0))

    def body(w_ref, m_ref, v_ref, g_ref, d_ref, mo_ref, vo_ref):
        d_ref[...], mo_ref[...], vo_ref[...] = _adam_math(w_ref[...], g_ref[...], m_ref[...], v_ref[...])

    shape = jax.ShapeDtypeStruct(w.shape, F32)
    return pl.pallas_call(body, name="adam_small", grid=(rows // tr,), in_specs=[spec] * 4, out_specs=[spec] * 3,
                          out_shape=[shape] * 3, compiler_params=_params("parallel"))(w, m, v, g)


def _add2(name, a, b):
    cdim = a.shape[-1]
    a2, b2 = a.reshape(-1, cdim), b.reshape(-1, cdim)
    rows = a2.shape[0]
    tr, tc = _tile(rows, 512, 16), _tile(cdim, 1408, LANES)
    spec = pl.BlockSpec((tr, tc), lambda i, j: (i, j))

    def body(a_ref, b_ref, o_ref):
        o_ref[...] = (a_ref[...].astype(F32) + b_ref[...].astype(F32)).astype(o_ref.dtype)

    out = pl.pallas_call(body, name=name, grid=(rows // tr, cdim // tc), in_specs=[spec, spec], out_specs=spec,
                         out_shape=jax.ShapeDtypeStruct(a2.shape, BF16),
                         compiler_params=_params("parallel", "parallel"))(a2, b2)
    return out.reshape(a.shape)


def _add4(name, own, recv):
    cdim = own.shape[-1]
    o2 = own.reshape(-1, cdim)
    r3 = recv.reshape(3, -1, cdim)
    rows = o2.shape[0]
    tr, tc = _tile(rows, 512, 16), _tile(cdim, 1408, LANES)
    spec = pl.BlockSpec((tr, tc), lambda i, j: (i, j))
    rspec = lambda k: pl.BlockSpec((None, tr, tc), lambda i, j: (k, i, j))

    def body(o_ref, x_ref, y_ref, d_ref, out_ref):
        out_ref[...] = ((o_ref[...].astype(F32) + d_ref[...].astype(F32))
                        + (x_ref[...].astype(F32) + y_ref[...].astype(F32)))

    out = pl.pallas_call(body, name=name, grid=(rows // tr, cdim // tc), in_specs=[spec, rspec(0), rspec(1), rspec(2)],
                         out_specs=spec, out_shape=jax.ShapeDtypeStruct(o2.shape, F32),
                         compiler_params=_params("parallel", "parallel"))(o2, r3, r3, r3)
    return out.reshape(own.shape)


def _place():
    x, y, c = lax.axis_index("x"), lax.axis_index("y"), lax.axis_index("c")
    chips = [(1 - x, y), (x, 1 - y), (1 - x, 1 - y)]
    return x, y, c, chips


ANY = pl.BlockSpec(memory_space=pl.ANY)


def _remote(src, dst, send, recv, to):
    return pltpu.make_async_remote_copy(src_ref=src, dst_ref=dst, send_sem=send, recv_sem=recv, device_id=to,
                                        device_id_type=MESH)


def _pieces(src, dst, bands):
    lead, rows = src.shape[:-2], src.shape[-2]
    band = rows // bands
    out = []
    for idx in itertools.product(*[range(dim) for dim in lead]):
        for q in range(bands):
            sl = (*idx, pl.ds(q * band, band))
            out.append((src.at[sl], dst.at[sl]))
    return out


def _allgather_big(halves):
    n = len(halves)

    def body(*refs):
        ins, outs = refs[:n], refs[n:2 * n]
        send, recv, loc = refs[2 * n:]
        x, y, c, chips = _place()
        kme = 2 * x + y
        sib = (x, y, 1 - c)
        for a in range(n):
            for s, d in _pieces(ins[a], outs[a].at[kme], 1):
                pltpu.make_async_copy(s, d, loc.at[a]).start()
        for a in range(n):
            for r, chip in enumerate(chips):
                for s, d in _pieces(ins[a].at[c], outs[a].at[kme, c], 2):
                    _remote(s, d, send.at[6 * a + r], recv.at[6 * a + r], (*chip, c)).start()
        for a in range(n):
            for r, chip in enumerate(chips):
                kp = 2 * chip[0] + chip[1]
                _remote(ins[a].at[c], outs[a].at[kp, c], send.at[6 * a + r], recv.at[6 * a + r], (*chip, c)).wait_recv()
                for s, d in _pieces(outs[a].at[kp, c], outs[a].at[kp, c], 2):
                    _remote(s, d, send.at[6 * a + 3 + r], recv.at[6 * a + 3 + r], sib).start()
        for a in range(n):
            for r, chip in enumerate(chips):
                kp = 2 * chip[0] + chip[1]
                _remote(outs[a].at[kp, 1 - c], outs[a].at[kp, 1 - c], send.at[6 * a + 3 + r], recv.at[6 * a + 3 + r],
                        sib).wait_recv()
        for a in range(n):
            for r, chip in enumerate(chips):
                kp = 2 * chip[0] + chip[1]
                _remote(ins[a].at[c], outs[a].at[kme, c], send.at[6 * a + r], recv.at[6 * a + r], (*chip, c)).wait_send()
                _remote(outs[a].at[kp, c], outs[a].at[kp, c], send.at[6 * a + 3 + r], recv.at[6 * a + 3 + r],
                        sib).wait_send()
            pltpu.make_async_copy(ins[a], outs[a].at[kme], loc.at[a]).wait()

    return pl.pallas_call(
        body, name="allgather_weights", in_specs=[ANY] * n, out_specs=[ANY] * n,
        out_shape=[jax.ShapeDtypeStruct((4, *h.shape), h.dtype) for h in halves],
        scratch_shapes=[pltpu.SemaphoreType.DMA((6 * n,)), pltpu.SemaphoreType.DMA((6 * n,)),
                        pltpu.SemaphoreType.DMA((n,))])(*halves)


def _allgather_small(shards):
    n = len(shards)

    def body(*refs):
        ins, outs = refs[:n], refs[n:2 * n]
        send, recv, loc = refs[2 * n:]
        x, y, c, chips = _place()
        kme = 2 * x + y
        local = [pltpu.make_async_copy(ins[a], outs[a].at[kme], loc.at[a]) for a in range(n)]
        for cp in local:
            cp.start()
        cps = [_remote(ins[a], outs[a].at[kme], send.at[3 * a + r], recv.at[3 * a + r], (*chip, c))
               for a in range(n) for r, chip in enumerate(chips)]
        for cp in cps:
            cp.start()
        for a in range(n):
            for r, chip in enumerate(chips):
                kp = 2 * chip[0] + chip[1]
                _remote(ins[a], outs[a].at[kp], send.at[3 * a + r], recv.at[3 * a + r], (*chip, c)).wait_recv()
        for cp in cps:
            cp.wait_send()
        for cp in local:
            cp.wait()

    return pl.pallas_call(
        body, name="allgather_small", in_specs=[ANY] * n, out_specs=[ANY] * n,
        out_shape=[jax.ShapeDtypeStruct((4, *s.shape), s.dtype) for s in shards],
        scratch_shapes=[pltpu.SemaphoreType.DMA((3 * n,)), pltpu.SemaphoreType.DMA((3 * n,)),
                        pltpu.SemaphoreType.DMA((n,))])(*shards)


def _swap_halves(parts):
    n = len(parts)

    def body(*refs):
        ins, own, got = refs[:n], refs[n:2 * n], refs[2 * n:3 * n]
        send, recv, loc = refs[3 * n:]
        x, y, c, _ = _place()
        sib = (x, y, 1 - c)
        for a in range(n):
            for s, d in _pieces(ins[a].at[:, 1 - c], got[a], 1):
                _remote(s, d, send.at[a], recv.at[a], sib).start()
            for s, d in _pieces(ins[a].at[:, c], own[a], 1):
                pltpu.make_async_copy(s, d, loc.at[a]).start()
        for a in range(n):
            _remote(ins[a].at[:, 1 - c], got[a], send.at[a], recv.at[a], sib).wait()
            pltpu.make_async_copy(ins[a].at[:, c], own[a], loc.at[a]).wait()

    shapes = [jax.ShapeDtypeStruct((4, *p.shape[2:]), p.dtype) for p in parts]
    res = pl.pallas_call(
        body, name="grad_swap_halves", in_specs=[ANY] * n, out_specs=[ANY] * (2 * n), out_shape=shapes + shapes,
        scratch_shapes=[pltpu.SemaphoreType.DMA((n,)), pltpu.SemaphoreType.DMA((n,)),
                        pltpu.SemaphoreType.DMA((n,))])(*parts)
    return res[:n], res[n:]


def _scatter_chips(sums):
    n = len(sums)

    def body(*refs):
        ins, own, got = refs[:n], refs[n:2 * n], refs[2 * n:3 * n]
        send, recv, loc = refs[3 * n:]
        x, y, c, chips = _place()
        kme = 2 * x + y
        for a in range(n):
            for r, chip in enumerate(chips):
                for s, d in _pieces(ins[a].at[2 * chip[0] + chip[1]], got[a].at[r], 2):
                    _remote(s, d, send.at[3 * a + r], recv.at[3 * a + r], (*chip, c)).start()
            for s, d in _pieces(ins[a].at[kme], own[a], 1):
                pltpu.make_async_copy(s, d, loc.at[a]).start()
        for a in range(n):
            for r, chip in enumerate(chips):
                _remote(ins[a].at[2 * chip[0] + chip[1]], got[a].at[r], send.at[3 * a + r], recv.at[3 * a + r],
                        (*chip, c)).wait()
            pltpu.make_async_copy(ins[a].at[kme], own[a], loc.at[a]).wait()

    own_shapes = [jax.ShapeDtypeStruct(s.shape[1:], s.dtype) for s in sums]
    got_shapes = [jax.ShapeDtypeStruct((3, *s.shape[1:]), s.dtype) for s in sums]
    res = pl.pallas_call(
        body, name="grad_scatter_chips", in_specs=[ANY] * n, out_specs=[ANY] * (2 * n),
        out_shape=own_shapes + got_shapes,
        scratch_shapes=[pltpu.SemaphoreType.DMA((3 * n,)), pltpu.SemaphoreType.DMA((3 * n,)),
                        pltpu.SemaphoreType.DMA((n,))])(*sums)
    return res[:n], res[n:]


def _join_halves(halves):
    n = len(halves)

    def body(*refs):
        ins, outs = refs[:n], refs[n:2 * n]
        send, recv, loc = refs[2 * n:]
        x, y, c, _ = _place()
        sib = (x, y, 1 - c)
        for a in range(n):
            for s, d in _pieces(ins[a], outs[a].at[c], 4):
                _remote(s, d, send.at[a], recv.at[a], sib).start()
            for s, d in _pieces(ins[a], outs[a].at[c], 1):
                pltpu.make_async_copy(s, d, loc.at[a]).start()
        for a in range(n):
            _remote(ins[a], outs[a].at[1 - c], send.at[a], recv.at[a], sib).wait_recv()
            _remote(ins[a], outs[a].at[c], send.at[a], recv.at[a], sib).wait_send()
            pltpu.make_async_copy(ins[a], outs[a].at[c], loc.at[a]).wait()

    return pl.pallas_call(
        body, name="grad_join_halves", in_specs=[ANY] * n, out_specs=[ANY] * n,
        out_shape=[jax.ShapeDtypeStruct((2, *h.shape), h.dtype) for h in halves],
        scratch_shapes=[pltpu.SemaphoreType.DMA((n,)), pltpu.SemaphoreType.DMA((n,)),
                        pltpu.SemaphoreType.DMA((n,))])(*halves)


def _allreduce_small(packed):
    rows = packed.shape[0]
    half = rows // 2

    def body(in_ref, out_ref, q_ref, s_ref, t_ref, send, recv):
        x, y, c, chips = _place()
        sib = (x, y, 1 - c)
        mine = pl.ds(pl.multiple_of(c * half, 8), half)
        theirs = pl.ds(pl.multiple_of((1 - c) * half, 8), half)
        first = _remote(in_ref.at[theirs], q_ref, send.at[0], recv.at[0], sib)
        first.start()
        first.wait()
        s_ref[...] = in_ref[mine, :] + q_ref[...]
        cps = [_remote(s_ref, t_ref.at[r], send.at[1 + r], recv.at[1 + r], (*chip, c)) for r, chip in enumerate(chips)]
        for cp in cps:
            cp.start()
        for cp in cps:
            cp.wait()
        out_ref[mine, :] = (s_ref[...] + t_ref[2]) + (t_ref[0] + t_ref[1])
        last = _remote(out_ref.at[mine], out_ref.at[mine], send.at[4], recv.at[4], sib)
        last.start()
        _remote(out_ref.at[theirs], out_ref.at[theirs], send.at[4], recv.at[4], sib).wait_recv()
        last.wait_send()

    vm = pl.BlockSpec(memory_space=pltpu.VMEM)
    return pl.pallas_call(
        body, name="allreduce_small", in_specs=[vm], out_specs=vm, out_shape=jax.ShapeDtypeStruct(packed.shape, F32),
        scratch_shapes=[pltpu.VMEM((half, LANES), F32), pltpu.VMEM((half, LANES), F32),
                        pltpu.VMEM((3, half, LANES), F32), pltpu.SemaphoreType.DMA((5,)), pltpu.SemaphoreType.DMA((5,))],
        compiler_params=pltpu.CompilerParams(vmem_limit_bytes=VMEM_LIMIT))(packed)


PACK_ROWS = 16


def _pack(arrs):
    parts, total = [], 0
    for a in arrs:
        flat = a.reshape(-1)
        rows = -(-flat.shape[0] // (LANES * PACK_ROWS)) * PACK_ROWS
        parts.append(jnp.pad(flat, (0, rows * LANES - flat.shape[0])).reshape(rows, LANES))
        total += rows
    return jnp.concatenate(parts, axis=0)


def _unpack(packed, shapes):
    out, row = [], 0
    for shp in shapes:
        size = math.prod(shp)
        rows = -(-size // (LANES * PACK_ROWS)) * PACK_ROWS
        out.append(packed[row:row + rows].reshape(-1)[:size].reshape(shp))
        row += rows
    return out


def kernel(x, norm_mix_g, norm_ffn_g, a_w_in, a_g_v, a_w_s, a_b_s, a_w_out, b_w_in, b_a_re, b_a_im, b_log_dt, b_b_re, b_b_im, b_c_re, b_c_im, b_d, b_w_glu, f_w_up, f_conv_w, f_conv_b, f_w_down, final_g, loss_target, m_norm_mix_g, m_norm_ffn_g, m_a_w_in, m_a_g_v, m_a_w_s, m_a_b_s, m_a_w_out, m_b_w_in, m_b_a_re, m_b_a_im, m_b_log_dt, m_b_b_re, m_b_b_im, m_b_c_re, m_b_c_im, m_b_d, m_b_w_glu, m_f_w_up, m_f_conv_w, m_f_conv_b, m_f_w_down, m_final_g, v_norm_mix_g, v_norm_ffn_g, v_a_w_in, v_a_g_v, v_a_w_s, v_a_b_s, v_a_w_out, v_b_w_in, v_b_a_re, v_b_a_im, v_b_log_dt, v_b_b_re, v_b_b_im, v_b_c_re, v_b_c_im, v_b_d, v_b_w_glu, v_f_w_up, v_f_conv_w, v_f_conv_b, v_f_w_down, v_final_g):
    w = dict(norm_mix_g=norm_mix_g, norm_ffn_g=norm_ffn_g, a_w_in=a_w_in, a_g_v=a_g_v, a_w_s=a_w_s, a_b_s=a_b_s,
             a_w_out=a_w_out, b_w_in=b_w_in, b_a_re=b_a_re, b_a_im=b_a_im, b_log_dt=b_log_dt, b_b_re=b_b_re,
             b_b_im=b_b_im, b_c_re=b_c_re, b_c_im=b_c_im, b_d=b_d, b_w_glu=b_w_glu, f_w_up=f_w_up, f_conv_w=f_conv_w,
             f_conv_b=f_conv_b, f_w_down=f_w_down, final_g=final_g)
    mom = dict(norm_mix_g=m_norm_mix_g, norm_ffn_g=m_norm_ffn_g, a_w_in=m_a_w_in, a_g_v=m_a_g_v, a_w_s=m_a_w_s,
               a_b_s=m_a_b_s, a_w_out=m_a_w_out, b_w_in=m_b_w_in, b_a_re=m_b_a_re, b_a_im=m_b_a_im,
               b_log_dt=m_b_log_dt, b_b_re=m_b_b_re, b_b_im=m_b_b_im, b_c_re=m_b_c_re, b_c_im=m_b_c_im, b_d=m_b_d,
               b_w_glu=m_b_w_glu, f_w_up=m_f_w_up, f_conv_w=m_f_conv_w, f_conv_b=m_f_conv_b, f_w_down=m_f_w_down,
               final_g=m_final_g)
    var = dict(norm_mix_g=v_norm_mix_g, norm_ffn_g=v_norm_ffn_g, a_w_in=v_a_w_in, a_g_v=v_a_g_v, a_w_s=v_a_w_s,
               a_b_s=v_a_b_s, a_w_out=v_a_w_out, b_w_in=v_b_w_in, b_a_re=v_b_a_re, b_a_im=v_b_a_im,
               b_log_dt=v_b_log_dt, b_b_re=v_b_b_re, b_b_im=v_b_b_im, b_c_re=v_b_c_re, b_c_im=v_b_c_im, b_d=v_b_d,
               b_w_glu=v_b_w_glu, f_w_up=v_f_w_up, f_conv_w=v_f_conv_w, f_conv_b=v_f_conv_b, f_w_down=v_f_w_down,
               final_g=v_final_g)

    rows, d = x.shape[1], x.shape[2]
    depth = norm_mix_g.shape[0]
    kchip = 2 * lax.axis_index("x") + lax.axis_index("y")
    big_names = list(BIG)
    dims = {n: _full_dims(BIG[n], w[n].shape) for n in big_names}

    halves = [_cast_halves("cast_" + n, w[n], BIG[n]) for n in big_names]
    gathered = dict(zip(big_names, _allgather_big(halves)))
    bd_all, cw_all = _allgather_small([b_d, f_conv_w.reshape(-1, f_conv_w.shape[-1])])
    bd_full = jnp.swapaxes(bd_all, 0, 1).reshape(b_d.shape[0], -1)
    cw_full = jnp.transpose(cw_all.reshape(4, *f_conv_w.shape), (1, 2, 0, 3)).reshape(depth, f_conv_w.shape[1], -1)

    pgrad = {n: lax.empty((4, 2, *h.shape[1:]), BF16) for n, h in zip(big_names, halves)}
    sgrad = {}

    def mm(name, a, wn, layer, out_dtype, residual=None):
        return _mm_x_w(name, a, gathered[wn], BIG[wn], *dims[wn], layer, out_dtype, residual)

    def mm_t(name, dy, wn, layer, out_dtype):
        return _mm_dy_wt(name, dy, gathered[wn], BIG[wn], *dims[wn], layer, out_dtype)

    def mm_g(name, xa, dy, wn, layer):
        pgrad[wn] = _mm_xt_dy(name, xa, dy, pgrad[wn], BIG[wn], *dims[wn], layer)

    e = d
    nb = e // LANES
    heads = e // SGU_GROUP
    h = x[0]
    saved = []
    for i in range(depth):
        j = i // 2
        gm = norm_mix_g[i:i + 1]
        hn = _rms_fwd("rms_mix_fwd", h, gm)
        if i % 2 == 0:
            pre = mm("sgu_in", hn, 'a_w_in', j, BF16)
            bsx = jnp.broadcast_to(a_b_s[j][:, :, None], (heads, CHUNK, LANES))
            us = _sgu_mix_fwd("sgu_mix_fwd", pre, a_g_v[j:j + 1], a_w_s[j], bsx)
            h_mid = mm("sgu_out", us, 'a_w_out', j, F32, residual=h)
            mix = dict(h=h, hn=hn, pre=pre, us=us, bsx=bsx)
        else:
            groups = b_a_re.shape[1]
            rep = lambda t: jnp.repeat(t, SSM_GROUP, axis=1)
            lr, li = b_a_re[j], b_a_im[j]
            ldt = jnp.broadcast_to(b_log_dt[j][:, None], lr.shape)
            bflat = lambda t: t.reshape(groups, SSM_STATE * SSM_GROUP)
            disc_in = (lr, li, ldt, rep(lr), rep(li), rep(ldt), bflat(b_b_re[j]), bflat(b_b_im[j]))
            abr, abi, bbr, bbi = _s5_disc("s5_disc", *disc_in)
            shape_b = (groups, SSM_STATE, SSM_GROUP)
            bre = _to_blockdiag_b(bbr.reshape(shape_b), nb).astype(BF16)
            bim = _to_blockdiag_b(bbi.reshape(shape_b), nb).astype(BF16)
            crt = _to_blockdiag_ct(b_c_re[j], nb).astype(BF16)
            cit = _to_blockdiag_ct(b_c_im[j], nb).astype(BF16)
            ar_row, ai_row = abr.reshape(1, -1), abi.reshape(1, -1)
            dd = bd_full[j:j + 1]
            u = mm("s5_in", hn, 'b_w_in', j, F32)
            yv, gy = _s5_fwd("s5_fwd", u, bre, bim, crt, cit, ar_row, ai_row, dd)
            gg = mm("s5_glu", gy, 'b_w_glu', j, BF16)
            h_mid = _glu_fwd("glu_fwd", gg, h)
            mix = dict(h=h, hn=hn, u=u, y=yv, gy=gy, gg=gg, disc_in=disc_in, mats=(bre, bim, crt, cit, ar_row, ai_row, dd))
        gf = norm_ffn_g[i:i + 1]
        hn2 = _rms_fwd("rms_ffn_fwd", h_mid, gf)
        z = mm("ffn_up", hn2, 'f_w_up', i, BF16)
        cw, cb = cw_full[i], f_conv_b[i:i + 1]
        act = _ffn_act_fwd("ffn_act_fwd", z, cw, cb)
        h_out = mm("ffn_down", act, 'f_w_down', i, F32, residual=h_mid)
        saved.append((mix, dict(h=h_mid, hn=hn2, z=z, act=act, cw=cw, cb=cb)))
        h = h_out

    dh, g_final, loss_vec = _loss_head(h, final_g.reshape(1, d), loss_target[0])
    loss = lax.psum(jnp.sum(loss_vec), ("x", "y", "c"))
    sgrad['final_g'] = g_final.reshape(d)

    g_mix, g_ffn = [None] * depth, [None] * depth
    g_cw, g_cb = [None] * depth, [None] * depth
    sg = {k: [None] * (depth // 2) for k in ('a_g_v', 'a_w_s', 'a_b_s')}
    bg = {k: [None] * (depth // 2) for k in ('b_a_re', 'b_a_im', 'b_log_dt', 'b_b_re', 'b_b_im', 'b_c_re', 'b_c_im', 'b_d')}
    for i in reversed(range(depth)):
        j = i // 2
        mix, ffn = saved[i]
        d_act = mm_t("ffn_down_dx", dh, 'f_w_down', i, BF16)
        mm_g("ffn_down_dw", ffn['act'], dh, 'f_w_down', i)
        dacc, g_cw[i], g_cb[i] = _ffn_act_bwd("ffn_act_bwd", ffn['z'], d_act, ffn['cw'], ffn['cb'])
        dz = _conv_bwd("ffn_conv_bwd", dacc, ffn['cw'])
        mm_g("ffn_up_dw", ffn['hn'], dz, 'f_w_up', i)
        dhn = mm_t("ffn_up_dx", dz, 'f_w_up', i, F32)
        dh, g_ffn[i] = _rms_bwd("rms_ffn_bwd", ffn['h'], norm_ffn_g[i:i + 1], dhn, dh)
        if i % 2 == 0:
            dus = mm_t("sgu_out_dx", dh, 'a_w_out', j, BF16)
            mm_g("sgu_out_dw", mix['us'], dh, 'a_w_out', j)
            dpre, dws, dbs, dgv = _sgu_mix_bwd("sgu_mix_bwd", mix['pre'], dus, a_g_v[j:j + 1], a_w_s[j], mix['bsx'])
            sg['a_w_s'][j], sg['a_b_s'][j], sg['a_g_v'][j] = dws, dbs[:, :, 0], dgv[0]
            mm_g("sgu_in_dw", mix['hn'], dpre, 'a_w_in', j)
            dhn = mm_t("sgu_in_dx", dpre, 'a_w_in', j, F32)
        else:
            dgg = _glu_bwd("glu_bwd", mix['gg'], dh)
            mm_g("s5_glu_dw", mix['gy'], dgg, 'b_w_glu', j)
            dgy = mm_t("s5_glu_dx", dgg, 'b_w_glu', j, BF16)
            du, dbr, dbi, dcr, dci, dar, dai, ddd = _s5_bwd("s5_bwd", mix['u'], mix['y'], dgy, *mix['mats'])
            groups = b_a_re.shape[1]
            flat = lambda t: _from_blockdiag_b(t, nb).reshape(groups, SSM_STATE * SSM_GROUP)
            sel = jnp.repeat(jnp.eye(SSM_STATE, dtype=F32), SSM_GROUP, axis=0)
            dlr, dli, dldt, dbre, dbim = _s5_disc_bwd(
                "s5_disc_bwd", *mix['disc_in'], dar.reshape(groups, SSM_STATE), dai.reshape(groups, SSM_STATE),
                flat(dbr), flat(dbi), sel)
            bg['b_a_re'][j], bg['b_a_im'][j], bg['b_log_dt'][j] = dlr, dli, dldt[:, 0]
            bg['b_b_re'][j] = dbre.reshape(groups, SSM_STATE, SSM_GROUP)
            bg['b_b_im'][j] = dbim.reshape(groups, SSM_STATE, SSM_GROUP)
            bg['b_c_re'][j], bg['b_c_im'][j] = _from_blockdiag_ct(dcr, nb), _from_blockdiag_ct(dci, nb)
            bg['b_d'][j] = ddd[0]
            mm_g("s5_in_dw", mix['hn'], du, 'b_w_in', j)
            dhn = mm_t("s5_in_dx", du, 'b_w_in', j, F32)
        dh, g_mix[i] = _rms_bwd("rms_mix_bwd", mix['h'], norm_mix_g[i:i + 1], dhn, dh)
    grad_x = dh[None]

    sgrad['norm_mix_g'] = jnp.concatenate(g_mix, axis=0)
    sgrad['norm_ffn_g'] = jnp.concatenate(g_ffn, axis=0)
    sgrad['f_conv_w'] = jnp.stack(g_cw)
    sgrad['f_conv_b'] = jnp.concatenate(g_cb, axis=0)
    for k, v_ in list(sg.items()) + list(bg.items()):
        sgrad[k] = jnp.stack(v_)

    total = _allreduce_small(_pack([sgrad[n] for n in SMALL]))
    full_shapes = [sgrad[n].shape for n in SMALL]
    gsmall = dict(zip(SMALL, _unpack(total, full_shapes)))
    for n, axis in CHIP_SHARDED_SMALL.items():
        width = w[n].shape[axis]
        gsmall[n] = lax.dynamic_slice_in_dim(gsmall[n], kchip * width, width, axis=axis)
    pk = lambda t: _pack([t[n] for n in SMALL])
    gpacked = pk(gsmall)
    dpk, mpk, vpk = _adam_small(pk(w), pk(mom), pk(var), gpacked)
    shard_shapes = [w[n].shape for n in SMALL]
    out_g = dict(gsmall)
    out_d = dict(zip(SMALL, _unpack(dpk, shard_shapes)))
    out_m = dict(zip(SMALL, _unpack(mpk, shard_shapes)))
    out_v = dict(zip(SMALL, _unpack(vpk, shard_shapes)))

    own, got = _swap_halves([pgrad[n] for n in big_names])
    sums = [_add2("grad_chip_sum", o, g) for o, g in zip(own, got)]
    mine, recvd = _scatter_chips(sums)
    hsum = [_add4("grad_total", o, r) for o, r in zip(mine, recvd)]
    gfull = _join_halves(hsum)
    for n, gf_ in zip(big_names, gfull):
        out_g[n], out_d[n], out_m[n], out_v[n] = _adam_big("adam_" + n, w[n], mom[n], var[n], gf_, BIG[n])

    return (loss, grad_x, *[out_g[n] for n in W_NAMES], *[out_d[n] for n in W_NAMES],
            *[out_m[n] for n in W_NAMES], *[out_v[n] for n in W_NAMES])
```

```python
import functools
import itertools
import math

import jax
import jax.numpy as jnp
from jax import lax
from jax.experimental import pallas as pl
from jax.experimental.pallas import tpu as pltpu

F32, BF16 = jnp.float32, jnp.bfloat16
MESH = pl.DeviceIdType.MESH

CHUNK = 128
SEG = CHUNK + 4
SGU_GROUP = 128
SSM_GROUP = 16
SSM_STATE = 64
EPS = 1e-6
LANES = 128
GROUPS_PER_BLOCK = LANES // SSM_GROUP
STATE_BLOCKS = SSM_STATE // SSM_GROUP
VMEM_LIMIT = 52 * 1024 * 1024

ADAM_LR, ADAM_B1, ADAM_B2, ADAM_EPS, ADAM_WD, ADAM_STEP = 0.001, 0.9, 0.999, 1e-08, 0.01, 10

W_NAMES = ['norm_mix_g', 'norm_ffn_g', 'a_w_in', 'a_g_v', 'a_w_s', 'a_b_s', 'a_w_out', 'b_w_in', 'b_a_re', 'b_a_im',
           'b_log_dt', 'b_b_re', 'b_b_im', 'b_c_re', 'b_c_im', 'b_d', 'b_w_glu', 'f_w_up', 'f_conv_w', 'f_conv_b',
           'f_w_down', 'final_g']
BIG = {'a_w_in': 'col', 'a_w_out': 'row', 'b_w_in': 'row', 'b_w_glu': 'col', 'f_w_up': 'col', 'f_w_down': 'row'}
SMALL = [n for n in W_NAMES if n not in BIG]
CHIP_SHARDED_SMALL = {'b_d': 1, 'f_conv_w': 2}


def _tile(n, pref, align):
    t = min(n, pref)
    t -= t % align
    while t >= align:
        if n % t == 0:
            return t
        t -= align
    return n


def _params(*sem):
    return pltpu.CompilerParams(dimension_semantics=sem, vmem_limit_bytes=VMEM_LIMIT)


def _gelu(x):
    c = math.sqrt(2.0 / math.pi)
    return 0.5 * x * (1.0 + jnp.tanh(c * (x + 0.044715 * x * x * x)))


def _gelu_grad(x):
    c = math.sqrt(2.0 / math.pi)
    t = jnp.tanh(c * (x + 0.044715 * x * x * x))
    return 0.5 * (1.0 + t) + 0.5 * x * (1.0 - t * t) * c * (1.0 + 3.0 * 0.044715 * x * x)


def _half_shape(kind, shard_shape):
    _, r, c = shard_shape
    return (r // 2, c) if kind == 'col' else (r, c // 2)


def _full_dims(kind, shard_shape):
    _, r, c = shard_shape
    return (r, 4 * c) if kind == 'col' else (4 * r, c)


def _gspec(kind, kdim, ndim, tr, tc, layer, rc):
    if kind == 'col':
        nr, nc = (kdim // 2) // tr, (ndim // 4) // tc

        def imap(*g):
            rb, cb = rc(*g)
            return (cb // nc, rb // nr, layer, rb % nr, cb % nc)
    else:
        nr, nc = (kdim // 4) // tr, (ndim // 2) // tc

        def imap(*g):
            rb, cb = rc(*g)
            return (rb // nr, cb // nc, layer, rb % nr, cb % nc)
    return pl.BlockSpec((None, None, None, tr, tc), imap)


def _wtiles(kind, kdim, ndim):
    if kind == 'col':
        return _tile(kdim // 2, 1024, LANES), _tile(ndim // 4, 1408, LANES)
    return _tile(kdim // 4, 1408, LANES), _tile(ndim // 2, 1024, LANES)


_DIMS = {'nn': (((1,), (0,)), ((), ())), 'nt': (((1,), (1,)), ((), ())), 'tn': (((0,), (0,)), ((), ()))}


def _matmul(name, mode, a, b, grid, a_spec, b_spec, out_shape, out_spec, acc_shape, extras=(), extra_specs=(),
            epilogue=None, aliases=None):
    nk = grid[2]
    dims = _DIMS[mode]
    n_extra = len(extras)

    def body(a_ref, b_ref, *rest):
        extra_refs, o_ref = rest[:n_extra], rest[n_extra]
        prod = lax.dot_general(a_ref[...].astype(BF16), b_ref[...].astype(BF16), dims, preferred_element_type=F32)

        def finish(r):
            if epilogue is not None:
                r = epilogue(r, *[e[...] for e in extra_refs])
            o_ref[...] = r.astype(o_ref.dtype)

        if nk == 1:
            finish(prod)
            return
        acc_ref = rest[n_extra + 1]
        kk = pl.program_id(2)

        @pl.when(kk == 0)
        def _():
            acc_ref[...] = prod

        @pl.when(kk > 0)
        def _():
            acc_ref[...] += prod

        @pl.when(kk == nk - 1)
        def _():
            finish(acc_ref[...])

    scratch = [pltpu.VMEM(acc_shape, F32)] if nk > 1 else []
    return pl.pallas_call(
        body, name=name, grid=grid, in_specs=[a_spec, b_spec, *extra_specs], out_specs=out_spec, out_shape=out_shape,
        scratch_shapes=scratch, input_output_aliases=aliases or {},
        compiler_params=_params("parallel", "parallel", "arbitrary"))(a, b, *extras)


def _mm_x_w(name, a, wg, kind, kdim, ndim, layer, out_dtype, residual=None):
    rows = a.shape[0]
    tk, tn = _wtiles(kind, kdim, ndim)
    tm = _tile(rows, 1024, 16)
    grid = (rows // tm, ndim // tn, kdim // tk)
    extras, especs, epi = (), (), None
    if residual is not None:
        extras, especs = (residual,), (pl.BlockSpec((tm, tn), lambda i, j, k: (i, j)),)
        epi = lambda r, res: r + res
    return _matmul(name, 'nn', a, wg, grid, pl.BlockSpec((tm, tk), lambda i, j, k: (i, k)),
                   _gspec(kind, kdim, ndim, tk, tn, layer, lambda i, j, k: (k, j)),
                   jax.ShapeDtypeStruct((rows, ndim), out_dtype), pl.BlockSpec((tm, tn), lambda i, j, k: (i, j)),
                   (tm, tn), extras, especs, epi)


def _mm_dy_wt(name, dy, wg, kind, kdim, ndim, layer, out_dtype):
    rows = dy.shape[0]
    tn, tk = _wtiles(kind, kdim, ndim)
    tm = _tile(rows, 1024, 16)
    grid = (rows // tm, kdim // tn, ndim // tk)
    return _matmul(name, 'nt', dy, wg, grid, pl.BlockSpec((tm, tk), lambda i, j, k: (i, k)),
                   _gspec(kind, kdim, ndim, tn, tk, layer, lambda i, j, k: (j, k)),
                   jax.ShapeDtypeStruct((rows, kdim), out_dtype), pl.BlockSpec((tm, tn), lambda i, j, k: (i, j)),
                   (tm, tn))


def _mm_xt_dy(name, xa, dy, pg, kind, kdim, ndim, layer):
    rows = xa.shape[0]
    tm, tn = _wtiles(kind, kdim, ndim)
    tl = _tile(rows, 1024, 16)
    grid = (kdim // tm, ndim // tn, rows // tl)
    return _matmul(name, 'tn', xa, dy, grid, pl.BlockSpec((tl, tm), lambda i, j, k: (k, i)),
                   pl.BlockSpec((tl, tn), lambda i, j, k: (k, j)),
                   jax.ShapeDtypeStruct(pg.shape, pg.dtype),
                   _gspec(kind, kdim, ndim, tm, tn, layer, lambda i, j, k: (i, j)), (tm, tn),
                   extras=(pg,), extra_specs=(pl.BlockSpec(memory_space=pl.ANY),), aliases={2: 0})


def _rms_fwd(name, h, g):
    rows, d = h.shape
    tm = _tile(rows, 256, 16)

    def body(h_ref, g_ref, o_ref):
        x = h_ref[...]
        r = lax.rsqrt(jnp.mean(x * x, axis=-1, keepdims=True) + EPS)
        o_ref[...] = (x * r * g_ref[...]).astype(o_ref.dtype)

    return pl.pallas_call(
        body, name=name, grid=(rows // tm,),
        in_specs=[pl.BlockSpec((tm, d), lambda i: (i, 0)), pl.BlockSpec((1, d), lambda i: (0, 0))],
        out_specs=pl.BlockSpec((tm, d), lambda i: (i, 0)), out_shape=jax.ShapeDtypeStruct((rows, d), BF16),
        compiler_params=_params("parallel"))(h, g)


def _rms_bwd(name, h, g, dhn, dres):
    rows, d = h.shape
    tm = _tile(rows, 256, 16)

    def body(h_ref, g_ref, dy_ref, dres_ref, dh_ref, dg_ref):
        x = h_ref[...]
        r = lax.rsqrt(jnp.mean(x * x, axis=-1, keepdims=True) + EPS)
        xh = x * r
        dy = dy_ref[...].astype(F32)
        gy = dy * g_ref[...]
        dh_ref[...] = dres_ref[...] + r * (gy - xh * jnp.mean(gy * xh, axis=-1, keepdims=True))
        part = jnp.sum(dy * xh, axis=0, keepdims=True)

        @pl.when(pl.program_id(0) == 0)
        def _():
            dg_ref[...] = part

        @pl.when(pl.program_id(0) > 0)
        def _():
            dg_ref[...] += part

    row = pl.BlockSpec((tm, d), lambda i: (i, 0))
    vec = pl.BlockSpec((1, d), lambda i: (0, 0))
    return pl.pallas_call(
        body, name=name, grid=(rows // tm,), in_specs=[row, vec, row, row], out_specs=[row, vec],
        out_shape=[jax.ShapeDtypeStruct((rows, d), F32), jax.ShapeDtypeStruct((1, d), F32)],
        compiler_params=_params("arbitrary"))(h, g, dhn, dres)


def _loss_head(h, g, target):
    rows, d = h.shape
    tm = _tile(rows, 256, 16)

    def body(h_ref, g_ref, t_ref, dh_ref, dg_ref, loss_ref):
        x = h_ref[...]
        r = lax.rsqrt(jnp.mean(x * x, axis=-1, keepdims=True) + EPS)
        xh = x * r
        err = xh * g_ref[...] - t_ref[...]
        dy = err * (1.0 / d)
        gy = dy * g_ref[...]
        dh_ref[...] = r * (gy - xh * jnp.mean(gy * xh, axis=-1, keepdims=True))
        part = jnp.sum(dy * xh, axis=0, keepdims=True)
        sq = jnp.sum(err * err, axis=0, keepdims=True) * (0.5 / d)

        @pl.when(pl.program_id(0) == 0)
        def _():
            dg_ref[...] = part
            loss_ref[...] = sq

        @pl.when(pl.program_id(0) > 0)
        def _():
            dg_ref[...] += part
            loss_ref[...] += sq

    row = pl.BlockSpec((tm, d), lambda i: (i, 0))
    vec = pl.BlockSpec((1, d), lambda i: (0, 0))
    return pl.pallas_call(
        body, name="loss_head", grid=(rows // tm,), in_specs=[row, vec, row], out_specs=[row, vec, vec],
        out_shape=[jax.ShapeDtypeStruct((rows, d), F32), jax.ShapeDtypeStruct((1, d), F32),
                   jax.ShapeDtypeStruct((1, d), F32)],
        compiler_params=_params("arbitrary"))(h, g, target)


def _shift_down(cur, prev8, first, k):
    rows = cur.shape[0]
    rolled = pltpu.roll(cur, k, axis=0)
    idx = lax.broadcasted_iota(jnp.int32, cur.shape, 0)
    prev8 = jnp.where(first, 0.0, prev8)
    out = rolled
    for r in range(k):
        out = jnp.where(idx == r, prev8[8 - k + r:8 - k + r + 1, :], out)
    del rows
    return out


def _shift_up(cur, next8, last, k):
    rows = cur.shape[0]
    rolled = pltpu.roll(cur, rows - k, axis=0)
    idx = lax.broadcasted_iota(jnp.int32, cur.shape, 0)
    next8 = jnp.where(last, 0.0, next8)
    out = rolled
    for r in range(k):
        out = jnp.where(idx == rows - k + r, next8[r:r + 1, :], out)
    return out


def _conv_acc(z, zprev, first, w_ref, b_ref):
    z1 = _shift_down(z, zprev, first, 1)
    z2 = _shift_down(z, zprev, first, 2)
    return b_ref[...] + w_ref[2:3, :] * z + w_ref[1:2, :] * z1 + w_ref[0:1, :] * z2, z1, z2


def _ffn_tiles(rows, f):
    return _tile(rows, 512, 16), _tile(f, 512, LANES)


def _ffn_act_fwd(name, z, cw, cb):
    rows, f2 = z.shape
    f = f2 // 2
    tm, tc = _ffn_tiles(rows, f)
    nf = f // tc
    hb = tm // 8

    def body(zg_ref, zgp_ref, zv_ref, zvp_ref, wg_ref, wv_ref, bg_ref, bv_ref, o_ref):
        first = pl.program_id(0) == 0
        gate, _, _ = _conv_acc(zg_ref[...].astype(F32), zgp_ref[...].astype(F32), first, wg_ref, bg_ref)
        val, _, _ = _conv_acc(zv_ref[...].astype(F32), zvp_ref[...].astype(F32), first, wv_ref, bv_ref)
        o_ref[...] = (gate * jax.nn.sigmoid(gate) * val).astype(o_ref.dtype)

    cur = lambda off: pl.BlockSpec((tm, tc), lambda i, j: (i, j + off))
    prev = lambda off: pl.BlockSpec((8, tc), lambda i, j: (jnp.maximum(i * hb - 1, 0), j + off))
    wsp = lambda off: pl.BlockSpec((3, tc), lambda i, j: (0, j + off))
    bsp = lambda off: pl.BlockSpec((1, tc), lambda i, j: (0, j + off))
    return pl.pallas_call(
        body, name=name, grid=(rows // tm, nf),
        in_specs=[cur(0), prev(0), cur(nf), prev(nf), wsp(0), wsp(nf), bsp(0), bsp(nf)],
        out_specs=pl.BlockSpec((tm, tc), lambda i, j: (i, j)), out_shape=jax.ShapeDtypeStruct((rows, f), BF16),
        compiler_params=_params("parallel", "parallel"))(z, z, z, z, cw, cw, cb, cb)


def _ffn_act_bwd(name, z, da, cw, cb):
    rows, f2 = z.shape
    f = f2 // 2
    tm, tc = _ffn_tiles(rows, f)
    nf = f // tc
    hb = tm // 8

    def body(zs_ref, zsp_ref, zo_ref, zop_ref, ws_ref, wo_ref, bs_ref, bo_ref, da_ref, dacc_ref, dcw_ref, dcb_ref):
        i = pl.program_id(1)
        first = i == 0
        is_gate = pl.program_id(0) < nf
        acc_s, z1, z2 = _conv_acc(zs_ref[...].astype(F32), zsp_ref[...].astype(F32), first, ws_ref, bs_ref)
        acc_o, _, _ = _conv_acc(zo_ref[...].astype(F32), zop_ref[...].astype(F32), first, wo_ref, bo_ref)
        d_a = da_ref[...].astype(F32)
        sig_s = jax.nn.sigmoid(acc_s)
        sig_o = jax.nn.sigmoid(acc_o)
        as_gate = d_a * acc_o * sig_s * (1.0 + acc_s * (1.0 - sig_s))
        as_val = d_a * acc_o * sig_o
        dacc = jnp.where(is_gate, as_gate, as_val)
        dacc_ref[...] = dacc.astype(dacc_ref.dtype)
        zc = zs_ref[...].astype(F32)
        taps = [jnp.sum(dacc * t, axis=0, keepdims=True) for t in (z2, z1, zc)]
        pb = jnp.sum(dacc, axis=0, keepdims=True)

        @pl.when(first)
        def _():
            for k in range(3):
                dcw_ref[k:k + 1, :] = taps[k]
            dcb_ref[...] = pb

        @pl.when(i > 0)
        def _():
            for k in range(3):
                dcw_ref[k:k + 1, :] += taps[k]
            dcb_ref[...] += pb

    n2 = 2 * nf
    other = lambda j: (j + nf) % n2
    cur = lambda col: pl.BlockSpec((tm, tc), lambda j, i: (i, col(j)))
    prev = lambda col: pl.BlockSpec((8, tc), lambda j, i: (jnp.maximum(i * hb - 1, 0), col(j)))
    wsp = lambda col: pl.BlockSpec((3, tc), lambda j, i: (0, col(j)))
    bsp = lambda col: pl.BlockSpec((1, tc), lambda j, i: (0, col(j)))
    ident = lambda j: j
    return pl.pallas_call(
        body, name=name, grid=(n2, rows // tm),
        in_specs=[cur(ident), prev(ident), cur(other), prev(other), wsp(ident), wsp(other), bsp(ident), bsp(other),
                  pl.BlockSpec((tm, tc), lambda j, i: (i, j % nf))],
        out_specs=[cur(ident), wsp(ident), bsp(ident)],
        out_shape=[jax.ShapeDtypeStruct((rows, f2), BF16), jax.ShapeDtypeStruct((3, f2), F32),
                   jax.ShapeDtypeStruct((1, f2), F32)],
        compiler_params=_params("parallel", "arbitrary"))(z, z, z, z, cw, cw, cb, cb, da)


def _conv_bwd(name, dacc, cw):
    rows, f2 = dacc.shape
    tm, tc = _ffn_tiles(rows, f2 // 2)
    hb = tm // 8
    nrow = rows // tm

    def body(d_ref, dn_ref, w_ref, o_ref):
        last = pl.program_id(0) == nrow - 1
        d = d_ref[...].astype(F32)
        nxt = dn_ref[...].astype(F32)
        d1 = _shift_up(d, nxt, last, 1)
        d2 = _shift_up(d, nxt, last, 2)
        o_ref[...] = (w_ref[2:3, :] * d + w_ref[1:2, :] * d1 + w_ref[0:1, :] * d2).astype(o_ref.dtype)

    return pl.pallas_call(
        body, name=name, grid=(nrow, f2 // tc),
        in_specs=[pl.BlockSpec((tm, tc), lambda i, j: (i, j)),
                  pl.BlockSpec((8, tc), lambda i, j: (jnp.minimum((i + 1) * hb, rows // 8 - 1), j)),
                  pl.BlockSpec((3, tc), lambda i, j: (0, j))],
        out_specs=pl.BlockSpec((tm, tc), lambda i, j: (i, j)), out_shape=jax.ShapeDtypeStruct((rows, f2), BF16),
        compiler_params=_params("parallel", "parallel"))(dacc, dacc, cw)


def _glu_fwd(name, gg, h):
    rows, d2 = gg.shape
    d = d2 // 2
    tm, tc = _tile(rows, 512, 16), _tile(d, 1024, LANES)
    nd = d // tc

    def body(a_ref, b_ref, h_ref, o_ref):
        o_ref[...] = h_ref[...] + a_ref[...].astype(F32) * jax.nn.sigmoid(b_ref[...].astype(F32))

    return pl.pallas_call(
        body, name=name, grid=(rows // tm, nd),
        in_specs=[pl.BlockSpec((tm, tc), lambda i, j: (i, j)), pl.BlockSpec((tm, tc), lambda i, j: (i, j + nd)),
                  pl.BlockSpec((tm, tc), lambda i, j: (i, j))],
        out_specs=pl.BlockSpec((tm, tc), lambda i, j: (i, j)), out_shape=jax.ShapeDtypeStruct((rows, d), F32),
        compiler_params=_params("parallel", "parallel"))(gg, gg, h)


def _glu_bwd(name, gg, dh):
    rows, d2 = gg.shape
    d = d2 // 2
    tm, tc = _tile(rows, 512, 16), _tile(d, 1024, LANES)
    nd = d // tc

    def body(s_ref, o_ref, dh_ref, out_ref):
        is_a = pl.program_id(1) < nd
        me = s_ref[...].astype(F32)
        other = o_ref[...].astype(F32)
        g = dh_ref[...]
        sig_o = jax.nn.sigmoid(other)
        sig_m = jax.nn.sigmoid(me)
        out_ref[...] = jnp.where(is_a, g * sig_o, g * other * sig_m * (1.0 - sig_m)).astype(out_ref.dtype)

    return pl.pallas_call(
        body, name=name, grid=(rows // tm, 2 * nd),
        in_specs=[pl.BlockSpec((tm, tc), lambda i, j: (i, j)),
                  pl.BlockSpec((tm, tc), lambda i, j: (i, (j + nd) % (2 * nd))),
                  pl.BlockSpec((tm, tc), lambda i, j: (i, j % nd))],
        out_specs=pl.BlockSpec((tm, tc), lambda i, j: (i, j)), out_shape=jax.ShapeDtypeStruct((rows, d2), BF16),
        compiler_params=_params("parallel", "parallel"))(gg, gg, dh)


def _sgu_common(pre_ref, gv_ref, e):
    u = _gelu(pre_ref[:, :e].astype(F32))
    v = _gelu(pre_ref[:, e:].astype(F32))
    r = lax.rsqrt(jnp.mean(v * v, axis=-1, keepdims=True) + EPS)
    vh = v * r
    return u, vh, r, (vh * gv_ref[...]).astype(BF16)


def _tril_bf16(ws_ref, hd):
    t = lax.broadcasted_iota(jnp.int32, (CHUNK, CHUNK), 0)
    s = lax.broadcasted_iota(jnp.int32, (CHUNK, CHUNK), 1)
    return jnp.where(s <= t, ws_ref[hd], 0.0).astype(BF16)


def _sgu_mix_fwd(name, pre, gv, ws, bsx):
    rows, e2 = pre.shape
    e = e2 // 2
    heads = e // SGU_GROUP
    tr = _tile(rows, 256, CHUNK)

    def body(pre_ref, gv_ref, ws_ref, bs_ref, o_ref):
        u, _, _, vn = _sgu_common(pre_ref, gv_ref, e)
        for hd in range(heads):
            wm = _tril_bf16(ws_ref, hd)
            cols = slice(hd * SGU_GROUP, (hd + 1) * SGU_GROUP)
            for ck in range(tr // CHUNK):
                rws = slice(ck * CHUNK, (ck + 1) * CHUNK)
                s = jnp.dot(wm, vn[rws, cols], preferred_element_type=F32) + bs_ref[hd]
                o_ref[rws, cols] = (u[rws, cols] * s).astype(o_ref.dtype)

    whole3 = pl.BlockSpec((heads, CHUNK, CHUNK), lambda i: (0, 0, 0))
    return pl.pallas_call(
        body, name=name, grid=(rows // tr,),
        in_specs=[pl.BlockSpec((tr, e2), lambda i: (i, 0)), pl.BlockSpec((1, e), lambda i: (0, 0)), whole3, whole3],
        out_specs=pl.BlockSpec((tr, e), lambda i: (i, 0)), out_shape=jax.ShapeDtypeStruct((rows, e), BF16),
        compiler_params=_params("parallel"))(pre, gv, ws, bsx)


def _sgu_mix_bwd(name, pre, dus, gv, ws, bsx):
    rows, e2 = pre.shape
    e = e2 // 2
    heads = e // SGU_GROUP
    tr = _tile(rows, 256, CHUNK)

    def body(pre_ref, dus_ref, gv_ref, ws_ref, bs_ref, dpre_ref, dws_ref, dbs_ref, dgv_ref, dvn_ref, du_ref):
        first = pl.program_id(0) == 0
        u, vh, r, vn = _sgu_common(pre_ref, gv_ref, e)
        ones = jnp.ones((SGU_GROUP, LANES), BF16)
        tt = lax.broadcasted_iota(jnp.int32, (CHUNK, CHUNK), 0)
        ss = lax.broadcasted_iota(jnp.int32, (CHUNK, CHUNK), 1)
        for hd in range(heads):
            wm = _tril_bf16(ws_ref, hd)
            cols = slice(hd * SGU_GROUP, (hd + 1) * SGU_GROUP)
            dw = jnp.zeros((CHUNK, CHUNK), F32)
            db = jnp.zeros((CHUNK, LANES), F32)
            for ck in range(tr // CHUNK):
                rws = slice(ck * CHUNK, (ck + 1) * CHUNK)
                vblk = vn[rws, cols]
                s = jnp.dot(wm, vblk, preferred_element_type=F32) + bs_ref[hd]
                d_us = dus_ref[rws, cols].astype(F32)
                du_ref[rws, cols] = d_us * s
                ds = (d_us * u[rws, cols]).astype(BF16)
                dvn_ref[rws, cols] = lax.dot_general(wm, ds, _DIMS['tn'], preferred_element_type=F32)
                dw = dw + lax.dot_general(ds, vblk, _DIMS['nt'], preferred_element_type=F32)
                db = db + jnp.dot(ds, ones, preferred_element_type=F32)
            dw = jnp.where(ss <= tt, dw, 0.0)

            @pl.when(first)
            def _():
                dws_ref[hd] = dw
                dbs_ref[hd] = db

            @pl.when(jnp.logical_not(first))
            def _():
                dws_ref[hd] += dw
                dbs_ref[hd] += db

        dvn = dvn_ref[...]
        part = jnp.sum(dvn * vh, axis=0, keepdims=True)

        @pl.when(first)
        def _():
            dgv_ref[...] = part

        @pl.when(jnp.logical_not(first))
        def _():
            dgv_ref[...] += part

        gy = dvn * gv_ref[...]
        dv = r * (gy - vh * jnp.mean(gy * vh, axis=-1, keepdims=True))
        dpre_ref[:, :e] = (du_ref[...] * _gelu_grad(pre_ref[:, :e].astype(F32))).astype(dpre_ref.dtype)
        dpre_ref[:, e:] = (dv * _gelu_grad(pre_ref[:, e:].astype(F32))).astype(dpre_ref.dtype)

    whole3 = pl.BlockSpec((heads, CHUNK, CHUNK), lambda i: (0, 0, 0))
    vec = pl.BlockSpec((1, e), lambda i: (0, 0))
    return pl.pallas_call(
        body, name=name, grid=(rows // tr,),
        in_specs=[pl.BlockSpec((tr, e2), lambda i: (i, 0)), pl.BlockSpec((tr, e), lambda i: (i, 0)), vec, whole3, whole3],
        out_specs=[pl.BlockSpec((tr, e2), lambda i: (i, 0)), whole3, whole3, vec],
        out_shape=[jax.ShapeDtypeStruct((rows, e2), BF16), jax.ShapeDtypeStruct((heads, CHUNK, CHUNK), F32),
                   jax.ShapeDtypeStruct((heads, CHUNK, LANES), F32), jax.ShapeDtypeStruct((1, e), F32)],
        scratch_shapes=[pltpu.VMEM((tr, e), F32), pltpu.VMEM((tr, e), F32)],
        compiler_params=_params("arbitrary"))(pre, dus, gv, ws, bsx)


def _disc_a(lr, li, ldt):
    dt = jnp.exp(ldt)
    mag = jnp.exp(dt * lr)
    return mag * jnp.cos(dt * li), mag * jnp.sin(dt * li)


def _disc_b(lr, li, ldt, br, bi):
    ar, ai = _disc_a(lr, li, ldt)
    den = lr * lr + li * li
    qr = ((ar - 1.0) * lr + ai * li) / den
    qi = (ai * lr - (ar - 1.0) * li) / den
    return qr * br - qi * bi, qr * bi + qi * br


def _s5_disc(name, lr, li, ldt, lrx, lix, ldtx, br, bi):
    def body(lr_ref, li_ref, ldt_ref, lrx_ref, lix_ref, ldtx_ref, br_ref, bi_ref, ar_ref, ai_ref, bbr_ref, bbi_ref):
        ar_ref[...], ai_ref[...] = _disc_a(lr_ref[...], li_ref[...], ldt_ref[...])
        bbr_ref[...], bbi_ref[...] = _disc_b(lrx_ref[...], lix_ref[...], ldtx_ref[...], br_ref[...], bi_ref[...])

    small = jax.ShapeDtypeStruct(lr.shape, F32)
    wide = jax.ShapeDtypeStruct(br.shape, F32)
    return pl.pallas_call(body, name=name, out_shape=[small, small, wide, wide],
                          compiler_params=pltpu.CompilerParams(vmem_limit_bytes=VMEM_LIMIT))(
        lr, li, ldt, lrx, lix, ldtx, br, bi)


def _s5_disc_bwd(name, lr, li, ldt, lrx, lix, ldtx, br, bi, dar, dai, dbbr, dbbi, sel):
    def body(lr_ref, li_ref, ldt_ref, lrx_ref, lix_ref, ldtx_ref, br_ref, bi_ref, dar_ref, dai_ref, dbbr_ref,
             dbbi_ref, sel_ref, dlr_ref, dli_ref, dldt_ref, dbr_ref, dbi_ref):
        _, vjp_a = jax.vjp(_disc_a, lr_ref[...], li_ref[...], ldt_ref[...])
        g_lr, g_li, g_ldt = vjp_a((dar_ref[...], dai_ref[...]))
        _, vjp_b = jax.vjp(_disc_b, lrx_ref[...], lix_ref[...], ldtx_ref[...], br_ref[...], bi_ref[...])
        x_lr, x_li, x_ldt, g_br, g_bi = vjp_b((dbbr_ref[...], dbbi_ref[...]))
        fold = lambda t: jnp.dot(t, sel_ref[...], precision=lax.Precision.HIGHEST, preferred_element_type=F32)
        dlr_ref[...] = g_lr + fold(x_lr)
        dli_ref[...] = g_li + fold(x_li)
        dldt_ref[...] = jnp.sum(g_ldt + fold(x_ldt), axis=1, keepdims=True)
        dbr_ref[...] = g_br
        dbi_ref[...] = g_bi

    small = jax.ShapeDtypeStruct(lr.shape, F32)
    wide = jax.ShapeDtypeStruct(br.shape, F32)
    return pl.pallas_call(body, name=name,
                          out_shape=[small, small, jax.ShapeDtypeStruct((lr.shape[0], 1), F32), wide, wide],
                          compiler_params=pltpu.CompilerParams(vmem_limit_bytes=VMEM_LIMIT))(
        lr, li, ldt, lrx, lix, ldtx, br, bi, dar, dai, dbbr, dbbi, sel)


def _cmul(ar, ai, br, bi):
    return ar * br - ai * bi, ar * bi + ai * br


def _pow_seg(ar, ai):
    res, base, n = None, (ar, ai), SEG
    while n:
        if n & 1:
            res = base if res is None else _cmul(*res, *base)
        n >>= 1
        if n:
            base = _cmul(*base, *base)
    return res


def _scan_forward(hr_ref, hi_ref, er_ref, ei_ref, sr_ref, si_ref, ar, ai, nck):
    arb, aib = jnp.broadcast_to(ar, (nck, LANES)), jnp.broadcast_to(ai, (nck, LANES))

    def intra(t, carry):
        sr, si = carry
        slab = pl.ds(t, nck, stride=SEG)
        nr = arb * sr - aib * si + hr_ref[slab, :]
        ni = arb * si + aib * sr + hi_ref[slab, :]
        hr_ref[slab, :] = nr
        hi_ref[slab, :] = ni
        return nr, ni

    zero = jnp.zeros((nck, LANES), F32)
    er_ref[...], ei_ref[...] = lax.fori_loop(0, SEG, intra, (zero, zero), unroll=4)
    pcr, pci = _pow_seg(ar, ai)
    sr_ref[0:1, :] = jnp.zeros((1, LANES), F32)
    si_ref[0:1, :] = jnp.zeros((1, LANES), F32)
    for ck in range(nck - 1):
        pr, pi = sr_ref[ck:ck + 1, :], si_ref[ck:ck + 1, :]
        sr_ref[ck + 1:ck + 2, :] = pcr * pr - pci * pi + er_ref[ck:ck + 1, :]
        si_ref[ck + 1:ck + 2, :] = pcr * pi + pci * pr + ei_ref[ck:ck + 1, :]
    s_r, s_i = sr_ref[...], si_ref[...]

    def fix(t, carry):
        pr, pi = carry
        slab = pl.ds(t, nck, stride=SEG)
        hr_ref[slab, :] = hr_ref[slab, :] + (pr * s_r - pi * s_i)
        hi_ref[slab, :] = hi_ref[slab, :] + (pr * s_i + pi * s_r)
        return _cmul(pr, pi, arb, aib)

    lax.fori_loop(0, SEG, fix, (arb, aib), unroll=4)


def _scan_backward(gr_ref, gi_ref, hr_ref, hi_ref, er_ref, ei_ref, sr_ref, si_ref, ar, ai, nck):
    arb, aib = jnp.broadcast_to(ar, (nck, LANES)), jnp.broadcast_to(-ai, (nck, LANES))

    def intra(k, carry):
        sr, si = carry
        slab = pl.ds(SEG - 1 - k, nck, stride=SEG)
        nr = arb * sr - aib * si + gr_ref[slab, :]
        ni = arb * si + aib * sr + gi_ref[slab, :]
        gr_ref[slab, :] = nr
        gi_ref[slab, :] = ni
        return nr, ni

    zero = jnp.zeros((nck, LANES), F32)
    er_ref[...], ei_ref[...] = lax.fori_loop(0, SEG, intra, (zero, zero), unroll=4)
    pcr, pci = _pow_seg(ar, -ai)
    sr_ref[nck - 1:nck, :] = jnp.zeros((1, LANES), F32)
    si_ref[nck - 1:nck, :] = jnp.zeros((1, LANES), F32)
    for ck in range(nck - 1, 0, -1):
        pr, pi = sr_ref[ck:ck + 1, :], si_ref[ck:ck + 1, :]
        sr_ref[ck - 1:ck, :] = pcr * pr - pci * pi + er_ref[ck:ck + 1, :]
        si_ref[ck - 1:ck, :] = pcr * pi + pci * pr + ei_ref[ck:ck + 1, :]
    s_r, s_i = sr_ref[...], si_ref[...]
    last = pl.ds(SEG - 1, nck, stride=SEG)
    row = lax.broadcasted_iota(jnp.int32, (nck, LANES), 0)
    hp_r = jnp.where(row == 0, 0.0, pltpu.roll(hr_ref[last, :], 1, axis=0)) if nck > 1 else zero
    hp_i = jnp.where(row == 0, 0.0, pltpu.roll(hi_ref[last, :], 1, axis=0)) if nck > 1 else zero

    def settle(t, pr, pi, h_r, h_i):
        slab = pl.ds(t, nck, stride=SEG)
        g_r = gr_ref[slab, :] + (pr * s_r - pi * s_i)
        g_i = gi_ref[slab, :] + (pr * s_i + pi * s_r)
        gr_ref[slab, :] = g_r
        gi_ref[slab, :] = g_i
        return g_r * h_r + g_i * h_i, g_i * h_r - g_r * h_i

    def fix(k, carry):
        pr, pi, acr, aci = carry
        t = SEG - 1 - k
        prev = pl.ds(t - 1, nck, stride=SEG)
        d_r, d_i = settle(t, pr, pi, hr_ref[prev, :], hi_ref[prev, :])
        nr, ni = _cmul(pr, pi, arb, aib)
        return nr, ni, acr + d_r, aci + d_i

    pr, pi, acr, aci = lax.fori_loop(0, SEG - 1, fix, (arb, aib, zero, zero), unroll=4)
    d_r, d_i = settle(0, pr, pi, hp_r, hp_i)
    return jnp.sum(acr + d_r, axis=0, keepdims=True), jnp.sum(aci + d_i, axis=0, keepdims=True)


def _s5_fill_states(u_ref, br_ref, bi_ref, hr_ref, hi_ref, rows):
    ub = u_ref[...].astype(BF16)
    hr_ref[0:rows, :] = jnp.dot(ub, br_ref[...], preferred_element_type=F32)
    hi_ref[0:rows, :] = jnp.dot(ub, bi_ref[...], preferred_element_type=F32)
    pad = jnp.zeros((hr_ref.shape[0] - rows, LANES), F32)
    hr_ref[rows:, :] = pad
    hi_ref[rows:, :] = pad


def _s5_specs(rows, e):
    sb = STATE_BLOCKS
    chan = pl.BlockSpec((rows, LANES), lambda j: (0, j // sb))
    bmat = pl.BlockSpec((None, LANES, LANES), lambda j: (j // sb, 0, j % sb))
    cmat = pl.BlockSpec((None, LANES, LANES), lambda j: (j // sb, j % sb, 0))
    avec = pl.BlockSpec((1, LANES), lambda j: (0, j))
    dvec = pl.BlockSpec((1, LANES), lambda j: (0, j // sb))
    return chan, bmat, cmat, avec, dvec


def _s5_fwd(name, u, bre, bim, crt, cit, ar, ai, dd):
    rows, e = u.shape
    nck = rows // CHUNK
    nsteps = (e // LANES) * STATE_BLOCKS
    chan, bmat, cmat, avec, dvec = _s5_specs(rows, e)

    def body(u_ref, br_ref, bi_ref, cr_ref, ci_ref, ar_ref, ai_ref, dd_ref, y_ref, gy_ref,
             hr_ref, hi_ref, er_ref, ei_ref, sr_ref, si_ref, acc_ref):
        j = pl.program_id(0) % STATE_BLOCKS
        _s5_fill_states(u_ref, br_ref, bi_ref, hr_ref, hi_ref, rows)
        _scan_forward(hr_ref, hi_ref, er_ref, ei_ref, sr_ref, si_ref, ar_ref[...], ai_ref[...], nck)
        contrib = (jnp.dot(hr_ref[0:rows, :].astype(BF16), cr_ref[...], preferred_element_type=F32)
                   - jnp.dot(hi_ref[0:rows, :].astype(BF16), ci_ref[...], preferred_element_type=F32))

        @pl.when(j == 0)
        def _():
            acc_ref[...] = dd_ref[...] * u_ref[...] + contrib

        @pl.when(j > 0)
        def _():
            acc_ref[...] += contrib

        @pl.when(j == STATE_BLOCKS - 1)
        def _():
            y = acc_ref[...]
            y_ref[...] = y.astype(y_ref.dtype)
            gy_ref[...] = _gelu(y).astype(gy_ref.dtype)

    flat = pltpu.VMEM((rows, LANES), F32)
    big = pltpu.VMEM((nck * SEG, LANES), F32)
    small = pltpu.VMEM((nck, LANES), F32)
    out = jax.ShapeDtypeStruct((rows, e), BF16)
    return pl.pallas_call(
        body, name=name, grid=(nsteps,), in_specs=[chan, bmat, bmat, cmat, cmat, avec, avec, dvec],
        out_specs=[chan, chan], out_shape=[out, out], scratch_shapes=[big, big, small, small, small, small, flat],
        compiler_params=_params("arbitrary"))(u, bre, bim, crt, cit, ar, ai, dd)


def _s5_bwd(name, u, y, dgy, bre, bim, crt, cit, ar, ai, dd):
    rows, e = u.shape
    nb = e // LANES
    nck = rows // CHUNK
    nsteps = nb * STATE_BLOCKS
    chan, bmat, cmat, avec, dvec = _s5_specs(rows, e)

    def body(u_ref, y_ref, dgy_ref, br_ref, bi_ref, cr_ref, ci_ref, ar_ref, ai_ref, dd_ref,
             du_ref, dbr_ref, dbi_ref, dcr_ref, dci_ref, dar_ref, dai_ref, ddd_ref,
             hr_ref, hi_ref, gr_ref, gi_ref, er_ref, ei_ref, sr_ref, si_ref, acc_ref, dy_ref):
        j = pl.program_id(0) % STATE_BLOCKS
        _s5_fill_states(u_ref, br_ref, bi_ref, hr_ref, hi_ref, rows)
        _scan_forward(hr_ref, hi_ref, er_ref, ei_ref, sr_ref, si_ref, ar_ref[...], ai_ref[...], nck)

        @pl.when(j == 0)
        def _():
            dy0 = dgy_ref[...].astype(F32) * _gelu_grad(y_ref[...].astype(F32))
            dy_ref[...] = dy0
            ddd_ref[...] = jnp.sum(dy0 * u_ref[...], axis=0, keepdims=True)

        dyb = dy_ref[...].astype(BF16)
        pad = jnp.zeros((gr_ref.shape[0] - rows, LANES), F32)
        gr_ref[0:rows, :] = lax.dot_general(dyb, cr_ref[...], _DIMS['nt'], preferred_element_type=F32)
        gi_ref[0:rows, :] = -lax.dot_general(dyb, ci_ref[...], _DIMS['nt'], preferred_element_type=F32)
        gr_ref[rows:, :] = pad
        gi_ref[rows:, :] = pad
        dcr_ref[...] = lax.dot_general(hr_ref[0:rows, :].astype(BF16), dyb, _DIMS['tn'], preferred_element_type=F32)
        dci_ref[...] = -lax.dot_general(hi_ref[0:rows, :].astype(BF16), dyb, _DIMS['tn'], preferred_element_type=F32)
        dar_ref[...], dai_ref[...] = _scan_backward(gr_ref, gi_ref, hr_ref, hi_ref, er_ref, ei_ref, sr_ref, si_ref,
                                                    ar_ref[...], ai_ref[...], nck)
        ub = u_ref[...].astype(BF16)
        grb, gib = gr_ref[0:rows, :].astype(BF16), gi_ref[0:rows, :].astype(BF16)
        dbr_ref[...] = lax.dot_general(ub, grb, _DIMS['tn'], preferred_element_type=F32)
        dbi_ref[...] = lax.dot_general(ub, gib, _DIMS['tn'], preferred_element_type=F32)
        contrib = (lax.dot_general(grb, br_ref[...], _DIMS['nt'], preferred_element_type=F32)
                   + lax.dot_general(gib, bi_ref[...], _DIMS['nt'], preferred_element_type=F32))

        @pl.when(j == 0)
        def _():
            acc_ref[...] = dd_ref[...] * dy_ref[...] + contrib

        @pl.when(j > 0)
        def _():
            acc_ref[...] += contrib

        @pl.when(j == STATE_BLOCKS - 1)
        def _():
            du_ref[...] = acc_ref[...]

    flat = pltpu.VMEM((rows, LANES), F32)
    big = pltpu.VMEM((nck * SEG, LANES), F32)
    small = pltpu.VMEM((nck, LANES), F32)
    bshape = jax.ShapeDtypeStruct((nb, LANES, LANES * STATE_BLOCKS), F32)
    cshape = jax.ShapeDtypeStruct((nb, LANES * STATE_BLOCKS, LANES), F32)
    ashape = jax.ShapeDtypeStruct((1, nb * LANES * STATE_BLOCKS), F32)
    return pl.pallas_call(
        body, name=name, grid=(nsteps,),
        in_specs=[chan, chan, chan, bmat, bmat, cmat, cmat, avec, avec, dvec],
        out_specs=[chan, bmat, bmat, cmat, cmat, avec, avec, dvec],
        out_shape=[jax.ShapeDtypeStruct((rows, e), F32), bshape, bshape, cshape, cshape, ashape, ashape,
                   jax.ShapeDtypeStruct((1, e), F32)],
        scratch_shapes=[big, big, big, big, small, small, small, small, flat, flat],
        compiler_params=_params("arbitrary"))(u, y, dgy, bre, bim, crt, cit, ar, ai, dd)


def _to_blockdiag_b(bbar, nb):
    eye = jnp.eye(GROUPS_PER_BLOCK, dtype=bbar.dtype)
    t = jnp.einsum('bgpc,gh->bgchp', bbar.reshape(nb, GROUPS_PER_BLOCK, SSM_STATE, SSM_GROUP), eye)
    return t.reshape(nb, LANES, GROUPS_PER_BLOCK * SSM_STATE)


def _from_blockdiag_b(dmat, nb):
    eye = jnp.eye(GROUPS_PER_BLOCK, dtype=dmat.dtype)
    t = dmat.reshape(nb, GROUPS_PER_BLOCK, SSM_GROUP, GROUPS_PER_BLOCK, SSM_STATE)
    return jnp.einsum('bgchp,gh->bgpc', t, eye).reshape(nb * GROUPS_PER_BLOCK, SSM_STATE, SSM_GROUP)


def _to_blockdiag_ct(c, nb):
    eye = jnp.eye(GROUPS_PER_BLOCK, dtype=c.dtype)
    t = jnp.einsum('bgop,gh->bgpho', c.reshape(nb, GROUPS_PER_BLOCK, SSM_GROUP, SSM_STATE), eye)
    return t.reshape(nb, GROUPS_PER_BLOCK * SSM_STATE, LANES)


def _from_blockdiag_ct(dmat, nb):
    eye = jnp.eye(GROUPS_PER_BLOCK, dtype=dmat.dtype)
    t = dmat.reshape(nb, GROUPS_PER_BLOCK, SSM_STATE, GROUPS_PER_BLOCK, SSM_GROUP)
    return jnp.einsum('bgpho,gh->bgop', t, eye).reshape(nb * GROUPS_PER_BLOCK, SSM_GROUP, SSM_STATE)


def _half_specs(kind, rdim, cdim, tr, tc):
    nr, nc = rdim // tr, cdim // tc
    if kind == 'col':
        nat = pl.BlockSpec((None, tr, tc), lambda c, l, rb, cb: (l, c * nr + rb, cb))
    else:
        nat = pl.BlockSpec((None, tr, tc), lambda c, l, rb, cb: (l, rb, c * nc + cb))
    half = pl.BlockSpec((None, None, tr, tc), lambda c, l, rb, cb: (c, l, rb, cb))
    return nat, half


def _my_chip():
    return 2 * lax.axis_index("x") + lax.axis_index("y")


def _cast_halves(name, w, kind):
    layers = w.shape[0]
    rdim, cdim = _half_shape(kind, w.shape)
    tr, tc = _tile(rdim, 512, 16), _tile(cdim, 1408, LANES)
    nat, _ = _half_specs(kind, rdim, cdim, tr, tc)
    slot = pl.BlockSpec((None, None, None, tr, tc), lambda c, l, rb, cb: (_my_chip(), c, l, rb, cb))

    def body(w_ref, o_ref):
        o_ref[...] = w_ref[...].astype(o_ref.dtype)

    return pl.pallas_call(
        body, name=name, grid=(2, layers, rdim // tr, cdim // tc), in_specs=[nat], out_specs=slot,
        out_shape=jax.ShapeDtypeStruct((4, 2, layers, rdim, cdim), BF16),
        compiler_params=_params("parallel", "parallel", "parallel", "parallel"))(w)


def _adam_math(w, g, m, v):
    m = ADAM_B1 * m + (1.0 - ADAM_B1) * g
    v = ADAM_B2 * v + (1.0 - ADAM_B2) * (g * g)
    m_hat = m / (1.0 - ADAM_B1 ** ADAM_STEP)
    v_hat = v / (1.0 - ADAM_B2 ** ADAM_STEP)
    delta = -ADAM_LR * (m_hat / (jnp.sqrt(v_hat) + ADAM_EPS) + ADAM_WD * w)
    return delta, m, v


def _adam_big(name, w, m, v, gfull, kind):
    layers = w.shape[0]
    rdim, cdim = _half_shape(kind, w.shape)
    tr, tc = _tile(rdim, 256, 8), _tile(cdim, 1408, LANES)
    nat, half = _half_specs(kind, rdim, cdim, tr, tc)

    def body(w_ref, m_ref, v_ref, g_ref, go_ref, d_ref, mo_ref, vo_ref):
        g = g_ref[...]
        go_ref[...] = g
        d_ref[...], mo_ref[...], vo_ref[...] = _adam_math(w_ref[...], g, m_ref[...], v_ref[...])

    shape = jax.ShapeDtypeStruct(w.shape, F32)
    return pl.pallas_call(
        body, name=name, grid=(2, layers, rdim // tr, cdim // tc), in_specs=[nat, nat, nat, half],
        out_specs=[nat, nat, nat, nat], out_shape=[shape, shape, shape, shape],
        compiler_params=_params("parallel", "parallel", "parallel", "parallel"))(w, m, v, gfull)


def _adam_small(w, m, v, g):
    rows = w.shape[0]
    tr = _tile(rows, 512, 8)
    spec = pl.BlockSpec((tr, LANES), lambda i: (i, 0))

    def body(w_ref, m_ref, v_ref, g_ref, d_ref, mo_ref, vo_ref):
        d_ref[...], mo_ref[...], vo_ref[...] = _adam_math(w_ref[...], g_ref[...], m_ref[...], v_ref[...])

    shape = jax.ShapeDtypeStruct(w.shape, F32)
    return pl.pallas_call(body, name="adam_small", grid=(rows // tr,), in_specs=[spec] * 4, out_specs=[spec] * 3,
                          out_shape=[shape] * 3, compiler_params=_params("parallel"))(w, m, v, g)


def _add2(name, part, got):
    cdim = part.shape[-1]
    a2, b2 = part.reshape(4, 2, -1, cdim), got.reshape(4, -1, cdim)
    rows = b2.shape[1]
    tr, tc = _tile(rows, 512, 16), _tile(cdim, 1408, LANES)
    mine = pl.BlockSpec((None, None, tr, tc), lambda k, i, j: (k, lax.axis_index("c"), i, j))
    spec = pl.BlockSpec((None, tr, tc), lambda k, i, j: (k, i, j))

    def body(a_ref, b_ref, o_ref):
        o_ref[...] = (a_ref[...].astype(F32) + b_ref[...].astype(F32)).astype(o_ref.dtype)

    out = pl.pallas_call(
        body, name=name, grid=(4, rows // tr, cdim // tc), in_specs=[mine, spec], out_specs=spec,
        out_shape=jax.ShapeDtypeStruct(b2.shape, BF16),
        compiler_params=_params("parallel", "parallel", "parallel"))(a2, b2)
    return out.reshape(got.shape)


def _add4(name, sums, recv):
    cdim = sums.shape[-1]
    s2 = sums.reshape(4, -1, cdim)
    r3 = recv.reshape(3, -1, cdim)
    rows = s2.shape[1]
    tr, tc = _tile(rows, 512, 16), _tile(cdim, 1408, LANES)
    own = pl.BlockSpec((None, tr, tc), lambda i, j: (_my_chip(), i, j))
    rspec = lambda k: pl.BlockSpec((None, tr, tc), lambda i, j: (k, i, j))
    slot = pl.BlockSpec((None, tr, tc), lambda i, j: (lax.axis_index("c"), i, j))

    def body(o_ref, x_ref, y_ref, d_ref, out_ref):
        out_ref[...] = ((o_ref[...].astype(F32) + d_ref[...].astype(F32))
                        + (x_ref[...].astype(F32) + y_ref[...].astype(F32)))

    out = pl.pallas_call(
        body, name=name, grid=(rows // tr, cdim // tc), in_specs=[own, rspec(0), rspec(1), rspec(2)], out_specs=slot,
        out_shape=jax.ShapeDtypeStruct((2, rows, cdim), F32),
        compiler_params=_params("parallel", "parallel"))(s2, r3, r3, r3)
    return out.reshape(2, *sums.shape[1:])


def _place():
    x, y, c = lax.axis_index("x"), lax.axis_index("y"), lax.axis_index("c")
    chips = [(1 - x, y), (x, 1 - y), (1 - x, 1 - y)]
    return x, y, c, chips


ANY = pl.BlockSpec(memory_space=pl.ANY)


def _remote(src, dst, send, recv, to):
    return pltpu.make_async_remote_copy(src_ref=src, dst_ref=dst, send_sem=send, recv_sem=recv, device_id=to,
                                        device_id_type=MESH)


def _pieces(src, dst, bands):
    lead, rows = src.shape[:-2], src.shape[-2]
    band = rows // bands
    out = []
    for idx in itertools.product(*[range(dim) for dim in lead]):
        for q in range(bands):
            sl = (*idx, pl.ds(q * band, band))
            out.append((src.at[sl], dst.at[sl]))
    return out


def _allgather_big(slots):
    n = len(slots)

    def body(*refs):
        outs = refs[n:2 * n]
        send, recv = refs[2 * n:]
        x, y, c, chips = _place()
        kme = 2 * x + y
        sib = (x, y, 1 - c)
        for a in range(n):
            for r, chip in enumerate(chips):
                for s, d in _pieces(outs[a].at[kme, c], outs[a].at[kme, c], 2):
                    _remote(s, d, send.at[6 * a + r], recv.at[6 * a + r], (*chip, c)).start()
        for a in range(n):
            for r, chip in enumerate(chips):
                kp = 2 * chip[0] + chip[1]
                _remote(outs[a].at[kp, c], outs[a].at[kp, c], send.at[6 * a + r], recv.at[6 * a + r],
                        (*chip, c)).wait_recv()
                for s, d in _pieces(outs[a].at[kp, c], outs[a].at[kp, c], 2):
                    _remote(s, d, send.at[6 * a + 3 + r], recv.at[6 * a + 3 + r], sib).start()
        for a in range(n):
            for r, chip in enumerate(chips):
                kp = 2 * chip[0] + chip[1]
                _remote(outs[a].at[kp, 1 - c], outs[a].at[kp, 1 - c], send.at[6 * a + 3 + r], recv.at[6 * a + 3 + r],
                        sib).wait_recv()
        for a in range(n):
            for r, chip in enumerate(chips):
                kp = 2 * chip[0] + chip[1]
                _remote(outs[a].at[kme, c], outs[a].at[kme, c], send.at[6 * a + r], recv.at[6 * a + r],
                        (*chip, c)).wait_send()
                _remote(outs[a].at[kp, c], outs[a].at[kp, c], send.at[6 * a + 3 + r], recv.at[6 * a + 3 + r],
                        sib).wait_send()

    return pl.pallas_call(
        body, name="allgather_weights", in_specs=[ANY] * n, out_specs=[ANY] * n,
        out_shape=[jax.ShapeDtypeStruct(s.shape, s.dtype) for s in slots],
        input_output_aliases={a: a for a in range(n)},
        scratch_shapes=[pltpu.SemaphoreType.DMA((6 * n,)), pltpu.SemaphoreType.DMA((6 * n,))])(*slots)


def _allgather_small(shards):
    n = len(shards)

    def body(*refs):
        ins, outs = refs[:n], refs[n:2 * n]
        send, recv, loc = refs[2 * n:]
        x, y, c, chips = _place()
        kme = 2 * x + y
        local = [pltpu.make_async_copy(ins[a], outs[a].at[kme], loc.at[a]) for a in range(n)]
        for cp in local:
            cp.start()
        cps = [_remote(ins[a], outs[a].at[kme], send.at[3 * a + r], recv.at[3 * a + r], (*chip, c))
               for a in range(n) for r, chip in enumerate(chips)]
        for cp in cps:
            cp.start()
        for a in range(n):
            for r, chip in enumerate(chips):
                kp = 2 * chip[0] + chip[1]
                _remote(ins[a], outs[a].at[kp], send.at[3 * a + r], recv.at[3 * a + r], (*chip, c)).wait_recv()
        for cp in cps:
            cp.wait_send()
        for cp in local:
            cp.wait()

    return pl.pallas_call(
        body, name="allgather_small", in_specs=[ANY] * n, out_specs=[ANY] * n,
        out_shape=[jax.ShapeDtypeStruct((4, *s.shape), s.dtype) for s in shards],
        scratch_shapes=[pltpu.SemaphoreType.DMA((3 * n,)), pltpu.SemaphoreType.DMA((3 * n,)),
                        pltpu.SemaphoreType.DMA((n,))])(*shards)


def _swap_halves(parts):
    n = len(parts)

    def body(*refs):
        ins, got = refs[:n], refs[n:2 * n]
        send, recv = refs[2 * n:]
        x, y, c, _ = _place()
        sib = (x, y, 1 - c)
        for a in range(n):
            for s, d in _pieces(ins[a].at[:, 1 - c], got[a], 1):
                _remote(s, d, send.at[a], recv.at[a], sib).start()
        for a in range(n):
            _remote(ins[a].at[:, 1 - c], got[a], send.at[a], recv.at[a], sib).wait()

    return pl.pallas_call(
        body, name="grad_swap_halves", in_specs=[ANY] * n, out_specs=[ANY] * n,
        out_shape=[jax.ShapeDtypeStruct((4, *p.shape[2:]), p.dtype) for p in parts],
        scratch_shapes=[pltpu.SemaphoreType.DMA((n,)), pltpu.SemaphoreType.DMA((n,))])(*parts)


def _scatter_chips(sums):
    n = len(sums)

    def body(*refs):
        ins, got = refs[:n], refs[n:2 * n]
        send, recv = refs[2 * n:]
        x, y, c, chips = _place()
        for a in range(n):
            for r, chip in enumerate(chips):
                for s, d in _pieces(ins[a].at[2 * chip[0] + chip[1]], got[a].at[r], 2):
                    _remote(s, d, send.at[3 * a + r], recv.at[3 * a + r], (*chip, c)).start()
        for a in range(n):
            for r, chip in enumerate(chips):
                _remote(ins[a].at[2 * chip[0] + chip[1]], got[a].at[r], send.at[3 * a + r], recv.at[3 * a + r],
                        (*chip, c)).wait()

    return pl.pallas_call(
        body, name="grad_scatter_chips", in_specs=[ANY] * n, out_specs=[ANY] * n,
        out_shape=[jax.ShapeDtypeStruct((3, *s.shape[1:]), s.dtype) for s in sums],
        scratch_shapes=[pltpu.SemaphoreType.DMA((3 * n,)), pltpu.SemaphoreType.DMA((3 * n,))])(*sums)


def _join_halves(totals):
    n = len(totals)

    def body(*refs):
        outs = refs[n:2 * n]
        send, recv = refs[2 * n:]
        x, y, c, _ = _place()
        sib = (x, y, 1 - c)
        for a in range(n):
            for s, d in _pieces(outs[a].at[c], outs[a].at[c], 4):
                _remote(s, d, send.at[a], recv.at[a], sib).start()
        for a in range(n):
            _remote(outs[a].at[1 - c], outs[a].at[1 - c], send.at[a], recv.at[a], sib).wait_recv()
            _remote(outs[a].at[c], outs[a].at[c], send.at[a], recv.at[a], sib).wait_send()

    return pl.pallas_call(
        body, name="grad_join_halves", in_specs=[ANY] * n, out_specs=[ANY] * n,
        out_shape=[jax.ShapeDtypeStruct(t.shape, t.dtype) for t in totals],
        input_output_aliases={a: a for a in range(n)},
        scratch_shapes=[pltpu.SemaphoreType.DMA((n,)), pltpu.SemaphoreType.DMA((n,))])(*totals)


def _allreduce_small(packed):
    rows = packed.shape[0]
    half = rows // 2

    def body(in_ref, out_ref, q_ref, s_ref, t_ref, send, recv):
        x, y, c, chips = _place()
        sib = (x, y, 1 - c)
        mine = pl.ds(pl.multiple_of(c * half, 8), half)
        theirs = pl.ds(pl.multiple_of((1 - c) * half, 8), half)
        first = _remote(in_ref.at[theirs], q_ref, send.at[0], recv.at[0], sib)
        first.start()
        first.wait()
        s_ref[...] = in_ref[mine, :] + q_ref[...]
        cps = [_remote(s_ref, t_ref.at[r], send.at[1 + r], recv.at[1 + r], (*chip, c)) for r, chip in enumerate(chips)]
        for cp in cps:
            cp.start()
        for cp in cps:
            cp.wait()
        out_ref[mine, :] = (s_ref[...] + t_ref[2]) + (t_ref[0] + t_ref[1])
        last = _remote(out_ref.at[mine], out_ref.at[mine], send.at[4], recv.at[4], sib)
        last.start()
        _remote(out_ref.at[theirs], out_ref.at[theirs], send.at[4], recv.at[4], sib).wait_recv()
        last.wait_send()

    vm = pl.BlockSpec(memory_space=pltpu.VMEM)
    return pl.pallas_call(
        body, name="allreduce_small", in_specs=[vm], out_specs=vm, out_shape=jax.ShapeDtypeStruct(packed.shape, F32),
        scratch_shapes=[pltpu.VMEM((half, LANES), F32), pltpu.VMEM((half, LANES), F32),
                        pltpu.VMEM((3, half, LANES), F32), pltpu.SemaphoreType.DMA((5,)), pltpu.SemaphoreType.DMA((5,))],
        compiler_params=pltpu.CompilerParams(vmem_limit_bytes=VMEM_LIMIT))(packed)


PACK_ROWS = 16


def _pack(arrs):
    parts, total = [], 0
    for a in arrs:
        flat = a.reshape(-1)
        rows = -(-flat.shape[0] // (LANES * PACK_ROWS)) * PACK_ROWS
        parts.append(jnp.pad(flat, (0, rows * LANES - flat.shape[0])).reshape(rows, LANES))
        total += rows
    return jnp.concatenate(parts, axis=0)


def _unpack(packed, shapes):
    out, row = [], 0
    for shp in shapes:
        size = math.prod(shp)
        rows = -(-size // (LANES * PACK_ROWS)) * PACK_ROWS
        out.append(packed[row:row + rows].reshape(-1)[:size].reshape(shp))
        row += rows
    return out


def kernel(x, norm_mix_g, norm_ffn_g, a_w_in, a_g_v, a_w_s, a_b_s, a_w_out, b_w_in, b_a_re, b_a_im, b_log_dt, b_b_re, b_b_im, b_c_re, b_c_im, b_d, b_w_glu, f_w_up, f_conv_w, f_conv_b, f_w_down, final_g, loss_target, m_norm_mix_g, m_norm_ffn_g, m_a_w_in, m_a_g_v, m_a_w_s, m_a_b_s, m_a_w_out, m_b_w_in, m_b_a_re, m_b_a_im, m_b_log_dt, m_b_b_re, m_b_b_im, m_b_c_re, m_b_c_im, m_b_d, m_b_w_glu, m_f_w_up, m_f_conv_w, m_f_conv_b, m_f_w_down, m_final_g, v_norm_mix_g, v_norm_ffn_g, v_a_w_in, v_a_g_v, v_a_w_s, v_a_b_s, v_a_w_out, v_b_w_in, v_b_a_re, v_b_a_im, v_b_log_dt, v_b_b_re, v_b_b_im, v_b_c_re, v_b_c_im, v_b_d, v_b_w_glu, v_f_w_up, v_f_conv_w, v_f_conv_b, v_f_w_down, v_final_g):
    w = dict(norm_mix_g=norm_mix_g, norm_ffn_g=norm_ffn_g, a_w_in=a_w_in, a_g_v=a_g_v, a_w_s=a_w_s, a_b_s=a_b_s,
             a_w_out=a_w_out, b_w_in=b_w_in, b_a_re=b_a_re, b_a_im=b_a_im, b_log_dt=b_log_dt, b_b_re=b_b_re,
             b_b_im=b_b_im, b_c_re=b_c_re, b_c_im=b_c_im, b_d=b_d, b_w_glu=b_w_glu, f_w_up=f_w_up, f_conv_w=f_conv_w,
             f_conv_b=f_conv_b, f_w_down=f_w_down, final_g=final_g)
    mom = dict(norm_mix_g=m_norm_mix_g, norm_ffn_g=m_norm_ffn_g, a_w_in=m_a_w_in, a_g_v=m_a_g_v, a_w_s=m_a_w_s,
               a_b_s=m_a_b_s, a_w_out=m_a_w_out, b_w_in=m_b_w_in, b_a_re=m_b_a_re, b_a_im=m_b_a_im,
               b_log_dt=m_b_log_dt, b_b_re=m_b_b_re, b_b_im=m_b_b_im, b_c_re=m_b_c_re, b_c_im=m_b_c_im, b_d=m_b_d,
               b_w_glu=m_b_w_glu, f_w_up=m_f_w_up, f_conv_w=m_f_conv_w, f_conv_b=m_f_conv_b, f_w_down=m_f_w_down,
               final_g=m_final_g)
    var = dict(norm_mix_g=v_norm_mix_g, norm_ffn_g=v_norm_ffn_g, a_w_in=v_a_w_in, a_g_v=v_a_g_v, a_w_s=v_a_w_s,
               a_b_s=v_a_b_s, a_w_out=v_a_w_out, b_w_in=v_b_w_in, b_a_re=v_b_a_re, b_a_im=v_b_a_im,
               b_log_dt=v_b_log_dt, b_b_re=v_b_b_re, b_b_im=v_b_b_im, b_c_re=v_b_c_re, b_c_im=v_b_c_im, b_d=v_b_d,
               b_w_glu=v_b_w_glu, f_w_up=v_f_w_up, f_conv_w=v_f_conv_w, f_conv_b=v_f_conv_b, f_w_down=v_f_w_down,
               final_g=v_final_g)

    rows, d = x.shape[1], x.shape[2]
    depth = norm_mix_g.shape[0]
    kchip = 2 * lax.axis_index("x") + lax.axis_index("y")
    big_names = list(BIG)
    dims = {n: _full_dims(BIG[n], w[n].shape) for n in big_names}

    slots = [_cast_halves("cast_" + n, w[n], BIG[n]) for n in big_names]
    gathered = dict(zip(big_names, _allgather_big(slots)))
    bd_all, cw_all = _allgather_small([b_d, f_conv_w.reshape(-1, f_conv_w.shape[-1])])
    bd_full = jnp.swapaxes(bd_all, 0, 1).reshape(b_d.shape[0], -1)
    cw_full = jnp.transpose(cw_all.reshape(4, *f_conv_w.shape), (1, 2, 0, 3)).reshape(depth, f_conv_w.shape[1], -1)

    pgrad = {n: lax.empty(s.shape, BF16) for n, s in zip(big_names, slots)}
    sgrad = {}

    def mm(name, a, wn, layer, out_dtype, residual=None):
        return _mm_x_w(name, a, gathered[wn], BIG[wn], *dims[wn], layer, out_dtype, residual)

    def mm_t(name, dy, wn, layer, out_dtype):
        return _mm_dy_wt(name, dy, gathered[wn], BIG[wn], *dims[wn], layer, out_dtype)

    def mm_g(name, xa, dy, wn, layer):
        pgrad[wn] = _mm_xt_dy(name, xa, dy, pgrad[wn], BIG[wn], *dims[wn], layer)

    e = d
    nb = e // LANES
    heads = e // SGU_GROUP
    h = x[0]
    saved = []
    for i in range(depth):
        j = i // 2
        gm = norm_mix_g[i:i + 1]
        hn = _rms_fwd("rms_mix_fwd", h, gm)
        if i % 2 == 0:
            pre = mm("sgu_in", hn, 'a_w_in', j, BF16)
            bsx = jnp.broadcast_to(a_b_s[j][:, :, None], (heads, CHUNK, LANES))
            us = _sgu_mix_fwd("sgu_mix_fwd", pre, a_g_v[j:j + 1], a_w_s[j], bsx)
            h_mid = mm("sgu_out", us, 'a_w_out', j, F32, residual=h)
            mix = dict(h=h, hn=hn, pre=pre, us=us, bsx=bsx)
        else:
            groups = b_a_re.shape[1]
            rep = lambda t: jnp.repeat(t, SSM_GROUP, axis=1)
            lr, li = b_a_re[j], b_a_im[j]
            ldt = jnp.broadcast_to(b_log_dt[j][:, None], lr.shape)
            bflat = lambda t: t.reshape(groups, SSM_STATE * SSM_GROUP)
            disc_in = (lr, li, ldt, rep(lr), rep(li), rep(ldt), bflat(b_b_re[j]), bflat(b_b_im[j]))
            abr, abi, bbr, bbi = _s5_disc("s5_disc", *disc_in)
            shape_b = (groups, SSM_STATE, SSM_GROUP)
            bre = _to_blockdiag_b(bbr.reshape(shape_b), nb).astype(BF16)
            bim = _to_blockdiag_b(bbi.reshape(shape_b), nb).astype(BF16)
            crt = _to_blockdiag_ct(b_c_re[j], nb).astype(BF16)
            cit = _to_blockdiag_ct(b_c_im[j], nb).astype(BF16)
            ar_row, ai_row = abr.reshape(1, -1), abi.reshape(1, -1)
            dd = bd_full[j:j + 1]
            u = mm("s5_in", hn, 'b_w_in', j, F32)
            yv, gy = _s5_fwd("s5_fwd", u, bre, bim, crt, cit, ar_row, ai_row, dd)
            gg = mm("s5_glu", gy, 'b_w_glu', j, BF16)
            h_mid = _glu_fwd("glu_fwd", gg, h)
            mix = dict(h=h, hn=hn, u=u, y=yv, gy=gy, gg=gg, disc_in=disc_in, mats=(bre, bim, crt, cit, ar_row, ai_row, dd))
        gf = norm_ffn_g[i:i + 1]
        hn2 = _rms_fwd("rms_ffn_fwd", h_mid, gf)
        z = mm("ffn_up", hn2, 'f_w_up', i, BF16)
        cw, cb = cw_full[i], f_conv_b[i:i + 1]
        act = _ffn_act_fwd("ffn_act_fwd", z, cw, cb)
        h_out = mm("ffn_down", act, 'f_w_down', i, F32, residual=h_mid)
        saved.append((mix, dict(h=h_mid, hn=hn2, z=z, act=act, cw=cw, cb=cb)))
        h = h_out

    dh, g_final, loss_vec = _loss_head(h, final_g.reshape(1, d), loss_target[0])
    loss = lax.psum(jnp.sum(loss_vec), ("x", "y", "c"))
    sgrad['final_g'] = g_final.reshape(d)

    g_mix, g_ffn = [None] * depth, [None] * depth
    g_cw, g_cb = [None] * depth, [None] * depth
    sg = {k: [None] * (depth // 2) for k in ('a_g_v', 'a_w_s', 'a_b_s')}
    bg = {k: [None] * (depth // 2) for k in ('b_a_re', 'b_a_im', 'b_log_dt', 'b_b_re', 'b_b_im', 'b_c_re', 'b_c_im', 'b_d')}
    for i in reversed(range(depth)):
        j = i // 2
        mix, ffn = saved[i]
        d_act = mm_t("ffn_down_dx", dh, 'f_w_down', i, BF16)
        mm_g("ffn_down_dw", ffn['act'], dh, 'f_w_down', i)
        dacc, g_cw[i], g_cb[i] = _ffn_act_bwd("ffn_act_bwd", ffn['z'], d_act, ffn['cw'], ffn['cb'])
        dz = _conv_bwd("ffn_conv_bwd", dacc, ffn['cw'])
        mm_g("ffn_up_dw", ffn['hn'], dz, 'f_w_up', i)
        dhn = mm_t("ffn_up_dx", dz, 'f_w_up', i, F32)
        dh, g_ffn[i] = _rms_bwd("rms_ffn_bwd", ffn['h'], norm_ffn_g[i:i + 1], dhn, dh)
        if i % 2 == 0:
            dus = mm_t("sgu_out_dx", dh, 'a_w_out', j, BF16)
            mm_g("sgu_out_dw", mix['us'], dh, 'a_w_out', j)
            dpre, dws, dbs, dgv = _sgu_mix_bwd("sgu_mix_bwd", mix['pre'], dus, a_g_v[j:j + 1], a_w_s[j], mix['bsx'])
            sg['a_w_s'][j], sg['a_b_s'][j], sg['a_g_v'][j] = dws, dbs[:, :, 0], dgv[0]
            mm_g("sgu_in_dw", mix['hn'], dpre, 'a_w_in', j)
            dhn = mm_t("sgu_in_dx", dpre, 'a_w_in', j, F32)
        else:
            dgg = _glu_bwd("glu_bwd", mix['gg'], dh)
            mm_g("s5_glu_dw", mix['gy'], dgg, 'b_w_glu', j)
            dgy = mm_t("s5_glu_dx", dgg, 'b_w_glu', j, BF16)
            du, dbr, dbi, dcr, dci, dar, dai, ddd = _s5_bwd("s5_bwd", mix['u'], mix['y'], dgy, *mix['mats'])
            groups = b_a_re.shape[1]
            flat = lambda t: _from_blockdiag_b(t, nb).reshape(groups, SSM_STATE * SSM_GROUP)
            sel = jnp.repeat(jnp.eye(SSM_STATE, dtype=F32), SSM_GROUP, axis=0)
            dlr, dli, dldt, dbre, dbim = _s5_disc_bwd(
                "s5_disc_bwd", *mix['disc_in'], dar.reshape(groups, SSM_STATE), dai.reshape(groups, SSM_STATE),
                flat(dbr), flat(dbi), sel)
            bg['b_a_re'][j], bg['b_a_im'][j], bg['b_log_dt'][j] = dlr, dli, dldt[:, 0]
            bg['b_b_re'][j] = dbre.reshape(groups, SSM_STATE, SSM_GROUP)
            bg['b_b_im'][j] = dbim.reshape(groups, SSM_STATE, SSM_GROUP)
            bg['b_c_re'][j], bg['b_c_im'][j] = _from_blockdiag_ct(dcr, nb), _from_blockdiag_ct(dci, nb)
            bg['b_d'][j] = ddd[0]
            mm_g("s5_in_dw", mix['hn'], du, 'b_w_in', j)
            dhn = mm_t("s5_in_dx", du, 'b_w_in', j, F32)
        dh, g_mix[i] = _rms_bwd("rms_mix_bwd", mix['h'], norm_mix_g[i:i + 1], dhn, dh)
    grad_x = dh[None]

    sgrad['norm_mix_g'] = jnp.concatenate(g_mix, axis=0)
    sgrad['norm_ffn_g'] = jnp.concatenate(g_ffn, axis=0)
    sgrad['f_conv_w'] = jnp.stack(g_cw)
    sgrad['f_conv_b'] = jnp.concatenate(g_cb, axis=0)
    for k, v_ in list(sg.items()) + list(bg.items()):
        sgrad[k] = jnp.stack(v_)

    total = _allreduce_small(_pack([sgrad[n] for n in SMALL]))
    full_shapes = [sgrad[n].shape for n in SMALL]
    gsmall = dict(zip(SMALL, _unpack(total, full_shapes)))
    for n, axis in CHIP_SHARDED_SMALL.items():
        width = w[n].shape[axis]
        gsmall[n] = lax.dynamic_slice_in_dim(gsmall[n], kchip * width, width, axis=axis)
    pk = lambda t: _pack([t[n] for n in SMALL])
    gpacked = pk(gsmall)
    dpk, mpk, vpk = _adam_small(pk(w), pk(mom), pk(var), gpacked)
    shard_shapes = [w[n].shape for n in SMALL]
    out_g = dict(gsmall)
    out_d = dict(zip(SMALL, _unpack(dpk, shard_shapes)))
    out_m = dict(zip(SMALL, _unpack(mpk, shard_shapes)))
    out_v = dict(zip(SMALL, _unpack(vpk, shard_shapes)))

    parts = [pgrad[n] for n in big_names]
    got = _swap_halves(parts)
    sums = [_add2("grad_chip_sum", p, g) for p, g in zip(parts, got)]
    recvd = _scatter_chips(sums)
    gfull = _join_halves([_add4("grad_total", s, r) for s, r in zip(sums, recvd)])
    for n, gf_ in zip(big_names, gfull):
        out_g[n], out_d[n], out_m[n], out_v[n] = _adam_big("adam_" + n, w[n], mom[n], var[n], gf_, BIG[n])

    return (loss, grad_x, *[out_g[n] for n in W_NAMES], *[out_d[n] for n in W_NAMES],
            *[out_m[n] for n in W_NAMES], *[out_v[n] for n in W_NAMES])
```

```python
import functools
import itertools
import math

import jax
import jax.numpy as jnp
from jax import lax
from jax.experimental import pallas as pl
from jax.experimental.pallas import tpu as pltpu

F32, BF16 = jnp.float32, jnp.bfloat16
MESH = pl.DeviceIdType.MESH

CHUNK = 128
SEG = CHUNK + 4
SGU_GROUP = 128
SSM_GROUP = 16
SSM_STATE = 64
EPS = 1e-6
LANES = 128
GROUPS_PER_BLOCK = LANES // SSM_GROUP
STATE_BLOCKS = SSM_STATE // SSM_GROUP
VMEM_LIMIT = 52 * 1024 * 1024

ADAM_LR, ADAM_B1, ADAM_B2, ADAM_EPS, ADAM_WD, ADAM_STEP = 0.001, 0.9, 0.999, 1e-08, 0.01, 10

W_NAMES = ['norm_mix_g', 'norm_ffn_g', 'a_w_in', 'a_g_v', 'a_w_s', 'a_b_s', 'a_w_out', 'b_w_in', 'b_a_re', 'b_a_im',
           'b_log_dt', 'b_b_re', 'b_b_im', 'b_c_re', 'b_c_im', 'b_d', 'b_w_glu', 'f_w_up', 'f_conv_w', 'f_conv_b',
           'f_w_down', 'final_g']
BIG = {'a_w_in': 'col', 'a_w_out': 'row', 'b_w_in': 'row', 'b_w_glu': 'col', 'f_w_up': 'col', 'f_w_down': 'row'}
SMALL = [n for n in W_NAMES if n not in BIG]
CHIP_SHARDED_SMALL = {'b_d': 1, 'f_conv_w': 2}


def _tile(n, pref, align):
    t = min(n, pref)
    t -= t % align
    while t >= align:
        if n % t == 0:
            return t
        t -= align
    return n


def _params(*sem):
    return pltpu.CompilerParams(dimension_semantics=sem, vmem_limit_bytes=VMEM_LIMIT)


def _gelu(x):
    c = math.sqrt(2.0 / math.pi)
    return 0.5 * x * (1.0 + jnp.tanh(c * (x + 0.044715 * x * x * x)))


def _gelu_grad(x):
    c = math.sqrt(2.0 / math.pi)
    t = jnp.tanh(c * (x + 0.044715 * x * x * x))
    return 0.5 * (1.0 + t) + 0.5 * x * (1.0 - t * t) * c * (1.0 + 3.0 * 0.044715 * x * x)


def _half_shape(kind, shard_shape):
    _, r, c = shard_shape
    return (r // 2, c) if kind == 'col' else (r, c // 2)


def _full_dims(kind, shard_shape):
    _, r, c = shard_shape
    return (r, 4 * c) if kind == 'col' else (4 * r, c)


def _gspec(kind, kdim, ndim, tr, tc, rc):
    if kind == 'col':
        nr, nc = (kdim // 2) // tr, (ndim // 4) // tc

        def imap(*g):
            rb, cb = rc(*g)
            return (cb // nc, rb // nr, rb % nr, cb % nc)
    else:
        nr, nc = (kdim // 4) // tr, (ndim // 2) // tc

        def imap(*g):
            rb, cb = rc(*g)
            return (rb // nr, cb // nc, rb % nr, cb % nc)
    return pl.BlockSpec((None, None, tr, tc), imap)


def _act_spec(rows_blk, cols_blk, ncol_half, at):
    if ncol_half is None:
        return pl.BlockSpec((rows_blk, cols_blk), at)

    def imap(*g):
        rb, cb = at(*g)
        return (cb // ncol_half, rb, cb % ncol_half)
    return pl.BlockSpec((None, rows_blk, cols_blk), imap)


def _wtiles(kind, kdim, ndim):
    if kind == 'col':
        return _tile(kdim // 2, 1024, LANES), _tile(ndim // 4, 1408, LANES)
    return _tile(kdim // 4, 1408, LANES), _tile(ndim // 2, 1024, LANES)


_DIMS = {'nn': (((1,), (0,)), ((), ())), 'nt': (((1,), (1,)), ((), ())), 'tn': (((0,), (0,)), ((), ()))}


def _matmul(name, mode, a, b, grid, a_spec, b_spec, out_shape, out_spec, acc_shape, extras=(), extra_specs=(),
            epilogue=None, aliases=None):
    nk = grid[2]
    dims = _DIMS[mode]
    n_extra = len(extras)

    def body(a_ref, b_ref, *rest):
        extra_refs, o_ref = rest[:n_extra], rest[n_extra]
        prod = lax.dot_general(a_ref[...].astype(BF16), b_ref[...].astype(BF16), dims, preferred_element_type=F32)

        def finish(r):
            if epilogue is not None:
                r = epilogue(r, *[e[...] for e in extra_refs])
            o_ref[...] = r.astype(o_ref.dtype)

        if nk == 1:
            finish(prod)
            return
        acc_ref = rest[n_extra + 1]
        kk = pl.program_id(2)

        @pl.when(kk == 0)
        def _():
            acc_ref[...] = prod

        @pl.when(kk > 0)
        def _():
            acc_ref[...] += prod

        @pl.when(kk == nk - 1)
        def _():
            finish(acc_ref[...])

    scratch = [pltpu.VMEM(acc_shape, F32)] if nk > 1 else []
    return pl.pallas_call(
        body, name=name, grid=grid, in_specs=[a_spec, b_spec, *extra_specs], out_specs=out_spec, out_shape=out_shape,
        scratch_shapes=scratch, input_output_aliases=aliases or {},
        compiler_params=_params("parallel", "parallel", "arbitrary"))(a, b, *extras)


def _mm_x_w(name, a, wg, kind, kdim, ndim, out_dtype, residual=None, split=False):
    rows = a.shape[0]
    tk, tn = _wtiles(kind, kdim, ndim)
    tm = _tile(rows, 1024, 16)
    grid = (rows // tm, ndim // tn, kdim // tk)
    extras, especs, epi = (), (), None
    if residual is not None:
        extras, especs = (residual,), (pl.BlockSpec((tm, tn), lambda i, j, k: (i, j)),)
        epi = lambda r, res: r + res
    out_shape = (2, rows, ndim // 2) if split else (rows, ndim)
    return _matmul(name, 'nn', a, wg, grid, pl.BlockSpec((tm, tk), lambda i, j, k: (i, k)),
                   _gspec(kind, kdim, ndim, tk, tn, lambda i, j, k: (k, j)),
                   jax.ShapeDtypeStruct(out_shape, out_dtype),
                   _act_spec(tm, tn, (ndim // 2) // tn if split else None, lambda i, j, k: (i, j)),
                   (tm, tn), extras, especs, epi)


def _mm_dy_wt(name, dy, wg, kind, kdim, ndim, out_dtype, split=False):
    rows = dy.shape[-2]
    tn, tk = _wtiles(kind, kdim, ndim)
    tm = _tile(rows, 1024, 16)
    grid = (rows // tm, kdim // tn, ndim // tk)
    return _matmul(name, 'nt', dy, wg, grid,
                   _act_spec(tm, tk, (ndim // 2) // tk if split else None, lambda i, j, k: (i, k)),
                   _gspec(kind, kdim, ndim, tn, tk, lambda i, j, k: (j, k)),
                   jax.ShapeDtypeStruct((rows, kdim), out_dtype), pl.BlockSpec((tm, tn), lambda i, j, k: (i, j)),
                   (tm, tn))


def _mm_xt_dy(name, xa, dy, kind, kdim, ndim, split=False):
    rows = xa.shape[0]
    tm, tn = _wtiles(kind, kdim, ndim)
    tl = _tile(rows, 1024, 16)
    grid = (kdim // tm, ndim // tn, rows // tl)
    rdim, cdim = (kdim // 2, ndim // 4) if kind == 'col' else (kdim // 4, ndim // 2)
    return _matmul(name, 'tn', xa, dy, grid, pl.BlockSpec((tl, tm), lambda i, j, k: (k, i)),
                   _act_spec(tl, tn, (ndim // 2) // tn if split else None, lambda i, j, k: (k, j)),
                   jax.ShapeDtypeStruct((4, 2, rdim, cdim), BF16),
                   _gspec(kind, kdim, ndim, tm, tn, lambda i, j, k: (i, j)), (tm, tn))


def _rms_fwd(name, h, g):
    rows, d = h.shape
    tm = _tile(rows, 256, 16)

    def body(h_ref, g_ref, o_ref):
        x = h_ref[...]
        r = lax.rsqrt(jnp.mean(x * x, axis=-1, keepdims=True) + EPS)
        o_ref[...] = (x * r * g_ref[...]).astype(o_ref.dtype)

    return pl.pallas_call(
        body, name=name, grid=(rows // tm,),
        in_specs=[pl.BlockSpec((tm, d), lambda i: (i, 0)), pl.BlockSpec((1, d), lambda i: (0, 0))],
        out_specs=pl.BlockSpec((tm, d), lambda i: (i, 0)), out_shape=jax.ShapeDtypeStruct((rows, d), BF16),
        compiler_params=_params("parallel"))(h, g)


def _rms_bwd(name, h, g, dhn, dres):
    rows, d = h.shape
    tm = _tile(rows, 256, 16)

    def body(h_ref, g_ref, dy_ref, dres_ref, dh_ref, dg_ref):
        x = h_ref[...]
        r = lax.rsqrt(jnp.mean(x * x, axis=-1, keepdims=True) + EPS)
        xh = x * r
        dy = dy_ref[...].astype(F32)
        gy = dy * g_ref[...]
        dh_ref[...] = dres_ref[...] + r * (gy - xh * jnp.mean(gy * xh, axis=-1, keepdims=True))
        part = jnp.sum(dy * xh, axis=0, keepdims=True)

        @pl.when(pl.program_id(0) == 0)
        def _():
            dg_ref[...] = part

        @pl.when(pl.program_id(0) > 0)
        def _():
            dg_ref[...] += part

    row = pl.BlockSpec((tm, d), lambda i: (i, 0))
    vec = pl.BlockSpec((1, d), lambda i: (0, 0))
    return pl.pallas_call(
        body, name=name, grid=(rows // tm,), in_specs=[row, vec, row, row], out_specs=[row, vec],
        out_shape=[jax.ShapeDtypeStruct((rows, d), F32), jax.ShapeDtypeStruct((1, d), F32)],
        compiler_params=_params("arbitrary"))(h, g, dhn, dres)


def _loss_head(h, g, target):
    rows, d = h.shape
    tm = _tile(rows, 256, 16)

    def body(h_ref, g_ref, t_ref, dh_ref, dg_ref, loss_ref):
        x = h_ref[...]
        r = lax.rsqrt(jnp.mean(x * x, axis=-1, keepdims=True) + EPS)
        xh = x * r
        err = xh * g_ref[...] - t_ref[...]
        dy = err * (1.0 / d)
        gy = dy * g_ref[...]
        dh_ref[...] = r * (gy - xh * jnp.mean(gy * xh, axis=-1, keepdims=True))
        part = jnp.sum(dy * xh, axis=0, keepdims=True)
        sq = jnp.sum(err * err, axis=0, keepdims=True) * (0.5 / d)

        @pl.when(pl.program_id(0) == 0)
        def _():
            dg_ref[...] = part
            loss_ref[...] = sq

        @pl.when(pl.program_id(0) > 0)
        def _():
            dg_ref[...] += part
            loss_ref[...] += sq

    row = pl.BlockSpec((tm, d), lambda i: (i, 0))
    vec = pl.BlockSpec((1, d), lambda i: (0, 0))
    return pl.pallas_call(
        body, name="loss_head", grid=(rows // tm,), in_specs=[row, vec, row], out_specs=[row, vec, vec],
        out_shape=[jax.ShapeDtypeStruct((rows, d), F32), jax.ShapeDtypeStruct((1, d), F32),
                   jax.ShapeDtypeStruct((1, d), F32)],
        compiler_params=_params("arbitrary"))(h, g, target)


def _shift_down(cur, prev8, first, k):
    rows = cur.shape[0]
    rolled = pltpu.roll(cur, k, axis=0)
    idx = lax.broadcasted_iota(jnp.int32, cur.shape, 0)
    prev8 = jnp.where(first, 0.0, prev8)
    out = rolled
    for r in range(k):
        out = jnp.where(idx == r, prev8[8 - k + r:8 - k + r + 1, :], out)
    del rows
    return out


def _shift_up(cur, next8, last, k):
    rows = cur.shape[0]
    rolled = pltpu.roll(cur, rows - k, axis=0)
    idx = lax.broadcasted_iota(jnp.int32, cur.shape, 0)
    next8 = jnp.where(last, 0.0, next8)
    out = rolled
    for r in range(k):
        out = jnp.where(idx == rows - k + r, next8[r:r + 1, :], out)
    return out


def _conv_acc(z, zprev, first, w, b):
    z1 = _shift_down(z, zprev, first, 1)
    z2 = _shift_down(z, zprev, first, 2)
    return b + w[2:3, :] * z + w[1:2, :] * z1 + w[0:1, :] * z2, z1, z2


def _ffn_tiles(rows, f):
    return _tile(rows, 512, 16), _tile(f, 512, LANES)


def _ffn_act_fwd(name, z3, cw3, cb3):
    _, rows, f = z3.shape
    tm, tc = _ffn_tiles(rows, f)
    hb = tm // 8

    def body(z_ref, zp_ref, w_ref, b_ref, o_ref):
        first = pl.program_id(0) == 0
        gate, _, _ = _conv_acc(z_ref[0].astype(F32), zp_ref[0].astype(F32), first, w_ref[0], b_ref[0])
        val, _, _ = _conv_acc(z_ref[1].astype(F32), zp_ref[1].astype(F32), first, w_ref[1], b_ref[1])
        o_ref[...] = (gate * jax.nn.sigmoid(gate) * val).astype(o_ref.dtype)

    return pl.pallas_call(
        body, name=name, grid=(rows // tm, f // tc),
        in_specs=[pl.BlockSpec((2, tm, tc), lambda i, j: (0, i, j)),
                  pl.BlockSpec((2, 8, tc), lambda i, j: (0, jnp.maximum(i * hb - 1, 0), j)),
                  pl.BlockSpec((2, 3, tc), lambda i, j: (0, 0, j)), pl.BlockSpec((2, 1, tc), lambda i, j: (0, 0, j))],
        out_specs=pl.BlockSpec((tm, tc), lambda i, j: (i, j)), out_shape=jax.ShapeDtypeStruct((rows, f), BF16),
        compiler_params=_params("parallel", "parallel"))(z3, z3, cw3, cb3)


def _gate_grads(d_a, acc_g, acc_v):
    sig = jax.nn.sigmoid(acc_g)
    return d_a * acc_v * sig * (1.0 + acc_g * (1.0 - sig)), d_a * acc_g * sig


def _ffn_act_bwd(name, z3, da, cw3, cb3):
    _, rows, f = z3.shape
    tm, tc = _ffn_tiles(rows, f)
    hb = tm // 8
    nrow = rows // tm

    def body(z_ref, zp_ref, zn_ref, da_ref, dan_ref, w_ref, b_ref, dz_ref, dcw_ref, dcb_ref):
        i = pl.program_id(1)
        first, last = i == 0, i == nrow - 1
        w, b = (w_ref[0], w_ref[1]), (b_ref[0], b_ref[1])
        z = (z_ref[0].astype(F32), z_ref[1].astype(F32))
        acc, taps = [], []
        for hf in range(2):
            a_h, z1, z2 = _conv_acc(z[hf], zp_ref[hf].astype(F32), first, w[hf], b[hf])
            acc.append(a_h)
            taps.append((z2, z1, z[hf]))
        dacc = _gate_grads(da_ref[...].astype(F32), acc[0], acc[1])
        acc_n = [_conv_acc(zn_ref[hf].astype(F32), z[hf][tm - 8:tm, :], False, w[hf], b[hf])[0] for hf in range(2)]
        dacc_n = _gate_grads(dan_ref[...].astype(F32), acc_n[0], acc_n[1])
        for hf in range(2):
            d = dacc[hf]
            d1 = _shift_up(d, dacc_n[hf], last, 1)
            d2 = _shift_up(d, dacc_n[hf], last, 2)
            dz_ref[hf] = (w[hf][2:3, :] * d + w[hf][1:2, :] * d1 + w[hf][0:1, :] * d2).astype(dz_ref.dtype)
        sums_w = [[jnp.sum(dacc[hf] * t, axis=0, keepdims=True) for t in taps[hf]] for hf in range(2)]
        sums_b = [jnp.sum(dacc[hf], axis=0, keepdims=True) for hf in range(2)]

        @pl.when(first)
        def _():
            for hf in range(2):
                for k in range(3):
                    dcw_ref[hf, k:k + 1, :] = sums_w[hf][k]
                dcb_ref[hf] = sums_b[hf]

        @pl.when(i > 0)
        def _():
            for hf in range(2):
                for k in range(3):
                    dcw_ref[hf, k:k + 1, :] += sums_w[hf][k]
                dcb_ref[hf] += sums_b[hf]

    nxt = lambda i: jnp.minimum((i + 1) * hb, rows // 8 - 1)
    wsp = pl.BlockSpec((2, 3, tc), lambda j, i: (0, 0, j))
    bsp = pl.BlockSpec((2, 1, tc), lambda j, i: (0, 0, j))
    cur = pl.BlockSpec((2, tm, tc), lambda j, i: (0, i, j))
    return pl.pallas_call(
        body, name=name, grid=(f // tc, nrow),
        in_specs=[cur, pl.BlockSpec((2, 8, tc), lambda j, i: (0, jnp.maximum(i * hb - 1, 0), j)),
                  pl.BlockSpec((2, 8, tc), lambda j, i: (0, nxt(i), j)), pl.BlockSpec((tm, tc), lambda j, i: (i, j)),
                  pl.BlockSpec((8, tc), lambda j, i: (nxt(i), j)), wsp, bsp],
        out_specs=[cur, wsp, bsp],
        out_shape=[jax.ShapeDtypeStruct((2, rows, f), BF16), jax.ShapeDtypeStruct((2, 3, f), F32),
                   jax.ShapeDtypeStruct((2, 1, f), F32)],
        compiler_params=_params("parallel", "arbitrary"))(z3, z3, z3, da, da, cw3, cb3)


def _glu_fwd(name, gg, h):
    rows, d2 = gg.shape
    d = d2 // 2
    tm, tc = _tile(rows, 512, 16), _tile(d, 1024, LANES)
    nd = d // tc

    def body(a_ref, b_ref, h_ref, o_ref):
        o_ref[...] = h_ref[...] + a_ref[...].astype(F32) * jax.nn.sigmoid(b_ref[...].astype(F32))

    return pl.pallas_call(
        body, name=name, grid=(rows // tm, nd),
        in_specs=[pl.BlockSpec((tm, tc), lambda i, j: (i, j)), pl.BlockSpec((tm, tc), lambda i, j: (i, j + nd)),
                  pl.BlockSpec((tm, tc), lambda i, j: (i, j))],
        out_specs=pl.BlockSpec((tm, tc), lambda i, j: (i, j)), out_shape=jax.ShapeDtypeStruct((rows, d), F32),
        compiler_params=_params("parallel", "parallel"))(gg, gg, h)


def _glu_bwd(name, gg, dh):
    rows, d2 = gg.shape
    d = d2 // 2
    tm, tc = _tile(rows, 512, 16), _tile(d, 1024, LANES)
    nd = d // tc

    def body(s_ref, o_ref, dh_ref, out_ref):
        is_a = pl.program_id(1) < nd
        me = s_ref[...].astype(F32)
        other = o_ref[...].astype(F32)
        g = dh_ref[...]
        sig_o = jax.nn.sigmoid(other)
        sig_m = jax.nn.sigmoid(me)
        out_ref[...] = jnp.where(is_a, g * sig_o, g * other * sig_m * (1.0 - sig_m)).astype(out_ref.dtype)

    return pl.pallas_call(
        body, name=name, grid=(rows // tm, 2 * nd),
        in_specs=[pl.BlockSpec((tm, tc), lambda i, j: (i, j)),
                  pl.BlockSpec((tm, tc), lambda i, j: (i, (j + nd) % (2 * nd))),
                  pl.BlockSpec((tm, tc), lambda i, j: (i, j % nd))],
        out_specs=pl.BlockSpec((tm, tc), lambda i, j: (i, j)), out_shape=jax.ShapeDtypeStruct((rows, d2), BF16),
        compiler_params=_params("parallel", "parallel"))(gg, gg, dh)


def _sgu_common(pre_ref, gv_ref, e):
    u = _gelu(pre_ref[:, :e].astype(F32))
    v = _gelu(pre_ref[:, e:].astype(F32))
    r = lax.rsqrt(jnp.mean(v * v, axis=-1, keepdims=True) + EPS)
    vh = v * r
    return u, vh, r, (vh * gv_ref[...]).astype(BF16)


def _tril_bf16(ws_ref, hd):
    t = lax.broadcasted_iota(jnp.int32, (CHUNK, CHUNK), 0)
    s = lax.broadcasted_iota(jnp.int32, (CHUNK, CHUNK), 1)
    return jnp.where(s <= t, ws_ref[hd], 0.0).astype(BF16)


def _sgu_mix_fwd(name, pre, gv, ws, bsx):
    rows, e2 = pre.shape
    e = e2 // 2
    heads = e // SGU_GROUP
    tr = _tile(rows, 256, CHUNK)

    def body(pre_ref, gv_ref, ws_ref, bs_ref, o_ref):
        u, _, _, vn = _sgu_common(pre_ref, gv_ref, e)
        for hd in range(heads):
            wm = _tril_bf16(ws_ref, hd)
            cols = slice(hd * SGU_GROUP, (hd + 1) * SGU_GROUP)
            for ck in range(tr // CHUNK):
                rws = slice(ck * CHUNK, (ck + 1) * CHUNK)
                s = jnp.dot(wm, vn[rws, cols], preferred_element_type=F32) + bs_ref[hd]
                o_ref[rws, cols] = (u[rws, cols] * s).astype(o_ref.dtype)

    whole3 = pl.BlockSpec((heads, CHUNK, CHUNK), lambda i: (0, 0, 0))
    return pl.pallas_call(
        body, name=name, grid=(rows // tr,),
        in_specs=[pl.BlockSpec((tr, e2), lambda i: (i, 0)), pl.BlockSpec((1, e), lambda i: (0, 0)), whole3, whole3],
        out_specs=pl.BlockSpec((tr, e), lambda i: (i, 0)), out_shape=jax.ShapeDtypeStruct((rows, e), BF16),
        compiler_params=_params("parallel"))(pre, gv, ws, bsx)


def _sgu_mix_bwd(name, pre, dus, gv, ws, bsx):
    rows, e2 = pre.shape
    e = e2 // 2
    heads = e // SGU_GROUP
    tr = _tile(rows, 256, CHUNK)

    def body(pre_ref, dus_ref, gv_ref, ws_ref, bs_ref, dpre_ref, dws_ref, dbs_ref, dgv_ref, dvn_ref, du_ref):
        first = pl.program_id(0) == 0
        u, vh, r, vn = _sgu_common(pre_ref, gv_ref, e)
        ones = jnp.ones((SGU_GROUP, LANES), BF16)
        tt = lax.broadcasted_iota(jnp.int32, (CHUNK, CHUNK), 0)
        ss = lax.broadcasted_iota(jnp.int32, (CHUNK, CHUNK), 1)
        for hd in range(heads):
            wm = _tril_bf16(ws_ref, hd)
            cols = slice(hd * SGU_GROUP, (hd + 1) * SGU_GROUP)
            dw = jnp.zeros((CHUNK, CHUNK), F32)
            db = jnp.zeros((CHUNK, LANES), F32)
            for ck in range(tr // CHUNK):
                rws = slice(ck * CHUNK, (ck + 1) * CHUNK)
                vblk = vn[rws, cols]
                s = jnp.dot(wm, vblk, preferred_element_type=F32) + bs_ref[hd]
                d_us = dus_ref[rws, cols].astype(F32)
                du_ref[rws, cols] = d_us * s
                ds = (d_us * u[rws, cols]).astype(BF16)
                dvn_ref[rws, cols] = lax.dot_general(wm, ds, _DIMS['tn'], preferred_element_type=F32)
                dw = dw + lax.dot_general(ds, vblk, _DIMS['nt'], preferred_element_type=F32)
                db = db + jnp.dot(ds, ones, preferred_element_type=F32)
            dw = jnp.where(ss <= tt, dw, 0.0)

            @pl.when(first)
            def _():
                dws_ref[hd] = dw
                dbs_ref[hd] = db

            @pl.when(jnp.logical_not(first))
            def _():
                dws_ref[hd] += dw
                dbs_ref[hd] += db

        dvn = dvn_ref[...]
        part = jnp.sum(dvn * vh, axis=0, keepdims=True)

        @pl.when(first)
        def _():
            dgv_ref[...] = part

        @pl.when(jnp.logical_not(first))
        def _():
            dgv_ref[...] += part

        gy = dvn * gv_ref[...]
        dv = r * (gy - vh * jnp.mean(gy * vh, axis=-1, keepdims=True))
        dpre_ref[:, :e] = (du_ref[...] * _gelu_grad(pre_ref[:, :e].astype(F32))).astype(dpre_ref.dtype)
        dpre_ref[:, e:] = (dv * _gelu_grad(pre_ref[:, e:].astype(F32))).astype(dpre_ref.dtype)

    whole3 = pl.BlockSpec((heads, CHUNK, CHUNK), lambda i: (0, 0, 0))
    vec = pl.BlockSpec((1, e), lambda i: (0, 0))
    return pl.pallas_call(
        body, name=name, grid=(rows // tr,),
        in_specs=[pl.BlockSpec((tr, e2), lambda i: (i, 0)), pl.BlockSpec((tr, e), lambda i: (i, 0)), vec, whole3, whole3],
        out_specs=[pl.BlockSpec((tr, e2), lambda i: (i, 0)), whole3, whole3, vec],
        out_shape=[jax.ShapeDtypeStruct((rows, e2), BF16), jax.ShapeDtypeStruct((heads, CHUNK, CHUNK), F32),
                   jax.ShapeDtypeStruct((heads, CHUNK, LANES), F32), jax.ShapeDtypeStruct((1, e), F32)],
        scratch_shapes=[pltpu.VMEM((tr, e), F32), pltpu.VMEM((tr, e), F32)],
        compiler_params=_params("arbitrary"))(pre, dus, gv, ws, bsx)


def _disc_a(lr, li, ldt):
    dt = jnp.exp(ldt)
    mag = jnp.exp(dt * lr)
    return mag * jnp.cos(dt * li), mag * jnp.sin(dt * li)


def _disc_b(lr, li, ldt, br, bi):
    ar, ai = _disc_a(lr, li, ldt)
    den = lr * lr + li * li
    qr = ((ar - 1.0) * lr + ai * li) / den
    qi = (ai * lr - (ar - 1.0) * li) / den
    return qr * br - qi * bi, qr * bi + qi * br


def _s5_disc(name, lr, li, ldt, lrx, lix, ldtx, br, bi):
    def body(lr_ref, li_ref, ldt_ref, lrx_ref, lix_ref, ldtx_ref, br_ref, bi_ref, ar_ref, ai_ref, bbr_ref, bbi_ref):
        ar_ref[...], ai_ref[...] = _disc_a(lr_ref[...], li_ref[...], ldt_ref[...])
        bbr_ref[...], bbi_ref[...] = _disc_b(lrx_ref[...], lix_ref[...], ldtx_ref[...], br_ref[...], bi_ref[...])

    small = jax.ShapeDtypeStruct(lr.shape, F32)
    wide = jax.ShapeDtypeStruct(br.shape, F32)
    return pl.pallas_call(body, name=name, out_shape=[small, small, wide, wide],
                          compiler_params=pltpu.CompilerParams(vmem_limit_bytes=VMEM_LIMIT))(
        lr, li, ldt, lrx, lix, ldtx, br, bi)


def _s5_disc_bwd(name, lr, li, ldt, lrx, lix, ldtx, br, bi, dar, dai, dbbr, dbbi, sel):
    def body(lr_ref, li_ref, ldt_ref, lrx_ref, lix_ref, ldtx_ref, br_ref, bi_ref, dar_ref, dai_ref, dbbr_ref,
             dbbi_ref, sel_ref, dlr_ref, dli_ref, dldt_ref, dbr_ref, dbi_ref):
        _, vjp_a = jax.vjp(_disc_a, lr_ref[...], li_ref[...], ldt_ref[...])
        g_lr, g_li, g_ldt = vjp_a((dar_ref[...], dai_ref[...]))
        _, vjp_b = jax.vjp(_disc_b, lrx_ref[...], lix_ref[...], ldtx_ref[...], br_ref[...], bi_ref[...])
        x_lr, x_li, x_ldt, g_br, g_bi = vjp_b((dbbr_ref[...], dbbi_ref[...]))
        fold = lambda t: jnp.dot(t, sel_ref[...], precision=lax.Precision.HIGHEST, preferred_element_type=F32)
        dlr_ref[...] = g_lr + fold(x_lr)
        dli_ref[...] = g_li + fold(x_li)
        dldt_ref[...] = jnp.sum(g_ldt + fold(x_ldt), axis=1, keepdims=True)
        dbr_ref[...] = g_br
        dbi_ref[...] = g_bi

    small = jax.ShapeDtypeStruct(lr.shape, F32)
    wide = jax.ShapeDtypeStruct(br.shape, F32)
    return pl.pallas_call(body, name=name,
                          out_shape=[small, small, jax.ShapeDtypeStruct((lr.shape[0], 1), F32), wide, wide],
                          compiler_params=pltpu.CompilerParams(vmem_limit_bytes=VMEM_LIMIT))(
        lr, li, ldt, lrx, lix, ldtx, br, bi, dar, dai, dbbr, dbbi, sel)


def _cmul(ar, ai, br, bi):
    return ar * br - ai * bi, ar * bi + ai * br


def _pow_seg(ar, ai):
    res, base, n = None, (ar, ai), SEG
    while n:
        if n & 1:
            res = base if res is None else _cmul(*res, *base)
        n >>= 1
        if n:
            base = _cmul(*base, *base)
    return res


def _scan_forward(hr_ref, hi_ref, er_ref, ei_ref, sr_ref, si_ref, ar, ai, nck):
    arb, aib = jnp.broadcast_to(ar, (nck, LANES)), jnp.broadcast_to(ai, (nck, LANES))

    def intra(t, carry):
        sr, si = carry
        slab = pl.ds(t, nck, stride=SEG)
        nr = arb * sr - aib * si + hr_ref[slab, :]
        ni = arb * si + aib * sr + hi_ref[slab, :]
        hr_ref[slab, :] = nr
        hi_ref[slab, :] = ni
        return nr, ni

    zero = jnp.zeros((nck, LANES), F32)
    er_ref[...], ei_ref[...] = lax.fori_loop(0, SEG, intra, (zero, zero), unroll=4)
    pcr, pci = _pow_seg(ar, ai)
    sr_ref[0:1, :] = jnp.zeros((1, LANES), F32)
    si_ref[0:1, :] = jnp.zeros((1, LANES), F32)
    for ck in range(nck - 1):
        pr, pi = sr_ref[ck:ck + 1, :], si_ref[ck:ck + 1, :]
        sr_ref[ck + 1:ck + 2, :] = pcr * pr - pci * pi + er_ref[ck:ck + 1, :]
        si_ref[ck + 1:ck + 2, :] = pcr * pi + pci * pr + ei_ref[ck:ck + 1, :]
    s_r, s_i = sr_ref[...], si_ref[...]

    def fix(t, carry):
        pr, pi = carry
        slab = pl.ds(t, nck, stride=SEG)
        hr_ref[slab, :] = hr_ref[slab, :] + (pr * s_r - pi * s_i)
        hi_ref[slab, :] = hi_ref[slab, :] + (pr * s_i + pi * s_r)
        return _cmul(pr, pi, arb, aib)

    lax.fori_loop(0, SEG, fix, (arb, aib), unroll=4)


def _scan_backward(gr_ref, gi_ref, hr_ref, hi_ref, er_ref, ei_ref, sr_ref, si_ref, ar, ai, nck):
    arb, aib = jnp.broadcast_to(ar, (nck, LANES)), jnp.broadcast_to(-ai, (nck, LANES))

    def intra(k, carry):
        sr, si = carry
        slab = pl.ds(SEG - 1 - k, nck, stride=SEG)
        nr = arb * sr - aib * si + gr_ref[slab, :]
        ni = arb * si + aib * sr + gi_ref[slab, :]
        gr_ref[slab, :] = nr
        gi_ref[slab, :] = ni
        return nr, ni

    zero = jnp.zeros((nck, LANES), F32)
    er_ref[...], ei_ref[...] = lax.fori_loop(0, SEG, intra, (zero, zero), unroll=4)
    pcr, pci = _pow_seg(ar, -ai)
    sr_ref[nck - 1:nck, :] = jnp.zeros((1, LANES), F32)
    si_ref[nck - 1:nck, :] = jnp.zeros((1, LANES), F32)
    for ck in range(nck - 1, 0, -1):
        pr, pi = sr_ref[ck:ck + 1, :], si_ref[ck:ck + 1, :]
        sr_ref[ck - 1:ck, :] = pcr * pr - pci * pi + er_ref[ck:ck + 1, :]
        si_ref[ck - 1:ck, :] = pcr * pi + pci * pr + ei_ref[ck:ck + 1, :]
    s_r, s_i = sr_ref[...], si_ref[...]
    last = pl.ds(SEG - 1, nck, stride=SEG)
    row = lax.broadcasted_iota(jnp.int32, (nck, LANES), 0)
    hp_r = jnp.where(row == 0, 0.0, pltpu.roll(hr_ref[last, :], 1, axis=0)) if nck > 1 else zero
    hp_i = jnp.where(row == 0, 0.0, pltpu.roll(hi_ref[last, :], 1, axis=0)) if nck > 1 else zero

    def settle(t, pr, pi, h_r, h_i):
        slab = pl.ds(t, nck, stride=SEG)
        g_r = gr_ref[slab, :] + (pr * s_r - pi * s_i)
        g_i = gi_ref[slab, :] + (pr * s_i + pi * s_r)
        gr_ref[slab, :] = g_r
        gi_ref[slab, :] = g_i
        return g_r * h_r + g_i * h_i, g_i * h_r - g_r * h_i

    def fix(k, carry):
        pr, pi, acr, aci = carry
        t = SEG - 1 - k
        prev = pl.ds(t - 1, nck, stride=SEG)
        d_r, d_i = settle(t, pr, pi, hr_ref[prev, :], hi_ref[prev, :])
        nr, ni = _cmul(pr, pi, arb, aib)
        return nr, ni, acr + d_r, aci + d_i

    pr, pi, acr, aci = lax.fori_loop(0, SEG - 1, fix, (arb, aib, zero, zero), unroll=4)
    d_r, d_i = settle(0, pr, pi, hp_r, hp_i)
    return jnp.sum(acr + d_r, axis=0, keepdims=True), jnp.sum(aci + d_i, axis=0, keepdims=True)


def _s5_fill_states(u_ref, br_ref, bi_ref, hr_ref, hi_ref, rows):
    ub = u_ref[...].astype(BF16)
    hr_ref[0:rows, :] = jnp.dot(ub, br_ref[...], preferred_element_type=F32)
    hi_ref[0:rows, :] = jnp.dot(ub, bi_ref[...], preferred_element_type=F32)
    pad = jnp.zeros((hr_ref.shape[0] - rows, LANES), F32)
    hr_ref[rows:, :] = pad
    hi_ref[rows:, :] = pad


def _s5_specs(rows, e):
    sb = STATE_BLOCKS
    chan = pl.BlockSpec((rows, LANES), lambda j: (0, j // sb))
    bmat = pl.BlockSpec((None, LANES, LANES), lambda j: (j // sb, 0, j % sb))
    cmat = pl.BlockSpec((None, LANES, LANES), lambda j: (j // sb, j % sb, 0))
    avec = pl.BlockSpec((1, LANES), lambda j: (0, j))
    dvec = pl.BlockSpec((1, LANES), lambda j: (0, j // sb))
    return chan, bmat, cmat, avec, dvec


def _s5_fwd(name, u, bre, bim, crt, cit, ar, ai, dd):
    rows, e = u.shape
    nck = rows // CHUNK
    nsteps = (e // LANES) * STATE_BLOCKS
    chan, bmat, cmat, avec, dvec = _s5_specs(rows, e)

    def body(u_ref, br_ref, bi_ref, cr_ref, ci_ref, ar_ref, ai_ref, dd_ref, y_ref, gy_ref,
             hr_ref, hi_ref, er_ref, ei_ref, sr_ref, si_ref, acc_ref):
        j = pl.program_id(0) % STATE_BLOCKS
        _s5_fill_states(u_ref, br_ref, bi_ref, hr_ref, hi_ref, rows)
        _scan_forward(hr_ref, hi_ref, er_ref, ei_ref, sr_ref, si_ref, ar_ref[...], ai_ref[...], nck)
        contrib = (jnp.dot(hr_ref[0:rows, :].astype(BF16), cr_ref[...], preferred_element_type=F32)
                   - jnp.dot(hi_ref[0:rows, :].astype(BF16), ci_ref[...], preferred_element_type=F32))

        @pl.when(j == 0)
        def _():
            acc_ref[...] = dd_ref[...] * u_ref[...] + contrib

        @pl.when(j > 0)
        def _():
            acc_ref[...] += contrib

        @pl.when(j == STATE_BLOCKS - 1)
        def _():
            y = acc_ref[...]
            y_ref[...] = y.astype(y_ref.dtype)
            gy_ref[...] = _gelu(y).astype(gy_ref.dtype)

    flat = pltpu.VMEM((rows, LANES), F32)
    big = pltpu.VMEM((nck * SEG, LANES), F32)
    small = pltpu.VMEM((nck, LANES), F32)
    out = jax.ShapeDtypeStruct((rows, e), BF16)
    return pl.pallas_call(
        body, name=name, grid=(nsteps,), in_specs=[chan, bmat, bmat, cmat, cmat, avec, avec, dvec],
        out_specs=[chan, chan], out_shape=[out, out], scratch_shapes=[big, big, small, small, small, small, flat],
        compiler_params=_params("arbitrary"))(u, bre, bim, crt, cit, ar, ai, dd)


def _s5_bwd(name, u, y, dgy, bre, bim, crt, cit, ar, ai, dd):
    rows, e = u.shape
    nb = e // LANES
    nck = rows // CHUNK
    nsteps = nb * STATE_BLOCKS
    chan, bmat, cmat, avec, dvec = _s5_specs(rows, e)

    def body(u_ref, y_ref, dgy_ref, br_ref, bi_ref, cr_ref, ci_ref, ar_ref, ai_ref, dd_ref,
             du_ref, dbr_ref, dbi_ref, dcr_ref, dci_ref, dar_ref, dai_ref, ddd_ref,
             hr_ref, hi_ref, gr_ref, gi_ref, er_ref, ei_ref, sr_ref, si_ref, acc_ref, dy_ref):
        j = pl.program_id(0) % STATE_BLOCKS
        _s5_fill_states(u_ref, br_ref, bi_ref, hr_ref, hi_ref, rows)
        _scan_forward(hr_ref, hi_ref, er_ref, ei_ref, sr_ref, si_ref, ar_ref[...], ai_ref[...], nck)

        @pl.when(j == 0)
        def _():
            dy0 = dgy_ref[...].astype(F32) * _gelu_grad(y_ref[...].astype(F32))
            dy_ref[...] = dy0
            ddd_ref[...] = jnp.sum(dy0 * u_ref[...], axis=0, keepdims=True)

        dyb = dy_ref[...].astype(BF16)
        pad = jnp.zeros((gr_ref.shape[0] - rows, LANES), F32)
        gr_ref[0:rows, :] = lax.dot_general(dyb, cr_ref[...], _DIMS['nt'], preferred_element_type=F32)
        gi_ref[0:rows, :] = -lax.dot_general(dyb, ci_ref[...], _DIMS['nt'], preferred_element_type=F32)
        gr_ref[rows:, :] = pad
        gi_ref[rows:, :] = pad
        dcr_ref[...] = lax.dot_general(hr_ref[0:rows, :].astype(BF16), dyb, _DIMS['tn'], preferred_element_type=F32)
        dci_ref[...] = -lax.dot_general(hi_ref[0:rows, :].astype(BF16), dyb, _DIMS['tn'], preferred_element_type=F32)
        dar_ref[...], dai_ref[...] = _scan_backward(gr_ref, gi_ref, hr_ref, hi_ref, er_ref, ei_ref, sr_ref, si_ref,
                                                    ar_ref[...], ai_ref[...], nck)
        ub = u_ref[...].astype(BF16)
        grb, gib = gr_ref[0:rows, :].astype(BF16), gi_ref[0:rows, :].astype(BF16)
        dbr_ref[...] = lax.dot_general(ub, grb, _DIMS['tn'], preferred_element_type=F32)
        dbi_ref[...] = lax.dot_general(ub, gib, _DIMS['tn'], preferred_element_type=F32)
        contrib = (lax.dot_general(grb, br_ref[...], _DIMS['nt'], preferred_element_type=F32)
                   + lax.dot_general(gib, bi_ref[...], _DIMS['nt'], preferred_element_type=F32))

        @pl.when(j == 0)
        def _():
            acc_ref[...] = dd_ref[...] * dy_ref[...] + contrib

        @pl.when(j > 0)
        def _():
            acc_ref[...] += contrib

        @pl.when(j == STATE_BLOCKS - 1)
        def _():
            du_ref[...] = acc_ref[...]

    flat = pltpu.VMEM((rows, LANES), F32)
    big = pltpu.VMEM((nck * SEG, LANES), F32)
    small = pltpu.VMEM((nck, LANES), F32)
    bshape = jax.ShapeDtypeStruct((nb, LANES, LANES * STATE_BLOCKS), F32)
    cshape = jax.ShapeDtypeStruct((nb, LANES * STATE_BLOCKS, LANES), F32)
    ashape = jax.ShapeDtypeStruct((1, nb * LANES * STATE_BLOCKS), F32)
    return pl.pallas_call(
        body, name=name, grid=(nsteps,),
        in_specs=[chan, chan, chan, bmat, bmat, cmat, cmat, avec, avec, dvec],
        out_specs=[chan, bmat, bmat, cmat, cmat, avec, avec, dvec],
        out_shape=[jax.ShapeDtypeStruct((rows, e), F32), bshape, bshape, cshape, cshape, ashape, ashape,
                   jax.ShapeDtypeStruct((1, e), F32)],
        scratch_shapes=[big, big, big, big, small, small, small, small, flat, flat],
        compiler_params=_params("arbitrary"))(u, y, dgy, bre, bim, crt, cit, ar, ai, dd)


def _to_blockdiag_b(bbar, nb):
    eye = jnp.eye(GROUPS_PER_BLOCK, dtype=bbar.dtype)
    t = jnp.einsum('bgpc,gh->bgchp', bbar.reshape(nb, GROUPS_PER_BLOCK, SSM_STATE, SSM_GROUP), eye)
    return t.reshape(nb, LANES, GROUPS_PER_BLOCK * SSM_STATE)


def _from_blockdiag_b(dmat, nb):
    eye = jnp.eye(GROUPS_PER_BLOCK, dtype=dmat.dtype)
    t = dmat.reshape(nb, GROUPS_PER_BLOCK, SSM_GROUP, GROUPS_PER_BLOCK, SSM_STATE)
    return jnp.einsum('bgchp,gh->bgpc', t, eye).reshape(nb * GROUPS_PER_BLOCK, SSM_STATE, SSM_GROUP)


def _to_blockdiag_ct(c, nb):
    eye = jnp.eye(GROUPS_PER_BLOCK, dtype=c.dtype)
    t = jnp.einsum('bgop,gh->bgpho', c.reshape(nb, GROUPS_PER_BLOCK, SSM_GROUP, SSM_STATE), eye)
    return t.reshape(nb, GROUPS_PER_BLOCK * SSM_STATE, LANES)


def _from_blockdiag_ct(dmat, nb):
    eye = jnp.eye(GROUPS_PER_BLOCK, dtype=dmat.dtype)
    t = dmat.reshape(nb, GROUPS_PER_BLOCK, SSM_STATE, GROUPS_PER_BLOCK, SSM_GROUP)
    return jnp.einsum('bgpho,gh->bgop', t, eye).reshape(nb * GROUPS_PER_BLOCK, SSM_GROUP, SSM_STATE)


ANY = pl.BlockSpec(memory_space=pl.ANY)


def _half_specs(kind, rdim, cdim, tr, tc, layer):
    nr, nc = rdim // tr, cdim // tc
    if kind == 'col':
        nat = pl.BlockSpec((None, tr, tc), lambda c, rb, cb: (layer, c * nr + rb, cb))
    else:
        nat = pl.BlockSpec((None, tr, tc), lambda c, rb, cb: (layer, rb, c * nc + cb))
    half = pl.BlockSpec((None, tr, tc), lambda c, rb, cb: (c, rb, cb))
    return nat, half


def _my_chip():
    return 2 * lax.axis_index("x") + lax.axis_index("y")


def _cast_halves(name, w, kind, layer):
    rdim, cdim = _half_shape(kind, w.shape)
    tr, tc = _tile(rdim, 512, 16), _tile(cdim, 1408, LANES)
    nat, _ = _half_specs(kind, rdim, cdim, tr, tc, layer)
    slot = pl.BlockSpec((None, None, tr, tc), lambda c, rb, cb: (_my_chip(), c, rb, cb))

    def body(w_ref, o_ref):
        o_ref[...] = w_ref[...].astype(o_ref.dtype)

    return pl.pallas_call(
        body, name=name, grid=(2, rdim // tr, cdim // tc), in_specs=[nat], out_specs=slot,
        out_shape=jax.ShapeDtypeStruct((4, 2, rdim, cdim), BF16),
        compiler_params=_params("parallel", "parallel", "parallel"))(w)


def _adam_math(w, g, m, v):
    m = ADAM_B1 * m + (1.0 - ADAM_B1) * g
    v = ADAM_B2 * v + (1.0 - ADAM_B2) * (g * g)
    m_hat = m / (1.0 - ADAM_B1 ** ADAM_STEP)
    v_hat = v / (1.0 - ADAM_B2 ** ADAM_STEP)
    delta = -ADAM_LR * (m_hat / (jnp.sqrt(v_hat) + ADAM_EPS) + ADAM_WD * w)
    return delta, m, v


def _adam_big(name, w, m, v, gfull, kind, layer, outs):
    rdim, cdim = _half_shape(kind, w.shape)
    tr, tc = _tile(rdim, 256, 8), _tile(cdim, 1408, LANES)
    nat, half = _half_specs(kind, rdim, cdim, tr, tc, layer)

    def body(w_ref, m_ref, v_ref, g_ref, *rest):
        go_ref, d_ref, mo_ref, vo_ref = rest[4:]
        g = g_ref[...]
        go_ref[...] = g
        d_ref[...], mo_ref[...], vo_ref[...] = _adam_math(w_ref[...], g, m_ref[...], v_ref[...])

    shape = jax.ShapeDtypeStruct(w.shape, F32)
    return pl.pallas_call(
        body, name=name, grid=(2, rdim // tr, cdim // tc), in_specs=[nat, nat, nat, half] + [ANY] * 4,
        out_specs=[nat, nat, nat, nat], out_shape=[shape, shape, shape, shape],
        input_output_aliases={4: 0, 5: 1, 6: 2, 7: 3},
        compiler_params=_params("parallel", "parallel", "parallel"))(w, m, v, gfull, *outs)


def _adam_small(w, m, v, g):
    rows = w.shape[0]
    tr = _tile(rows, 512, 8)
    spec = pl.BlockSpec((tr, LANES), lambda i: (i, 0))

    def body(w_ref, m_ref, v_ref, g_ref, d_ref, mo_ref, vo_ref):
        d_ref[...], mo_ref[...], vo_ref[...] = _adam_math(w_ref[...], g_ref[...], m_ref[...], v_ref[...])

    shape = jax.ShapeDtypeStruct(w.shape, F32)
    return pl.pallas_call(body, name="adam_small", grid=(rows // tr,), in_specs=[spec] * 4, out_specs=[spec] * 3,
                          out_shape=[shape] * 3, compiler_params=_params("parallel"))(w, m, v, g)


def _add2(name, part, got):
    cdim = part.shape[-1]
    a2, b2 = part.reshape(4, 2, -1, cdim), got.reshape(4, -1, cdim)
    rows = b2.shape[1]
    tr, tc = _tile(rows, 512, 16), _tile(cdim, 1408, LANES)
    mine = pl.BlockSpec((None, None, tr, tc), lambda k, i, j: (k, lax.axis_index("c"), i, j))
    spec = pl.BlockSpec((None, tr, tc), lambda k, i, j: (k, i, j))

    def body(a_ref, b_ref, o_ref):
        o_ref[...] = (a_ref[...].astype(F32) + b_ref[...].astype(F32)).astype(o_ref.dtype)

    out = pl.pallas_call(
        body, name=name, grid=(4, rows // tr, cdim // tc), in_specs=[mine, spec], out_specs=spec,
        out_shape=jax.ShapeDtypeStruct(b2.shape, BF16),
        compiler_params=_params("parallel", "parallel", "parallel"))(a2, b2)
    return out.reshape(got.shape)


def _add4(name, sums, recv):
    cdim = sums.shape[-1]
    s2 = sums.reshape(4, -1, cdim)
    r3 = recv.reshape(3, -1, cdim)
    rows = s2.shape[1]
    tr, tc = _tile(rows, 512, 16), _tile(cdim, 1408, LANES)
    own = pl.BlockSpec((None, tr, tc), lambda i, j: (_my_chip(), i, j))
    rspec = lambda k: pl.BlockSpec((None, tr, tc), lambda i, j: (k, i, j))
    slot = pl.BlockSpec((None, tr, tc), lambda i, j: (lax.axis_index("c"), i, j))

    def body(o_ref, x_ref, y_ref, d_ref, out_ref):
        out_ref[...] = ((o_ref[...].astype(F32) + d_ref[...].astype(F32))
                        + (x_ref[...].astype(F32) + y_ref[...].astype(F32)))

    out = pl.pallas_call(
        body, name=name, grid=(rows // tr, cdim // tc), in_specs=[own, rspec(0), rspec(1), rspec(2)], out_specs=slot,
        out_shape=jax.ShapeDtypeStruct((2, rows, cdim), F32),
        compiler_params=_params("parallel", "parallel"))(s2, r3, r3, r3)
    return out.reshape(2, *sums.shape[1:])


def _place():
    x, y, c = lax.axis_index("x"), lax.axis_index("y"), lax.axis_index("c")
    chips = [(1 - x, y), (x, 1 - y), (1 - x, 1 - y)]
    return x, y, c, chips


def _remote(src, dst, send, recv, to):
    return pltpu.make_async_remote_copy(src_ref=src, dst_ref=dst, send_sem=send, recv_sem=recv, device_id=to,
                                        device_id_type=MESH)


def _pieces(src, dst, bands):
    lead, rows = src.shape[:-2], src.shape[-2]
    band = rows // bands
    out = []
    for idx in itertools.product(*[range(dim) for dim in lead]):
        for q in range(bands):
            sl = (*idx, pl.ds(q * band, band))
            out.append((src.at[sl], dst.at[sl]))
    return out


def _allgather_big(slots):
    n = len(slots)

    def body(*refs):
        outs = refs[n:2 * n]
        send, recv = refs[2 * n:]
        x, y, c, chips = _place()
        kme = 2 * x + y
        sib = (x, y, 1 - c)
        for a in range(n):
            for r, chip in enumerate(chips):
                for s, d in _pieces(outs[a].at[kme, c], outs[a].at[kme, c], 2):
                    _remote(s, d, send.at[6 * a + r], recv.at[6 * a + r], (*chip, c)).start()
        for a in range(n):
            for r, chip in enumerate(chips):
                kp = 2 * chip[0] + chip[1]
                _remote(outs[a].at[kp, c], outs[a].at[kp, c], send.at[6 * a + r], recv.at[6 * a + r],
                        (*chip, c)).wait_recv()
                for s, d in _pieces(outs[a].at[kp, c], outs[a].at[kp, c], 2):
                    _remote(s, d, send.at[6 * a + 3 + r], recv.at[6 * a + 3 + r], sib).start()
        for a in range(n):
            for r, chip in enumerate(chips):
                kp = 2 * chip[0] + chip[1]
                _remote(outs[a].at[kp, 1 - c], outs[a].at[kp, 1 - c], send.at[6 * a + 3 + r], recv.at[6 * a + 3 + r],
                        sib).wait_recv()
        for a in range(n):
            for r, chip in enumerate(chips):
                kp = 2 * chip[0] + chip[1]
                _remote(outs[a].at[kme, c], outs[a].at[kme, c], send.at[6 * a + r], recv.at[6 * a + r],
                        (*chip, c)).wait_send()
                _remote(outs[a].at[kp, c], outs[a].at[kp, c], send.at[6 * a + 3 + r], recv.at[6 * a + 3 + r],
                        sib).wait_send()

    return pl.pallas_call(
        body, name="allgather_weights", in_specs=[ANY] * n, out_specs=[ANY] * n,
        out_shape=[jax.ShapeDtypeStruct(s.shape, s.dtype) for s in slots],
        input_output_aliases={a: a for a in range(n)},
        scratch_shapes=[pltpu.SemaphoreType.DMA((6 * n,)), pltpu.SemaphoreType.DMA((6 * n,))])(*slots)


def _allgather_small(shards):
    n = len(shards)

    def body(*refs):
        ins, outs = refs[:n], refs[n:2 * n]
        send, recv, loc = refs[2 * n:]
        x, y, c, chips = _place()
        kme = 2 * x + y
        local = [pltpu.make_async_copy(ins[a], outs[a].at[kme], loc.at[a]) for a in range(n)]
        for cp in local:
            cp.start()
        cps = [_remote(ins[a], outs[a].at[kme], send.at[3 * a + r], recv.at[3 * a + r], (*chip, c))
               for a in range(n) for r, chip in enumerate(chips)]
        for cp in cps:
            cp.start()
        for a in range(n):
            for r, chip in enumerate(chips):
                kp = 2 * chip[0] + chip[1]
                _remote(ins[a], outs[a].at[kp], send.at[3 * a + r], recv.at[3 * a + r], (*chip, c)).wait_recv()
        for cp in cps:
            cp.wait_send()
        for cp in local:
            cp.wait()

    return pl.pallas_call(
        body, name="allgather_small", in_specs=[ANY] * n, out_specs=[ANY] * n,
        out_shape=[jax.ShapeDtypeStruct((4, *s.shape), s.dtype) for s in shards],
        scratch_shapes=[pltpu.SemaphoreType.DMA((3 * n,)), pltpu.SemaphoreType.DMA((3 * n,)),
                        pltpu.SemaphoreType.DMA((n,))])(*shards)


def _swap_halves(parts):
    n = len(parts)

    def body(*refs):
        ins, got = refs[:n], refs[n:2 * n]
        send, recv = refs[2 * n:]
        x, y, c, _ = _place()
        sib = (x, y, 1 - c)
        for a in range(n):
            for s, d in _pieces(ins[a].at[:, 1 - c], got[a], 1):
                _remote(s, d, send.at[a], recv.at[a], sib).start()
        for a in range(n):
            _remote(ins[a].at[:, 1 - c], got[a], send.at[a], recv.at[a], sib).wait()

    return pl.pallas_call(
        body, name="grad_swap_halves", in_specs=[ANY] * n, out_specs=[ANY] * n,
        out_shape=[jax.ShapeDtypeStruct((4, *p.shape[2:]), p.dtype) for p in parts],
        scratch_shapes=[pltpu.SemaphoreType.DMA((n,)), pltpu.SemaphoreType.DMA((n,))])(*parts)


def _scatter_chips(sums):
    n = len(sums)

    def body(*refs):
        ins, got = refs[:n], refs[n:2 * n]
        send, recv = refs[2 * n:]
        x, y, c, chips = _place()
        for a in range(n):
            for r, chip in enumerate(chips):
                for s, d in _pieces(ins[a].at[2 * chip[0] + chip[1]], got[a].at[r], 2):
                    _remote(s, d, send.at[3 * a + r], recv.at[3 * a + r], (*chip, c)).start()
        for a in range(n):
            for r, chip in enumerate(chips):
                _remote(ins[a].at[2 * chip[0] + chip[1]], got[a].at[r], send.at[3 * a + r], recv.at[3 * a + r],
                        (*chip, c)).wait()

    return pl.pallas_call(
        body, name="grad_scatter_chips", in_specs=[ANY] * n, out_specs=[ANY] * n,
        out_shape=[jax.ShapeDtypeStruct((3, *s.shape[1:]), s.dtype) for s in sums],
        scratch_shapes=[pltpu.SemaphoreType.DMA((3 * n,)), pltpu.SemaphoreType.DMA((3 * n,))])(*sums)


def _join_halves(totals):
    n = len(totals)

    def body(*refs):
        outs = refs[n:2 * n]
        send, recv = refs[2 * n:]
        x, y, c, _ = _place()
        sib = (x, y, 1 - c)
        for a in range(n):
            for s, d in _pieces(outs[a].at[c], outs[a].at[c], 4):
                _remote(s, d, send.at[a], recv.at[a], sib).start()
        for a in range(n):
            _remote(outs[a].at[1 - c], outs[a].at[1 - c], send.at[a], recv.at[a], sib).wait_recv()
            _remote(outs[a].at[c], outs[a].at[c], send.at[a], recv.at[a], sib).wait_send()

    return pl.pallas_call(
        body, name="grad_join_halves", in_specs=[ANY] * n, out_specs=[ANY] * n,
        out_shape=[jax.ShapeDtypeStruct(t.shape, t.dtype) for t in totals],
        input_output_aliases={a: a for a in range(n)},
        scratch_shapes=[pltpu.SemaphoreType.DMA((n,)), pltpu.SemaphoreType.DMA((n,))])(*totals)


def _allreduce_small(packed):
    rows = packed.shape[0]
    half = rows // 2

    def body(in_ref, out_ref, q_ref, s_ref, t_ref, send, recv):
        x, y, c, chips = _place()
        sib = (x, y, 1 - c)
        mine = pl.ds(pl.multiple_of(c * half, 8), half)
        theirs = pl.ds(pl.multiple_of((1 - c) * half, 8), half)
        first = _remote(in_ref.at[theirs], q_ref, send.at[0], recv.at[0], sib)
        first.start()
        first.wait()
        s_ref[...] = in_ref[mine, :] + q_ref[...]
        cps = [_remote(s_ref, t_ref.at[r], send.at[1 + r], recv.at[1 + r], (*chip, c)) for r, chip in enumerate(chips)]
        for cp in cps:
            cp.start()
        for cp in cps:
            cp.wait()
        out_ref[mine, :] = (s_ref[...] + t_ref[2]) + (t_ref[0] + t_ref[1])
        last = _remote(out_ref.at[mine], out_ref.at[mine], send.at[4], recv.at[4], sib)
        last.start()
        _remote(out_ref.at[theirs], out_ref.at[theirs], send.at[4], recv.at[4], sib).wait_recv()
        last.wait_send()

    vm = pl.BlockSpec(memory_space=pltpu.VMEM)
    return pl.pallas_call(
        body, name="allreduce_small", in_specs=[vm], out_specs=vm, out_shape=jax.ShapeDtypeStruct(packed.shape, F32),
        scratch_shapes=[pltpu.VMEM((half, LANES), F32), pltpu.VMEM((half, LANES), F32),
                        pltpu.VMEM((3, half, LANES), F32), pltpu.SemaphoreType.DMA((5,)), pltpu.SemaphoreType.DMA((5,))],
        compiler_params=pltpu.CompilerParams(vmem_limit_bytes=VMEM_LIMIT))(packed)


PACK_ROWS = 16


def _pack(arrs):
    parts, total = [], 0
    for a in arrs:
        flat = a.reshape(-1)
        rows = -(-flat.shape[0] // (LANES * PACK_ROWS)) * PACK_ROWS
        parts.append(jnp.pad(flat, (0, rows * LANES - flat.shape[0])).reshape(rows, LANES))
        total += rows
    return jnp.concatenate(parts, axis=0)


def _unpack(packed, shapes):
    out, row = [], 0
    for shp in shapes:
        size = math.prod(shp)
        rows = -(-size // (LANES * PACK_ROWS)) * PACK_ROWS
        out.append(packed[row:row + rows].reshape(-1)[:size].reshape(shp))
        row += rows
    return out


def kernel(x, norm_mix_g, norm_ffn_g, a_w_in, a_g_v, a_w_s, a_b_s, a_w_out, b_w_in, b_a_re, b_a_im, b_log_dt, b_b_re, b_b_im, b_c_re, b_c_im, b_d, b_w_glu, f_w_up, f_conv_w, f_conv_b, f_w_down, final_g, loss_target, m_norm_mix_g, m_norm_ffn_g, m_a_w_in, m_a_g_v, m_a_w_s, m_a_b_s, m_a_w_out, m_b_w_in, m_b_a_re, m_b_a_im, m_b_log_dt, m_b_b_re, m_b_b_im, m_b_c_re, m_b_c_im, m_b_d, m_b_w_glu, m_f_w_up, m_f_conv_w, m_f_conv_b, m_f_w_down, m_final_g, v_norm_mix_g, v_norm_ffn_g, v_a_w_in, v_a_g_v, v_a_w_s, v_a_b_s, v_a_w_out, v_b_w_in, v_b_a_re, v_b_a_im, v_b_log_dt, v_b_b_re, v_b_b_im, v_b_c_re, v_b_c_im, v_b_d, v_b_w_glu, v_f_w_up, v_f_conv_w, v_f_conv_b, v_f_w_down, v_final_g):
    w = dict(norm_mix_g=norm_mix_g, norm_ffn_g=norm_ffn_g, a_w_in=a_w_in, a_g_v=a_g_v, a_w_s=a_w_s, a_b_s=a_b_s,
             a_w_out=a_w_out, b_w_in=b_w_in, b_a_re=b_a_re, b_a_im=b_a_im, b_log_dt=b_log_dt, b_b_re=b_b_re,
             b_b_im=b_b_im, b_c_re=b_c_re, b_c_im=b_c_im, b_d=b_d, b_w_glu=b_w_glu, f_w_up=f_w_up, f_conv_w=f_conv_w,
             f_conv_b=f_conv_b, f_w_down=f_w_down, final_g=final_g)
    mom = dict(norm_mix_g=m_norm_mix_g, norm_ffn_g=m_norm_ffn_g, a_w_in=m_a_w_in, a_g_v=m_a_g_v, a_w_s=m_a_w_s,
               a_b_s=m_a_b_s, a_w_out=m_a_w_out, b_w_in=m_b_w_in, b_a_re=m_b_a_re, b_a_im=m_b_a_im,
               b_log_dt=m_b_log_dt, b_b_re=m_b_b_re, b_b_im=m_b_b_im, b_c_re=m_b_c_re, b_c_im=m_b_c_im, b_d=m_b_d,
               b_w_glu=m_b_w_glu, f_w_up=m_f_w_up, f_conv_w=m_f_conv_w, f_conv_b=m_f_conv_b, f_w_down=m_f_w_down,
               final_g=m_final_g)
    var = dict(norm_mix_g=v_norm_mix_g, norm_ffn_g=v_norm_ffn_g, a_w_in=v_a_w_in, a_g_v=v_a_g_v, a_w_s=v_a_w_s,
               a_b_s=v_a_b_s, a_w_out=v_a_w_out, b_w_in=v_b_w_in, b_a_re=v_b_a_re, b_a_im=v_b_a_im,
               b_log_dt=v_b_log_dt, b_b_re=v_b_b_re, b_b_im=v_b_b_im, b_c_re=v_b_c_re, b_c_im=v_b_c_im, b_d=v_b_d,
               b_w_glu=v_b_w_glu, f_w_up=v_f_w_up, f_conv_w=v_f_conv_w, f_conv_b=v_f_conv_b, f_w_down=v_f_w_down,
               final_g=v_final_g)

    rows, d = x.shape[1], x.shape[2]
    depth = norm_mix_g.shape[0]
    kchip = 2 * lax.axis_index("x") + lax.axis_index("y")
    big_names = list(BIG)
    dims = {n: _full_dims(BIG[n], w[n].shape) for n in big_names}

    keys = [(n, l) for n in big_names for l in range(w[n].shape[0])]
    slots = [_cast_halves("cast_" + n, w[n], BIG[n], l) for n, l in keys]
    gathered = dict(zip(keys, _allgather_big(slots)))
    bd_all, cw_all = _allgather_small([b_d, f_conv_w.reshape(-1, f_conv_w.shape[-1])])
    bd_full = jnp.swapaxes(bd_all, 0, 1).reshape(b_d.shape[0], -1)
    cw_full = jnp.transpose(cw_all.reshape(4, *f_conv_w.shape), (1, 2, 0, 3)).reshape(depth, f_conv_w.shape[1], -1)

    pgrad = {}
    sgrad = {}

    def mm(name, a, wn, layer, out_dtype, residual=None, split=False):
        return _mm_x_w(name, a, gathered[wn, layer], BIG[wn], *dims[wn], out_dtype, residual, split)

    def mm_t(name, dy, wn, layer, out_dtype, split=False):
        return _mm_dy_wt(name, dy, gathered[wn, layer], BIG[wn], *dims[wn], out_dtype, split)

    def mm_g(name, xa, dy, wn, layer, split=False):
        pgrad[wn, layer] = _mm_xt_dy(name, xa, dy, BIG[wn], *dims[wn], split)

    e = d
    nb = e // LANES
    heads = e // SGU_GROUP
    h = x[0]
    saved = []
    for i in range(depth):
        j = i // 2
        gm = norm_mix_g[i:i + 1]
        hn = _rms_fwd("rms_mix_fwd", h, gm)
        if i % 2 == 0:
            pre = mm("sgu_in", hn, 'a_w_in', j, BF16)
            bsx = jnp.broadcast_to(a_b_s[j][:, :, None], (heads, CHUNK, LANES))
            us = _sgu_mix_fwd("sgu_mix_fwd", pre, a_g_v[j:j + 1], a_w_s[j], bsx)
            h_mid = mm("sgu_out", us, 'a_w_out', j, F32, residual=h)
            mix = dict(h=h, hn=hn, pre=pre, us=us, bsx=bsx)
        else:
            groups = b_a_re.shape[1]
            rep = lambda t: jnp.repeat(t, SSM_GROUP, axis=1)
            lr, li = b_a_re[j], b_a_im[j]
            ldt = jnp.broadcast_to(b_log_dt[j][:, None], lr.shape)
            bflat = lambda t: t.reshape(groups, SSM_STATE * SSM_GROUP)
            disc_in = (lr, li, ldt, rep(lr), rep(li), rep(ldt), bflat(b_b_re[j]), bflat(b_b_im[j]))
            abr, abi, bbr, bbi = _s5_disc("s5_disc", *disc_in)
            shape_b = (groups, SSM_STATE, SSM_GROUP)
            bre = _to_blockdiag_b(bbr.reshape(shape_b), nb).astype(BF16)
            bim = _to_blockdiag_b(bbi.reshape(shape_b), nb).astype(BF16)
            crt = _to_blockdiag_ct(b_c_re[j], nb).astype(BF16)
            cit = _to_blockdiag_ct(b_c_im[j], nb).astype(BF16)
            ar_row, ai_row = abr.reshape(1, -1), abi.reshape(1, -1)
            dd = bd_full[j:j + 1]
            u = mm("s5_in", hn, 'b_w_in', j, F32)
            yv, gy = _s5_fwd("s5_fwd", u, bre, bim, crt, cit, ar_row, ai_row, dd)
            gg = mm("s5_glu", gy, 'b_w_glu', j, BF16)
            h_mid = _glu_fwd("glu_fwd", gg, h)
            mix = dict(h=h, hn=hn, u=u, y=yv, gy=gy, gg=gg, disc_in=disc_in, mats=(bre, bim, crt, cit, ar_row, ai_row, dd))
        gf = norm_ffn_g[i:i + 1]
        hn2 = _rms_fwd("rms_ffn_fwd", h_mid, gf)
        z = mm("ffn_up", hn2, 'f_w_up', i, BF16, split=True)
        cw = jnp.swapaxes(cw_full[i].reshape(cw_full.shape[1], 2, -1), 0, 1)
        cb = f_conv_b[i].reshape(2, 1, -1)
        act = _ffn_act_fwd("ffn_act_fwd", z, cw, cb)
        h_out = mm("ffn_down", act, 'f_w_down', i, F32, residual=h_mid)
        saved.append((mix, dict(h=h_mid, hn=hn2, z=z, act=act, cw=cw, cb=cb)))
        h = h_out

    dh, g_final, loss_vec = _loss_head(h, final_g.reshape(1, d), loss_target[0])
    loss = lax.psum(jnp.sum(loss_vec), ("x", "y", "c"))
    sgrad['final_g'] = g_final.reshape(d)

    g_mix, g_ffn = [None] * depth, [None] * depth
    g_cw, g_cb = [None] * depth, [None] * depth
    sg = {k: [None] * (depth // 2) for k in ('a_g_v', 'a_w_s', 'a_b_s')}
    bg = {k: [None] * (depth // 2) for k in ('b_a_re', 'b_a_im', 'b_log_dt', 'b_b_re', 'b_b_im', 'b_c_re', 'b_c_im', 'b_d')}
    for i in reversed(range(depth)):
        j = i // 2
        mix, ffn = saved[i]
        d_act = mm_t("ffn_down_dx", dh, 'f_w_down', i, BF16)
        mm_g("ffn_down_dw", ffn['act'], dh, 'f_w_down', i)
        dz, dcw, dcb = _ffn_act_bwd("ffn_act_bwd", ffn['z'], d_act, ffn['cw'], ffn['cb'])
        g_cw[i], g_cb[i] = jnp.swapaxes(dcw, 0, 1).reshape(dcw.shape[1], -1), dcb.reshape(1, -1)
        mm_g("ffn_up_dw", ffn['hn'], dz, 'f_w_up', i, split=True)
        dhn = mm_t("ffn_up_dx", dz, 'f_w_up', i, F32, split=True)
        dh, g_ffn[i] = _rms_bwd("rms_ffn_bwd", ffn['h'], norm_ffn_g[i:i + 1], dhn, dh)
        if i % 2 == 0:
            dus = mm_t("sgu_out_dx", dh, 'a_w_out', j, BF16)
            mm_g("sgu_out_dw", mix['us'], dh, 'a_w_out', j)
            dpre, dws, dbs, dgv = _sgu_mix_bwd("sgu_mix_bwd", mix['pre'], dus, a_g_v[j:j + 1], a_w_s[j], mix['bsx'])
            sg['a_w_s'][j], sg['a_b_s'][j], sg['a_g_v'][j] = dws, dbs[:, :, 0], dgv[0]
            mm_g("sgu_in_dw", mix['hn'], dpre, 'a_w_in', j)
            dhn = mm_t("sgu_in_dx", dpre, 'a_w_in', j, F32)
        else:
            dgg = _glu_bwd("glu_bwd", mix['gg'], dh)
            mm_g("s5_glu_dw", mix['gy'], dgg, 'b_w_glu', j)
            dgy = mm_t("s5_glu_dx", dgg, 'b_w_glu', j, BF16)
            du, dbr, dbi, dcr, dci, dar, dai, ddd = _s5_bwd("s5_bwd", mix['u'], mix['y'], dgy, *mix['mats'])
            groups = b_a_re.shape[1]
            flat = lambda t: _from_blockdiag_b(t, nb).reshape(groups, SSM_STATE * SSM_GROUP)
            sel = jnp.repeat(jnp.eye(SSM_STATE, dtype=F32), SSM_GROUP, axis=0)
            dlr, dli, dldt, dbre, dbim = _s5_disc_bwd(
                "s5_disc_bwd", *mix['disc_in'], dar.reshape(groups, SSM_STATE), dai.reshape(groups, SSM_STATE),
                flat(dbr), flat(dbi), sel)
            bg['b_a_re'][j], bg['b_a_im'][j], bg['b_log_dt'][j] = dlr, dli, dldt[:, 0]
            bg['b_b_re'][j] = dbre.reshape(groups, SSM_STATE, SSM_GROUP)
            bg['b_b_im'][j] = dbim.reshape(groups, SSM_STATE, SSM_GROUP)
            bg['b_c_re'][j], bg['b_c_im'][j] = _from_blockdiag_ct(dcr, nb), _from_blockdiag_ct(dci, nb)
            bg['b_d'][j] = ddd[0]
            mm_g("s5_in_dw", mix['hn'], du, 'b_w_in', j)
            dhn = mm_t("s5_in_dx", du, 'b_w_in', j, F32)
        dh, g_mix[i] = _rms_bwd("rms_mix_bwd", mix['h'], norm_mix_g[i:i + 1], dhn, dh)
    grad_x = dh[None]

    sgrad['norm_mix_g'] = jnp.concatenate(g_mix, axis=0)
    sgrad['norm_ffn_g'] = jnp.concatenate(g_ffn, axis=0)
    sgrad['f_conv_w'] = jnp.stack(g_cw)
    sgrad['f_conv_b'] = jnp.concatenate(g_cb, axis=0)
    for k, v_ in list(sg.items()) + list(bg.items()):
        sgrad[k] = jnp.stack(v_)

    total = _allreduce_small(_pack([sgrad[n] for n in SMALL]))
    full_shapes = [sgrad[n].shape for n in SMALL]
    gsmall = dict(zip(SMALL, _unpack(total, full_shapes)))
    for n, axis in CHIP_SHARDED_SMALL.items():
        width = w[n].shape[axis]
        gsmall[n] = lax.dynamic_slice_in_dim(gsmall[n], kchip * width, width, axis=axis)
    pk = lambda t: _pack([t[n] for n in SMALL])
    gpacked = pk(gsmall)
    dpk, mpk, vpk = _adam_small(pk(w), pk(mom), pk(var), gpacked)
    shard_shapes = [w[n].shape for n in SMALL]
    out_g = dict(gsmall)
    out_d = dict(zip(SMALL, _unpack(dpk, shard_shapes)))
    out_m = dict(zip(SMALL, _unpack(mpk, shard_shapes)))
    out_v = dict(zip(SMALL, _unpack(vpk, shard_shapes)))

    parts = [pgrad[k] for k in keys]
    got = _swap_halves(parts)
    sums = [_add2("grad_chip_sum", p, g) for p, g in zip(parts, got)]
    recvd = _scatter_chips(sums)
    gfull = _join_halves([_add4("grad_total", s, r) for s, r in zip(sums, recvd)])
    stacked = {n: [lax.empty(w[n].shape, F32) for _ in range(4)] for n in big_names}
    for (n, l), gf_ in zip(keys, gfull):
        stacked[n] = _adam_big("adam_" + n, w[n], mom[n], var[n], gf_, BIG[n], l, stacked[n])
    for n in big_names:
        out_g[n], out_d[n], out_m[n], out_v[n] = stacked[n]

    return (loss, grad_x, *[out_g[n] for n in W_NAMES], *[out_d[n] for n in W_NAMES],
            *[out_m[n] for n in W_NAMES], *[out_v[n] for n in W_NAMES])
```

```python
import functools
import itertools
import math

import jax
import jax.numpy as jnp
from jax import lax
from jax.experimental import pallas as pl
from jax.experimental.pallas import tpu as pltpu

F32, BF16 = jnp.float32, jnp.bfloat16
MESH = pl.DeviceIdType.MESH

CHUNK = 128
SEG = CHUNK + 4
SGU_GROUP = 128
SSM_GROUP = 16
SSM_STATE = 64
EPS = 1e-6
LANES = 128
GROUPS_PER_BLOCK = LANES // SSM_GROUP
STATE_BLOCKS = SSM_STATE // SSM_GROUP
VMEM_LIMIT = 52 * 1024 * 1024

ADAM_LR, ADAM_B1, ADAM_B2, ADAM_EPS, ADAM_WD, ADAM_STEP = 0.001, 0.9, 0.999, 1e-08, 0.01, 10

W_NAMES = ['norm_mix_g', 'norm_ffn_g', 'a_w_in', 'a_g_v', 'a_w_s', 'a_b_s', 'a_w_out', 'b_w_in', 'b_a_re', 'b_a_im',
           'b_log_dt', 'b_b_re', 'b_b_im', 'b_c_re', 'b_c_im', 'b_d', 'b_w_glu', 'f_w_up', 'f_conv_w', 'f_conv_b',
           'f_w_down', 'final_g']
BIG = {'a_w_in': 'col', 'a_w_out': 'row', 'b_w_in': 'row', 'b_w_glu': 'col', 'f_w_up': 'col', 'f_w_down': 'row'}
SMALL = [n for n in W_NAMES if n not in BIG]
CHIP_SHARDED_SMALL = {'b_d': 1, 'f_conv_w': 2}


def _tile(n, pref, align):
    t = min(n, pref)
    t -= t % align
    while t >= align:
        if n % t == 0:
            return t
        t -= align
    return n


def _params(*sem):
    return pltpu.CompilerParams(dimension_semantics=sem, vmem_limit_bytes=VMEM_LIMIT)


def _gelu(x):
    c = math.sqrt(2.0 / math.pi)
    return 0.5 * x * (1.0 + jnp.tanh(c * (x + 0.044715 * x * x * x)))


def _gelu_grad(x):
    c = math.sqrt(2.0 / math.pi)
    t = jnp.tanh(c * (x + 0.044715 * x * x * x))
    return 0.5 * (1.0 + t) + 0.5 * x * (1.0 - t * t) * c * (1.0 + 3.0 * 0.044715 * x * x)


def _half_shape(kind, shard_shape):
    _, r, c = shard_shape
    return (r // 2, c) if kind == 'col' else (r, c // 2)


def _full_dims(kind, shard_shape):
    _, r, c = shard_shape
    return (r, 4 * c) if kind == 'col' else (4 * r, c)


def _gspec(kind, kdim, ndim, tr, tc, rc):
    if kind == 'col':
        nr, nc = (kdim // 2) // tr, (ndim // 4) // tc

        def imap(*g):
            rb, cb = rc(*g)
            return (cb // nc, rb // nr, rb % nr, cb % nc)
    else:
        nr, nc = (kdim // 4) // tr, (ndim // 2) // tc

        def imap(*g):
            rb, cb = rc(*g)
            return (rb // nr, cb // nc, rb % nr, cb % nc)
    return pl.BlockSpec((None, None, tr, tc), imap)


def _act_spec(rows_blk, cols_blk, ncol_half, at):
    if ncol_half is None:
        return pl.BlockSpec((rows_blk, cols_blk), at)

    def imap(*g):
        rb, cb = at(*g)
        return (cb // ncol_half, rb, cb % ncol_half)
    return pl.BlockSpec((None, rows_blk, cols_blk), imap)


def _wtiles(kind, kdim, ndim):
    if kind == 'col':
        return _tile(kdim // 2, 1024, LANES), _tile(ndim // 4, 1408, LANES)
    return _tile(kdim // 4, 1408, LANES), _tile(ndim // 2, 1024, LANES)


_DIMS = {'nn': (((1,), (0,)), ((), ())), 'nt': (((1,), (1,)), ((), ())), 'tn': (((0,), (0,)), ((), ()))}


def _matmul(name, mode, a, b, grid, a_spec, b_spec, out_shape, out_spec, acc_shape, extras=(), extra_specs=(),
            epilogue=None, aliases=None):
    nk = grid[2]
    dims = _DIMS[mode]
    n_extra = len(extras)

    def body(a_ref, b_ref, *rest):
        extra_refs, o_ref = rest[:n_extra], rest[n_extra]
        prod = lax.dot_general(a_ref[...].astype(BF16), b_ref[...].astype(BF16), dims, preferred_element_type=F32)

        def finish(r):
            if epilogue is not None:
                r = epilogue(r, *[e[...] for e in extra_refs])
            o_ref[...] = r.astype(o_ref.dtype)

        if nk == 1:
            finish(prod)
            return
        acc_ref = rest[n_extra + 1]
        kk = pl.program_id(2)

        @pl.when(kk == 0)
        def _():
            acc_ref[...] = prod

        @pl.when(kk > 0)
        def _():
            acc_ref[...] += prod

        @pl.when(kk == nk - 1)
        def _():
            finish(acc_ref[...])

    scratch = [pltpu.VMEM(acc_shape, F32)] if nk > 1 else []
    return pl.pallas_call(
        body, name=name, grid=grid, in_specs=[a_spec, b_spec, *extra_specs], out_specs=out_spec, out_shape=out_shape,
        scratch_shapes=scratch, input_output_aliases=aliases or {},
        compiler_params=_params("parallel", "parallel", "arbitrary"))(a, b, *extras)


def _mm_x_w(name, a, wg, kind, kdim, ndim, out_dtype, residual=None, split=False):
    rows = a.shape[0]
    tk, tn = _wtiles(kind, kdim, ndim)
    tm = _tile(rows, 1024, 16)
    grid = (rows // tm, ndim // tn, kdim // tk)
    extras, especs, epi = (), (), None
    if residual is not None:
        extras, especs = (residual,), (pl.BlockSpec((tm, tn), lambda i, j, k: (i, j)),)
        epi = lambda r, res: r + res
    out_shape = (2, rows, ndim // 2) if split else (rows, ndim)
    return _matmul(name, 'nn', a, wg, grid, pl.BlockSpec((tm, tk), lambda i, j, k: (i, k)),
                   _gspec(kind, kdim, ndim, tk, tn, lambda i, j, k: (k, j)),
                   jax.ShapeDtypeStruct(out_shape, out_dtype),
                   _act_spec(tm, tn, (ndim // 2) // tn if split else None, lambda i, j, k: (i, j)),
                   (tm, tn), extras, especs, epi)


def _mm_dy_wt(name, dy, wg, kind, kdim, ndim, out_dtype, split=False, deps=()):
    rows = dy.shape[-2]
    tn, tk = _wtiles(kind, kdim, ndim)
    tm = _tile(rows, 1024, 16)
    grid = (rows // tm, kdim // tn, ndim // tk)
    return _matmul(name, 'nt', dy, wg, grid,
                   _act_spec(tm, tk, (ndim // 2) // tk if split else None, lambda i, j, k: (i, k)),
                   _gspec(kind, kdim, ndim, tn, tk, lambda i, j, k: (j, k)),
                   jax.ShapeDtypeStruct((rows, kdim), out_dtype), pl.BlockSpec((tm, tn), lambda i, j, k: (i, j)),
                   (tm, tn), extras=tuple(deps), extra_specs=(ANY,) * len(deps))


def _mm_xt_dy(name, xa, dy, kind, kdim, ndim, split=False):
    rows = xa.shape[0]
    tm, tn = _wtiles(kind, kdim, ndim)
    tl = _tile(rows, 1024, 16)
    grid = (kdim // tm, ndim // tn, rows // tl)
    rdim, cdim = (kdim // 2, ndim // 4) if kind == 'col' else (kdim // 4, ndim // 2)
    return _matmul(name, 'tn', xa, dy, grid, pl.BlockSpec((tl, tm), lambda i, j, k: (k, i)),
                   _act_spec(tl, tn, (ndim // 2) // tn if split else None, lambda i, j, k: (k, j)),
                   jax.ShapeDtypeStruct((4, 2, rdim, cdim), BF16),
                   _gspec(kind, kdim, ndim, tm, tn, lambda i, j, k: (i, j)), (tm, tn))


def _rms_fwd(name, h, g):
    rows, d = h.shape
    tm = _tile(rows, 256, 16)

    def body(h_ref, g_ref, o_ref):
        x = h_ref[...]
        r = lax.rsqrt(jnp.mean(x * x, axis=-1, keepdims=True) + EPS)
        o_ref[...] = (x * r * g_ref[...]).astype(o_ref.dtype)

    return pl.pallas_call(
        body, name=name, grid=(rows // tm,),
        in_specs=[pl.BlockSpec((tm, d), lambda i: (i, 0)), pl.BlockSpec((1, d), lambda i: (0, 0))],
        out_specs=pl.BlockSpec((tm, d), lambda i: (i, 0)), out_shape=jax.ShapeDtypeStruct((rows, d), BF16),
        compiler_params=_params("parallel"))(h, g)


def _rms_bwd(name, h, g, dhn, dres):
    rows, d = h.shape
    tm = _tile(rows, 256, 16)

    def body(h_ref, g_ref, dy_ref, dres_ref, dh_ref, dg_ref):
        x = h_ref[...]
        r = lax.rsqrt(jnp.mean(x * x, axis=-1, keepdims=True) + EPS)
        xh = x * r
        dy = dy_ref[...].astype(F32)
        gy = dy * g_ref[...]
        dh_ref[...] = dres_ref[...] + r * (gy - xh * jnp.mean(gy * xh, axis=-1, keepdims=True))
        part = jnp.sum(dy * xh, axis=0, keepdims=True)

        @pl.when(pl.program_id(0) == 0)
        def _():
            dg_ref[...] = part

        @pl.when(pl.program_id(0) > 0)
        def _():
            dg_ref[...] += part

    row = pl.BlockSpec((tm, d), lambda i: (i, 0))
    vec = pl.BlockSpec((1, d), lambda i: (0, 0))
    return pl.pallas_call(
        body, name=name, grid=(rows // tm,), in_specs=[row, vec, row, row], out_specs=[row, vec],
        out_shape=[jax.ShapeDtypeStruct((rows, d), F32), jax.ShapeDtypeStruct((1, d), F32)],
        compiler_params=_params("arbitrary"))(h, g, dhn, dres)


def _loss_head(h, g, target):
    rows, d = h.shape
    tm = _tile(rows, 256, 16)

    def body(h_ref, g_ref, t_ref, dh_ref, dg_ref, loss_ref):
        x = h_ref[...]
        r = lax.rsqrt(jnp.mean(x * x, axis=-1, keepdims=True) + EPS)
        xh = x * r
        err = xh * g_ref[...] - t_ref[...]
        dy = err * (1.0 / d)
        gy = dy * g_ref[...]
        dh_ref[...] = r * (gy - xh * jnp.mean(gy * xh, axis=-1, keepdims=True))
        part = jnp.sum(dy * xh, axis=0, keepdims=True)
        sq = jnp.sum(err * err, axis=0, keepdims=True) * (0.5 / d)

        @pl.when(pl.program_id(0) == 0)
        def _():
            dg_ref[...] = part
            loss_ref[...] = sq

        @pl.when(pl.program_id(0) > 0)
        def _():
            dg_ref[...] += part
            loss_ref[...] += sq

    row = pl.BlockSpec((tm, d), lambda i: (i, 0))
    vec = pl.BlockSpec((1, d), lambda i: (0, 0))
    return pl.pallas_call(
        body, name="loss_head", grid=(rows // tm,), in_specs=[row, vec, row], out_specs=[row, vec, vec],
        out_shape=[jax.ShapeDtypeStruct((rows, d), F32), jax.ShapeDtypeStruct((1, d), F32),
                   jax.ShapeDtypeStruct((1, d), F32)],
        compiler_params=_params("arbitrary"))(h, g, target)


def _shift_down(cur, prev8, first, k):
    rows = cur.shape[0]
    rolled = pltpu.roll(cur, k, axis=0)
    idx = lax.broadcasted_iota(jnp.int32, cur.shape, 0)
    prev8 = jnp.where(first, 0.0, prev8)
    out = rolled
    for r in range(k):
        out = jnp.where(idx == r, prev8[8 - k + r:8 - k + r + 1, :], out)
    del rows
    return out


def _shift_up(cur, next8, last, k):
    rows = cur.shape[0]
    rolled = pltpu.roll(cur, rows - k, axis=0)
    idx = lax.broadcasted_iota(jnp.int32, cur.shape, 0)
    next8 = jnp.where(last, 0.0, next8)
    out = rolled
    for r in range(k):
        out = jnp.where(idx == rows - k + r, next8[r:r + 1, :], out)
    return out


def _conv_acc(z, zprev, first, w, b):
    z1 = _shift_down(z, zprev, first, 1)
    z2 = _shift_down(z, zprev, first, 2)
    return b + w[2:3, :] * z + w[1:2, :] * z1 + w[0:1, :] * z2, z1, z2


def _ffn_tiles(rows, f):
    return _tile(rows, 512, 16), _tile(f, 512, LANES)


def _ffn_act_fwd(name, z3, cw3, cb3):
    _, rows, f = z3.shape
    tm, tc = _ffn_tiles(rows, f)
    hb = tm // 8

    def body(z_ref, zp_ref, w_ref, b_ref, o_ref):
        first = pl.program_id(0) == 0
        gate, _, _ = _conv_acc(z_ref[0].astype(F32), zp_ref[0].astype(F32), first, w_ref[0], b_ref[0])
        val, _, _ = _conv_acc(z_ref[1].astype(F32), zp_ref[1].astype(F32), first, w_ref[1], b_ref[1])
        o_ref[...] = (gate * jax.nn.sigmoid(gate) * val).astype(o_ref.dtype)

    return pl.pallas_call(
        body, name=name, grid=(rows // tm, f // tc),
        in_specs=[pl.BlockSpec((2, tm, tc), lambda i, j: (0, i, j)),
                  pl.BlockSpec((2, 8, tc), lambda i, j: (0, jnp.maximum(i * hb - 1, 0), j)),
                  pl.BlockSpec((2, 3, tc), lambda i, j: (0, 0, j)), pl.BlockSpec((2, 1, tc), lambda i, j: (0, 0, j))],
        out_specs=pl.BlockSpec((tm, tc), lambda i, j: (i, j)), out_shape=jax.ShapeDtypeStruct((rows, f), BF16),
        compiler_params=_params("parallel", "parallel"))(z3, z3, cw3, cb3)


def _gate_grads(d_a, acc_g, acc_v):
    sig = jax.nn.sigmoid(acc_g)
    return d_a * acc_v * sig * (1.0 + acc_g * (1.0 - sig)), d_a * acc_g * sig


def _ffn_act_bwd(name, z3, da, cw3, cb3):
    _, rows, f = z3.shape
    tm, tc = _ffn_tiles(rows, f)
    hb = tm // 8
    nrow = rows // tm

    def body(z_ref, zp_ref, zn_ref, da_ref, dan_ref, w_ref, b_ref, dz_ref, dcw_ref, dcb_ref):
        i = pl.program_id(1)
        first, last = i == 0, i == nrow - 1
        w, b = (w_ref[0], w_ref[1]), (b_ref[0], b_ref[1])
        z = (z_ref[0].astype(F32), z_ref[1].astype(F32))
        acc, taps = [], []
        for hf in range(2):
            a_h, z1, z2 = _conv_acc(z[hf], zp_ref[hf].astype(F32), first, w[hf], b[hf])
            acc.append(a_h)
            taps.append((z2, z1, z[hf]))
        dacc = _gate_grads(da_ref[...].astype(F32), acc[0], acc[1])
        acc_n = [_conv_acc(zn_ref[hf].astype(F32), z[hf][tm - 8:tm, :], False, w[hf], b[hf])[0] for hf in range(2)]
        dacc_n = _gate_grads(dan_ref[...].astype(F32), acc_n[0], acc_n[1])
        for hf in range(2):
            d = dacc[hf]
            d1 = _shift_up(d, dacc_n[hf], last, 1)
            d2 = _shift_up(d, dacc_n[hf], last, 2)
            dz_ref[hf] = (w[hf][2:3, :] * d + w[hf][1:2, :] * d1 + w[hf][0:1, :] * d2).astype(dz_ref.dtype)
        sums_w = [[jnp.sum(dacc[hf] * t, axis=0, keepdims=True) for t in taps[hf]] for hf in range(2)]
        sums_b = [jnp.sum(dacc[hf], axis=0, keepdims=True) for hf in range(2)]

        @pl.when(first)
        def _():
            for hf in range(2):
                for k in range(3):
                    dcw_ref[hf, k:k + 1, :] = sums_w[hf][k]
                dcb_ref[hf] = sums_b[hf]

        @pl.when(i > 0)
        def _():
            for hf in range(2):
                for k in range(3):
                    dcw_ref[hf, k:k + 1, :] += sums_w[hf][k]
                dcb_ref[hf] += sums_b[hf]

    nxt = lambda i: jnp.minimum((i + 1) * hb, rows // 8 - 1)
    wsp = pl.BlockSpec((2, 3, tc), lambda j, i: (0, 0, j))
    bsp = pl.BlockSpec((2, 1, tc), lambda j, i: (0, 0, j))
    cur = pl.BlockSpec((2, tm, tc), lambda j, i: (0, i, j))
    return pl.pallas_call(
        body, name=name, grid=(f // tc, nrow),
        in_specs=[cur, pl.BlockSpec((2, 8, tc), lambda j, i: (0, jnp.maximum(i * hb - 1, 0), j)),
                  pl.BlockSpec((2, 8, tc), lambda j, i: (0, nxt(i), j)), pl.BlockSpec((tm, tc), lambda j, i: (i, j)),
                  pl.BlockSpec((8, tc), lambda j, i: (nxt(i), j)), wsp, bsp],
        out_specs=[cur, wsp, bsp],
        out_shape=[jax.ShapeDtypeStruct((2, rows, f), BF16), jax.ShapeDtypeStruct((2, 3, f), F32),
                   jax.ShapeDtypeStruct((2, 1, f), F32)],
        compiler_params=_params("parallel", "arbitrary"))(z3, z3, z3, da, da, cw3, cb3)


def _glu_fwd(name, gg, h):
    rows, d2 = gg.shape
    d = d2 // 2
    tm, tc = _tile(rows, 512, 16), _tile(d, 1024, LANES)
    nd = d // tc

    def body(a_ref, b_ref, h_ref, o_ref):
        o_ref[...] = h_ref[...] + a_ref[...].astype(F32) * jax.nn.sigmoid(b_ref[...].astype(F32))

    return pl.pallas_call(
        body, name=name, grid=(rows // tm, nd),
        in_specs=[pl.BlockSpec((tm, tc), lambda i, j: (i, j)), pl.BlockSpec((tm, tc), lambda i, j: (i, j + nd)),
                  pl.BlockSpec((tm, tc), lambda i, j: (i, j))],
        out_specs=pl.BlockSpec((tm, tc), lambda i, j: (i, j)), out_shape=jax.ShapeDtypeStruct((rows, d), F32),
        compiler_params=_params("parallel", "parallel"))(gg, gg, h)


def _glu_bwd(name, gg, dh):
    rows, d2 = gg.shape
    d = d2 // 2
    tm, tc = _tile(rows, 512, 16), _tile(d, 1024, LANES)
    nd = d // tc

    def body(s_ref, o_ref, dh_ref, out_ref):
        is_a = pl.program_id(1) < nd
        me = s_ref[...].astype(F32)
        other = o_ref[...].astype(F32)
        g = dh_ref[...]
        sig_o = jax.nn.sigmoid(other)
        sig_m = jax.nn.sigmoid(me)
        out_ref[...] = jnp.where(is_a, g * sig_o, g * other * sig_m * (1.0 - sig_m)).astype(out_ref.dtype)

    return pl.pallas_call(
        body, name=name, grid=(rows // tm, 2 * nd),
        in_specs=[pl.BlockSpec((tm, tc), lambda i, j: (i, j)),
                  pl.BlockSpec((tm, tc), lambda i, j: (i, (j + nd) % (2 * nd))),
                  pl.BlockSpec((tm, tc), lambda i, j: (i, j % nd))],
        out_specs=pl.BlockSpec((tm, tc), lambda i, j: (i, j)), out_shape=jax.ShapeDtypeStruct((rows, d2), BF16),
        compiler_params=_params("parallel", "parallel"))(gg, gg, dh)


def _sgu_common(pre_ref, gv_ref, e):
    u = _gelu(pre_ref[:, :e].astype(F32))
    v = _gelu(pre_ref[:, e:].astype(F32))
    r = lax.rsqrt(jnp.mean(v * v, axis=-1, keepdims=True) + EPS)
    vh = v * r
    return u, vh, r, (vh * gv_ref[...]).astype(BF16)


def _tril_bf16(ws_ref, hd):
    t = lax.broadcasted_iota(jnp.int32, (CHUNK, CHUNK), 0)
    s = lax.broadcasted_iota(jnp.int32, (CHUNK, CHUNK), 1)
    return jnp.where(s <= t, ws_ref[hd], 0.0).astype(BF16)


def _sgu_mix_fwd(name, pre, gv, ws, bsx):
    rows, e2 = pre.shape
    e = e2 // 2
    heads = e // SGU_GROUP
    tr = _tile(rows, 256, CHUNK)

    def body(pre_ref, gv_ref, ws_ref, bs_ref, o_ref):
        u, _, _, vn = _sgu_common(pre_ref, gv_ref, e)
        for hd in range(heads):
            wm = _tril_bf16(ws_ref, hd)
            cols = slice(hd * SGU_GROUP, (hd + 1) * SGU_GROUP)
            for ck in range(tr // CHUNK):
                rws = slice(ck * CHUNK, (ck + 1) * CHUNK)
                s = jnp.dot(wm, vn[rws, cols], preferred_element_type=F32) + bs_ref[hd]
                o_ref[rws, cols] = (u[rws, cols] * s).astype(o_ref.dtype)

    whole3 = pl.BlockSpec((heads, CHUNK, CHUNK), lambda i: (0, 0, 0))
    return pl.pallas_call(
        body, name=name, grid=(rows // tr,),
        in_specs=[pl.BlockSpec((tr, e2), lambda i: (i, 0)), pl.BlockSpec((1, e), lambda i: (0, 0)), whole3, whole3],
        out_specs=pl.BlockSpec((tr, e), lambda i: (i, 0)), out_shape=jax.ShapeDtypeStruct((rows, e), BF16),
        compiler_params=_params("parallel"))(pre, gv, ws, bsx)


def _sgu_mix_bwd(name, pre, dus, gv, ws, bsx):
    rows, e2 = pre.shape
    e = e2 // 2
    heads = e // SGU_GROUP
    tr = _tile(rows, 256, CHUNK)

    def body(pre_ref, dus_ref, gv_ref, ws_ref, bs_ref, dpre_ref, dws_ref, dbs_ref, dgv_ref, dvn_ref, du_ref):
        first = pl.program_id(0) == 0
        u, vh, r, vn = _sgu_common(pre_ref, gv_ref, e)
        ones = jnp.ones((SGU_GROUP, LANES), BF16)
        tt = lax.broadcasted_iota(jnp.int32, (CHUNK, CHUNK), 0)
        ss = lax.broadcasted_iota(jnp.int32, (CHUNK, CHUNK), 1)
        for hd in range(heads):
            wm = _tril_bf16(ws_ref, hd)
            cols = slice(hd * SGU_GROUP, (hd + 1) * SGU_GROUP)
            dw = jnp.zeros((CHUNK, CHUNK), F32)
            db = jnp.zeros((CHUNK, LANES), F32)
            for ck in range(tr // CHUNK):
                rws = slice(ck * CHUNK, (ck + 1) * CHUNK)
                vblk = vn[rws, cols]
                s = jnp.dot(wm, vblk, preferred_element_type=F32) + bs_ref[hd]
                d_us = dus_ref[rws, cols].astype(F32)
                du_ref[rws, cols] = d_us * s
                ds = (d_us * u[rws, cols]).astype(BF16)
                dvn_ref[rws, cols] = lax.dot_general(wm, ds, _DIMS['tn'], preferred_element_type=F32)
                dw = dw + lax.dot_general(ds, vblk, _DIMS['nt'], preferred_element_type=F32)
                db = db + jnp.dot(ds, ones, preferred_element_type=F32)
            dw = jnp.where(ss <= tt, dw, 0.0)

            @pl.when(first)
            def _():
                dws_ref[hd] = dw
                dbs_ref[hd] = db

            @pl.when(jnp.logical_not(first))
            def _():
                dws_ref[hd] += dw
                dbs_ref[hd] += db

        dvn = dvn_ref[...]
        part = jnp.sum(dvn * vh, axis=0, keepdims=True)

        @pl.when(first)
        def _():
            dgv_ref[...] = part

        @pl.when(jnp.logical_not(first))
        def _():
            dgv_ref[...] += part

        gy = dvn * gv_ref[...]
        dv = r * (gy - vh * jnp.mean(gy * vh, axis=-1, keepdims=True))
        dpre_ref[:, :e] = (du_ref[...] * _gelu_grad(pre_ref[:, :e].astype(F32))).astype(dpre_ref.dtype)
        dpre_ref[:, e:] = (dv * _gelu_grad(pre_ref[:, e:].astype(F32))).astype(dpre_ref.dtype)

    whole3 = pl.BlockSpec((heads, CHUNK, CHUNK), lambda i: (0, 0, 0))
    vec = pl.BlockSpec((1, e), lambda i: (0, 0))
    return pl.pallas_call(
        body, name=name, grid=(rows // tr,),
        in_specs=[pl.BlockSpec((tr, e2), lambda i: (i, 0)), pl.BlockSpec((tr, e), lambda i: (i, 0)), vec, whole3, whole3],
        out_specs=[pl.BlockSpec((tr, e2), lambda i: (i, 0)), whole3, whole3, vec],
        out_shape=[jax.ShapeDtypeStruct((rows, e2), BF16), jax.ShapeDtypeStruct((heads, CHUNK, CHUNK), F32),
                   jax.ShapeDtypeStruct((heads, CHUNK, LANES), F32), jax.ShapeDtypeStruct((1, e), F32)],
        scratch_shapes=[pltpu.VMEM((tr, e), F32), pltpu.VMEM((tr, e), F32)],
        compiler_params=_params("arbitrary"))(pre, dus, gv, ws, bsx)


def _disc_a(lr, li, ldt):
    dt = jnp.exp(ldt)
    mag = jnp.exp(dt * lr)
    return mag * jnp.cos(dt * li), mag * jnp.sin(dt * li)


def _disc_b(lr, li, ldt, br, bi):
    ar, ai = _disc_a(lr, li, ldt)
    den = lr * lr + li * li
    qr = ((ar - 1.0) * lr + ai * li) / den
    qi = (ai * lr - (ar - 1.0) * li) / den
    return qr * br - qi * bi, qr * bi + qi * br


def _s5_disc(name, lr, li, ldt, lrx, lix, ldtx, br, bi):
    def body(lr_ref, li_ref, ldt_ref, lrx_ref, lix_ref, ldtx_ref, br_ref, bi_ref, ar_ref, ai_ref, bbr_ref, bbi_ref):
        ar_ref[...], ai_ref[...] = _disc_a(lr_ref[...], li_ref[...], ldt_ref[...])
        bbr_ref[...], bbi_ref[...] = _disc_b(lrx_ref[...], lix_ref[...], ldtx_ref[...], br_ref[...], bi_ref[...])

    small = jax.ShapeDtypeStruct(lr.shape, F32)
    wide = jax.ShapeDtypeStruct(br.shape, F32)
    return pl.pallas_call(body, name=name, out_shape=[small, small, wide, wide],
                          compiler_params=pltpu.CompilerParams(vmem_limit_bytes=VMEM_LIMIT))(
        lr, li, ldt, lrx, lix, ldtx, br, bi)


def _s5_disc_bwd(name, lr, li, ldt, lrx, lix, ldtx, br, bi, dar, dai, dbbr, dbbi, sel):
    def body(lr_ref, li_ref, ldt_ref, lrx_ref, lix_ref, ldtx_ref, br_ref, bi_ref, dar_ref, dai_ref, dbbr_ref,
             dbbi_ref, sel_ref, dlr_ref, dli_ref, dldt_ref, dbr_ref, dbi_ref):
        _, vjp_a = jax.vjp(_disc_a, lr_ref[...], li_ref[...], ldt_ref[...])
        g_lr, g_li, g_ldt = vjp_a((dar_ref[...], dai_ref[...]))
        _, vjp_b = jax.vjp(_disc_b, lrx_ref[...], lix_ref[...], ldtx_ref[...], br_ref[...], bi_ref[...])
        x_lr, x_li, x_ldt, g_br, g_bi = vjp_b((dbbr_ref[...], dbbi_ref[...]))
        fold = lambda t: jnp.dot(t, sel_ref[...], precision=lax.Precision.HIGHEST, preferred_element_type=F32)
        dlr_ref[...] = g_lr + fold(x_lr)
        dli_ref[...] = g_li + fold(x_li)
        dldt_ref[...] = jnp.sum(g_ldt + fold(x_ldt), axis=1, keepdims=True)
        dbr_ref[...] = g_br
        dbi_ref[...] = g_bi

    small = jax.ShapeDtypeStruct(lr.shape, F32)
    wide = jax.ShapeDtypeStruct(br.shape, F32)
    return pl.pallas_call(body, name=name,
                          out_shape=[small, small, jax.ShapeDtypeStruct((lr.shape[0], 1), F32), wide, wide],
                          compiler_params=pltpu.CompilerParams(vmem_limit_bytes=VMEM_LIMIT))(
        lr, li, ldt, lrx, lix, ldtx, br, bi, dar, dai, dbbr, dbbi, sel)


def _cmul(ar, ai, br, bi):
    return ar * br - ai * bi, ar * bi + ai * br


def _pow_seg(ar, ai):
    res, base, n = None, (ar, ai), SEG
    while n:
        if n & 1:
            res = base if res is None else _cmul(*res, *base)
        n >>= 1
        if n:
            base = _cmul(*base, *base)
    return res


def _scan_forward(hr_ref, hi_ref, er_ref, ei_ref, sr_ref, si_ref, ar, ai, nck):
    arb, aib = jnp.broadcast_to(ar, (nck, LANES)), jnp.broadcast_to(ai, (nck, LANES))

    def intra(t, carry):
        sr, si = carry
        slab = pl.ds(t, nck, stride=SEG)
        nr = arb * sr - aib * si + hr_ref[slab, :]
        ni = arb * si + aib * sr + hi_ref[slab, :]
        hr_ref[slab, :] = nr
        hi_ref[slab, :] = ni
        return nr, ni

    zero = jnp.zeros((nck, LANES), F32)
    er_ref[...], ei_ref[...] = lax.fori_loop(0, SEG, intra, (zero, zero), unroll=4)
    pcr, pci = _pow_seg(ar, ai)
    sr_ref[0:1, :] = jnp.zeros((1, LANES), F32)
    si_ref[0:1, :] = jnp.zeros((1, LANES), F32)
    for ck in range(nck - 1):
        pr, pi = sr_ref[ck:ck + 1, :], si_ref[ck:ck + 1, :]
        sr_ref[ck + 1:ck + 2, :] = pcr * pr - pci * pi + er_ref[ck:ck + 1, :]
        si_ref[ck + 1:ck + 2, :] = pcr * pi + pci * pr + ei_ref[ck:ck + 1, :]
    s_r, s_i = sr_ref[...], si_ref[...]

    def fix(t, carry):
        pr, pi = carry
        slab = pl.ds(t, nck, stride=SEG)
        hr_ref[slab, :] = hr_ref[slab, :] + (pr * s_r - pi * s_i)
        hi_ref[slab, :] = hi_ref[slab, :] + (pr * s_i + pi * s_r)
        return _cmul(pr, pi, arb, aib)

    lax.fori_loop(0, SEG, fix, (arb, aib), unroll=4)


def _scan_backward(gr_ref, gi_ref, hr_ref, hi_ref, er_ref, ei_ref, sr_ref, si_ref, ar, ai, nck):
    arb, aib = jnp.broadcast_to(ar, (nck, LANES)), jnp.broadcast_to(-ai, (nck, LANES))

    def intra(k, carry):
        sr, si = carry
        slab = pl.ds(SEG - 1 - k, nck, stride=SEG)
        nr = arb * sr - aib * si + gr_ref[slab, :]
        ni = arb * si + aib * sr + gi_ref[slab, :]
        gr_ref[slab, :] = nr
        gi_ref[slab, :] = ni
        return nr, ni

    zero = jnp.zeros((nck, LANES), F32)
    er_ref[...], ei_ref[...] = lax.fori_loop(0, SEG, intra, (zero, zero), unroll=4)
    pcr, pci = _pow_seg(ar, -ai)
    sr_ref[nck - 1:nck, :] = jnp.zeros((1, LANES), F32)
    si_ref[nck - 1:nck, :] = jnp.zeros((1, LANES), F32)
    for ck in range(nck - 1, 0, -1):
        pr, pi = sr_ref[ck:ck + 1, :], si_ref[ck:ck + 1, :]
        sr_ref[ck - 1:ck, :] = pcr * pr - pci * pi + er_ref[ck:ck + 1, :]
        si_ref[ck - 1:ck, :] = pcr * pi + pci * pr + ei_ref[ck:ck + 1, :]
    s_r, s_i = sr_ref[...], si_ref[...]
    last = pl.ds(SEG - 1, nck, stride=SEG)
    row = lax.broadcasted_iota(jnp.int32, (nck, LANES), 0)
    hp_r = jnp.where(row == 0, 0.0, pltpu.roll(hr_ref[last, :], 1, axis=0)) if nck > 1 else zero
    hp_i = jnp.where(row == 0, 0.0, pltpu.roll(hi_ref[last, :], 1, axis=0)) if nck > 1 else zero

    def settle(t, pr, pi, h_r, h_i):
        slab = pl.ds(t, nck, stride=SEG)
        g_r = gr_ref[slab, :] + (pr * s_r - pi * s_i)
        g_i = gi_ref[slab, :] + (pr * s_i + pi * s_r)
        gr_ref[slab, :] = g_r
        gi_ref[slab, :] = g_i
        return g_r * h_r + g_i * h_i, g_i * h_r - g_r * h_i

    def fix(k, carry):
        pr, pi, acr, aci = carry
        t = SEG - 1 - k
        prev = pl.ds(t - 1, nck, stride=SEG)
        d_r, d_i = settle(t, pr, pi, hr_ref[prev, :], hi_ref[prev, :])
        nr, ni = _cmul(pr, pi, arb, aib)
        return nr, ni, acr + d_r, aci + d_i

    pr, pi, acr, aci = lax.fori_loop(0, SEG - 1, fix, (arb, aib, zero, zero), unroll=4)
    d_r, d_i = settle(0, pr, pi, hp_r, hp_i)
    return jnp.sum(acr + d_r, axis=0, keepdims=True), jnp.sum(aci + d_i, axis=0, keepdims=True)


def _s5_fill_states(u_ref, br_ref, bi_ref, hr_ref, hi_ref, rows):
    ub = u_ref[...].astype(BF16)
    hr_ref[0:rows, :] = jnp.dot(ub, br_ref[...], preferred_element_type=F32)
    hi_ref[0:rows, :] = jnp.dot(ub, bi_ref[...], preferred_element_type=F32)
    pad = jnp.zeros((hr_ref.shape[0] - rows, LANES), F32)
    hr_ref[rows:, :] = pad
    hi_ref[rows:, :] = pad


def _s5_specs(rows, e):
    sb = STATE_BLOCKS
    chan = pl.BlockSpec((rows, LANES), lambda j: (0, j // sb))
    bmat = pl.BlockSpec((None, LANES, LANES), lambda j: (j // sb, 0, j % sb))
    cmat = pl.BlockSpec((None, LANES, LANES), lambda j: (j // sb, j % sb, 0))
    avec = pl.BlockSpec((1, LANES), lambda j: (0, j))
    dvec = pl.BlockSpec((1, LANES), lambda j: (0, j // sb))
    return chan, bmat, cmat, avec, dvec


def _s5_fwd(name, u, bre, bim, crt, cit, ar, ai, dd):
    rows, e = u.shape
    nck = rows // CHUNK
    nsteps = (e // LANES) * STATE_BLOCKS
    chan, bmat, cmat, avec, dvec = _s5_specs(rows, e)

    def body(u_ref, br_ref, bi_ref, cr_ref, ci_ref, ar_ref, ai_ref, dd_ref, y_ref, gy_ref,
             hr_ref, hi_ref, er_ref, ei_ref, sr_ref, si_ref, acc_ref):
        j = pl.program_id(0) % STATE_BLOCKS
        _s5_fill_states(u_ref, br_ref, bi_ref, hr_ref, hi_ref, rows)
        _scan_forward(hr_ref, hi_ref, er_ref, ei_ref, sr_ref, si_ref, ar_ref[...], ai_ref[...], nck)
        contrib = (jnp.dot(hr_ref[0:rows, :].astype(BF16), cr_ref[...], preferred_element_type=F32)
                   - jnp.dot(hi_ref[0:rows, :].astype(BF16), ci_ref[...], preferred_element_type=F32))

        @pl.when(j == 0)
        def _():
            acc_ref[...] = dd_ref[...] * u_ref[...] + contrib

        @pl.when(j > 0)
        def _():
            acc_ref[...] += contrib

        @pl.when(j == STATE_BLOCKS - 1)
        def _():
            y = acc_ref[...]
            y_ref[...] = y.astype(y_ref.dtype)
            gy_ref[...] = _gelu(y).astype(gy_ref.dtype)

    flat = pltpu.VMEM((rows, LANES), F32)
    big = pltpu.VMEM((nck * SEG, LANES), F32)
    small = pltpu.VMEM((nck, LANES), F32)
    out = jax.ShapeDtypeStruct((rows, e), BF16)
    return pl.pallas_call(
        body, name=name, grid=(nsteps,), in_specs=[chan, bmat, bmat, cmat, cmat, avec, avec, dvec],
        out_specs=[chan, chan], out_shape=[out, out], scratch_shapes=[big, big, small, small, small, small, flat],
        compiler_params=_params("arbitrary"))(u, bre, bim, crt, cit, ar, ai, dd)


def _s5_bwd(name, u, y, dgy, bre, bim, crt, cit, ar, ai, dd):
    rows, e = u.shape
    nb = e // LANES
    nck = rows // CHUNK
    nsteps = nb * STATE_BLOCKS
    chan, bmat, cmat, avec, dvec = _s5_specs(rows, e)

    def body(u_ref, y_ref, dgy_ref, br_ref, bi_ref, cr_ref, ci_ref, ar_ref, ai_ref, dd_ref,
             du_ref, dbr_ref, dbi_ref, dcr_ref, dci_ref, dar_ref, dai_ref, ddd_ref,
             hr_ref, hi_ref, gr_ref, gi_ref, er_ref, ei_ref, sr_ref, si_ref, acc_ref, dy_ref):
        j = pl.program_id(0) % STATE_BLOCKS
        _s5_fill_states(u_ref, br_ref, bi_ref, hr_ref, hi_ref, rows)
        _scan_forward(hr_ref, hi_ref, er_ref, ei_ref, sr_ref, si_ref, ar_ref[...], ai_ref[...], nck)

        @pl.when(j == 0)
        def _():
            dy0 = dgy_ref[...].astype(F32) * _gelu_grad(y_ref[...].astype(F32))
            dy_ref[...] = dy0
            ddd_ref[...] = jnp.sum(dy0 * u_ref[...], axis=0, keepdims=True)

        dyb = dy_ref[...].astype(BF16)
        pad = jnp.zeros((gr_ref.shape[0] - rows, LANES), F32)
        gr_ref[0:rows, :] = lax.dot_general(dyb, cr_ref[...], _DIMS['nt'], preferred_element_type=F32)
        gi_ref[0:rows, :] = -lax.dot_general(dyb, ci_ref[...], _DIMS['nt'], preferred_element_type=F32)
        gr_ref[rows:, :] = pad
        gi_ref[rows:, :] = pad
        dcr_ref[...] = lax.dot_general(hr_ref[0:rows, :].astype(BF16), dyb, _DIMS['tn'], preferred_element_type=F32)
        dci_ref[...] = -lax.dot_general(hi_ref[0:rows, :].astype(BF16), dyb, _DIMS['tn'], preferred_element_type=F32)
        dar_ref[...], dai_ref[...] = _scan_backward(gr_ref, gi_ref, hr_ref, hi_ref, er_ref, ei_ref, sr_ref, si_ref,
                                                    ar_ref[...], ai_ref[...], nck)
        ub = u_ref[...].astype(BF16)
        grb, gib = gr_ref[0:rows, :].astype(BF16), gi_ref[0:rows, :].astype(BF16)
        dbr_ref[...] = lax.dot_general(ub, grb, _DIMS['tn'], preferred_element_type=F32)
        dbi_ref[...] = lax.dot_general(ub, gib, _DIMS['tn'], preferred_element_type=F32)
        contrib = (lax.dot_general(grb, br_ref[...], _DIMS['nt'], preferred_element_type=F32)
                   + lax.dot_general(gib, bi_ref[...], _DIMS['nt'], preferred_element_type=F32))

        @pl.when(j == 0)
        def _():
            acc_ref[...] = dd_ref[...] * dy_ref[...] + contrib

        @pl.when(j > 0)
        def _():
            acc_ref[...] += contrib

        @pl.when(j == STATE_BLOCKS - 1)
        def _():
            du_ref[...] = acc_ref[...]

    flat = pltpu.VMEM((rows, LANES), F32)
    big = pltpu.VMEM((nck * SEG, LANES), F32)
    small = pltpu.VMEM((nck, LANES), F32)
    bshape = jax.ShapeDtypeStruct((nb, LANES, LANES * STATE_BLOCKS), F32)
    cshape = jax.ShapeDtypeStruct((nb, LANES * STATE_BLOCKS, LANES), F32)
    ashape = jax.ShapeDtypeStruct((1, nb * LANES * STATE_BLOCKS), F32)
    return pl.pallas_call(
        body, name=name, grid=(nsteps,),
        in_specs=[chan, chan, chan, bmat, bmat, cmat, cmat, avec, avec, dvec],
        out_specs=[chan, bmat, bmat, cmat, cmat, avec, avec, dvec],
        out_shape=[jax.ShapeDtypeStruct((rows, e), F32), bshape, bshape, cshape, cshape, ashape, ashape,
                   jax.ShapeDtypeStruct((1, e), F32)],
        scratch_shapes=[big, big, big, big, small, small, small, small, flat, flat],
        compiler_params=_params("arbitrary"))(u, y, dgy, bre, bim, crt, cit, ar, ai, dd)


def _to_blockdiag_b(bbar, nb):
    eye = jnp.eye(GROUPS_PER_BLOCK, dtype=bbar.dtype)
    t = jnp.einsum('bgpc,gh->bgchp', bbar.reshape(nb, GROUPS_PER_BLOCK, SSM_STATE, SSM_GROUP), eye)
    return t.reshape(nb, LANES, GROUPS_PER_BLOCK * SSM_STATE)


def _from_blockdiag_b(dmat, nb):
    eye = jnp.eye(GROUPS_PER_BLOCK, dtype=dmat.dtype)
    t = dmat.reshape(nb, GROUPS_PER_BLOCK, SSM_GROUP, GROUPS_PER_BLOCK, SSM_STATE)
    return jnp.einsum('bgchp,gh->bgpc', t, eye).reshape(nb * GROUPS_PER_BLOCK, SSM_STATE, SSM_GROUP)


def _to_blockdiag_ct(c, nb):
    eye = jnp.eye(GROUPS_PER_BLOCK, dtype=c.dtype)
    t = jnp.einsum('bgop,gh->bgpho', c.reshape(nb, GROUPS_PER_BLOCK, SSM_GROUP, SSM_STATE), eye)
    return t.reshape(nb, GROUPS_PER_BLOCK * SSM_STATE, LANES)


def _from_blockdiag_ct(dmat, nb):
    eye = jnp.eye(GROUPS_PER_BLOCK, dtype=dmat.dtype)
    t = dmat.reshape(nb, GROUPS_PER_BLOCK, SSM_STATE, GROUPS_PER_BLOCK, SSM_GROUP)
    return jnp.einsum('bgpho,gh->bgop', t, eye).reshape(nb * GROUPS_PER_BLOCK, SSM_GROUP, SSM_STATE)


ANY = pl.BlockSpec(memory_space=pl.ANY)


def _half_specs(kind, rdim, cdim, tr, tc, layer):
    nr, nc = rdim // tr, cdim // tc
    if kind == 'col':
        nat = pl.BlockSpec((None, tr, tc), lambda c, rb, cb: (layer, c * nr + rb, cb))
    else:
        nat = pl.BlockSpec((None, tr, tc), lambda c, rb, cb: (layer, rb, c * nc + cb))
    half = pl.BlockSpec((None, tr, tc), lambda c, rb, cb: (c, rb, cb))
    return nat, half


def _my_chip():
    return 2 * lax.axis_index("x") + lax.axis_index("y")


def _cast_halves(name, w, kind, layer):
    rdim, cdim = _half_shape(kind, w.shape)
    tr, tc = _tile(rdim, 512, 16), _tile(cdim, 1408, LANES)
    nat, _ = _half_specs(kind, rdim, cdim, tr, tc, layer)
    slot = pl.BlockSpec((None, None, tr, tc), lambda c, rb, cb: (_my_chip(), c, rb, cb))

    def body(w_ref, o_ref):
        o_ref[...] = w_ref[...].astype(o_ref.dtype)

    return pl.pallas_call(
        body, name=name, grid=(2, rdim // tr, cdim // tc), in_specs=[nat], out_specs=slot,
        out_shape=jax.ShapeDtypeStruct((4, 2, rdim, cdim), BF16),
        compiler_params=_params("parallel", "parallel", "parallel"))(w)


def _adam_math(w, g, m, v):
    m = ADAM_B1 * m + (1.0 - ADAM_B1) * g
    v = ADAM_B2 * v + (1.0 - ADAM_B2) * (g * g)
    m_hat = m / (1.0 - ADAM_B1 ** ADAM_STEP)
    v_hat = v / (1.0 - ADAM_B2 ** ADAM_STEP)
    delta = -ADAM_LR * (m_hat / (jnp.sqrt(v_hat) + ADAM_EPS) + ADAM_WD * w)
    return delta, m, v


def _adam_big(name, w, m, v, gfull, kind, layer, outs):
    rdim, cdim = _half_shape(kind, w.shape)
    tr, tc = _tile(rdim, 256, 8), _tile(cdim, 1408, LANES)
    nat, half = _half_specs(kind, rdim, cdim, tr, tc, layer)

    def body(w_ref, m_ref, v_ref, g_ref, *rest):
        go_ref, d_ref, mo_ref, vo_ref = rest[4:]
        g = g_ref[...]
        go_ref[...] = g
        d_ref[...], mo_ref[...], vo_ref[...] = _adam_math(w_ref[...], g, m_ref[...], v_ref[...])

    shape = jax.ShapeDtypeStruct(w.shape, F32)
    return pl.pallas_call(
        body, name=name, grid=(2, rdim // tr, cdim // tc), in_specs=[nat, nat, nat, half] + [ANY] * 4,
        out_specs=[nat, nat, nat, nat], out_shape=[shape, shape, shape, shape],
        input_output_aliases={4: 0, 5: 1, 6: 2, 7: 3},
        compiler_params=_params("parallel", "parallel", "parallel"))(w, m, v, gfull, *outs)


def _adam_small(w, m, v, g):
    rows = w.shape[0]
    tr = _tile(rows, 512, 8)
    spec = pl.BlockSpec((tr, LANES), lambda i: (i, 0))

    def body(w_ref, m_ref, v_ref, g_ref, d_ref, mo_ref, vo_ref):
        d_ref[...], mo_ref[...], vo_ref[...] = _adam_math(w_ref[...], g_ref[...], m_ref[...], v_ref[...])

    shape = jax.ShapeDtypeStruct(w.shape, F32)
    return pl.pallas_call(body, name="adam_small", grid=(rows // tr,), in_specs=[spec] * 4, out_specs=[spec] * 3,
                          out_shape=[shape] * 3, compiler_params=_params("parallel"))(w, m, v, g)


def _add2(name, part, got):
    cdim = part.shape[-1]
    a2, b2 = part.reshape(4, 2, -1, cdim), got.reshape(4, -1, cdim)
    rows = b2.shape[1]
    tr, tc = _tile(rows, 512, 16), _tile(cdim, 1408, LANES)
    mine = pl.BlockSpec((None, None, tr, tc), lambda k, i, j: (k, lax.axis_index("c"), i, j))
    spec = pl.BlockSpec((None, tr, tc), lambda k, i, j: (k, i, j))

    def body(a_ref, b_ref, o_ref):
        o_ref[...] = (a_ref[...].astype(F32) + b_ref[...].astype(F32)).astype(o_ref.dtype)

    out = pl.pallas_call(
        body, name=name, grid=(4, rows // tr, cdim // tc), in_specs=[mine, spec], out_specs=spec,
        out_shape=jax.ShapeDtypeStruct(b2.shape, BF16),
        compiler_params=_params("parallel", "parallel", "parallel"))(a2, b2)
    return out.reshape(got.shape)


def _add4(name, sums, recv):
    cdim = sums.shape[-1]
    s2 = sums.reshape(4, -1, cdim)
    r3 = recv.reshape(3, -1, cdim)
    rows = s2.shape[1]
    tr, tc = _tile(rows, 512, 16), _tile(cdim, 1408, LANES)
    own = pl.BlockSpec((None, tr, tc), lambda i, j: (_my_chip(), i, j))
    rspec = lambda k: pl.BlockSpec((None, tr, tc), lambda i, j: (k, i, j))
    slot = pl.BlockSpec((None, tr, tc), lambda i, j: (lax.axis_index("c"), i, j))

    def body(o_ref, x_ref, y_ref, d_ref, out_ref):
        out_ref[...] = ((o_ref[...].astype(F32) + d_ref[...].astype(F32))
                        + (x_ref[...].astype(F32) + y_ref[...].astype(F32)))

    out = pl.pallas_call(
        body, name=name, grid=(rows // tr, cdim // tc), in_specs=[own, rspec(0), rspec(1), rspec(2)], out_specs=slot,
        out_shape=jax.ShapeDtypeStruct((2, rows, cdim), F32),
        compiler_params=_params("parallel", "parallel"))(s2, r3, r3, r3)
    return out.reshape(2, *sums.shape[1:])


def _place():
    x, y, c = lax.axis_index("x"), lax.axis_index("y"), lax.axis_index("c")
    chips = [(1 - x, y), (x, 1 - y), (1 - x, 1 - y)]
    return x, y, c, chips


def _remote(src, dst, send, recv, to):
    return pltpu.make_async_remote_copy(src_ref=src, dst_ref=dst, send_sem=send, recv_sem=recv, device_id=to,
                                        device_id_type=MESH)


def _pieces(src, dst, bands):
    lead, rows = src.shape[:-2], src.shape[-2]
    band = rows // bands
    out = []
    for idx in itertools.product(*[range(dim) for dim in lead]):
        for q in range(bands):
            sl = (*idx, pl.ds(q * band, band))
            out.append((src.at[sl], dst.at[sl]))
    return out


HBM = pl.BlockSpec(memory_space=pltpu.HBM)
SEM = pl.BlockSpec(memory_space=pltpu.SEMAPHORE)
EFFECT = pltpu.SideEffectType.DATAFLOW_SIDE_EFFECTING


def _split_start(name, bufs, ncopy, plan, deps=()):
    nb, nd = len(bufs), len(deps)

    def body(*refs):
        ins, send, recv, token = refs[:nb], refs[nb + nd], refs[nb + nd + 1], refs[2 * nb + nd + 2]
        for k, (src, dst, _, to, bands) in enumerate(plan(ins)):
            for s, d in _pieces(src, dst, bands):
                _remote(s, d, send.at[k], recv.at[k], to).start()
        token[...] = jnp.zeros_like(token)

    res = pl.pallas_call(
        body, name=name, in_specs=[HBM] * nb + [ANY] * nd,
        out_specs=[SEM, SEM] + [HBM] * nb + [pl.BlockSpec(memory_space=pltpu.VMEM)],
        out_shape=[pltpu.SemaphoreType.DMA((ncopy,)), pltpu.SemaphoreType.DMA((ncopy,))]
        + [pltpu.HBM(b.shape, b.dtype) for b in bufs] + [jax.ShapeDtypeStruct((8, LANES), F32)],
        input_output_aliases={a: a + 2 for a in range(nb)},
        compiler_params=pltpu.CompilerParams(has_side_effects=EFFECT))(
        *[pltpu.with_memory_space_constraint(b, pltpu.HBM) for b in bufs], *deps)
    return res[0], res[1], list(res[2:2 + nb]), res[2 + nb]


def _split_wait(name, send, recv, bufs, plan, after):
    nb = len(bufs)

    def body(*refs):
        ins, send_sem, recv_sem = refs[:nb], refs[nb], refs[nb + 1]
        for k, (src, dst, landing, to, _) in enumerate(plan(ins)):
            _remote(src, dst, send_sem.at[k], recv_sem.at[k], to).wait_send()
            _remote(src, landing, send_sem.at[k], recv_sem.at[k], to).wait_recv()

    return pl.pallas_call(
        body, name=name, in_specs=[HBM] * nb + [SEM, SEM, ANY], out_specs=[HBM] * nb,
        out_shape=[pltpu.HBM(b.shape, b.dtype) for b in bufs], input_output_aliases={a: a for a in range(nb)},
        compiler_params=pltpu.CompilerParams(has_side_effects=EFFECT))(*bufs, send, recv, after)


def _gather_plan(n):
    def plan(refs):
        x, y, c, chips = _place()
        kme = 2 * x + y
        return [(refs[a].at[kme, c], refs[a].at[kme, c], refs[a].at[2 * chip[0] + chip[1], c], (*chip, c), 2)
                for a in range(n) for chip in chips]
    return plan


def _scatter_plan(n):
    def plan(refs):
        x, y, c, chips = _place()
        return [(refs[a].at[2 * chip[0] + chip[1]], refs[n + a].at[r], refs[n + a].at[r], (*chip, c), 2)
                for a in range(n) for r, chip in enumerate(chips)]
    return plan


def _forward_halves(slots):
    n = len(slots)

    def body(*refs):
        outs = refs[n:2 * n]
        send, recv = refs[2 * n:]
        x, y, c, chips = _place()
        sib = (x, y, 1 - c)
        for a in range(n):
            for r, chip in enumerate(chips):
                kp = 2 * chip[0] + chip[1]
                for s, d in _pieces(outs[a].at[kp, c], outs[a].at[kp, c], 2):
                    _remote(s, d, send.at[3 * a + r], recv.at[3 * a + r], sib).start()
        for a in range(n):
            for r, chip in enumerate(chips):
                kp = 2 * chip[0] + chip[1]
                _remote(outs[a].at[kp, 1 - c], outs[a].at[kp, 1 - c], send.at[3 * a + r], recv.at[3 * a + r],
                        sib).wait_recv()
                _remote(outs[a].at[kp, c], outs[a].at[kp, c], send.at[3 * a + r], recv.at[3 * a + r], sib).wait_send()

    return pl.pallas_call(
        body, name="allgather_forward", in_specs=[ANY] * n, out_specs=[ANY] * n,
        out_shape=[jax.ShapeDtypeStruct(s.shape, s.dtype) for s in slots],
        input_output_aliases={a: a for a in range(n)},
        scratch_shapes=[pltpu.SemaphoreType.DMA((3 * n,)), pltpu.SemaphoreType.DMA((3 * n,))])(*slots)


def _allgather_small(shards):
    n = len(shards)

    def body(*refs):
        ins, outs = refs[:n], refs[n:2 * n]
        send, recv, loc = refs[2 * n:]
        x, y, c, chips = _place()
        kme = 2 * x + y
        local = [pltpu.make_async_copy(ins[a], outs[a].at[kme], loc.at[a]) for a in range(n)]
        for cp in local:
            cp.start()
        cps = [_remote(ins[a], outs[a].at[kme], send.at[3 * a + r], recv.at[3 * a + r], (*chip, c))
               for a in range(n) for r, chip in enumerate(chips)]
        for cp in cps:
            cp.start()
        for a in range(n):
            for r, chip in enumerate(chips):
                kp = 2 * chip[0] + chip[1]
                _remote(ins[a], outs[a].at[kp], send.at[3 * a + r], recv.at[3 * a + r], (*chip, c)).wait_recv()
        for cp in cps:
            cp.wait_send()
        for cp in local:
            cp.wait()

    return pl.pallas_call(
        body, name="allgather_small", in_specs=[ANY] * n, out_specs=[ANY] * n,
        out_shape=[jax.ShapeDtypeStruct((4, *s.shape), s.dtype) for s in shards],
        scratch_shapes=[pltpu.SemaphoreType.DMA((3 * n,)), pltpu.SemaphoreType.DMA((3 * n,)),
                        pltpu.SemaphoreType.DMA((n,))])(*shards)


def _swap_halves(parts):
    n = len(parts)

    def body(*refs):
        ins, got = refs[:n], refs[n:2 * n]
        send, recv = refs[2 * n:]
        x, y, c, _ = _place()
        sib = (x, y, 1 - c)
        for a in range(n):
            for s, d in _pieces(ins[a].at[:, 1 - c], got[a], 1):
                _remote(s, d, send.at[a], recv.at[a], sib).start()
        for a in range(n):
            _remote(ins[a].at[:, 1 - c], got[a], send.at[a], recv.at[a], sib).wait()

    return pl.pallas_call(
        body, name="grad_swap_halves", in_specs=[ANY] * n, out_specs=[ANY] * n,
        out_shape=[jax.ShapeDtypeStruct((4, *p.shape[2:]), p.dtype) for p in parts],
        scratch_shapes=[pltpu.SemaphoreType.DMA((n,)), pltpu.SemaphoreType.DMA((n,))])(*parts)


def _join_halves(totals):
    n = len(totals)

    def body(*refs):
        outs = refs[n:2 * n]
        send, recv = refs[2 * n:]
        x, y, c, _ = _place()
        sib = (x, y, 1 - c)
        for a in range(n):
            for s, d in _pieces(outs[a].at[c], outs[a].at[c], 4):
                _remote(s, d, send.at[a], recv.at[a], sib).start()
        for a in range(n):
            _remote(outs[a].at[1 - c], outs[a].at[1 - c], send.at[a], recv.at[a], sib).wait_recv()
            _remote(outs[a].at[c], outs[a].at[c], send.at[a], recv.at[a], sib).wait_send()

    return pl.pallas_call(
        body, name="grad_join_halves", in_specs=[ANY] * n, out_specs=[ANY] * n,
        out_shape=[jax.ShapeDtypeStruct(t.shape, t.dtype) for t in totals],
        input_output_aliases={a: a for a in range(n)},
        scratch_shapes=[pltpu.SemaphoreType.DMA((n,)), pltpu.SemaphoreType.DMA((n,))])(*totals)


def _allreduce_small(packed):
    rows = packed.shape[0]
    half = rows // 2

    def body(in_ref, out_ref, q_ref, s_ref, t_ref, send, recv):
        x, y, c, chips = _place()
        sib = (x, y, 1 - c)
        mine = pl.ds(pl.multiple_of(c * half, 8), half)
        theirs = pl.ds(pl.multiple_of((1 - c) * half, 8), half)
        first = _remote(in_ref.at[theirs], q_ref, send.at[0], recv.at[0], sib)
        first.start()
        first.wait()
        s_ref[...] = in_ref[mine, :] + q_ref[...]
        cps = [_remote(s_ref, t_ref.at[r], send.at[1 + r], recv.at[1 + r], (*chip, c)) for r, chip in enumerate(chips)]
        for cp in cps:
            cp.start()
        for cp in cps:
            cp.wait()
        out_ref[mine, :] = (s_ref[...] + t_ref[2]) + (t_ref[0] + t_ref[1])
        last = _remote(out_ref.at[mine], out_ref.at[mine], send.at[4], recv.at[4], sib)
        last.start()
        _remote(out_ref.at[theirs], out_ref.at[theirs], send.at[4], recv.at[4], sib).wait_recv()
        last.wait_send()

    vm = pl.BlockSpec(memory_space=pltpu.VMEM)
    return pl.pallas_call(
        body, name="allreduce_small", in_specs=[vm], out_specs=vm, out_shape=jax.ShapeDtypeStruct(packed.shape, F32),
        scratch_shapes=[pltpu.VMEM((half, LANES), F32), pltpu.VMEM((half, LANES), F32),
                        pltpu.VMEM((3, half, LANES), F32), pltpu.SemaphoreType.DMA((5,)), pltpu.SemaphoreType.DMA((5,))],
        compiler_params=pltpu.CompilerParams(vmem_limit_bytes=VMEM_LIMIT))(packed)


PACK_ROWS = 16


def _pack(arrs):
    parts, total = [], 0
    for a in arrs:
        flat = a.reshape(-1)
        rows = -(-flat.shape[0] // (LANES * PACK_ROWS)) * PACK_ROWS
        parts.append(jnp.pad(flat, (0, rows * LANES - flat.shape[0])).reshape(rows, LANES))
        total += rows
    return jnp.concatenate(parts, axis=0)


def _unpack(packed, shapes):
    out, row = [], 0
    for shp in shapes:
        size = math.prod(shp)
        rows = -(-size // (LANES * PACK_ROWS)) * PACK_ROWS
        out.append(packed[row:row + rows].reshape(-1)[:size].reshape(shp))
        row += rows
    return out


def kernel(x, norm_mix_g, norm_ffn_g, a_w_in, a_g_v, a_w_s, a_b_s, a_w_out, b_w_in, b_a_re, b_a_im, b_log_dt, b_b_re, b_b_im, b_c_re, b_c_im, b_d, b_w_glu, f_w_up, f_conv_w, f_conv_b, f_w_down, final_g, loss_target, m_norm_mix_g, m_norm_ffn_g, m_a_w_in, m_a_g_v, m_a_w_s, m_a_b_s, m_a_w_out, m_b_w_in, m_b_a_re, m_b_a_im, m_b_log_dt, m_b_b_re, m_b_b_im, m_b_c_re, m_b_c_im, m_b_d, m_b_w_glu, m_f_w_up, m_f_conv_w, m_f_conv_b, m_f_w_down, m_final_g, v_norm_mix_g, v_norm_ffn_g, v_a_w_in, v_a_g_v, v_a_w_s, v_a_b_s, v_a_w_out, v_b_w_in, v_b_a_re, v_b_a_im, v_b_log_dt, v_b_b_re, v_b_b_im, v_b_c_re, v_b_c_im, v_b_d, v_b_w_glu, v_f_w_up, v_f_conv_w, v_f_conv_b, v_f_w_down, v_final_g):
    w = dict(norm_mix_g=norm_mix_g, norm_ffn_g=norm_ffn_g, a_w_in=a_w_in, a_g_v=a_g_v, a_w_s=a_w_s, a_b_s=a_b_s,
             a_w_out=a_w_out, b_w_in=b_w_in, b_a_re=b_a_re, b_a_im=b_a_im, b_log_dt=b_log_dt, b_b_re=b_b_re,
             b_b_im=b_b_im, b_c_re=b_c_re, b_c_im=b_c_im, b_d=b_d, b_w_glu=b_w_glu, f_w_up=f_w_up, f_conv_w=f_conv_w,
             f_conv_b=f_conv_b, f_w_down=f_w_down, final_g=final_g)
    mom = dict(norm_mix_g=m_norm_mix_g, norm_ffn_g=m_norm_ffn_g, a_w_in=m_a_w_in, a_g_v=m_a_g_v, a_w_s=m_a_w_s,
               a_b_s=m_a_b_s, a_w_out=m_a_w_out, b_w_in=m_b_w_in, b_a_re=m_b_a_re, b_a_im=m_b_a_im,
               b_log_dt=m_b_log_dt, b_b_re=m_b_b_re, b_b_im=m_b_b_im, b_c_re=m_b_c_re, b_c_im=m_b_c_im, b_d=m_b_d,
               b_w_glu=m_b_w_glu, f_w_up=m_f_w_up, f_conv_w=m_f_conv_w, f_conv_b=m_f_conv_b, f_w_down=m_f_w_down,
               final_g=m_final_g)
    var = dict(norm_mix_g=v_norm_mix_g, norm_ffn_g=v_norm_ffn_g, a_w_in=v_a_w_in, a_g_v=v_a_g_v, a_w_s=v_a_w_s,
               a_b_s=v_a_b_s, a_w_out=v_a_w_out, b_w_in=v_b_w_in, b_a_re=v_b_a_re, b_a_im=v_b_a_im,
               b_log_dt=v_b_log_dt, b_b_re=v_b_b_re, b_b_im=v_b_b_im, b_c_re=v_b_c_re, b_c_im=v_b_c_im, b_d=v_b_d,
               b_w_glu=v_b_w_glu, f_w_up=v_f_w_up, f_conv_w=v_f_conv_w, f_conv_b=v_f_conv_b, f_w_down=v_f_w_down,
               final_g=v_final_g)

    rows, d = x.shape[1], x.shape[2]
    depth = norm_mix_g.shape[0]
    kchip = 2 * lax.axis_index("x") + lax.axis_index("y")
    big_names = list(BIG)
    dims = {n: _full_dims(BIG[n], w[n].shape) for n in big_names}

    keys = [(n, l) for n in big_names for l in range(w[n].shape[0])]
    slots = {(n, l): _cast_halves("cast_" + n, w[n], BIG[n], l) for n, l in keys}

    def layer_keys(i):
        mixer = [('a_w_in', i // 2), ('a_w_out', i // 2)] if i % 2 == 0 else [('b_w_in', i // 2), ('b_w_glu', i // 2)]
        return mixer + [('f_w_up', i), ('f_w_down', i)]

    gathered, gather_waits, token = {}, [], None
    for i in range(depth):
        arrs = [slots[k] for k in layer_keys(i)]
        send, recv, thru, token = _split_start(f"allgather_start_{i}", arrs, 3 * len(arrs), _gather_plan(len(arrs)),
                                               deps=() if token is None else (token,))
        gather_waits.append((send, recv, thru))
    gather_after = token

    def gather_layer(i, after):
        send, recv, thru = gather_waits[i]
        landed = _split_wait(f"allgather_wait_{i}", send, recv, thru, _gather_plan(len(thru)), after)
        gathered.update(zip(layer_keys(i), _forward_halves(landed)))

    bd_all, cw_all = _allgather_small([b_d, f_conv_w.reshape(-1, f_conv_w.shape[-1])])
    bd_full = jnp.swapaxes(bd_all, 0, 1).reshape(b_d.shape[0], -1)
    cw_full = jnp.transpose(cw_all.reshape(4, *f_conv_w.shape), (1, 2, 0, 3)).reshape(depth, f_conv_w.shape[1], -1)

    pgrad = {}
    sgrad = {}

    def mm(name, a, wn, layer, out_dtype, residual=None, split=False):
        return _mm_x_w(name, a, gathered[wn, layer], BIG[wn], *dims[wn], out_dtype, residual, split)

    def mm_t(name, dy, wn, layer, out_dtype, split=False, deps=()):
        return _mm_dy_wt(name, dy, gathered[wn, layer], BIG[wn], *dims[wn], out_dtype, split, deps)

    def mm_g(name, xa, dy, wn, layer, split=False):
        pgrad[wn, layer] = _mm_xt_dy(name, xa, dy, BIG[wn], *dims[wn], split)

    e = d
    nb = e // LANES
    heads = e // SGU_GROUP
    h = x[0]
    saved = []
    for i in range(depth):
        j = i // 2
        gather_layer(i, gather_after if i == 0 else h)
        gm = norm_mix_g[i:i + 1]
        hn = _rms_fwd("rms_mix_fwd", h, gm)
        if i % 2 == 0:
            pre = mm("sgu_in", hn, 'a_w_in', j, BF16)
            bsx = jnp.broadcast_to(a_b_s[j][:, :, None], (heads, CHUNK, LANES))
            us = _sgu_mix_fwd("sgu_mix_fwd", pre, a_g_v[j:j + 1], a_w_s[j], bsx)
            h_mid = mm("sgu_out", us, 'a_w_out', j, F32, residual=h)
            mix = dict(h=h, hn=hn, pre=pre, us=us, bsx=bsx)
        else:
            groups = b_a_re.shape[1]
            rep = lambda t: jnp.repeat(t, SSM_GROUP, axis=1)
            lr, li = b_a_re[j], b_a_im[j]
            ldt = jnp.broadcast_to(b_log_dt[j][:, None], lr.shape)
            bflat = lambda t: t.reshape(groups, SSM_STATE * SSM_GROUP)
            disc_in = (lr, li, ldt, rep(lr), rep(li), rep(ldt), bflat(b_b_re[j]), bflat(b_b_im[j]))
            abr, abi, bbr, bbi = _s5_disc("s5_disc", *disc_in)
            shape_b = (groups, SSM_STATE, SSM_GROUP)
            bre = _to_blockdiag_b(bbr.reshape(shape_b), nb).astype(BF16)
            bim = _to_blockdiag_b(bbi.reshape(shape_b), nb).astype(BF16)
            crt = _to_blockdiag_ct(b_c_re[j], nb).astype(BF16)
            cit = _to_blockdiag_ct(b_c_im[j], nb).astype(BF16)
            ar_row, ai_row = abr.reshape(1, -1), abi.reshape(1, -1)
            dd = bd_full[j:j + 1]
            u = mm("s5_in", hn, 'b_w_in', j, F32)
            yv, gy = _s5_fwd("s5_fwd", u, bre, bim, crt, cit, ar_row, ai_row, dd)
            gg = mm("s5_glu", gy, 'b_w_glu', j, BF16)
            h_mid = _glu_fwd("glu_fwd", gg, h)
            mix = dict(h=h, hn=hn, u=u, y=yv, gy=gy, gg=gg, disc_in=disc_in, mats=(bre, bim, crt, cit, ar_row, ai_row, dd))
        gf = norm_ffn_g[i:i + 1]
        hn2 = _rms_fwd("rms_ffn_fwd", h_mid, gf)
        z = mm("ffn_up", hn2, 'f_w_up', i, BF16, split=True)
        cw = jnp.swapaxes(cw_full[i].reshape(cw_full.shape[1], 2, -1), 0, 1)
        cb = f_conv_b[i].reshape(2, 1, -1)
        act = _ffn_act_fwd("ffn_act_fwd", z, cw, cb)
        h_out = mm("ffn_down", act, 'f_w_down', i, F32, residual=h_mid)
        saved.append((mix, dict(h=h_mid, hn=hn2, z=z, act=act, cw=cw, cb=cb)))
        h = h_out

    dh, g_final, loss_vec = _loss_head(h, final_g.reshape(1, d), loss_target[0])
    loss = lax.psum(jnp.sum(loss_vec), ("x", "y", "c"))
    sgrad['final_g'] = g_final.reshape(d)

    g_mix, g_ffn = [None] * depth, [None] * depth
    g_cw, g_cb = [None] * depth, [None] * depth
    sg = {k: [None] * (depth // 2) for k in ('a_g_v', 'a_w_s', 'a_b_s')}
    bg = {k: [None] * (depth // 2) for k in ('b_a_re', 'b_a_im', 'b_log_dt', 'b_b_re', 'b_b_im', 'b_c_re', 'b_c_im', 'b_d')}
    scatters, scatter_token = {}, None
    for i in reversed(range(depth)):
        j = i // 2
        mix, ffn = saved[i]
        d_act = mm_t("ffn_down_dx", dh, 'f_w_down', i, BF16, deps=() if scatter_token is None else (scatter_token,))
        mm_g("ffn_down_dw", ffn['act'], dh, 'f_w_down', i)
        dz, dcw, dcb = _ffn_act_bwd("ffn_act_bwd", ffn['z'], d_act, ffn['cw'], ffn['cb'])
        g_cw[i], g_cb[i] = jnp.swapaxes(dcw, 0, 1).reshape(dcw.shape[1], -1), dcb.reshape(1, -1)
        mm_g("ffn_up_dw", ffn['hn'], dz, 'f_w_up', i, split=True)
        dhn = mm_t("ffn_up_dx", dz, 'f_w_up', i, F32, split=True)
        dh, g_ffn[i] = _rms_bwd("rms_ffn_bwd", ffn['h'], norm_ffn_g[i:i + 1], dhn, dh)
        if i % 2 == 0:
            dus = mm_t("sgu_out_dx", dh, 'a_w_out', j, BF16)
            mm_g("sgu_out_dw", mix['us'], dh, 'a_w_out', j)
            dpre, dws, dbs, dgv = _sgu_mix_bwd("sgu_mix_bwd", mix['pre'], dus, a_g_v[j:j + 1], a_w_s[j], mix['bsx'])
            sg['a_w_s'][j], sg['a_b_s'][j], sg['a_g_v'][j] = dws, dbs[:, :, 0], dgv[0]
            mm_g("sgu_in_dw", mix['hn'], dpre, 'a_w_in', j)
            dhn = mm_t("sgu_in_dx", dpre, 'a_w_in', j, F32)
        else:
            dgg = _glu_bwd("glu_bwd", mix['gg'], dh)
            mm_g("s5_glu_dw", mix['gy'], dgg, 'b_w_glu', j)
            dgy = mm_t("s5_glu_dx", dgg, 'b_w_glu', j, BF16)
            du, dbr, dbi, dcr, dci, dar, dai, ddd = _s5_bwd("s5_bwd", mix['u'], mix['y'], dgy, *mix['mats'])
            groups = b_a_re.shape[1]
            flat = lambda t: _from_blockdiag_b(t, nb).reshape(groups, SSM_STATE * SSM_GROUP)
            sel = jnp.repeat(jnp.eye(SSM_STATE, dtype=F32), SSM_GROUP, axis=0)
            dlr, dli, dldt, dbre, dbim = _s5_disc_bwd(
                "s5_disc_bwd", *mix['disc_in'], dar.reshape(groups, SSM_STATE), dai.reshape(groups, SSM_STATE),
                flat(dbr), flat(dbi), sel)
            bg['b_a_re'][j], bg['b_a_im'][j], bg['b_log_dt'][j] = dlr, dli, dldt[:, 0]
            bg['b_b_re'][j] = dbre.reshape(groups, SSM_STATE, SSM_GROUP)
            bg['b_b_im'][j] = dbim.reshape(groups, SSM_STATE, SSM_GROUP)
            bg['b_c_re'][j], bg['b_c_im'][j] = _from_blockdiag_ct(dcr, nb), _from_blockdiag_ct(dci, nb)
            bg['b_d'][j] = ddd[0]
            mm_g("s5_in_dw", mix['hn'], du, 'b_w_in', j)
            dhn = mm_t("s5_in_dx", du, 'b_w_in', j, F32)
        dh, g_mix[i] = _rms_bwd("rms_mix_bwd", mix['h'], norm_mix_g[i:i + 1], dhn, dh)
        if i + 1 in scatters:
            send, recv, thru = scatters[i + 1]
            scatters[i + 1] = _split_wait(f"grad_scatter_wait_{i + 1}", send, recv, thru, _scatter_plan(len(thru) // 2), dh)
        parts = [pgrad[k] for k in layer_keys(i)]
        sums = [_add2("grad_chip_sum", p, g) for p, g in zip(parts, _swap_halves(parts))]
        land = [lax.empty((3, *s.shape[1:]), BF16) for s in sums]
        send, recv, thru, scatter_token = _split_start(f"grad_scatter_start_{i}", sums + land, 3 * len(sums),
                                                       _scatter_plan(len(sums)))
        scatters[i] = (send, recv, thru)
    send, recv, thru = scatters[0]
    scatters[0] = _split_wait("grad_scatter_wait_0", send, recv, thru, _scatter_plan(len(thru) // 2), scatter_token)
    grad_x = dh[None]

    sgrad['norm_mix_g'] = jnp.concatenate(g_mix, axis=0)
    sgrad['norm_ffn_g'] = jnp.concatenate(g_ffn, axis=0)
    sgrad['f_conv_w'] = jnp.stack(g_cw)
    sgrad['f_conv_b'] = jnp.concatenate(g_cb, axis=0)
    for k, v_ in list(sg.items()) + list(bg.items()):
        sgrad[k] = jnp.stack(v_)

    total = _allreduce_small(_pack([sgrad[n] for n in SMALL]))
    full_shapes = [sgrad[n].shape for n in SMALL]
    gsmall = dict(zip(SMALL, _unpack(total, full_shapes)))
    for n, axis in CHIP_SHARDED_SMALL.items():
        width = w[n].shape[axis]
        gsmall[n] = lax.dynamic_slice_in_dim(gsmall[n], kchip * width, width, axis=axis)
    pk = lambda t: _pack([t[n] for n in SMALL])
    gpacked = pk(gsmall)
    dpk, mpk, vpk = _adam_small(pk(w), pk(mom), pk(var), gpacked)
    shard_shapes = [w[n].shape for n in SMALL]
    out_g = dict(gsmall)
    out_d = dict(zip(SMALL, _unpack(dpk, shard_shapes)))
    out_m = dict(zip(SMALL, _unpack(mpk, shard_shapes)))
    out_v = dict(zip(SMALL, _unpack(vpk, shard_shapes)))

    totals = {}
    for i in range(depth):
        n = len(scatters[i]) // 2
        for k, s, r in zip(layer_keys(i), scatters[i][:n], scatters[i][n:]):
            totals[k] = _add4("grad_total", s, r)
    gfull = _join_halves([totals[k] for k in keys])
    stacked = {n: [lax.empty(w[n].shape, F32) for _ in range(4)] for n in big_names}
    for (n, l), gf_ in zip(keys, gfull):
        stacked[n] = _adam_big("adam_" + n, w[n], mom[n], var[n], gf_, BIG[n], l, stacked[n])
    for n in big_names:
        out_g[n], out_d[n], out_m[n], out_v[n] = stacked[n]

    return (loss, grad_x, *[out_g[n] for n in W_NAMES], *[out_d[n] for n in W_NAMES],
            *[out_m[n] for n in W_NAMES], *[out_v[n] for n in W_NAMES])
```

```python
import functools
import itertools
import math

import jax
import jax.numpy as jnp
from jax import lax
from jax.experimental import pallas as pl
from jax.experimental.pallas import tpu as pltpu

F32, BF16 = jnp.float32, jnp.bfloat16
MESH = pl.DeviceIdType.MESH

CHUNK = 128
SEG = CHUNK + 4
SGU_GROUP = 128
SSM_GROUP = 16
SSM_STATE = 64
EPS = 1e-6
LANES = 128
GROUPS_PER_BLOCK = LANES // SSM_GROUP
STATE_BLOCKS = SSM_STATE // SSM_GROUP
VMEM_LIMIT = 52 * 1024 * 1024

ADAM_LR, ADAM_B1, ADAM_B2, ADAM_EPS, ADAM_WD, ADAM_STEP = 0.001, 0.9, 0.999, 1e-08, 0.01, 10

W_NAMES = ['norm_mix_g', 'norm_ffn_g', 'a_w_in', 'a_g_v', 'a_w_s', 'a_b_s', 'a_w_out', 'b_w_in', 'b_a_re', 'b_a_im',
           'b_log_dt', 'b_b_re', 'b_b_im', 'b_c_re', 'b_c_im', 'b_d', 'b_w_glu', 'f_w_up', 'f_conv_w', 'f_conv_b',
           'f_w_down', 'final_g']
BIG = {'a_w_in': 'col', 'a_w_out': 'row', 'b_w_in': 'row', 'b_w_glu': 'col', 'f_w_up': 'col', 'f_w_down': 'row'}
SMALL = [n for n in W_NAMES if n not in BIG]
CHIP_SHARDED_SMALL = {'b_d': 1, 'f_conv_w': 2}


def _tile(n, pref, align):
    t = min(n, pref)
    t -= t % align
    while t >= align:
        if n % t == 0:
            return t
        t -= align
    return n


def _params(*sem):
    return pltpu.CompilerParams(dimension_semantics=sem, vmem_limit_bytes=VMEM_LIMIT)


def _gelu(x):
    c = math.sqrt(2.0 / math.pi)
    return 0.5 * x * (1.0 + jnp.tanh(c * (x + 0.044715 * x * x * x)))


def _gelu_grad(x):
    c = math.sqrt(2.0 / math.pi)
    t = jnp.tanh(c * (x + 0.044715 * x * x * x))
    return 0.5 * (1.0 + t) + 0.5 * x * (1.0 - t * t) * c * (1.0 + 3.0 * 0.044715 * x * x)


def _half_shape(kind, shard_shape):
    _, r, c = shard_shape
    return (r // 2, c) if kind == 'col' else (r, c // 2)


def _full_dims(kind, shard_shape):
    _, r, c = shard_shape
    return (r, 4 * c) if kind == 'col' else (4 * r, c)


def _gspec(kind, kdim, ndim, tr, tc, rc):
    if kind == 'col':
        nr, nc = (kdim // 2) // tr, (ndim // 4) // tc

        def imap(*g):
            rb, cb = rc(*g)
            return (cb // nc, rb // nr, rb % nr, cb % nc)
    else:
        nr, nc = (kdim // 4) // tr, (ndim // 2) // tc

        def imap(*g):
            rb, cb = rc(*g)
            return (rb // nr, cb // nc, rb % nr, cb % nc)
    return pl.BlockSpec((None, None, tr, tc), imap)


def _act_spec(rows_blk, cols_blk, ncol_half, at):
    if ncol_half is None:
        return pl.BlockSpec((rows_blk, cols_blk), at)

    def imap(*g):
        rb, cb = at(*g)
        return (cb // ncol_half, rb, cb % ncol_half)
    return pl.BlockSpec((None, rows_blk, cols_blk), imap)


def _wtiles(kind, kdim, ndim):
    if kind == 'col':
        return _tile(kdim // 2, 1024, LANES), _tile(ndim // 4, 1408, LANES)
    return _tile(kdim // 4, 1408, LANES), _tile(ndim // 2, 1024, LANES)


_DIMS = {'nn': (((1,), (0,)), ((), ())), 'nt': (((1,), (1,)), ((), ())), 'tn': (((0,), (0,)), ((), ()))}


def _matmul(name, mode, a, b, grid, a_spec, b_spec, out_shape, out_spec, acc_shape, extras=(), extra_specs=(),
            epilogue=None, aliases=None):
    nk = grid[2]
    dims = _DIMS[mode]
    n_extra = len(extras)

    def body(a_ref, b_ref, *rest):
        extra_refs, o_ref = rest[:n_extra], rest[n_extra]
        prod = lax.dot_general(a_ref[...].astype(BF16), b_ref[...].astype(BF16), dims, preferred_element_type=F32)

        def finish(r):
            if epilogue is not None:
                r = epilogue(r, *[e[...] for e in extra_refs])
            o_ref[...] = r.astype(o_ref.dtype)

        if nk == 1:
            finish(prod)
            return
        acc_ref = rest[n_extra + 1]
        kk = pl.program_id(2)

        @pl.when(kk == 0)
        def _():
            acc_ref[...] = prod

        @pl.when(kk > 0)
        def _():
            acc_ref[...] += prod

        @pl.when(kk == nk - 1)
        def _():
            finish(acc_ref[...])

    scratch = [pltpu.VMEM(acc_shape, F32)] if nk > 1 else []
    return pl.pallas_call(
        body, name=name, grid=grid, in_specs=[a_spec, b_spec, *extra_specs], out_specs=out_spec, out_shape=out_shape,
        scratch_shapes=scratch, input_output_aliases=aliases or {},
        compiler_params=_params("parallel", "parallel", "arbitrary"))(a, b, *extras)


def _mm_x_w(name, a, wg, kind, kdim, ndim, out_dtype, residual=None, split=False):
    rows = a.shape[0]
    tk, tn = _wtiles(kind, kdim, ndim)
    tm = _tile(rows, 1024, 16)
    grid = (rows // tm, ndim // tn, kdim // tk)
    extras, especs, epi = (), (), None
    if residual is not None:
        extras, especs = (residual,), (pl.BlockSpec((tm, tn), lambda i, j, k: (i, j)),)
        epi = lambda r, res: r + res
    out_shape = (2, rows, ndim // 2) if split else (rows, ndim)
    return _matmul(name, 'nn', a, wg, grid, pl.BlockSpec((tm, tk), lambda i, j, k: (i, k)),
                   _gspec(kind, kdim, ndim, tk, tn, lambda i, j, k: (k, j)),
                   jax.ShapeDtypeStruct(out_shape, out_dtype),
                   _act_spec(tm, tn, (ndim // 2) // tn if split else None, lambda i, j, k: (i, j)),
                   (tm, tn), extras, especs, epi)


def _mm_dy_wt(name, dy, wg, kind, kdim, ndim, out_dtype, split=False, deps=()):
    rows = dy.shape[-2]
    tn, tk = _wtiles(kind, kdim, ndim)
    tm = _tile(rows, 1024, 16)
    grid = (rows // tm, kdim // tn, ndim // tk)
    return _matmul(name, 'nt', dy, wg, grid,
                   _act_spec(tm, tk, (ndim // 2) // tk if split else None, lambda i, j, k: (i, k)),
                   _gspec(kind, kdim, ndim, tn, tk, lambda i, j, k: (j, k)),
                   jax.ShapeDtypeStruct((rows, kdim), out_dtype), pl.BlockSpec((tm, tn), lambda i, j, k: (i, j)),
                   (tm, tn), extras=tuple(deps), extra_specs=(ANY,) * len(deps))


def _mm_xt_dy(name, xa, dy, kind, kdim, ndim, split=False):
    rows = xa.shape[0]
    tm, tn = _wtiles(kind, kdim, ndim)
    tl = _tile(rows, 1024, 16)
    grid = (kdim // tm, ndim // tn, rows // tl)
    rdim, cdim = (kdim // 2, ndim // 4) if kind == 'col' else (kdim // 4, ndim // 2)
    return _matmul(name, 'tn', xa, dy, grid, pl.BlockSpec((tl, tm), lambda i, j, k: (k, i)),
                   _act_spec(tl, tn, (ndim // 2) // tn if split else None, lambda i, j, k: (k, j)),
                   jax.ShapeDtypeStruct((4, 2, rdim, cdim), BF16),
                   _gspec(kind, kdim, ndim, tm, tn, lambda i, j, k: (i, j)), (tm, tn))


def _rms_fwd(name, h, g):
    rows, d = h.shape
    tm = _tile(rows, 256, 16)

    def body(h_ref, g_ref, o_ref):
        x = h_ref[...]
        r = lax.rsqrt(jnp.mean(x * x, axis=-1, keepdims=True) + EPS)
        o_ref[...] = (x * r * g_ref[...]).astype(o_ref.dtype)

    return pl.pallas_call(
        body, name=name, grid=(rows // tm,),
        in_specs=[pl.BlockSpec((tm, d), lambda i: (i, 0)), pl.BlockSpec((1, d), lambda i: (0, 0))],
        out_specs=pl.BlockSpec((tm, d), lambda i: (i, 0)), out_shape=jax.ShapeDtypeStruct((rows, d), BF16),
        compiler_params=_params("parallel"))(h, g)


def _rms_bwd(name, h, g, dhn, dres):
    rows, d = h.shape
    tm = _tile(rows, 256, 16)

    def body(h_ref, g_ref, dy_ref, dres_ref, dh_ref, dg_ref):
        x = h_ref[...]
        r = lax.rsqrt(jnp.mean(x * x, axis=-1, keepdims=True) + EPS)
        xh = x * r
        dy = dy_ref[...].astype(F32)
        gy = dy * g_ref[...]
        dh_ref[...] = dres_ref[...] + r * (gy - xh * jnp.mean(gy * xh, axis=-1, keepdims=True))
        part = jnp.sum(dy * xh, axis=0, keepdims=True)

        @pl.when(pl.program_id(0) == 0)
        def _():
            dg_ref[...] = part

        @pl.when(pl.program_id(0) > 0)
        def _():
            dg_ref[...] += part

    row = pl.BlockSpec((tm, d), lambda i: (i, 0))
    vec = pl.BlockSpec((1, d), lambda i: (0, 0))
    return pl.pallas_call(
        body, name=name, grid=(rows // tm,), in_specs=[row, vec, row, row], out_specs=[row, vec],
        out_shape=[jax.ShapeDtypeStruct((rows, d), F32), jax.ShapeDtypeStruct((1, d), F32)],
        compiler_params=_params("arbitrary"))(h, g, dhn, dres)


def _loss_head(h, g, target):
    rows, d = h.shape
    tm = _tile(rows, 256, 16)

    def body(h_ref, g_ref, t_ref, dh_ref, dg_ref, loss_ref):
        x = h_ref[...]
        r = lax.rsqrt(jnp.mean(x * x, axis=-1, keepdims=True) + EPS)
        xh = x * r
        err = xh * g_ref[...] - t_ref[...]
        dy = err * (1.0 / d)
        gy = dy * g_ref[...]
        dh_ref[...] = r * (gy - xh * jnp.mean(gy * xh, axis=-1, keepdims=True))
        part = jnp.sum(dy * xh, axis=0, keepdims=True)
        sq = jnp.sum(err * err, axis=0, keepdims=True) * (0.5 / d)

        @pl.when(pl.program_id(0) == 0)
        def _():
            dg_ref[...] = part
            loss_ref[...] = sq

        @pl.when(pl.program_id(0) > 0)
        def _():
            dg_ref[...] += part
            loss_ref[...] += sq

    row = pl.BlockSpec((tm, d), lambda i: (i, 0))
    vec = pl.BlockSpec((1, d), lambda i: (0, 0))
    return pl.pallas_call(
        body, name="loss_head", grid=(rows // tm,), in_specs=[row, vec, row], out_specs=[row, vec, vec],
        out_shape=[jax.ShapeDtypeStruct((rows, d), F32), jax.ShapeDtypeStruct((1, d), F32),
                   jax.ShapeDtypeStruct((1, d), F32)],
        compiler_params=_params("arbitrary"))(h, g, target)


def _shift_down(cur, prev8, first, k):
    rows = cur.shape[0]
    rolled = pltpu.roll(cur, k, axis=0)
    idx = lax.broadcasted_iota(jnp.int32, cur.shape, 0)
    prev8 = jnp.where(first, 0.0, prev8)
    out = rolled
    for r in range(k):
        out = jnp.where(idx == r, prev8[8 - k + r:8 - k + r + 1, :], out)
    del rows
    return out


def _shift_up(cur, next8, last, k):
    rows = cur.shape[0]
    rolled = pltpu.roll(cur, rows - k, axis=0)
    idx = lax.broadcasted_iota(jnp.int32, cur.shape, 0)
    next8 = jnp.where(last, 0.0, next8)
    out = rolled
    for r in range(k):
        out = jnp.where(idx == rows - k + r, next8[r:r + 1, :], out)
    return out


def _conv_acc(z, zprev, first, w, b):
    z1 = _shift_down(z, zprev, first, 1)
    z2 = _shift_down(z, zprev, first, 2)
    return b + w[2:3, :] * z + w[1:2, :] * z1 + w[0:1, :] * z2, z1, z2


def _ffn_tiles(rows, f):
    return _tile(rows, 512, 16), _tile(f, 512, LANES)


def _ffn_act_fwd(name, z3, cw3, cb3):
    _, rows, f = z3.shape
    tm, tc = _ffn_tiles(rows, f)
    hb = tm // 8

    def body(z_ref, zp_ref, w_ref, b_ref, o_ref):
        first = pl.program_id(0) == 0
        gate, _, _ = _conv_acc(z_ref[0].astype(F32), zp_ref[0].astype(F32), first, w_ref[0], b_ref[0])
        val, _, _ = _conv_acc(z_ref[1].astype(F32), zp_ref[1].astype(F32), first, w_ref[1], b_ref[1])
        o_ref[...] = (gate * jax.nn.sigmoid(gate) * val).astype(o_ref.dtype)

    return pl.pallas_call(
        body, name=name, grid=(rows // tm, f // tc),
        in_specs=[pl.BlockSpec((2, tm, tc), lambda i, j: (0, i, j)),
                  pl.BlockSpec((2, 8, tc), lambda i, j: (0, jnp.maximum(i * hb - 1, 0), j)),
                  pl.BlockSpec((2, 3, tc), lambda i, j: (0, 0, j)), pl.BlockSpec((2, 1, tc), lambda i, j: (0, 0, j))],
        out_specs=pl.BlockSpec((tm, tc), lambda i, j: (i, j)), out_shape=jax.ShapeDtypeStruct((rows, f), BF16),
        compiler_params=_params("parallel", "parallel"))(z3, z3, cw3, cb3)


def _gate_grads(d_a, acc_g, acc_v):
    sig = jax.nn.sigmoid(acc_g)
    return d_a * acc_v * sig * (1.0 + acc_g * (1.0 - sig)), d_a * acc_g * sig


def _ffn_act_bwd(name, z3, da, cw3, cb3):
    _, rows, f = z3.shape
    tm, tc = _ffn_tiles(rows, f)
    hb = tm // 8
    nrow = rows // tm

    def body(z_ref, zp_ref, zn_ref, da_ref, dan_ref, w_ref, b_ref, dz_ref, dcw_ref, dcb_ref):
        i = pl.program_id(1)
        first, last = i == 0, i == nrow - 1
        w, b = (w_ref[0], w_ref[1]), (b_ref[0], b_ref[1])
        z = (z_ref[0].astype(F32), z_ref[1].astype(F32))
        acc, taps = [], []
        for hf in range(2):
            a_h, z1, z2 = _conv_acc(z[hf], zp_ref[hf].astype(F32), first, w[hf], b[hf])
            acc.append(a_h)
            taps.append((z2, z1, z[hf]))
        dacc = _gate_grads(da_ref[...].astype(F32), acc[0], acc[1])
        acc_n = [_conv_acc(zn_ref[hf].astype(F32), z[hf][tm - 8:tm, :], False, w[hf], b[hf])[0] for hf in range(2)]
        dacc_n = _gate_grads(dan_ref[...].astype(F32), acc_n[0], acc_n[1])
        for hf in range(2):
            d = dacc[hf]
            d1 = _shift_up(d, dacc_n[hf], last, 1)
            d2 = _shift_up(d, dacc_n[hf], last, 2)
            dz_ref[hf] = (w[hf][2:3, :] * d + w[hf][1:2, :] * d1 + w[hf][0:1, :] * d2).astype(dz_ref.dtype)
        sums_w = [[jnp.sum(dacc[hf] * t, axis=0, keepdims=True) for t in taps[hf]] for hf in range(2)]
        sums_b = [jnp.sum(dacc[hf], axis=0, keepdims=True) for hf in range(2)]

        @pl.when(first)
        def _():
            for hf in range(2):
                for k in range(3):
                    dcw_ref[hf, k:k + 1, :] = sums_w[hf][k]
                dcb_ref[hf] = sums_b[hf]

        @pl.when(i > 0)
        def _():
            for hf in range(2):
                for k in range(3):
                    dcw_ref[hf, k:k + 1, :] += sums_w[hf][k]
                dcb_ref[hf] += sums_b[hf]

    nxt = lambda i: jnp.minimum((i + 1) * hb, rows // 8 - 1)
    wsp = pl.BlockSpec((2, 3, tc), lambda j, i: (0, 0, j))
    bsp = pl.BlockSpec((2, 1, tc), lambda j, i: (0, 0, j))
    cur = pl.BlockSpec((2, tm, tc), lambda j, i: (0, i, j))
    return pl.pallas_call(
        body, name=name, grid=(f // tc, nrow),
        in_specs=[cur, pl.BlockSpec((2, 8, tc), lambda j, i: (0, jnp.maximum(i * hb - 1, 0), j)),
                  pl.BlockSpec((2, 8, tc), lambda j, i: (0, nxt(i), j)), pl.BlockSpec((tm, tc), lambda j, i: (i, j)),
                  pl.BlockSpec((8, tc), lambda j, i: (nxt(i), j)), wsp, bsp],
        out_specs=[cur, wsp, bsp],
        out_shape=[jax.ShapeDtypeStruct((2, rows, f), BF16), jax.ShapeDtypeStruct((2, 3, f), F32),
                   jax.ShapeDtypeStruct((2, 1, f), F32)],
        compiler_params=_params("parallel", "arbitrary"))(z3, z3, z3, da, da, cw3, cb3)


def _glu_fwd(name, gg, h):
    rows, d2 = gg.shape
    d = d2 // 2
    tm, tc = _tile(rows, 512, 16), _tile(d, 1024, LANES)
    nd = d // tc

    def body(a_ref, b_ref, h_ref, o_ref):
        o_ref[...] = h_ref[...] + a_ref[...].astype(F32) * jax.nn.sigmoid(b_ref[...].astype(F32))

    return pl.pallas_call(
        body, name=name, grid=(rows // tm, nd),
        in_specs=[pl.BlockSpec((tm, tc), lambda i, j: (i, j)), pl.BlockSpec((tm, tc), lambda i, j: (i, j + nd)),
                  pl.BlockSpec((tm, tc), lambda i, j: (i, j))],
        out_specs=pl.BlockSpec((tm, tc), lambda i, j: (i, j)), out_shape=jax.ShapeDtypeStruct((rows, d), F32),
        compiler_params=_params("parallel", "parallel"))(gg, gg, h)


def _glu_bwd(name, gg, dh):
    rows, d2 = gg.shape
    d = d2 // 2
    tm, tc = _tile(rows, 512, 16), _tile(d, 1024, LANES)
    nd = d // tc

    def body(s_ref, o_ref, dh_ref, out_ref):
        is_a = pl.program_id(1) < nd
        me = s_ref[...].astype(F32)
        other = o_ref[...].astype(F32)
        g = dh_ref[...]
        sig_o = jax.nn.sigmoid(other)
        sig_m = jax.nn.sigmoid(me)
        out_ref[...] = jnp.where(is_a, g * sig_o, g * other * sig_m * (1.0 - sig_m)).astype(out_ref.dtype)

    return pl.pallas_call(
        body, name=name, grid=(rows // tm, 2 * nd),
        in_specs=[pl.BlockSpec((tm, tc), lambda i, j: (i, j)),
                  pl.BlockSpec((tm, tc), lambda i, j: (i, (j + nd) % (2 * nd))),
                  pl.BlockSpec((tm, tc), lambda i, j: (i, j % nd))],
        out_specs=pl.BlockSpec((tm, tc), lambda i, j: (i, j)), out_shape=jax.ShapeDtypeStruct((rows, d2), BF16),
        compiler_params=_params("parallel", "parallel"))(gg, gg, dh)


def _sgu_common(pre_ref, gv_ref, e):
    u = _gelu(pre_ref[:, :e].astype(F32))
    v = _gelu(pre_ref[:, e:].astype(F32))
    r = lax.rsqrt(jnp.mean(v * v, axis=-1, keepdims=True) + EPS)
    vh = v * r
    return u, vh, r, (vh * gv_ref[...]).astype(BF16)


def _tril_bf16(ws_ref, hd):
    t = lax.broadcasted_iota(jnp.int32, (CHUNK, CHUNK), 0)
    s = lax.broadcasted_iota(jnp.int32, (CHUNK, CHUNK), 1)
    return jnp.where(s <= t, ws_ref[hd], 0.0).astype(BF16)


def _sgu_mix_fwd(name, pre, gv, ws, bsx):
    rows, e2 = pre.shape
    e = e2 // 2
    heads = e // SGU_GROUP
    tr = _tile(rows, 256, CHUNK)

    def body(pre_ref, gv_ref, ws_ref, bs_ref, o_ref):
        u, _, _, vn = _sgu_common(pre_ref, gv_ref, e)
        for hd in range(heads):
            wm = _tril_bf16(ws_ref, hd)
            cols = slice(hd * SGU_GROUP, (hd + 1) * SGU_GROUP)
            for ck in range(tr // CHUNK):
                rws = slice(ck * CHUNK, (ck + 1) * CHUNK)
                s = jnp.dot(wm, vn[rws, cols], preferred_element_type=F32) + bs_ref[hd]
                o_ref[rws, cols] = (u[rws, cols] * s).astype(o_ref.dtype)

    whole3 = pl.BlockSpec((heads, CHUNK, CHUNK), lambda i: (0, 0, 0))
    return pl.pallas_call(
        body, name=name, grid=(rows // tr,),
        in_specs=[pl.BlockSpec((tr, e2), lambda i: (i, 0)), pl.BlockSpec((1, e), lambda i: (0, 0)), whole3, whole3],
        out_specs=pl.BlockSpec((tr, e), lambda i: (i, 0)), out_shape=jax.ShapeDtypeStruct((rows, e), BF16),
        compiler_params=_params("parallel"))(pre, gv, ws, bsx)


def _sgu_mix_bwd(name, pre, dus, gv, ws, bsx):
    rows, e2 = pre.shape
    e = e2 // 2
    heads = e // SGU_GROUP
    tr = _tile(rows, 256, CHUNK)

    def body(pre_ref, dus_ref, gv_ref, ws_ref, bs_ref, dpre_ref, dws_ref, dbs_ref, dgv_ref, dvn_ref, du_ref):
        first = pl.program_id(0) == 0
        u, vh, r, vn = _sgu_common(pre_ref, gv_ref, e)
        ones = jnp.ones((SGU_GROUP, LANES), BF16)
        tt = lax.broadcasted_iota(jnp.int32, (CHUNK, CHUNK), 0)
        ss = lax.broadcasted_iota(jnp.int32, (CHUNK, CHUNK), 1)
        for hd in range(heads):
            wm = _tril_bf16(ws_ref, hd)
            cols = slice(hd * SGU_GROUP, (hd + 1) * SGU_GROUP)
            dw = jnp.zeros((CHUNK, CHUNK), F32)
            db = jnp.zeros((CHUNK, LANES), F32)
            for ck in range(tr // CHUNK):
                rws = slice(ck * CHUNK, (ck + 1) * CHUNK)
                vblk = vn[rws, cols]
                s = jnp.dot(wm, vblk, preferred_element_type=F32) + bs_ref[hd]
                d_us = dus_ref[rws, cols].astype(F32)
                du_ref[rws, cols] = d_us * s
                ds = (d_us * u[rws, cols]).astype(BF16)
                dvn_ref[rws, cols] = lax.dot_general(wm, ds, _DIMS['tn'], preferred_element_type=F32)
                dw = dw + lax.dot_general(ds, vblk, _DIMS['nt'], preferred_element_type=F32)
                db = db + jnp.dot(ds, ones, preferred_element_type=F32)
            dw = jnp.where(ss <= tt, dw, 0.0)

            @pl.when(first)
            def _():
                dws_ref[hd] = dw
                dbs_ref[hd] = db

            @pl.when(jnp.logical_not(first))
            def _():
                dws_ref[hd] += dw
                dbs_ref[hd] += db

        dvn = dvn_ref[...]
        part = jnp.sum(dvn * vh, axis=0, keepdims=True)

        @pl.when(first)
        def _():
            dgv_ref[...] = part

        @pl.when(jnp.logical_not(first))
        def _():
            dgv_ref[...] += part

        gy = dvn * gv_ref[...]
        dv = r * (gy - vh * jnp.mean(gy * vh, axis=-1, keepdims=True))
        dpre_ref[:, :e] = (du_ref[...] * _gelu_grad(pre_ref[:, :e].astype(F32))).astype(dpre_ref.dtype)
        dpre_ref[:, e:] = (dv * _gelu_grad(pre_ref[:, e:].astype(F32))).astype(dpre_ref.dtype)

    whole3 = pl.BlockSpec((heads, CHUNK, CHUNK), lambda i: (0, 0, 0))
    vec = pl.BlockSpec((1, e), lambda i: (0, 0))
    return pl.pallas_call(
        body, name=name, grid=(rows // tr,),
        in_specs=[pl.BlockSpec((tr, e2), lambda i: (i, 0)), pl.BlockSpec((tr, e), lambda i: (i, 0)), vec, whole3, whole3],
        out_specs=[pl.BlockSpec((tr, e2), lambda i: (i, 0)), whole3, whole3, vec],
        out_shape=[jax.ShapeDtypeStruct((rows, e2), BF16), jax.ShapeDtypeStruct((heads, CHUNK, CHUNK), F32),
                   jax.ShapeDtypeStruct((heads, CHUNK, LANES), F32), jax.ShapeDtypeStruct((1, e), F32)],
        scratch_shapes=[pltpu.VMEM((tr, e), F32), pltpu.VMEM((tr, e), F32)],
        compiler_params=_params("arbitrary"))(pre, dus, gv, ws, bsx)


def _disc_a(lr, li, ldt):
    dt = jnp.exp(ldt)
    mag = jnp.exp(dt * lr)
    return mag * jnp.cos(dt * li), mag * jnp.sin(dt * li)


def _disc_b(lr, li, ldt, br, bi):
    ar, ai = _disc_a(lr, li, ldt)
    den = lr * lr + li * li
    qr = ((ar - 1.0) * lr + ai * li) / den
    qi = (ai * lr - (ar - 1.0) * li) / den
    return qr * br - qi * bi, qr * bi + qi * br


def _s5_disc(name, lr, li, ldt, lrx, lix, ldtx, br, bi):
    def body(lr_ref, li_ref, ldt_ref, lrx_ref, lix_ref, ldtx_ref, br_ref, bi_ref, ar_ref, ai_ref, bbr_ref, bbi_ref):
        ar_ref[...], ai_ref[...] = _disc_a(lr_ref[...], li_ref[...], ldt_ref[...])
        bbr_ref[...], bbi_ref[...] = _disc_b(lrx_ref[...], lix_ref[...], ldtx_ref[...], br_ref[...], bi_ref[...])

    small = jax.ShapeDtypeStruct(lr.shape, F32)
    wide = jax.ShapeDtypeStruct(br.shape, F32)
    return pl.pallas_call(body, name=name, out_shape=[small, small, wide, wide],
                          compiler_params=pltpu.CompilerParams(vmem_limit_bytes=VMEM_LIMIT))(
        lr, li, ldt, lrx, lix, ldtx, br, bi)


def _s5_disc_bwd(name, lr, li, ldt, lrx, lix, ldtx, br, bi, dar, dai, dbbr, dbbi, sel):
    def body(lr_ref, li_ref, ldt_ref, lrx_ref, lix_ref, ldtx_ref, br_ref, bi_ref, dar_ref, dai_ref, dbbr_ref,
             dbbi_ref, sel_ref, dlr_ref, dli_ref, dldt_ref, dbr_ref, dbi_ref):
        _, vjp_a = jax.vjp(_disc_a, lr_ref[...], li_ref[...], ldt_ref[...])
        g_lr, g_li, g_ldt = vjp_a((dar_ref[...], dai_ref[...]))
        _, vjp_b = jax.vjp(_disc_b, lrx_ref[...], lix_ref[...], ldtx_ref[...], br_ref[...], bi_ref[...])
        x_lr, x_li, x_ldt, g_br, g_bi = vjp_b((dbbr_ref[...], dbbi_ref[...]))
        fold = lambda t: jnp.dot(t, sel_ref[...], precision=lax.Precision.HIGHEST, preferred_element_type=F32)
        dlr_ref[...] = g_lr + fold(x_lr)
        dli_ref[...] = g_li + fold(x_li)
        dldt_ref[...] = jnp.sum(g_ldt + fold(x_ldt), axis=1, keepdims=True)
        dbr_ref[...] = g_br
        dbi_ref[...] = g_bi

    small = jax.ShapeDtypeStruct(lr.shape, F32)
    wide = jax.ShapeDtypeStruct(br.shape, F32)
    return pl.pallas_call(body, name=name,
                          out_shape=[small, small, jax.ShapeDtypeStruct((lr.shape[0], 1), F32), wide, wide],
                          compiler_params=pltpu.CompilerParams(vmem_limit_bytes=VMEM_LIMIT))(
        lr, li, ldt, lrx, lix, ldtx, br, bi, dar, dai, dbbr, dbbi, sel)


def _cmul(ar, ai, br, bi):
    return ar * br - ai * bi, ar * bi + ai * br


def _pow_seg(ar, ai):
    res, base, n = None, (ar, ai), SEG
    while n:
        if n & 1:
            res = base if res is None else _cmul(*res, *base)
        n >>= 1
        if n:
            base = _cmul(*base, *base)
    return res


def _scan_forward(hr_ref, hi_ref, er_ref, ei_ref, sr_ref, si_ref, ar, ai, nck):
    arb, aib = jnp.broadcast_to(ar, (nck, LANES)), jnp.broadcast_to(ai, (nck, LANES))

    def intra(t, carry):
        sr, si = carry
        slab = pl.ds(t, nck, stride=SEG)
        nr = arb * sr - aib * si + hr_ref[slab, :]
        ni = arb * si + aib * sr + hi_ref[slab, :]
        hr_ref[slab, :] = nr
        hi_ref[slab, :] = ni
        return nr, ni

    zero = jnp.zeros((nck, LANES), F32)
    er_ref[...], ei_ref[...] = lax.fori_loop(0, SEG, intra, (zero, zero), unroll=4)
    pcr, pci = _pow_seg(ar, ai)
    sr_ref[0:1, :] = jnp.zeros((1, LANES), F32)
    si_ref[0:1, :] = jnp.zeros((1, LANES), F32)
    for ck in range(nck - 1):
        pr, pi = sr_ref[ck:ck + 1, :], si_ref[ck:ck + 1, :]
        sr_ref[ck + 1:ck + 2, :] = pcr * pr - pci * pi + er_ref[ck:ck + 1, :]
        si_ref[ck + 1:ck + 2, :] = pcr * pi + pci * pr + ei_ref[ck:ck + 1, :]
    s_r, s_i = sr_ref[...], si_ref[...]

    def fix(t, carry):
        pr, pi = carry
        slab = pl.ds(t, nck, stride=SEG)
        hr_ref[slab, :] = hr_ref[slab, :] + (pr * s_r - pi * s_i)
        hi_ref[slab, :] = hi_ref[slab, :] + (pr * s_i + pi * s_r)
        return _cmul(pr, pi, arb, aib)

    lax.fori_loop(0, SEG, fix, (arb, aib), unroll=4)


def _scan_backward(gr_ref, gi_ref, hr_ref, hi_ref, er_ref, ei_ref, sr_ref, si_ref, ar, ai, nck):
    arb, aib = jnp.broadcast_to(ar, (nck, LANES)), jnp.broadcast_to(-ai, (nck, LANES))

    def intra(k, carry):
        sr, si = carry
        slab = pl.ds(SEG - 1 - k, nck, stride=SEG)
        nr = arb * sr - aib * si + gr_ref[slab, :]
        ni = arb * si + aib * sr + gi_ref[slab, :]
        gr_ref[slab, :] = nr
        gi_ref[slab, :] = ni
        return nr, ni

    zero = jnp.zeros((nck, LANES), F32)
    er_ref[...], ei_ref[...] = lax.fori_loop(0, SEG, intra, (zero, zero), unroll=4)
    pcr, pci = _pow_seg(ar, -ai)
    sr_ref[nck - 1:nck, :] = jnp.zeros((1, LANES), F32)
    si_ref[nck - 1:nck, :] = jnp.zeros((1, LANES), F32)
    for ck in range(nck - 1, 0, -1):
        pr, pi = sr_ref[ck:ck + 1, :], si_ref[ck:ck + 1, :]
        sr_ref[ck - 1:ck, :] = pcr * pr - pci * pi + er_ref[ck:ck + 1, :]
        si_ref[ck - 1:ck, :] = pcr * pi + pci * pr + ei_ref[ck:ck + 1, :]
    s_r, s_i = sr_ref[...], si_ref[...]
    last = pl.ds(SEG - 1, nck, stride=SEG)
    row = lax.broadcasted_iota(jnp.int32, (nck, LANES), 0)
    hp_r = jnp.where(row == 0, 0.0, pltpu.roll(hr_ref[last, :], 1, axis=0)) if nck > 1 else zero
    hp_i = jnp.where(row == 0, 0.0, pltpu.roll(hi_ref[last, :], 1, axis=0)) if nck > 1 else zero

    def settle(t, pr, pi, h_r, h_i):
        slab = pl.ds(t, nck, stride=SEG)
        g_r = gr_ref[slab, :] + (pr * s_r - pi * s_i)
        g_i = gi_ref[slab, :] + (pr * s_i + pi * s_r)
        gr_ref[slab, :] = g_r
        gi_ref[slab, :] = g_i
        return g_r * h_r + g_i * h_i, g_i * h_r - g_r * h_i

    def fix(k, carry):
        pr, pi, acr, aci = carry
        t = SEG - 1 - k
        prev = pl.ds(t - 1, nck, stride=SEG)
        d_r, d_i = settle(t, pr, pi, hr_ref[prev, :], hi_ref[prev, :])
        nr, ni = _cmul(pr, pi, arb, aib)
        return nr, ni, acr + d_r, aci + d_i

    pr, pi, acr, aci = lax.fori_loop(0, SEG - 1, fix, (arb, aib, zero, zero), unroll=4)
    d_r, d_i = settle(0, pr, pi, hp_r, hp_i)
    return jnp.sum(acr + d_r, axis=0, keepdims=True), jnp.sum(aci + d_i, axis=0, keepdims=True)


def _s5_fill_states(u_ref, br_ref, bi_ref, hr_ref, hi_ref, rows):
    ub = u_ref[...].astype(BF16)
    hr_ref[0:rows, :] = jnp.dot(ub, br_ref[...], preferred_element_type=F32)
    hi_ref[0:rows, :] = jnp.dot(ub, bi_ref[...], preferred_element_type=F32)
    pad = jnp.zeros((hr_ref.shape[0] - rows, LANES), F32)
    hr_ref[rows:, :] = pad
    hi_ref[rows:, :] = pad


def _s5_specs(rows, e):
    sb = STATE_BLOCKS
    chan = pl.BlockSpec((rows, LANES), lambda j: (0, j // sb))
    bmat = pl.BlockSpec((None, LANES, LANES), lambda j: (j // sb, 0, j % sb))
    cmat = pl.BlockSpec((None, LANES, LANES), lambda j: (j // sb, j % sb, 0))
    avec = pl.BlockSpec((1, LANES), lambda j: (0, j))
    dvec = pl.BlockSpec((1, LANES), lambda j: (0, j // sb))
    return chan, bmat, cmat, avec, dvec


def _s5_fwd(name, u, bre, bim, crt, cit, ar, ai, dd):
    rows, e = u.shape
    nck = rows // CHUNK
    nsteps = (e // LANES) * STATE_BLOCKS
    chan, bmat, cmat, avec, dvec = _s5_specs(rows, e)

    def body(u_ref, br_ref, bi_ref, cr_ref, ci_ref, ar_ref, ai_ref, dd_ref, y_ref, gy_ref,
             hr_ref, hi_ref, er_ref, ei_ref, sr_ref, si_ref, acc_ref):
        j = pl.program_id(0) % STATE_BLOCKS
        _s5_fill_states(u_ref, br_ref, bi_ref, hr_ref, hi_ref, rows)
        _scan_forward(hr_ref, hi_ref, er_ref, ei_ref, sr_ref, si_ref, ar_ref[...], ai_ref[...], nck)
        contrib = (jnp.dot(hr_ref[0:rows, :].astype(BF16), cr_ref[...], preferred_element_type=F32)
                   - jnp.dot(hi_ref[0:rows, :].astype(BF16), ci_ref[...], preferred_element_type=F32))

        @pl.when(j == 0)
        def _():
            acc_ref[...] = dd_ref[...] * u_ref[...] + contrib

        @pl.when(j > 0)
        def _():
            acc_ref[...] += contrib

        @pl.when(j == STATE_BLOCKS - 1)
        def _():
            y = acc_ref[...]
            y_ref[...] = y.astype(y_ref.dtype)
            gy_ref[...] = _gelu(y).astype(gy_ref.dtype)

    flat = pltpu.VMEM((rows, LANES), F32)
    big = pltpu.VMEM((nck * SEG, LANES), F32)
    small = pltpu.VMEM((nck, LANES), F32)
    out = jax.ShapeDtypeStruct((rows, e), BF16)
    return pl.pallas_call(
        body, name=name, grid=(nsteps,), in_specs=[chan, bmat, bmat, cmat, cmat, avec, avec, dvec],
        out_specs=[chan, chan], out_shape=[out, out], scratch_shapes=[big, big, small, small, small, small, flat],
        compiler_params=_params("arbitrary"))(u, bre, bim, crt, cit, ar, ai, dd)


def _s5_bwd(name, u, y, dgy, bre, bim, crt, cit, ar, ai, dd):
    rows, e = u.shape
    nb = e // LANES
    nck = rows // CHUNK
    nsteps = nb * STATE_BLOCKS
    chan, bmat, cmat, avec, dvec = _s5_specs(rows, e)

    def body(u_ref, y_ref, dgy_ref, br_ref, bi_ref, cr_ref, ci_ref, ar_ref, ai_ref, dd_ref,
             du_ref, dbr_ref, dbi_ref, dcr_ref, dci_ref, dar_ref, dai_ref, ddd_ref,
             hr_ref, hi_ref, gr_ref, gi_ref, er_ref, ei_ref, sr_ref, si_ref, acc_ref, dy_ref):
        j = pl.program_id(0) % STATE_BLOCKS
        _s5_fill_states(u_ref, br_ref, bi_ref, hr_ref, hi_ref, rows)
        _scan_forward(hr_ref, hi_ref, er_ref, ei_ref, sr_ref, si_ref, ar_ref[...], ai_ref[...], nck)

        @pl.when(j == 0)
        def _():
            dy0 = dgy_ref[...].astype(F32) * _gelu_grad(y_ref[...].astype(F32))
            dy_ref[...] = dy0
            ddd_ref[...] = jnp.sum(dy0 * u_ref[...], axis=0, keepdims=True)

        dyb = dy_ref[...].astype(BF16)
        pad = jnp.zeros((gr_ref.shape[0] - rows, LANES), F32)
        gr_ref[0:rows, :] = lax.dot_general(dyb, cr_ref[...], _DIMS['nt'], preferred_element_type=F32)
        gi_ref[0:rows, :] = -lax.dot_general(dyb, ci_ref[...], _DIMS['nt'], preferred_element_type=F32)
        gr_ref[rows:, :] = pad
        gi_ref[rows:, :] = pad
        dcr_ref[...] = lax.dot_general(hr_ref[0:rows, :].astype(BF16), dyb, _DIMS['tn'], preferred_element_type=F32)
        dci_ref[...] = -lax.dot_general(hi_ref[0:rows, :].astype(BF16), dyb, _DIMS['tn'], preferred_element_type=F32)
        dar_ref[...], dai_ref[...] = _scan_backward(gr_ref, gi_ref, hr_ref, hi_ref, er_ref, ei_ref, sr_ref, si_ref,
                                                    ar_ref[...], ai_ref[...], nck)
        ub = u_ref[...].astype(BF16)
        grb, gib = gr_ref[0:rows, :].astype(BF16), gi_ref[0:rows, :].astype(BF16)
        dbr_ref[...] = lax.dot_general(ub, grb, _DIMS['tn'], preferred_element_type=F32)
        dbi_ref[...] = lax.dot_general(ub, gib, _DIMS['tn'], preferred_element_type=F32)
        contrib = (lax.dot_general(grb, br_ref[...], _DIMS['nt'], preferred_element_type=F32)
                   + lax.dot_general(gib, bi_ref[...], _DIMS['nt'], preferred_element_type=F32))

        @pl.when(j == 0)
        def _():
            acc_ref[...] = dd_ref[...] * dy_ref[...] + contrib

        @pl.when(j > 0)
        def _():
            acc_ref[...] += contrib

        @pl.when(j == STATE_BLOCKS - 1)
        def _():
            du_ref[...] = acc_ref[...]

    flat = pltpu.VMEM((rows, LANES), F32)
    big = pltpu.VMEM((nck * SEG, LANES), F32)
    small = pltpu.VMEM((nck, LANES), F32)
    bshape = jax.ShapeDtypeStruct((nb, LANES, LANES * STATE_BLOCKS), F32)
    cshape = jax.ShapeDtypeStruct((nb, LANES * STATE_BLOCKS, LANES), F32)
    ashape = jax.ShapeDtypeStruct((1, nb * LANES * STATE_BLOCKS), F32)
    return pl.pallas_call(
        body, name=name, grid=(nsteps,),
        in_specs=[chan, chan, chan, bmat, bmat, cmat, cmat, avec, avec, dvec],
        out_specs=[chan, bmat, bmat, cmat, cmat, avec, avec, dvec],
        out_shape=[jax.ShapeDtypeStruct((rows, e), F32), bshape, bshape, cshape, cshape, ashape, ashape,
                   jax.ShapeDtypeStruct((1, e), F32)],
        scratch_shapes=[big, big, big, big, small, small, small, small, flat, flat],
        compiler_params=_params("arbitrary"))(u, y, dgy, bre, bim, crt, cit, ar, ai, dd)


def _to_blockdiag_b(bbar, nb):
    eye = jnp.eye(GROUPS_PER_BLOCK, dtype=bbar.dtype)
    t = jnp.einsum('bgpc,gh->bgchp', bbar.reshape(nb, GROUPS_PER_BLOCK, SSM_STATE, SSM_GROUP), eye)
    return t.reshape(nb, LANES, GROUPS_PER_BLOCK * SSM_STATE)


def _from_blockdiag_b(dmat, nb):
    eye = jnp.eye(GROUPS_PER_BLOCK, dtype=dmat.dtype)
    t = dmat.reshape(nb, GROUPS_PER_BLOCK, SSM_GROUP, GROUPS_PER_BLOCK, SSM_STATE)
    return jnp.einsum('bgchp,gh->bgpc', t, eye).reshape(nb * GROUPS_PER_BLOCK, SSM_STATE, SSM_GROUP)


def _to_blockdiag_ct(c, nb):
    eye = jnp.eye(GROUPS_PER_BLOCK, dtype=c.dtype)
    t = jnp.einsum('bgop,gh->bgpho', c.reshape(nb, GROUPS_PER_BLOCK, SSM_GROUP, SSM_STATE), eye)
    return t.reshape(nb, GROUPS_PER_BLOCK * SSM_STATE, LANES)


def _from_blockdiag_ct(dmat, nb):
    eye = jnp.eye(GROUPS_PER_BLOCK, dtype=dmat.dtype)
    t = dmat.reshape(nb, GROUPS_PER_BLOCK, SSM_STATE, GROUPS_PER_BLOCK, SSM_GROUP)
    return jnp.einsum('bgpho,gh->bgop', t, eye).reshape(nb * GROUPS_PER_BLOCK, SSM_GROUP, SSM_STATE)


ANY = pl.BlockSpec(memory_space=pl.ANY)


def _half_specs(kind, rdim, cdim, tr, tc, layer):
    nr, nc = rdim // tr, cdim // tc
    if kind == 'col':
        nat = pl.BlockSpec((None, tr, tc), lambda c, rb, cb: (layer, c * nr + rb, cb))
    else:
        nat = pl.BlockSpec((None, tr, tc), lambda c, rb, cb: (layer, rb, c * nc + cb))
    half = pl.BlockSpec((None, tr, tc), lambda c, rb, cb: (c, rb, cb))
    return nat, half


def _my_chip():
    return 2 * lax.axis_index("x") + lax.axis_index("y")


def _cast_halves(name, w, kind, layer):
    rdim, cdim = _half_shape(kind, w.shape)
    tr, tc = _tile(rdim, 512, 16), _tile(cdim, 1408, LANES)
    nat, _ = _half_specs(kind, rdim, cdim, tr, tc, layer)
    slot = pl.BlockSpec((None, None, tr, tc), lambda c, rb, cb: (_my_chip(), c, rb, cb))

    def body(w_ref, o_ref):
        o_ref[...] = w_ref[...].astype(o_ref.dtype)

    return pl.pallas_call(
        body, name=name, grid=(2, rdim // tr, cdim // tc), in_specs=[nat], out_specs=slot,
        out_shape=jax.ShapeDtypeStruct((4, 2, rdim, cdim), BF16),
        compiler_params=_params("parallel", "parallel", "parallel"))(w)


def _adam_math(w, g, m, v):
    m = ADAM_B1 * m + (1.0 - ADAM_B1) * g
    v = ADAM_B2 * v + (1.0 - ADAM_B2) * (g * g)
    m_hat = m / (1.0 - ADAM_B1 ** ADAM_STEP)
    v_hat = v / (1.0 - ADAM_B2 ** ADAM_STEP)
    delta = -ADAM_LR * (m_hat / (jnp.sqrt(v_hat) + ADAM_EPS) + ADAM_WD * w)
    return delta, m, v


def _adam_big(name, w, m, v, gfull, kind, layer, outs):
    rdim, cdim = _half_shape(kind, w.shape)
    tr, tc = _tile(rdim, 256, 8), _tile(cdim, 1408, LANES)
    nat, half = _half_specs(kind, rdim, cdim, tr, tc, layer)

    def body(w_ref, m_ref, v_ref, g_ref, *rest):
        go_ref, d_ref, mo_ref, vo_ref = rest[4:]
        g = g_ref[...]
        go_ref[...] = g
        d_ref[...], mo_ref[...], vo_ref[...] = _adam_math(w_ref[...], g, m_ref[...], v_ref[...])

    shape = jax.ShapeDtypeStruct(w.shape, F32)
    return pl.pallas_call(
        body, name=name, grid=(2, rdim // tr, cdim // tc), in_specs=[nat, nat, nat, half] + [ANY] * 4,
        out_specs=[nat, nat, nat, nat], out_shape=[shape, shape, shape, shape],
        input_output_aliases={4: 0, 5: 1, 6: 2, 7: 3},
        compiler_params=_params("parallel", "parallel", "parallel"))(w, m, v, gfull, *outs)


def _adam_small(w, m, v, g):
    rows = w.shape[0]
    tr = _tile(rows, 512, 8)
    spec = pl.BlockSpec((tr, LANES), lambda i: (i, 0))

    def body(w_ref, m_ref, v_ref, g_ref, d_ref, mo_ref, vo_ref):
        d_ref[...], mo_ref[...], vo_ref[...] = _adam_math(w_ref[...], g_ref[...], m_ref[...], v_ref[...])

    shape = jax.ShapeDtypeStruct(w.shape, F32)
    return pl.pallas_call(body, name="adam_small", grid=(rows // tr,), in_specs=[spec] * 4, out_specs=[spec] * 3,
                          out_shape=[shape] * 3, compiler_params=_params("parallel"))(w, m, v, g)


def _add2(name, part, got):
    cdim = part.shape[-1]
    a2, b2 = part.reshape(4, 2, -1, cdim), got.reshape(4, -1, cdim)
    rows = b2.shape[1]
    tr, tc = _tile(rows, 512, 16), _tile(cdim, 1408, LANES)
    mine = pl.BlockSpec((None, None, tr, tc), lambda k, i, j: (k, lax.axis_index("c"), i, j))
    spec = pl.BlockSpec((None, tr, tc), lambda k, i, j: (k, i, j))

    def body(a_ref, b_ref, o_ref):
        o_ref[...] = (a_ref[...].astype(F32) + b_ref[...].astype(F32)).astype(o_ref.dtype)

    out = pl.pallas_call(
        body, name=name, grid=(4, rows // tr, cdim // tc), in_specs=[mine, spec], out_specs=spec,
        out_shape=jax.ShapeDtypeStruct(b2.shape, BF16),
        compiler_params=_params("parallel", "parallel", "parallel"))(a2, b2)
    return out.reshape(got.shape)


def _add4(name, sums, recv):
    cdim = sums.shape[-1]
    s2 = sums.reshape(4, -1, cdim)
    r3 = recv.reshape(3, -1, cdim)
    rows = s2.shape[1]
    tr, tc = _tile(rows, 512, 16), _tile(cdim, 1408, LANES)
    own = pl.BlockSpec((None, tr, tc), lambda i, j: (_my_chip(), i, j))
    rspec = lambda k: pl.BlockSpec((None, tr, tc), lambda i, j: (k, i, j))
    slot = pl.BlockSpec((None, tr, tc), lambda i, j: (lax.axis_index("c"), i, j))

    def body(o_ref, x_ref, y_ref, d_ref, out_ref):
        out_ref[...] = ((o_ref[...].astype(F32) + d_ref[...].astype(F32))
                        + (x_ref[...].astype(F32) + y_ref[...].astype(F32)))

    out = pl.pallas_call(
        body, name=name, grid=(rows // tr, cdim // tc), in_specs=[own, rspec(0), rspec(1), rspec(2)], out_specs=slot,
        out_shape=jax.ShapeDtypeStruct((2, rows, cdim), F32),
        compiler_params=_params("parallel", "parallel"))(s2, r3, r3, r3)
    return out.reshape(2, *sums.shape[1:])


def _place():
    x, y, c = lax.axis_index("x"), lax.axis_index("y"), lax.axis_index("c")
    chips = [(1 - x, y), (x, 1 - y), (1 - x, 1 - y)]
    return x, y, c, chips


def _remote(src, dst, send, recv, to):
    return pltpu.make_async_remote_copy(src_ref=src, dst_ref=dst, send_sem=send, recv_sem=recv, device_id=to,
                                        device_id_type=MESH)


def _pieces(src, dst, bands):
    lead, rows = src.shape[:-2], src.shape[-2]
    band = rows // bands
    out = []
    for idx in itertools.product(*[range(dim) for dim in lead]):
        for q in range(bands):
            sl = (*idx, pl.ds(q * band, band))
            out.append((src.at[sl], dst.at[sl]))
    return out


HBM = pl.BlockSpec(memory_space=pltpu.HBM)
SEM = pl.BlockSpec(memory_space=pltpu.SEMAPHORE)
EFFECT = pltpu.SideEffectType.DATAFLOW_SIDE_EFFECTING


def _split_start(name, bufs, ncopy, plan, deps=()):
    nb, nd = len(bufs), len(deps)

    def body(*refs):
        ins, send, recv, token = refs[:nb], refs[nb + nd], refs[nb + nd + 1], refs[2 * nb + nd + 2]
        for k, (src, dst, _, to, bands) in enumerate(plan(ins)):
            for s, d in _pieces(src, dst, bands):
                _remote(s, d, send.at[k], recv.at[k], to).start()
        token[...] = jnp.zeros_like(token)

    res = pl.pallas_call(
        body, name=name, in_specs=[HBM] * nb + [ANY] * nd,
        out_specs=[SEM, SEM] + [HBM] * nb + [pl.BlockSpec(memory_space=pltpu.VMEM)],
        out_shape=[pltpu.SemaphoreType.DMA((ncopy,)), pltpu.SemaphoreType.DMA((ncopy,))]
        + [pltpu.HBM(b.shape, b.dtype) for b in bufs] + [jax.ShapeDtypeStruct((8, LANES), F32)],
        input_output_aliases={a: a + 2 for a in range(nb)},
        compiler_params=pltpu.CompilerParams(has_side_effects=EFFECT))(
        *[pltpu.with_memory_space_constraint(b, pltpu.HBM) for b in bufs], *deps)
    return res[0], res[1], list(res[2:2 + nb]), res[2 + nb]


def _split_wait(name, send, recv, bufs, plan, after):
    nb = len(bufs)

    def body(*refs):
        ins, send_sem, recv_sem = refs[:nb], refs[nb], refs[nb + 1]
        for k, (src, dst, landing, to, _) in enumerate(plan(ins)):
            _remote(src, dst, send_sem.at[k], recv_sem.at[k], to).wait_send()
            _remote(src, landing, send_sem.at[k], recv_sem.at[k], to).wait_recv()

    return pl.pallas_call(
        body, name=name, in_specs=[HBM] * nb + [SEM, SEM, ANY], out_specs=[HBM] * nb,
        out_shape=[pltpu.HBM(b.shape, b.dtype) for b in bufs], input_output_aliases={a: a for a in range(nb)},
        compiler_params=pltpu.CompilerParams(has_side_effects=EFFECT))(*bufs, send, recv, after)


def _gather_plan(n):
    def plan(refs):
        x, y, c, chips = _place()
        kme = 2 * x + y
        return [(refs[a].at[kme, c], refs[a].at[kme, c], refs[a].at[2 * chip[0] + chip[1], c], (*chip, c), 2)
                for a in range(n) for chip in chips]
    return plan


def _scatter_plan(n):
    def plan(refs):
        x, y, c, chips = _place()
        return [(refs[a].at[2 * chip[0] + chip[1]], refs[n + a].at[r], refs[n + a].at[r], (*chip, c), 2)
                for a in range(n) for r, chip in enumerate(chips)]
    return plan


def _forward_halves(slots):
    n = len(slots)

    def body(*refs):
        outs = refs[n:2 * n]
        send, recv = refs[2 * n:]
        x, y, c, chips = _place()
        sib = (x, y, 1 - c)
        for a in range(n):
            for r, chip in enumerate(chips):
                kp = 2 * chip[0] + chip[1]
                for s, d in _pieces(outs[a].at[kp, c], outs[a].at[kp, c], 2):
                    _remote(s, d, send.at[3 * a + r], recv.at[3 * a + r], sib).start()
        for a in range(n):
            for r, chip in enumerate(chips):
                kp = 2 * chip[0] + chip[1]
                _remote(outs[a].at[kp, 1 - c], outs[a].at[kp, 1 - c], send.at[3 * a + r], recv.at[3 * a + r],
                        sib).wait_recv()
                _remote(outs[a].at[kp, c], outs[a].at[kp, c], send.at[3 * a + r], recv.at[3 * a + r], sib).wait_send()

    return pl.pallas_call(
        body, name="allgather_forward", in_specs=[ANY] * n, out_specs=[ANY] * n,
        out_shape=[jax.ShapeDtypeStruct(s.shape, s.dtype) for s in slots],
        input_output_aliases={a: a for a in range(n)},
        scratch_shapes=[pltpu.SemaphoreType.DMA((3 * n,)), pltpu.SemaphoreType.DMA((3 * n,))])(*slots)


def _allgather_small(shards):
    n = len(shards)

    def body(*refs):
        ins, outs = refs[:n], refs[n:2 * n]
        send, recv, loc = refs[2 * n:]
        x, y, c, chips = _place()
        kme = 2 * x + y
        local = [pltpu.make_async_copy(ins[a], outs[a].at[kme], loc.at[a]) for a in range(n)]
        for cp in local:
            cp.start()
        cps = [_remote(ins[a], outs[a].at[kme], send.at[3 * a + r], recv.at[3 * a + r], (*chip, c))
               for a in range(n) for r, chip in enumerate(chips)]
        for cp in cps:
            cp.start()
        for a in range(n):
            for r, chip in enumerate(chips):
                kp = 2 * chip[0] + chip[1]
                _remote(ins[a], outs[a].at[kp], send.at[3 * a + r], recv.at[3 * a + r], (*chip, c)).wait_recv()
        for cp in cps:
            cp.wait_send()
        for cp in local:
            cp.wait()

    return pl.pallas_call(
        body, name="allgather_small", in_specs=[ANY] * n, out_specs=[ANY] * n,
        out_shape=[jax.ShapeDtypeStruct((4, *s.shape), s.dtype) for s in shards],
        scratch_shapes=[pltpu.SemaphoreType.DMA((3 * n,)), pltpu.SemaphoreType.DMA((3 * n,)),
                        pltpu.SemaphoreType.DMA((n,))])(*shards)


def _swap_halves(parts):
    n = len(parts)

    def body(*refs):
        ins, got = refs[:n], refs[n:2 * n]
        send, recv = refs[2 * n:]
        x, y, c, _ = _place()
        sib = (x, y, 1 - c)
        for a in range(n):
            for s, d in _pieces(ins[a].at[:, 1 - c], got[a], 1):
                _remote(s, d, send.at[a], recv.at[a], sib).start()
        for a in range(n):
            _remote(ins[a].at[:, 1 - c], got[a], send.at[a], recv.at[a], sib).wait()

    return pl.pallas_call(
        body, name="grad_swap_halves", in_specs=[ANY] * n, out_specs=[ANY] * n,
        out_shape=[jax.ShapeDtypeStruct((4, *p.shape[2:]), p.dtype) for p in parts],
        scratch_shapes=[pltpu.SemaphoreType.DMA((n,)), pltpu.SemaphoreType.DMA((n,))])(*parts)


def _join_halves(totals):
    n = len(totals)

    def body(*refs):
        outs = refs[n:2 * n]
        send, recv = refs[2 * n:]
        x, y, c, _ = _place()
        sib = (x, y, 1 - c)
        for a in range(n):
            for s, d in _pieces(outs[a].at[c], outs[a].at[c], 4):
                _remote(s, d, send.at[a], recv.at[a], sib).start()
        for a in range(n):
            _remote(outs[a].at[1 - c], outs[a].at[1 - c], send.at[a], recv.at[a], sib).wait_recv()
            _remote(outs[a].at[c], outs[a].at[c], send.at[a], recv.at[a], sib).wait_send()

    return pl.pallas_call(
        body, name="grad_join_halves", in_specs=[ANY] * n, out_specs=[ANY] * n,
        out_shape=[jax.ShapeDtypeStruct(t.shape, t.dtype) for t in totals],
        input_output_aliases={a: a for a in range(n)},
        scratch_shapes=[pltpu.SemaphoreType.DMA((n,)), pltpu.SemaphoreType.DMA((n,))])(*totals)


def _allreduce_small(packed):
    rows = packed.shape[0]
    half = rows // 2

    def body(in_ref, out_ref, q_ref, s_ref, t_ref, send, recv):
        x, y, c, chips = _place()
        sib = (x, y, 1 - c)
        mine = pl.ds(pl.multiple_of(c * half, 8), half)
        theirs = pl.ds(pl.multiple_of((1 - c) * half, 8), half)
        first = _remote(in_ref.at[theirs], q_ref, send.at[0], recv.at[0], sib)
        first.start()
        first.wait()
        s_ref[...] = in_ref[mine, :] + q_ref[...]
        cps = [_remote(s_ref, t_ref.at[r], send.at[1 + r], recv.at[1 + r], (*chip, c)) for r, chip in enumerate(chips)]
        for cp in cps:
            cp.start()
        for cp in cps:
            cp.wait()
        out_ref[mine, :] = (s_ref[...] + t_ref[2]) + (t_ref[0] + t_ref[1])
        last = _remote(out_ref.at[mine], out_ref.at[mine], send.at[4], recv.at[4], sib)
        last.start()
        _remote(out_ref.at[theirs], out_ref.at[theirs], send.at[4], recv.at[4], sib).wait_recv()
        last.wait_send()

    vm = pl.BlockSpec(memory_space=pltpu.VMEM)
    return pl.pallas_call(
        body, name="allreduce_small", in_specs=[vm], out_specs=vm, out_shape=jax.ShapeDtypeStruct(packed.shape, F32),
        scratch_shapes=[pltpu.VMEM((half, LANES), F32), pltpu.VMEM((half, LANES), F32),
                        pltpu.VMEM((3, half, LANES), F32), pltpu.SemaphoreType.DMA((5,)), pltpu.SemaphoreType.DMA((5,))],
        compiler_params=pltpu.CompilerParams(vmem_limit_bytes=VMEM_LIMIT))(packed)


PACK_ROWS = 16


def _pack(arrs):
    parts, total = [], 0
    for a in arrs:
        flat = a.reshape(-1)
        rows = -(-flat.shape[0] // (LANES * PACK_ROWS)) * PACK_ROWS
        parts.append(jnp.pad(flat, (0, rows * LANES - flat.shape[0])).reshape(rows, LANES))
        total += rows
    return jnp.concatenate(parts, axis=0)


def _unpack(packed, shapes):
    out, row = [], 0
    for shp in shapes:
        size = math.prod(shp)
        rows = -(-size // (LANES * PACK_ROWS)) * PACK_ROWS
        out.append(packed[row:row + rows].reshape(-1)[:size].reshape(shp))
        row += rows
    return out


def kernel(x, norm_mix_g, norm_ffn_g, a_w_in, a_g_v, a_w_s, a_b_s, a_w_out, b_w_in, b_a_re, b_a_im, b_log_dt, b_b_re, b_b_im, b_c_re, b_c_im, b_d, b_w_glu, f_w_up, f_conv_w, f_conv_b, f_w_down, final_g, loss_target, m_norm_mix_g, m_norm_ffn_g, m_a_w_in, m_a_g_v, m_a_w_s, m_a_b_s, m_a_w_out, m_b_w_in, m_b_a_re, m_b_a_im, m_b_log_dt, m_b_b_re, m_b_b_im, m_b_c_re, m_b_c_im, m_b_d, m_b_w_glu, m_f_w_up, m_f_conv_w, m_f_conv_b, m_f_w_down, m_final_g, v_norm_mix_g, v_norm_ffn_g, v_a_w_in, v_a_g_v, v_a_w_s, v_a_b_s, v_a_w_out, v_b_w_in, v_b_a_re, v_b_a_im, v_b_log_dt, v_b_b_re, v_b_b_im, v_b_c_re, v_b_c_im, v_b_d, v_b_w_glu, v_f_w_up, v_f_conv_w, v_f_conv_b, v_f_w_down, v_final_g):
    w = dict(norm_mix_g=norm_mix_g, norm_ffn_g=norm_ffn_g, a_w_in=a_w_in, a_g_v=a_g_v, a_w_s=a_w_s, a_b_s=a_b_s,
             a_w_out=a_w_out, b_w_in=b_w_in, b_a_re=b_a_re, b_a_im=b_a_im, b_log_dt=b_log_dt, b_b_re=b_b_re,
             b_b_im=b_b_im, b_c_re=b_c_re, b_c_im=b_c_im, b_d=b_d, b_w_glu=b_w_glu, f_w_up=f_w_up, f_conv_w=f_conv_w,
             f_conv_b=f_conv_b, f_w_down=f_w_down, final_g=final_g)
    mom = dict(norm_mix_g=m_norm_mix_g, norm_ffn_g=m_norm_ffn_g, a_w_in=m_a_w_in, a_g_v=m_a_g_v, a_w_s=m_a_w_s,
               a_b_s=m_a_b_s, a_w_out=m_a_w_out, b_w_in=m_b_w_in, b_a_re=m_b_a_re, b_a_im=m_b_a_im,
               b_log_dt=m_b_log_dt, b_b_re=m_b_b_re, b_b_im=m_b_b_im, b_c_re=m_b_c_re, b_c_im=m_b_c_im, b_d=m_b_d,
               b_w_glu=m_b_w_glu, f_w_up=m_f_w_up, f_conv_w=m_f_conv_w, f_conv_b=m_f_conv_b, f_w_down=m_f_w_down,
               final_g=m_final_g)
    var = dict(norm_mix_g=v_norm_mix_g, norm_ffn_g=v_norm_ffn_g, a_w_in=v_a_w_in, a_g_v=v_a_g_v, a_w_s=v_a_w_s,
               a_b_s=v_a_b_s, a_w_out=v_a_w_out, b_w_in=v_b_w_in, b_a_re=v_b_a_re, b_a_im=v_b_a_im,
               b_log_dt=v_b_log_dt, b_b_re=v_b_b_re, b_b_im=v_b_b_im, b_c_re=v_b_c_re, b_c_im=v_b_c_im, b_d=v_b_d,
               b_w_glu=v_b_w_glu, f_w_up=v_f_w_up, f_conv_w=v_f_conv_w, f_conv_b=v_f_conv_b, f_w_down=v_f_w_down,
               final_g=v_final_g)

    rows, d = x.shape[1], x.shape[2]
    depth = norm_mix_g.shape[0]
    kchip = 2 * lax.axis_index("x") + lax.axis_index("y")
    big_names = list(BIG)
    dims = {n: _full_dims(BIG[n], w[n].shape) for n in big_names}

    keys = [(n, l) for n in big_names for l in range(w[n].shape[0])]
    slots = {(n, l): _cast_halves("cast_" + n, w[n], BIG[n], l) for n, l in keys}

    def group_keys(g):
        i = g // 2
        if g % 2 == 1:
            return [('f_w_up', i), ('f_w_down', i)]
        return [('a_w_in', i // 2), ('a_w_out', i // 2)] if i % 2 == 0 else [('b_w_in', i // 2), ('b_w_glu', i // 2)]

    bd_all, cw_all = _allgather_small([b_d, f_conv_w.reshape(-1, f_conv_w.shape[-1])])

    gathered, gather_waits, token = {}, [], bd_all
    for g in range(2 * depth):
        arrs = [slots[k] for k in group_keys(g)]
        send, recv, thru, token = _split_start(f"allgather_start_{g}", arrs, 3 * len(arrs), _gather_plan(len(arrs)),
                                               deps=(token,))
        gather_waits.append((send, recv, thru))
    gather_after = token

    def gather_group(g, after):
        send, recv, thru = gather_waits[g]
        landed = _split_wait(f"allgather_wait_{g}", send, recv, thru, _gather_plan(len(thru)), after)
        gathered.update(zip(group_keys(g), _forward_halves(landed)))

    bd_full = jnp.swapaxes(bd_all, 0, 1).reshape(b_d.shape[0], -1)
    cw_full = jnp.transpose(cw_all.reshape(4, *f_conv_w.shape), (1, 2, 0, 3)).reshape(depth, f_conv_w.shape[1], -1)

    pgrad = {}
    sgrad = {}

    def mm(name, a, wn, layer, out_dtype, residual=None, split=False):
        return _mm_x_w(name, a, gathered[wn, layer], BIG[wn], *dims[wn], out_dtype, residual, split)

    def mm_t(name, dy, wn, layer, out_dtype, split=False, deps=()):
        return _mm_dy_wt(name, dy, gathered[wn, layer], BIG[wn], *dims[wn], out_dtype, split, deps)

    def mm_g(name, xa, dy, wn, layer, split=False):
        pgrad[wn, layer] = _mm_xt_dy(name, xa, dy, BIG[wn], *dims[wn], split)

    e = d
    nb = e // LANES
    heads = e // SGU_GROUP
    h = x[0]
    saved = []
    for i in range(depth):
        j = i // 2
        gather_group(2 * i, gather_after if i == 0 else h)
        gm = norm_mix_g[i:i + 1]
        hn = _rms_fwd("rms_mix_fwd", h, gm)
        if i % 2 == 0:
            pre = mm("sgu_in", hn, 'a_w_in', j, BF16)
            bsx = jnp.broadcast_to(a_b_s[j][:, :, None], (heads, CHUNK, LANES))
            us = _sgu_mix_fwd("sgu_mix_fwd", pre, a_g_v[j:j + 1], a_w_s[j], bsx)
            h_mid = mm("sgu_out", us, 'a_w_out', j, F32, residual=h)
            mix = dict(h=h, hn=hn, pre=pre, us=us, bsx=bsx)
        else:
            groups = b_a_re.shape[1]
            rep = lambda t: jnp.repeat(t, SSM_GROUP, axis=1)
            lr, li = b_a_re[j], b_a_im[j]
            ldt = jnp.broadcast_to(b_log_dt[j][:, None], lr.shape)
            bflat = lambda t: t.reshape(groups, SSM_STATE * SSM_GROUP)
            disc_in = (lr, li, ldt, rep(lr), rep(li), rep(ldt), bflat(b_b_re[j]), bflat(b_b_im[j]))
            abr, abi, bbr, bbi = _s5_disc("s5_disc", *disc_in)
            shape_b = (groups, SSM_STATE, SSM_GROUP)
            bre = _to_blockdiag_b(bbr.reshape(shape_b), nb).astype(BF16)
            bim = _to_blockdiag_b(bbi.reshape(shape_b), nb).astype(BF16)
            crt = _to_blockdiag_ct(b_c_re[j], nb).astype(BF16)
            cit = _to_blockdiag_ct(b_c_im[j], nb).astype(BF16)
            ar_row, ai_row = abr.reshape(1, -1), abi.reshape(1, -1)
            dd = bd_full[j:j + 1]
            u = mm("s5_in", hn, 'b_w_in', j, F32)
            yv, gy = _s5_fwd("s5_fwd", u, bre, bim, crt, cit, ar_row, ai_row, dd)
            gg = mm("s5_glu", gy, 'b_w_glu', j, BF16)
            h_mid = _glu_fwd("glu_fwd", gg, h)
            mix = dict(h=h, hn=hn, u=u, y=yv, gy=gy, gg=gg, disc_in=disc_in, mats=(bre, bim, crt, cit, ar_row, ai_row, dd))
        gather_group(2 * i + 1, h_mid)
        gf = norm_ffn_g[i:i + 1]
        hn2 = _rms_fwd("rms_ffn_fwd", h_mid, gf)
        z = mm("ffn_up", hn2, 'f_w_up', i, BF16, split=True)
        cw = jnp.swapaxes(cw_full[i].reshape(cw_full.shape[1], 2, -1), 0, 1)
        cb = f_conv_b[i].reshape(2, 1, -1)
        act = _ffn_act_fwd("ffn_act_fwd", z, cw, cb)
        h_out = mm("ffn_down", act, 'f_w_down', i, F32, residual=h_mid)
        saved.append((mix, dict(h=h_mid, hn=hn2, z=z, act=act, cw=cw, cb=cb)))
        h = h_out

    dh, g_final, loss_vec = _loss_head(h, final_g.reshape(1, d), loss_target[0])
    loss = lax.psum(jnp.sum(loss_vec), ("x", "y", "c"))
    sgrad['final_g'] = g_final.reshape(d)

    g_mix, g_ffn = [None] * depth, [None] * depth
    g_cw, g_cb = [None] * depth, [None] * depth
    sg = {k: [None] * (depth // 2) for k in ('a_g_v', 'a_w_s', 'a_b_s')}
    bg = {k: [None] * (depth // 2) for k in ('b_a_re', 'b_a_im', 'b_log_dt', 'b_b_re', 'b_b_im', 'b_c_re', 'b_c_im', 'b_d')}
    scatters, flight = {}, dict(pending=None, token=())

    def scatter_group(g, done):
        if flight['pending'] is not None:
            prev, send, recv, thru = flight['pending']
            scatters[prev] = _split_wait(f"grad_scatter_wait_{prev}", send, recv, thru, _scatter_plan(len(thru) // 2), done)
        parts = [pgrad[k] for k in group_keys(g)]
        sums = [_add2("grad_chip_sum", p, q) for p, q in zip(parts, _swap_halves(parts))]
        land = [lax.empty((3, *s.shape[1:]), BF16) for s in sums]
        send, recv, thru, token = _split_start(f"grad_scatter_start_{g}", sums + land, 3 * len(sums),
                                               _scatter_plan(len(sums)))
        flight['pending'], flight['token'] = (g, send, recv, thru), (token,)

    for i in reversed(range(depth)):
        j = i // 2
        mix, ffn = saved[i]
        d_act = mm_t("ffn_down_dx", dh, 'f_w_down', i, BF16, deps=flight['token'])
        mm_g("ffn_down_dw", ffn['act'], dh, 'f_w_down', i)
        dz, dcw, dcb = _ffn_act_bwd("ffn_act_bwd", ffn['z'], d_act, ffn['cw'], ffn['cb'])
        g_cw[i], g_cb[i] = jnp.swapaxes(dcw, 0, 1).reshape(dcw.shape[1], -1), dcb.reshape(1, -1)
        mm_g("ffn_up_dw", ffn['hn'], dz, 'f_w_up', i, split=True)
        dhn = mm_t("ffn_up_dx", dz, 'f_w_up', i, F32, split=True)
        dh, g_ffn[i] = _rms_bwd("rms_ffn_bwd", ffn['h'], norm_ffn_g[i:i + 1], dhn, dh)
        scatter_group(2 * i + 1, dh)
        if i % 2 == 0:
            dus = mm_t("sgu_out_dx", dh, 'a_w_out', j, BF16, deps=flight['token'])
            mm_g("sgu_out_dw", mix['us'], dh, 'a_w_out', j)
            dpre, dws, dbs, dgv = _sgu_mix_bwd("sgu_mix_bwd", mix['pre'], dus, a_g_v[j:j + 1], a_w_s[j], mix['bsx'])
            sg['a_w_s'][j], sg['a_b_s'][j], sg['a_g_v'][j] = dws, dbs[:, :, 0], dgv[0]
            mm_g("sgu_in_dw", mix['hn'], dpre, 'a_w_in', j)
            dhn = mm_t("sgu_in_dx", dpre, 'a_w_in', j, F32)
        else:
            dgg = _glu_bwd("glu_bwd", mix['gg'], dh)
            mm_g("s5_glu_dw", mix['gy'], dgg, 'b_w_glu', j)
            dgy = mm_t("s5_glu_dx", dgg, 'b_w_glu', j, BF16, deps=flight['token'])
            du, dbr, dbi, dcr, dci, dar, dai, ddd = _s5_bwd("s5_bwd", mix['u'], mix['y'], dgy, *mix['mats'])
            groups = b_a_re.shape[1]
            flat = lambda t: _from_blockdiag_b(t, nb).reshape(groups, SSM_STATE * SSM_GROUP)
            sel = jnp.repeat(jnp.eye(SSM_STATE, dtype=F32), SSM_GROUP, axis=0)
            dlr, dli, dldt, dbre, dbim = _s5_disc_bwd(
                "s5_disc_bwd", *mix['disc_in'], dar.reshape(groups, SSM_STATE), dai.reshape(groups, SSM_STATE),
                flat(dbr), flat(dbi), sel)
            bg['b_a_re'][j], bg['b_a_im'][j], bg['b_log_dt'][j] = dlr, dli, dldt[:, 0]
            bg['b_b_re'][j] = dbre.reshape(groups, SSM_STATE, SSM_GROUP)
            bg['b_b_im'][j] = dbim.reshape(groups, SSM_STATE, SSM_GROUP)
            bg['b_c_re'][j], bg['b_c_im'][j] = _from_blockdiag_ct(dcr, nb), _from_blockdiag_ct(dci, nb)
            bg['b_d'][j] = ddd[0]
            mm_g("s5_in_dw", mix['hn'], du, 'b_w_in', j)
            dhn = mm_t("s5_in_dx", du, 'b_w_in', j, F32)
        dh, g_mix[i] = _rms_bwd("rms_mix_bwd", mix['h'], norm_mix_g[i:i + 1], dhn, dh)
        scatter_group(2 * i, dh)
    _, send, recv, thru = flight['pending']
    scatters[0] = _split_wait("grad_scatter_wait_0", send, recv, thru, _scatter_plan(len(thru) // 2), flight['token'][0])
    grad_x = dh[None]

    sgrad['norm_mix_g'] = jnp.concatenate(g_mix, axis=0)
    sgrad['norm_ffn_g'] = jnp.concatenate(g_ffn, axis=0)
    sgrad['f_conv_w'] = jnp.stack(g_cw)
    sgrad['f_conv_b'] = jnp.concatenate(g_cb, axis=0)
    for k, v_ in list(sg.items()) + list(bg.items()):
        sgrad[k] = jnp.stack(v_)

    total = _allreduce_small(_pack([sgrad[n] for n in SMALL]))
    full_shapes = [sgrad[n].shape for n in SMALL]
    gsmall = dict(zip(SMALL, _unpack(total, full_shapes)))
    for n, axis in CHIP_SHARDED_SMALL.items():
        width = w[n].shape[axis]
        gsmall[n] = lax.dynamic_slice_in_dim(gsmall[n], kchip * width, width, axis=axis)
    pk = lambda t: _pack([t[n] for n in SMALL])
    gpacked = pk(gsmall)
    dpk, mpk, vpk = _adam_small(pk(w), pk(mom), pk(var), gpacked)
    shard_shapes = [w[n].shape for n in SMALL]
    out_g = dict(gsmall)
    out_d = dict(zip(SMALL, _unpack(dpk, shard_shapes)))
    out_m = dict(zip(SMALL, _unpack(mpk, shard_shapes)))
    out_v = dict(zip(SMALL, _unpack(vpk, shard_shapes)))

    totals = {}
    for g in range(2 * depth):
        n = len(scatters[g]) // 2
        for k, s, r in zip(group_keys(g), scatters[g][:n], scatters[g][n:]):
            totals[k] = _add4("grad_total", s, r)
    gfull = _join_halves([totals[k] for k in keys])
    stacked = {n: [lax.empty(w[n].shape, F32) for _ in range(4)] for n in big_names}
    for (n, l), gf_ in zip(keys, gfull):
        stacked[n] = _adam_big("adam_" + n, w[n], mom[n], var[n], gf_, BIG[n], l, stacked[n])
    for n in big_names:
        out_g[n], out_d[n], out_m[n], out_v[n] = stacked[n]

    return (loss, grad_x, *[out_g[n] for n in W_NAMES], *[out_d[n] for n in W_NAMES],
            *[out_m[n] for n in W_NAMES], *[out_v[n] for n in W_NAMES])
```

```python
import functools
import itertools
import math

import jax
import jax.numpy as jnp
from jax import lax
from jax.experimental import pallas as pl
from jax.experimental.pallas import tpu as pltpu

F32, BF16 = jnp.float32, jnp.bfloat16
MESH = pl.DeviceIdType.MESH

CHUNK = 128
SEG = CHUNK + 4
SGU_GROUP = 128
SSM_GROUP = 16
SSM_STATE = 64
EPS = 1e-6
LANES = 128
GROUPS_PER_BLOCK = LANES // SSM_GROUP
STATE_BLOCKS = SSM_STATE // SSM_GROUP
VMEM_LIMIT = 52 * 1024 * 1024

ADAM_LR, ADAM_B1, ADAM_B2, ADAM_EPS, ADAM_WD, ADAM_STEP = 0.001, 0.9, 0.999, 1e-08, 0.01, 10

W_NAMES = ['norm_mix_g', 'norm_ffn_g', 'a_w_in', 'a_g_v', 'a_w_s', 'a_b_s', 'a_w_out', 'b_w_in', 'b_a_re', 'b_a_im',
           'b_log_dt', 'b_b_re', 'b_b_im', 'b_c_re', 'b_c_im', 'b_d', 'b_w_glu', 'f_w_up', 'f_conv_w', 'f_conv_b',
           'f_w_down', 'final_g']
BIG = {'a_w_in': 'col', 'a_w_out': 'row', 'b_w_in': 'row', 'b_w_glu': 'col', 'f_w_up': 'col', 'f_w_down': 'row'}
SMALL = [n for n in W_NAMES if n not in BIG]
CHIP_SHARDED_SMALL = {'b_d': 1, 'f_conv_w': 2}


def _tile(n, pref, align):
    t = min(n, pref)
    t -= t % align
    while t >= align:
        if n % t == 0:
            return t
        t -= align
    return n


def _params(*sem):
    return pltpu.CompilerParams(dimension_semantics=sem, vmem_limit_bytes=VMEM_LIMIT)


def _gelu(x):
    c = math.sqrt(2.0 / math.pi)
    return 0.5 * x * (1.0 + jnp.tanh(c * (x + 0.044715 * x * x * x)))


def _gelu_grad(x):
    c = math.sqrt(2.0 / math.pi)
    t = jnp.tanh(c * (x + 0.044715 * x * x * x))
    return 0.5 * (1.0 + t) + 0.5 * x * (1.0 - t * t) * c * (1.0 + 3.0 * 0.044715 * x * x)


def _half_shape(kind, shard_shape):
    _, r, c = shard_shape
    return (r // 2, c) if kind == 'col' else (r, c // 2)


def _full_dims(kind, shard_shape):
    _, r, c = shard_shape
    return (r, 4 * c) if kind == 'col' else (4 * r, c)


def _gspec(kind, kdim, ndim, tr, tc, rc):
    if kind == 'col':
        nr, nc = (kdim // 2) // tr, (ndim // 4) // tc

        def imap(*g):
            rb, cb = rc(*g)
            return (cb // nc, rb // nr, rb % nr, cb % nc)
    else:
        nr, nc = (kdim // 4) // tr, (ndim // 2) // tc

        def imap(*g):
            rb, cb = rc(*g)
            return (rb // nr, cb // nc, rb % nr, cb % nc)
    return pl.BlockSpec((None, None, tr, tc), imap)


def _act_spec(rows_blk, cols_blk, ncol_half, at):
    if ncol_half is None:
        return pl.BlockSpec((rows_blk, cols_blk), at)

    def imap(*g):
        rb, cb = at(*g)
        return (cb // ncol_half, rb, cb % ncol_half)
    return pl.BlockSpec((None, rows_blk, cols_blk), imap)


def _wtiles(kind, kdim, ndim):
    if kind == 'col':
        return _tile(kdim // 2, 1024, LANES), _tile(ndim // 4, 1408, LANES)
    return _tile(kdim // 4, 1408, LANES), _tile(ndim // 2, 1024, LANES)


_DIMS = {'nn': (((1,), (0,)), ((), ())), 'nt': (((1,), (1,)), ((), ())), 'tn': (((0,), (0,)), ((), ()))}


def _matmul(name, mode, a, b, grid, a_spec, b_spec, out_shape, out_spec, acc_shape, extras=(), extra_specs=(),
            epilogue=None, aliases=None):
    nk = grid[2]
    dims = _DIMS[mode]
    n_extra = len(extras)
    b_specs = b_spec if isinstance(b_spec, (list, tuple)) else [b_spec]
    nb = len(b_specs)

    def body(a_ref, *rest):
        b_refs, rest = rest[:nb], rest[nb:]
        extra_refs, o_ref = rest[:n_extra], rest[n_extra]
        width = a_ref.shape[1] // nb
        prod = None
        for p, b_ref in enumerate(b_refs):
            a_blk = a_ref[...] if nb == 1 else a_ref[:, p * width:(p + 1) * width]
            term = lax.dot_general(a_blk.astype(BF16), b_ref[...].astype(BF16), dims, preferred_element_type=F32)
            prod = term if prod is None else prod + term

        def finish(r):
            if epilogue is not None:
                r = epilogue(r, *[e[...] for e in extra_refs])
            o_ref[...] = r.astype(o_ref.dtype)

        if nk == 1:
            finish(prod)
            return
        acc_ref = rest[n_extra + 1]
        kk = pl.program_id(2)

        @pl.when(kk == 0)
        def _():
            acc_ref[...] = prod

        @pl.when(kk > 0)
        def _():
            acc_ref[...] += prod

        @pl.when(kk == nk - 1)
        def _():
            finish(acc_ref[...])

    scratch = [pltpu.VMEM(acc_shape, F32)] if nk > 1 else []
    return pl.pallas_call(
        body, name=name, grid=grid, in_specs=[a_spec, *b_specs, *extra_specs], out_specs=out_spec, out_shape=out_shape,
        scratch_shapes=scratch, input_output_aliases=aliases or {},
        compiler_params=_params("parallel", "parallel", "arbitrary"))(a, *([b] * nb), *extras)


def _mm_x_w(name, a, wg, kind, kdim, ndim, out_dtype, residual=None, split=False):
    rows = a.shape[0]
    tk, tn = _wtiles(kind, kdim, ndim)
    tm = _tile(rows, 1024, 16)
    if kind == 'col':
        tk = kdim
        b_spec = [_gspec(kind, kdim, ndim, kdim // 2, tn, lambda i, j, k, p=p: (p, j)) for p in range(2)]
    else:
        b_spec = _gspec(kind, kdim, ndim, tk, tn, lambda i, j, k: (k, j))
    grid = (rows // tm, ndim // tn, kdim // tk)
    extras, especs, epi = (), (), None
    if residual is not None:
        extras, especs = (residual,), (pl.BlockSpec((tm, tn), lambda i, j, k: (i, j)),)
        epi = lambda r, res: r + res
    out_shape = (2, rows, ndim // 2) if split else (rows, ndim)
    return _matmul(name, 'nn', a, wg, grid, pl.BlockSpec((tm, tk), lambda i, j, k: (i, k)), b_spec,
                   jax.ShapeDtypeStruct(out_shape, out_dtype),
                   _act_spec(tm, tn, (ndim // 2) // tn if split else None, lambda i, j, k: (i, j)),
                   (tm, tn), extras, especs, epi)


def _mm_dy_wt(name, dy, wg, kind, kdim, ndim, out_dtype, split=False, deps=()):
    rows = dy.shape[-2]
    tn, tk = _wtiles(kind, kdim, ndim)
    tm = _tile(rows, 1024, 16)
    if kind == 'col':
        tk = ndim // 4
        b_spec = _gspec(kind, kdim, ndim, tn, tk, lambda i, j, k: (j, k))
    else:
        assert not split
        tk = ndim
        b_spec = [_gspec(kind, kdim, ndim, tn, ndim // 2, lambda i, j, k, p=p: (j, p)) for p in range(2)]
    grid = (rows // tm, kdim // tn, ndim // tk)
    return _matmul(name, 'nt', dy, wg, grid,
                   _act_spec(tm, tk, (ndim // 2) // tk if split else None, lambda i, j, k: (i, k)), b_spec,
                   jax.ShapeDtypeStruct((rows, kdim), out_dtype), pl.BlockSpec((tm, tn), lambda i, j, k: (i, j)),
                   (tm, tn), extras=tuple(deps), extra_specs=(ANY,) * len(deps))


def _mm_xt_dy(name, xa, dy, kind, kdim, ndim, split=False):
    rows = xa.shape[0]
    tm, tn = _wtiles(kind, kdim, ndim)
    tl = _tile(rows, 2048, 16)
    grid = (kdim // tm, ndim // tn, rows // tl)
    rdim, cdim = (kdim // 2, ndim // 4) if kind == 'col' else (kdim // 4, ndim // 2)
    return _matmul(name, 'tn', xa, dy, grid, pl.BlockSpec((tl, tm), lambda i, j, k: (k, i)),
                   _act_spec(tl, tn, (ndim // 2) // tn if split else None, lambda i, j, k: (k, j)),
                   jax.ShapeDtypeStruct((4, 2, rdim, cdim), BF16),
                   _gspec(kind, kdim, ndim, tm, tn, lambda i, j, k: (i, j)), (tm, tn))


def _rms_fwd(name, h, g):
    rows, d = h.shape
    tm = _tile(rows, 256, 16)

    def body(h_ref, g_ref, o_ref):
        x = h_ref[...]
        r = lax.rsqrt(jnp.mean(x * x, axis=-1, keepdims=True) + EPS)
        o_ref[...] = (x * r * g_ref[...]).astype(o_ref.dtype)

    return pl.pallas_call(
        body, name=name, grid=(rows // tm,),
        in_specs=[pl.BlockSpec((tm, d), lambda i: (i, 0)), pl.BlockSpec((1, d), lambda i: (0, 0))],
        out_specs=pl.BlockSpec((tm, d), lambda i: (i, 0)), out_shape=jax.ShapeDtypeStruct((rows, d), BF16),
        compiler_params=_params("parallel"))(h, g)


def _rms_bwd(name, h, g, dhn, dres):
    rows, d = h.shape
    tm = _tile(rows, 256, 16)

    def body(h_ref, g_ref, dy_ref, dres_ref, dh_ref, dg_ref):
        x = h_ref[...]
        r = lax.rsqrt(jnp.mean(x * x, axis=-1, keepdims=True) + EPS)
        xh = x * r
        dy = dy_ref[...].astype(F32)
        gy = dy * g_ref[...]
        dh_ref[...] = dres_ref[...] + r * (gy - xh * jnp.mean(gy * xh, axis=-1, keepdims=True))
        part = jnp.sum(dy * xh, axis=0, keepdims=True)

        @pl.when(pl.program_id(0) == 0)
        def _():
            dg_ref[...] = part

        @pl.when(pl.program_id(0) > 0)
        def _():
            dg_ref[...] += part

    row = pl.BlockSpec((tm, d), lambda i: (i, 0))
    vec = pl.BlockSpec((1, d), lambda i: (0, 0))
    return pl.pallas_call(
        body, name=name, grid=(rows // tm,), in_specs=[row, vec, row, row], out_specs=[row, vec],
        out_shape=[jax.ShapeDtypeStruct((rows, d), F32), jax.ShapeDtypeStruct((1, d), F32)],
        compiler_params=_params("arbitrary"))(h, g, dhn, dres)


def _loss_head(h, g, target):
    rows, d = h.shape
    tm = _tile(rows, 256, 16)

    def body(h_ref, g_ref, t_ref, dh_ref, dg_ref, loss_ref):
        x = h_ref[...]
        r = lax.rsqrt(jnp.mean(x * x, axis=-1, keepdims=True) + EPS)
        xh = x * r
        err = xh * g_ref[...] - t_ref[...]
        dy = err * (1.0 / d)
        gy = dy * g_ref[...]
        dh_ref[...] = r * (gy - xh * jnp.mean(gy * xh, axis=-1, keepdims=True))
        part = jnp.sum(dy * xh, axis=0, keepdims=True)
        sq = jnp.sum(err * err, axis=0, keepdims=True) * (0.5 / d)

        @pl.when(pl.program_id(0) == 0)
        def _():
            dg_ref[...] = part
            loss_ref[...] = sq

        @pl.when(pl.program_id(0) > 0)
        def _():
            dg_ref[...] += part
            loss_ref[...] += sq

    row = pl.BlockSpec((tm, d), lambda i: (i, 0))
    vec = pl.BlockSpec((1, d), lambda i: (0, 0))
    return pl.pallas_call(
        body, name="loss_head", grid=(rows // tm,), in_specs=[row, vec, row], out_specs=[row, vec, vec],
        out_shape=[jax.ShapeDtypeStruct((rows, d), F32), jax.ShapeDtypeStruct((1, d), F32),
                   jax.ShapeDtypeStruct((1, d), F32)],
        compiler_params=_params("arbitrary"))(h, g, target)


def _shift_down(cur, prev8, first, k):
    rows = cur.shape[0]
    rolled = pltpu.roll(cur, k, axis=0)
    idx = lax.broadcasted_iota(jnp.int32, cur.shape, 0)
    prev8 = jnp.where(first, 0.0, prev8)
    out = rolled
    for r in range(k):
        out = jnp.where(idx == r, prev8[8 - k + r:8 - k + r + 1, :], out)
    del rows
    return out


def _shift_up(cur, next8, last, k):
    rows = cur.shape[0]
    rolled = pltpu.roll(cur, rows - k, axis=0)
    idx = lax.broadcasted_iota(jnp.int32, cur.shape, 0)
    next8 = jnp.where(last, 0.0, next8)
    out = rolled
    for r in range(k):
        out = jnp.where(idx == rows - k + r, next8[r:r + 1, :], out)
    return out


def _conv_acc(z, zprev, first, w, b):
    z1 = _shift_down(z, zprev, first, 1)
    z2 = _shift_down(z, zprev, first, 2)
    return b + w[2:3, :] * z + w[1:2, :] * z1 + w[0:1, :] * z2, z1, z2


def _ffn_tiles(rows, f):
    return _tile(rows, 512, 16), _tile(f, 512, LANES)


def _ffn_act_fwd(name, z3, cw3, cb3):
    _, rows, f = z3.shape
    tm, tc = _ffn_tiles(rows, f)
    hb = tm // 8

    def body(z_ref, zp_ref, w_ref, b_ref, o_ref):
        first = pl.program_id(0) == 0
        gate, _, _ = _conv_acc(z_ref[0].astype(F32), zp_ref[0].astype(F32), first, w_ref[0], b_ref[0])
        val, _, _ = _conv_acc(z_ref[1].astype(F32), zp_ref[1].astype(F32), first, w_ref[1], b_ref[1])
        o_ref[...] = (gate * jax.nn.sigmoid(gate) * val).astype(o_ref.dtype)

    return pl.pallas_call(
        body, name=name, grid=(rows // tm, f // tc),
        in_specs=[pl.BlockSpec((2, tm, tc), lambda i, j: (0, i, j)),
                  pl.BlockSpec((2, 8, tc), lambda i, j: (0, jnp.maximum(i * hb - 1, 0), j)),
                  pl.BlockSpec((2, 3, tc), lambda i, j: (0, 0, j)), pl.BlockSpec((2, 1, tc), lambda i, j: (0, 0, j))],
        out_specs=pl.BlockSpec((tm, tc), lambda i, j: (i, j)), out_shape=jax.ShapeDtypeStruct((rows, f), BF16),
        compiler_params=_params("parallel", "parallel"))(z3, z3, cw3, cb3)


def _gate_grads(d_a, acc_g, acc_v):
    sig = jax.nn.sigmoid(acc_g)
    return d_a * acc_v * sig * (1.0 + acc_g * (1.0 - sig)), d_a * acc_g * sig


def _ffn_act_bwd(name, z3, da, cw3, cb3):
    _, rows, f = z3.shape
    tm, tc = _ffn_tiles(rows, f)
    hb = tm // 8
    nrow = rows // tm

    def body(z_ref, zp_ref, zn_ref, da_ref, dan_ref, w_ref, b_ref, dz_ref, dcw_ref, dcb_ref):
        i = pl.program_id(1)
        first, last = i == 0, i == nrow - 1
        w, b = (w_ref[0], w_ref[1]), (b_ref[0], b_ref[1])
        z = (z_ref[0].astype(F32), z_ref[1].astype(F32))
        acc, taps = [], []
        for hf in range(2):
            a_h, z1, z2 = _conv_acc(z[hf], zp_ref[hf].astype(F32), first, w[hf], b[hf])
            acc.append(a_h)
            taps.append((z2, z1, z[hf]))
        dacc = _gate_grads(da_ref[...].astype(F32), acc[0], acc[1])
        acc_n = [_conv_acc(zn_ref[hf].astype(F32), z[hf][tm - 8:tm, :], False, w[hf], b[hf])[0] for hf in range(2)]
        dacc_n = _gate_grads(dan_ref[...].astype(F32), acc_n[0], acc_n[1])
        for hf in range(2):
            d = dacc[hf]
            d1 = _shift_up(d, dacc_n[hf], last, 1)
            d2 = _shift_up(d, dacc_n[hf], last, 2)
            dz_ref[hf] = (w[hf][2:3, :] * d + w[hf][1:2, :] * d1 + w[hf][0:1, :] * d2).astype(dz_ref.dtype)
        sums_w = [[jnp.sum(dacc[hf] * t, axis=0, keepdims=True) for t in taps[hf]] for hf in range(2)]
        sums_b = [jnp.sum(dacc[hf], axis=0, keepdims=True) for hf in range(2)]

        @pl.when(first)
        def _():
            for hf in range(2):
                for k in range(3):
                    dcw_ref[hf, k:k + 1, :] = sums_w[hf][k]
                dcb_ref[hf] = sums_b[hf]

        @pl.when(i > 0)
        def _():
            for hf in range(2):
                for k in range(3):
                    dcw_ref[hf, k:k + 1, :] += sums_w[hf][k]
                dcb_ref[hf] += sums_b[hf]

    nxt = lambda i: jnp.minimum((i + 1) * hb, rows // 8 - 1)
    wsp = pl.BlockSpec((2, 3, tc), lambda j, i: (0, 0, j))
    bsp = pl.BlockSpec((2, 1, tc), lambda j, i: (0, 0, j))
    cur = pl.BlockSpec((2, tm, tc), lambda j, i: (0, i, j))
    return pl.pallas_call(
        body, name=name, grid=(f // tc, nrow),
        in_specs=[cur, pl.BlockSpec((2, 8, tc), lambda j, i: (0, jnp.maximum(i * hb - 1, 0), j)),
                  pl.BlockSpec((2, 8, tc), lambda j, i: (0, nxt(i), j)), pl.BlockSpec((tm, tc), lambda j, i: (i, j)),
                  pl.BlockSpec((8, tc), lambda j, i: (nxt(i), j)), wsp, bsp],
        out_specs=[cur, wsp, bsp],
        out_shape=[jax.ShapeDtypeStruct((2, rows, f), BF16), jax.ShapeDtypeStruct((2, 3, f), F32),
                   jax.ShapeDtypeStruct((2, 1, f), F32)],
        compiler_params=_params("parallel", "arbitrary"))(z3, z3, z3, da, da, cw3, cb3)


def _glu_fwd(name, gg, h):
    rows, d2 = gg.shape
    d = d2 // 2
    tm, tc = _tile(rows, 512, 16), _tile(d, 1024, LANES)
    nd = d // tc

    def body(a_ref, b_ref, h_ref, o_ref):
        o_ref[...] = h_ref[...] + a_ref[...].astype(F32) * jax.nn.sigmoid(b_ref[...].astype(F32))

    return pl.pallas_call(
        body, name=name, grid=(rows // tm, nd),
        in_specs=[pl.BlockSpec((tm, tc), lambda i, j: (i, j)), pl.BlockSpec((tm, tc), lambda i, j: (i, j + nd)),
                  pl.BlockSpec((tm, tc), lambda i, j: (i, j))],
        out_specs=pl.BlockSpec((tm, tc), lambda i, j: (i, j)), out_shape=jax.ShapeDtypeStruct((rows, d), F32),
        compiler_params=_params("parallel", "parallel"))(gg, gg, h)


def _glu_bwd(name, gg, dh):
    rows, d2 = gg.shape
    d = d2 // 2
    tm, tc = _tile(rows, 512, 16), _tile(d, 1024, LANES)
    nd = d // tc

    def body(s_ref, o_ref, dh_ref, out_ref):
        is_a = pl.program_id(1) < nd
        me = s_ref[...].astype(F32)
        other = o_ref[...].astype(F32)
        g = dh_ref[...]
        sig_o = jax.nn.sigmoid(other)
        sig_m = jax.nn.sigmoid(me)
        out_ref[...] = jnp.where(is_a, g * sig_o, g * other * sig_m * (1.0 - sig_m)).astype(out_ref.dtype)

    return pl.pallas_call(
        body, name=name, grid=(rows // tm, 2 * nd),
        in_specs=[pl.BlockSpec((tm, tc), lambda i, j: (i, j)),
                  pl.BlockSpec((tm, tc), lambda i, j: (i, (j + nd) % (2 * nd))),
                  pl.BlockSpec((tm, tc), lambda i, j: (i, j % nd))],
        out_specs=pl.BlockSpec((tm, tc), lambda i, j: (i, j)), out_shape=jax.ShapeDtypeStruct((rows, d2), BF16),
        compiler_params=_params("parallel", "parallel"))(gg, gg, dh)


def _sgu_common(pre_ref, gv_ref, e):
    u = _gelu(pre_ref[:, :e].astype(F32))
    v = _gelu(pre_ref[:, e:].astype(F32))
    r = lax.rsqrt(jnp.mean(v * v, axis=-1, keepdims=True) + EPS)
    vh = v * r
    return u, vh, r, (vh * gv_ref[...]).astype(BF16)


def _tril_bf16(ws_ref, hd):
    t = lax.broadcasted_iota(jnp.int32, (CHUNK, CHUNK), 0)
    s = lax.broadcasted_iota(jnp.int32, (CHUNK, CHUNK), 1)
    return jnp.where(s <= t, ws_ref[hd], 0.0).astype(BF16)


def _sgu_mix_fwd(name, pre, gv, ws, bsx):
    rows, e2 = pre.shape
    e = e2 // 2
    heads = e // SGU_GROUP
    tr = _tile(rows, 256, CHUNK)

    def body(pre_ref, gv_ref, ws_ref, bs_ref, o_ref):
        u, _, _, vn = _sgu_common(pre_ref, gv_ref, e)
        for hd in range(heads):
            wm = _tril_bf16(ws_ref, hd)
            cols = slice(hd * SGU_GROUP, (hd + 1) * SGU_GROUP)
            for ck in range(tr // CHUNK):
                rws = slice(ck * CHUNK, (ck + 1) * CHUNK)
                s = jnp.dot(wm, vn[rws, cols], preferred_element_type=F32) + bs_ref[hd]
                o_ref[rws, cols] = (u[rws, cols] * s).astype(o_ref.dtype)

    whole3 = pl.BlockSpec((heads, CHUNK, CHUNK), lambda i: (0, 0, 0))
    return pl.pallas_call(
        body, name=name, grid=(rows // tr,),
        in_specs=[pl.BlockSpec((tr, e2), lambda i: (i, 0)), pl.BlockSpec((1, e), lambda i: (0, 0)), whole3, whole3],
        out_specs=pl.BlockSpec((tr, e), lambda i: (i, 0)), out_shape=jax.ShapeDtypeStruct((rows, e), BF16),
        compiler_params=_params("parallel"))(pre, gv, ws, bsx)


def _sgu_mix_bwd(name, pre, dus, gv, ws, bsx):
    rows, e2 = pre.shape
    e = e2 // 2
    heads = e // SGU_GROUP
    tr = _tile(rows, 256, CHUNK)

    def body(pre_ref, dus_ref, gv_ref, ws_ref, bs_ref, dpre_ref, dws_ref, dbs_ref, dgv_ref, dvn_ref, du_ref):
        first = pl.program_id(0) == 0
        u, vh, r, vn = _sgu_common(pre_ref, gv_ref, e)
        ones = jnp.ones((SGU_GROUP, LANES), BF16)
        tt = lax.broadcasted_iota(jnp.int32, (CHUNK, CHUNK), 0)
        ss = lax.broadcasted_iota(jnp.int32, (CHUNK, CHUNK), 1)
        for hd in range(heads):
            wm = _tril_bf16(ws_ref, hd)
            cols = slice(hd * SGU_GROUP, (hd + 1) * SGU_GROUP)
            dw = jnp.zeros((CHUNK, CHUNK), F32)
            db = jnp.zeros((CHUNK, LANES), F32)
            for ck in range(tr // CHUNK):
                rws = slice(ck * CHUNK, (ck + 1) * CHUNK)
                vblk = vn[rws, cols]
                s = jnp.dot(wm, vblk, preferred_element_type=F32) + bs_ref[hd]
                d_us = dus_ref[rws, cols].astype(F32)
                du_ref[rws, cols] = d_us * s
                ds = (d_us * u[rws, cols]).astype(BF16)
                dvn_ref[rws, cols] = lax.dot_general(wm, ds, _DIMS['tn'], preferred_element_type=F32)
                dw = dw + lax.dot_general(ds, vblk, _DIMS['nt'], preferred_element_type=F32)
                db = db + jnp.dot(ds, ones, preferred_element_type=F32)
            dw = jnp.where(ss <= tt, dw, 0.0)

            @pl.when(first)
            def _():
                dws_ref[hd] = dw
                dbs_ref[hd] = db

            @pl.when(jnp.logical_not(first))
            def _():
                dws_ref[hd] += dw
                dbs_ref[hd] += db

        dvn = dvn_ref[...]
        part = jnp.sum(dvn * vh, axis=0, keepdims=True)

        @pl.when(first)
        def _():
            dgv_ref[...] = part

        @pl.when(jnp.logical_not(first))
        def _():
            dgv_ref[...] += part

        gy = dvn * gv_ref[...]
        dv = r * (gy - vh * jnp.mean(gy * vh, axis=-1, keepdims=True))
        dpre_ref[:, :e] = (du_ref[...] * _gelu_grad(pre_ref[:, :e].astype(F32))).astype(dpre_ref.dtype)
        dpre_ref[:, e:] = (dv * _gelu_grad(pre_ref[:, e:].astype(F32))).astype(dpre_ref.dtype)

    whole3 = pl.BlockSpec((heads, CHUNK, CHUNK), lambda i: (0, 0, 0))
    vec = pl.BlockSpec((1, e), lambda i: (0, 0))
    return pl.pallas_call(
        body, name=name, grid=(rows // tr,),
        in_specs=[pl.BlockSpec((tr, e2), lambda i: (i, 0)), pl.BlockSpec((tr, e), lambda i: (i, 0)), vec, whole3, whole3],
        out_specs=[pl.BlockSpec((tr, e2), lambda i: (i, 0)), whole3, whole3, vec],
        out_shape=[jax.ShapeDtypeStruct((rows, e2), BF16), jax.ShapeDtypeStruct((heads, CHUNK, CHUNK), F32),
                   jax.ShapeDtypeStruct((heads, CHUNK, LANES), F32), jax.ShapeDtypeStruct((1, e), F32)],
        scratch_shapes=[pltpu.VMEM((tr, e), F32), pltpu.VMEM((tr, e), F32)],
        compiler_params=_params("arbitrary"))(pre, dus, gv, ws, bsx)


def _disc_a(lr, li, ldt):
    dt = jnp.exp(ldt)
    mag = jnp.exp(dt * lr)
    return mag * jnp.cos(dt * li), mag * jnp.sin(dt * li)


def _disc_b(lr, li, ldt, br, bi):
    ar, ai = _disc_a(lr, li, ldt)
    den = lr * lr + li * li
    qr = ((ar - 1.0) * lr + ai * li) / den
    qi = (ai * lr - (ar - 1.0) * li) / den
    return qr * br - qi * bi, qr * bi + qi * br


def _s5_disc(name, lr, li, ldt, lrx, lix, ldtx, br, bi):
    def body(lr_ref, li_ref, ldt_ref, lrx_ref, lix_ref, ldtx_ref, br_ref, bi_ref, ar_ref, ai_ref, bbr_ref, bbi_ref):
        ar_ref[...], ai_ref[...] = _disc_a(lr_ref[...], li_ref[...], ldt_ref[...])
        bbr_ref[...], bbi_ref[...] = _disc_b(lrx_ref[...], lix_ref[...], ldtx_ref[...], br_ref[...], bi_ref[...])

    small = jax.ShapeDtypeStruct(lr.shape, F32)
    wide = jax.ShapeDtypeStruct(br.shape, F32)
    return pl.pallas_call(body, name=name, out_shape=[small, small, wide, wide],
                          compiler_params=pltpu.CompilerParams(vmem_limit_bytes=VMEM_LIMIT))(
        lr, li, ldt, lrx, lix, ldtx, br, bi)


def _s5_disc_bwd(name, lr, li, ldt, lrx, lix, ldtx, br, bi, dar, dai, dbbr, dbbi, sel):
    def body(lr_ref, li_ref, ldt_ref, lrx_ref, lix_ref, ldtx_ref, br_ref, bi_ref, dar_ref, dai_ref, dbbr_ref,
             dbbi_ref, sel_ref, dlr_ref, dli_ref, dldt_ref, dbr_ref, dbi_ref):
        _, vjp_a = jax.vjp(_disc_a, lr_ref[...], li_ref[...], ldt_ref[...])
        g_lr, g_li, g_ldt = vjp_a((dar_ref[...], dai_ref[...]))
        _, vjp_b = jax.vjp(_disc_b, lrx_ref[...], lix_ref[...], ldtx_ref[...], br_ref[...], bi_ref[...])
        x_lr, x_li, x_ldt, g_br, g_bi = vjp_b((dbbr_ref[...], dbbi_ref[...]))
        fold = lambda t: jnp.dot(t, sel_ref[...], precision=lax.Precision.HIGHEST, preferred_element_type=F32)
        dlr_ref[...] = g_lr + fold(x_lr)
        dli_ref[...] = g_li + fold(x_li)
        dldt_ref[...] = jnp.sum(g_ldt + fold(x_ldt), axis=1, keepdims=True)
        dbr_ref[...] = g_br
        dbi_ref[...] = g_bi

    small = jax.ShapeDtypeStruct(lr.shape, F32)
    wide = jax.ShapeDtypeStruct(br.shape, F32)
    return pl.pallas_call(body, name=name,
                          out_shape=[small, small, jax.ShapeDtypeStruct((lr.shape[0], 1), F32), wide, wide],
                          compiler_params=pltpu.CompilerParams(vmem_limit_bytes=VMEM_LIMIT))(
        lr, li, ldt, lrx, lix, ldtx, br, bi, dar, dai, dbbr, dbbi, sel)


def _cmul(ar, ai, br, bi):
    return ar * br - ai * bi, ar * bi + ai * br


def _pow_seg(ar, ai):
    res, base, n = None, (ar, ai), SEG
    while n:
        if n & 1:
            res = base if res is None else _cmul(*res, *base)
        n >>= 1
        if n:
            base = _cmul(*base, *base)
    return res


def _scan_forward(hr_ref, hi_ref, er_ref, ei_ref, sr_ref, si_ref, ar, ai, nck):
    arb, aib = jnp.broadcast_to(ar, (nck, LANES)), jnp.broadcast_to(ai, (nck, LANES))

    def intra(t, carry):
        sr, si = carry
        slab = pl.ds(t, nck, stride=SEG)
        nr = arb * sr - aib * si + hr_ref[slab, :]
        ni = arb * si + aib * sr + hi_ref[slab, :]
        hr_ref[slab, :] = nr
        hi_ref[slab, :] = ni
        return nr, ni

    zero = jnp.zeros((nck, LANES), F32)
    er_ref[...], ei_ref[...] = lax.fori_loop(0, SEG, intra, (zero, zero), unroll=4)
    pcr, pci = _pow_seg(ar, ai)
    sr_ref[0:1, :] = jnp.zeros((1, LANES), F32)
    si_ref[0:1, :] = jnp.zeros((1, LANES), F32)
    for ck in range(nck - 1):
        pr, pi = sr_ref[ck:ck + 1, :], si_ref[ck:ck + 1, :]
        sr_ref[ck + 1:ck + 2, :] = pcr * pr - pci * pi + er_ref[ck:ck + 1, :]
        si_ref[ck + 1:ck + 2, :] = pcr * pi + pci * pr + ei_ref[ck:ck + 1, :]
    s_r, s_i = sr_ref[...], si_ref[...]

    def fix(t, carry):
        pr, pi = carry
        slab = pl.ds(t, nck, stride=SEG)
        hr_ref[slab, :] = hr_ref[slab, :] + (pr * s_r - pi * s_i)
        hi_ref[slab, :] = hi_ref[slab, :] + (pr * s_i + pi * s_r)
        return _cmul(pr, pi, arb, aib)

    lax.fori_loop(0, SEG, fix, (arb, aib), unroll=4)


def _scan_backward(gr_ref, gi_ref, hr_ref, hi_ref, er_ref, ei_ref, sr_ref, si_ref, ar, ai, nck):
    arb, aib = jnp.broadcast_to(ar, (nck, LANES)), jnp.broadcast_to(-ai, (nck, LANES))

    def intra(k, carry):
        sr, si = carry
        slab = pl.ds(SEG - 1 - k, nck, stride=SEG)
        nr = arb * sr - aib * si + gr_ref[slab, :]
        ni = arb * si + aib * sr + gi_ref[slab, :]
        gr_ref[slab, :] = nr
        gi_ref[slab, :] = ni
        return nr, ni

    zero = jnp.zeros((nck, LANES), F32)
    er_ref[...], ei_ref[...] = lax.fori_loop(0, SEG, intra, (zero, zero), unroll=4)
    pcr, pci = _pow_seg(ar, -ai)
    sr_ref[nck - 1:nck, :] = jnp.zeros((1, LANES), F32)
    si_ref[nck - 1:nck, :] = jnp.zeros((1, LANES), F32)
    for ck in range(nck - 1, 0, -1):
        pr, pi = sr_ref[ck:ck + 1, :], si_ref[ck:ck + 1, :]
        sr_ref[ck - 1:ck, :] = pcr * pr - pci * pi + er_ref[ck:ck + 1, :]
        si_ref[ck - 1:ck, :] = pcr * pi + pci * pr + ei_ref[ck:ck + 1, :]
    s_r, s_i = sr_ref[...], si_ref[...]
    last = pl.ds(SEG - 1, nck, stride=SEG)
    row = lax.broadcasted_iota(jnp.int32, (nck, LANES), 0)
    hp_r = jnp.where(row == 0, 0.0, pltpu.roll(hr_ref[last, :], 1, axis=0)) if nck > 1 else zero
    hp_i = jnp.where(row == 0, 0.0, pltpu.roll(hi_ref[last, :], 1, axis=0)) if nck > 1 else zero

    def settle(t, pr, pi, h_r, h_i):
        slab = pl.ds(t, nck, stride=SEG)
        g_r = gr_ref[slab, :] + (pr * s_r - pi * s_i)
        g_i = gi_ref[slab, :] + (pr * s_i + pi * s_r)
        gr_ref[slab, :] = g_r
        gi_ref[slab, :] = g_i
        return g_r * h_r + g_i * h_i, g_i * h_r - g_r * h_i

    def fix(k, carry):
        pr, pi, acr, aci = carry
        t = SEG - 1 - k
        prev = pl.ds(t - 1, nck, stride=SEG)
        d_r, d_i = settle(t, pr, pi, hr_ref[prev, :], hi_ref[prev, :])
        nr, ni = _cmul(pr, pi, arb, aib)
        return nr, ni, acr + d_r, aci + d_i

    pr, pi, acr, aci = lax.fori_loop(0, SEG - 1, fix, (arb, aib, zero, zero), unroll=4)
    d_r, d_i = settle(0, pr, pi, hp_r, hp_i)
    return jnp.sum(acr + d_r, axis=0, keepdims=True), jnp.sum(aci + d_i, axis=0, keepdims=True)


def _s5_fill_states(u_ref, br_ref, bi_ref, hr_ref, hi_ref, rows):
    ub = u_ref[...].astype(BF16)
    hr_ref[0:rows, :] = jnp.dot(ub, br_ref[...], preferred_element_type=F32)
    hi_ref[0:rows, :] = jnp.dot(ub, bi_ref[...], preferred_element_type=F32)
    pad = jnp.zeros((hr_ref.shape[0] - rows, LANES), F32)
    hr_ref[rows:, :] = pad
    hi_ref[rows:, :] = pad


def _s5_specs(rows, e):
    sb = STATE_BLOCKS
    chan = pl.BlockSpec((rows, LANES), lambda j: (0, j // sb))
    bmat = pl.BlockSpec((None, LANES, LANES), lambda j: (j // sb, 0, j % sb))
    cmat = pl.BlockSpec((None, LANES, LANES), lambda j: (j // sb, j % sb, 0))
    avec = pl.BlockSpec((1, LANES), lambda j: (0, j))
    dvec = pl.BlockSpec((1, LANES), lambda j: (0, j // sb))
    return chan, bmat, cmat, avec, dvec


def _s5_fwd(name, u, bre, bim, crt, cit, ar, ai, dd):
    rows, e = u.shape
    nck = rows // CHUNK
    nsteps = (e // LANES) * STATE_BLOCKS
    chan, bmat, cmat, avec, dvec = _s5_specs(rows, e)

    def body(u_ref, br_ref, bi_ref, cr_ref, ci_ref, ar_ref, ai_ref, dd_ref, y_ref, gy_ref,
             hr_ref, hi_ref, er_ref, ei_ref, sr_ref, si_ref, acc_ref):
        j = pl.program_id(0) % STATE_BLOCKS
        _s5_fill_states(u_ref, br_ref, bi_ref, hr_ref, hi_ref, rows)
        _scan_forward(hr_ref, hi_ref, er_ref, ei_ref, sr_ref, si_ref, ar_ref[...], ai_ref[...], nck)
        contrib = (jnp.dot(hr_ref[0:rows, :].astype(BF16), cr_ref[...], preferred_element_type=F32)
                   - jnp.dot(hi_ref[0:rows, :].astype(BF16), ci_ref[...], preferred_element_type=F32))

        @pl.when(j == 0)
        def _():
            acc_ref[...] = dd_ref[...] * u_ref[...] + contrib

        @pl.when(j > 0)
        def _():
            acc_ref[...] += contrib

        @pl.when(j == STATE_BLOCKS - 1)
        def _():
            y = acc_ref[...]
            y_ref[...] = y.astype(y_ref.dtype)
            gy_ref[...] = _gelu(y).astype(gy_ref.dtype)

    flat = pltpu.VMEM((rows, LANES), F32)
    big = pltpu.VMEM((nck * SEG, LANES), F32)
    small = pltpu.VMEM((nck, LANES), F32)
    out = jax.ShapeDtypeStruct((rows, e), BF16)
    return pl.pallas_call(
        body, name=name, grid=(nsteps,), in_specs=[chan, bmat, bmat, cmat, cmat, avec, avec, dvec],
        out_specs=[chan, chan], out_shape=[out, out], scratch_shapes=[big, big, small, small, small, small, flat],
        compiler_params=_params("arbitrary"))(u, bre, bim, crt, cit, ar, ai, dd)


def _s5_bwd(name, u, y, dgy, bre, bim, crt, cit, ar, ai, dd):
    rows, e = u.shape
    nb = e // LANES
    nck = rows // CHUNK
    nsteps = nb * STATE_BLOCKS
    chan, bmat, cmat, avec, dvec = _s5_specs(rows, e)

    def body(u_ref, y_ref, dgy_ref, br_ref, bi_ref, cr_ref, ci_ref, ar_ref, ai_ref, dd_ref,
             du_ref, dbr_ref, dbi_ref, dcr_ref, dci_ref, dar_ref, dai_ref, ddd_ref,
             hr_ref, hi_ref, gr_ref, gi_ref, er_ref, ei_ref, sr_ref, si_ref, acc_ref, dy_ref):
        j = pl.program_id(0) % STATE_BLOCKS
        _s5_fill_states(u_ref, br_ref, bi_ref, hr_ref, hi_ref, rows)
        _scan_forward(hr_ref, hi_ref, er_ref, ei_ref, sr_ref, si_ref, ar_ref[...], ai_ref[...], nck)

        @pl.when(j == 0)
        def _():
            dy0 = dgy_ref[...].astype(F32) * _gelu_grad(y_ref[...].astype(F32))
            dy_ref[...] = dy0
            ddd_ref[...] = jnp.sum(dy0 * u_ref[...], axis=0, keepdims=True)

        dyb = dy_ref[...].astype(BF16)
        pad = jnp.zeros((gr_ref.shape[0] - rows, LANES), F32)
        gr_ref[0:rows, :] = lax.dot_general(dyb, cr_ref[...], _DIMS['nt'], preferred_element_type=F32)
        gi_ref[0:rows, :] = -lax.dot_general(dyb, ci_ref[...], _DIMS['nt'], preferred_element_type=F32)
        gr_ref[rows:, :] = pad
        gi_ref[rows:, :] = pad
        dcr_ref[...] = lax.dot_general(hr_ref[0:rows, :].astype(BF16), dyb, _DIMS['tn'], preferred_element_type=F32)
        dci_ref[...] = -lax.dot_general(hi_ref[0:rows, :].astype(BF16), dyb, _DIMS['tn'], preferred_element_type=F32)
        dar_ref[...], dai_ref[...] = _scan_backward(gr_ref, gi_ref, hr_ref, hi_ref, er_ref, ei_ref, sr_ref, si_ref,
                                                    ar_ref[...], ai_ref[...], nck)
        ub = u_ref[...].astype(BF16)
        grb, gib = gr_ref[0:rows, :].astype(BF16), gi_ref[0:rows, :].astype(BF16)
        dbr_ref[...] = lax.dot_general(ub, grb, _DIMS['tn'], preferred_element_type=F32)
        dbi_ref[...] = lax.dot_general(ub, gib, _DIMS['tn'], preferred_element_type=F32)
        contrib = (lax.dot_general(grb, br_ref[...], _DIMS['nt'], preferred_element_type=F32)
                   + lax.dot_general(gib, bi_ref[...], _DIMS['nt'], preferred_element_type=F32))

        @pl.when(j == 0)
        def _():
            acc_ref[...] = dd_ref[...] * dy_ref[...] + contrib

        @pl.when(j > 0)
        def _():
            acc_ref[...] += contrib

        @pl.when(j == STATE_BLOCKS - 1)
        def _():
            du_ref[...] = acc_ref[...]

    flat = pltpu.VMEM((rows, LANES), F32)
    big = pltpu.VMEM((nck * SEG, LANES), F32)
    small = pltpu.VMEM((nck, LANES), F32)
    bshape = jax.ShapeDtypeStruct((nb, LANES, LANES * STATE_BLOCKS), F32)
    cshape = jax.ShapeDtypeStruct((nb, LANES * STATE_BLOCKS, LANES), F32)
    ashape = jax.ShapeDtypeStruct((1, nb * LANES * STATE_BLOCKS), F32)
    return pl.pallas_call(
        body, name=name, grid=(nsteps,),
        in_specs=[chan, chan, chan, bmat, bmat, cmat, cmat, avec, avec, dvec],
        out_specs=[chan, bmat, bmat, cmat, cmat, avec, avec, dvec],
        out_shape=[jax.ShapeDtypeStruct((rows, e), F32), bshape, bshape, cshape, cshape, ashape, ashape,
                   jax.ShapeDtypeStruct((1, e), F32)],
        scratch_shapes=[big, big, big, big, small, small, small, small, flat, flat],
        compiler_params=_params("arbitrary"))(u, y, dgy, bre, bim, crt, cit, ar, ai, dd)


def _to_blockdiag_b(bbar, nb):
    eye = jnp.eye(GROUPS_PER_BLOCK, dtype=bbar.dtype)
    t = jnp.einsum('bgpc,gh->bgchp', bbar.reshape(nb, GROUPS_PER_BLOCK, SSM_STATE, SSM_GROUP), eye)
    return t.reshape(nb, LANES, GROUPS_PER_BLOCK * SSM_STATE)


def _from_blockdiag_b(dmat, nb):
    eye = jnp.eye(GROUPS_PER_BLOCK, dtype=dmat.dtype)
    t = dmat.reshape(nb, GROUPS_PER_BLOCK, SSM_GROUP, GROUPS_PER_BLOCK, SSM_STATE)
    return jnp.einsum('bgchp,gh->bgpc', t, eye).reshape(nb * GROUPS_PER_BLOCK, SSM_STATE, SSM_GROUP)


def _to_blockdiag_ct(c, nb):
    eye = jnp.eye(GROUPS_PER_BLOCK, dtype=c.dtype)
    t = jnp.einsum('bgop,gh->bgpho', c.reshape(nb, GROUPS_PER_BLOCK, SSM_GROUP, SSM_STATE), eye)
    return t.reshape(nb, GROUPS_PER_BLOCK * SSM_STATE, LANES)


def _from_blockdiag_ct(dmat, nb):
    eye = jnp.eye(GROUPS_PER_BLOCK, dtype=dmat.dtype)
    t = dmat.reshape(nb, GROUPS_PER_BLOCK, SSM_STATE, GROUPS_PER_BLOCK, SSM_GROUP)
    return jnp.einsum('bgpho,gh->bgop', t, eye).reshape(nb * GROUPS_PER_BLOCK, SSM_GROUP, SSM_STATE)


ANY = pl.BlockSpec(memory_space=pl.ANY)


def _half_specs(kind, rdim, cdim, tr, tc, layer):
    nr, nc = rdim // tr, cdim // tc
    if kind == 'col':
        nat = pl.BlockSpec((None, tr, tc), lambda c, rb, cb: (layer, c * nr + rb, cb))
    else:
        nat = pl.BlockSpec((None, tr, tc), lambda c, rb, cb: (layer, rb, c * nc + cb))
    half = pl.BlockSpec((None, tr, tc), lambda c, rb, cb: (c, rb, cb))
    return nat, half


def _my_chip():
    return 2 * lax.axis_index("x") + lax.axis_index("y")


def _cast_halves(name, w, kind, layer):
    rdim, cdim = _half_shape(kind, w.shape)
    tr, tc = _tile(rdim, 512, 16), _tile(cdim, 1408, LANES)
    nat, _ = _half_specs(kind, rdim, cdim, tr, tc, layer)
    slot = pl.BlockSpec((None, None, tr, tc), lambda c, rb, cb: (_my_chip(), c, rb, cb))

    def body(w_ref, o_ref):
        o_ref[...] = w_ref[...].astype(o_ref.dtype)

    return pl.pallas_call(
        body, name=name, grid=(2, rdim // tr, cdim // tc), in_specs=[nat], out_specs=slot,
        out_shape=jax.ShapeDtypeStruct((4, 2, rdim, cdim), BF16),
        compiler_params=_params("parallel", "parallel", "parallel"))(w)


def _adam_math(w, g, m, v):
    m = ADAM_B1 * m + (1.0 - ADAM_B1) * g
    v = ADAM_B2 * v + (1.0 - ADAM_B2) * (g * g)
    m_hat = m / (1.0 - ADAM_B1 ** ADAM_STEP)
    v_hat = v / (1.0 - ADAM_B2 ** ADAM_STEP)
    delta = -ADAM_LR * (m_hat / (jnp.sqrt(v_hat) + ADAM_EPS) + ADAM_WD * w)
    return delta, m, v


def _adam_big(name, w, m, v, gfull, kind, layer, outs):
    rdim, cdim = _half_shape(kind, w.shape)
    tr, tc = _tile(rdim, 256, 8), _tile(cdim, 1408, LANES)
    nat, half = _half_specs(kind, rdim, cdim, tr, tc, layer)

    def body(w_ref, m_ref, v_ref, g_ref, *rest):
        go_ref, d_ref, mo_ref, vo_ref = rest[4:]
        g = g_ref[...]
        go_ref[...] = g
        d_ref[...], mo_ref[...], vo_ref[...] = _adam_math(w_ref[...], g, m_ref[...], v_ref[...])

    shape = jax.ShapeDtypeStruct(w.shape, F32)
    return pl.pallas_call(
        body, name=name, grid=(2, rdim // tr, cdim // tc), in_specs=[nat, nat, nat, half] + [ANY] * 4,
        out_specs=[nat, nat, nat, nat], out_shape=[shape, shape, shape, shape],
        input_output_aliases={4: 0, 5: 1, 6: 2, 7: 3},
        compiler_params=_params("parallel", "parallel", "parallel"))(w, m, v, gfull, *outs)


def _adam_small(w, m, v, g):
    rows = w.shape[0]
    tr = _tile(rows, 512, 8)
    spec = pl.BlockSpec((tr, LANES), lambda i: (i, 0))

    def body(w_ref, m_ref, v_ref, g_ref, d_ref, mo_ref, vo_ref):
        d_ref[...], mo_ref[...], vo_ref[...] = _adam_math(w_ref[...], g_ref[...], m_ref[...], v_ref[...])

    shape = jax.ShapeDtypeStruct(w.shape, F32)
    return pl.pallas_call(body, name="adam_small", grid=(rows // tr,), in_specs=[spec] * 4, out_specs=[spec] * 3,
                          out_shape=[shape] * 3, compiler_params=_params("parallel"))(w, m, v, g)


def _add2(name, part, got):
    cdim = part.shape[-1]
    a2, b2 = part.reshape(4, 2, -1, cdim), got.reshape(4, -1, cdim)
    rows = b2.shape[1]
    tr, tc = _tile(rows, 512, 16), _tile(cdim, 1408, LANES)
    mine = pl.BlockSpec((None, None, tr, tc), lambda k, i, j: (k, lax.axis_index("c"), i, j))
    spec = pl.BlockSpec((None, tr, tc), lambda k, i, j: (k, i, j))

    def body(a_ref, b_ref, o_ref):
        o_ref[...] = (a_ref[...].astype(F32) + b_ref[...].astype(F32)).astype(o_ref.dtype)

    out = pl.pallas_call(
        body, name=name, grid=(4, rows // tr, cdim // tc), in_specs=[mine, spec], out_specs=spec,
        out_shape=jax.ShapeDtypeStruct(b2.shape, BF16),
        compiler_params=_params("parallel", "parallel", "parallel"))(a2, b2)
    return out.reshape(got.shape)


def _add4(name, sums, recv):
    cdim = sums.shape[-1]
    s2 = sums.reshape(4, -1, cdim)
    r3 = recv.reshape(3, -1, cdim)
    rows = s2.shape[1]
    tr, tc = _tile(rows, 512, 16), _tile(cdim, 1408, LANES)
    own = pl.BlockSpec((None, tr, tc), lambda i, j: (_my_chip(), i, j))
    rspec = lambda k: pl.BlockSpec((None, tr, tc), lambda i, j: (k, i, j))
    slot = pl.BlockSpec((None, tr, tc), lambda i, j: (lax.axis_index("c"), i, j))

    def body(o_ref, x_ref, y_ref, d_ref, out_ref):
        out_ref[...] = ((o_ref[...].astype(F32) + d_ref[...].astype(F32))
                        + (x_ref[...].astype(F32) + y_ref[...].astype(F32)))

    out = pl.pallas_call(
        body, name=name, grid=(rows // tr, cdim // tc), in_specs=[own, rspec(0), rspec(1), rspec(2)], out_specs=slot,
        out_shape=jax.ShapeDtypeStruct((2, rows, cdim), F32),
        compiler_params=_params("parallel", "parallel"))(s2, r3, r3, r3)
    return out.reshape(2, *sums.shape[1:])


def _place():
    x, y, c = lax.axis_index("x"), lax.axis_index("y"), lax.axis_index("c")
    chips = [(1 - x, y), (x, 1 - y), (1 - x, 1 - y)]
    return x, y, c, chips


def _remote(src, dst, send, recv, to):
    return pltpu.make_async_remote_copy(src_ref=src, dst_ref=dst, send_sem=send, recv_sem=recv, device_id=to,
                                        device_id_type=MESH)


def _pieces(src, dst, bands):
    lead, rows = src.shape[:-2], src.shape[-2]
    band = rows // bands
    out = []
    for idx in itertools.product(*[range(dim) for dim in lead]):
        for q in range(bands):
            sl = (*idx, pl.ds(q * band, band))
            out.append((src.at[sl], dst.at[sl]))
    return out


HBM = pl.BlockSpec(memory_space=pltpu.HBM)
SEM = pl.BlockSpec(memory_space=pltpu.SEMAPHORE)
EFFECT = pltpu.SideEffectType.DATAFLOW_SIDE_EFFECTING


def _split_start(name, bufs, ncopy, plan, deps=()):
    nb, nd = len(bufs), len(deps)

    def body(*refs):
        ins, send, recv, token = refs[:nb], refs[nb + nd], refs[nb + nd + 1], refs[2 * nb + nd + 2]
        for k, (src, dst, _, to, bands) in enumerate(plan(ins)):
            for s, d in _pieces(src, dst, bands):
                _remote(s, d, send.at[k], recv.at[k], to).start()
        token[...] = jnp.zeros_like(token)

    res = pl.pallas_call(
        body, name=name, in_specs=[HBM] * nb + [ANY] * nd,
        out_specs=[SEM, SEM] + [HBM] * nb + [pl.BlockSpec(memory_space=pltpu.VMEM)],
        out_shape=[pltpu.SemaphoreType.DMA((ncopy,)), pltpu.SemaphoreType.DMA((ncopy,))]
        + [pltpu.HBM(b.shape, b.dtype) for b in bufs] + [jax.ShapeDtypeStruct((8, LANES), F32)],
        input_output_aliases={a: a + 2 for a in range(nb)},
        compiler_params=pltpu.CompilerParams(has_side_effects=EFFECT))(
        *[pltpu.with_memory_space_constraint(b, pltpu.HBM) for b in bufs], *deps)
    return res[0], res[1], list(res[2:2 + nb]), res[2 + nb]


def _split_wait(name, send, recv, bufs, plan, after):
    nb = len(bufs)

    def body(*refs):
        ins, send_sem, recv_sem = refs[:nb], refs[nb], refs[nb + 1]
        for k, (src, dst, landing, to, _) in enumerate(plan(ins)):
            _remote(src, dst, send_sem.at[k], recv_sem.at[k], to).wait_send()
            _remote(src, landing, send_sem.at[k], recv_sem.at[k], to).wait_recv()

    return pl.pallas_call(
        body, name=name, in_specs=[HBM] * nb + [SEM, SEM, ANY], out_specs=[HBM] * nb,
        out_shape=[pltpu.HBM(b.shape, b.dtype) for b in bufs], input_output_aliases={a: a for a in range(nb)},
        compiler_params=pltpu.CompilerParams(has_side_effects=EFFECT))(*bufs, send, recv, after)


def _gather_plan(n):
    def plan(refs):
        x, y, c, chips = _place()
        kme = 2 * x + y
        return [(refs[a].at[kme, c], refs[a].at[kme, c], refs[a].at[2 * chip[0] + chip[1], c], (*chip, c), 2)
                for a in range(n) for chip in chips]
    return plan


def _scatter_plan(n):
    def plan(refs):
        x, y, c, chips = _place()
        return [(refs[a].at[2 * chip[0] + chip[1]], refs[n + a].at[r], refs[n + a].at[r], (*chip, c), 2)
                for a in range(n) for r, chip in enumerate(chips)]
    return plan


def _forward_halves(slots):
    n = len(slots)

    def body(*refs):
        outs = refs[n:2 * n]
        send, recv = refs[2 * n:]
        x, y, c, chips = _place()
        sib = (x, y, 1 - c)
        for a in range(n):
            for r, chip in enumerate(chips):
                kp = 2 * chip[0] + chip[1]
                for s, d in _pieces(outs[a].at[kp, c], outs[a].at[kp, c], 2):
                    _remote(s, d, send.at[3 * a + r], recv.at[3 * a + r], sib).start()
        for a in range(n):
            for r, chip in enumerate(chips):
                kp = 2 * chip[0] + chip[1]
                _remote(outs[a].at[kp, 1 - c], outs[a].at[kp, 1 - c], send.at[3 * a + r], recv.at[3 * a + r],
                        sib).wait_recv()
                _remote(outs[a].at[kp, c], outs[a].at[kp, c], send.at[3 * a + r], recv.at[3 * a + r], sib).wait_send()

    return pl.pallas_call(
        body, name="allgather_forward", in_specs=[ANY] * n, out_specs=[ANY] * n,
        out_shape=[jax.ShapeDtypeStruct(s.shape, s.dtype) for s in slots],
        input_output_aliases={a: a for a in range(n)},
        scratch_shapes=[pltpu.SemaphoreType.DMA((3 * n,)), pltpu.SemaphoreType.DMA((3 * n,))])(*slots)


def _allgather_small(shards):
    n = len(shards)

    def body(*refs):
        ins, outs = refs[:n], refs[n:2 * n]
        send, recv, loc = refs[2 * n:]
        x, y, c, chips = _place()
        kme = 2 * x + y
        local = [pltpu.make_async_copy(ins[a], outs[a].at[kme], loc.at[a]) for a in range(n)]
        for cp in local:
            cp.start()
        cps = [_remote(ins[a], outs[a].at[kme], send.at[3 * a + r], recv.at[3 * a + r], (*chip, c))
               for a in range(n) for r, chip in enumerate(chips)]
        for cp in cps:
            cp.start()
        for a in range(n):
            for r, chip in enumerate(chips):
                kp = 2 * chip[0] + chip[1]
                _remote(ins[a], outs[a].at[kp], send.at[3 * a + r], recv.at[3 * a + r], (*chip, c)).wait_recv()
        for cp in cps:
            cp.wait_send()
        for cp in local:
            cp.wait()

    return pl.pallas_call(
        body, name="allgather_small", in_specs=[ANY] * n, out_specs=[ANY] * n,
        out_shape=[jax.ShapeDtypeStruct((4, *s.shape), s.dtype) for s in shards],
        scratch_shapes=[pltpu.SemaphoreType.DMA((3 * n,)), pltpu.SemaphoreType.DMA((3 * n,)),
                        pltpu.SemaphoreType.DMA((n,))])(*shards)


def _swap_halves(parts):
    n = len(parts)

    def body(*refs):
        ins, got = refs[:n], refs[n:2 * n]
        send, recv = refs[2 * n:]
        x, y, c, _ = _place()
        sib = (x, y, 1 - c)
        for a in range(n):
            for s, d in _pieces(ins[a].at[:, 1 - c], got[a], 1):
                _remote(s, d, send.at[a], recv.at[a], sib).start()
        for a in range(n):
            _remote(ins[a].at[:, 1 - c], got[a], send.at[a], recv.at[a], sib).wait()

    return pl.pallas_call(
        body, name="grad_swap_halves", in_specs=[ANY] * n, out_specs=[ANY] * n,
        out_shape=[jax.ShapeDtypeStruct((4, *p.shape[2:]), p.dtype) for p in parts],
        scratch_shapes=[pltpu.SemaphoreType.DMA((n,)), pltpu.SemaphoreType.DMA((n,))])(*parts)


def _join_halves(totals):
    n = len(totals)

    def body(*refs):
        outs = refs[n:2 * n]
        send, recv = refs[2 * n:]
        x, y, c, _ = _place()
        sib = (x, y, 1 - c)
        for a in range(n):
            for s, d in _pieces(outs[a].at[c], outs[a].at[c], 4):
                _remote(s, d, send.at[a], recv.at[a], sib).start()
        for a in range(n):
            _remote(outs[a].at[1 - c], outs[a].at[1 - c], send.at[a], recv.at[a], sib).wait_recv()
            _remote(outs[a].at[c], outs[a].at[c], send.at[a], recv.at[a], sib).wait_send()

    return pl.pallas_call(
        body, name="grad_join_halves", in_specs=[ANY] * n, out_specs=[ANY] * n,
        out_shape=[jax.ShapeDtypeStruct(t.shape, t.dtype) for t in totals],
        input_output_aliases={a: a for a in range(n)},
        scratch_shapes=[pltpu.SemaphoreType.DMA((n,)), pltpu.SemaphoreType.DMA((n,))])(*totals)


def _allreduce_small(packed):
    rows = packed.shape[0]
    half = rows // 2

    def body(in_ref, out_ref, q_ref, s_ref, t_ref, send, recv):
        x, y, c, chips = _place()
        sib = (x, y, 1 - c)
        mine = pl.ds(pl.multiple_of(c * half, 8), half)
        theirs = pl.ds(pl.multiple_of((1 - c) * half, 8), half)
        first = _remote(in_ref.at[theirs], q_ref, send.at[0], recv.at[0], sib)
        first.start()
        first.wait()
        s_ref[...] = in_ref[mine, :] + q_ref[...]
        cps = [_remote(s_ref, t_ref.at[r], send.at[1 + r], recv.at[1 + r], (*chip, c)) for r, chip in enumerate(chips)]
        for cp in cps:
            cp.start()
        for cp in cps:
            cp.wait()
        out_ref[mine, :] = (s_ref[...] + t_ref[2]) + (t_ref[0] + t_ref[1])
        last = _remote(out_ref.at[mine], out_ref.at[mine], send.at[4], recv.at[4], sib)
        last.start()
        _remote(out_ref.at[theirs], out_ref.at[theirs], send.at[4], recv.at[4], sib).wait_recv()
        last.wait_send()

    vm = pl.BlockSpec(memory_space=pltpu.VMEM)
    return pl.pallas_call(
        body, name="allreduce_small", in_specs=[vm], out_specs=vm, out_shape=jax.ShapeDtypeStruct(packed.shape, F32),
        scratch_shapes=[pltpu.VMEM((half, LANES), F32), pltpu.VMEM((half, LANES), F32),
                        pltpu.VMEM((3, half, LANES), F32), pltpu.SemaphoreType.DMA((5,)), pltpu.SemaphoreType.DMA((5,))],
        compiler_params=pltpu.CompilerParams(vmem_limit_bytes=VMEM_LIMIT))(packed)


PACK_ROWS = 16
PACK_BLOCK = 512


def _pack(arrs):
    parts, total = [], 0
    for a in arrs:
        flat = a.reshape(-1)
        rows = -(-flat.shape[0] // (LANES * PACK_ROWS)) * PACK_ROWS
        parts.append(jnp.pad(flat, (0, rows * LANES - flat.shape[0])).reshape(rows, LANES))
        total += rows
    tail = -total % PACK_BLOCK
    if tail:
        parts.append(jnp.zeros((tail, LANES), parts[0].dtype))
    return jnp.concatenate(parts, axis=0)


def _unpack(packed, shapes):
    out, row = [], 0
    for shp in shapes:
        size = math.prod(shp)
        rows = -(-size // (LANES * PACK_ROWS)) * PACK_ROWS
        out.append(packed[row:row + rows].reshape(-1)[:size].reshape(shp))
        row += rows
    return out


def kernel(x, norm_mix_g, norm_ffn_g, a_w_in, a_g_v, a_w_s, a_b_s, a_w_out, b_w_in, b_a_re, b_a_im, b_log_dt, b_b_re, b_b_im, b_c_re, b_c_im, b_d, b_w_glu, f_w_up, f_conv_w, f_conv_b, f_w_down, final_g, loss_target, m_norm_mix_g, m_norm_ffn_g, m_a_w_in, m_a_g_v, m_a_w_s, m_a_b_s, m_a_w_out, m_b_w_in, m_b_a_re, m_b_a_im, m_b_log_dt, m_b_b_re, m_b_b_im, m_b_c_re, m_b_c_im, m_b_d, m_b_w_glu, m_f_w_up, m_f_conv_w, m_f_conv_b, m_f_w_down, m_final_g, v_norm_mix_g, v_norm_ffn_g, v_a_w_in, v_a_g_v, v_a_w_s, v_a_b_s, v_a_w_out, v_b_w_in, v_b_a_re, v_b_a_im, v_b_log_dt, v_b_b_re, v_b_b_im, v_b_c_re, v_b_c_im, v_b_d, v_b_w_glu, v_f_w_up, v_f_conv_w, v_f_conv_b, v_f_w_down, v_final_g):
    w = dict(norm_mix_g=norm_mix_g, norm_ffn_g=norm_ffn_g, a_w_in=a_w_in, a_g_v=a_g_v, a_w_s=a_w_s, a_b_s=a_b_s,
             a_w_out=a_w_out, b_w_in=b_w_in, b_a_re=b_a_re, b_a_im=b_a_im, b_log_dt=b_log_dt, b_b_re=b_b_re,
             b_b_im=b_b_im, b_c_re=b_c_re, b_c_im=b_c_im, b_d=b_d, b_w_glu=b_w_glu, f_w_up=f_w_up, f_conv_w=f_conv_w,
             f_conv_b=f_conv_b, f_w_down=f_w_down, final_g=final_g)
    mom = dict(norm_mix_g=m_norm_mix_g, norm_ffn_g=m_norm_ffn_g, a_w_in=m_a_w_in, a_g_v=m_a_g_v, a_w_s=m_a_w_s,
               a_b_s=m_a_b_s, a_w_out=m_a_w_out, b_w_in=m_b_w_in, b_a_re=m_b_a_re, b_a_im=m_b_a_im,
               b_log_dt=m_b_log_dt, b_b_re=m_b_b_re, b_b_im=m_b_b_im, b_c_re=m_b_c_re, b_c_im=m_b_c_im, b_d=m_b_d,
               b_w_glu=m_b_w_glu, f_w_up=m_f_w_up, f_conv_w=m_f_conv_w, f_conv_b=m_f_conv_b, f_w_down=m_f_w_down,
               final_g=m_final_g)
    var = dict(norm_mix_g=v_norm_mix_g, norm_ffn_g=v_norm_ffn_g, a_w_in=v_a_w_in, a_g_v=v_a_g_v, a_w_s=v_a_w_s,
               a_b_s=v_a_b_s, a_w_out=v_a_w_out, b_w_in=v_b_w_in, b_a_re=v_b_a_re, b_a_im=v_b_a_im,
               b_log_dt=v_b_log_dt, b_b_re=v_b_b_re, b_b_im=v_b_b_im, b_c_re=v_b_c_re, b_c_im=v_b_c_im, b_d=v_b_d,
               b_w_glu=v_b_w_glu, f_w_up=v_f_w_up, f_conv_w=v_f_conv_w, f_conv_b=v_f_conv_b, f_w_down=v_f_w_down,
               final_g=v_final_g)

    rows, d = x.shape[1], x.shape[2]
    depth = norm_mix_g.shape[0]
    kchip = 2 * lax.axis_index("x") + lax.axis_index("y")
    big_names = list(BIG)
    dims = {n: _full_dims(BIG[n], w[n].shape) for n in big_names}

    keys = [(n, l) for n in big_names for l in range(w[n].shape[0])]
    slots = {(n, l): _cast_halves("cast_" + n, w[n], BIG[n], l) for n, l in keys}

    def group_keys(g):
        i = g // 2
        if g % 2 == 1:
            return [('f_w_up', i), ('f_w_down', i)]
        return [('a_w_in', i // 2), ('a_w_out', i // 2)] if i % 2 == 0 else [('b_w_in', i // 2), ('b_w_glu', i // 2)]

    bd_all, cw_all = _allgather_small([b_d, f_conv_w.reshape(-1, f_conv_w.shape[-1])])

    gathered, gather_waits, token = {}, [], bd_all
    for g in range(2 * depth):
        arrs = [slots[k] for k in group_keys(g)]
        send, recv, thru, token = _split_start(f"allgather_start_{g}", arrs, 3 * len(arrs), _gather_plan(len(arrs)),
                                               deps=(token,))
        gather_waits.append((send, recv, thru))
    gather_after = token

    def gather_group(g, after):
        send, recv, thru = gather_waits[g]
        landed = _split_wait(f"allgather_wait_{g}", send, recv, thru, _gather_plan(len(thru)), after)
        gathered.update(zip(group_keys(g), _forward_halves(landed)))

    bd_full = jnp.swapaxes(bd_all, 0, 1).reshape(b_d.shape[0], -1)
    cw_full = jnp.transpose(cw_all.reshape(4, *f_conv_w.shape), (1, 2, 0, 3)).reshape(depth, f_conv_w.shape[1], -1)

    pgrad = {}
    sgrad = {}

    def mm(name, a, wn, layer, out_dtype, residual=None, split=False):
        return _mm_x_w(name, a, gathered[wn, layer], BIG[wn], *dims[wn], out_dtype, residual, split)

    def mm_t(name, dy, wn, layer, out_dtype, split=False, deps=()):
        return _mm_dy_wt(name, dy, gathered[wn, layer], BIG[wn], *dims[wn], out_dtype, split, deps)

    def mm_g(name, xa, dy, wn, layer, split=False):
        pgrad[wn, layer] = _mm_xt_dy(name, xa, dy, BIG[wn], *dims[wn], split)

    e = d
    nb = e // LANES
    heads = e // SGU_GROUP
    h = x[0]
    saved = []
    for i in range(depth):
        j = i // 2
        gather_group(2 * i, gather_after if i == 0 else h)
        gm = norm_mix_g[i:i + 1]
        hn = _rms_fwd("rms_mix_fwd", h, gm)
        if i % 2 == 0:
            pre = mm("sgu_in", hn, 'a_w_in', j, BF16)
            bsx = jnp.broadcast_to(a_b_s[j][:, :, None], (heads, CHUNK, LANES))
            us = _sgu_mix_fwd("sgu_mix_fwd", pre, a_g_v[j:j + 1], a_w_s[j], bsx)
            h_mid = mm("sgu_out", us, 'a_w_out', j, F32, residual=h)
            mix = dict(h=h, hn=hn, pre=pre, us=us, bsx=bsx)
        else:
            groups = b_a_re.shape[1]
            rep = lambda t: jnp.repeat(t, SSM_GROUP, axis=1)
            lr, li = b_a_re[j], b_a_im[j]
            ldt = jnp.broadcast_to(b_log_dt[j][:, None], lr.shape)
            bflat = lambda t: t.reshape(groups, SSM_STATE * SSM_GROUP)
            disc_in = (lr, li, ldt, rep(lr), rep(li), rep(ldt), bflat(b_b_re[j]), bflat(b_b_im[j]))
            abr, abi, bbr, bbi = _s5_disc("s5_disc", *disc_in)
            shape_b = (groups, SSM_STATE, SSM_GROUP)
            bre = _to_blockdiag_b(bbr.reshape(shape_b), nb).astype(BF16)
            bim = _to_blockdiag_b(bbi.reshape(shape_b), nb).astype(BF16)
            crt = _to_blockdiag_ct(b_c_re[j], nb).astype(BF16)
            cit = _to_blockdiag_ct(b_c_im[j], nb).astype(BF16)
            ar_row, ai_row = abr.reshape(1, -1), abi.reshape(1, -1)
            dd = bd_full[j:j + 1]
            u = mm("s5_in", hn, 'b_w_in', j, F32)
            yv, gy = _s5_fwd("s5_fwd", u, bre, bim, crt, cit, ar_row, ai_row, dd)
            gg = mm("s5_glu", gy, 'b_w_glu', j, BF16)
            h_mid = _glu_fwd("glu_fwd", gg, h)
            mix = dict(h=h, hn=hn, u=u, y=yv, gy=gy, gg=gg, disc_in=disc_in, mats=(bre, bim, crt, cit, ar_row, ai_row, dd))
        gather_group(2 * i + 1, h_mid)
        gf = norm_ffn_g[i:i + 1]
        hn2 = _rms_fwd("rms_ffn_fwd", h_mid, gf)
        z = mm("ffn_up", hn2, 'f_w_up', i, BF16, split=True)
        cw = jnp.swapaxes(cw_full[i].reshape(cw_full.shape[1], 2, -1), 0, 1)
        cb = f_conv_b[i].reshape(2, 1, -1)
        act = _ffn_act_fwd("ffn_act_fwd", z, cw, cb)
        h_out = mm("ffn_down", act, 'f_w_down', i, F32, residual=h_mid)
        saved.append((mix, dict(h=h_mid, hn=hn2, z=z, act=act, cw=cw, cb=cb)))
        h = h_out

    dh, g_final, loss_vec = _loss_head(h, final_g.reshape(1, d), loss_target[0])
    loss = lax.psum(jnp.sum(loss_vec), ("x", "y", "c"))
    sgrad['final_g'] = g_final.reshape(d)

    g_mix, g_ffn = [None] * depth, [None] * depth
    g_cw, g_cb = [None] * depth, [None] * depth
    sg = {k: [None] * (depth // 2) for k in ('a_g_v', 'a_w_s', 'a_b_s')}
    bg = {k: [None] * (depth // 2) for k in ('b_a_re', 'b_a_im', 'b_log_dt', 'b_b_re', 'b_b_im', 'b_c_re', 'b_c_im', 'b_d')}
    scatters, flight = {}, dict(pending=None, token=())

    def scatter_group(g, done):
        if flight['pending'] is not None:
            prev, send, recv, thru = flight['pending']
            scatters[prev] = _split_wait(f"grad_scatter_wait_{prev}", send, recv, thru, _scatter_plan(len(thru) // 2), done)
        parts = [pgrad[k] for k in group_keys(g)]
        sums = [_add2("grad_chip_sum", p, q) for p, q in zip(parts, _swap_halves(parts))]
        land = [lax.empty((3, *s.shape[1:]), BF16) for s in sums]
        send, recv, thru, token = _split_start(f"grad_scatter_start_{g}", sums + land, 3 * len(sums),
                                               _scatter_plan(len(sums)))
        flight['pending'], flight['token'] = (g, send, recv, thru), (token,)

    for i in reversed(range(depth)):
        j = i // 2
        mix, ffn = saved[i]
        d_act = mm_t("ffn_down_dx", dh, 'f_w_down', i, BF16, deps=flight['token'])
        mm_g("ffn_down_dw", ffn['act'], dh, 'f_w_down', i)
        dz, dcw, dcb = _ffn_act_bwd("ffn_act_bwd", ffn['z'], d_act, ffn['cw'], ffn['cb'])
        g_cw[i], g_cb[i] = jnp.swapaxes(dcw, 0, 1).reshape(dcw.shape[1], -1), dcb.reshape(1, -1)
        mm_g("ffn_up_dw", ffn['hn'], dz, 'f_w_up', i, split=True)
        dhn = mm_t("ffn_up_dx", dz, 'f_w_up', i, F32, split=True)
        dh, g_ffn[i] = _rms_bwd("rms_ffn_bwd", ffn['h'], norm_ffn_g[i:i + 1], dhn, dh)
        scatter_group(2 * i + 1, dh)
        if i % 2 == 0:
            dus = mm_t("sgu_out_dx", dh, 'a_w_out', j, BF16, deps=flight['token'])
            mm_g("sgu_out_dw", mix['us'], dh, 'a_w_out', j)
            dpre, dws, dbs, dgv = _sgu_mix_bwd("sgu_mix_bwd", mix['pre'], dus, a_g_v[j:j + 1], a_w_s[j], mix['bsx'])
            sg['a_w_s'][j], sg['a_b_s'][j], sg['a_g_v'][j] = dws, dbs[:, :, 0], dgv[0]
            mm_g("sgu_in_dw", mix['hn'], dpre, 'a_w_in', j)
            dhn = mm_t("sgu_in_dx", dpre, 'a_w_in', j, F32)
        else:
            dgg = _glu_bwd("glu_bwd", mix['gg'], dh)
            mm_g("s5_glu_dw", mix['gy'], dgg, 'b_w_glu', j)
            dgy = mm_t("s5_glu_dx", dgg, 'b_w_glu', j, BF16, deps=flight['token'])
            du, dbr, dbi, dcr, dci, dar, dai, ddd = _s5_bwd("s5_bwd", mix['u'], mix['y'], dgy, *mix['mats'])
            groups = b_a_re.shape[1]
            flat = lambda t: _from_blockdiag_b(t, nb).reshape(groups, SSM_STATE * SSM_GROUP)
            sel = jnp.repeat(jnp.eye(SSM_STATE, dtype=F32), SSM_GROUP, axis=0)
            dlr, dli, dldt, dbre, dbim = _s5_disc_bwd(
                "s5_disc_bwd", *mix['disc_in'], dar.reshape(groups, SSM_STATE), dai.reshape(groups, SSM_STATE),
                flat(dbr), flat(dbi), sel)
            bg['b_a_re'][j], bg['b_a_im'][j], bg['b_log_dt'][j] = dlr, dli, dldt[:, 0]
            bg['b_b_re'][j] = dbre.reshape(groups, SSM_STATE, SSM_GROUP)
            bg['b_b_im'][j] = dbim.reshape(groups, SSM_STATE, SSM_GROUP)
            bg['b_c_re'][j], bg['b_c_im'][j] = _from_blockdiag_ct(dcr, nb), _from_blockdiag_ct(dci, nb)
            bg['b_d'][j] = ddd[0]
            mm_g("s5_in_dw", mix['hn'], du, 'b_w_in', j)
            dhn = mm_t("s5_in_dx", du, 'b_w_in', j, F32)
        dh, g_mix[i] = _rms_bwd("rms_mix_bwd", mix['h'], norm_mix_g[i:i + 1], dhn, dh)
        scatter_group(2 * i, dh)
    _, send, recv, thru = flight['pending']
    scatters[0] = _split_wait("grad_scatter_wait_0", send, recv, thru, _scatter_plan(len(thru) // 2), flight['token'][0])
    grad_x = dh[None]

    sgrad['norm_mix_g'] = jnp.concatenate(g_mix, axis=0)
    sgrad['norm_ffn_g'] = jnp.concatenate(g_ffn, axis=0)
    sgrad['f_conv_w'] = jnp.stack(g_cw)
    sgrad['f_conv_b'] = jnp.concatenate(g_cb, axis=0)
    for k, v_ in list(sg.items()) + list(bg.items()):
        sgrad[k] = jnp.stack(v_)

    total = _allreduce_small(_pack([sgrad[n] for n in SMALL]))
    full_shapes = [sgrad[n].shape for n in SMALL]
    gsmall = dict(zip(SMALL, _unpack(total, full_shapes)))
    for n, axis in CHIP_SHARDED_SMALL.items():
        width = w[n].shape[axis]
        gsmall[n] = lax.dynamic_slice_in_dim(gsmall[n], kchip * width, width, axis=axis)
    pk = lambda t: _pack([t[n] for n in SMALL])
    gpacked = pk(gsmall)
    dpk, mpk, vpk = _adam_small(pk(w), pk(mom), pk(var), gpacked)
    shard_shapes = [w[n].shape for n in SMALL]
    out_g = dict(gsmall)
    out_d = dict(zip(SMALL, _unpack(dpk, shard_shapes)))
    out_m = dict(zip(SMALL, _unpack(mpk, shard_shapes)))
    out_v = dict(zip(SMALL, _unpack(vpk, shard_shapes)))

    totals = {}
    for g in range(2 * depth):
        n = len(scatters[g]) // 2
        for k, s, r in zip(group_keys(g), scatters[g][:n], scatters[g][n:]):
            totals[k] = _add4("grad_total", s, r)
    gfull = _join_halves([totals[k] for k in keys])
    stacked = {n: [lax.empty(w[n].shape, F32) for _ in range(4)] for n in big_names}
    for (n, l), gf_ in zip(keys, gfull):
        stacked[n] = _adam_big("adam_" + n, w[n], mom[n], var[n], gf_, BIG[n], l, stacked[n])
    for n in big_names:
        out_g[n], out_d[n], out_m[n], out_v[n] = stacked[n]

    return (loss, grad_x, *[out_g[n] for n in W_NAMES], *[out_d[n] for n in W_NAMES],
            *[out_m[n] for n in W_NAMES], *[out_v[n] for n in W_NAMES])
```

```python
import functools
import itertools
import math

import jax
import jax.numpy as jnp
from jax import lax
from jax.experimental import pallas as pl
from jax.experimental.pallas import tpu as pltpu

F32, BF16 = jnp.float32, jnp.bfloat16
MESH = pl.DeviceIdType.MESH

CHUNK = 128
SEG = CHUNK + 4
SGU_GROUP = 128
SSM_GROUP = 16
SSM_STATE = 64
EPS = 1e-6
LANES = 128
GROUPS_PER_BLOCK = LANES // SSM_GROUP
STATE_BLOCKS = SSM_STATE // SSM_GROUP
VMEM_LIMIT = 52 * 1024 * 1024

ADAM_LR, ADAM_B1, ADAM_B2, ADAM_EPS, ADAM_WD, ADAM_STEP = 0.001, 0.9, 0.999, 1e-08, 0.01, 10

W_NAMES = ['norm_mix_g', 'norm_ffn_g', 'a_w_in', 'a_g_v', 'a_w_s', 'a_b_s', 'a_w_out', 'b_w_in', 'b_a_re', 'b_a_im',
           'b_log_dt', 'b_b_re', 'b_b_im', 'b_c_re', 'b_c_im', 'b_d', 'b_w_glu', 'f_w_up', 'f_conv_w', 'f_conv_b',
           'f_w_down', 'final_g']
BIG = {'a_w_in': 'col', 'a_w_out': 'row', 'b_w_in': 'row', 'b_w_glu': 'col', 'f_w_up': 'col', 'f_w_down': 'row'}
SMALL = [n for n in W_NAMES if n not in BIG]
CHIP_SHARDED_SMALL = {'b_d': 1, 'f_conv_w': 2}


def _tile(n, pref, align):
    t = min(n, pref)
    t -= t % align
    while t >= align:
        if n % t == 0:
            return t
        t -= align
    return n


def _params(*sem):
    return pltpu.CompilerParams(dimension_semantics=sem, vmem_limit_bytes=VMEM_LIMIT)


def _gelu(x):
    c = math.sqrt(2.0 / math.pi)
    return 0.5 * x * (1.0 + jnp.tanh(c * (x + 0.044715 * x * x * x)))


def _gelu_grad(x):
    c = math.sqrt(2.0 / math.pi)
    t = jnp.tanh(c * (x + 0.044715 * x * x * x))
    return 0.5 * (1.0 + t) + 0.5 * x * (1.0 - t * t) * c * (1.0 + 3.0 * 0.044715 * x * x)


def _half_shape(kind, shard_shape):
    _, r, c = shard_shape
    return (r // 2, c) if kind == 'col' else (r, c // 2)


def _full_dims(kind, shard_shape):
    _, r, c = shard_shape
    return (r, 4 * c) if kind == 'col' else (4 * r, c)


def _gspec(kind, kdim, ndim, tr, tc, rc):
    if kind == 'col':
        nr, nc = (kdim // 2) // tr, (ndim // 4) // tc

        def imap(*g):
            rb, cb = rc(*g)
            return (cb // nc, rb // nr, rb % nr, cb % nc)
    else:
        nr, nc = (kdim // 4) // tr, (ndim // 2) // tc

        def imap(*g):
            rb, cb = rc(*g)
            return (rb // nr, cb // nc, rb % nr, cb % nc)
    return pl.BlockSpec((None, None, tr, tc), imap)


def _act_spec(rows_blk, cols_blk, ncol_half, at):
    if ncol_half is None:
        return pl.BlockSpec((rows_blk, cols_blk), at)

    def imap(*g):
        rb, cb = at(*g)
        return (cb // ncol_half, rb, cb % ncol_half)
    return pl.BlockSpec((None, rows_blk, cols_blk), imap)


def _wtiles(kind, kdim, ndim):
    if kind == 'col':
        return _tile(kdim // 2, 1024, LANES), _tile(ndim // 4, 1408, LANES)
    return _tile(kdim // 4, 1408, LANES), _tile(ndim // 2, 1024, LANES)


_DIMS = {'nn': (((1,), (0,)), ((), ())), 'nt': (((1,), (1,)), ((), ())), 'tn': (((0,), (0,)), ((), ()))}


def _matmul(name, mode, a, b, grid, a_spec, b_spec, out_shape, out_spec, acc_shape, extras=(), extra_specs=(),
            epilogue=None, aliases=None):
    nk = grid[2]
    dims = _DIMS[mode]
    n_extra = len(extras)
    b_specs = b_spec if isinstance(b_spec, (list, tuple)) else [b_spec]
    nb = len(b_specs)

    def body(a_ref, *rest):
        b_refs, rest = rest[:nb], rest[nb:]
        extra_refs, o_ref = rest[:n_extra], rest[n_extra]
        width = a_ref.shape[1] // nb
        prod = None
        for p, b_ref in enumerate(b_refs):
            a_blk = a_ref[...] if nb == 1 else a_ref[:, p * width:(p + 1) * width]
            term = lax.dot_general(a_blk.astype(BF16), b_ref[...].astype(BF16), dims, preferred_element_type=F32)
            prod = term if prod is None else prod + term

        def finish(r):
            if epilogue is not None:
                r = epilogue(r, *[e[...] for e in extra_refs])
            o_ref[...] = r.astype(o_ref.dtype)

        if nk == 1:
            finish(prod)
            return
        acc_ref = rest[n_extra + 1]
        kk = pl.program_id(2)

        @pl.when(kk == 0)
        def _():
            acc_ref[...] = prod

        @pl.when(kk > 0)
        def _():
            acc_ref[...] += prod

        @pl.when(kk == nk - 1)
        def _():
            finish(acc_ref[...])

    scratch = [pltpu.VMEM(acc_shape, F32)] if nk > 1 else []
    return pl.pallas_call(
        body, name=name, grid=grid, in_specs=[a_spec, *b_specs, *extra_specs], out_specs=out_spec, out_shape=out_shape,
        scratch_shapes=scratch, input_output_aliases=aliases or {},
        compiler_params=_params("parallel", "parallel", "arbitrary"))(a, *([b] * nb), *extras)


PART_BYTES = 17 * 1024 * 1024


def _shard_parts(tr, tc):
    for parts in (4, 2):
        if parts * tr * tc * 2 * 2 <= PART_BYTES:
            return parts
    return 1


def _mm_x_w(name, a, wg, kind, kdim, ndim, out_dtype, residual=None, split=False):
    rows = a.shape[0]
    tk, tn = _wtiles(kind, kdim, ndim)
    tm = _tile(rows, 1024, 16)
    if kind == 'col':
        tk = kdim
        b_spec = [_gspec(kind, kdim, ndim, kdim // 2, tn, lambda i, j, k, p=p: (p, j)) for p in range(2)]
    else:
        parts = _shard_parts(tk, tn) if tk == kdim // 4 else 1
        b_spec = [_gspec(kind, kdim, ndim, tk, tn, lambda i, j, k, p=p: (k * parts + p, j)) for p in range(parts)]
        tk = parts * tk
    grid = (rows // tm, ndim // tn, kdim // tk)
    extras, especs, epi = (), (), None
    if residual is not None:
        extras, especs = (residual,), (pl.BlockSpec((tm, tn), lambda i, j, k: (i, j)),)
        epi = lambda r, res: r + res
    out_shape = (2, rows, ndim // 2) if split else (rows, ndim)
    return _matmul(name, 'nn', a, wg, grid, pl.BlockSpec((tm, tk), lambda i, j, k: (i, k)), b_spec,
                   jax.ShapeDtypeStruct(out_shape, out_dtype),
                   _act_spec(tm, tn, (ndim // 2) // tn if split else None, lambda i, j, k: (i, j)),
                   (tm, tn), extras, especs, epi)


def _mm_dy_wt(name, dy, wg, kind, kdim, ndim, out_dtype, split=False, deps=()):
    rows = dy.shape[-2]
    tn, tk = _wtiles(kind, kdim, ndim)
    tm = _tile(rows, 1024, 16)
    if kind == 'col':
        shard = ndim // 4
        parts = 1 if split else _shard_parts(tn, shard)
        b_spec = [_gspec(kind, kdim, ndim, tn, shard, lambda i, j, k, p=p: (j, k * parts + p)) for p in range(parts)]
        tk = parts * shard
    else:
        assert not split
        tk = ndim
        b_spec = [_gspec(kind, kdim, ndim, tn, ndim // 2, lambda i, j, k, p=p: (j, p)) for p in range(2)]
    grid = (rows // tm, kdim // tn, ndim // tk)
    return _matmul(name, 'nt', dy, wg, grid,
                   _act_spec(tm, tk, (ndim // 2) // tk if split else None, lambda i, j, k: (i, k)), b_spec,
                   jax.ShapeDtypeStruct((rows, kdim), out_dtype), pl.BlockSpec((tm, tn), lambda i, j, k: (i, j)),
                   (tm, tn), extras=tuple(deps), extra_specs=(ANY,) * len(deps))


def _mm_xt_dy(name, xa, dy, kind, kdim, ndim, split=False):
    rows = xa.shape[0]
    tm, tn = _wtiles(kind, kdim, ndim)
    tl = _tile(rows, 2048, 16)
    grid = (kdim // tm, ndim // tn, rows // tl)
    rdim, cdim = (kdim // 2, ndim // 4) if kind == 'col' else (kdim // 4, ndim // 2)
    return _matmul(name, 'tn', xa, dy, grid, pl.BlockSpec((tl, tm), lambda i, j, k: (k, i)),
                   _act_spec(tl, tn, (ndim // 2) // tn if split else None, lambda i, j, k: (k, j)),
                   jax.ShapeDtypeStruct((4, 2, rdim, cdim), BF16),
                   _gspec(kind, kdim, ndim, tm, tn, lambda i, j, k: (i, j)), (tm, tn))


def _rms_fwd(name, h, g):
    rows, d = h.shape
    tm = _tile(rows, 256, 16)

    def body(h_ref, g_ref, o_ref):
        x = h_ref[...]
        r = lax.rsqrt(jnp.mean(x * x, axis=-1, keepdims=True) + EPS)
        o_ref[...] = (x * r * g_ref[...]).astype(o_ref.dtype)

    return pl.pallas_call(
        body, name=name, grid=(rows // tm,),
        in_specs=[pl.BlockSpec((tm, d), lambda i: (i, 0)), pl.BlockSpec((1, d), lambda i: (0, 0))],
        out_specs=pl.BlockSpec((tm, d), lambda i: (i, 0)), out_shape=jax.ShapeDtypeStruct((rows, d), BF16),
        compiler_params=_params("parallel"))(h, g)


def _rms_bwd(name, h, g, dhn, dres):
    rows, d = h.shape
    tm = _tile(rows, 256, 16)

    def body(h_ref, g_ref, dy_ref, dres_ref, dh_ref, dg_ref):
        x = h_ref[...]
        r = lax.rsqrt(jnp.mean(x * x, axis=-1, keepdims=True) + EPS)
        xh = x * r
        dy = dy_ref[...].astype(F32)
        gy = dy * g_ref[...]
        dh_ref[...] = dres_ref[...] + r * (gy - xh * jnp.mean(gy * xh, axis=-1, keepdims=True))
        part = jnp.sum(dy * xh, axis=0, keepdims=True)

        @pl.when(pl.program_id(0) == 0)
        def _():
            dg_ref[...] = part

        @pl.when(pl.program_id(0) > 0)
        def _():
            dg_ref[...] += part

    row = pl.BlockSpec((tm, d), lambda i: (i, 0))
    vec = pl.BlockSpec((1, d), lambda i: (0, 0))
    return pl.pallas_call(
        body, name=name, grid=(rows // tm,), in_specs=[row, vec, row, row], out_specs=[row, vec],
        out_shape=[jax.ShapeDtypeStruct((rows, d), F32), jax.ShapeDtypeStruct((1, d), F32)],
        compiler_params=_params("arbitrary"))(h, g, dhn, dres)


def _loss_head(h, g, target):
    rows, d = h.shape
    tm = _tile(rows, 256, 16)

    def body(h_ref, g_ref, t_ref, dh_ref, dg_ref, loss_ref):
        x = h_ref[...]
        r = lax.rsqrt(jnp.mean(x * x, axis=-1, keepdims=True) + EPS)
        xh = x * r
        err = xh * g_ref[...] - t_ref[...]
        dy = err * (1.0 / d)
        gy = dy * g_ref[...]
        dh_ref[...] = r * (gy - xh * jnp.mean(gy * xh, axis=-1, keepdims=True))
        part = jnp.sum(dy * xh, axis=0, keepdims=True)
        sq = jnp.sum(err * err, axis=0, keepdims=True) * (0.5 / d)

        @pl.when(pl.program_id(0) == 0)
        def _():
            dg_ref[...] = part
            loss_ref[...] = sq

        @pl.when(pl.program_id(0) > 0)
        def _():
            dg_ref[...] += part
            loss_ref[...] += sq

    row = pl.BlockSpec((tm, d), lambda i: (i, 0))
    vec = pl.BlockSpec((1, d), lambda i: (0, 0))
    return pl.pallas_call(
        body, name="loss_head", grid=(rows // tm,), in_specs=[row, vec, row], out_specs=[row, vec, vec],
        out_shape=[jax.ShapeDtypeStruct((rows, d), F32), jax.ShapeDtypeStruct((1, d), F32),
                   jax.ShapeDtypeStruct((1, d), F32)],
        compiler_params=_params("arbitrary"))(h, g, target)


def _shift_down(cur, prev8, first, k):
    rows = cur.shape[0]
    rolled = pltpu.roll(cur, k, axis=0)
    idx = lax.broadcasted_iota(jnp.int32, cur.shape, 0)
    prev8 = jnp.where(first, 0.0, prev8)
    out = rolled
    for r in range(k):
        out = jnp.where(idx == r, prev8[8 - k + r:8 - k + r + 1, :], out)
    del rows
    return out


def _shift_up(cur, next8, last, k):
    rows = cur.shape[0]
    rolled = pltpu.roll(cur, rows - k, axis=0)
    idx = lax.broadcasted_iota(jnp.int32, cur.shape, 0)
    next8 = jnp.where(last, 0.0, next8)
    out = rolled
    for r in range(k):
        out = jnp.where(idx == rows - k + r, next8[r:r + 1, :], out)
    return out


def _conv_acc(z, zprev, first, w, b):
    z1 = _shift_down(z, zprev, first, 1)
    z2 = _shift_down(z, zprev, first, 2)
    return b + w[2:3, :] * z + w[1:2, :] * z1 + w[0:1, :] * z2, z1, z2


def _ffn_tiles(rows, f):
    return _tile(rows, 512, 16), _tile(f, 512, LANES)


def _ffn_act_fwd(name, z3, cw3, cb3):
    _, rows, f = z3.shape
    tm, tc = _ffn_tiles(rows, f)
    hb = tm // 8

    def body(z_ref, zp_ref, w_ref, b_ref, o_ref):
        first = pl.program_id(0) == 0
        gate, _, _ = _conv_acc(z_ref[0].astype(F32), zp_ref[0].astype(F32), first, w_ref[0], b_ref[0])
        val, _, _ = _conv_acc(z_ref[1].astype(F32), zp_ref[1].astype(F32), first, w_ref[1], b_ref[1])
        o_ref[...] = (gate * jax.nn.sigmoid(gate) * val).astype(o_ref.dtype)

    return pl.pallas_call(
        body, name=name, grid=(rows // tm, f // tc),
        in_specs=[pl.BlockSpec((2, tm, tc), lambda i, j: (0, i, j)),
                  pl.BlockSpec((2, 8, tc), lambda i, j: (0, jnp.maximum(i * hb - 1, 0), j)),
                  pl.BlockSpec((2, 3, tc), lambda i, j: (0, 0, j)), pl.BlockSpec((2, 1, tc), lambda i, j: (0, 0, j))],
        out_specs=pl.BlockSpec((tm, tc), lambda i, j: (i, j)), out_shape=jax.ShapeDtypeStruct((rows, f), BF16),
        compiler_params=_params("parallel", "parallel"))(z3, z3, cw3, cb3)


def _gate_grads(d_a, acc_g, acc_v):
    sig = jax.nn.sigmoid(acc_g)
    return d_a * acc_v * sig * (1.0 + acc_g * (1.0 - sig)), d_a * acc_g * sig


def _ffn_act_bwd(name, z3, da, cw3, cb3):
    _, rows, f = z3.shape
    tm, tc = _ffn_tiles(rows, f)
    hb = tm // 8
    nrow = rows // tm

    def body(z_ref, zp_ref, zn_ref, da_ref, dan_ref, w_ref, b_ref, dz_ref, dcw_ref, dcb_ref):
        i = pl.program_id(1)
        first, last = i == 0, i == nrow - 1
        w, b = (w_ref[0], w_ref[1]), (b_ref[0], b_ref[1])
        z = (z_ref[0].astype(F32), z_ref[1].astype(F32))
        acc, taps = [], []
        for hf in range(2):
            a_h, z1, z2 = _conv_acc(z[hf], zp_ref[hf].astype(F32), first, w[hf], b[hf])
            acc.append(a_h)
            taps.append((z2, z1, z[hf]))
        dacc = _gate_grads(da_ref[...].astype(F32), acc[0], acc[1])
        acc_n = [_conv_acc(zn_ref[hf].astype(F32), z[hf][tm - 8:tm, :], False, w[hf], b[hf])[0] for hf in range(2)]
        dacc_n = _gate_grads(dan_ref[...].astype(F32), acc_n[0], acc_n[1])
        for hf in range(2):
            d = dacc[hf]
            d1 = _shift_up(d, dacc_n[hf], last, 1)
            d2 = _shift_up(d, dacc_n[hf], last, 2)
            dz_ref[hf] = (w[hf][2:3, :] * d + w[hf][1:2, :] * d1 + w[hf][0:1, :] * d2).astype(dz_ref.dtype)
        sums_w = [[jnp.sum(dacc[hf] * t, axis=0, keepdims=True) for t in taps[hf]] for hf in range(2)]
        sums_b = [jnp.sum(dacc[hf], axis=0, keepdims=True) for hf in range(2)]

        @pl.when(first)
        def _():
            for hf in range(2):
                for k in range(3):
                    dcw_ref[hf, k:k + 1, :] = sums_w[hf][k]
                dcb_ref[hf] = sums_b[hf]

        @pl.when(i > 0)
        def _():
            for hf in range(2):
                for k in range(3):
                    dcw_ref[hf, k:k + 1, :] += sums_w[hf][k]
                dcb_ref[hf] += sums_b[hf]

    nxt = lambda i: jnp.minimum((i + 1) * hb, rows // 8 - 1)
    wsp = pl.BlockSpec((2, 3, tc), lambda j, i: (0, 0, j))
    bsp = pl.BlockSpec((2, 1, tc), lambda j, i: (0, 0, j))
    cur = pl.BlockSpec((2, tm, tc), lambda j, i: (0, i, j))
    return pl.pallas_call(
        body, name=name, grid=(f // tc, nrow),
        in_specs=[cur, pl.BlockSpec((2, 8, tc), lambda j, i: (0, jnp.maximum(i * hb - 1, 0), j)),
                  pl.BlockSpec((2, 8, tc), lambda j, i: (0, nxt(i), j)), pl.BlockSpec((tm, tc), lambda j, i: (i, j)),
                  pl.BlockSpec((8, tc), lambda j, i: (nxt(i), j)), wsp, bsp],
        out_specs=[cur, wsp, bsp],
        out_shape=[jax.ShapeDtypeStruct((2, rows, f), BF16), jax.ShapeDtypeStruct((2, 3, f), F32),
                   jax.ShapeDtypeStruct((2, 1, f), F32)],
        compiler_params=_params("parallel", "arbitrary"))(z3, z3, z3, da, da, cw3, cb3)


def _glu_fwd(name, gg, h):
    rows, d2 = gg.shape
    d = d2 // 2
    tm, tc = _tile(rows, 512, 16), _tile(d, 1024, LANES)
    nd = d // tc

    def body(a_ref, b_ref, h_ref, o_ref):
        o_ref[...] = h_ref[...] + a_ref[...].astype(F32) * jax.nn.sigmoid(b_ref[...].astype(F32))

    return pl.pallas_call(
        body, name=name, grid=(rows // tm, nd),
        in_specs=[pl.BlockSpec((tm, tc), lambda i, j: (i, j)), pl.BlockSpec((tm, tc), lambda i, j: (i, j + nd)),
                  pl.BlockSpec((tm, tc), lambda i, j: (i, j))],
        out_specs=pl.BlockSpec((tm, tc), lambda i, j: (i, j)), out_shape=jax.ShapeDtypeStruct((rows, d), F32),
        compiler_params=_params("parallel", "parallel"))(gg, gg, h)


def _glu_bwd(name, gg, dh):
    rows, d2 = gg.shape
    d = d2 // 2
    tm, tc = _tile(rows, 512, 16), _tile(d, 1024, LANES)
    nd = d // tc

    def body(s_ref, o_ref, dh_ref, out_ref):
        is_a = pl.program_id(1) < nd
        me = s_ref[...].astype(F32)
        other = o_ref[...].astype(F32)
        g = dh_ref[...]
        sig_o = jax.nn.sigmoid(other)
        sig_m = jax.nn.sigmoid(me)
        out_ref[...] = jnp.where(is_a, g * sig_o, g * other * sig_m * (1.0 - sig_m)).astype(out_ref.dtype)

    return pl.pallas_call(
        body, name=name, grid=(rows // tm, 2 * nd),
        in_specs=[pl.BlockSpec((tm, tc), lambda i, j: (i, j)),
                  pl.BlockSpec((tm, tc), lambda i, j: (i, (j + nd) % (2 * nd))),
                  pl.BlockSpec((tm, tc), lambda i, j: (i, j % nd))],
        out_specs=pl.BlockSpec((tm, tc), lambda i, j: (i, j)), out_shape=jax.ShapeDtypeStruct((rows, d2), BF16),
        compiler_params=_params("parallel", "parallel"))(gg, gg, dh)


def _sgu_common(pre_ref, gv_ref, e):
    u = _gelu(pre_ref[:, :e].astype(F32))
    v = _gelu(pre_ref[:, e:].astype(F32))
    r = lax.rsqrt(jnp.mean(v * v, axis=-1, keepdims=True) + EPS)
    vh = v * r
    return u, vh, r, (vh * gv_ref[...]).astype(BF16)


def _tril_bf16(ws_ref, hd):
    t = lax.broadcasted_iota(jnp.int32, (CHUNK, CHUNK), 0)
    s = lax.broadcasted_iota(jnp.int32, (CHUNK, CHUNK), 1)
    return jnp.where(s <= t, ws_ref[hd], 0.0).astype(BF16)


def _sgu_mix_fwd(name, pre, gv, ws, bsx):
    rows, e2 = pre.shape
    e = e2 // 2
    heads = e // SGU_GROUP
    tr = _tile(rows, 256, CHUNK)

    def body(pre_ref, gv_ref, ws_ref, bs_ref, o_ref):
        u, _, _, vn = _sgu_common(pre_ref, gv_ref, e)
        for hd in range(heads):
            wm = _tril_bf16(ws_ref, hd)
            cols = slice(hd * SGU_GROUP, (hd + 1) * SGU_GROUP)
            for ck in range(tr // CHUNK):
                rws = slice(ck * CHUNK, (ck + 1) * CHUNK)
                s = jnp.dot(wm, vn[rws, cols], preferred_element_type=F32) + bs_ref[hd]
                o_ref[rws, cols] = (u[rws, cols] * s).astype(o_ref.dtype)

    whole3 = pl.BlockSpec((heads, CHUNK, CHUNK), lambda i: (0, 0, 0))
    return pl.pallas_call(
        body, name=name, grid=(rows // tr,),
        in_specs=[pl.BlockSpec((tr, e2), lambda i: (i, 0)), pl.BlockSpec((1, e), lambda i: (0, 0)), whole3, whole3],
        out_specs=pl.BlockSpec((tr, e), lambda i: (i, 0)), out_shape=jax.ShapeDtypeStruct((rows, e), BF16),
        compiler_params=_params("parallel"))(pre, gv, ws, bsx)


def _sgu_mix_bwd(name, pre, dus, gv, ws, bsx):
    rows, e2 = pre.shape
    e = e2 // 2
    heads = e // SGU_GROUP
    tr = _tile(rows, 256, CHUNK)

    def body(pre_ref, dus_ref, gv_ref, ws_ref, bs_ref, dpre_ref, dws_ref, dbs_ref, dgv_ref, dvn_ref, du_ref):
        first = pl.program_id(0) == 0
        u, vh, r, vn = _sgu_common(pre_ref, gv_ref, e)
        ones = jnp.ones((SGU_GROUP, LANES), BF16)
        tt = lax.broadcasted_iota(jnp.int32, (CHUNK, CHUNK), 0)
        ss = lax.broadcasted_iota(jnp.int32, (CHUNK, CHUNK), 1)
        for hd in range(heads):
            wm = _tril_bf16(ws_ref, hd)
            cols = slice(hd * SGU_GROUP, (hd + 1) * SGU_GROUP)
            dw = jnp.zeros((CHUNK, CHUNK), F32)
            db = jnp.zeros((CHUNK, LANES), F32)
            for ck in range(tr // CHUNK):
                rws = slice(ck * CHUNK, (ck + 1) * CHUNK)
                vblk = vn[rws, cols]
                s = jnp.dot(wm, vblk, preferred_element_type=F32) + bs_ref[hd]
                d_us = dus_ref[rws, cols].astype(F32)
                du_ref[rws, cols] = d_us * s
                ds = (d_us * u[rws, cols]).astype(BF16)
                dvn_ref[rws, cols] = lax.dot_general(wm, ds, _DIMS['tn'], preferred_element_type=F32)
                dw = dw + lax.dot_general(ds, vblk, _DIMS['nt'], preferred_element_type=F32)
                db = db + jnp.dot(ds, ones, preferred_element_type=F32)
            dw = jnp.where(ss <= tt, dw, 0.0)

            @pl.when(first)
            def _():
                dws_ref[hd] = dw
                dbs_ref[hd] = db

            @pl.when(jnp.logical_not(first))
            def _():
                dws_ref[hd] += dw
                dbs_ref[hd] += db

        dvn = dvn_ref[...]
        part = jnp.sum(dvn * vh, axis=0, keepdims=True)

        @pl.when(first)
        def _():
            dgv_ref[...] = part

        @pl.when(jnp.logical_not(first))
        def _():
            dgv_ref[...] += part

        gy = dvn * gv_ref[...]
        dv = r * (gy - vh * jnp.mean(gy * vh, axis=-1, keepdims=True))
        dpre_ref[:, :e] = (du_ref[...] * _gelu_grad(pre_ref[:, :e].astype(F32))).astype(dpre_ref.dtype)
        dpre_ref[:, e:] = (dv * _gelu_grad(pre_ref[:, e:].astype(F32))).astype(dpre_ref.dtype)

    whole3 = pl.BlockSpec((heads, CHUNK, CHUNK), lambda i: (0, 0, 0))
    vec = pl.BlockSpec((1, e), lambda i: (0, 0))
    return pl.pallas_call(
        body, name=name, grid=(rows // tr,),
        in_specs=[pl.BlockSpec((tr, e2), lambda i: (i, 0)), pl.BlockSpec((tr, e), lambda i: (i, 0)), vec, whole3, whole3],
        out_specs=[pl.BlockSpec((tr, e2), lambda i: (i, 0)), whole3, whole3, vec],
        out_shape=[jax.ShapeDtypeStruct((rows, e2), BF16), jax.ShapeDtypeStruct((heads, CHUNK, CHUNK), F32),
                   jax.ShapeDtypeStruct((heads, CHUNK, LANES), F32), jax.ShapeDtypeStruct((1, e), F32)],
        scratch_shapes=[pltpu.VMEM((tr, e), F32), pltpu.VMEM((tr, e), F32)],
        compiler_params=_params("arbitrary"))(pre, dus, gv, ws, bsx)


def _disc_a(lr, li, ldt):
    dt = jnp.exp(ldt)
    mag = jnp.exp(dt * lr)
    return mag * jnp.cos(dt * li), mag * jnp.sin(dt * li)


def _disc_b(lr, li, ldt, br, bi):
    ar, ai = _disc_a(lr, li, ldt)
    den = lr * lr + li * li
    qr = ((ar - 1.0) * lr + ai * li) / den
    qi = (ai * lr - (ar - 1.0) * li) / den
    return qr * br - qi * bi, qr * bi + qi * br


def _s5_disc(name, lr, li, ldt, lrx, lix, ldtx, br, bi):
    def body(lr_ref, li_ref, ldt_ref, lrx_ref, lix_ref, ldtx_ref, br_ref, bi_ref, ar_ref, ai_ref, bbr_ref, bbi_ref):
        ar_ref[...], ai_ref[...] = _disc_a(lr_ref[...], li_ref[...], ldt_ref[...])
        bbr_ref[...], bbi_ref[...] = _disc_b(lrx_ref[...], lix_ref[...], ldtx_ref[...], br_ref[...], bi_ref[...])

    small = jax.ShapeDtypeStruct(lr.shape, F32)
    wide = jax.ShapeDtypeStruct(br.shape, F32)
    return pl.pallas_call(body, name=name, out_shape=[small, small, wide, wide],
                          compiler_params=pltpu.CompilerParams(vmem_limit_bytes=VMEM_LIMIT))(
        lr, li, ldt, lrx, lix, ldtx, br, bi)


def _s5_disc_bwd(name, lr, li, ldt, lrx, lix, ldtx, br, bi, dar, dai, dbbr, dbbi, sel):
    def body(lr_ref, li_ref, ldt_ref, lrx_ref, lix_ref, ldtx_ref, br_ref, bi_ref, dar_ref, dai_ref, dbbr_ref,
             dbbi_ref, sel_ref, dlr_ref, dli_ref, dldt_ref, dbr_ref, dbi_ref):
        _, vjp_a = jax.vjp(_disc_a, lr_ref[...], li_ref[...], ldt_ref[...])
        g_lr, g_li, g_ldt = vjp_a((dar_ref[...], dai_ref[...]))
        _, vjp_b = jax.vjp(_disc_b, lrx_ref[...], lix_ref[...], ldtx_ref[...], br_ref[...], bi_ref[...])
        x_lr, x_li, x_ldt, g_br, g_bi = vjp_b((dbbr_ref[...], dbbi_ref[...]))
        fold = lambda t: jnp.dot(t, sel_ref[...], precision=lax.Precision.HIGHEST, preferred_element_type=F32)
        dlr_ref[...] = g_lr + fold(x_lr)
        dli_ref[...] = g_li + fold(x_li)
        dldt_ref[...] = jnp.sum(g_ldt + fold(x_ldt), axis=1, keepdims=True)
        dbr_ref[...] = g_br
        dbi_ref[...] = g_bi

    small = jax.ShapeDtypeStruct(lr.shape, F32)
    wide = jax.ShapeDtypeStruct(br.shape, F32)
    return pl.pallas_call(body, name=name,
                          out_shape=[small, small, jax.ShapeDtypeStruct((lr.shape[0], 1), F32), wide, wide],
                          compiler_params=pltpu.CompilerParams(vmem_limit_bytes=VMEM_LIMIT))(
        lr, li, ldt, lrx, lix, ldtx, br, bi, dar, dai, dbbr, dbbi, sel)


def _cmul(ar, ai, br, bi):
    return ar * br - ai * bi, ar * bi + ai * br


def _pow_seg(ar, ai):
    res, base, n = None, (ar, ai), SEG
    while n:
        if n & 1:
            res = base if res is None else _cmul(*res, *base)
        n >>= 1
        if n:
            base = _cmul(*base, *base)
    return res


def _scan_forward(hr_ref, hi_ref, er_ref, ei_ref, sr_ref, si_ref, ar, ai, nck):
    arb, aib = jnp.broadcast_to(ar, (nck, LANES)), jnp.broadcast_to(ai, (nck, LANES))

    def intra(t, carry):
        sr, si = carry
        slab = pl.ds(t, nck, stride=SEG)
        nr = arb * sr - aib * si + hr_ref[slab, :]
        ni = arb * si + aib * sr + hi_ref[slab, :]
        hr_ref[slab, :] = nr
        hi_ref[slab, :] = ni
        return nr, ni

    zero = jnp.zeros((nck, LANES), F32)
    er_ref[...], ei_ref[...] = lax.fori_loop(0, SEG, intra, (zero, zero), unroll=4)
    pcr, pci = _pow_seg(ar, ai)
    sr_ref[0:1, :] = jnp.zeros((1, LANES), F32)
    si_ref[0:1, :] = jnp.zeros((1, LANES), F32)
    for ck in range(nck - 1):
        pr, pi = sr_ref[ck:ck + 1, :], si_ref[ck:ck + 1, :]
        sr_ref[ck + 1:ck + 2, :] = pcr * pr - pci * pi + er_ref[ck:ck + 1, :]
        si_ref[ck + 1:ck + 2, :] = pcr * pi + pci * pr + ei_ref[ck:ck + 1, :]
    s_r, s_i = sr_ref[...], si_ref[...]

    def fix(t, carry):
        pr, pi = carry
        slab = pl.ds(t, nck, stride=SEG)
        hr_ref[slab, :] = hr_ref[slab, :] + (pr * s_r - pi * s_i)
        hi_ref[slab, :] = hi_ref[slab, :] + (pr * s_i + pi * s_r)
        return _cmul(pr, pi, arb, aib)

    lax.fori_loop(0, SEG, fix, (arb, aib), unroll=4)


def _scan_backward(gr_ref, gi_ref, hr_ref, hi_ref, er_ref, ei_ref, sr_ref, si_ref, ar, ai, nck):
    arb, aib = jnp.broadcast_to(ar, (nck, LANES)), jnp.broadcast_to(-ai, (nck, LANES))

    def intra(k, carry):
        sr, si = carry
        slab = pl.ds(SEG - 1 - k, nck, stride=SEG)
        nr = arb * sr - aib * si + gr_ref[slab, :]
        ni = arb * si + aib * sr + gi_ref[slab, :]
        gr_ref[slab, :] = nr
        gi_ref[slab, :] = ni
        return nr, ni

    zero = jnp.zeros((nck, LANES), F32)
    er_ref[...], ei_ref[...] = lax.fori_loop(0, SEG, intra, (zero, zero), unroll=4)
    pcr, pci = _pow_seg(ar, -ai)
    sr_ref[nck - 1:nck, :] = jnp.zeros((1, LANES), F32)
    si_ref[nck - 1:nck, :] = jnp.zeros((1, LANES), F32)
    for ck in range(nck - 1, 0, -1):
        pr, pi = sr_ref[ck:ck + 1, :], si_ref[ck:ck + 1, :]
        sr_ref[ck - 1:ck, :] = pcr * pr - pci * pi + er_ref[ck:ck + 1, :]
        si_ref[ck - 1:ck, :] = pcr * pi + pci * pr + ei_ref[ck:ck + 1, :]
    s_r, s_i = sr_ref[...], si_ref[...]
    last = pl.ds(SEG - 1, nck, stride=SEG)
    row = lax.broadcasted_iota(jnp.int32, (nck, LANES), 0)
    hp_r = jnp.where(row == 0, 0.0, pltpu.roll(hr_ref[last, :], 1, axis=0)) if nck > 1 else zero
    hp_i = jnp.where(row == 0, 0.0, pltpu.roll(hi_ref[last, :], 1, axis=0)) if nck > 1 else zero

    def settle(t, pr, pi, h_r, h_i):
        slab = pl.ds(t, nck, stride=SEG)
        g_r = gr_ref[slab, :] + (pr * s_r - pi * s_i)
        g_i = gi_ref[slab, :] + (pr * s_i + pi * s_r)
        gr_ref[slab, :] = g_r
        gi_ref[slab, :] = g_i
        return g_r * h_r + g_i * h_i, g_i * h_r - g_r * h_i

    def fix(k, carry):
        pr, pi, acr, aci = carry
        t = SEG - 1 - k
        prev = pl.ds(t - 1, nck, stride=SEG)
        d_r, d_i = settle(t, pr, pi, hr_ref[prev, :], hi_ref[prev, :])
        nr, ni = _cmul(pr, pi, arb, aib)
        return nr, ni, acr + d_r, aci + d_i

    pr, pi, acr, aci = lax.fori_loop(0, SEG - 1, fix, (arb, aib, zero, zero), unroll=4)
    d_r, d_i = settle(0, pr, pi, hp_r, hp_i)
    return jnp.sum(acr + d_r, axis=0, keepdims=True), jnp.sum(aci + d_i, axis=0, keepdims=True)


def _s5_fill_states(u_ref, br_ref, bi_ref, hr_ref, hi_ref, rows):
    ub = u_ref[...].astype(BF16)
    hr_ref[0:rows, :] = jnp.dot(ub, br_ref[...], preferred_element_type=F32)
    hi_ref[0:rows, :] = jnp.dot(ub, bi_ref[...], preferred_element_type=F32)
    pad = jnp.zeros((hr_ref.shape[0] - rows, LANES), F32)
    hr_ref[rows:, :] = pad
    hi_ref[rows:, :] = pad


def _s5_specs(rows, e):
    sb = STATE_BLOCKS
    chan = pl.BlockSpec((rows, LANES), lambda j: (0, j // sb))
    bmat = pl.BlockSpec((None, LANES, LANES), lambda j: (j // sb, 0, j % sb))
    cmat = pl.BlockSpec((None, LANES, LANES), lambda j: (j // sb, j % sb, 0))
    avec = pl.BlockSpec((1, LANES), lambda j: (0, j))
    dvec = pl.BlockSpec((1, LANES), lambda j: (0, j // sb))
    return chan, bmat, cmat, avec, dvec


def _s5_fwd(name, u, bre, bim, crt, cit, ar, ai, dd):
    rows, e = u.shape
    nck = rows // CHUNK
    nsteps = (e // LANES) * STATE_BLOCKS
    chan, bmat, cmat, avec, dvec = _s5_specs(rows, e)

    def body(u_ref, br_ref, bi_ref, cr_ref, ci_ref, ar_ref, ai_ref, dd_ref, y_ref, gy_ref,
             hr_ref, hi_ref, er_ref, ei_ref, sr_ref, si_ref, acc_ref):
        j = pl.program_id(0) % STATE_BLOCKS
        _s5_fill_states(u_ref, br_ref, bi_ref, hr_ref, hi_ref, rows)
        _scan_forward(hr_ref, hi_ref, er_ref, ei_ref, sr_ref, si_ref, ar_ref[...], ai_ref[...], nck)
        contrib = (jnp.dot(hr_ref[0:rows, :].astype(BF16), cr_ref[...], preferred_element_type=F32)
                   - jnp.dot(hi_ref[0:rows, :].astype(BF16), ci_ref[...], preferred_element_type=F32))

        @pl.when(j == 0)
        def _():
            acc_ref[...] = dd_ref[...] * u_ref[...] + contrib

        @pl.when(j > 0)
        def _():
            acc_ref[...] += contrib

        @pl.when(j == STATE_BLOCKS - 1)
        def _():
            y = acc_ref[...]
            y_ref[...] = y.astype(y_ref.dtype)
            gy_ref[...] = _gelu(y).astype(gy_ref.dtype)

    flat = pltpu.VMEM((rows, LANES), F32)
    big = pltpu.VMEM((nck * SEG, LANES), F32)
    small = pltpu.VMEM((nck, LANES), F32)
    out = jax.ShapeDtypeStruct((rows, e), BF16)
    return pl.pallas_call(
        body, name=name, grid=(nsteps,), in_specs=[chan, bmat, bmat, cmat, cmat, avec, avec, dvec],
        out_specs=[chan, chan], out_shape=[out, out], scratch_shapes=[big, big, small, small, small, small, flat],
        compiler_params=_params("arbitrary"))(u, bre, bim, crt, cit, ar, ai, dd)


def _s5_bwd(name, u, y, dgy, bre, bim, crt, cit, ar, ai, dd):
    rows, e = u.shape
    nb = e // LANES
    nck = rows // CHUNK
    nsteps = nb * STATE_BLOCKS
    chan, bmat, cmat, avec, dvec = _s5_specs(rows, e)

    def body(u_ref, y_ref, dgy_ref, br_ref, bi_ref, cr_ref, ci_ref, ar_ref, ai_ref, dd_ref,
             du_ref, dbr_ref, dbi_ref, dcr_ref, dci_ref, dar_ref, dai_ref, ddd_ref,
             hr_ref, hi_ref, gr_ref, gi_ref, er_ref, ei_ref, sr_ref, si_ref, acc_ref, dy_ref):
        j = pl.program_id(0) % STATE_BLOCKS
        _s5_fill_states(u_ref, br_ref, bi_ref, hr_ref, hi_ref, rows)
        _scan_forward(hr_ref, hi_ref, er_ref, ei_ref, sr_ref, si_ref, ar_ref[...], ai_ref[...], nck)

        @pl.when(j == 0)
        def _():
            dy0 = dgy_ref[...].astype(F32) * _gelu_grad(y_ref[...].astype(F32))
            dy_ref[...] = dy0
            ddd_ref[...] = jnp.sum(dy0 * u_ref[...], axis=0, keepdims=True)

        dyb = dy_ref[...].astype(BF16)
        pad = jnp.zeros((gr_ref.shape[0] - rows, LANES), F32)
        gr_ref[0:rows, :] = lax.dot_general(dyb, cr_ref[...], _DIMS['nt'], preferred_element_type=F32)
        gi_ref[0:rows, :] = -lax.dot_general(dyb, ci_ref[...], _DIMS['nt'], preferred_element_type=F32)
        gr_ref[rows:, :] = pad
        gi_ref[rows:, :] = pad
        dcr_ref[...] = lax.dot_general(hr_ref[0:rows, :].astype(BF16), dyb, _DIMS['tn'], preferred_element_type=F32)
        dci_ref[...] = -lax.dot_general(hi_ref[0:rows, :].astype(BF16), dyb, _DIMS['tn'], preferred_element_type=F32)
        dar_ref[...], dai_ref[...] = _scan_backward(gr_ref, gi_ref, hr_ref, hi_ref, er_ref, ei_ref, sr_ref, si_ref,
                                                    ar_ref[...], ai_ref[...], nck)
        ub = u_ref[...].astype(BF16)
        grb, gib = gr_ref[0:rows, :].astype(BF16), gi_ref[0:rows, :].astype(BF16)
        dbr_ref[...] = lax.dot_general(ub, grb, _DIMS['tn'], preferred_element_type=F32)
        dbi_ref[...] = lax.dot_general(ub, gib, _DIMS['tn'], preferred_element_type=F32)
        contrib = (lax.dot_general(grb, br_ref[...], _DIMS['nt'], preferred_element_type=F32)
                   + lax.dot_general(gib, bi_ref[...], _DIMS['nt'], preferred_element_type=F32))

        @pl.when(j == 0)
        def _():
            acc_ref[...] = dd_ref[...] * dy_ref[...] + contrib

        @pl.when(j > 0)
        def _():
            acc_ref[...] += contrib

        @pl.when(j == STATE_BLOCKS - 1)
        def _():
            du_ref[...] = acc_ref[...]

    flat = pltpu.VMEM((rows, LANES), F32)
    big = pltpu.VMEM((nck * SEG, LANES), F32)
    small = pltpu.VMEM((nck, LANES), F32)
    bshape = jax.ShapeDtypeStruct((nb, LANES, LANES * STATE_BLOCKS), F32)
    cshape = jax.ShapeDtypeStruct((nb, LANES * STATE_BLOCKS, LANES), F32)
    ashape = jax.ShapeDtypeStruct((1, nb * LANES * STATE_BLOCKS), F32)
    return pl.pallas_call(
        body, name=name, grid=(nsteps,),
        in_specs=[chan, chan, chan, bmat, bmat, cmat, cmat, avec, avec, dvec],
        out_specs=[chan, bmat, bmat, cmat, cmat, avec, avec, dvec],
        out_shape=[jax.ShapeDtypeStruct((rows, e), F32), bshape, bshape, cshape, cshape, ashape, ashape,
                   jax.ShapeDtypeStruct((1, e), F32)],
        scratch_shapes=[big, big, big, big, small, small, small, small, flat, flat],
        compiler_params=_params("arbitrary"))(u, y, dgy, bre, bim, crt, cit, ar, ai, dd)


def _to_blockdiag_b(bbar, nb):
    eye = jnp.eye(GROUPS_PER_BLOCK, dtype=bbar.dtype)
    t = jnp.einsum('bgpc,gh->bgchp', bbar.reshape(nb, GROUPS_PER_BLOCK, SSM_STATE, SSM_GROUP), eye)
    return t.reshape(nb, LANES, GROUPS_PER_BLOCK * SSM_STATE)


def _from_blockdiag_b(dmat, nb):
    eye = jnp.eye(GROUPS_PER_BLOCK, dtype=dmat.dtype)
    t = dmat.reshape(nb, GROUPS_PER_BLOCK, SSM_GROUP, GROUPS_PER_BLOCK, SSM_STATE)
    return jnp.einsum('bgchp,gh->bgpc', t, eye).reshape(nb * GROUPS_PER_BLOCK, SSM_STATE, SSM_GROUP)


def _to_blockdiag_ct(c, nb):
    eye = jnp.eye(GROUPS_PER_BLOCK, dtype=c.dtype)
    t = jnp.einsum('bgop,gh->bgpho', c.reshape(nb, GROUPS_PER_BLOCK, SSM_GROUP, SSM_STATE), eye)
    return t.reshape(nb, GROUPS_PER_BLOCK * SSM_STATE, LANES)


def _from_blockdiag_ct(dmat, nb):
    eye = jnp.eye(GROUPS_PER_BLOCK, dtype=dmat.dtype)
    t = dmat.reshape(nb, GROUPS_PER_BLOCK, SSM_STATE, GROUPS_PER_BLOCK, SSM_GROUP)
    return jnp.einsum('bgpho,gh->bgop', t, eye).reshape(nb * GROUPS_PER_BLOCK, SSM_GROUP, SSM_STATE)


ANY = pl.BlockSpec(memory_space=pl.ANY)


def _half_specs(kind, rdim, cdim, tr, tc, layer):
    nr, nc = rdim // tr, cdim // tc
    if kind == 'col':
        nat = pl.BlockSpec((None, tr, tc), lambda c, rb, cb: (layer, c * nr + rb, cb))
    else:
        nat = pl.BlockSpec((None, tr, tc), lambda c, rb, cb: (layer, rb, c * nc + cb))
    half = pl.BlockSpec((None, tr, tc), lambda c, rb, cb: (c, rb, cb))
    return nat, half


def _my_chip():
    return 2 * lax.axis_index("x") + lax.axis_index("y")


def _cast_halves(name, w, kind, layer):
    rdim, cdim = _half_shape(kind, w.shape)
    tr, tc = _tile(rdim, 512, 16), _tile(cdim, 1408, LANES)
    nat, _ = _half_specs(kind, rdim, cdim, tr, tc, layer)
    slot = pl.BlockSpec((None, None, tr, tc), lambda c, rb, cb: (_my_chip(), c, rb, cb))

    def body(w_ref, o_ref):
        o_ref[...] = w_ref[...].astype(o_ref.dtype)

    return pl.pallas_call(
        body, name=name, grid=(2, rdim // tr, cdim // tc), in_specs=[nat], out_specs=slot,
        out_shape=jax.ShapeDtypeStruct((4, 2, rdim, cdim), BF16),
        compiler_params=_params("parallel", "parallel", "parallel"))(w)


def _adam_math(w, g, m, v):
    m = ADAM_B1 * m + (1.0 - ADAM_B1) * g
    v = ADAM_B2 * v + (1.0 - ADAM_B2) * (g * g)
    m_hat = m / (1.0 - ADAM_B1 ** ADAM_STEP)
    v_hat = v / (1.0 - ADAM_B2 ** ADAM_STEP)
    delta = -ADAM_LR * (m_hat / (jnp.sqrt(v_hat) + ADAM_EPS) + ADAM_WD * w)
    return delta, m, v


def _adam_big(name, w, m, v, gfull, kind, layer, outs):
    rdim, cdim = _half_shape(kind, w.shape)
    tr, tc = _tile(rdim, 256, 8), _tile(cdim, 1408, LANES)
    nat, half = _half_specs(kind, rdim, cdim, tr, tc, layer)

    def body(w_ref, m_ref, v_ref, g_ref, *rest):
        go_ref, d_ref, mo_ref, vo_ref = rest[4:]
        g = g_ref[...]
        go_ref[...] = g
        d_ref[...], mo_ref[...], vo_ref[...] = _adam_math(w_ref[...], g, m_ref[...], v_ref[...])

    shape = jax.ShapeDtypeStruct(w.shape, F32)
    return pl.pallas_call(
        body, name=name, grid=(2, rdim // tr, cdim // tc), in_specs=[nat, nat, nat, half] + [ANY] * 4,
        out_specs=[nat, nat, nat, nat], out_shape=[shape, shape, shape, shape],
        input_output_aliases={4: 0, 5: 1, 6: 2, 7: 3},
        compiler_params=_params("parallel", "parallel", "parallel"))(w, m, v, gfull, *outs)


def _adam_small(w, m, v, g):
    rows = w.shape[0]
    tr = _tile(rows, 512, 8)
    spec = pl.BlockSpec((tr, LANES), lambda i: (i, 0))

    def body(w_ref, m_ref, v_ref, g_ref, d_ref, mo_ref, vo_ref):
        d_ref[...], mo_ref[...], vo_ref[...] = _adam_math(w_ref[...], g_ref[...], m_ref[...], v_ref[...])

    shape = jax.ShapeDtypeStruct(w.shape, F32)
    return pl.pallas_call(body, name="adam_small", grid=(rows // tr,), in_specs=[spec] * 4, out_specs=[spec] * 3,
                          out_shape=[shape] * 3, compiler_params=_params("parallel"))(w, m, v, g)


def _add2(name, part, got):
    cdim = part.shape[-1]
    a2, b2 = part.reshape(4, 2, -1, cdim), got.reshape(4, -1, cdim)
    rows = b2.shape[1]
    tr, tc = _tile(rows, 512, 16), _tile(cdim, 1408, LANES)
    mine = pl.BlockSpec((None, None, tr, tc), lambda k, i, j: (k, lax.axis_index("c"), i, j))
    spec = pl.BlockSpec((None, tr, tc), lambda k, i, j: (k, i, j))

    def body(a_ref, b_ref, o_ref):
        o_ref[...] = (a_ref[...].astype(F32) + b_ref[...].astype(F32)).astype(o_ref.dtype)

    out = pl.pallas_call(
        body, name=name, grid=(4, rows // tr, cdim // tc), in_specs=[mine, spec], out_specs=spec,
        out_shape=jax.ShapeDtypeStruct(b2.shape, BF16),
        compiler_params=_params("parallel", "parallel", "parallel"))(a2, b2)
    return out.reshape(got.shape)


def _add4(name, sums, recv):
    cdim = sums.shape[-1]
    s2 = sums.reshape(4, -1, cdim)
    r3 = recv.reshape(3, -1, cdim)
    rows = s2.shape[1]
    tr, tc = _tile(rows, 512, 16), _tile(cdim, 1408, LANES)
    own = pl.BlockSpec((None, tr, tc), lambda i, j: (_my_chip(), i, j))
    rspec = lambda k: pl.BlockSpec((None, tr, tc), lambda i, j: (k, i, j))
    slot = pl.BlockSpec((None, tr, tc), lambda i, j: (lax.axis_index("c"), i, j))

    def body(o_ref, x_ref, y_ref, d_ref, out_ref):
        out_ref[...] = ((o_ref[...].astype(F32) + d_ref[...].astype(F32))
                        + (x_ref[...].astype(F32) + y_ref[...].astype(F32)))

    out = pl.pallas_call(
        body, name=name, grid=(rows // tr, cdim // tc), in_specs=[own, rspec(0), rspec(1), rspec(2)], out_specs=slot,
        out_shape=jax.ShapeDtypeStruct((2, rows, cdim), F32),
        compiler_params=_params("parallel", "parallel"))(s2, r3, r3, r3)
    return out.reshape(2, *sums.shape[1:])


def _place():
    x, y, c = lax.axis_index("x"), lax.axis_index("y"), lax.axis_index("c")
    chips = [(1 - x, y), (x, 1 - y), (1 - x, 1 - y)]
    return x, y, c, chips


def _remote(src, dst, send, recv, to):
    return pltpu.make_async_remote_copy(src_ref=src, dst_ref=dst, send_sem=send, recv_sem=recv, device_id=to,
                                        device_id_type=MESH)


def _pieces(src, dst, bands):
    lead, rows = src.shape[:-2], src.shape[-2]
    band = rows // bands
    out = []
    for idx in itertools.product(*[range(dim) for dim in lead]):
        for q in range(bands):
            sl = (*idx, pl.ds(q * band, band))
            out.append((src.at[sl], dst.at[sl]))
    return out


HBM = pl.BlockSpec(memory_space=pltpu.HBM)
SEM = pl.BlockSpec(memory_space=pltpu.SEMAPHORE)
EFFECT = pltpu.SideEffectType.DATAFLOW_SIDE_EFFECTING


def _split_start(name, bufs, ncopy, plan, deps=()):
    nb, nd = len(bufs), len(deps)

    def body(*refs):
        ins, send, recv, token = refs[:nb], refs[nb + nd], refs[nb + nd + 1], refs[2 * nb + nd + 2]
        for k, (src, dst, _, to, bands) in enumerate(plan(ins)):
            for s, d in _pieces(src, dst, bands):
                _remote(s, d, send.at[k], recv.at[k], to).start()
        token[...] = jnp.zeros_like(token)

    res = pl.pallas_call(
        body, name=name, in_specs=[HBM] * nb + [ANY] * nd,
        out_specs=[SEM, SEM] + [HBM] * nb + [pl.BlockSpec(memory_space=pltpu.VMEM)],
        out_shape=[pltpu.SemaphoreType.DMA((ncopy,)), pltpu.SemaphoreType.DMA((ncopy,))]
        + [pltpu.HBM(b.shape, b.dtype) for b in bufs] + [jax.ShapeDtypeStruct((8, LANES), F32)],
        input_output_aliases={a: a + 2 for a in range(nb)},
        compiler_params=pltpu.CompilerParams(has_side_effects=EFFECT))(
        *[pltpu.with_memory_space_constraint(b, pltpu.HBM) for b in bufs], *deps)
    return res[0], res[1], list(res[2:2 + nb]), res[2 + nb]


def _split_wait(name, send, recv, bufs, plan, after):
    nb = len(bufs)

    def body(*refs):
        ins, send_sem, recv_sem = refs[:nb], refs[nb], refs[nb + 1]
        for k, (src, dst, landing, to, _) in enumerate(plan(ins)):
            _remote(src, dst, send_sem.at[k], recv_sem.at[k], to).wait_send()
            _remote(src, landing, send_sem.at[k], recv_sem.at[k], to).wait_recv()

    return pl.pallas_call(
        body, name=name, in_specs=[HBM] * nb + [SEM, SEM, ANY], out_specs=[HBM] * nb,
        out_shape=[pltpu.HBM(b.shape, b.dtype) for b in bufs], input_output_aliases={a: a for a in range(nb)},
        compiler_params=pltpu.CompilerParams(has_side_effects=EFFECT))(*bufs, send, recv, after)


def _gather_plan(n):
    def plan(refs):
        x, y, c, chips = _place()
        kme = 2 * x + y
        return [(refs[a].at[kme, c], refs[a].at[kme, c], refs[a].at[2 * chip[0] + chip[1], c], (*chip, c), 2)
                for a in range(n) for chip in chips]
    return plan


def _scatter_plan(n):
    def plan(refs):
        x, y, c, chips = _place()
        return [(refs[a].at[2 * chip[0] + chip[1]], refs[n + a].at[r], refs[n + a].at[r], (*chip, c), 2)
                for a in range(n) for r, chip in enumerate(chips)]
    return plan


def _forward_halves(slots):
    n = len(slots)

    def body(*refs):
        outs = refs[n:2 * n]
        send, recv = refs[2 * n:]
        x, y, c, chips = _place()
        sib = (x, y, 1 - c)
        for a in range(n):
            for r, chip in enumerate(chips):
                kp = 2 * chip[0] + chip[1]
                for s, d in _pieces(outs[a].at[kp, c], outs[a].at[kp, c], 2):
                    _remote(s, d, send.at[3 * a + r], recv.at[3 * a + r], sib).start()
        for a in range(n):
            for r, chip in enumerate(chips):
                kp = 2 * chip[0] + chip[1]
                _remote(outs[a].at[kp, 1 - c], outs[a].at[kp, 1 - c], send.at[3 * a + r], recv.at[3 * a + r],
                        sib).wait_recv()
                _remote(outs[a].at[kp, c], outs[a].at[kp, c], send.at[3 * a + r], recv.at[3 * a + r], sib).wait_send()

    return pl.pallas_call(
        body, name="allgather_forward", in_specs=[ANY] * n, out_specs=[ANY] * n,
        out_shape=[jax.ShapeDtypeStruct(s.shape, s.dtype) for s in slots],
        input_output_aliases={a: a for a in range(n)},
        scratch_shapes=[pltpu.SemaphoreType.DMA((3 * n,)), pltpu.SemaphoreType.DMA((3 * n,))])(*slots)


def _allgather_small(shards):
    n = len(shards)

    def body(*refs):
        ins, outs = refs[:n], refs[n:2 * n]
        send, recv, loc = refs[2 * n:]
        x, y, c, chips = _place()
        kme = 2 * x + y
        local = [pltpu.make_async_copy(ins[a], outs[a].at[kme], loc.at[a]) for a in range(n)]
        for cp in local:
            cp.start()
        cps = [_remote(ins[a], outs[a].at[kme], send.at[3 * a + r], recv.at[3 * a + r], (*chip, c))
               for a in range(n) for r, chip in enumerate(chips)]
        for cp in cps:
            cp.start()
        for a in range(n):
            for r, chip in enumerate(chips):
                kp = 2 * chip[0] + chip[1]
                _remote(ins[a], outs[a].at[kp], send.at[3 * a + r], recv.at[3 * a + r], (*chip, c)).wait_recv()
        for cp in cps:
            cp.wait_send()
        for cp in local:
            cp.wait()

    return pl.pallas_call(
        body, name="allgather_small", in_specs=[ANY] * n, out_specs=[ANY] * n,
        out_shape=[jax.ShapeDtypeStruct((4, *s.shape), s.dtype) for s in shards],
        scratch_shapes=[pltpu.SemaphoreType.DMA((3 * n,)), pltpu.SemaphoreType.DMA((3 * n,)),
                        pltpu.SemaphoreType.DMA((n,))])(*shards)


def _swap_halves(parts):
    n = len(parts)

    def body(*refs):
        ins, got = refs[:n], refs[n:2 * n]
        send, recv = refs[2 * n:]
        x, y, c, _ = _place()
        sib = (x, y, 1 - c)
        for a in range(n):
            for s, d in _pieces(ins[a].at[:, 1 - c], got[a], 1):
                _remote(s, d, send.at[a], recv.at[a], sib).start()
        for a in range(n):
            _remote(ins[a].at[:, 1 - c], got[a], send.at[a], recv.at[a], sib).wait()

    return pl.pallas_call(
        body, name="grad_swap_halves", in_specs=[ANY] * n, out_specs=[ANY] * n,
        out_shape=[jax.ShapeDtypeStruct((4, *p.shape[2:]), p.dtype) for p in parts],
        scratch_shapes=[pltpu.SemaphoreType.DMA((n,)), pltpu.SemaphoreType.DMA((n,))])(*parts)


def _join_halves(totals):
    n = len(totals)

    def body(*refs):
        outs = refs[n:2 * n]
        send, recv = refs[2 * n:]
        x, y, c, _ = _place()
        sib = (x, y, 1 - c)
        for a in range(n):
            for s, d in _pieces(outs[a].at[c], outs[a].at[c], 4):
                _remote(s, d, send.at[a], recv.at[a], sib).start()
        for a in range(n):
            _remote(outs[a].at[1 - c], outs[a].at[1 - c], send.at[a], recv.at[a], sib).wait_recv()
            _remote(outs[a].at[c], outs[a].at[c], send.at[a], recv.at[a], sib).wait_send()

    return pl.pallas_call(
        body, name="grad_join_halves", in_specs=[ANY] * n, out_specs=[ANY] * n,
        out_shape=[jax.ShapeDtypeStruct(t.shape, t.dtype) for t in totals],
        input_output_aliases={a: a for a in range(n)},
        scratch_shapes=[pltpu.SemaphoreType.DMA((n,)), pltpu.SemaphoreType.DMA((n,))])(*totals)


def _allreduce_small(packed):
    rows = packed.shape[0]
    half = rows // 2

    def body(in_ref, out_ref, q_ref, s_ref, t_ref, send, recv):
        x, y, c, chips = _place()
        sib = (x, y, 1 - c)
        mine = pl.ds(pl.multiple_of(c * half, 8), half)
        theirs = pl.ds(pl.multiple_of((1 - c) * half, 8), half)
        first = _remote(in_ref.at[theirs], q_ref, send.at[0], recv.at[0], sib)
        first.start()
        first.wait()
        s_ref[...] = in_ref[mine, :] + q_ref[...]
        cps = [_remote(s_ref, t_ref.at[r], send.at[1 + r], recv.at[1 + r], (*chip, c)) for r, chip in enumerate(chips)]
        for cp in cps:
            cp.start()
        for cp in cps:
            cp.wait()
        out_ref[mine, :] = (s_ref[...] + t_ref[2]) + (t_ref[0] + t_ref[1])
        last = _remote(out_ref.at[mine], out_ref.at[mine], send.at[4], recv.at[4], sib)
        last.start()
        _remote(out_ref.at[theirs], out_ref.at[theirs], send.at[4], recv.at[4], sib).wait_recv()
        last.wait_send()

    vm = pl.BlockSpec(memory_space=pltpu.VMEM)
    return pl.pallas_call(
        body, name="allreduce_small", in_specs=[vm], out_specs=vm, out_shape=jax.ShapeDtypeStruct(packed.shape, F32),
        scratch_shapes=[pltpu.VMEM((half, LANES), F32), pltpu.VMEM((half, LANES), F32),
                        pltpu.VMEM((3, half, LANES), F32), pltpu.SemaphoreType.DMA((5,)), pltpu.SemaphoreType.DMA((5,))],
        compiler_params=pltpu.CompilerParams(vmem_limit_bytes=VMEM_LIMIT))(packed)


PACK_ROWS = 16
PACK_BLOCK = 512


def _pack(arrs):
    parts, total = [], 0
    for a in arrs:
        flat = a.reshape(-1)
        rows = -(-flat.shape[0] // (LANES * PACK_ROWS)) * PACK_ROWS
        parts.append(jnp.pad(flat, (0, rows * LANES - flat.shape[0])).reshape(rows, LANES))
        total += rows
    tail = -total % PACK_BLOCK
    if tail:
        parts.append(jnp.zeros((tail, LANES), parts[0].dtype))
    return jnp.concatenate(parts, axis=0)


def _unpack(packed, shapes):
    out, row = [], 0
    for shp in shapes:
        size = math.prod(shp)
        rows = -(-size // (LANES * PACK_ROWS)) * PACK_ROWS
        out.append(packed[row:row + rows].reshape(-1)[:size].reshape(shp))
        row += rows
    return out


def kernel(x, norm_mix_g, norm_ffn_g, a_w_in, a_g_v, a_w_s, a_b_s, a_w_out, b_w_in, b_a_re, b_a_im, b_log_dt, b_b_re, b_b_im, b_c_re, b_c_im, b_d, b_w_glu, f_w_up, f_conv_w, f_conv_b, f_w_down, final_g, loss_target, m_norm_mix_g, m_norm_ffn_g, m_a_w_in, m_a_g_v, m_a_w_s, m_a_b_s, m_a_w_out, m_b_w_in, m_b_a_re, m_b_a_im, m_b_log_dt, m_b_b_re, m_b_b_im, m_b_c_re, m_b_c_im, m_b_d, m_b_w_glu, m_f_w_up, m_f_conv_w, m_f_conv_b, m_f_w_down, m_final_g, v_norm_mix_g, v_norm_ffn_g, v_a_w_in, v_a_g_v, v_a_w_s, v_a_b_s, v_a_w_out, v_b_w_in, v_b_a_re, v_b_a_im, v_b_log_dt, v_b_b_re, v_b_b_im, v_b_c_re, v_b_c_im, v_b_d, v_b_w_glu, v_f_w_up, v_f_conv_w, v_f_conv_b, v_f_w_down, v_final_g):
    w = dict(norm_mix_g=norm_mix_g, norm_ffn_g=norm_ffn_g, a_w_in=a_w_in, a_g_v=a_g_v, a_w_s=a_w_s, a_b_s=a_b_s,
             a_w_out=a_w_out, b_w_in=b_w_in, b_a_re=b_a_re, b_a_im=b_a_im, b_log_dt=b_log_dt, b_b_re=b_b_re,
             b_b_im=b_b_im, b_c_re=b_c_re, b_c_im=b_c_im, b_d=b_d, b_w_glu=b_w_glu, f_w_up=f_w_up, f_conv_w=f_conv_w,
             f_conv_b=f_conv_b, f_w_down=f_w_down, final_g=final_g)
    mom = dict(norm_mix_g=m_norm_mix_g, norm_ffn_g=m_norm_ffn_g, a_w_in=m_a_w_in, a_g_v=m_a_g_v, a_w_s=m_a_w_s,
               a_b_s=m_a_b_s, a_w_out=m_a_w_out, b_w_in=m_b_w_in, b_a_re=m_b_a_re, b_a_im=m_b_a_im,
               b_log_dt=m_b_log_dt, b_b_re=m_b_b_re, b_b_im=m_b_b_im, b_c_re=m_b_c_re, b_c_im=m_b_c_im, b_d=m_b_d,
               b_w_glu=m_b_w_glu, f_w_up=m_f_w_up, f_conv_w=m_f_conv_w, f_conv_b=m_f_conv_b, f_w_down=m_f_w_down,
               final_g=m_final_g)
    var = dict(norm_mix_g=v_norm_mix_g, norm_ffn_g=v_norm_ffn_g, a_w_in=v_a_w_in, a_g_v=v_a_g_v, a_w_s=v_a_w_s,
               a_b_s=v_a_b_s, a_w_out=v_a_w_out, b_w_in=v_b_w_in, b_a_re=v_b_a_re, b_a_im=v_b_a_im,
               b_log_dt=v_b_log_dt, b_b_re=v_b_b_re, b_b_im=v_b_b_im, b_c_re=v_b_c_re, b_c_im=v_b_c_im, b_d=v_b_d,
               b_w_glu=v_b_w_glu, f_w_up=v_f_w_up, f_conv_w=v_f_conv_w, f_conv_b=v_f_conv_b, f_w_down=v_f_w_down,
               final_g=v_final_g)

    rows, d = x.shape[1], x.shape[2]
    depth = norm_mix_g.shape[0]
    kchip = 2 * lax.axis_index("x") + lax.axis_index("y")
    big_names = list(BIG)
    dims = {n: _full_dims(BIG[n], w[n].shape) for n in big_names}

    keys = [(n, l) for n in big_names for l in range(w[n].shape[0])]
    slots = {(n, l): _cast_halves("cast_" + n, w[n], BIG[n], l) for n, l in keys}

    def group_keys(g):
        i = g // 2
        if g % 2 == 1:
            return [('f_w_up', i), ('f_w_down', i)]
        return [('a_w_in', i // 2), ('a_w_out', i // 2)] if i % 2 == 0 else [('b_w_in', i // 2), ('b_w_glu', i // 2)]

    bd_all, cw_all = _allgather_small([b_d, f_conv_w.reshape(-1, f_conv_w.shape[-1])])

    gathered, gather_waits, token = {}, [], bd_all
    for g in range(2 * depth):
        arrs = [slots[k] for k in group_keys(g)]
        send, recv, thru, token = _split_start(f"allgather_start_{g}", arrs, 3 * len(arrs), _gather_plan(len(arrs)),
                                               deps=(token,))
        gather_waits.append((send, recv, thru))
    gather_after = token

    def gather_group(g, after):
        send, recv, thru = gather_waits[g]
        landed = _split_wait(f"allgather_wait_{g}", send, recv, thru, _gather_plan(len(thru)), after)
        gathered.update(zip(group_keys(g), _forward_halves(landed)))

    bd_full = jnp.swapaxes(bd_all, 0, 1).reshape(b_d.shape[0], -1)
    cw_full = jnp.transpose(cw_all.reshape(4, *f_conv_w.shape), (1, 2, 0, 3)).reshape(depth, f_conv_w.shape[1], -1)

    pgrad = {}
    sgrad = {}

    def mm(name, a, wn, layer, out_dtype, residual=None, split=False):
        return _mm_x_w(name, a, gathered[wn, layer], BIG[wn], *dims[wn], out_dtype, residual, split)

    def mm_t(name, dy, wn, layer, out_dtype, split=False, deps=()):
        return _mm_dy_wt(name, dy, gathered[wn, layer], BIG[wn], *dims[wn], out_dtype, split, deps)

    def mm_g(name, xa, dy, wn, layer, split=False):
        pgrad[wn, layer] = _mm_xt_dy(name, xa, dy, BIG[wn], *dims[wn], split)

    e = d
    nb = e // LANES
    heads = e // SGU_GROUP
    h = x[0]
    saved = []
    for i in range(depth):
        j = i // 2
        gather_group(2 * i, gather_after if i == 0 else h)
        gm = norm_mix_g[i:i + 1]
        hn = _rms_fwd("rms_mix_fwd", h, gm)
        if i % 2 == 0:
            pre = mm("sgu_in", hn, 'a_w_in', j, BF16)
            bsx = jnp.broadcast_to(a_b_s[j][:, :, None], (heads, CHUNK, LANES))
            us = _sgu_mix_fwd("sgu_mix_fwd", pre, a_g_v[j:j + 1], a_w_s[j], bsx)
            h_mid = mm("sgu_out", us, 'a_w_out', j, F32, residual=h)
            mix = dict(h=h, hn=hn, pre=pre, us=us, bsx=bsx)
        else:
            groups = b_a_re.shape[1]
            rep = lambda t: jnp.repeat(t, SSM_GROUP, axis=1)
            lr, li = b_a_re[j], b_a_im[j]
            ldt = jnp.broadcast_to(b_log_dt[j][:, None], lr.shape)
            bflat = lambda t: t.reshape(groups, SSM_STATE * SSM_GROUP)
            disc_in = (lr, li, ldt, rep(lr), rep(li), rep(ldt), bflat(b_b_re[j]), bflat(b_b_im[j]))
            abr, abi, bbr, bbi = _s5_disc("s5_disc", *disc_in)
            shape_b = (groups, SSM_STATE, SSM_GROUP)
            bre = _to_blockdiag_b(bbr.reshape(shape_b), nb).astype(BF16)
            bim = _to_blockdiag_b(bbi.reshape(shape_b), nb).astype(BF16)
            crt = _to_blockdiag_ct(b_c_re[j], nb).astype(BF16)
            cit = _to_blockdiag_ct(b_c_im[j], nb).astype(BF16)
            ar_row, ai_row = abr.reshape(1, -1), abi.reshape(1, -1)
            dd = bd_full[j:j + 1]
            u = mm("s5_in", hn, 'b_w_in', j, F32)
            yv, gy = _s5_fwd("s5_fwd", u, bre, bim, crt, cit, ar_row, ai_row, dd)
            gg = mm("s5_glu", gy, 'b_w_glu', j, BF16)
            h_mid = _glu_fwd("glu_fwd", gg, h)
            mix = dict(h=h, hn=hn, u=u, y=yv, gy=gy, gg=gg, disc_in=disc_in, mats=(bre, bim, crt, cit, ar_row, ai_row, dd))
        gather_group(2 * i + 1, h_mid)
        gf = norm_ffn_g[i:i + 1]
        hn2 = _rms_fwd("rms_ffn_fwd", h_mid, gf)
        z = mm("ffn_up", hn2, 'f_w_up', i, BF16, split=True)
        cw = jnp.swapaxes(cw_full[i].reshape(cw_full.shape[1], 2, -1), 0, 1)
        cb = f_conv_b[i].reshape(2, 1, -1)
        act = _ffn_act_fwd("ffn_act_fwd", z, cw, cb)
        h_out = mm("ffn_down", act, 'f_w_down', i, F32, residual=h_mid)
        saved.append((mix, dict(h=h_mid, hn=hn2, z=z, act=act, cw=cw, cb=cb)))
        h = h_out

    dh, g_final, loss_vec = _loss_head(h, final_g.reshape(1, d), loss_target[0])
    loss = lax.psum(jnp.sum(loss_vec), ("x", "y", "c"))
    sgrad['final_g'] = g_final.reshape(d)

    g_mix, g_ffn = [None] * depth, [None] * depth
    g_cw, g_cb = [None] * depth, [None] * depth
    sg = {k: [None] * (depth // 2) for k in ('a_g_v', 'a_w_s', 'a_b_s')}
    bg = {k: [None] * (depth // 2) for k in ('b_a_re', 'b_a_im', 'b_log_dt', 'b_b_re', 'b_b_im', 'b_c_re', 'b_c_im', 'b_d')}
    scatters, flight = {}, dict(pending=None, token=())

    def scatter_group(g, done):
        if flight['pending'] is not None:
            prev, send, recv, thru = flight['pending']
            scatters[prev] = _split_wait(f"grad_scatter_wait_{prev}", send, recv, thru, _scatter_plan(len(thru) // 2), done)
        parts = [pgrad[k] for k in group_keys(g)]
        sums = [_add2("grad_chip_sum", p, q) for p, q in zip(parts, _swap_halves(parts))]
        land = [lax.empty((3, *s.shape[1:]), BF16) for s in sums]
        send, recv, thru, token = _split_start(f"grad_scatter_start_{g}", sums + land, 3 * len(sums),
                                               _scatter_plan(len(sums)))
        flight['pending'], flight['token'] = (g, send, recv, thru), (token,)

    for i in reversed(range(depth)):
        j = i // 2
        mix, ffn = saved[i]
        d_act = mm_t("ffn_down_dx", dh, 'f_w_down', i, BF16, deps=flight['token'])
        mm_g("ffn_down_dw", ffn['act'], dh, 'f_w_down', i)
        dz, dcw, dcb = _ffn_act_bwd("ffn_act_bwd", ffn['z'], d_act, ffn['cw'], ffn['cb'])
        g_cw[i], g_cb[i] = jnp.swapaxes(dcw, 0, 1).reshape(dcw.shape[1], -1), dcb.reshape(1, -1)
        mm_g("ffn_up_dw", ffn['hn'], dz, 'f_w_up', i, split=True)
        dhn = mm_t("ffn_up_dx", dz, 'f_w_up', i, F32, split=True)
        dh, g_ffn[i] = _rms_bwd("rms_ffn_bwd", ffn['h'], norm_ffn_g[i:i + 1], dhn, dh)
        scatter_group(2 * i + 1, dh)
        if i % 2 == 0:
            dus = mm_t("sgu_out_dx", dh, 'a_w_out', j, BF16, deps=flight['token'])
            mm_g("sgu_out_dw", mix['us'], dh, 'a_w_out', j)
            dpre, dws, dbs, dgv = _sgu_mix_bwd("sgu_mix_bwd", mix['pre'], dus, a_g_v[j:j + 1], a_w_s[j], mix['bsx'])
            sg['a_w_s'][j], sg['a_b_s'][j], sg['a_g_v'][j] = dws, dbs[:, :, 0], dgv[0]
            mm_g("sgu_in_dw", mix['hn'], dpre, 'a_w_in', j)
            dhn = mm_t("sgu_in_dx", dpre, 'a_w_in', j, F32)
        else:
            dgg = _glu_bwd("glu_bwd", mix['gg'], dh)
            mm_g("s5_glu_dw", mix['gy'], dgg, 'b_w_glu', j)
            dgy = mm_t("s5_glu_dx", dgg, 'b_w_glu', j, BF16, deps=flight['token'])
            du, dbr, dbi, dcr, dci, dar, dai, ddd = _s5_bwd("s5_bwd", mix['u'], mix['y'], dgy, *mix['mats'])
            groups = b_a_re.shape[1]
            flat = lambda t: _from_blockdiag_b(t, nb).reshape(groups, SSM_STATE * SSM_GROUP)
            sel = jnp.repeat(jnp.eye(SSM_STATE, dtype=F32), SSM_GROUP, axis=0)
            dlr, dli, dldt, dbre, dbim = _s5_disc_bwd(
                "s5_disc_bwd", *mix['disc_in'], dar.reshape(groups, SSM_STATE), dai.reshape(groups, SSM_STATE),
                flat(dbr), flat(dbi), sel)
            bg['b_a_re'][j], bg['b_a_im'][j], bg['b_log_dt'][j] = dlr, dli, dldt[:, 0]
            bg['b_b_re'][j] = dbre.reshape(groups, SSM_STATE, SSM_GROUP)
            bg['b_b_im'][j] = dbim.reshape(groups, SSM_STATE, SSM_GROUP)
            bg['b_c_re'][j], bg['b_c_im'][j] = _from_blockdiag_ct(dcr, nb), _from_blockdiag_ct(dci, nb)
            bg['b_d'][j] = ddd[0]
            mm_g("s5_in_dw", mix['hn'], du, 'b_w_in', j)
            dhn = mm_t("s5_in_dx", du, 'b_w_in', j, F32)
        dh, g_mix[i] = _rms_bwd("rms_mix_bwd", mix['h'], norm_mix_g[i:i + 1], dhn, dh)
        scatter_group(2 * i, dh)
    _, send, recv, thru = flight['pending']
    scatters[0] = _split_wait("grad_scatter_wait_0", send, recv, thru, _scatter_plan(len(thru) // 2), flight['token'][0])
    grad_x = dh[None]

    sgrad['norm_mix_g'] = jnp.concatenate(g_mix, axis=0)
    sgrad['norm_ffn_g'] = jnp.concatenate(g_ffn, axis=0)
    sgrad['f_conv_w'] = jnp.stack(g_cw)
    sgrad['f_conv_b'] = jnp.concatenate(g_cb, axis=0)
    for k, v_ in list(sg.items()) + list(bg.items()):
        sgrad[k] = jnp.stack(v_)

    total = _allreduce_small(_pack([sgrad[n] for n in SMALL]))
    full_shapes = [sgrad[n].shape for n in SMALL]
    gsmall = dict(zip(SMALL, _unpack(total, full_shapes)))
    for n, axis in CHIP_SHARDED_SMALL.items():
        width = w[n].shape[axis]
        gsmall[n] = lax.dynamic_slice_in_dim(gsmall[n], kchip * width, width, axis=axis)
    pk = lambda t: _pack([t[n] for n in SMALL])
    gpacked = pk(gsmall)
    dpk, mpk, vpk = _adam_small(pk(w), pk(mom), pk(var), gpacked)
    shard_shapes = [w[n].shape for n in SMALL]
    out_g = dict(gsmall)
    out_d = dict(zip(SMALL, _unpack(dpk, shard_shapes)))
    out_m = dict(zip(SMALL, _unpack(mpk, shard_shapes)))
    out_v = dict(zip(SMALL, _unpack(vpk, shard_shapes)))

    totals = {}
    for g in range(2 * depth):
        n = len(scatters[g]) // 2
        for k, s, r in zip(group_keys(g), scatters[g][:n], scatters[g][n:]):
            totals[k] = _add4("grad_total", s, r)
    gfull = _join_halves([totals[k] for k in keys])
    stacked = {n: [lax.empty(w[n].shape, F32) for _ in range(4)] for n in big_names}
    for (n, l), gf_ in zip(keys, gfull):
        stacked[n] = _adam_big("adam_" + n, w[n], mom[n], var[n], gf_, BIG[n], l, stacked[n])
    for n in big_names:
        out_g[n], out_d[n], out_m[n], out_v[n] = stacked[n]

    return (loss, grad_x, *[out_g[n] for n in W_NAMES], *[out_d[n] for n in W_NAMES],
            *[out_m[n] for n in W_NAMES], *[out_v[n] for n in W_NAMES])
```

```python
import functools
import itertools
import math

import jax
import jax.numpy as jnp
from jax import lax
from jax.experimental import pallas as pl
from jax.experimental.pallas import tpu as pltpu

F32, BF16 = jnp.float32, jnp.bfloat16
MESH = pl.DeviceIdType.MESH

CHUNK = 128
SEG = CHUNK + 4
SGU_GROUP = 128
SSM_GROUP = 16
SSM_STATE = 64
EPS = 1e-6
LANES = 128
GROUPS_PER_BLOCK = LANES // SSM_GROUP
STATE_BLOCKS = SSM_STATE // SSM_GROUP
VMEM_LIMIT = 52 * 1024 * 1024

ADAM_LR, ADAM_B1, ADAM_B2, ADAM_EPS, ADAM_WD, ADAM_STEP = 0.001, 0.9, 0.999, 1e-08, 0.01, 10

W_NAMES = ['norm_mix_g', 'norm_ffn_g', 'a_w_in', 'a_g_v', 'a_w_s', 'a_b_s', 'a_w_out', 'b_w_in', 'b_a_re', 'b_a_im',
           'b_log_dt', 'b_b_re', 'b_b_im', 'b_c_re', 'b_c_im', 'b_d', 'b_w_glu', 'f_w_up', 'f_conv_w', 'f_conv_b',
           'f_w_down', 'final_g']
BIG = {'a_w_in': 'col', 'a_w_out': 'row', 'b_w_in': 'row', 'b_w_glu': 'col', 'f_w_up': 'col', 'f_w_down': 'row'}
SMALL = [n for n in W_NAMES if n not in BIG]
CHIP_SHARDED_SMALL = {'b_d': 1, 'f_conv_w': 2}


def _tile(n, pref, align):
    t = min(n, pref)
    t -= t % align
    while t >= align:
        if n % t == 0:
            return t
        t -= align
    return n


def _params(*sem):
    return pltpu.CompilerParams(dimension_semantics=sem, vmem_limit_bytes=VMEM_LIMIT)


def _gelu(x):
    c = math.sqrt(2.0 / math.pi)
    return 0.5 * x * (1.0 + jnp.tanh(c * (x + 0.044715 * x * x * x)))


def _gelu_grad(x):
    c = math.sqrt(2.0 / math.pi)
    t = jnp.tanh(c * (x + 0.044715 * x * x * x))
    return 0.5 * (1.0 + t) + 0.5 * x * (1.0 - t * t) * c * (1.0 + 3.0 * 0.044715 * x * x)


def _half_shape(kind, shard_shape):
    _, r, c = shard_shape
    return (r // 2, c) if kind == 'col' else (r, c // 2)


def _full_dims(kind, shard_shape):
    _, r, c = shard_shape
    return (r, 4 * c) if kind == 'col' else (4 * r, c)


def _gspec(kind, kdim, ndim, tr, tc, rc):
    if kind == 'col':
        nr, nc = (kdim // 2) // tr, (ndim // 4) // tc

        def imap(*g):
            rb, cb = rc(*g)
            return (cb // nc, rb // nr, rb % nr, cb % nc)
    else:
        nr, nc = (kdim // 4) // tr, (ndim // 2) // tc

        def imap(*g):
            rb, cb = rc(*g)
            return (rb // nr, cb // nc, rb % nr, cb % nc)
    return pl.BlockSpec((None, None, tr, tc), imap)


def _act_spec(rows_blk, cols_blk, ncol_half, at):
    if ncol_half is None:
        return pl.BlockSpec((rows_blk, cols_blk), at)

    def imap(*g):
        rb, cb = at(*g)
        return (cb // ncol_half, rb, cb % ncol_half)
    return pl.BlockSpec((None, rows_blk, cols_blk), imap)


def _wtiles(kind, kdim, ndim):
    if kind == 'col':
        return _tile(kdim // 2, 1024, LANES), _tile(ndim // 4, 1408, LANES)
    return _tile(kdim // 4, 1408, LANES), _tile(ndim // 2, 1024, LANES)


_DIMS = {'nn': (((1,), (0,)), ((), ())), 'nt': (((1,), (1,)), ((), ())), 'tn': (((0,), (0,)), ((), ()))}


def _matmul(name, mode, a, b, grid, a_spec, b_spec, out_shape, out_spec, acc_shape, extras=(), extra_specs=(),
            epilogue=None, aliases=None, deps=()):
    nk = grid[2]
    dims = _DIMS[mode]
    n_epi = len(extras)
    extras, extra_specs = tuple(extras) + tuple(deps), tuple(extra_specs) + (ANY,) * len(deps)
    n_extra = len(extras)
    b_specs = b_spec if isinstance(b_spec, (list, tuple)) else [b_spec]
    nb = len(b_specs)

    def body(a_ref, *rest):
        b_refs, rest = rest[:nb], rest[nb:]
        extra_refs, o_ref = rest[:n_extra], rest[n_extra]
        width = a_ref.shape[1] // nb
        prod = None
        for p, b_ref in enumerate(b_refs):
            a_blk = a_ref[...] if nb == 1 else a_ref[:, p * width:(p + 1) * width]
            term = lax.dot_general(a_blk.astype(BF16), b_ref[...].astype(BF16), dims, preferred_element_type=F32)
            prod = term if prod is None else prod + term

        def finish(r):
            if epilogue is not None:
                r = epilogue(r, *[e[...] for e in extra_refs[:n_epi]])
            o_ref[...] = r.astype(o_ref.dtype)

        if nk == 1:
            finish(prod)
            return
        acc_ref = rest[n_extra + 1]
        kk = pl.program_id(2)

        @pl.when(kk == 0)
        def _():
            acc_ref[...] = prod

        @pl.when(kk > 0)
        def _():
            acc_ref[...] += prod

        @pl.when(kk == nk - 1)
        def _():
            finish(acc_ref[...])

    scratch = [pltpu.VMEM(acc_shape, F32)] if nk > 1 else []
    return pl.pallas_call(
        body, name=name, grid=grid, in_specs=[a_spec, *b_specs, *extra_specs], out_specs=out_spec, out_shape=out_shape,
        scratch_shapes=scratch, input_output_aliases=aliases or {},
        compiler_params=_params("parallel", "parallel", "arbitrary"))(a, *([b] * nb), *extras)


PART_BYTES = 17 * 1024 * 1024


def _shard_parts(tr, tc):
    for parts in (4, 2):
        if parts * tr * tc * 2 * 2 <= PART_BYTES:
            return parts
    return 1


def _mm_x_w(name, a, wg, kind, kdim, ndim, out_dtype, residual=None, split=False, deps=()):
    rows = a.shape[0]
    tk, tn = _wtiles(kind, kdim, ndim)
    tm = _tile(rows, 1024, 16)
    if kind == 'col':
        tk = kdim
        b_spec = [_gspec(kind, kdim, ndim, kdim // 2, tn, lambda i, j, k, p=p: (p, j)) for p in range(2)]
    else:
        parts = _shard_parts(tk, tn) if tk == kdim // 4 else 1
        b_spec = [_gspec(kind, kdim, ndim, tk, tn, lambda i, j, k, p=p: (k * parts + p, j)) for p in range(parts)]
        tk = parts * tk
    grid = (rows // tm, ndim // tn, kdim // tk)
    extras, especs, epi = (), (), None
    if residual is not None:
        extras, especs = (residual,), (pl.BlockSpec((tm, tn), lambda i, j, k: (i, j)),)
        epi = lambda r, res: r + res
    out_shape = (2, rows, ndim // 2) if split else (rows, ndim)
    return _matmul(name, 'nn', a, wg, grid, pl.BlockSpec((tm, tk), lambda i, j, k: (i, k)), b_spec,
                   jax.ShapeDtypeStruct(out_shape, out_dtype),
                   _act_spec(tm, tn, (ndim // 2) // tn if split else None, lambda i, j, k: (i, j)),
                   (tm, tn), extras, especs, epi, deps=deps)


def _mm_dy_wt(name, dy, wg, kind, kdim, ndim, out_dtype, split=False, deps=()):
    rows = dy.shape[-2]
    tn, tk = _wtiles(kind, kdim, ndim)
    tm = _tile(rows, 1024, 16)
    if kind == 'col':
        shard = ndim // 4
        parts = 1 if split else _shard_parts(tn, shard)
        b_spec = [_gspec(kind, kdim, ndim, tn, shard, lambda i, j, k, p=p: (j, k * parts + p)) for p in range(parts)]
        tk = parts * shard
    else:
        assert not split
        tk = ndim
        b_spec = [_gspec(kind, kdim, ndim, tn, ndim // 2, lambda i, j, k, p=p: (j, p)) for p in range(2)]
    grid = (rows // tm, kdim // tn, ndim // tk)
    return _matmul(name, 'nt', dy, wg, grid,
                   _act_spec(tm, tk, (ndim // 2) // tk if split else None, lambda i, j, k: (i, k)), b_spec,
                   jax.ShapeDtypeStruct((rows, kdim), out_dtype), pl.BlockSpec((tm, tn), lambda i, j, k: (i, j)),
                   (tm, tn), extras=tuple(deps), extra_specs=(ANY,) * len(deps))


def _mm_xt_dy(name, xa, dy, kind, kdim, ndim, split=False):
    rows = xa.shape[0]
    tm, tn = _wtiles(kind, kdim, ndim)
    tl = _tile(rows, 2048, 16)
    grid = (kdim // tm, ndim // tn, rows // tl)
    rdim, cdim = (kdim // 2, ndim // 4) if kind == 'col' else (kdim // 4, ndim // 2)
    return _matmul(name, 'tn', xa, dy, grid, pl.BlockSpec((tl, tm), lambda i, j, k: (k, i)),
                   _act_spec(tl, tn, (ndim // 2) // tn if split else None, lambda i, j, k: (k, j)),
                   jax.ShapeDtypeStruct((4, 2, rdim, cdim), BF16),
                   _gspec(kind, kdim, ndim, tm, tn, lambda i, j, k: (i, j)), (tm, tn))


def _rms_fwd(name, h, g):
    rows, d = h.shape
    tm = _tile(rows, 256, 16)

    def body(h_ref, g_ref, o_ref):
        x = h_ref[...]
        r = lax.rsqrt(jnp.mean(x * x, axis=-1, keepdims=True) + EPS)
        o_ref[...] = (x * r * g_ref[...]).astype(o_ref.dtype)

    return pl.pallas_call(
        body, name=name, grid=(rows // tm,),
        in_specs=[pl.BlockSpec((tm, d), lambda i: (i, 0)), pl.BlockSpec((1, d), lambda i: (0, 0))],
        out_specs=pl.BlockSpec((tm, d), lambda i: (i, 0)), out_shape=jax.ShapeDtypeStruct((rows, d), BF16),
        compiler_params=_params("parallel"))(h, g)


def _rms_bwd(name, h, g, dhn, dres):
    rows, d = h.shape
    tm = _tile(rows, 256, 16)

    def body(h_ref, g_ref, dy_ref, dres_ref, dh_ref, dg_ref):
        x = h_ref[...]
        r = lax.rsqrt(jnp.mean(x * x, axis=-1, keepdims=True) + EPS)
        xh = x * r
        dy = dy_ref[...].astype(F32)
        gy = dy * g_ref[...]
        dh_ref[...] = dres_ref[...] + r * (gy - xh * jnp.mean(gy * xh, axis=-1, keepdims=True))
        part = jnp.sum(dy * xh, axis=0, keepdims=True)

        @pl.when(pl.program_id(0) == 0)
        def _():
            dg_ref[...] = part

        @pl.when(pl.program_id(0) > 0)
        def _():
            dg_ref[...] += part

    row = pl.BlockSpec((tm, d), lambda i: (i, 0))
    vec = pl.BlockSpec((1, d), lambda i: (0, 0))
    return pl.pallas_call(
        body, name=name, grid=(rows // tm,), in_specs=[row, vec, row, row], out_specs=[row, vec],
        out_shape=[jax.ShapeDtypeStruct((rows, d), F32), jax.ShapeDtypeStruct((1, d), F32)],
        compiler_params=_params("arbitrary"))(h, g, dhn, dres)


def _loss_head(h, g, target):
    rows, d = h.shape
    tm = _tile(rows, 256, 16)

    def body(h_ref, g_ref, t_ref, dh_ref, dg_ref, loss_ref):
        x = h_ref[...]
        r = lax.rsqrt(jnp.mean(x * x, axis=-1, keepdims=True) + EPS)
        xh = x * r
        err = xh * g_ref[...] - t_ref[...]
        dy = err * (1.0 / d)
        gy = dy * g_ref[...]
        dh_ref[...] = r * (gy - xh * jnp.mean(gy * xh, axis=-1, keepdims=True))
        part = jnp.sum(dy * xh, axis=0, keepdims=True)
        sq = jnp.sum(err * err, axis=0, keepdims=True) * (0.5 / d)

        @pl.when(pl.program_id(0) == 0)
        def _():
            dg_ref[...] = part
            loss_ref[...] = sq

        @pl.when(pl.program_id(0) > 0)
        def _():
            dg_ref[...] += part
            loss_ref[...] += sq

    row = pl.BlockSpec((tm, d), lambda i: (i, 0))
    vec = pl.BlockSpec((1, d), lambda i: (0, 0))
    return pl.pallas_call(
        body, name="loss_head", grid=(rows // tm,), in_specs=[row, vec, row], out_specs=[row, vec, vec],
        out_shape=[jax.ShapeDtypeStruct((rows, d), F32), jax.ShapeDtypeStruct((1, d), F32),
                   jax.ShapeDtypeStruct((1, d), F32)],
        compiler_params=_params("arbitrary"))(h, g, target)


def _shift_down(cur, prev8, first, k):
    rows = cur.shape[0]
    rolled = pltpu.roll(cur, k, axis=0)
    idx = lax.broadcasted_iota(jnp.int32, cur.shape, 0)
    prev8 = jnp.where(first, 0.0, prev8)
    out = rolled
    for r in range(k):
        out = jnp.where(idx == r, prev8[8 - k + r:8 - k + r + 1, :], out)
    del rows
    return out


def _shift_up(cur, next8, last, k):
    rows = cur.shape[0]
    rolled = pltpu.roll(cur, rows - k, axis=0)
    idx = lax.broadcasted_iota(jnp.int32, cur.shape, 0)
    next8 = jnp.where(last, 0.0, next8)
    out = rolled
    for r in range(k):
        out = jnp.where(idx == rows - k + r, next8[r:r + 1, :], out)
    return out


def _conv_acc(z, zprev, first, w, b):
    z1 = _shift_down(z, zprev, first, 1)
    z2 = _shift_down(z, zprev, first, 2)
    return b + w[2:3, :] * z + w[1:2, :] * z1 + w[0:1, :] * z2, z1, z2


def _ffn_tiles(rows, f):
    return _tile(rows, 512, 16), _tile(f, 512, LANES)


def _ffn_act_fwd(name, z3, cw3, cb3):
    _, rows, f = z3.shape
    tm, tc = _ffn_tiles(rows, f)
    hb = tm // 8

    def body(z_ref, zp_ref, w_ref, b_ref, o_ref):
        first = pl.program_id(0) == 0
        gate, _, _ = _conv_acc(z_ref[0].astype(F32), zp_ref[0].astype(F32), first, w_ref[0], b_ref[0])
        val, _, _ = _conv_acc(z_ref[1].astype(F32), zp_ref[1].astype(F32), first, w_ref[1], b_ref[1])
        o_ref[...] = (gate * jax.nn.sigmoid(gate) * val).astype(o_ref.dtype)

    return pl.pallas_call(
        body, name=name, grid=(rows // tm, f // tc),
        in_specs=[pl.BlockSpec((2, tm, tc), lambda i, j: (0, i, j)),
                  pl.BlockSpec((2, 8, tc), lambda i, j: (0, jnp.maximum(i * hb - 1, 0), j)),
                  pl.BlockSpec((2, 3, tc), lambda i, j: (0, 0, j)), pl.BlockSpec((2, 1, tc), lambda i, j: (0, 0, j))],
        out_specs=pl.BlockSpec((tm, tc), lambda i, j: (i, j)), out_shape=jax.ShapeDtypeStruct((rows, f), BF16),
        compiler_params=_params("parallel", "parallel"))(z3, z3, cw3, cb3)


def _gate_grads(d_a, acc_g, acc_v):
    sig = jax.nn.sigmoid(acc_g)
    return d_a * acc_v * sig * (1.0 + acc_g * (1.0 - sig)), d_a * acc_g * sig


def _ffn_act_bwd(name, z3, da, cw3, cb3):
    _, rows, f = z3.shape
    tm, tc = _ffn_tiles(rows, f)
    hb = tm // 8
    nrow = rows // tm

    def body(z_ref, zp_ref, zn_ref, da_ref, dan_ref, w_ref, b_ref, dz_ref, dcw_ref, dcb_ref):
        i = pl.program_id(1)
        first, last = i == 0, i == nrow - 1
        w, b = (w_ref[0], w_ref[1]), (b_ref[0], b_ref[1])
        z = (z_ref[0].astype(F32), z_ref[1].astype(F32))
        acc, taps = [], []
        for hf in range(2):
            a_h, z1, z2 = _conv_acc(z[hf], zp_ref[hf].astype(F32), first, w[hf], b[hf])
            acc.append(a_h)
            taps.append((z2, z1, z[hf]))
        dacc = _gate_grads(da_ref[...].astype(F32), acc[0], acc[1])
        acc_n = [_conv_acc(zn_ref[hf].astype(F32), z[hf][tm - 8:tm, :], False, w[hf], b[hf])[0] for hf in range(2)]
        dacc_n = _gate_grads(dan_ref[...].astype(F32), acc_n[0], acc_n[1])
        for hf in range(2):
            d = dacc[hf]
            d1 = _shift_up(d, dacc_n[hf], last, 1)
            d2 = _shift_up(d, dacc_n[hf], last, 2)
            dz_ref[hf] = (w[hf][2:3, :] * d + w[hf][1:2, :] * d1 + w[hf][0:1, :] * d2).astype(dz_ref.dtype)
        sums_w = [[jnp.sum(dacc[hf] * t, axis=0, keepdims=True) for t in taps[hf]] for hf in range(2)]
        sums_b = [jnp.sum(dacc[hf], axis=0, keepdims=True) for hf in range(2)]

        @pl.when(first)
        def _():
            for hf in range(2):
                for k in range(3):
                    dcw_ref[hf, k:k + 1, :] = sums_w[hf][k]
                dcb_ref[hf] = sums_b[hf]

        @pl.when(i > 0)
        def _():
            for hf in range(2):
                for k in range(3):
                    dcw_ref[hf, k:k + 1, :] += sums_w[hf][k]
                dcb_ref[hf] += sums_b[hf]

    nxt = lambda i: jnp.minimum((i + 1) * hb, rows // 8 - 1)
    wsp = pl.BlockSpec((2, 3, tc), lambda j, i: (0, 0, j))
    bsp = pl.BlockSpec((2, 1, tc), lambda j, i: (0, 0, j))
    cur = pl.BlockSpec((2, tm, tc), lambda j, i: (0, i, j))
    return pl.pallas_call(
        body, name=name, grid=(f // tc, nrow),
        in_specs=[cur, pl.BlockSpec((2, 8, tc), lambda j, i: (0, jnp.maximum(i * hb - 1, 0), j)),
                  pl.BlockSpec((2, 8, tc), lambda j, i: (0, nxt(i), j)), pl.BlockSpec((tm, tc), lambda j, i: (i, j)),
                  pl.BlockSpec((8, tc), lambda j, i: (nxt(i), j)), wsp, bsp],
        out_specs=[cur, wsp, bsp],
        out_shape=[jax.ShapeDtypeStruct((2, rows, f), BF16), jax.ShapeDtypeStruct((2, 3, f), F32),
                   jax.ShapeDtypeStruct((2, 1, f), F32)],
        compiler_params=_params("parallel", "arbitrary"))(z3, z3, z3, da, da, cw3, cb3)


def _glu_fwd(name, gg, h):
    rows, d2 = gg.shape
    d = d2 // 2
    tm, tc = _tile(rows, 512, 16), _tile(d, 1024, LANES)
    nd = d // tc

    def body(a_ref, b_ref, h_ref, o_ref):
        o_ref[...] = h_ref[...] + a_ref[...].astype(F32) * jax.nn.sigmoid(b_ref[...].astype(F32))

    return pl.pallas_call(
        body, name=name, grid=(rows // tm, nd),
        in_specs=[pl.BlockSpec((tm, tc), lambda i, j: (i, j)), pl.BlockSpec((tm, tc), lambda i, j: (i, j + nd)),
                  pl.BlockSpec((tm, tc), lambda i, j: (i, j))],
        out_specs=pl.BlockSpec((tm, tc), lambda i, j: (i, j)), out_shape=jax.ShapeDtypeStruct((rows, d), F32),
        compiler_params=_params("parallel", "parallel"))(gg, gg, h)


def _glu_bwd(name, gg, dh):
    rows, d2 = gg.shape
    d = d2 // 2
    tm, tc = _tile(rows, 512, 16), _tile(d, 1024, LANES)
    nd = d // tc

    def body(s_ref, o_ref, dh_ref, out_ref):
        is_a = pl.program_id(1) < nd
        me = s_ref[...].astype(F32)
        other = o_ref[...].astype(F32)
        g = dh_ref[...]
        sig_o = jax.nn.sigmoid(other)
        sig_m = jax.nn.sigmoid(me)
        out_ref[...] = jnp.where(is_a, g * sig_o, g * other * sig_m * (1.0 - sig_m)).astype(out_ref.dtype)

    return pl.pallas_call(
        body, name=name, grid=(rows // tm, 2 * nd),
        in_specs=[pl.BlockSpec((tm, tc), lambda i, j: (i, j)),
                  pl.BlockSpec((tm, tc), lambda i, j: (i, (j + nd) % (2 * nd))),
                  pl.BlockSpec((tm, tc), lambda i, j: (i, j % nd))],
        out_specs=pl.BlockSpec((tm, tc), lambda i, j: (i, j)), out_shape=jax.ShapeDtypeStruct((rows, d2), BF16),
        compiler_params=_params("parallel", "parallel"))(gg, gg, dh)


def _sgu_common(pre_ref, gv_ref, e):
    u = _gelu(pre_ref[:, :e].astype(F32))
    v = _gelu(pre_ref[:, e:].astype(F32))
    r = lax.rsqrt(jnp.mean(v * v, axis=-1, keepdims=True) + EPS)
    vh = v * r
    return u, vh, r, (vh * gv_ref[...]).astype(BF16)


def _tril_bf16(ws_ref, hd):
    t = lax.broadcasted_iota(jnp.int32, (CHUNK, CHUNK), 0)
    s = lax.broadcasted_iota(jnp.int32, (CHUNK, CHUNK), 1)
    return jnp.where(s <= t, ws_ref[hd], 0.0).astype(BF16)


def _sgu_mix_fwd(name, pre, gv, ws, bsx):
    rows, e2 = pre.shape
    e = e2 // 2
    heads = e // SGU_GROUP
    tr = _tile(rows, 256, CHUNK)

    def body(pre_ref, gv_ref, ws_ref, bs_ref, o_ref):
        u, _, _, vn = _sgu_common(pre_ref, gv_ref, e)
        for hd in range(heads):
            wm = _tril_bf16(ws_ref, hd)
            cols = slice(hd * SGU_GROUP, (hd + 1) * SGU_GROUP)
            for ck in range(tr // CHUNK):
                rws = slice(ck * CHUNK, (ck + 1) * CHUNK)
                s = jnp.dot(wm, vn[rws, cols], preferred_element_type=F32) + bs_ref[hd]
                o_ref[rws, cols] = (u[rws, cols] * s).astype(o_ref.dtype)

    whole3 = pl.BlockSpec((heads, CHUNK, CHUNK), lambda i: (0, 0, 0))
    return pl.pallas_call(
        body, name=name, grid=(rows // tr,),
        in_specs=[pl.BlockSpec((tr, e2), lambda i: (i, 0)), pl.BlockSpec((1, e), lambda i: (0, 0)), whole3, whole3],
        out_specs=pl.BlockSpec((tr, e), lambda i: (i, 0)), out_shape=jax.ShapeDtypeStruct((rows, e), BF16),
        compiler_params=_params("parallel"))(pre, gv, ws, bsx)


def _sgu_mix_bwd(name, pre, dus, gv, ws, bsx):
    rows, e2 = pre.shape
    e = e2 // 2
    heads = e // SGU_GROUP
    tr = _tile(rows, 256, CHUNK)

    def body(pre_ref, dus_ref, gv_ref, ws_ref, bs_ref, dpre_ref, dws_ref, dbs_ref, dgv_ref, dvn_ref, du_ref):
        first = pl.program_id(0) == 0
        u, vh, r, vn = _sgu_common(pre_ref, gv_ref, e)
        ones = jnp.ones((SGU_GROUP, LANES), BF16)
        tt = lax.broadcasted_iota(jnp.int32, (CHUNK, CHUNK), 0)
        ss = lax.broadcasted_iota(jnp.int32, (CHUNK, CHUNK), 1)
        for hd in range(heads):
            wm = _tril_bf16(ws_ref, hd)
            cols = slice(hd * SGU_GROUP, (hd + 1) * SGU_GROUP)
            dw = jnp.zeros((CHUNK, CHUNK), F32)
            db = jnp.zeros((CHUNK, LANES), F32)
            for ck in range(tr // CHUNK):
                rws = slice(ck * CHUNK, (ck + 1) * CHUNK)
                vblk = vn[rws, cols]
                s = jnp.dot(wm, vblk, preferred_element_type=F32) + bs_ref[hd]
                d_us = dus_ref[rws, cols].astype(F32)
                du_ref[rws, cols] = d_us * s
                ds = (d_us * u[rws, cols]).astype(BF16)
                dvn_ref[rws, cols] = lax.dot_general(wm, ds, _DIMS['tn'], preferred_element_type=F32)
                dw = dw + lax.dot_general(ds, vblk, _DIMS['nt'], preferred_element_type=F32)
                db = db + jnp.dot(ds, ones, preferred_element_type=F32)
            dw = jnp.where(ss <= tt, dw, 0.0)

            @pl.when(first)
            def _():
                dws_ref[hd] = dw
                dbs_ref[hd] = db

            @pl.when(jnp.logical_not(first))
            def _():
                dws_ref[hd] += dw
                dbs_ref[hd] += db

        dvn = dvn_ref[...]
        part = jnp.sum(dvn * vh, axis=0, keepdims=True)

        @pl.when(first)
        def _():
            dgv_ref[...] = part

        @pl.when(jnp.logical_not(first))
        def _():
            dgv_ref[...] += part

        gy = dvn * gv_ref[...]
        dv = r * (gy - vh * jnp.mean(gy * vh, axis=-1, keepdims=True))
        dpre_ref[:, :e] = (du_ref[...] * _gelu_grad(pre_ref[:, :e].astype(F32))).astype(dpre_ref.dtype)
        dpre_ref[:, e:] = (dv * _gelu_grad(pre_ref[:, e:].astype(F32))).astype(dpre_ref.dtype)

    whole3 = pl.BlockSpec((heads, CHUNK, CHUNK), lambda i: (0, 0, 0))
    vec = pl.BlockSpec((1, e), lambda i: (0, 0))
    return pl.pallas_call(
        body, name=name, grid=(rows // tr,),
        in_specs=[pl.BlockSpec((tr, e2), lambda i: (i, 0)), pl.BlockSpec((tr, e), lambda i: (i, 0)), vec, whole3, whole3],
        out_specs=[pl.BlockSpec((tr, e2), lambda i: (i, 0)), whole3, whole3, vec],
        out_shape=[jax.ShapeDtypeStruct((rows, e2), BF16), jax.ShapeDtypeStruct((heads, CHUNK, CHUNK), F32),
                   jax.ShapeDtypeStruct((heads, CHUNK, LANES), F32), jax.ShapeDtypeStruct((1, e), F32)],
        scratch_shapes=[pltpu.VMEM((tr, e), F32), pltpu.VMEM((tr, e), F32)],
        compiler_params=_params("arbitrary"))(pre, dus, gv, ws, bsx)


def _disc_a(lr, li, ldt):
    dt = jnp.exp(ldt)
    mag = jnp.exp(dt * lr)
    return mag * jnp.cos(dt * li), mag * jnp.sin(dt * li)


def _disc_b(lr, li, ldt, br, bi):
    ar, ai = _disc_a(lr, li, ldt)
    den = lr * lr + li * li
    qr = ((ar - 1.0) * lr + ai * li) / den
    qi = (ai * lr - (ar - 1.0) * li) / den
    return qr * br - qi * bi, qr * bi + qi * br


def _s5_disc(name, lr, li, ldt, lrx, lix, ldtx, br, bi):
    def body(lr_ref, li_ref, ldt_ref, lrx_ref, lix_ref, ldtx_ref, br_ref, bi_ref, ar_ref, ai_ref, bbr_ref, bbi_ref):
        ar_ref[...], ai_ref[...] = _disc_a(lr_ref[...], li_ref[...], ldt_ref[...])
        bbr_ref[...], bbi_ref[...] = _disc_b(lrx_ref[...], lix_ref[...], ldtx_ref[...], br_ref[...], bi_ref[...])

    small = jax.ShapeDtypeStruct(lr.shape, F32)
    wide = jax.ShapeDtypeStruct(br.shape, F32)
    return pl.pallas_call(body, name=name, out_shape=[small, small, wide, wide],
                          compiler_params=pltpu.CompilerParams(vmem_limit_bytes=VMEM_LIMIT))(
        lr, li, ldt, lrx, lix, ldtx, br, bi)


def _s5_disc_bwd(name, lr, li, ldt, lrx, lix, ldtx, br, bi, dar, dai, dbbr, dbbi, sel):
    def body(lr_ref, li_ref, ldt_ref, lrx_ref, lix_ref, ldtx_ref, br_ref, bi_ref, dar_ref, dai_ref, dbbr_ref,
             dbbi_ref, sel_ref, dlr_ref, dli_ref, dldt_ref, dbr_ref, dbi_ref):
        _, vjp_a = jax.vjp(_disc_a, lr_ref[...], li_ref[...], ldt_ref[...])
        g_lr, g_li, g_ldt = vjp_a((dar_ref[...], dai_ref[...]))
        _, vjp_b = jax.vjp(_disc_b, lrx_ref[...], lix_ref[...], ldtx_ref[...], br_ref[...], bi_ref[...])
        x_lr, x_li, x_ldt, g_br, g_bi = vjp_b((dbbr_ref[...], dbbi_ref[...]))
        fold = lambda t: jnp.dot(t, sel_ref[...], precision=lax.Precision.HIGHEST, preferred_element_type=F32)
        dlr_ref[...] = g_lr + fold(x_lr)
        dli_ref[...] = g_li + fold(x_li)
        dldt_ref[...] = jnp.sum(g_ldt + fold(x_ldt), axis=1, keepdims=True)
        dbr_ref[...] = g_br
        dbi_ref[...] = g_bi

    small = jax.ShapeDtypeStruct(lr.shape, F32)
    wide = jax.ShapeDtypeStruct(br.shape, F32)
    return pl.pallas_call(body, name=name,
                          out_shape=[small, small, jax.ShapeDtypeStruct((lr.shape[0], 1), F32), wide, wide],
                          compiler_params=pltpu.CompilerParams(vmem_limit_bytes=VMEM_LIMIT))(
        lr, li, ldt, lrx, lix, ldtx, br, bi, dar, dai, dbbr, dbbi, sel)


def _cmul(ar, ai, br, bi):
    return ar * br - ai * bi, ar * bi + ai * br


def _pow_seg(ar, ai):
    res, base, n = None, (ar, ai), SEG
    while n:
        if n & 1:
            res = base if res is None else _cmul(*res, *base)
        n >>= 1
        if n:
            base = _cmul(*base, *base)
    return res


def _scan_forward(hr_ref, hi_ref, er_ref, ei_ref, sr_ref, si_ref, ar, ai, nck):
    arb, aib = jnp.broadcast_to(ar, (nck, LANES)), jnp.broadcast_to(ai, (nck, LANES))

    def intra(t, carry):
        sr, si = carry
        slab = pl.ds(t, nck, stride=SEG)
        nr = arb * sr - aib * si + hr_ref[slab, :]
        ni = arb * si + aib * sr + hi_ref[slab, :]
        hr_ref[slab, :] = nr
        hi_ref[slab, :] = ni
        return nr, ni

    zero = jnp.zeros((nck, LANES), F32)
    er_ref[...], ei_ref[...] = lax.fori_loop(0, SEG, intra, (zero, zero), unroll=4)
    pcr, pci = _pow_seg(ar, ai)
    sr_ref[0:1, :] = jnp.zeros((1, LANES), F32)
    si_ref[0:1, :] = jnp.zeros((1, LANES), F32)
    for ck in range(nck - 1):
        pr, pi = sr_ref[ck:ck + 1, :], si_ref[ck:ck + 1, :]
        sr_ref[ck + 1:ck + 2, :] = pcr * pr - pci * pi + er_ref[ck:ck + 1, :]
        si_ref[ck + 1:ck + 2, :] = pcr * pi + pci * pr + ei_ref[ck:ck + 1, :]
    s_r, s_i = sr_ref[...], si_ref[...]

    def fix(t, carry):
        pr, pi = carry
        slab = pl.ds(t, nck, stride=SEG)
        hr_ref[slab, :] = hr_ref[slab, :] + (pr * s_r - pi * s_i)
        hi_ref[slab, :] = hi_ref[slab, :] + (pr * s_i + pi * s_r)
        return _cmul(pr, pi, arb, aib)

    lax.fori_loop(0, SEG, fix, (arb, aib), unroll=4)


def _scan_backward(gr_ref, gi_ref, hr_ref, hi_ref, er_ref, ei_ref, sr_ref, si_ref, ar, ai, nck):
    arb, aib = jnp.broadcast_to(ar, (nck, LANES)), jnp.broadcast_to(-ai, (nck, LANES))

    def intra(k, carry):
        sr, si = carry
        slab = pl.ds(SEG - 1 - k, nck, stride=SEG)
        nr = arb * sr - aib * si + gr_ref[slab, :]
        ni = arb * si + aib * sr + gi_ref[slab, :]
        gr_ref[slab, :] = nr
        gi_ref[slab, :] = ni
        return nr, ni

    zero = jnp.zeros((nck, LANES), F32)
    er_ref[...], ei_ref[...] = lax.fori_loop(0, SEG, intra, (zero, zero), unroll=4)
    pcr, pci = _pow_seg(ar, -ai)
    sr_ref[nck - 1:nck, :] = jnp.zeros((1, LANES), F32)
    si_ref[nck - 1:nck, :] = jnp.zeros((1, LANES), F32)
    for ck in range(nck - 1, 0, -1):
        pr, pi = sr_ref[ck:ck + 1, :], si_ref[ck:ck + 1, :]
        sr_ref[ck - 1:ck, :] = pcr * pr - pci * pi + er_ref[ck:ck + 1, :]
        si_ref[ck - 1:ck, :] = pcr * pi + pci * pr + ei_ref[ck:ck + 1, :]
    s_r, s_i = sr_ref[...], si_ref[...]
    last = pl.ds(SEG - 1, nck, stride=SEG)
    row = lax.broadcasted_iota(jnp.int32, (nck, LANES), 0)
    hp_r = jnp.where(row == 0, 0.0, pltpu.roll(hr_ref[last, :], 1, axis=0)) if nck > 1 else zero
    hp_i = jnp.where(row == 0, 0.0, pltpu.roll(hi_ref[last, :], 1, axis=0)) if nck > 1 else zero

    def settle(t, pr, pi, h_r, h_i):
        slab = pl.ds(t, nck, stride=SEG)
        g_r = gr_ref[slab, :] + (pr * s_r - pi * s_i)
        g_i = gi_ref[slab, :] + (pr * s_i + pi * s_r)
        gr_ref[slab, :] = g_r
        gi_ref[slab, :] = g_i
        return g_r * h_r + g_i * h_i, g_i * h_r - g_r * h_i

    def fix(k, carry):
        pr, pi, acr, aci = carry
        t = SEG - 1 - k
        prev = pl.ds(t - 1, nck, stride=SEG)
        d_r, d_i = settle(t, pr, pi, hr_ref[prev, :], hi_ref[prev, :])
        nr, ni = _cmul(pr, pi, arb, aib)
        return nr, ni, acr + d_r, aci + d_i

    pr, pi, acr, aci = lax.fori_loop(0, SEG - 1, fix, (arb, aib, zero, zero), unroll=4)
    d_r, d_i = settle(0, pr, pi, hp_r, hp_i)
    return jnp.sum(acr + d_r, axis=0, keepdims=True), jnp.sum(aci + d_i, axis=0, keepdims=True)


def _s5_fill_states(u_ref, br_ref, bi_ref, hr_ref, hi_ref, rows):
    ub = u_ref[...].astype(BF16)
    hr_ref[0:rows, :] = jnp.dot(ub, br_ref[...], preferred_element_type=F32)
    hi_ref[0:rows, :] = jnp.dot(ub, bi_ref[...], preferred_element_type=F32)
    pad = jnp.zeros((hr_ref.shape[0] - rows, LANES), F32)
    hr_ref[rows:, :] = pad
    hi_ref[rows:, :] = pad


def _s5_specs(rows, e):
    sb = STATE_BLOCKS
    chan = pl.BlockSpec((rows, LANES), lambda j: (0, j // sb))
    bmat = pl.BlockSpec((None, LANES, LANES), lambda j: (j // sb, 0, j % sb))
    cmat = pl.BlockSpec((None, LANES, LANES), lambda j: (j // sb, j % sb, 0))
    avec = pl.BlockSpec((1, LANES), lambda j: (0, j))
    dvec = pl.BlockSpec((1, LANES), lambda j: (0, j // sb))
    return chan, bmat, cmat, avec, dvec


def _s5_fwd(name, u, bre, bim, crt, cit, ar, ai, dd):
    rows, e = u.shape
    nck = rows // CHUNK
    nsteps = (e // LANES) * STATE_BLOCKS
    chan, bmat, cmat, avec, dvec = _s5_specs(rows, e)

    def body(u_ref, br_ref, bi_ref, cr_ref, ci_ref, ar_ref, ai_ref, dd_ref, y_ref, gy_ref,
             hr_ref, hi_ref, er_ref, ei_ref, sr_ref, si_ref, acc_ref):
        j = pl.program_id(0) % STATE_BLOCKS
        _s5_fill_states(u_ref, br_ref, bi_ref, hr_ref, hi_ref, rows)
        _scan_forward(hr_ref, hi_ref, er_ref, ei_ref, sr_ref, si_ref, ar_ref[...], ai_ref[...], nck)
        contrib = (jnp.dot(hr_ref[0:rows, :].astype(BF16), cr_ref[...], preferred_element_type=F32)
                   - jnp.dot(hi_ref[0:rows, :].astype(BF16), ci_ref[...], preferred_element_type=F32))

        @pl.when(j == 0)
        def _():
            acc_ref[...] = dd_ref[...] * u_ref[...] + contrib

        @pl.when(j > 0)
        def _():
            acc_ref[...] += contrib

        @pl.when(j == STATE_BLOCKS - 1)
        def _():
            y = acc_ref[...]
            y_ref[...] = y.astype(y_ref.dtype)
            gy_ref[...] = _gelu(y).astype(gy_ref.dtype)

    flat = pltpu.VMEM((rows, LANES), F32)
    big = pltpu.VMEM((nck * SEG, LANES), F32)
    small = pltpu.VMEM((nck, LANES), F32)
    out = jax.ShapeDtypeStruct((rows, e), BF16)
    return pl.pallas_call(
        body, name=name, grid=(nsteps,), in_specs=[chan, bmat, bmat, cmat, cmat, avec, avec, dvec],
        out_specs=[chan, chan], out_shape=[out, out], scratch_shapes=[big, big, small, small, small, small, flat],
        compiler_params=_params("arbitrary"))(u, bre, bim, crt, cit, ar, ai, dd)


def _s5_bwd(name, u, y, dgy, bre, bim, crt, cit, ar, ai, dd):
    rows, e = u.shape
    nb = e // LANES
    nck = rows // CHUNK
    nsteps = nb * STATE_BLOCKS
    chan, bmat, cmat, avec, dvec = _s5_specs(rows, e)

    def body(u_ref, y_ref, dgy_ref, br_ref, bi_ref, cr_ref, ci_ref, ar_ref, ai_ref, dd_ref,
             du_ref, dbr_ref, dbi_ref, dcr_ref, dci_ref, dar_ref, dai_ref, ddd_ref,
             hr_ref, hi_ref, gr_ref, gi_ref, er_ref, ei_ref, sr_ref, si_ref, acc_ref, dy_ref):
        j = pl.program_id(0) % STATE_BLOCKS
        _s5_fill_states(u_ref, br_ref, bi_ref, hr_ref, hi_ref, rows)
        _scan_forward(hr_ref, hi_ref, er_ref, ei_ref, sr_ref, si_ref, ar_ref[...], ai_ref[...], nck)

        @pl.when(j == 0)
        def _():
            dy0 = dgy_ref[...].astype(F32) * _gelu_grad(y_ref[...].astype(F32))
            dy_ref[...] = dy0
            ddd_ref[...] = jnp.sum(dy0 * u_ref[...], axis=0, keepdims=True)

        dyb = dy_ref[...].astype(BF16)
        pad = jnp.zeros((gr_ref.shape[0] - rows, LANES), F32)
        gr_ref[0:rows, :] = lax.dot_general(dyb, cr_ref[...], _DIMS['nt'], preferred_element_type=F32)
        gi_ref[0:rows, :] = -lax.dot_general(dyb, ci_ref[...], _DIMS['nt'], preferred_element_type=F32)
        gr_ref[rows:, :] = pad
        gi_ref[rows:, :] = pad
        dcr_ref[...] = lax.dot_general(hr_ref[0:rows, :].astype(BF16), dyb, _DIMS['tn'], preferred_element_type=F32)
        dci_ref[...] = -lax.dot_general(hi_ref[0:rows, :].astype(BF16), dyb, _DIMS['tn'], preferred_element_type=F32)
        dar_ref[...], dai_ref[...] = _scan_backward(gr_ref, gi_ref, hr_ref, hi_ref, er_ref, ei_ref, sr_ref, si_ref,
                                                    ar_ref[...], ai_ref[...], nck)
        ub = u_ref[...].astype(BF16)
        grb, gib = gr_ref[0:rows, :].astype(BF16), gi_ref[0:rows, :].astype(BF16)
        dbr_ref[...] = lax.dot_general(ub, grb, _DIMS['tn'], preferred_element_type=F32)
        dbi_ref[...] = lax.dot_general(ub, gib, _DIMS['tn'], preferred_element_type=F32)
        contrib = (lax.dot_general(grb, br_ref[...], _DIMS['nt'], preferred_element_type=F32)
                   + lax.dot_general(gib, bi_ref[...], _DIMS['nt'], preferred_element_type=F32))

        @pl.when(j == 0)
        def _():
            acc_ref[...] = dd_ref[...] * dy_ref[...] + contrib

        @pl.when(j > 0)
        def _():
            acc_ref[...] += contrib

        @pl.when(j == STATE_BLOCKS - 1)
        def _():
            du_ref[...] = acc_ref[...]

    flat = pltpu.VMEM((rows, LANES), F32)
    big = pltpu.VMEM((nck * SEG, LANES), F32)
    small = pltpu.VMEM((nck, LANES), F32)
    bshape = jax.ShapeDtypeStruct((nb, LANES, LANES * STATE_BLOCKS), F32)
    cshape = jax.ShapeDtypeStruct((nb, LANES * STATE_BLOCKS, LANES), F32)
    ashape = jax.ShapeDtypeStruct((1, nb * LANES * STATE_BLOCKS), F32)
    return pl.pallas_call(
        body, name=name, grid=(nsteps,),
        in_specs=[chan, chan, chan, bmat, bmat, cmat, cmat, avec, avec, dvec],
        out_specs=[chan, bmat, bmat, cmat, cmat, avec, avec, dvec],
        out_shape=[jax.ShapeDtypeStruct((rows, e), F32), bshape, bshape, cshape, cshape, ashape, ashape,
                   jax.ShapeDtypeStruct((1, e), F32)],
        scratch_shapes=[big, big, big, big, small, small, small, small, flat, flat],
        compiler_params=_params("arbitrary"))(u, y, dgy, bre, bim, crt, cit, ar, ai, dd)


def _to_blockdiag_b(bbar, nb):
    eye = jnp.eye(GROUPS_PER_BLOCK, dtype=bbar.dtype)
    t = jnp.einsum('bgpc,gh->bgchp', bbar.reshape(nb, GROUPS_PER_BLOCK, SSM_STATE, SSM_GROUP), eye)
    return t.reshape(nb, LANES, GROUPS_PER_BLOCK * SSM_STATE)


def _from_blockdiag_b(dmat, nb):
    eye = jnp.eye(GROUPS_PER_BLOCK, dtype=dmat.dtype)
    t = dmat.reshape(nb, GROUPS_PER_BLOCK, SSM_GROUP, GROUPS_PER_BLOCK, SSM_STATE)
    return jnp.einsum('bgchp,gh->bgpc', t, eye).reshape(nb * GROUPS_PER_BLOCK, SSM_STATE, SSM_GROUP)


def _to_blockdiag_ct(c, nb):
    eye = jnp.eye(GROUPS_PER_BLOCK, dtype=c.dtype)
    t = jnp.einsum('bgop,gh->bgpho', c.reshape(nb, GROUPS_PER_BLOCK, SSM_GROUP, SSM_STATE), eye)
    return t.reshape(nb, GROUPS_PER_BLOCK * SSM_STATE, LANES)


def _from_blockdiag_ct(dmat, nb):
    eye = jnp.eye(GROUPS_PER_BLOCK, dtype=dmat.dtype)
    t = dmat.reshape(nb, GROUPS_PER_BLOCK, SSM_STATE, GROUPS_PER_BLOCK, SSM_GROUP)
    return jnp.einsum('bgpho,gh->bgop', t, eye).reshape(nb * GROUPS_PER_BLOCK, SSM_GROUP, SSM_STATE)


ANY = pl.BlockSpec(memory_space=pl.ANY)


def _half_specs(kind, rdim, cdim, tr, tc, layer):
    nr, nc = rdim // tr, cdim // tc
    if kind == 'col':
        nat = pl.BlockSpec((None, tr, tc), lambda c, rb, cb: (layer, c * nr + rb, cb))
    else:
        nat = pl.BlockSpec((None, tr, tc), lambda c, rb, cb: (layer, rb, c * nc + cb))
    half = pl.BlockSpec((None, tr, tc), lambda c, rb, cb: (c, rb, cb))
    return nat, half


def _my_chip():
    return 2 * lax.axis_index("x") + lax.axis_index("y")


def _cast_halves(name, w, kind, layer):
    rdim, cdim = _half_shape(kind, w.shape)
    tr, tc = _tile(rdim, 512, 16), _tile(cdim, 1408, LANES)
    nat, _ = _half_specs(kind, rdim, cdim, tr, tc, layer)
    slot = pl.BlockSpec((None, None, tr, tc), lambda c, rb, cb: (_my_chip(), c, rb, cb))

    def body(w_ref, o_ref):
        o_ref[...] = w_ref[...].astype(o_ref.dtype)

    return pl.pallas_call(
        body, name=name, grid=(2, rdim // tr, cdim // tc), in_specs=[nat], out_specs=slot,
        out_shape=jax.ShapeDtypeStruct((4, 2, rdim, cdim), BF16),
        compiler_params=_params("parallel", "parallel", "parallel"))(w)


def _adam_math(w, g, m, v):
    m = ADAM_B1 * m + (1.0 - ADAM_B1) * g
    v = ADAM_B2 * v + (1.0 - ADAM_B2) * (g * g)
    m_hat = m / (1.0 - ADAM_B1 ** ADAM_STEP)
    v_hat = v / (1.0 - ADAM_B2 ** ADAM_STEP)
    delta = -ADAM_LR * (m_hat / (jnp.sqrt(v_hat) + ADAM_EPS) + ADAM_WD * w)
    return delta, m, v


def _adam_big(name, w, m, v, gfull, kind, layer, outs):
    rdim, cdim = _half_shape(kind, w.shape)
    tr, tc = _tile(rdim, 256, 8), _tile(cdim, 1408, LANES)
    nat, half = _half_specs(kind, rdim, cdim, tr, tc, layer)

    def body(w_ref, m_ref, v_ref, g_ref, *rest):
        go_ref, d_ref, mo_ref, vo_ref = rest[4:]
        g = g_ref[...]
        go_ref[...] = g
        d_ref[...], mo_ref[...], vo_ref[...] = _adam_math(w_ref[...], g, m_ref[...], v_ref[...])

    shape = jax.ShapeDtypeStruct(w.shape, F32)
    return pl.pallas_call(
        body, name=name, grid=(2, rdim // tr, cdim // tc), in_specs=[nat, nat, nat, half] + [ANY] * 4,
        out_specs=[nat, nat, nat, nat], out_shape=[shape, shape, shape, shape],
        input_output_aliases={4: 0, 5: 1, 6: 2, 7: 3},
        compiler_params=_params("parallel", "parallel", "parallel"))(w, m, v, gfull, *outs)


def _adam_small(w, m, v, g):
    rows = w.shape[0]
    tr = _tile(rows, 512, 8)
    spec = pl.BlockSpec((tr, LANES), lambda i: (i, 0))

    def body(w_ref, m_ref, v_ref, g_ref, d_ref, mo_ref, vo_ref):
        d_ref[...], mo_ref[...], vo_ref[...] = _adam_math(w_ref[...], g_ref[...], m_ref[...], v_ref[...])

    shape = jax.ShapeDtypeStruct(w.shape, F32)
    return pl.pallas_call(body, name="adam_small", grid=(rows // tr,), in_specs=[spec] * 4, out_specs=[spec] * 3,
                          out_shape=[shape] * 3, compiler_params=_params("parallel"))(w, m, v, g)


def _add2(name, part, got):
    cdim = part.shape[-1]
    a2, b2 = part.reshape(4, 2, -1, cdim), got.reshape(4, -1, cdim)
    rows = b2.shape[1]
    tr, tc = _tile(rows, 512, 16), _tile(cdim, 1408, LANES)
    mine = pl.BlockSpec((None, None, tr, tc), lambda k, i, j: (k, lax.axis_index("c"), i, j))
    spec = pl.BlockSpec((None, tr, tc), lambda k, i, j: (k, i, j))

    def body(a_ref, b_ref, o_ref):
        o_ref[...] = (a_ref[...].astype(F32) + b_ref[...].astype(F32)).astype(o_ref.dtype)

    out = pl.pallas_call(
        body, name=name, grid=(4, rows // tr, cdim // tc), in_specs=[mine, spec], out_specs=spec,
        out_shape=jax.ShapeDtypeStruct(b2.shape, BF16),
        compiler_params=_params("parallel", "parallel", "parallel"))(a2, b2)
    return out.reshape(got.shape)


def _add4(name, sums, recv):
    cdim = sums.shape[-1]
    s2 = sums.reshape(4, -1, cdim)
    r3 = recv.reshape(3, -1, cdim)
    rows = s2.shape[1]
    tr, tc = _tile(rows, 512, 16), _tile(cdim, 1408, LANES)
    own = pl.BlockSpec((None, tr, tc), lambda i, j: (_my_chip(), i, j))
    rspec = lambda k: pl.BlockSpec((None, tr, tc), lambda i, j: (k, i, j))
    slot = pl.BlockSpec((None, tr, tc), lambda i, j: (lax.axis_index("c"), i, j))

    def body(o_ref, x_ref, y_ref, d_ref, out_ref):
        out_ref[...] = ((o_ref[...].astype(F32) + d_ref[...].astype(F32))
                        + (x_ref[...].astype(F32) + y_ref[...].astype(F32)))

    out = pl.pallas_call(
        body, name=name, grid=(rows // tr, cdim // tc), in_specs=[own, rspec(0), rspec(1), rspec(2)], out_specs=slot,
        out_shape=jax.ShapeDtypeStruct((2, rows, cdim), F32),
        compiler_params=_params("parallel", "parallel"))(s2, r3, r3, r3)
    return out.reshape(2, *sums.shape[1:])


def _place():
    x, y, c = lax.axis_index("x"), lax.axis_index("y"), lax.axis_index("c")
    chips = [(1 - x, y), (x, 1 - y), (1 - x, 1 - y)]
    return x, y, c, chips


def _remote(src, dst, send, recv, to):
    return pltpu.make_async_remote_copy(src_ref=src, dst_ref=dst, send_sem=send, recv_sem=recv, device_id=to,
                                        device_id_type=MESH)


def _pieces(src, dst, bands):
    lead, rows = src.shape[:-2], src.shape[-2]
    band = rows // bands
    out = []
    for idx in itertools.product(*[range(dim) for dim in lead]):
        for q in range(bands):
            sl = (*idx, pl.ds(q * band, band))
            out.append((src.at[sl], dst.at[sl]))
    return out


HBM = pl.BlockSpec(memory_space=pltpu.HBM)
SEM = pl.BlockSpec(memory_space=pltpu.SEMAPHORE)
EFFECT = pltpu.SideEffectType.DATAFLOW_SIDE_EFFECTING


def _split_start(name, bufs, ncopy, plan, deps=()):
    nb, nd = len(bufs), len(deps)

    def body(*refs):
        ins, send, recv, token = refs[:nb], refs[nb + nd], refs[nb + nd + 1], refs[2 * nb + nd + 2]
        for k, (src, dst, _, to, bands) in enumerate(plan(ins)):
            for s, d in _pieces(src, dst, bands):
                _remote(s, d, send.at[k], recv.at[k], to).start()
        token[...] = jnp.zeros_like(token)

    res = pl.pallas_call(
        body, name=name, in_specs=[HBM] * nb + [ANY] * nd,
        out_specs=[SEM, SEM] + [HBM] * nb + [pl.BlockSpec(memory_space=pltpu.VMEM)],
        out_shape=[pltpu.SemaphoreType.DMA((ncopy,)), pltpu.SemaphoreType.DMA((ncopy,))]
        + [pltpu.HBM(b.shape, b.dtype) for b in bufs] + [jax.ShapeDtypeStruct((8, LANES), F32)],
        input_output_aliases={a: a + 2 for a in range(nb)},
        compiler_params=pltpu.CompilerParams(has_side_effects=EFFECT))(
        *[pltpu.with_memory_space_constraint(b, pltpu.HBM) for b in bufs], *deps)
    return res[0], res[1], list(res[2:2 + nb]), res[2 + nb]


def _split_wait(name, send, recv, bufs, plan, after):
    nb = len(bufs)

    def body(*refs):
        ins, send_sem, recv_sem = refs[:nb], refs[nb], refs[nb + 1]
        for k, (src, dst, landing, to, _) in enumerate(plan(ins)):
            _remote(src, dst, send_sem.at[k], recv_sem.at[k], to).wait_send()
            _remote(src, landing, send_sem.at[k], recv_sem.at[k], to).wait_recv()

    return pl.pallas_call(
        body, name=name, in_specs=[HBM] * nb + [SEM, SEM, ANY], out_specs=[HBM] * nb,
        out_shape=[pltpu.HBM(b.shape, b.dtype) for b in bufs], input_output_aliases={a: a for a in range(nb)},
        compiler_params=pltpu.CompilerParams(has_side_effects=EFFECT))(*bufs, send, recv, after)


def _gather_plan(n):
    def plan(refs):
        x, y, c, chips = _place()
        kme = 2 * x + y
        return [(refs[a].at[kme, c], refs[a].at[kme, c], refs[a].at[2 * chip[0] + chip[1], c], (*chip, c), 2)
                for a in range(n) for chip in chips]
    return plan


def _scatter_plan(n):
    def plan(refs):
        x, y, c, chips = _place()
        return [(refs[a].at[2 * chip[0] + chip[1]], refs[n + a].at[r], refs[n + a].at[r], (*chip, c), 2)
                for a in range(n) for r, chip in enumerate(chips)]
    return plan


def _forward_plan(n):
    def plan(refs):
        x, y, c, chips = _place()
        sib = (x, y, 1 - c)
        return [(refs[a].at[2 * chip[0] + chip[1], c], refs[a].at[2 * chip[0] + chip[1], c],
                 refs[a].at[2 * chip[0] + chip[1], 1 - c], sib, 2) for a in range(n) for chip in chips]
    return plan


def _join_plan(n):
    def plan(refs):
        x, y, c, _ = _place()
        return [(refs[a].at[c], refs[a].at[c], refs[a].at[1 - c], (x, y, 1 - c), 4) for a in range(n)]
    return plan


def _allgather_small(shards):
    n = len(shards)

    def body(*refs):
        ins, outs = refs[:n], refs[n:2 * n]
        send, recv, loc = refs[2 * n:]
        x, y, c, chips = _place()
        kme = 2 * x + y
        local = [pltpu.make_async_copy(ins[a], outs[a].at[kme], loc.at[a]) for a in range(n)]
        for cp in local:
            cp.start()
        cps = [_remote(ins[a], outs[a].at[kme], send.at[3 * a + r], recv.at[3 * a + r], (*chip, c))
               for a in range(n) for r, chip in enumerate(chips)]
        for cp in cps:
            cp.start()
        for a in range(n):
            for r, chip in enumerate(chips):
                kp = 2 * chip[0] + chip[1]
                _remote(ins[a], outs[a].at[kp], send.at[3 * a + r], recv.at[3 * a + r], (*chip, c)).wait_recv()
        for cp in cps:
            cp.wait_send()
        for cp in local:
            cp.wait()

    return pl.pallas_call(
        body, name="allgather_small", in_specs=[ANY] * n, out_specs=[ANY] * n,
        out_shape=[jax.ShapeDtypeStruct((4, *s.shape), s.dtype) for s in shards],
        scratch_shapes=[pltpu.SemaphoreType.DMA((3 * n,)), pltpu.SemaphoreType.DMA((3 * n,)),
                        pltpu.SemaphoreType.DMA((n,))])(*shards)


def _swap_halves(parts):
    n = len(parts)

    def body(*refs):
        ins, got = refs[:n], refs[n:2 * n]
        send, recv = refs[2 * n:]
        x, y, c, _ = _place()
        sib = (x, y, 1 - c)
        for a in range(n):
            for s, d in _pieces(ins[a].at[:, 1 - c], got[a], 1):
                _remote(s, d, send.at[a], recv.at[a], sib).start()
        for a in range(n):
            _remote(ins[a].at[:, 1 - c], got[a], send.at[a], recv.at[a], sib).wait()

    return pl.pallas_call(
        body, name="grad_swap_halves", in_specs=[ANY] * n, out_specs=[ANY] * n,
        out_shape=[jax.ShapeDtypeStruct((4, *p.shape[2:]), p.dtype) for p in parts],
        scratch_shapes=[pltpu.SemaphoreType.DMA((n,)), pltpu.SemaphoreType.DMA((n,))])(*parts)


def _join_halves(totals):
    n = len(totals)

    def body(*refs):
        outs = refs[n:2 * n]
        send, recv = refs[2 * n:]
        x, y, c, _ = _place()
        sib = (x, y, 1 - c)
        for a in range(n):
            for s, d in _pieces(outs[a].at[c], outs[a].at[c], 4):
                _remote(s, d, send.at[a], recv.at[a], sib).start()
        for a in range(n):
            _remote(outs[a].at[1 - c], outs[a].at[1 - c], send.at[a], recv.at[a], sib).wait_recv()
            _remote(outs[a].at[c], outs[a].at[c], send.at[a], recv.at[a], sib).wait_send()

    return pl.pallas_call(
        body, name="grad_join_halves", in_specs=[ANY] * n, out_specs=[ANY] * n,
        out_shape=[jax.ShapeDtypeStruct(t.shape, t.dtype) for t in totals],
        input_output_aliases={a: a for a in range(n)},
        scratch_shapes=[pltpu.SemaphoreType.DMA((n,)), pltpu.SemaphoreType.DMA((n,))])(*totals)


def _allreduce_small(packed):
    rows = packed.shape[0]
    half = rows // 2

    def body(in_ref, out_ref, q_ref, s_ref, t_ref, send, recv):
        x, y, c, chips = _place()
        sib = (x, y, 1 - c)
        mine = pl.ds(pl.multiple_of(c * half, 8), half)
        theirs = pl.ds(pl.multiple_of((1 - c) * half, 8), half)
        first = _remote(in_ref.at[theirs], q_ref, send.at[0], recv.at[0], sib)
        first.start()
        first.wait()
        s_ref[...] = in_ref[mine, :] + q_ref[...]
        cps = [_remote(s_ref, t_ref.at[r], send.at[1 + r], recv.at[1 + r], (*chip, c)) for r, chip in enumerate(chips)]
        for cp in cps:
            cp.start()
        for cp in cps:
            cp.wait()
        out_ref[mine, :] = (s_ref[...] + t_ref[2]) + (t_ref[0] + t_ref[1])
        last = _remote(out_ref.at[mine], out_ref.at[mine], send.at[4], recv.at[4], sib)
        last.start()
        _remote(out_ref.at[theirs], out_ref.at[theirs], send.at[4], recv.at[4], sib).wait_recv()
        last.wait_send()

    vm = pl.BlockSpec(memory_space=pltpu.VMEM)
    return pl.pallas_call(
        body, name="allreduce_small", in_specs=[vm], out_specs=vm, out_shape=jax.ShapeDtypeStruct(packed.shape, F32),
        scratch_shapes=[pltpu.VMEM((half, LANES), F32), pltpu.VMEM((half, LANES), F32),
                        pltpu.VMEM((3, half, LANES), F32), pltpu.SemaphoreType.DMA((5,)), pltpu.SemaphoreType.DMA((5,))],
        compiler_params=pltpu.CompilerParams(vmem_limit_bytes=VMEM_LIMIT))(packed)


PACK_ROWS = 16
PACK_BLOCK = 512


def _pack(arrs):
    parts, total = [], 0
    for a in arrs:
        flat = a.reshape(-1)
        rows = -(-flat.shape[0] // (LANES * PACK_ROWS)) * PACK_ROWS
        parts.append(jnp.pad(flat, (0, rows * LANES - flat.shape[0])).reshape(rows, LANES))
        total += rows
    tail = -total % PACK_BLOCK
    if tail:
        parts.append(jnp.zeros((tail, LANES), parts[0].dtype))
    return jnp.concatenate(parts, axis=0)


def _unpack(packed, shapes):
    out, row = [], 0
    for shp in shapes:
        size = math.prod(shp)
        rows = -(-size // (LANES * PACK_ROWS)) * PACK_ROWS
        out.append(packed[row:row + rows].reshape(-1)[:size].reshape(shp))
        row += rows
    return out


def kernel(x, norm_mix_g, norm_ffn_g, a_w_in, a_g_v, a_w_s, a_b_s, a_w_out, b_w_in, b_a_re, b_a_im, b_log_dt, b_b_re, b_b_im, b_c_re, b_c_im, b_d, b_w_glu, f_w_up, f_conv_w, f_conv_b, f_w_down, final_g, loss_target, m_norm_mix_g, m_norm_ffn_g, m_a_w_in, m_a_g_v, m_a_w_s, m_a_b_s, m_a_w_out, m_b_w_in, m_b_a_re, m_b_a_im, m_b_log_dt, m_b_b_re, m_b_b_im, m_b_c_re, m_b_c_im, m_b_d, m_b_w_glu, m_f_w_up, m_f_conv_w, m_f_conv_b, m_f_w_down, m_final_g, v_norm_mix_g, v_norm_ffn_g, v_a_w_in, v_a_g_v, v_a_w_s, v_a_b_s, v_a_w_out, v_b_w_in, v_b_a_re, v_b_a_im, v_b_log_dt, v_b_b_re, v_b_b_im, v_b_c_re, v_b_c_im, v_b_d, v_b_w_glu, v_f_w_up, v_f_conv_w, v_f_conv_b, v_f_w_down, v_final_g):
    w = dict(norm_mix_g=norm_mix_g, norm_ffn_g=norm_ffn_g, a_w_in=a_w_in, a_g_v=a_g_v, a_w_s=a_w_s, a_b_s=a_b_s,
             a_w_out=a_w_out, b_w_in=b_w_in, b_a_re=b_a_re, b_a_im=b_a_im, b_log_dt=b_log_dt, b_b_re=b_b_re,
             b_b_im=b_b_im, b_c_re=b_c_re, b_c_im=b_c_im, b_d=b_d, b_w_glu=b_w_glu, f_w_up=f_w_up, f_conv_w=f_conv_w,
             f_conv_b=f_conv_b, f_w_down=f_w_down, final_g=final_g)
    mom = dict(norm_mix_g=m_norm_mix_g, norm_ffn_g=m_norm_ffn_g, a_w_in=m_a_w_in, a_g_v=m_a_g_v, a_w_s=m_a_w_s,
               a_b_s=m_a_b_s, a_w_out=m_a_w_out, b_w_in=m_b_w_in, b_a_re=m_b_a_re, b_a_im=m_b_a_im,
               b_log_dt=m_b_log_dt, b_b_re=m_b_b_re, b_b_im=m_b_b_im, b_c_re=m_b_c_re, b_c_im=m_b_c_im, b_d=m_b_d,
               b_w_glu=m_b_w_glu, f_w_up=m_f_w_up, f_conv_w=m_f_conv_w, f_conv_b=m_f_conv_b, f_w_down=m_f_w_down,
               final_g=m_final_g)
    var = dict(norm_mix_g=v_norm_mix_g, norm_ffn_g=v_norm_ffn_g, a_w_in=v_a_w_in, a_g_v=v_a_g_v, a_w_s=v_a_w_s,
               a_b_s=v_a_b_s, a_w_out=v_a_w_out, b_w_in=v_b_w_in, b_a_re=v_b_a_re, b_a_im=v_b_a_im,
               b_log_dt=v_b_log_dt, b_b_re=v_b_b_re, b_b_im=v_b_b_im, b_c_re=v_b_c_re, b_c_im=v_b_c_im, b_d=v_b_d,
               b_w_glu=v_b_w_glu, f_w_up=v_f_w_up, f_conv_w=v_f_conv_w, f_conv_b=v_f_conv_b, f_w_down=v_f_w_down,
               final_g=v_final_g)

    rows, d = x.shape[1], x.shape[2]
    depth = norm_mix_g.shape[0]
    kchip = 2 * lax.axis_index("x") + lax.axis_index("y")
    big_names = list(BIG)
    dims = {n: _full_dims(BIG[n], w[n].shape) for n in big_names}

    keys = [(n, l) for n in big_names for l in range(w[n].shape[0])]
    slots = {(n, l): _cast_halves("cast_" + n, w[n], BIG[n], l) for n, l in keys}

    def group_keys(g):
        i = g // 2
        if g % 2 == 1:
            return [('f_w_up', i), ('f_w_down', i)]
        return [('a_w_in', i // 2), ('a_w_out', i // 2)] if i % 2 == 0 else [('b_w_in', i // 2), ('b_w_glu', i // 2)]

    bd_all, cw_all = _allgather_small([b_d, f_conv_w.reshape(-1, f_conv_w.shape[-1])])

    gathered, gather_waits, token = {}, [], bd_all
    for g in range(2 * depth):
        arrs = [slots[k] for k in group_keys(g)]
        send, recv, thru, token = _split_start(f"allgather_start_{g}", arrs, 3 * len(arrs), _gather_plan(len(arrs)),
                                               deps=(token,))
        gather_waits.append((send, recv, thru))
    gather_after = token

    forwards = {}

    def prefetch_group(g, after):
        send, recv, thru = gather_waits[g]
        landed = _split_wait(f"allgather_wait_{g}", send, recv, thru, _gather_plan(len(thru)), after)
        send, recv, thru, token = _split_start(f"allgather_pass_start_{g}", landed, 3 * len(landed),
                                               _forward_plan(len(landed)))
        forwards[g] = (send, recv, thru)
        return (token,)

    def ready_group(g, after):
        send, recv, thru = forwards.pop(g)
        arrs = _split_wait(f"allgather_pass_wait_{g}", send, recv, thru, _forward_plan(len(thru)), after)
        gathered.update(zip(group_keys(g), arrs))

    bd_full = jnp.swapaxes(bd_all, 0, 1).reshape(b_d.shape[0], -1)
    cw_full = jnp.transpose(cw_all.reshape(4, *f_conv_w.shape), (1, 2, 0, 3)).reshape(depth, f_conv_w.shape[1], -1)

    pgrad = {}
    sgrad = {}

    def mm(name, a, wn, layer, out_dtype, residual=None, split=False, deps=()):
        return _mm_x_w(name, a, gathered[wn, layer], BIG[wn], *dims[wn], out_dtype, residual, split, deps)

    def mm_t(name, dy, wn, layer, out_dtype, split=False, deps=()):
        return _mm_dy_wt(name, dy, gathered[wn, layer], BIG[wn], *dims[wn], out_dtype, split, deps)

    def mm_g(name, xa, dy, wn, layer, split=False):
        pgrad[wn, layer] = _mm_xt_dy(name, xa, dy, BIG[wn], *dims[wn], split)

    e = d
    nb = e // LANES
    heads = e // SGU_GROUP
    h = x[0]
    saved = []
    for i in range(depth):
        j = i // 2
        if i == 0:
            prefetch_group(0, gather_after)
        ready_group(2 * i, gather_after if i == 0 else h)
        gm = norm_mix_g[i:i + 1]
        hn = _rms_fwd("rms_mix_fwd", h, gm)
        if i % 2 == 0:
            pre = mm("sgu_in", hn, 'a_w_in', j, BF16)
            passing = prefetch_group(2 * i + 1, pre)
            bsx = jnp.broadcast_to(a_b_s[j][:, :, None], (heads, CHUNK, LANES))
            us = _sgu_mix_fwd("sgu_mix_fwd", pre, a_g_v[j:j + 1], a_w_s[j], bsx)
            h_mid = mm("sgu_out", us, 'a_w_out', j, F32, residual=h, deps=passing)
            mix = dict(h=h, hn=hn, pre=pre, us=us, bsx=bsx)
        else:
            groups = b_a_re.shape[1]
            rep = lambda t: jnp.repeat(t, SSM_GROUP, axis=1)
            lr, li = b_a_re[j], b_a_im[j]
            ldt = jnp.broadcast_to(b_log_dt[j][:, None], lr.shape)
            bflat = lambda t: t.reshape(groups, SSM_STATE * SSM_GROUP)
            disc_in = (lr, li, ldt, rep(lr), rep(li), rep(ldt), bflat(b_b_re[j]), bflat(b_b_im[j]))
            abr, abi, bbr, bbi = _s5_disc("s5_disc", *disc_in)
            shape_b = (groups, SSM_STATE, SSM_GROUP)
            bre = _to_blockdiag_b(bbr.reshape(shape_b), nb).astype(BF16)
            bim = _to_blockdiag_b(bbi.reshape(shape_b), nb).astype(BF16)
            crt = _to_blockdiag_ct(b_c_re[j], nb).astype(BF16)
            cit = _to_blockdiag_ct(b_c_im[j], nb).astype(BF16)
            ar_row, ai_row = abr.reshape(1, -1), abi.reshape(1, -1)
            dd = bd_full[j:j + 1]
            u = mm("s5_in", hn, 'b_w_in', j, F32)
            passing = prefetch_group(2 * i + 1, u)
            yv, gy = _s5_fwd("s5_fwd", u, bre, bim, crt, cit, ar_row, ai_row, dd)
            gg = mm("s5_glu", gy, 'b_w_glu', j, BF16, deps=passing)
            h_mid = _glu_fwd("glu_fwd", gg, h)
            mix = dict(h=h, hn=hn, u=u, y=yv, gy=gy, gg=gg, disc_in=disc_in, mats=(bre, bim, crt, cit, ar_row, ai_row, dd))
        ready_group(2 * i + 1, h_mid)
        gf = norm_ffn_g[i:i + 1]
        hn2 = _rms_fwd("rms_ffn_fwd", h_mid, gf)
        z = mm("ffn_up", hn2, 'f_w_up', i, BF16, split=True)
        passing = prefetch_group(2 * i + 2, z) if i + 1 < depth else ()
        cw = jnp.swapaxes(cw_full[i].reshape(cw_full.shape[1], 2, -1), 0, 1)
        cb = f_conv_b[i].reshape(2, 1, -1)
        act = _ffn_act_fwd("ffn_act_fwd", z, cw, cb)
        h_out = mm("ffn_down", act, 'f_w_down', i, F32, residual=h_mid, deps=passing)
        saved.append((mix, dict(h=h_mid, hn=hn2, z=z, act=act, cw=cw, cb=cb)))
        h = h_out

    dh, g_final, loss_vec = _loss_head(h, final_g.reshape(1, d), loss_target[0])
    loss = lax.psum(jnp.sum(loss_vec), ("x", "y", "c"))
    sgrad['final_g'] = g_final.reshape(d)

    g_mix, g_ffn = [None] * depth, [None] * depth
    g_cw, g_cb = [None] * depth, [None] * depth
    sg = {k: [None] * (depth // 2) for k in ('a_g_v', 'a_w_s', 'a_b_s')}
    bg = {k: [None] * (depth // 2) for k in ('b_a_re', 'b_a_im', 'b_log_dt', 'b_b_re', 'b_b_im', 'b_c_re', 'b_c_im', 'b_d')}
    scatters, flight = {}, dict(pending=None, token=())

    def scatter_group(g, done):
        if flight['pending'] is not None:
            prev, send, recv, thru = flight['pending']
            scatters[prev] = _split_wait(f"grad_scatter_wait_{prev}", send, recv, thru, _scatter_plan(len(thru) // 2), done)
        parts = [pgrad[k] for k in group_keys(g)]
        sums = [_add2("grad_chip_sum", p, q) for p, q in zip(parts, _swap_halves(parts))]
        land = [lax.empty((3, *s.shape[1:]), BF16) for s in sums]
        send, recv, thru, token = _split_start(f"grad_scatter_start_{g}", sums + land, 3 * len(sums),
                                               _scatter_plan(len(sums)))
        flight['pending'], flight['token'] = (g, send, recv, thru), (token,)

    for i in reversed(range(depth)):
        j = i // 2
        mix, ffn = saved[i]
        d_act = mm_t("ffn_down_dx", dh, 'f_w_down', i, BF16, deps=flight['token'])
        mm_g("ffn_down_dw", ffn['act'], dh, 'f_w_down', i)
        dz, dcw, dcb = _ffn_act_bwd("ffn_act_bwd", ffn['z'], d_act, ffn['cw'], ffn['cb'])
        g_cw[i], g_cb[i] = jnp.swapaxes(dcw, 0, 1).reshape(dcw.shape[1], -1), dcb.reshape(1, -1)
        mm_g("ffn_up_dw", ffn['hn'], dz, 'f_w_up', i, split=True)
        dhn = mm_t("ffn_up_dx", dz, 'f_w_up', i, F32, split=True)
        dh, g_ffn[i] = _rms_bwd("rms_ffn_bwd", ffn['h'], norm_ffn_g[i:i + 1], dhn, dh)
        scatter_group(2 * i + 1, dh)
        if i % 2 == 0:
            dus = mm_t("sgu_out_dx", dh, 'a_w_out', j, BF16, deps=flight['token'])
            mm_g("sgu_out_dw", mix['us'], dh, 'a_w_out', j)
            dpre, dws, dbs, dgv = _sgu_mix_bwd("sgu_mix_bwd", mix['pre'], dus, a_g_v[j:j + 1], a_w_s[j], mix['bsx'])
            sg['a_w_s'][j], sg['a_b_s'][j], sg['a_g_v'][j] = dws, dbs[:, :, 0], dgv[0]
            mm_g("sgu_in_dw", mix['hn'], dpre, 'a_w_in', j)
            dhn = mm_t("sgu_in_dx", dpre, 'a_w_in', j, F32)
        else:
            dgg = _glu_bwd("glu_bwd", mix['gg'], dh)
            mm_g("s5_glu_dw", mix['gy'], dgg, 'b_w_glu', j)
            dgy = mm_t("s5_glu_dx", dgg, 'b_w_glu', j, BF16, deps=flight['token'])
            du, dbr, dbi, dcr, dci, dar, dai, ddd = _s5_bwd("s5_bwd", mix['u'], mix['y'], dgy, *mix['mats'])
            groups = b_a_re.shape[1]
            flat = lambda t: _from_blockdiag_b(t, nb).reshape(groups, SSM_STATE * SSM_GROUP)
            sel = jnp.repeat(jnp.eye(SSM_STATE, dtype=F32), SSM_GROUP, axis=0)
            dlr, dli, dldt, dbre, dbim = _s5_disc_bwd(
                "s5_disc_bwd", *mix['disc_in'], dar.reshape(groups, SSM_STATE), dai.reshape(groups, SSM_STATE),
                flat(dbr), flat(dbi), sel)
            bg['b_a_re'][j], bg['b_a_im'][j], bg['b_log_dt'][j] = dlr, dli, dldt[:, 0]
            bg['b_b_re'][j] = dbre.reshape(groups, SSM_STATE, SSM_GROUP)
            bg['b_b_im'][j] = dbim.reshape(groups, SSM_STATE, SSM_GROUP)
            bg['b_c_re'][j], bg['b_c_im'][j] = _from_blockdiag_ct(dcr, nb), _from_blockdiag_ct(dci, nb)
            bg['b_d'][j] = ddd[0]
            mm_g("s5_in_dw", mix['hn'], du, 'b_w_in', j)
            dhn = mm_t("s5_in_dx", du, 'b_w_in', j, F32)
        dh, g_mix[i] = _rms_bwd("rms_mix_bwd", mix['h'], norm_mix_g[i:i + 1], dhn, dh)
        scatter_group(2 * i, dh)
    grad_x = dh[None]

    sgrad['norm_mix_g'] = jnp.concatenate(g_mix, axis=0)
    sgrad['norm_ffn_g'] = jnp.concatenate(g_ffn, axis=0)
    sgrad['f_conv_w'] = jnp.stack(g_cw)
    sgrad['f_conv_b'] = jnp.concatenate(g_cb, axis=0)
    for k, v_ in list(sg.items()) + list(bg.items()):
        sgrad[k] = jnp.stack(v_)

    total = _allreduce_small(_pack([sgrad[n] for n in SMALL]))
    full_shapes = [sgrad[n].shape for n in SMALL]
    gsmall = dict(zip(SMALL, _unpack(total, full_shapes)))
    for n, axis in CHIP_SHARDED_SMALL.items():
        width = w[n].shape[axis]
        gsmall[n] = lax.dynamic_slice_in_dim(gsmall[n], kchip * width, width, axis=axis)
    pk = lambda t: _pack([t[n] for n in SMALL])
    gpacked = pk(gsmall)
    dpk, mpk, vpk = _adam_small(pk(w), pk(mom), pk(var), gpacked)
    shard_shapes = [w[n].shape for n in SMALL]
    out_g = dict(gsmall)
    out_d = dict(zip(SMALL, _unpack(dpk, shard_shapes)))
    out_m = dict(zip(SMALL, _unpack(mpk, shard_shapes)))
    out_v = dict(zip(SMALL, _unpack(vpk, shard_shapes)))

    def group_totals(g):
        n = len(scatters[g]) // 2
        return [_add4("grad_total", s, r) for s, r in zip(scatters[g][:n], scatters[g][n:])]

    late_keys = [k for g in range(1, 2 * depth) for k in group_keys(g)]
    late = [t for g in range(1, 2 * depth) for t in group_totals(g)]
    send, recv, late, _ = _split_start("grad_join_start", late, len(late), _join_plan(len(late)))
    _, s0, r0, thru0 = flight['pending']
    scatters[0] = _split_wait("grad_scatter_wait_0", s0, r0, thru0, _scatter_plan(len(thru0) // 2), dpk)
    first = _join_halves(group_totals(0))
    late = _split_wait("grad_join_wait", send, recv, late, _join_plan(len(late)), first[0])
    gfull = dict(zip(group_keys(0) + late_keys, list(first) + list(late)))
    stacked = {n: [lax.empty(w[n].shape, F32) for _ in range(4)] for n in big_names}
    for n, l in keys:
        stacked[n] = _adam_big("adam_" + n, w[n], mom[n], var[n], gfull[n, l], BIG[n], l, stacked[n])
    for n in big_names:
        out_g[n], out_d[n], out_m[n], out_v[n] = stacked[n]

    return (loss, grad_x, *[out_g[n] for n in W_NAMES], *[out_d[n] for n in W_NAMES],
            *[out_m[n] for n in W_NAMES], *[out_v[n] for n in W_NAMES])
```

```python
import functools
import itertools
import math

import jax
import jax.numpy as jnp
from jax import lax
from jax.experimental import pallas as pl
from jax.experimental.pallas import tpu as pltpu

F32, BF16 = jnp.float32, jnp.bfloat16
MESH = pl.DeviceIdType.MESH

CHUNK = 128
SEG = CHUNK + 4
SGU_GROUP = 128
SSM_GROUP = 16
SSM_STATE = 64
EPS = 1e-6
LANES = 128
GROUPS_PER_BLOCK = LANES // SSM_GROUP
STATE_BLOCKS = SSM_STATE // SSM_GROUP
VMEM_LIMIT = 52 * 1024 * 1024

ADAM_LR, ADAM_B1, ADAM_B2, ADAM_EPS, ADAM_WD, ADAM_STEP = 0.001, 0.9, 0.999, 1e-08, 0.01, 10

W_NAMES = ['norm_mix_g', 'norm_ffn_g', 'a_w_in', 'a_g_v', 'a_w_s', 'a_b_s', 'a_w_out', 'b_w_in', 'b_a_re', 'b_a_im',
           'b_log_dt', 'b_b_re', 'b_b_im', 'b_c_re', 'b_c_im', 'b_d', 'b_w_glu', 'f_w_up', 'f_conv_w', 'f_conv_b',
           'f_w_down', 'final_g']
BIG = {'a_w_in': 'col', 'a_w_out': 'row', 'b_w_in': 'row', 'b_w_glu': 'col', 'f_w_up': 'col', 'f_w_down': 'row'}
SMALL = [n for n in W_NAMES if n not in BIG]
CHIP_SHARDED_SMALL = {'b_d': 1, 'f_conv_w': 2}


def _tile(n, pref, align):
    t = min(n, pref)
    t -= t % align
    while t >= align:
        if n % t == 0:
            return t
        t -= align
    return n


def _params(*sem):
    return pltpu.CompilerParams(dimension_semantics=sem, vmem_limit_bytes=VMEM_LIMIT)


def _gelu(x):
    c = math.sqrt(2.0 / math.pi)
    return 0.5 * x * (1.0 + jnp.tanh(c * (x + 0.044715 * x * x * x)))


def _gelu_grad(x):
    c = math.sqrt(2.0 / math.pi)
    t = jnp.tanh(c * (x + 0.044715 * x * x * x))
    return 0.5 * (1.0 + t) + 0.5 * x * (1.0 - t * t) * c * (1.0 + 3.0 * 0.044715 * x * x)


def _half_shape(kind, shard_shape):
    _, r, c = shard_shape
    return (r // 2, c) if kind == 'col' else (r, c // 2)


def _full_dims(kind, shard_shape):
    _, r, c = shard_shape
    return (r, 4 * c) if kind == 'col' else (4 * r, c)


def _gspec(kind, kdim, ndim, tr, tc, rc):
    if kind == 'col':
        nr, nc = (kdim // 2) // tr, (ndim // 4) // tc

        def imap(*g):
            rb, cb = rc(*g)
            return (cb // nc, rb // nr, rb % nr, cb % nc)
    else:
        nr, nc = (kdim // 4) // tr, (ndim // 2) // tc

        def imap(*g):
            rb, cb = rc(*g)
            return (rb // nr, cb // nc, rb % nr, cb % nc)
    return pl.BlockSpec((None, None, tr, tc), imap)


def _act_spec(rows_blk, cols_blk, ncol_half, at):
    if ncol_half is None:
        return pl.BlockSpec((rows_blk, cols_blk), at)

    def imap(*g):
        rb, cb = at(*g)
        return (cb // ncol_half, rb, cb % ncol_half)
    return pl.BlockSpec((None, rows_blk, cols_blk), imap)


def _wtiles(kind, kdim, ndim):
    if kind == 'col':
        return _tile(kdim // 2, 1024, LANES), _tile(ndim // 4, 1408, LANES)
    return _tile(kdim // 4, 1408, LANES), _tile(ndim // 2, 1024, LANES)


_DIMS = {'nn': (((1,), (0,)), ((), ())), 'nt': (((1,), (1,)), ((), ())), 'tn': (((0,), (0,)), ((), ()))}


def _matmul(name, mode, a, b, grid, a_spec, b_spec, out_shape, out_spec, acc_shape, extras=(), extra_specs=(),
            epilogue=None, aliases=None, deps=()):
    nk = grid[2]
    dims = _DIMS[mode]
    n_epi = len(extras)
    extras, extra_specs = tuple(extras) + tuple(deps), tuple(extra_specs) + (ANY,) * len(deps)
    n_extra = len(extras)
    b_specs = b_spec if isinstance(b_spec, (list, tuple)) else [b_spec]
    nb = len(b_specs)

    def body(a_ref, *rest):
        b_refs, rest = rest[:nb], rest[nb:]
        extra_refs, o_ref = rest[:n_extra], rest[n_extra]
        width = a_ref.shape[1] // nb
        prod = None
        for p, b_ref in enumerate(b_refs):
            a_blk = a_ref[...] if nb == 1 else a_ref[:, p * width:(p + 1) * width]
            term = lax.dot_general(a_blk.astype(BF16), b_ref[...].astype(BF16), dims, preferred_element_type=F32)
            prod = term if prod is None else prod + term

        def finish(r):
            if epilogue is not None:
                r = epilogue(r, *[e[...] for e in extra_refs[:n_epi]])
            o_ref[...] = r.astype(o_ref.dtype)

        if nk == 1:
            finish(prod)
            return
        acc_ref = rest[n_extra + 1]
        kk = pl.program_id(2)

        @pl.when(kk == 0)
        def _():
            acc_ref[...] = prod

        @pl.when(kk > 0)
        def _():
            acc_ref[...] += prod

        @pl.when(kk == nk - 1)
        def _():
            finish(acc_ref[...])

    scratch = [pltpu.VMEM(acc_shape, F32)] if nk > 1 else []
    return pl.pallas_call(
        body, name=name, grid=grid, in_specs=[a_spec, *b_specs, *extra_specs], out_specs=out_spec, out_shape=out_shape,
        scratch_shapes=scratch, input_output_aliases=aliases or {},
        compiler_params=_params("parallel", "parallel", "arbitrary"))(a, *([b] * nb), *extras)


PART_BYTES = 17 * 1024 * 1024


def _shard_parts(tr, tc):
    for parts in (4, 2):
        if parts * tr * tc * 2 * 2 <= PART_BYTES:
            return parts
    return 1


def _mm_x_w(name, a, wg, kind, kdim, ndim, out_dtype, residual=None, split=False, deps=()):
    rows = a.shape[0]
    tk, tn = _wtiles(kind, kdim, ndim)
    tm = _tile(rows, 1024, 16)
    if kind == 'col':
        tk = kdim
        b_spec = [_gspec(kind, kdim, ndim, kdim // 2, tn, lambda i, j, k, p=p: (p, j)) for p in range(2)]
    else:
        parts = _shard_parts(tk, tn) if tk == kdim // 4 else 1
        b_spec = [_gspec(kind, kdim, ndim, tk, tn, lambda i, j, k, p=p: (k * parts + p, j)) for p in range(parts)]
        tk = parts * tk
    grid = (rows // tm, ndim // tn, kdim // tk)
    extras, especs, epi = (), (), None
    if residual is not None:
        extras, especs = (residual,), (pl.BlockSpec((tm, tn), lambda i, j, k: (i, j)),)
        epi = lambda r, res: r + res
    out_shape = (2, rows, ndim // 2) if split else (rows, ndim)
    return _matmul(name, 'nn', a, wg, grid, pl.BlockSpec((tm, tk), lambda i, j, k: (i, k)), b_spec,
                   jax.ShapeDtypeStruct(out_shape, out_dtype),
                   _act_spec(tm, tn, (ndim // 2) // tn if split else None, lambda i, j, k: (i, j)),
                   (tm, tn), extras, especs, epi, deps=deps)


def _mm_dy_wt(name, dy, wg, kind, kdim, ndim, out_dtype, split=False, deps=()):
    rows = dy.shape[-2]
    tn, tk = _wtiles(kind, kdim, ndim)
    tm = _tile(rows, 1024, 16)
    if kind == 'col':
        shard = ndim // 4
        parts = 1 if split else _shard_parts(tn, shard)
        b_spec = [_gspec(kind, kdim, ndim, tn, shard, lambda i, j, k, p=p: (j, k * parts + p)) for p in range(parts)]
        tk = parts * shard
    else:
        assert not split
        tk = ndim
        b_spec = [_gspec(kind, kdim, ndim, tn, ndim // 2, lambda i, j, k, p=p: (j, p)) for p in range(2)]
    grid = (rows // tm, kdim // tn, ndim // tk)
    return _matmul(name, 'nt', dy, wg, grid,
                   _act_spec(tm, tk, (ndim // 2) // tk if split else None, lambda i, j, k: (i, k)), b_spec,
                   jax.ShapeDtypeStruct((rows, kdim), out_dtype), pl.BlockSpec((tm, tn), lambda i, j, k: (i, j)),
                   (tm, tn), extras=tuple(deps), extra_specs=(ANY,) * len(deps))


def _mm_xt_dy(name, xa, dy, kind, kdim, ndim, split=False):
    rows = xa.shape[0]
    tm, tn = _wtiles(kind, kdim, ndim)
    tl = _tile(rows, 2048, 16)
    grid = (kdim // tm, ndim // tn, rows // tl)
    rdim, cdim = (kdim // 2, ndim // 4) if kind == 'col' else (kdim // 4, ndim // 2)
    return _matmul(name, 'tn', xa, dy, grid, pl.BlockSpec((tl, tm), lambda i, j, k: (k, i)),
                   _act_spec(tl, tn, (ndim // 2) // tn if split else None, lambda i, j, k: (k, j)),
                   jax.ShapeDtypeStruct((4, 2, rdim, cdim), BF16),
                   _gspec(kind, kdim, ndim, tm, tn, lambda i, j, k: (i, j)), (tm, tn))


def _rms_fwd(name, h, g):
    rows, d = h.shape
    tm = _tile(rows, 256, 16)

    def body(h_ref, g_ref, o_ref):
        x = h_ref[...]
        r = lax.rsqrt(jnp.mean(x * x, axis=-1, keepdims=True) + EPS)
        o_ref[...] = (x * r * g_ref[...]).astype(o_ref.dtype)

    return pl.pallas_call(
        body, name=name, grid=(rows // tm,),
        in_specs=[pl.BlockSpec((tm, d), lambda i: (i, 0)), pl.BlockSpec((1, d), lambda i: (0, 0))],
        out_specs=pl.BlockSpec((tm, d), lambda i: (i, 0)), out_shape=jax.ShapeDtypeStruct((rows, d), BF16),
        compiler_params=_params("parallel"))(h, g)


def _rms_bwd(name, h, g, dhn, dres):
    rows, d = h.shape
    tm = _tile(rows, 256, 16)

    def body(h_ref, g_ref, dy_ref, dres_ref, dh_ref, dg_ref):
        x = h_ref[...]
        r = lax.rsqrt(jnp.mean(x * x, axis=-1, keepdims=True) + EPS)
        xh = x * r
        dy = dy_ref[...].astype(F32)
        gy = dy * g_ref[...]
        dh_ref[...] = dres_ref[...] + r * (gy - xh * jnp.mean(gy * xh, axis=-1, keepdims=True))
        part = jnp.sum(dy * xh, axis=0, keepdims=True)

        @pl.when(pl.program_id(0) == 0)
        def _():
            dg_ref[...] = part

        @pl.when(pl.program_id(0) > 0)
        def _():
            dg_ref[...] += part

    row = pl.BlockSpec((tm, d), lambda i: (i, 0))
    vec = pl.BlockSpec((1, d), lambda i: (0, 0))
    return pl.pallas_call(
        body, name=name, grid=(rows // tm,), in_specs=[row, vec, row, row], out_specs=[row, vec],
        out_shape=[jax.ShapeDtypeStruct((rows, d), F32), jax.ShapeDtypeStruct((1, d), F32)],
        compiler_params=_params("arbitrary"))(h, g, dhn, dres)


def _loss_head(h, g, target):
    rows, d = h.shape
    tm = _tile(rows, 256, 16)

    def body(h_ref, g_ref, t_ref, dh_ref, dg_ref, loss_ref):
        x = h_ref[...]
        r = lax.rsqrt(jnp.mean(x * x, axis=-1, keepdims=True) + EPS)
        xh = x * r
        err = xh * g_ref[...] - t_ref[...]
        dy = err * (1.0 / d)
        gy = dy * g_ref[...]
        dh_ref[...] = r * (gy - xh * jnp.mean(gy * xh, axis=-1, keepdims=True))
        part = jnp.sum(dy * xh, axis=0, keepdims=True)
        sq = jnp.sum(err * err, axis=0, keepdims=True) * (0.5 / d)

        @pl.when(pl.program_id(0) == 0)
        def _():
            dg_ref[...] = part
            loss_ref[...] = sq

        @pl.when(pl.program_id(0) > 0)
        def _():
            dg_ref[...] += part
            loss_ref[...] += sq

    row = pl.BlockSpec((tm, d), lambda i: (i, 0))
    vec = pl.BlockSpec((1, d), lambda i: (0, 0))
    return pl.pallas_call(
        body, name="loss_head", grid=(rows // tm,), in_specs=[row, vec, row], out_specs=[row, vec, vec],
        out_shape=[jax.ShapeDtypeStruct((rows, d), F32), jax.ShapeDtypeStruct((1, d), F32),
                   jax.ShapeDtypeStruct((1, d), F32)],
        compiler_params=_params("arbitrary"))(h, g, target)


def _shift_down(cur, prev8, first, k):
    rows = cur.shape[0]
    rolled = pltpu.roll(cur, k, axis=0)
    idx = lax.broadcasted_iota(jnp.int32, cur.shape, 0)
    prev8 = jnp.where(first, 0.0, prev8)
    out = rolled
    for r in range(k):
        out = jnp.where(idx == r, prev8[8 - k + r:8 - k + r + 1, :], out)
    del rows
    return out


def _shift_up(cur, next8, last, k):
    rows = cur.shape[0]
    rolled = pltpu.roll(cur, rows - k, axis=0)
    idx = lax.broadcasted_iota(jnp.int32, cur.shape, 0)
    next8 = jnp.where(last, 0.0, next8)
    out = rolled
    for r in range(k):
        out = jnp.where(idx == rows - k + r, next8[r:r + 1, :], out)
    return out


def _conv_acc(z, zprev, first, w, b):
    z1 = _shift_down(z, zprev, first, 1)
    z2 = _shift_down(z, zprev, first, 2)
    return b + w[2:3, :] * z + w[1:2, :] * z1 + w[0:1, :] * z2, z1, z2


def _ffn_tiles(rows, f):
    return _tile(rows, 512, 16), _tile(f, 512, LANES)


def _ffn_act_fwd(name, z3, cw3, cb3):
    _, rows, f = z3.shape
    tm, tc = _ffn_tiles(rows, f)
    hb = tm // 8

    def body(z_ref, zp_ref, w_ref, b_ref, o_ref):
        first = pl.program_id(0) == 0
        gate, _, _ = _conv_acc(z_ref[0].astype(F32), zp_ref[0].astype(F32), first, w_ref[0], b_ref[0])
        val, _, _ = _conv_acc(z_ref[1].astype(F32), zp_ref[1].astype(F32), first, w_ref[1], b_ref[1])
        o_ref[...] = (gate * jax.nn.sigmoid(gate) * val).astype(o_ref.dtype)

    return pl.pallas_call(
        body, name=name, grid=(rows // tm, f // tc),
        in_specs=[pl.BlockSpec((2, tm, tc), lambda i, j: (0, i, j)),
                  pl.BlockSpec((2, 8, tc), lambda i, j: (0, jnp.maximum(i * hb - 1, 0), j)),
                  pl.BlockSpec((2, 3, tc), lambda i, j: (0, 0, j)), pl.BlockSpec((2, 1, tc), lambda i, j: (0, 0, j))],
        out_specs=pl.BlockSpec((tm, tc), lambda i, j: (i, j)), out_shape=jax.ShapeDtypeStruct((rows, f), BF16),
        compiler_params=_params("parallel", "parallel"))(z3, z3, cw3, cb3)


def _gate_grads(d_a, acc_g, acc_v):
    sig = jax.nn.sigmoid(acc_g)
    return d_a * acc_v * sig * (1.0 + acc_g * (1.0 - sig)), d_a * acc_g * sig


def _ffn_act_bwd(name, z3, da, cw3, cb3):
    _, rows, f = z3.shape
    tm, tc = _ffn_tiles(rows, f)
    hb = tm // 8
    nrow = rows // tm

    def body(z_ref, zp_ref, zn_ref, da_ref, dan_ref, w_ref, b_ref, dz_ref, dcw_ref, dcb_ref):
        i = pl.program_id(1)
        first, last = i == 0, i == nrow - 1
        w, b = (w_ref[0], w_ref[1]), (b_ref[0], b_ref[1])
        z = (z_ref[0].astype(F32), z_ref[1].astype(F32))
        acc, taps = [], []
        for hf in range(2):
            a_h, z1, z2 = _conv_acc(z[hf], zp_ref[hf].astype(F32), first, w[hf], b[hf])
            acc.append(a_h)
            taps.append((z2, z1, z[hf]))
        dacc = _gate_grads(da_ref[...].astype(F32), acc[0], acc[1])
        acc_n = [_conv_acc(zn_ref[hf].astype(F32), z[hf][tm - 8:tm, :], False, w[hf], b[hf])[0] for hf in range(2)]
        dacc_n = _gate_grads(dan_ref[...].astype(F32), acc_n[0], acc_n[1])
        for hf in range(2):
            d = dacc[hf]
            d1 = _shift_up(d, dacc_n[hf], last, 1)
            d2 = _shift_up(d, dacc_n[hf], last, 2)
            dz_ref[hf] = (w[hf][2:3, :] * d + w[hf][1:2, :] * d1 + w[hf][0:1, :] * d2).astype(dz_ref.dtype)
        sums_w = [[jnp.sum(dacc[hf] * t, axis=0, keepdims=True) for t in taps[hf]] for hf in range(2)]
        sums_b = [jnp.sum(dacc[hf], axis=0, keepdims=True) for hf in range(2)]

        @pl.when(first)
        def _():
            for hf in range(2):
                for k in range(3):
                    dcw_ref[hf, k:k + 1, :] = sums_w[hf][k]
                dcb_ref[hf] = sums_b[hf]

        @pl.when(i > 0)
        def _():
            for hf in range(2):
                for k in range(3):
                    dcw_ref[hf, k:k + 1, :] += sums_w[hf][k]
                dcb_ref[hf] += sums_b[hf]

    nxt = lambda i: jnp.minimum((i + 1) * hb, rows // 8 - 1)
    wsp = pl.BlockSpec((2, 3, tc), lambda j, i: (0, 0, j))
    bsp = pl.BlockSpec((2, 1, tc), lambda j, i: (0, 0, j))
    cur = pl.BlockSpec((2, tm, tc), lambda j, i: (0, i, j))
    return pl.pallas_call(
        body, name=name, grid=(f // tc, nrow),
        in_specs=[cur, pl.BlockSpec((2, 8, tc), lambda j, i: (0, jnp.maximum(i * hb - 1, 0), j)),
                  pl.BlockSpec((2, 8, tc), lambda j, i: (0, nxt(i), j)), pl.BlockSpec((tm, tc), lambda j, i: (i, j)),
                  pl.BlockSpec((8, tc), lambda j, i: (nxt(i), j)), wsp, bsp],
        out_specs=[cur, wsp, bsp],
        out_shape=[jax.ShapeDtypeStruct((2, rows, f), BF16), jax.ShapeDtypeStruct((2, 3, f), F32),
                   jax.ShapeDtypeStruct((2, 1, f), F32)],
        compiler_params=_params("parallel", "arbitrary"))(z3, z3, z3, da, da, cw3, cb3)


def _glu_fwd(name, gg, h):
    rows, d2 = gg.shape
    d = d2 // 2
    tm, tc = _tile(rows, 512, 16), _tile(d, 1024, LANES)
    nd = d // tc

    def body(a_ref, b_ref, h_ref, o_ref):
        o_ref[...] = h_ref[...] + a_ref[...].astype(F32) * jax.nn.sigmoid(b_ref[...].astype(F32))

    return pl.pallas_call(
        body, name=name, grid=(rows // tm, nd),
        in_specs=[pl.BlockSpec((tm, tc), lambda i, j: (i, j)), pl.BlockSpec((tm, tc), lambda i, j: (i, j + nd)),
                  pl.BlockSpec((tm, tc), lambda i, j: (i, j))],
        out_specs=pl.BlockSpec((tm, tc), lambda i, j: (i, j)), out_shape=jax.ShapeDtypeStruct((rows, d), F32),
        compiler_params=_params("parallel", "parallel"))(gg, gg, h)


def _glu_bwd(name, gg, dh):
    rows, d2 = gg.shape
    d = d2 // 2
    tm, tc = _tile(rows, 512, 16), _tile(d, 1024, LANES)
    nd = d // tc

    def body(s_ref, o_ref, dh_ref, out_ref):
        is_a = pl.program_id(1) < nd
        me = s_ref[...].astype(F32)
        other = o_ref[...].astype(F32)
        g = dh_ref[...]
        sig_o = jax.nn.sigmoid(other)
        sig_m = jax.nn.sigmoid(me)
        out_ref[...] = jnp.where(is_a, g * sig_o, g * other * sig_m * (1.0 - sig_m)).astype(out_ref.dtype)

    return pl.pallas_call(
        body, name=name, grid=(rows // tm, 2 * nd),
        in_specs=[pl.BlockSpec((tm, tc), lambda i, j: (i, j)),
                  pl.BlockSpec((tm, tc), lambda i, j: (i, (j + nd) % (2 * nd))),
                  pl.BlockSpec((tm, tc), lambda i, j: (i, j % nd))],
        out_specs=pl.BlockSpec((tm, tc), lambda i, j: (i, j)), out_shape=jax.ShapeDtypeStruct((rows, d2), BF16),
        compiler_params=_params("parallel", "parallel"))(gg, gg, dh)


def _sgu_common(pre_ref, gv_ref, e):
    u = _gelu(pre_ref[:, :e].astype(F32))
    v = _gelu(pre_ref[:, e:].astype(F32))
    r = lax.rsqrt(jnp.mean(v * v, axis=-1, keepdims=True) + EPS)
    vh = v * r
    return u, vh, r, (vh * gv_ref[...]).astype(BF16)


def _tril_bf16(ws_ref, hd):
    t = lax.broadcasted_iota(jnp.int32, (CHUNK, CHUNK), 0)
    s = lax.broadcasted_iota(jnp.int32, (CHUNK, CHUNK), 1)
    return jnp.where(s <= t, ws_ref[hd], 0.0).astype(BF16)


def _sgu_mix_fwd(name, pre, gv, ws, bsx):
    rows, e2 = pre.shape
    e = e2 // 2
    heads = e // SGU_GROUP
    tr = _tile(rows, 256, CHUNK)

    def body(pre_ref, gv_ref, ws_ref, bs_ref, o_ref):
        u, _, _, vn = _sgu_common(pre_ref, gv_ref, e)
        for hd in range(heads):
            wm = _tril_bf16(ws_ref, hd)
            cols = slice(hd * SGU_GROUP, (hd + 1) * SGU_GROUP)
            for ck in range(tr // CHUNK):
                rws = slice(ck * CHUNK, (ck + 1) * CHUNK)
                s = jnp.dot(wm, vn[rws, cols], preferred_element_type=F32) + bs_ref[hd]
                o_ref[rws, cols] = (u[rws, cols] * s).astype(o_ref.dtype)

    whole3 = pl.BlockSpec((heads, CHUNK, CHUNK), lambda i: (0, 0, 0))
    return pl.pallas_call(
        body, name=name, grid=(rows // tr,),
        in_specs=[pl.BlockSpec((tr, e2), lambda i: (i, 0)), pl.BlockSpec((1, e), lambda i: (0, 0)), whole3, whole3],
        out_specs=pl.BlockSpec((tr, e), lambda i: (i, 0)), out_shape=jax.ShapeDtypeStruct((rows, e), BF16),
        compiler_params=_params("parallel"))(pre, gv, ws, bsx)


def _sgu_mix_bwd(name, pre, dus, gv, ws, bsx):
    rows, e2 = pre.shape
    e = e2 // 2
    heads = e // SGU_GROUP
    tr = _tile(rows, 256, CHUNK)

    def body(pre_ref, dus_ref, gv_ref, ws_ref, bs_ref, dpre_ref, dws_ref, dbs_ref, dgv_ref, dvn_ref, du_ref):
        first = pl.program_id(0) == 0
        u, vh, r, vn = _sgu_common(pre_ref, gv_ref, e)
        ones = jnp.ones((SGU_GROUP, LANES), BF16)
        tt = lax.broadcasted_iota(jnp.int32, (CHUNK, CHUNK), 0)
        ss = lax.broadcasted_iota(jnp.int32, (CHUNK, CHUNK), 1)
        for hd in range(heads):
            wm = _tril_bf16(ws_ref, hd)
            cols = slice(hd * SGU_GROUP, (hd + 1) * SGU_GROUP)
            dw = jnp.zeros((CHUNK, CHUNK), F32)
            db = jnp.zeros((CHUNK, LANES), F32)
            for ck in range(tr // CHUNK):
                rws = slice(ck * CHUNK, (ck + 1) * CHUNK)
                vblk = vn[rws, cols]
                s = jnp.dot(wm, vblk, preferred_element_type=F32) + bs_ref[hd]
                d_us = dus_ref[rws, cols].astype(F32)
                du_ref[rws, cols] = d_us * s
                ds = (d_us * u[rws, cols]).astype(BF16)
                dvn_ref[rws, cols] = lax.dot_general(wm, ds, _DIMS['tn'], preferred_element_type=F32)
                dw = dw + lax.dot_general(ds, vblk, _DIMS['nt'], preferred_element_type=F32)
                db = db + jnp.dot(ds, ones, preferred_element_type=F32)
            dw = jnp.where(ss <= tt, dw, 0.0)

            @pl.when(first)
            def _():
                dws_ref[hd] = dw
                dbs_ref[hd] = db

            @pl.when(jnp.logical_not(first))
            def _():
                dws_ref[hd] += dw
                dbs_ref[hd] += db

        dvn = dvn_ref[...]
        part = jnp.sum(dvn * vh, axis=0, keepdims=True)

        @pl.when(first)
        def _():
            dgv_ref[...] = part

        @pl.when(jnp.logical_not(first))
        def _():
            dgv_ref[...] += part

        gy = dvn * gv_ref[...]
        dv = r * (gy - vh * jnp.mean(gy * vh, axis=-1, keepdims=True))
        dpre_ref[:, :e] = (du_ref[...] * _gelu_grad(pre_ref[:, :e].astype(F32))).astype(dpre_ref.dtype)
        dpre_ref[:, e:] = (dv * _gelu_grad(pre_ref[:, e:].astype(F32))).astype(dpre_ref.dtype)

    whole3 = pl.BlockSpec((heads, CHUNK, CHUNK), lambda i: (0, 0, 0))
    vec = pl.BlockSpec((1, e), lambda i: (0, 0))
    return pl.pallas_call(
        body, name=name, grid=(rows // tr,),
        in_specs=[pl.BlockSpec((tr, e2), lambda i: (i, 0)), pl.BlockSpec((tr, e), lambda i: (i, 0)), vec, whole3, whole3],
        out_specs=[pl.BlockSpec((tr, e2), lambda i: (i, 0)), whole3, whole3, vec],
        out_shape=[jax.ShapeDtypeStruct((rows, e2), BF16), jax.ShapeDtypeStruct((heads, CHUNK, CHUNK), F32),
                   jax.ShapeDtypeStruct((heads, CHUNK, LANES), F32), jax.ShapeDtypeStruct((1, e), F32)],
        scratch_shapes=[pltpu.VMEM((tr, e), F32), pltpu.VMEM((tr, e), F32)],
        compiler_params=_params("arbitrary"))(pre, dus, gv, ws, bsx)


def _disc_a(lr, li, ldt):
    dt = jnp.exp(ldt)
    mag = jnp.exp(dt * lr)
    return mag * jnp.cos(dt * li), mag * jnp.sin(dt * li)


def _disc_b(lr, li, ldt, br, bi):
    ar, ai = _disc_a(lr, li, ldt)
    den = lr * lr + li * li
    qr = ((ar - 1.0) * lr + ai * li) / den
    qi = (ai * lr - (ar - 1.0) * li) / den
    return qr * br - qi * bi, qr * bi + qi * br


def _s5_disc(name, lr, li, ldt, lrx, lix, ldtx, br, bi):
    def body(lr_ref, li_ref, ldt_ref, lrx_ref, lix_ref, ldtx_ref, br_ref, bi_ref, ar_ref, ai_ref, bbr_ref, bbi_ref):
        ar_ref[...], ai_ref[...] = _disc_a(lr_ref[...], li_ref[...], ldt_ref[...])
        bbr_ref[...], bbi_ref[...] = _disc_b(lrx_ref[...], lix_ref[...], ldtx_ref[...], br_ref[...], bi_ref[...])

    small = jax.ShapeDtypeStruct(lr.shape, F32)
    wide = jax.ShapeDtypeStruct(br.shape, F32)
    return pl.pallas_call(body, name=name, out_shape=[small, small, wide, wide],
                          compiler_params=pltpu.CompilerParams(vmem_limit_bytes=VMEM_LIMIT))(
        lr, li, ldt, lrx, lix, ldtx, br, bi)


def _s5_disc_bwd(name, lr, li, ldt, lrx, lix, ldtx, br, bi, dar, dai, dbbr, dbbi, sel):
    def body(lr_ref, li_ref, ldt_ref, lrx_ref, lix_ref, ldtx_ref, br_ref, bi_ref, dar_ref, dai_ref, dbbr_ref,
             dbbi_ref, sel_ref, dlr_ref, dli_ref, dldt_ref, dbr_ref, dbi_ref):
        _, vjp_a = jax.vjp(_disc_a, lr_ref[...], li_ref[...], ldt_ref[...])
        g_lr, g_li, g_ldt = vjp_a((dar_ref[...], dai_ref[...]))
        _, vjp_b = jax.vjp(_disc_b, lrx_ref[...], lix_ref[...], ldtx_ref[...], br_ref[...], bi_ref[...])
        x_lr, x_li, x_ldt, g_br, g_bi = vjp_b((dbbr_ref[...], dbbi_ref[...]))
        fold = lambda t: jnp.dot(t, sel_ref[...], precision=lax.Precision.HIGHEST, preferred_element_type=F32)
        dlr_ref[...] = g_lr + fold(x_lr)
        dli_ref[...] = g_li + fold(x_li)
        dldt_ref[...] = jnp.sum(g_ldt + fold(x_ldt), axis=1, keepdims=True)
        dbr_ref[...] = g_br
        dbi_ref[...] = g_bi

    small = jax.ShapeDtypeStruct(lr.shape, F32)
    wide = jax.ShapeDtypeStruct(br.shape, F32)
    return pl.pallas_call(body, name=name,
                          out_shape=[small, small, jax.ShapeDtypeStruct((lr.shape[0], 1), F32), wide, wide],
                          compiler_params=pltpu.CompilerParams(vmem_limit_bytes=VMEM_LIMIT))(
        lr, li, ldt, lrx, lix, ldtx, br, bi, dar, dai, dbbr, dbbi, sel)


def _cmul(ar, ai, br, bi):
    return ar * br - ai * bi, ar * bi + ai * br


def _pow_seg(ar, ai):
    res, base, n = None, (ar, ai), SEG
    while n:
        if n & 1:
            res = base if res is None else _cmul(*res, *base)
        n >>= 1
        if n:
            base = _cmul(*base, *base)
    return res


def _scan_forward(hr_ref, hi_ref, er_ref, ei_ref, sr_ref, si_ref, ar, ai, nck):
    arb, aib = jnp.broadcast_to(ar, (nck, LANES)), jnp.broadcast_to(ai, (nck, LANES))

    def intra(t, carry):
        sr, si = carry
        slab = pl.ds(t, nck, stride=SEG)
        nr = arb * sr - aib * si + hr_ref[slab, :]
        ni = arb * si + aib * sr + hi_ref[slab, :]
        hr_ref[slab, :] = nr
        hi_ref[slab, :] = ni
        return nr, ni

    zero = jnp.zeros((nck, LANES), F32)
    er_ref[...], ei_ref[...] = lax.fori_loop(0, SEG, intra, (zero, zero), unroll=4)
    pcr, pci = _pow_seg(ar, ai)
    sr_ref[0:1, :] = jnp.zeros((1, LANES), F32)
    si_ref[0:1, :] = jnp.zeros((1, LANES), F32)
    for ck in range(nck - 1):
        pr, pi = sr_ref[ck:ck + 1, :], si_ref[ck:ck + 1, :]
        sr_ref[ck + 1:ck + 2, :] = pcr * pr - pci * pi + er_ref[ck:ck + 1, :]
        si_ref[ck + 1:ck + 2, :] = pcr * pi + pci * pr + ei_ref[ck:ck + 1, :]
    s_r, s_i = sr_ref[...], si_ref[...]

    def fix(t, carry):
        pr, pi = carry
        slab = pl.ds(t, nck, stride=SEG)
        hr_ref[slab, :] = hr_ref[slab, :] + (pr * s_r - pi * s_i)
        hi_ref[slab, :] = hi_ref[slab, :] + (pr * s_i + pi * s_r)
        return _cmul(pr, pi, arb, aib)

    lax.fori_loop(0, SEG, fix, (arb, aib), unroll=4)


def _scan_backward(gr_ref, gi_ref, hr_ref, hi_ref, er_ref, ei_ref, sr_ref, si_ref, ar, ai, nck):
    arb, aib = jnp.broadcast_to(ar, (nck, LANES)), jnp.broadcast_to(-ai, (nck, LANES))

    def intra(k, carry):
        sr, si = carry
        slab = pl.ds(SEG - 1 - k, nck, stride=SEG)
        nr = arb * sr - aib * si + gr_ref[slab, :]
        ni = arb * si + aib * sr + gi_ref[slab, :]
        gr_ref[slab, :] = nr
        gi_ref[slab, :] = ni
        return nr, ni

    zero = jnp.zeros((nck, LANES), F32)
    er_ref[...], ei_ref[...] = lax.fori_loop(0, SEG, intra, (zero, zero), unroll=4)
    pcr, pci = _pow_seg(ar, -ai)
    sr_ref[nck - 1:nck, :] = jnp.zeros((1, LANES), F32)
    si_ref[nck - 1:nck, :] = jnp.zeros((1, LANES), F32)
    for ck in range(nck - 1, 0, -1):
        pr, pi = sr_ref[ck:ck + 1, :], si_ref[ck:ck + 1, :]
        sr_ref[ck - 1:ck, :] = pcr * pr - pci * pi + er_ref[ck:ck + 1, :]
        si_ref[ck - 1:ck, :] = pcr * pi + pci * pr + ei_ref[ck:ck + 1, :]
    s_r, s_i = sr_ref[...], si_ref[...]
    last = pl.ds(SEG - 1, nck, stride=SEG)
    row = lax.broadcasted_iota(jnp.int32, (nck, LANES), 0)
    hp_r = jnp.where(row == 0, 0.0, pltpu.roll(hr_ref[last, :], 1, axis=0)) if nck > 1 else zero
    hp_i = jnp.where(row == 0, 0.0, pltpu.roll(hi_ref[last, :], 1, axis=0)) if nck > 1 else zero

    def settle(t, pr, pi, h_r, h_i):
        slab = pl.ds(t, nck, stride=SEG)
        g_r = gr_ref[slab, :] + (pr * s_r - pi * s_i)
        g_i = gi_ref[slab, :] + (pr * s_i + pi * s_r)
        gr_ref[slab, :] = g_r
        gi_ref[slab, :] = g_i
        return g_r * h_r + g_i * h_i, g_i * h_r - g_r * h_i

    def fix(k, carry):
        pr, pi, acr, aci = carry
        t = SEG - 1 - k
        prev = pl.ds(t - 1, nck, stride=SEG)
        d_r, d_i = settle(t, pr, pi, hr_ref[prev, :], hi_ref[prev, :])
        nr, ni = _cmul(pr, pi, arb, aib)
        return nr, ni, acr + d_r, aci + d_i

    pr, pi, acr, aci = lax.fori_loop(0, SEG - 1, fix, (arb, aib, zero, zero), unroll=4)
    d_r, d_i = settle(0, pr, pi, hp_r, hp_i)
    return jnp.sum(acr + d_r, axis=0, keepdims=True), jnp.sum(aci + d_i, axis=0, keepdims=True)


def _s5_fill_states(u_ref, br_ref, bi_ref, hr_ref, hi_ref, rows):
    ub = u_ref[...].astype(BF16)
    hr_ref[0:rows, :] = jnp.dot(ub, br_ref[...], preferred_element_type=F32)
    hi_ref[0:rows, :] = jnp.dot(ub, bi_ref[...], preferred_element_type=F32)
    pad = jnp.zeros((hr_ref.shape[0] - rows, LANES), F32)
    hr_ref[rows:, :] = pad
    hi_ref[rows:, :] = pad


def _s5_specs(rows, e):
    sb = STATE_BLOCKS
    chan = pl.BlockSpec((rows, LANES), lambda j: (0, j // sb))
    bmat = pl.BlockSpec((None, LANES, LANES), lambda j: (j // sb, 0, j % sb))
    cmat = pl.BlockSpec((None, LANES, LANES), lambda j: (j // sb, j % sb, 0))
    avec = pl.BlockSpec((1, LANES), lambda j: (0, j))
    dvec = pl.BlockSpec((1, LANES), lambda j: (0, j // sb))
    return chan, bmat, cmat, avec, dvec


def _s5_fwd(name, u, bre, bim, crt, cit, ar, ai, dd):
    rows, e = u.shape
    nck = rows // CHUNK
    nsteps = (e // LANES) * STATE_BLOCKS
    chan, bmat, cmat, avec, dvec = _s5_specs(rows, e)

    def body(u_ref, br_ref, bi_ref, cr_ref, ci_ref, ar_ref, ai_ref, dd_ref, y_ref, gy_ref,
             hr_ref, hi_ref, er_ref, ei_ref, sr_ref, si_ref, acc_ref):
        j = pl.program_id(0) % STATE_BLOCKS
        _s5_fill_states(u_ref, br_ref, bi_ref, hr_ref, hi_ref, rows)
        _scan_forward(hr_ref, hi_ref, er_ref, ei_ref, sr_ref, si_ref, ar_ref[...], ai_ref[...], nck)
        contrib = (jnp.dot(hr_ref[0:rows, :].astype(BF16), cr_ref[...], preferred_element_type=F32)
                   - jnp.dot(hi_ref[0:rows, :].astype(BF16), ci_ref[...], preferred_element_type=F32))

        @pl.when(j == 0)
        def _():
            acc_ref[...] = dd_ref[...] * u_ref[...] + contrib

        @pl.when(j > 0)
        def _():
            acc_ref[...] += contrib

        @pl.when(j == STATE_BLOCKS - 1)
        def _():
            y = acc_ref[...]
            y_ref[...] = y.astype(y_ref.dtype)
            gy_ref[...] = _gelu(y).astype(gy_ref.dtype)

    flat = pltpu.VMEM((rows, LANES), F32)
    big = pltpu.VMEM((nck * SEG, LANES), F32)
    small = pltpu.VMEM((nck, LANES), F32)
    out = jax.ShapeDtypeStruct((rows, e), BF16)
    return pl.pallas_call(
        body, name=name, grid=(nsteps,), in_specs=[chan, bmat, bmat, cmat, cmat, avec, avec, dvec],
        out_specs=[chan, chan], out_shape=[out, out], scratch_shapes=[big, big, small, small, small, small, flat],
        compiler_params=_params("arbitrary"))(u, bre, bim, crt, cit, ar, ai, dd)


def _s5_bwd(name, u, y, dgy, bre, bim, crt, cit, ar, ai, dd):
    rows, e = u.shape
    nb = e // LANES
    nck = rows // CHUNK
    nsteps = nb * STATE_BLOCKS
    chan, bmat, cmat, avec, dvec = _s5_specs(rows, e)

    def body(u_ref, y_ref, dgy_ref, br_ref, bi_ref, cr_ref, ci_ref, ar_ref, ai_ref, dd_ref,
             du_ref, dbr_ref, dbi_ref, dcr_ref, dci_ref, dar_ref, dai_ref, ddd_ref,
             hr_ref, hi_ref, gr_ref, gi_ref, er_ref, ei_ref, sr_ref, si_ref, acc_ref, dy_ref):
        j = pl.program_id(0) % STATE_BLOCKS
        _s5_fill_states(u_ref, br_ref, bi_ref, hr_ref, hi_ref, rows)
        _scan_forward(hr_ref, hi_ref, er_ref, ei_ref, sr_ref, si_ref, ar_ref[...], ai_ref[...], nck)

        @pl.when(j == 0)
        def _():
            dy0 = dgy_ref[...].astype(F32) * _gelu_grad(y_ref[...].astype(F32))
            dy_ref[...] = dy0
            ddd_ref[...] = jnp.sum(dy0 * u_ref[...], axis=0, keepdims=True)

        dyb = dy_ref[...].astype(BF16)
        pad = jnp.zeros((gr_ref.shape[0] - rows, LANES), F32)
        gr_ref[0:rows, :] = lax.dot_general(dyb, cr_ref[...], _DIMS['nt'], preferred_element_type=F32)
        gi_ref[0:rows, :] = -lax.dot_general(dyb, ci_ref[...], _DIMS['nt'], preferred_element_type=F32)
        gr_ref[rows:, :] = pad
        gi_ref[rows:, :] = pad
        dcr_ref[...] = lax.dot_general(hr_ref[0:rows, :].astype(BF16), dyb, _DIMS['tn'], preferred_element_type=F32)
        dci_ref[...] = -lax.dot_general(hi_ref[0:rows, :].astype(BF16), dyb, _DIMS['tn'], preferred_element_type=F32)
        dar_ref[...], dai_ref[...] = _scan_backward(gr_ref, gi_ref, hr_ref, hi_ref, er_ref, ei_ref, sr_ref, si_ref,
                                                    ar_ref[...], ai_ref[...], nck)
        ub = u_ref[...].astype(BF16)
        grb, gib = gr_ref[0:rows, :].astype(BF16), gi_ref[0:rows, :].astype(BF16)
        dbr_ref[...] = lax.dot_general(ub, grb, _DIMS['tn'], preferred_element_type=F32)
        dbi_ref[...] = lax.dot_general(ub, gib, _DIMS['tn'], preferred_element_type=F32)
        contrib = (lax.dot_general(grb, br_ref[...], _DIMS['nt'], preferred_element_type=F32)
                   + lax.dot_general(gib, bi_ref[...], _DIMS['nt'], preferred_element_type=F32))

        @pl.when(j == 0)
        def _():
            acc_ref[...] = dd_ref[...] * dy_ref[...] + contrib

        @pl.when(j > 0)
        def _():
            acc_ref[...] += contrib

        @pl.when(j == STATE_BLOCKS - 1)
        def _():
            du_ref[...] = acc_ref[...]

    flat = pltpu.VMEM((rows, LANES), F32)
    big = pltpu.VMEM((nck * SEG, LANES), F32)
    small = pltpu.VMEM((nck, LANES), F32)
    bshape = jax.ShapeDtypeStruct((nb, LANES, LANES * STATE_BLOCKS), F32)
    cshape = jax.ShapeDtypeStruct((nb, LANES * STATE_BLOCKS, LANES), F32)
    ashape = jax.ShapeDtypeStruct((1, nb * LANES * STATE_BLOCKS), F32)
    return pl.pallas_call(
        body, name=name, grid=(nsteps,),
        in_specs=[chan, chan, chan, bmat, bmat, cmat, cmat, avec, avec, dvec],
        out_specs=[chan, bmat, bmat, cmat, cmat, avec, avec, dvec],
        out_shape=[jax.ShapeDtypeStruct((rows, e), F32), bshape, bshape, cshape, cshape, ashape, ashape,
                   jax.ShapeDtypeStruct((1, e), F32)],
        scratch_shapes=[big, big, big, big, small, small, small, small, flat, flat],
        compiler_params=_params("arbitrary"))(u, y, dgy, bre, bim, crt, cit, ar, ai, dd)


def _to_blockdiag_b(bbar, nb):
    eye = jnp.eye(GROUPS_PER_BLOCK, dtype=bbar.dtype)
    t = jnp.einsum('bgpc,gh->bgchp', bbar.reshape(nb, GROUPS_PER_BLOCK, SSM_STATE, SSM_GROUP), eye)
    return t.reshape(nb, LANES, GROUPS_PER_BLOCK * SSM_STATE)


def _from_blockdiag_b(dmat, nb):
    eye = jnp.eye(GROUPS_PER_BLOCK, dtype=dmat.dtype)
    t = dmat.reshape(nb, GROUPS_PER_BLOCK, SSM_GROUP, GROUPS_PER_BLOCK, SSM_STATE)
    return jnp.einsum('bgchp,gh->bgpc', t, eye).reshape(nb * GROUPS_PER_BLOCK, SSM_STATE, SSM_GROUP)


def _to_blockdiag_ct(c, nb):
    eye = jnp.eye(GROUPS_PER_BLOCK, dtype=c.dtype)
    t = jnp.einsum('bgop,gh->bgpho', c.reshape(nb, GROUPS_PER_BLOCK, SSM_GROUP, SSM_STATE), eye)
    return t.reshape(nb, GROUPS_PER_BLOCK * SSM_STATE, LANES)


def _from_blockdiag_ct(dmat, nb):
    eye = jnp.eye(GROUPS_PER_BLOCK, dtype=dmat.dtype)
    t = dmat.reshape(nb, GROUPS_PER_BLOCK, SSM_STATE, GROUPS_PER_BLOCK, SSM_GROUP)
    return jnp.einsum('bgpho,gh->bgop', t, eye).reshape(nb * GROUPS_PER_BLOCK, SSM_GROUP, SSM_STATE)


ANY = pl.BlockSpec(memory_space=pl.ANY)


def _half_specs(kind, rdim, cdim, tr, tc, layer):
    nr, nc = rdim // tr, cdim // tc
    if kind == 'col':
        nat = pl.BlockSpec((None, tr, tc), lambda c, rb, cb: (layer, c * nr + rb, cb))
    else:
        nat = pl.BlockSpec((None, tr, tc), lambda c, rb, cb: (layer, rb, c * nc + cb))
    half = pl.BlockSpec((None, tr, tc), lambda c, rb, cb: (c, rb, cb))
    return nat, half


def _my_chip():
    return 2 * lax.axis_index("x") + lax.axis_index("y")


def _cast_halves(name, w, kind, layer):
    rdim, cdim = _half_shape(kind, w.shape)
    tr, tc = _tile(rdim, 512, 16), _tile(cdim, 1408, LANES)
    nat, _ = _half_specs(kind, rdim, cdim, tr, tc, layer)
    slot = pl.BlockSpec((None, None, tr, tc), lambda c, rb, cb: (_my_chip(), c, rb, cb))

    def body(w_ref, o_ref):
        o_ref[...] = w_ref[...].astype(o_ref.dtype)

    return pl.pallas_call(
        body, name=name, grid=(2, rdim // tr, cdim // tc), in_specs=[nat], out_specs=slot,
        out_shape=jax.ShapeDtypeStruct((4, 2, rdim, cdim), BF16),
        compiler_params=_params("parallel", "parallel", "parallel"))(w)


def _adam_math(w, g, m, v):
    m = ADAM_B1 * m + (1.0 - ADAM_B1) * g
    v = ADAM_B2 * v + (1.0 - ADAM_B2) * (g * g)
    m_hat = m / (1.0 - ADAM_B1 ** ADAM_STEP)
    v_hat = v / (1.0 - ADAM_B2 ** ADAM_STEP)
    delta = -ADAM_LR * (m_hat / (jnp.sqrt(v_hat) + ADAM_EPS) + ADAM_WD * w)
    return delta, m, v


def _adam_big(name, w, m, v, gfull, kind, layer, outs):
    rdim, cdim = _half_shape(kind, w.shape)
    tr, tc = _tile(rdim, 256, 8), _tile(cdim, 1408, LANES)
    nat, half = _half_specs(kind, rdim, cdim, tr, tc, layer)

    def body(w_ref, m_ref, v_ref, g_ref, *rest):
        go_ref, d_ref, mo_ref, vo_ref = rest[4:]
        g = g_ref[...]
        go_ref[...] = g
        d_ref[...], mo_ref[...], vo_ref[...] = _adam_math(w_ref[...], g, m_ref[...], v_ref[...])

    shape = jax.ShapeDtypeStruct(w.shape, F32)
    return pl.pallas_call(
        body, name=name, grid=(2, rdim // tr, cdim // tc), in_specs=[nat, nat, nat, half] + [ANY] * 4,
        out_specs=[nat, nat, nat, nat], out_shape=[shape, shape, shape, shape],
        input_output_aliases={4: 0, 5: 1, 6: 2, 7: 3},
        compiler_params=_params("parallel", "parallel", "parallel"))(w, m, v, gfull, *outs)


def _adam_small(w, m, v, g):
    rows = w.shape[0]
    tr = _tile(rows, 512, 8)
    spec = pl.BlockSpec((tr, LANES), lambda i: (i, 0))

    def body(w_ref, m_ref, v_ref, g_ref, d_ref, mo_ref, vo_ref):
        d_ref[...], mo_ref[...], vo_ref[...] = _adam_math(w_ref[...], g_ref[...], m_ref[...], v_ref[...])

    shape = jax.ShapeDtypeStruct(w.shape, F32)
    return pl.pallas_call(body, name="adam_small", grid=(rows // tr,), in_specs=[spec] * 4, out_specs=[spec] * 3,
                          out_shape=[shape] * 3, compiler_params=_params("parallel"))(w, m, v, g)


def _add2(name, part, got):
    cdim = part.shape[-1]
    a2, b2 = part.reshape(4, 2, -1, cdim), got.reshape(4, -1, cdim)
    rows = b2.shape[1]
    tr, tc = _tile(rows, 512, 16), _tile(cdim, 1408, LANES)
    mine = pl.BlockSpec((None, None, tr, tc), lambda k, i, j: (k, lax.axis_index("c"), i, j))
    spec = pl.BlockSpec((None, tr, tc), lambda k, i, j: (k, i, j))

    def body(a_ref, b_ref, o_ref):
        o_ref[...] = (a_ref[...].astype(F32) + b_ref[...].astype(F32)).astype(o_ref.dtype)

    out = pl.pallas_call(
        body, name=name, grid=(4, rows // tr, cdim // tc), in_specs=[mine, spec], out_specs=spec,
        out_shape=jax.ShapeDtypeStruct(b2.shape, BF16),
        compiler_params=_params("parallel", "parallel", "parallel"))(a2, b2)
    return out.reshape(got.shape)


def _add4(name, sums, recv):
    cdim = sums.shape[-1]
    s2 = sums.reshape(4, -1, cdim)
    r3 = recv.reshape(3, -1, cdim)
    rows = s2.shape[1]
    tr, tc = _tile(rows, 512, 16), _tile(cdim, 1408, LANES)
    own = pl.BlockSpec((None, tr, tc), lambda i, j: (_my_chip(), i, j))
    rspec = lambda k: pl.BlockSpec((None, tr, tc), lambda i, j: (k, i, j))
    slot = pl.BlockSpec((None, tr, tc), lambda i, j: (lax.axis_index("c"), i, j))

    def body(o_ref, x_ref, y_ref, d_ref, out_ref):
        out_ref[...] = ((o_ref[...].astype(F32) + d_ref[...].astype(F32))
                        + (x_ref[...].astype(F32) + y_ref[...].astype(F32)))

    out = pl.pallas_call(
        body, name=name, grid=(rows // tr, cdim // tc), in_specs=[own, rspec(0), rspec(1), rspec(2)], out_specs=slot,
        out_shape=jax.ShapeDtypeStruct((2, rows, cdim), F32),
        compiler_params=_params("parallel", "parallel"))(s2, r3, r3, r3)
    return out.reshape(2, *sums.shape[1:])


def _place():
    x, y, c = lax.axis_index("x"), lax.axis_index("y"), lax.axis_index("c")
    chips = [(1 - x, y), (x, 1 - y), (1 - x, 1 - y)]
    return x, y, c, chips


def _remote(src, dst, send, recv, to):
    return pltpu.make_async_remote_copy(src_ref=src, dst_ref=dst, send_sem=send, recv_sem=recv, device_id=to,
                                        device_id_type=MESH)


def _pieces(src, dst, bands):
    lead, rows = src.shape[:-2], src.shape[-2]
    band = rows // bands
    out = []
    for idx in itertools.product(*[range(dim) for dim in lead]):
        for q in range(bands):
            sl = (*idx, pl.ds(q * band, band))
            out.append((src.at[sl], dst.at[sl]))
    return out


HBM = pl.BlockSpec(memory_space=pltpu.HBM)
SEM = pl.BlockSpec(memory_space=pltpu.SEMAPHORE)
EFFECT = pltpu.SideEffectType.DATAFLOW_SIDE_EFFECTING


def _split_start(name, bufs, ncopy, plan, deps=()):
    nb, nd = len(bufs), len(deps)

    def body(*refs):
        ins, send, recv, token = refs[:nb], refs[nb + nd], refs[nb + nd + 1], refs[2 * nb + nd + 2]
        for k, (src, dst, _, to, bands) in enumerate(plan(ins)):
            for s, d in _pieces(src, dst, bands):
                _remote(s, d, send.at[k], recv.at[k], to).start()
        token[...] = jnp.zeros_like(token)

    res = pl.pallas_call(
        body, name=name, in_specs=[HBM] * nb + [ANY] * nd,
        out_specs=[SEM, SEM] + [HBM] * nb + [pl.BlockSpec(memory_space=pltpu.VMEM)],
        out_shape=[pltpu.SemaphoreType.DMA((ncopy,)), pltpu.SemaphoreType.DMA((ncopy,))]
        + [pltpu.HBM(b.shape, b.dtype) for b in bufs] + [jax.ShapeDtypeStruct((8, LANES), F32)],
        input_output_aliases={a: a + 2 for a in range(nb)},
        compiler_params=pltpu.CompilerParams(has_side_effects=EFFECT))(
        *[pltpu.with_memory_space_constraint(b, pltpu.HBM) for b in bufs], *deps)
    return res[0], res[1], list(res[2:2 + nb]), res[2 + nb]


def _split_wait(name, send, recv, bufs, plan, after):
    nb = len(bufs)

    def body(*refs):
        ins, send_sem, recv_sem = refs[:nb], refs[nb], refs[nb + 1]
        for k, (src, dst, landing, to, _) in enumerate(plan(ins)):
            _remote(src, dst, send_sem.at[k], recv_sem.at[k], to).wait_send()
            _remote(src, landing, send_sem.at[k], recv_sem.at[k], to).wait_recv()

    return pl.pallas_call(
        body, name=name, in_specs=[HBM] * nb + [SEM, SEM, ANY], out_specs=[HBM] * nb,
        out_shape=[pltpu.HBM(b.shape, b.dtype) for b in bufs], input_output_aliases={a: a for a in range(nb)},
        compiler_params=pltpu.CompilerParams(has_side_effects=EFFECT))(*bufs, send, recv, after)


def _gather_plan(n):
    def plan(refs):
        x, y, c, chips = _place()
        kme = 2 * x + y
        return [(refs[a].at[kme, c], refs[a].at[kme, c], refs[a].at[2 * chip[0] + chip[1], c], (*chip, c), 2)
                for a in range(n) for chip in chips]
    return plan


def _scatter_plan(n):
    def plan(refs):
        x, y, c, chips = _place()
        return [(refs[a].at[2 * chip[0] + chip[1]], refs[n + a].at[r], refs[n + a].at[r], (*chip, c), 2)
                for a in range(n) for r, chip in enumerate(chips)]
    return plan


def _forward_plan(n):
    def plan(refs):
        x, y, c, chips = _place()
        sib = (x, y, 1 - c)
        return [(refs[a].at[2 * chip[0] + chip[1], c], refs[a].at[2 * chip[0] + chip[1], c],
                 refs[a].at[2 * chip[0] + chip[1], 1 - c], sib, 2) for a in range(n) for chip in chips]
    return plan


def _join_plan(n):
    def plan(refs):
        x, y, c, _ = _place()
        return [(refs[a].at[c], refs[a].at[c], refs[a].at[1 - c], (x, y, 1 - c), 4) for a in range(n)]
    return plan


def _allgather_small(shards):
    n = len(shards)

    def body(*refs):
        ins, outs = refs[:n], refs[n:2 * n]
        send, recv, loc = refs[2 * n:]
        x, y, c, chips = _place()
        kme = 2 * x + y
        local = [pltpu.make_async_copy(ins[a], outs[a].at[kme], loc.at[a]) for a in range(n)]
        for cp in local:
            cp.start()
        cps = [_remote(ins[a], outs[a].at[kme], send.at[3 * a + r], recv.at[3 * a + r], (*chip, c))
               for a in range(n) for r, chip in enumerate(chips)]
        for cp in cps:
            cp.start()
        for a in range(n):
            for r, chip in enumerate(chips):
                kp = 2 * chip[0] + chip[1]
                _remote(ins[a], outs[a].at[kp], send.at[3 * a + r], recv.at[3 * a + r], (*chip, c)).wait_recv()
        for cp in cps:
            cp.wait_send()
        for cp in local:
            cp.wait()

    return pl.pallas_call(
        body, name="allgather_small", in_specs=[ANY] * n, out_specs=[ANY] * n,
        out_shape=[jax.ShapeDtypeStruct((4, *s.shape), s.dtype) for s in shards],
        scratch_shapes=[pltpu.SemaphoreType.DMA((3 * n,)), pltpu.SemaphoreType.DMA((3 * n,)),
                        pltpu.SemaphoreType.DMA((n,))])(*shards)


def _swap_plan(n):
    def plan(refs):
        x, y, c, _ = _place()
        return [(refs[a].at[:, 1 - c], refs[n + a], refs[n + a], (x, y, 1 - c), 1) for a in range(n)]
    return plan


def _join_halves(totals):
    n = len(totals)

    def body(*refs):
        outs = refs[n:2 * n]
        send, recv = refs[2 * n:]
        x, y, c, _ = _place()
        sib = (x, y, 1 - c)
        for a in range(n):
            for s, d in _pieces(outs[a].at[c], outs[a].at[c], 4):
                _remote(s, d, send.at[a], recv.at[a], sib).start()
        for a in range(n):
            _remote(outs[a].at[1 - c], outs[a].at[1 - c], send.at[a], recv.at[a], sib).wait_recv()
            _remote(outs[a].at[c], outs[a].at[c], send.at[a], recv.at[a], sib).wait_send()

    return pl.pallas_call(
        body, name="grad_join_halves", in_specs=[ANY] * n, out_specs=[ANY] * n,
        out_shape=[jax.ShapeDtypeStruct(t.shape, t.dtype) for t in totals],
        input_output_aliases={a: a for a in range(n)},
        scratch_shapes=[pltpu.SemaphoreType.DMA((n,)), pltpu.SemaphoreType.DMA((n,))])(*totals)


def _allreduce_small(packed):
    rows = packed.shape[0]
    half = rows // 2

    def body(in_ref, out_ref, q_ref, s_ref, t_ref, send, recv):
        x, y, c, chips = _place()
        sib = (x, y, 1 - c)
        mine = pl.ds(pl.multiple_of(c * half, 8), half)
        theirs = pl.ds(pl.multiple_of((1 - c) * half, 8), half)
        first = _remote(in_ref.at[theirs], q_ref, send.at[0], recv.at[0], sib)
        first.start()
        first.wait()
        s_ref[...] = in_ref[mine, :] + q_ref[...]
        cps = [_remote(s_ref, t_ref.at[r], send.at[1 + r], recv.at[1 + r], (*chip, c)) for r, chip in enumerate(chips)]
        for cp in cps:
            cp.start()
        for cp in cps:
            cp.wait()
        out_ref[mine, :] = (s_ref[...] + t_ref[2]) + (t_ref[0] + t_ref[1])
        last = _remote(out_ref.at[mine], out_ref.at[mine], send.at[4], recv.at[4], sib)
        last.start()
        _remote(out_ref.at[theirs], out_ref.at[theirs], send.at[4], recv.at[4], sib).wait_recv()
        last.wait_send()

    vm = pl.BlockSpec(memory_space=pltpu.VMEM)
    return pl.pallas_call(
        body, name="allreduce_small", in_specs=[vm], out_specs=vm, out_shape=jax.ShapeDtypeStruct(packed.shape, F32),
        scratch_shapes=[pltpu.VMEM((half, LANES), F32), pltpu.VMEM((half, LANES), F32),
                        pltpu.VMEM((3, half, LANES), F32), pltpu.SemaphoreType.DMA((5,)), pltpu.SemaphoreType.DMA((5,))],
        compiler_params=pltpu.CompilerParams(vmem_limit_bytes=VMEM_LIMIT))(packed)


PACK_ROWS = 16
PACK_BLOCK = 512


def _pack(arrs):
    parts, total = [], 0
    for a in arrs:
        flat = a.reshape(-1)
        rows = -(-flat.shape[0] // (LANES * PACK_ROWS)) * PACK_ROWS
        parts.append(jnp.pad(flat, (0, rows * LANES - flat.shape[0])).reshape(rows, LANES))
        total += rows
    tail = -total % PACK_BLOCK
    if tail:
        parts.append(jnp.zeros((tail, LANES), parts[0].dtype))
    return jnp.concatenate(parts, axis=0)


def _unpack(packed, shapes):
    out, row = [], 0
    for shp in shapes:
        size = math.prod(shp)
        rows = -(-size // (LANES * PACK_ROWS)) * PACK_ROWS
        out.append(packed[row:row + rows].reshape(-1)[:size].reshape(shp))
        row += rows
    return out


def kernel(x, norm_mix_g, norm_ffn_g, a_w_in, a_g_v, a_w_s, a_b_s, a_w_out, b_w_in, b_a_re, b_a_im, b_log_dt, b_b_re, b_b_im, b_c_re, b_c_im, b_d, b_w_glu, f_w_up, f_conv_w, f_conv_b, f_w_down, final_g, loss_target, m_norm_mix_g, m_norm_ffn_g, m_a_w_in, m_a_g_v, m_a_w_s, m_a_b_s, m_a_w_out, m_b_w_in, m_b_a_re, m_b_a_im, m_b_log_dt, m_b_b_re, m_b_b_im, m_b_c_re, m_b_c_im, m_b_d, m_b_w_glu, m_f_w_up, m_f_conv_w, m_f_conv_b, m_f_w_down, m_final_g, v_norm_mix_g, v_norm_ffn_g, v_a_w_in, v_a_g_v, v_a_w_s, v_a_b_s, v_a_w_out, v_b_w_in, v_b_a_re, v_b_a_im, v_b_log_dt, v_b_b_re, v_b_b_im, v_b_c_re, v_b_c_im, v_b_d, v_b_w_glu, v_f_w_up, v_f_conv_w, v_f_conv_b, v_f_w_down, v_final_g):
    w = dict(norm_mix_g=norm_mix_g, norm_ffn_g=norm_ffn_g, a_w_in=a_w_in, a_g_v=a_g_v, a_w_s=a_w_s, a_b_s=a_b_s,
             a_w_out=a_w_out, b_w_in=b_w_in, b_a_re=b_a_re, b_a_im=b_a_im, b_log_dt=b_log_dt, b_b_re=b_b_re,
             b_b_im=b_b_im, b_c_re=b_c_re, b_c_im=b_c_im, b_d=b_d, b_w_glu=b_w_glu, f_w_up=f_w_up, f_conv_w=f_conv_w,
             f_conv_b=f_conv_b, f_w_down=f_w_down, final_g=final_g)
    mom = dict(norm_mix_g=m_norm_mix_g, norm_ffn_g=m_norm_ffn_g, a_w_in=m_a_w_in, a_g_v=m_a_g_v, a_w_s=m_a_w_s,
               a_b_s=m_a_b_s, a_w_out=m_a_w_out, b_w_in=m_b_w_in, b_a_re=m_b_a_re, b_a_im=m_b_a_im,
               b_log_dt=m_b_log_dt, b_b_re=m_b_b_re, b_b_im=m_b_b_im, b_c_re=m_b_c_re, b_c_im=m_b_c_im, b_d=m_b_d,
               b_w_glu=m_b_w_glu, f_w_up=m_f_w_up, f_conv_w=m_f_conv_w, f_conv_b=m_f_conv_b, f_w_down=m_f_w_down,
               final_g=m_final_g)
    var = dict(norm_mix_g=v_norm_mix_g, norm_ffn_g=v_norm_ffn_g, a_w_in=v_a_w_in, a_g_v=v_a_g_v, a_w_s=v_a_w_s,
               a_b_s=v_a_b_s, a_w_out=v_a_w_out, b_w_in=v_b_w_in, b_a_re=v_b_a_re, b_a_im=v_b_a_im,
               b_log_dt=v_b_log_dt, b_b_re=v_b_b_re, b_b_im=v_b_b_im, b_c_re=v_b_c_re, b_c_im=v_b_c_im, b_d=v_b_d,
               b_w_glu=v_b_w_glu, f_w_up=v_f_w_up, f_conv_w=v_f_conv_w, f_conv_b=v_f_conv_b, f_w_down=v_f_w_down,
               final_g=v_final_g)

    rows, d = x.shape[1], x.shape[2]
    depth = norm_mix_g.shape[0]
    kchip = 2 * lax.axis_index("x") + lax.axis_index("y")
    big_names = list(BIG)
    dims = {n: _full_dims(BIG[n], w[n].shape) for n in big_names}

    keys = [(n, l) for n in big_names for l in range(w[n].shape[0])]
    slots = {(n, l): _cast_halves("cast_" + n, w[n], BIG[n], l) for n, l in keys}

    def group_keys(g):
        i = g // 2
        if g % 2 == 1:
            return [('f_w_up', i), ('f_w_down', i)]
        return [('a_w_in', i // 2), ('a_w_out', i // 2)] if i % 2 == 0 else [('b_w_in', i // 2), ('b_w_glu', i // 2)]

    bd_all, cw_all = _allgather_small([b_d, f_conv_w.reshape(-1, f_conv_w.shape[-1])])

    gathered, gather_waits, token = {}, [], bd_all
    for g in range(2 * depth):
        arrs = [slots[k] for k in group_keys(g)]
        send, recv, thru, token = _split_start(f"allgather_start_{g}", arrs, 3 * len(arrs), _gather_plan(len(arrs)),
                                               deps=(token,))
        gather_waits.append((send, recv, thru))
    gather_after = token

    forwards = {}

    def prefetch_group(g, after):
        send, recv, thru = gather_waits[g]
        landed = _split_wait(f"allgather_wait_{g}", send, recv, thru, _gather_plan(len(thru)), after)
        send, recv, thru, token = _split_start(f"allgather_pass_start_{g}", landed, 3 * len(landed),
                                               _forward_plan(len(landed)))
        forwards[g] = (send, recv, thru)
        return (token,)

    def ready_group(g, after):
        send, recv, thru = forwards.pop(g)
        arrs = _split_wait(f"allgather_pass_wait_{g}", send, recv, thru, _forward_plan(len(thru)), after)
        gathered.update(zip(group_keys(g), arrs))

    bd_full = jnp.swapaxes(bd_all, 0, 1).reshape(b_d.shape[0], -1)
    cw_full = jnp.transpose(cw_all.reshape(4, *f_conv_w.shape), (1, 2, 0, 3)).reshape(depth, f_conv_w.shape[1], -1)

    pgrad = {}
    sgrad = {}

    def mm(name, a, wn, layer, out_dtype, residual=None, split=False, deps=()):
        return _mm_x_w(name, a, gathered[wn, layer], BIG[wn], *dims[wn], out_dtype, residual, split, deps)

    def mm_t(name, dy, wn, layer, out_dtype, split=False, deps=()):
        return _mm_dy_wt(name, dy, gathered[wn, layer], BIG[wn], *dims[wn], out_dtype, split, deps)

    def mm_g(name, xa, dy, wn, layer, split=False):
        pgrad[wn, layer] = _mm_xt_dy(name, xa, dy, BIG[wn], *dims[wn], split)

    e = d
    nb = e // LANES
    heads = e // SGU_GROUP
    h = x[0]
    saved = []
    for i in range(depth):
        j = i // 2
        if i == 0:
            prefetch_group(0, gather_after)
        ready_group(2 * i, gather_after if i == 0 else h)
        gm = norm_mix_g[i:i + 1]
        hn = _rms_fwd("rms_mix_fwd", h, gm)
        if i % 2 == 0:
            pre = mm("sgu_in", hn, 'a_w_in', j, BF16)
            passing = prefetch_group(2 * i + 1, pre)
            bsx = jnp.broadcast_to(a_b_s[j][:, :, None], (heads, CHUNK, LANES))
            us = _sgu_mix_fwd("sgu_mix_fwd", pre, a_g_v[j:j + 1], a_w_s[j], bsx)
            h_mid = mm("sgu_out", us, 'a_w_out', j, F32, residual=h, deps=passing)
            mix = dict(h=h, hn=hn, pre=pre, us=us, bsx=bsx)
        else:
            groups = b_a_re.shape[1]
            rep = lambda t: jnp.repeat(t, SSM_GROUP, axis=1)
            lr, li = b_a_re[j], b_a_im[j]
            ldt = jnp.broadcast_to(b_log_dt[j][:, None], lr.shape)
            bflat = lambda t: t.reshape(groups, SSM_STATE * SSM_GROUP)
            disc_in = (lr, li, ldt, rep(lr), rep(li), rep(ldt), bflat(b_b_re[j]), bflat(b_b_im[j]))
            abr, abi, bbr, bbi = _s5_disc("s5_disc", *disc_in)
            shape_b = (groups, SSM_STATE, SSM_GROUP)
            bre = _to_blockdiag_b(bbr.reshape(shape_b), nb).astype(BF16)
            bim = _to_blockdiag_b(bbi.reshape(shape_b), nb).astype(BF16)
            crt = _to_blockdiag_ct(b_c_re[j], nb).astype(BF16)
            cit = _to_blockdiag_ct(b_c_im[j], nb).astype(BF16)
            ar_row, ai_row = abr.reshape(1, -1), abi.reshape(1, -1)
            dd = bd_full[j:j + 1]
            u = mm("s5_in", hn, 'b_w_in', j, F32)
            passing = prefetch_group(2 * i + 1, u)
            yv, gy = _s5_fwd("s5_fwd", u, bre, bim, crt, cit, ar_row, ai_row, dd)
            gg = mm("s5_glu", gy, 'b_w_glu', j, BF16, deps=passing)
            h_mid = _glu_fwd("glu_fwd", gg, h)
            mix = dict(h=h, hn=hn, u=u, y=yv, gy=gy, gg=gg, disc_in=disc_in, mats=(bre, bim, crt, cit, ar_row, ai_row, dd))
        ready_group(2 * i + 1, h_mid)
        gf = norm_ffn_g[i:i + 1]
        hn2 = _rms_fwd("rms_ffn_fwd", h_mid, gf)
        z = mm("ffn_up", hn2, 'f_w_up', i, BF16, split=True)
        passing = prefetch_group(2 * i + 2, z) if i + 1 < depth else ()
        cw = jnp.swapaxes(cw_full[i].reshape(cw_full.shape[1], 2, -1), 0, 1)
        cb = f_conv_b[i].reshape(2, 1, -1)
        act = _ffn_act_fwd("ffn_act_fwd", z, cw, cb)
        h_out = mm("ffn_down", act, 'f_w_down', i, F32, residual=h_mid, deps=passing)
        saved.append((mix, dict(h=h_mid, hn=hn2, z=z, act=act, cw=cw, cb=cb)))
        h = h_out

    dh, g_final, loss_vec = _loss_head(h, final_g.reshape(1, d), loss_target[0])
    loss = lax.psum(jnp.sum(loss_vec), ("x", "y", "c"))
    sgrad['final_g'] = g_final.reshape(d)

    g_mix, g_ffn = [None] * depth, [None] * depth
    g_cw, g_cb = [None] * depth, [None] * depth
    sg = {k: [None] * (depth // 2) for k in ('a_g_v', 'a_w_s', 'a_b_s')}
    bg = {k: [None] * (depth // 2) for k in ('b_a_re', 'b_a_im', 'b_log_dt', 'b_b_re', 'b_b_im', 'b_c_re', 'b_c_im', 'b_d')}
    scatters, flight = {}, dict(pending=None, token=())

    def swap_group(g):
        parts = [pgrad[k] for k in group_keys(g)]
        land = [lax.empty((4, *p.shape[2:]), BF16) for p in parts]
        send, recv, thru, token = _split_start(f"grad_swap_start_{g}", parts + land, len(parts), _swap_plan(len(parts)))
        flight['swap'] = (send, recv, thru)
        return (token,)

    def scatter_group(g, done):
        if flight['pending'] is not None:
            prev, send, recv, thru = flight['pending']
            scatters[prev] = _split_wait(f"grad_scatter_wait_{prev}", send, recv, thru, _scatter_plan(len(thru) // 2), done)
        send, recv, thru = flight.pop('swap')
        thru = _split_wait(f"grad_swap_wait_{g}", send, recv, thru, _swap_plan(len(thru) // 2), done)
        n = len(thru) // 2
        sums = [_add2("grad_chip_sum", p, q) for p, q in zip(thru[:n], thru[n:])]
        land = [lax.empty((3, *s.shape[1:]), BF16) for s in sums]
        send, recv, thru, token = _split_start(f"grad_scatter_start_{g}", sums + land, 3 * len(sums),
                                               _scatter_plan(len(sums)))
        flight['pending'], flight['token'] = (g, send, recv, thru), (token,)

    for i in reversed(range(depth)):
        j = i // 2
        mix, ffn = saved[i]
        d_act = mm_t("ffn_down_dx", dh, 'f_w_down', i, BF16, deps=flight['token'])
        mm_g("ffn_down_dw", ffn['act'], dh, 'f_w_down', i)
        dz, dcw, dcb = _ffn_act_bwd("ffn_act_bwd", ffn['z'], d_act, ffn['cw'], ffn['cb'])
        g_cw[i], g_cb[i] = jnp.swapaxes(dcw, 0, 1).reshape(dcw.shape[1], -1), dcb.reshape(1, -1)
        mm_g("ffn_up_dw", ffn['hn'], dz, 'f_w_up', i, split=True)
        dhn = mm_t("ffn_up_dx", dz, 'f_w_up', i, F32, split=True, deps=swap_group(2 * i + 1))
        dh, g_ffn[i] = _rms_bwd("rms_ffn_bwd", ffn['h'], norm_ffn_g[i:i + 1], dhn, dh)
        scatter_group(2 * i + 1, dh)
        if i % 2 == 0:
            dus = mm_t("sgu_out_dx", dh, 'a_w_out', j, BF16, deps=flight['token'])
            mm_g("sgu_out_dw", mix['us'], dh, 'a_w_out', j)
            dpre, dws, dbs, dgv = _sgu_mix_bwd("sgu_mix_bwd", mix['pre'], dus, a_g_v[j:j + 1], a_w_s[j], mix['bsx'])
            sg['a_w_s'][j], sg['a_b_s'][j], sg['a_g_v'][j] = dws, dbs[:, :, 0], dgv[0]
            mm_g("sgu_in_dw", mix['hn'], dpre, 'a_w_in', j)
            dhn = mm_t("sgu_in_dx", dpre, 'a_w_in', j, F32, deps=swap_group(2 * i))
        else:
            dgg = _glu_bwd("glu_bwd", mix['gg'], dh)
            mm_g("s5_glu_dw", mix['gy'], dgg, 'b_w_glu', j)
            dgy = mm_t("s5_glu_dx", dgg, 'b_w_glu', j, BF16, deps=flight['token'])
            du, dbr, dbi, dcr, dci, dar, dai, ddd = _s5_bwd("s5_bwd", mix['u'], mix['y'], dgy, *mix['mats'])
            groups = b_a_re.shape[1]
            flat = lambda t: _from_blockdiag_b(t, nb).reshape(groups, SSM_STATE * SSM_GROUP)
            sel = jnp.repeat(jnp.eye(SSM_STATE, dtype=F32), SSM_GROUP, axis=0)
            dlr, dli, dldt, dbre, dbim = _s5_disc_bwd(
                "s5_disc_bwd", *mix['disc_in'], dar.reshape(groups, SSM_STATE), dai.reshape(groups, SSM_STATE),
                flat(dbr), flat(dbi), sel)
            bg['b_a_re'][j], bg['b_a_im'][j], bg['b_log_dt'][j] = dlr, dli, dldt[:, 0]
            bg['b_b_re'][j] = dbre.reshape(groups, SSM_STATE, SSM_GROUP)
            bg['b_b_im'][j] = dbim.reshape(groups, SSM_STATE, SSM_GROUP)
            bg['b_c_re'][j], bg['b_c_im'][j] = _from_blockdiag_ct(dcr, nb), _from_blockdiag_ct(dci, nb)
            bg['b_d'][j] = ddd[0]
            mm_g("s5_in_dw", mix['hn'], du, 'b_w_in', j)
            dhn = mm_t("s5_in_dx", du, 'b_w_in', j, F32, deps=swap_group(2 * i))
        dh, g_mix[i] = _rms_bwd("rms_mix_bwd", mix['h'], norm_mix_g[i:i + 1], dhn, dh)
        scatter_group(2 * i, dh)
    grad_x = dh[None]

    sgrad['norm_mix_g'] = jnp.concatenate(g_mix, axis=0)
    sgrad['norm_ffn_g'] = jnp.concatenate(g_ffn, axis=0)
    sgrad['f_conv_w'] = jnp.stack(g_cw)
    sgrad['f_conv_b'] = jnp.concatenate(g_cb, axis=0)
    for k, v_ in list(sg.items()) + list(bg.items()):
        sgrad[k] = jnp.stack(v_)

    total = _allreduce_small(_pack([sgrad[n] for n in SMALL]))
    full_shapes = [sgrad[n].shape for n in SMALL]
    gsmall = dict(zip(SMALL, _unpack(total, full_shapes)))
    for n, axis in CHIP_SHARDED_SMALL.items():
        width = w[n].shape[axis]
        gsmall[n] = lax.dynamic_slice_in_dim(gsmall[n], kchip * width, width, axis=axis)
    pk = lambda t: _pack([t[n] for n in SMALL])
    gpacked = pk(gsmall)
    dpk, mpk, vpk = _adam_small(pk(w), pk(mom), pk(var), gpacked)
    shard_shapes = [w[n].shape for n in SMALL]
    out_g = dict(gsmall)
    out_d = dict(zip(SMALL, _unpack(dpk, shard_shapes)))
    out_m = dict(zip(SMALL, _unpack(mpk, shard_shapes)))
    out_v = dict(zip(SMALL, _unpack(vpk, shard_shapes)))

    def group_totals(g):
        n = len(scatters[g]) // 2
        return [_add4("grad_total", s, r) for s, r in zip(scatters[g][:n], scatters[g][n:])]

    late_keys = [k for g in range(1, 2 * depth) for k in group_keys(g)]
    late = [t for g in range(1, 2 * depth) for t in group_totals(g)]
    send, recv, late, _ = _split_start("grad_join_start", late, len(late), _join_plan(len(late)))
    _, s0, r0, thru0 = flight['pending']
    scatters[0] = _split_wait("grad_scatter_wait_0", s0, r0, thru0, _scatter_plan(len(thru0) // 2), dpk)
    first = _join_halves(group_totals(0))
    late = _split_wait("grad_join_wait", send, recv, late, _join_plan(len(late)), first[0])
    gfull = dict(zip(group_keys(0) + late_keys, list(first) + list(late)))
    stacked = {n: [lax.empty(w[n].shape, F32) for _ in range(4)] for n in big_names}
    for n, l in keys:
        stacked[n] = _adam_big("adam_" + n, w[n], mom[n], var[n], gfull[n, l], BIG[n], l, stacked[n])
    for n in big_names:
        out_g[n], out_d[n], out_m[n], out_v[n] = stacked[n]

    return (loss, grad_x, *[out_g[n] for n in W_NAMES], *[out_d[n] for n in W_NAMES],
            *[out_m[n] for n in W_NAMES], *[out_v[n] for n in W_NAMES])
```

```python
import functools
import itertools
import math

import jax
import jax.numpy as jnp
from jax import lax
from jax.experimental import pallas as pl
from jax.experimental.pallas import tpu as pltpu

F32, BF16 = jnp.float32, jnp.bfloat16
MESH = pl.DeviceIdType.MESH

CHUNK = 128
SEG = CHUNK + 4
SGU_GROUP = 128
SSM_GROUP = 16
SSM_STATE = 64
EPS = 1e-6
LANES = 128
GROUPS_PER_BLOCK = LANES // SSM_GROUP
STATE_BLOCKS = SSM_STATE // SSM_GROUP
VMEM_LIMIT = 52 * 1024 * 1024

ADAM_LR, ADAM_B1, ADAM_B2, ADAM_EPS, ADAM_WD, ADAM_STEP = 0.001, 0.9, 0.999, 1e-08, 0.01, 10

W_NAMES = ['norm_mix_g', 'norm_ffn_g', 'a_w_in', 'a_g_v', 'a_w_s', 'a_b_s', 'a_w_out', 'b_w_in', 'b_a_re', 'b_a_im',
           'b_log_dt', 'b_b_re', 'b_b_im', 'b_c_re', 'b_c_im', 'b_d', 'b_w_glu', 'f_w_up', 'f_conv_w', 'f_conv_b',
           'f_w_down', 'final_g']
BIG = {'a_w_in': 'col', 'a_w_out': 'row', 'b_w_in': 'row', 'b_w_glu': 'col', 'f_w_up': 'col', 'f_w_down': 'row'}
SMALL = [n for n in W_NAMES if n not in BIG]
CHIP_SHARDED_SMALL = {'b_d': 1, 'f_conv_w': 2}


def _tile(n, pref, align):
    t = min(n, pref)
    t -= t % align
    while t >= align:
        if n % t == 0:
            return t
        t -= align
    return n


def _params(*sem):
    return pltpu.CompilerParams(dimension_semantics=sem, vmem_limit_bytes=VMEM_LIMIT)


def _gelu(x):
    c = math.sqrt(2.0 / math.pi)
    return 0.5 * x * (1.0 + jnp.tanh(c * (x + 0.044715 * x * x * x)))


def _gelu_grad(x):
    c = math.sqrt(2.0 / math.pi)
    t = jnp.tanh(c * (x + 0.044715 * x * x * x))
    return 0.5 * (1.0 + t) + 0.5 * x * (1.0 - t * t) * c * (1.0 + 3.0 * 0.044715 * x * x)


def _half_shape(kind, shard_shape):
    _, r, c = shard_shape
    return (r // 2, c) if kind == 'col' else (r, c // 2)


def _full_dims(kind, shard_shape):
    _, r, c = shard_shape
    return (r, 4 * c) if kind == 'col' else (4 * r, c)


def _gspec(kind, kdim, ndim, tr, tc, rc):
    if kind == 'col':
        nr, nc = (kdim // 2) // tr, (ndim // 4) // tc

        def imap(*g):
            rb, cb = rc(*g)
            return (cb // nc, rb // nr, rb % nr, cb % nc)
    else:
        nr, nc = (kdim // 4) // tr, (ndim // 2) // tc

        def imap(*g):
            rb, cb = rc(*g)
            return (rb // nr, cb // nc, rb % nr, cb % nc)
    return pl.BlockSpec((None, None, tr, tc), imap)


def _act_spec(rows_blk, cols_blk, ncol_half, at):
    if ncol_half is None:
        return pl.BlockSpec((rows_blk, cols_blk), at)

    def imap(*g):
        rb, cb = at(*g)
        return (cb // ncol_half, rb, cb % ncol_half)
    return pl.BlockSpec((None, rows_blk, cols_blk), imap)


def _wtiles(kind, kdim, ndim):
    if kind == 'col':
        return _tile(kdim // 2, 1024, LANES), _tile(ndim // 4, 1408, LANES)
    return _tile(kdim // 4, 1408, LANES), _tile(ndim // 2, 1024, LANES)


_DIMS = {'nn': (((1,), (0,)), ((), ())), 'nt': (((1,), (1,)), ((), ())), 'tn': (((0,), (0,)), ((), ()))}


def _matmul(name, mode, a, b, grid, a_spec, b_spec, out_shape, out_spec, acc_shape, extras=(), extra_specs=(),
            epilogue=None, aliases=None, deps=()):
    nk = grid[2]
    dims = _DIMS[mode]
    n_epi = len(extras)
    extras, extra_specs = tuple(extras) + tuple(deps), tuple(extra_specs) + (ANY,) * len(deps)
    n_extra = len(extras)
    b_specs = b_spec if isinstance(b_spec, (list, tuple)) else [b_spec]
    nb = len(b_specs)

    def body(a_ref, *rest):
        b_refs, rest = rest[:nb], rest[nb:]
        extra_refs, o_ref = rest[:n_extra], rest[n_extra]
        width = a_ref.shape[1] // nb
        prod = None
        for p, b_ref in enumerate(b_refs):
            a_blk = a_ref[...] if nb == 1 else a_ref[:, p * width:(p + 1) * width]
            term = lax.dot_general(a_blk.astype(BF16), b_ref[...].astype(BF16), dims, preferred_element_type=F32)
            prod = term if prod is None else prod + term

        def finish(r):
            if epilogue is not None:
                r = epilogue(r, *[e[...] for e in extra_refs[:n_epi]])
            o_ref[...] = r.astype(o_ref.dtype)

        if nk == 1:
            finish(prod)
            return
        acc_ref = rest[n_extra + 1]
        kk = pl.program_id(2)

        @pl.when(kk == 0)
        def _():
            acc_ref[...] = prod

        @pl.when(kk > 0)
        def _():
            acc_ref[...] += prod

        @pl.when(kk == nk - 1)
        def _():
            finish(acc_ref[...])

    scratch = [pltpu.VMEM(acc_shape, F32)] if nk > 1 else []
    return pl.pallas_call(
        body, name=name, grid=grid, in_specs=[a_spec, *b_specs, *extra_specs], out_specs=out_spec, out_shape=out_shape,
        scratch_shapes=scratch, input_output_aliases=aliases or {},
        compiler_params=_params("parallel", "parallel", "arbitrary"))(a, *([b] * nb), *extras)


PART_BYTES = 17 * 1024 * 1024


def _shard_parts(tr, tc):
    for parts in (4, 2):
        if parts * tr * tc * 2 * 2 <= PART_BYTES:
            return parts
    return 1


def _mm_x_w(name, a, wg, kind, kdim, ndim, out_dtype, residual=None, split=False, deps=()):
    rows = a.shape[0]
    tk, tn = _wtiles(kind, kdim, ndim)
    tm = _tile(rows, 1024, 16)
    if kind == 'col':
        tk = kdim
        b_spec = [_gspec(kind, kdim, ndim, kdim // 2, tn, lambda i, j, k, p=p: (p, j)) for p in range(2)]
    else:
        parts = _shard_parts(tk, tn) if tk == kdim // 4 else 1
        b_spec = [_gspec(kind, kdim, ndim, tk, tn, lambda i, j, k, p=p: (k * parts + p, j)) for p in range(parts)]
        tk = parts * tk
    grid = (rows // tm, ndim // tn, kdim // tk)
    extras, especs, epi = (), (), None
    if residual is not None:
        extras, especs = (residual,), (pl.BlockSpec((tm, tn), lambda i, j, k: (i, j)),)
        epi = lambda r, res: r + res
    out_shape = (2, rows, ndim // 2) if split else (rows, ndim)
    return _matmul(name, 'nn', a, wg, grid, pl.BlockSpec((tm, tk), lambda i, j, k: (i, k)), b_spec,
                   jax.ShapeDtypeStruct(out_shape, out_dtype),
                   _act_spec(tm, tn, (ndim // 2) // tn if split else None, lambda i, j, k: (i, j)),
                   (tm, tn), extras, especs, epi, deps=deps)


def _mm_dy_wt(name, dy, wg, kind, kdim, ndim, out_dtype, split=False, deps=()):
    rows = dy.shape[-2]
    tn, tk = _wtiles(kind, kdim, ndim)
    tm = _tile(rows, 1024, 16)
    if kind == 'col':
        shard = ndim // 4
        parts = 1 if split else _shard_parts(tn, shard)
        b_spec = [_gspec(kind, kdim, ndim, tn, shard, lambda i, j, k, p=p: (j, k * parts + p)) for p in range(parts)]
        tk = parts * shard
    else:
        assert not split
        tk = ndim
        b_spec = [_gspec(kind, kdim, ndim, tn, ndim // 2, lambda i, j, k, p=p: (j, p)) for p in range(2)]
    grid = (rows // tm, kdim // tn, ndim // tk)
    return _matmul(name, 'nt', dy, wg, grid,
                   _act_spec(tm, tk, (ndim // 2) // tk if split else None, lambda i, j, k: (i, k)), b_spec,
                   jax.ShapeDtypeStruct((rows, kdim), out_dtype), pl.BlockSpec((tm, tn), lambda i, j, k: (i, j)),
                   (tm, tn), extras=tuple(deps), extra_specs=(ANY,) * len(deps))


def _mm_xt_dy(name, xa, dy, kind, kdim, ndim, split=False):
    rows = xa.shape[0]
    tm, tn = _wtiles(kind, kdim, ndim)
    tl = _tile(rows, 2048, 16)
    grid = (kdim // tm, ndim // tn, rows // tl)
    rdim, cdim = (kdim // 2, ndim // 4) if kind == 'col' else (kdim // 4, ndim // 2)
    return _matmul(name, 'tn', xa, dy, grid, pl.BlockSpec((tl, tm), lambda i, j, k: (k, i)),
                   _act_spec(tl, tn, (ndim // 2) // tn if split else None, lambda i, j, k: (k, j)),
                   jax.ShapeDtypeStruct((4, 2, rdim, cdim), BF16),
                   _gspec(kind, kdim, ndim, tm, tn, lambda i, j, k: (i, j)), (tm, tn))


def _rms_fwd(name, h, g):
    rows, d = h.shape
    tm = _tile(rows, 256, 16)

    def body(h_ref, g_ref, o_ref):
        x = h_ref[...]
        r = lax.rsqrt(jnp.mean(x * x, axis=-1, keepdims=True) + EPS)
        o_ref[...] = (x * r * g_ref[...]).astype(o_ref.dtype)

    return pl.pallas_call(
        body, name=name, grid=(rows // tm,),
        in_specs=[pl.BlockSpec((tm, d), lambda i: (i, 0)), pl.BlockSpec((1, d), lambda i: (0, 0))],
        out_specs=pl.BlockSpec((tm, d), lambda i: (i, 0)), out_shape=jax.ShapeDtypeStruct((rows, d), BF16),
        compiler_params=_params("parallel"))(h, g)


def _rms_bwd(name, h, g, dhn, dres):
    rows, d = h.shape
    tm = _tile(rows, 256, 16)

    def body(h_ref, g_ref, dy_ref, dres_ref, dh_ref, dg_ref):
        x = h_ref[...]
        r = lax.rsqrt(jnp.mean(x * x, axis=-1, keepdims=True) + EPS)
        xh = x * r
        dy = dy_ref[...].astype(F32)
        gy = dy * g_ref[...]
        dh_ref[...] = dres_ref[...] + r * (gy - xh * jnp.mean(gy * xh, axis=-1, keepdims=True))
        part = jnp.sum(dy * xh, axis=0, keepdims=True)

        @pl.when(pl.program_id(0) == 0)
        def _():
            dg_ref[...] = part

        @pl.when(pl.program_id(0) > 0)
        def _():
            dg_ref[...] += part

    row = pl.BlockSpec((tm, d), lambda i: (i, 0))
    vec = pl.BlockSpec((1, d), lambda i: (0, 0))
    return pl.pallas_call(
        body, name=name, grid=(rows // tm,), in_specs=[row, vec, row, row], out_specs=[row, vec],
        out_shape=[jax.ShapeDtypeStruct((rows, d), F32), jax.ShapeDtypeStruct((1, d), F32)],
        compiler_params=_params("arbitrary"))(h, g, dhn, dres)


def _loss_head(h, g, target):
    rows, d = h.shape
    tm = _tile(rows, 256, 16)

    def body(h_ref, g_ref, t_ref, dh_ref, dg_ref, loss_ref):
        x = h_ref[...]
        r = lax.rsqrt(jnp.mean(x * x, axis=-1, keepdims=True) + EPS)
        xh = x * r
        err = xh * g_ref[...] - t_ref[...]
        dy = err * (1.0 / d)
        gy = dy * g_ref[...]
        dh_ref[...] = r * (gy - xh * jnp.mean(gy * xh, axis=-1, keepdims=True))
        part = jnp.sum(dy * xh, axis=0, keepdims=True)
        sq = jnp.sum(err * err, axis=0, keepdims=True) * (0.5 / d)

        @pl.when(pl.program_id(0) == 0)
        def _():
            dg_ref[...] = part
            loss_ref[...] = sq

        @pl.when(pl.program_id(0) > 0)
        def _():
            dg_ref[...] += part
            loss_ref[...] += sq

    row = pl.BlockSpec((tm, d), lambda i: (i, 0))
    vec = pl.BlockSpec((1, d), lambda i: (0, 0))
    return pl.pallas_call(
        body, name="loss_head", grid=(rows // tm,), in_specs=[row, vec, row], out_specs=[row, vec, vec],
        out_shape=[jax.ShapeDtypeStruct((rows, d), F32), jax.ShapeDtypeStruct((1, d), F32),
                   jax.ShapeDtypeStruct((1, d), F32)],
        compiler_params=_params("arbitrary"))(h, g, target)


def _shift_down(cur, prev8, first, k):
    rows = cur.shape[0]
    rolled = pltpu.roll(cur, k, axis=0)
    idx = lax.broadcasted_iota(jnp.int32, cur.shape, 0)
    prev8 = jnp.where(first, 0.0, prev8)
    out = rolled
    for r in range(k):
        out = jnp.where(idx == r, prev8[8 - k + r:8 - k + r + 1, :], out)
    del rows
    return out


def _shift_up(cur, next8, last, k):
    rows = cur.shape[0]
    rolled = pltpu.roll(cur, rows - k, axis=0)
    idx = lax.broadcasted_iota(jnp.int32, cur.shape, 0)
    next8 = jnp.where(last, 0.0, next8)
    out = rolled
    for r in range(k):
        out = jnp.where(idx == rows - k + r, next8[r:r + 1, :], out)
    return out


def _conv_acc(z, zprev, first, w, b):
    z1 = _shift_down(z, zprev, first, 1)
    z2 = _shift_down(z, zprev, first, 2)
    return b + w[2:3, :] * z + w[1:2, :] * z1 + w[0:1, :] * z2, z1, z2


def _ffn_tiles(rows, f):
    return _tile(rows, 512, 16), _tile(f, 512, LANES)


def _ffn_act_fwd(name, z3, cw3, cb3):
    _, rows, f = z3.shape
    tm, tc = _ffn_tiles(rows, f)
    hb = tm // 8

    def body(z_ref, zp_ref, w_ref, b_ref, o_ref):
        first = pl.program_id(0) == 0
        gate, _, _ = _conv_acc(z_ref[0].astype(F32), zp_ref[0].astype(F32), first, w_ref[0], b_ref[0])
        val, _, _ = _conv_acc(z_ref[1].astype(F32), zp_ref[1].astype(F32), first, w_ref[1], b_ref[1])
        o_ref[...] = (gate * jax.nn.sigmoid(gate) * val).astype(o_ref.dtype)

    return pl.pallas_call(
        body, name=name, grid=(rows // tm, f // tc),
        in_specs=[pl.BlockSpec((2, tm, tc), lambda i, j: (0, i, j)),
                  pl.BlockSpec((2, 8, tc), lambda i, j: (0, jnp.maximum(i * hb - 1, 0), j)),
                  pl.BlockSpec((2, 3, tc), lambda i, j: (0, 0, j)), pl.BlockSpec((2, 1, tc), lambda i, j: (0, 0, j))],
        out_specs=pl.BlockSpec((tm, tc), lambda i, j: (i, j)), out_shape=jax.ShapeDtypeStruct((rows, f), BF16),
        compiler_params=_params("parallel", "parallel"))(z3, z3, cw3, cb3)


def _gate_grads(d_a, acc_g, acc_v):
    sig = jax.nn.sigmoid(acc_g)
    return d_a * acc_v * sig * (1.0 + acc_g * (1.0 - sig)), d_a * acc_g * sig


def _ffn_act_bwd(name, z3, da, cw3, cb3):
    _, rows, f = z3.shape
    tm, tc = _ffn_tiles(rows, f)
    hb = tm // 8
    nrow = rows // tm

    def body(z_ref, zp_ref, zn_ref, da_ref, dan_ref, w_ref, b_ref, dz_ref, dcw_ref, dcb_ref):
        i = pl.program_id(1)
        first, last = i == 0, i == nrow - 1
        w, b = (w_ref[0], w_ref[1]), (b_ref[0], b_ref[1])
        z = (z_ref[0].astype(F32), z_ref[1].astype(F32))
        acc, taps = [], []
        for hf in range(2):
            a_h, z1, z2 = _conv_acc(z[hf], zp_ref[hf].astype(F32), first, w[hf], b[hf])
            acc.append(a_h)
            taps.append((z2, z1, z[hf]))
        dacc = _gate_grads(da_ref[...].astype(F32), acc[0], acc[1])
        acc_n = [_conv_acc(zn_ref[hf].astype(F32), z[hf][tm - 8:tm, :], False, w[hf], b[hf])[0] for hf in range(2)]
        dacc_n = _gate_grads(dan_ref[...].astype(F32), acc_n[0], acc_n[1])
        for hf in range(2):
            d = dacc[hf]
            d1 = _shift_up(d, dacc_n[hf], last, 1)
            d2 = _shift_up(d, dacc_n[hf], last, 2)
            dz_ref[hf] = (w[hf][2:3, :] * d + w[hf][1:2, :] * d1 + w[hf][0:1, :] * d2).astype(dz_ref.dtype)
        sums_w = [[jnp.sum(dacc[hf] * t, axis=0, keepdims=True) for t in taps[hf]] for hf in range(2)]
        sums_b = [jnp.sum(dacc[hf], axis=0, keepdims=True) for hf in range(2)]

        @pl.when(first)
        def _():
            for hf in range(2):
                for k in range(3):
                    dcw_ref[hf, k:k + 1, :] = sums_w[hf][k]
                dcb_ref[hf] = sums_b[hf]

        @pl.when(i > 0)
        def _():
            for hf in range(2):
                for k in range(3):
                    dcw_ref[hf, k:k + 1, :] += sums_w[hf][k]
                dcb_ref[hf] += sums_b[hf]

    nxt = lambda i: jnp.minimum((i + 1) * hb, rows // 8 - 1)
    wsp = pl.BlockSpec((2, 3, tc), lambda j, i: (0, 0, j))
    bsp = pl.BlockSpec((2, 1, tc), lambda j, i: (0, 0, j))
    cur = pl.BlockSpec((2, tm, tc), lambda j, i: (0, i, j))
    return pl.pallas_call(
        body, name=name, grid=(f // tc, nrow),
        in_specs=[cur, pl.BlockSpec((2, 8, tc), lambda j, i: (0, jnp.maximum(i * hb - 1, 0), j)),
                  pl.BlockSpec((2, 8, tc), lambda j, i: (0, nxt(i), j)), pl.BlockSpec((tm, tc), lambda j, i: (i, j)),
                  pl.BlockSpec((8, tc), lambda j, i: (nxt(i), j)), wsp, bsp],
        out_specs=[cur, wsp, bsp],
        out_shape=[jax.ShapeDtypeStruct((2, rows, f), BF16), jax.ShapeDtypeStruct((2, 3, f), F32),
                   jax.ShapeDtypeStruct((2, 1, f), F32)],
        compiler_params=_params("parallel", "arbitrary"))(z3, z3, z3, da, da, cw3, cb3)


def _glu_fwd(name, gg, h):
    rows, d2 = gg.shape
    d = d2 // 2
    tm, tc = _tile(rows, 512, 16), _tile(d, 1024, LANES)
    nd = d // tc

    def body(a_ref, b_ref, h_ref, o_ref):
        o_ref[...] = h_ref[...] + a_ref[...].astype(F32) * jax.nn.sigmoid(b_ref[...].astype(F32))

    return pl.pallas_call(
        body, name=name, grid=(rows // tm, nd),
        in_specs=[pl.BlockSpec((tm, tc), lambda i, j: (i, j)), pl.BlockSpec((tm, tc), lambda i, j: (i, j + nd)),
                  pl.BlockSpec((tm, tc), lambda i, j: (i, j))],
        out_specs=pl.BlockSpec((tm, tc), lambda i, j: (i, j)), out_shape=jax.ShapeDtypeStruct((rows, d), F32),
        compiler_params=_params("parallel", "parallel"))(gg, gg, h)


def _glu_bwd(name, gg, dh):
    rows, d2 = gg.shape
    d = d2 // 2
    tm, tc = _tile(rows, 512, 16), _tile(d, 1024, LANES)
    nd = d // tc

    def body(s_ref, o_ref, dh_ref, out_ref):
        is_a = pl.program_id(1) < nd
        me = s_ref[...].astype(F32)
        other = o_ref[...].astype(F32)
        g = dh_ref[...]
        sig_o = jax.nn.sigmoid(other)
        sig_m = jax.nn.sigmoid(me)
        out_ref[...] = jnp.where(is_a, g * sig_o, g * other * sig_m * (1.0 - sig_m)).astype(out_ref.dtype)

    return pl.pallas_call(
        body, name=name, grid=(rows // tm, 2 * nd),
        in_specs=[pl.BlockSpec((tm, tc), lambda i, j: (i, j)),
                  pl.BlockSpec((tm, tc), lambda i, j: (i, (j + nd) % (2 * nd))),
                  pl.BlockSpec((tm, tc), lambda i, j: (i, j % nd))],
        out_specs=pl.BlockSpec((tm, tc), lambda i, j: (i, j)), out_shape=jax.ShapeDtypeStruct((rows, d2), BF16),
        compiler_params=_params("parallel", "parallel"))(gg, gg, dh)


def _sgu_common(pre_ref, gv_ref, e):
    u = _gelu(pre_ref[:, :e].astype(F32))
    v = _gelu(pre_ref[:, e:].astype(F32))
    r = lax.rsqrt(jnp.mean(v * v, axis=-1, keepdims=True) + EPS)
    vh = v * r
    return u, vh, r, (vh * gv_ref[...]).astype(BF16)


def _tril_bf16(ws_ref, hd):
    t = lax.broadcasted_iota(jnp.int32, (CHUNK, CHUNK), 0)
    s = lax.broadcasted_iota(jnp.int32, (CHUNK, CHUNK), 1)
    return jnp.where(s <= t, ws_ref[hd], 0.0).astype(BF16)


def _sgu_mix_fwd(name, pre, gv, ws, bsx):
    rows, e2 = pre.shape
    e = e2 // 2
    heads = e // SGU_GROUP
    tr = _tile(rows, 256, CHUNK)

    def body(pre_ref, gv_ref, ws_ref, bs_ref, o_ref):
        u, _, _, vn = _sgu_common(pre_ref, gv_ref, e)
        for hd in range(heads):
            wm = _tril_bf16(ws_ref, hd)
            cols = slice(hd * SGU_GROUP, (hd + 1) * SGU_GROUP)
            for ck in range(tr // CHUNK):
                rws = slice(ck * CHUNK, (ck + 1) * CHUNK)
                s = jnp.dot(wm, vn[rws, cols], preferred_element_type=F32) + bs_ref[hd]
                o_ref[rws, cols] = (u[rws, cols] * s).astype(o_ref.dtype)

    whole3 = pl.BlockSpec((heads, CHUNK, CHUNK), lambda i: (0, 0, 0))
    return pl.pallas_call(
        body, name=name, grid=(rows // tr,),
        in_specs=[pl.BlockSpec((tr, e2), lambda i: (i, 0)), pl.BlockSpec((1, e), lambda i: (0, 0)), whole3, whole3],
        out_specs=pl.BlockSpec((tr, e), lambda i: (i, 0)), out_shape=jax.ShapeDtypeStruct((rows, e), BF16),
        compiler_params=_params("parallel"))(pre, gv, ws, bsx)


def _sgu_mix_bwd(name, pre, dus, gv, ws, bsx):
    rows, e2 = pre.shape
    e = e2 // 2
    heads = e // SGU_GROUP
    tr = _tile(rows, 256, CHUNK)

    def body(pre_ref, dus_ref, gv_ref, ws_ref, bs_ref, dpre_ref, dws_ref, dbs_ref, dgv_ref, dvn_ref, du_ref):
        first = pl.program_id(0) == 0
        u, vh, r, vn = _sgu_common(pre_ref, gv_ref, e)
        ones = jnp.ones((SGU_GROUP, LANES), BF16)
        tt = lax.broadcasted_iota(jnp.int32, (CHUNK, CHUNK), 0)
        ss = lax.broadcasted_iota(jnp.int32, (CHUNK, CHUNK), 1)
        for hd in range(heads):
            wm = _tril_bf16(ws_ref, hd)
            cols = slice(hd * SGU_GROUP, (hd + 1) * SGU_GROUP)
            dw = jnp.zeros((CHUNK, CHUNK), F32)
            db = jnp.zeros((CHUNK, LANES), F32)
            for ck in range(tr // CHUNK):
                rws = slice(ck * CHUNK, (ck + 1) * CHUNK)
                vblk = vn[rws, cols]
                s = jnp.dot(wm, vblk, preferred_element_type=F32) + bs_ref[hd]
                d_us = dus_ref[rws, cols].astype(F32)
                du_ref[rws, cols] = d_us * s
                ds = (d_us * u[rws, cols]).astype(BF16)
                dvn_ref[rws, cols] = lax.dot_general(wm, ds, _DIMS['tn'], preferred_element_type=F32)
                dw = dw + lax.dot_general(ds, vblk, _DIMS['nt'], preferred_element_type=F32)
                db = db + jnp.dot(ds, ones, preferred_element_type=F32)
            dw = jnp.where(ss <= tt, dw, 0.0)

            @pl.when(first)
            def _():
                dws_ref[hd] = dw
                dbs_ref[hd] = db

            @pl.when(jnp.logical_not(first))
            def _():
                dws_ref[hd] += dw
                dbs_ref[hd] += db

        dvn = dvn_ref[...]
        part = jnp.sum(dvn * vh, axis=0, keepdims=True)

        @pl.when(first)
        def _():
            dgv_ref[...] = part

        @pl.when(jnp.logical_not(first))
        def _():
            dgv_ref[...] += part

        gy = dvn * gv_ref[...]
        dv = r * (gy - vh * jnp.mean(gy * vh, axis=-1, keepdims=True))
        dpre_ref[:, :e] = (du_ref[...] * _gelu_grad(pre_ref[:, :e].astype(F32))).astype(dpre_ref.dtype)
        dpre_ref[:, e:] = (dv * _gelu_grad(pre_ref[:, e:].astype(F32))).astype(dpre_ref.dtype)

    whole3 = pl.BlockSpec((heads, CHUNK, CHUNK), lambda i: (0, 0, 0))
    vec = pl.BlockSpec((1, e), lambda i: (0, 0))
    return pl.pallas_call(
        body, name=name, grid=(rows // tr,),
        in_specs=[pl.BlockSpec((tr, e2), lambda i: (i, 0)), pl.BlockSpec((tr, e), lambda i: (i, 0)), vec, whole3, whole3],
        out_specs=[pl.BlockSpec((tr, e2), lambda i: (i, 0)), whole3, whole3, vec],
        out_shape=[jax.ShapeDtypeStruct((rows, e2), BF16), jax.ShapeDtypeStruct((heads, CHUNK, CHUNK), F32),
                   jax.ShapeDtypeStruct((heads, CHUNK, LANES), F32), jax.ShapeDtypeStruct((1, e), F32)],
        scratch_shapes=[pltpu.VMEM((tr, e), F32), pltpu.VMEM((tr, e), F32)],
        compiler_params=_params("arbitrary"))(pre, dus, gv, ws, bsx)


def _disc_a(lr, li, ldt):
    dt = jnp.exp(ldt)
    mag = jnp.exp(dt * lr)
    return mag * jnp.cos(dt * li), mag * jnp.sin(dt * li)


def _disc_b(lr, li, ldt, br, bi):
    ar, ai = _disc_a(lr, li, ldt)
    den = lr * lr + li * li
    qr = ((ar - 1.0) * lr + ai * li) / den
    qi = (ai * lr - (ar - 1.0) * li) / den
    return qr * br - qi * bi, qr * bi + qi * br


def _s5_disc(name, lr, li, ldt, lrx, lix, ldtx, br, bi):
    def body(lr_ref, li_ref, ldt_ref, lrx_ref, lix_ref, ldtx_ref, br_ref, bi_ref, ar_ref, ai_ref, bbr_ref, bbi_ref):
        ar_ref[...], ai_ref[...] = _disc_a(lr_ref[...], li_ref[...], ldt_ref[...])
        bbr_ref[...], bbi_ref[...] = _disc_b(lrx_ref[...], lix_ref[...], ldtx_ref[...], br_ref[...], bi_ref[...])

    small = jax.ShapeDtypeStruct(lr.shape, F32)
    wide = jax.ShapeDtypeStruct(br.shape, F32)
    return pl.pallas_call(body, name=name, out_shape=[small, small, wide, wide],
                          compiler_params=pltpu.CompilerParams(vmem_limit_bytes=VMEM_LIMIT))(
        lr, li, ldt, lrx, lix, ldtx, br, bi)


def _s5_disc_bwd(name, lr, li, ldt, lrx, lix, ldtx, br, bi, dar, dai, dbbr, dbbi, sel):
    def body(lr_ref, li_ref, ldt_ref, lrx_ref, lix_ref, ldtx_ref, br_ref, bi_ref, dar_ref, dai_ref, dbbr_ref,
             dbbi_ref, sel_ref, dlr_ref, dli_ref, dldt_ref, dbr_ref, dbi_ref):
        _, vjp_a = jax.vjp(_disc_a, lr_ref[...], li_ref[...], ldt_ref[...])
        g_lr, g_li, g_ldt = vjp_a((dar_ref[...], dai_ref[...]))
        _, vjp_b = jax.vjp(_disc_b, lrx_ref[...], lix_ref[...], ldtx_ref[...], br_ref[...], bi_ref[...])
        x_lr, x_li, x_ldt, g_br, g_bi = vjp_b((dbbr_ref[...], dbbi_ref[...]))
        fold = lambda t: jnp.dot(t, sel_ref[...], precision=lax.Precision.HIGHEST, preferred_element_type=F32)
        dlr_ref[...] = g_lr + fold(x_lr)
        dli_ref[...] = g_li + fold(x_li)
        dldt_ref[...] = jnp.sum(g_ldt + fold(x_ldt), axis=1, keepdims=True)
        dbr_ref[...] = g_br
        dbi_ref[...] = g_bi

    small = jax.ShapeDtypeStruct(lr.shape, F32)
    wide = jax.ShapeDtypeStruct(br.shape, F32)
    return pl.pallas_call(body, name=name,
                          out_shape=[small, small, jax.ShapeDtypeStruct((lr.shape[0], 1), F32), wide, wide],
                          compiler_params=pltpu.CompilerParams(vmem_limit_bytes=VMEM_LIMIT))(
        lr, li, ldt, lrx, lix, ldtx, br, bi, dar, dai, dbbr, dbbi, sel)


def _cmul(ar, ai, br, bi):
    return ar * br - ai * bi, ar * bi + ai * br


def _pow_seg(ar, ai):
    res, base, n = None, (ar, ai), SEG
    while n:
        if n & 1:
            res = base if res is None else _cmul(*res, *base)
        n >>= 1
        if n:
            base = _cmul(*base, *base)
    return res


def _scan_forward(hr_ref, hi_ref, er_ref, ei_ref, sr_ref, si_ref, ar, ai, nck):
    arb, aib = jnp.broadcast_to(ar, (nck, LANES)), jnp.broadcast_to(ai, (nck, LANES))

    def intra(t, carry):
        sr, si = carry
        slab = pl.ds(t, nck, stride=SEG)
        nr = arb * sr - aib * si + hr_ref[slab, :]
        ni = arb * si + aib * sr + hi_ref[slab, :]
        hr_ref[slab, :] = nr
        hi_ref[slab, :] = ni
        return nr, ni

    zero = jnp.zeros((nck, LANES), F32)
    er_ref[...], ei_ref[...] = lax.fori_loop(0, SEG, intra, (zero, zero), unroll=4)
    pcr, pci = _pow_seg(ar, ai)
    sr_ref[0:1, :] = jnp.zeros((1, LANES), F32)
    si_ref[0:1, :] = jnp.zeros((1, LANES), F32)
    for ck in range(nck - 1):
        pr, pi = sr_ref[ck:ck + 1, :], si_ref[ck:ck + 1, :]
        sr_ref[ck + 1:ck + 2, :] = pcr * pr - pci * pi + er_ref[ck:ck + 1, :]
        si_ref[ck + 1:ck + 2, :] = pcr * pi + pci * pr + ei_ref[ck:ck + 1, :]
    s_r, s_i = sr_ref[...], si_ref[...]

    def fix(t, carry):
        pr, pi = carry
        slab = pl.ds(t, nck, stride=SEG)
        hr_ref[slab, :] = hr_ref[slab, :] + (pr * s_r - pi * s_i)
        hi_ref[slab, :] = hi_ref[slab, :] + (pr * s_i + pi * s_r)
        return _cmul(pr, pi, arb, aib)

    lax.fori_loop(0, SEG, fix, (arb, aib), unroll=4)


def _scan_backward(gr_ref, gi_ref, hr_ref, hi_ref, er_ref, ei_ref, sr_ref, si_ref, ar, ai, nck):
    arb, aib = jnp.broadcast_to(ar, (nck, LANES)), jnp.broadcast_to(-ai, (nck, LANES))

    def intra(k, carry):
        sr, si = carry
        slab = pl.ds(SEG - 1 - k, nck, stride=SEG)
        nr = arb * sr - aib * si + gr_ref[slab, :]
        ni = arb * si + aib * sr + gi_ref[slab, :]
        gr_ref[slab, :] = nr
        gi_ref[slab, :] = ni
        return nr, ni

    zero = jnp.zeros((nck, LANES), F32)
    er_ref[...], ei_ref[...] = lax.fori_loop(0, SEG, intra, (zero, zero), unroll=4)
    pcr, pci = _pow_seg(ar, -ai)
    sr_ref[nck - 1:nck, :] = jnp.zeros((1, LANES), F32)
    si_ref[nck - 1:nck, :] = jnp.zeros((1, LANES), F32)
    for ck in range(nck - 1, 0, -1):
        pr, pi = sr_ref[ck:ck + 1, :], si_ref[ck:ck + 1, :]
        sr_ref[ck - 1:ck, :] = pcr * pr - pci * pi + er_ref[ck:ck + 1, :]
        si_ref[ck - 1:ck, :] = pcr * pi + pci * pr + ei_ref[ck:ck + 1, :]
    s_r, s_i = sr_ref[...], si_ref[...]
    last = pl.ds(SEG - 1, nck, stride=SEG)
    row = lax.broadcasted_iota(jnp.int32, (nck, LANES), 0)
    hp_r = jnp.where(row == 0, 0.0, pltpu.roll(hr_ref[last, :], 1, axis=0)) if nck > 1 else zero
    hp_i = jnp.where(row == 0, 0.0, pltpu.roll(hi_ref[last, :], 1, axis=0)) if nck > 1 else zero

    def settle(t, pr, pi, h_r, h_i):
        slab = pl.ds(t, nck, stride=SEG)
        g_r = gr_ref[slab, :] + (pr * s_r - pi * s_i)
        g_i = gi_ref[slab, :] + (pr * s_i + pi * s_r)
        gr_ref[slab, :] = g_r
        gi_ref[slab, :] = g_i
        return g_r * h_r + g_i * h_i, g_i * h_r - g_r * h_i

    def fix(k, carry):
        pr, pi, acr, aci = carry
        t = SEG - 1 - k
        prev = pl.ds(t - 1, nck, stride=SEG)
        d_r, d_i = settle(t, pr, pi, hr_ref[prev, :], hi_ref[prev, :])
        nr, ni = _cmul(pr, pi, arb, aib)
        return nr, ni, acr + d_r, aci + d_i

    pr, pi, acr, aci = lax.fori_loop(0, SEG - 1, fix, (arb, aib, zero, zero), unroll=4)
    d_r, d_i = settle(0, pr, pi, hp_r, hp_i)
    return jnp.sum(acr + d_r, axis=0, keepdims=True), jnp.sum(aci + d_i, axis=0, keepdims=True)


def _s5_fill_states(u_ref, br_ref, bi_ref, hr_ref, hi_ref, rows):
    ub = u_ref[...].astype(BF16)
    hr_ref[0:rows, :] = jnp.dot(ub, br_ref[...], preferred_element_type=F32)
    hi_ref[0:rows, :] = jnp.dot(ub, bi_ref[...], preferred_element_type=F32)
    pad = jnp.zeros((hr_ref.shape[0] - rows, LANES), F32)
    hr_ref[rows:, :] = pad
    hi_ref[rows:, :] = pad


def _s5_specs(rows, e):
    sb = STATE_BLOCKS
    chan = pl.BlockSpec((rows, LANES), lambda j: (0, j // sb))
    bmat = pl.BlockSpec((None, LANES, LANES), lambda j: (j // sb, 0, j % sb))
    cmat = pl.BlockSpec((None, LANES, LANES), lambda j: (j // sb, j % sb, 0))
    avec = pl.BlockSpec((1, LANES), lambda j: (0, j))
    dvec = pl.BlockSpec((1, LANES), lambda j: (0, j // sb))
    return chan, bmat, cmat, avec, dvec


def _s5_fwd(name, u, bre, bim, crt, cit, ar, ai, dd):
    rows, e = u.shape
    nck = rows // CHUNK
    nsteps = (e // LANES) * STATE_BLOCKS
    chan, bmat, cmat, avec, dvec = _s5_specs(rows, e)

    def body(u_ref, br_ref, bi_ref, cr_ref, ci_ref, ar_ref, ai_ref, dd_ref, y_ref, gy_ref,
             hr_ref, hi_ref, er_ref, ei_ref, sr_ref, si_ref, acc_ref):
        j = pl.program_id(0) % STATE_BLOCKS
        _s5_fill_states(u_ref, br_ref, bi_ref, hr_ref, hi_ref, rows)
        _scan_forward(hr_ref, hi_ref, er_ref, ei_ref, sr_ref, si_ref, ar_ref[...], ai_ref[...], nck)
        contrib = (jnp.dot(hr_ref[0:rows, :].astype(BF16), cr_ref[...], preferred_element_type=F32)
                   - jnp.dot(hi_ref[0:rows, :].astype(BF16), ci_ref[...], preferred_element_type=F32))

        @pl.when(j == 0)
        def _():
            acc_ref[...] = dd_ref[...] * u_ref[...] + contrib

        @pl.when(j > 0)
        def _():
            acc_ref[...] += contrib

        @pl.when(j == STATE_BLOCKS - 1)
        def _():
            y = acc_ref[...]
            y_ref[...] = y.astype(y_ref.dtype)
            gy_ref[...] = _gelu(y).astype(gy_ref.dtype)

    flat = pltpu.VMEM((rows, LANES), F32)
    big = pltpu.VMEM((nck * SEG, LANES), F32)
    small = pltpu.VMEM((nck, LANES), F32)
    out = jax.ShapeDtypeStruct((rows, e), BF16)
    return pl.pallas_call(
        body, name=name, grid=(nsteps,), in_specs=[chan, bmat, bmat, cmat, cmat, avec, avec, dvec],
        out_specs=[chan, chan], out_shape=[out, out], scratch_shapes=[big, big, small, small, small, small, flat],
        compiler_params=_params("arbitrary"))(u, bre, bim, crt, cit, ar, ai, dd)


def _s5_bwd(name, u, y, dgy, bre, bim, crt, cit, ar, ai, dd):
    rows, e = u.shape
    nb = e // LANES
    nck = rows // CHUNK
    nsteps = nb * STATE_BLOCKS
    chan, bmat, cmat, avec, dvec = _s5_specs(rows, e)

    def body(u_ref, y_ref, dgy_ref, br_ref, bi_ref, cr_ref, ci_ref, ar_ref, ai_ref, dd_ref,
             du_ref, dbr_ref, dbi_ref, dcr_ref, dci_ref, dar_ref, dai_ref, ddd_ref,
             hr_ref, hi_ref, gr_ref, gi_ref, er_ref, ei_ref, sr_ref, si_ref, acc_ref, dy_ref):
        j = pl.program_id(0) % STATE_BLOCKS
        _s5_fill_states(u_ref, br_ref, bi_ref, hr_ref, hi_ref, rows)
        _scan_forward(hr_ref, hi_ref, er_ref, ei_ref, sr_ref, si_ref, ar_ref[...], ai_ref[...], nck)

        @pl.when(j == 0)
        def _():
            dy0 = dgy_ref[...].astype(F32) * _gelu_grad(y_ref[...].astype(F32))
            dy_ref[...] = dy0
            ddd_ref[...] = jnp.sum(dy0 * u_ref[...], axis=0, keepdims=True)

        dyb = dy_ref[...].astype(BF16)
        pad = jnp.zeros((gr_ref.shape[0] - rows, LANES), F32)
        gr_ref[0:rows, :] = lax.dot_general(dyb, cr_ref[...], _DIMS['nt'], preferred_element_type=F32)
        gi_ref[0:rows, :] = -lax.dot_general(dyb, ci_ref[...], _DIMS['nt'], preferred_element_type=F32)
        gr_ref[rows:, :] = pad
        gi_ref[rows:, :] = pad
        dcr_ref[...] = lax.dot_general(hr_ref[0:rows, :].astype(BF16), dyb, _DIMS['tn'], preferred_element_type=F32)
        dci_ref[...] = -lax.dot_general(hi_ref[0:rows, :].astype(BF16), dyb, _DIMS['tn'], preferred_element_type=F32)
        dar_ref[...], dai_ref[...] = _scan_backward(gr_ref, gi_ref, hr_ref, hi_ref, er_ref, ei_ref, sr_ref, si_ref,
                                                    ar_ref[...], ai_ref[...], nck)
        ub = u_ref[...].astype(BF16)
        grb, gib = gr_ref[0:rows, :].astype(BF16), gi_ref[0:rows, :].astype(BF16)
        dbr_ref[...] = lax.dot_general(ub, grb, _DIMS['tn'], preferred_element_type=F32)
        dbi_ref[...] = lax.dot_general(ub, gib, _DIMS['tn'], preferred_element_type=F32)
        contrib = (lax.dot_general(grb, br_ref[...], _DIMS['nt'], preferred_element_type=F32)
                   + lax.dot_general(gib, bi_ref[...], _DIMS['nt'], preferred_element_type=F32))

        @pl.when(j == 0)
        def _():
            acc_ref[...] = dd_ref[...] * dy_ref[...] + contrib

        @pl.when(j > 0)
        def _():
            acc_ref[...] += contrib

        @pl.when(j == STATE_BLOCKS - 1)
        def _():
            du_ref[...] = acc_ref[...]

    flat = pltpu.VMEM((rows, LANES), F32)
    big = pltpu.VMEM((nck * SEG, LANES), F32)
    small = pltpu.VMEM((nck, LANES), F32)
    bshape = jax.ShapeDtypeStruct((nb, LANES, LANES * STATE_BLOCKS), F32)
    cshape = jax.ShapeDtypeStruct((nb, LANES * STATE_BLOCKS, LANES), F32)
    ashape = jax.ShapeDtypeStruct((1, nb * LANES * STATE_BLOCKS), F32)
    return pl.pallas_call(
        body, name=name, grid=(nsteps,),
        in_specs=[chan, chan, chan, bmat, bmat, cmat, cmat, avec, avec, dvec],
        out_specs=[chan, bmat, bmat, cmat, cmat, avec, avec, dvec],
        out_shape=[jax.ShapeDtypeStruct((rows, e), F32), bshape, bshape, cshape, cshape, ashape, ashape,
                   jax.ShapeDtypeStruct((1, e), F32)],
        scratch_shapes=[big, big, big, big, small, small, small, small, flat, flat],
        compiler_params=_params("arbitrary"))(u, y, dgy, bre, bim, crt, cit, ar, ai, dd)


def _to_blockdiag_b(bbar, nb):
    eye = jnp.eye(GROUPS_PER_BLOCK, dtype=bbar.dtype)
    t = jnp.einsum('bgpc,gh->bgchp', bbar.reshape(nb, GROUPS_PER_BLOCK, SSM_STATE, SSM_GROUP), eye)
    return t.reshape(nb, LANES, GROUPS_PER_BLOCK * SSM_STATE)


def _from_blockdiag_b(dmat, nb):
    eye = jnp.eye(GROUPS_PER_BLOCK, dtype=dmat.dtype)
    t = dmat.reshape(nb, GROUPS_PER_BLOCK, SSM_GROUP, GROUPS_PER_BLOCK, SSM_STATE)
    return jnp.einsum('bgchp,gh->bgpc', t, eye).reshape(nb * GROUPS_PER_BLOCK, SSM_STATE, SSM_GROUP)


def _to_blockdiag_ct(c, nb):
    eye = jnp.eye(GROUPS_PER_BLOCK, dtype=c.dtype)
    t = jnp.einsum('bgop,gh->bgpho', c.reshape(nb, GROUPS_PER_BLOCK, SSM_GROUP, SSM_STATE), eye)
    return t.reshape(nb, GROUPS_PER_BLOCK * SSM_STATE, LANES)


def _from_blockdiag_ct(dmat, nb):
    eye = jnp.eye(GROUPS_PER_BLOCK, dtype=dmat.dtype)
    t = dmat.reshape(nb, GROUPS_PER_BLOCK, SSM_STATE, GROUPS_PER_BLOCK, SSM_GROUP)
    return jnp.einsum('bgpho,gh->bgop', t, eye).reshape(nb * GROUPS_PER_BLOCK, SSM_GROUP, SSM_STATE)


ANY = pl.BlockSpec(memory_space=pl.ANY)


def _half_specs(kind, rdim, cdim, tr, tc, layer):
    nr, nc = rdim // tr, cdim // tc
    if kind == 'col':
        nat = pl.BlockSpec((None, tr, tc), lambda c, rb, cb: (layer, c * nr + rb, cb))
    else:
        nat = pl.BlockSpec((None, tr, tc), lambda c, rb, cb: (layer, rb, c * nc + cb))
    half = pl.BlockSpec((None, tr, tc), lambda c, rb, cb: (c, rb, cb))
    return nat, half


def _my_chip():
    return 2 * lax.axis_index("x") + lax.axis_index("y")


def _cast_halves(name, w, kind, layer, deps=()):
    rdim, cdim = _half_shape(kind, w.shape)
    tr, tc = _tile(rdim, 512, 16), _tile(cdim, 1408, LANES)
    nat, _ = _half_specs(kind, rdim, cdim, tr, tc, layer)
    slot = pl.BlockSpec((None, None, tr, tc), lambda c, rb, cb: (_my_chip(), c, rb, cb))

    def body(w_ref, *rest):
        o_ref = rest[-1]
        o_ref[...] = w_ref[...].astype(o_ref.dtype)

    return pl.pallas_call(
        body, name=name, grid=(2, rdim // tr, cdim // tc), in_specs=[nat] + [ANY] * len(deps), out_specs=slot,
        out_shape=jax.ShapeDtypeStruct((4, 2, rdim, cdim), BF16),
        compiler_params=_params("parallel", "parallel", "parallel"))(w, *deps)


def _adam_math(w, g, m, v):
    m = ADAM_B1 * m + (1.0 - ADAM_B1) * g
    v = ADAM_B2 * v + (1.0 - ADAM_B2) * (g * g)
    m_hat = m / (1.0 - ADAM_B1 ** ADAM_STEP)
    v_hat = v / (1.0 - ADAM_B2 ** ADAM_STEP)
    delta = -ADAM_LR * (m_hat / (jnp.sqrt(v_hat) + ADAM_EPS) + ADAM_WD * w)
    return delta, m, v


def _adam_big(name, w, m, v, gfull, kind, layer, outs):
    rdim, cdim = _half_shape(kind, w.shape)
    tr, tc = _tile(rdim, 256, 8), _tile(cdim, 1408, LANES)
    nat, half = _half_specs(kind, rdim, cdim, tr, tc, layer)

    def body(w_ref, m_ref, v_ref, g_ref, *rest):
        go_ref, d_ref, mo_ref, vo_ref = rest[4:]
        g = g_ref[...]
        go_ref[...] = g
        d_ref[...], mo_ref[...], vo_ref[...] = _adam_math(w_ref[...], g, m_ref[...], v_ref[...])

    shape = jax.ShapeDtypeStruct(w.shape, F32)
    return pl.pallas_call(
        body, name=name, grid=(2, rdim // tr, cdim // tc), in_specs=[nat, nat, nat, half] + [ANY] * 4,
        out_specs=[nat, nat, nat, nat], out_shape=[shape, shape, shape, shape],
        input_output_aliases={4: 0, 5: 1, 6: 2, 7: 3},
        compiler_params=_params("parallel", "parallel", "parallel"))(w, m, v, gfull, *outs)


def _adam_small(w, m, v, g):
    rows = w.shape[0]
    tr = _tile(rows, 512, 8)
    spec = pl.BlockSpec((tr, LANES), lambda i: (i, 0))

    def body(w_ref, m_ref, v_ref, g_ref, d_ref, mo_ref, vo_ref):
        d_ref[...], mo_ref[...], vo_ref[...] = _adam_math(w_ref[...], g_ref[...], m_ref[...], v_ref[...])

    shape = jax.ShapeDtypeStruct(w.shape, F32)
    return pl.pallas_call(body, name="adam_small", grid=(rows // tr,), in_specs=[spec] * 4, out_specs=[spec] * 3,
                          out_shape=[shape] * 3, compiler_params=_params("parallel"))(w, m, v, g)


def _add2(name, part, got):
    cdim = part.shape[-1]
    a2, b2 = part.reshape(4, 2, -1, cdim), got.reshape(4, -1, cdim)
    rows = b2.shape[1]
    tr, tc = _tile(rows, 512, 16), _tile(cdim, 1408, LANES)
    mine = pl.BlockSpec((None, None, tr, tc), lambda k, i, j: (k, lax.axis_index("c"), i, j))
    spec = pl.BlockSpec((None, tr, tc), lambda k, i, j: (k, i, j))

    def body(a_ref, b_ref, o_ref):
        o_ref[...] = (a_ref[...].astype(F32) + b_ref[...].astype(F32)).astype(o_ref.dtype)

    out = pl.pallas_call(
        body, name=name, grid=(4, rows // tr, cdim // tc), in_specs=[mine, spec], out_specs=spec,
        out_shape=jax.ShapeDtypeStruct(b2.shape, BF16),
        compiler_params=_params("parallel", "parallel", "parallel"))(a2, b2)
    return out.reshape(got.shape)


def _add4(name, sums, recv):
    cdim = sums.shape[-1]
    s2 = sums.reshape(4, -1, cdim)
    r3 = recv.reshape(3, -1, cdim)
    rows = s2.shape[1]
    tr, tc = _tile(rows, 512, 16), _tile(cdim, 1408, LANES)
    own = pl.BlockSpec((None, tr, tc), lambda i, j: (_my_chip(), i, j))
    rspec = lambda k: pl.BlockSpec((None, tr, tc), lambda i, j: (k, i, j))
    slot = pl.BlockSpec((None, tr, tc), lambda i, j: (lax.axis_index("c"), i, j))

    def body(o_ref, x_ref, y_ref, d_ref, out_ref):
        out_ref[...] = ((o_ref[...].astype(F32) + d_ref[...].astype(F32))
                        + (x_ref[...].astype(F32) + y_ref[...].astype(F32)))

    out = pl.pallas_call(
        body, name=name, grid=(rows // tr, cdim // tc), in_specs=[own, rspec(0), rspec(1), rspec(2)], out_specs=slot,
        out_shape=jax.ShapeDtypeStruct((2, rows, cdim), F32),
        compiler_params=_params("parallel", "parallel"))(s2, r3, r3, r3)
    return out.reshape(2, *sums.shape[1:])


def _place():
    x, y, c = lax.axis_index("x"), lax.axis_index("y"), lax.axis_index("c")
    chips = [(1 - x, y), (x, 1 - y), (1 - x, 1 - y)]
    return x, y, c, chips


def _remote(src, dst, send, recv, to):
    return pltpu.make_async_remote_copy(src_ref=src, dst_ref=dst, send_sem=send, recv_sem=recv, device_id=to,
                                        device_id_type=MESH)


def _pieces(src, dst, bands):
    lead, rows = src.shape[:-2], src.shape[-2]
    band = rows // bands
    out = []
    for idx in itertools.product(*[range(dim) for dim in lead]):
        for q in range(bands):
            sl = (*idx, pl.ds(q * band, band))
            out.append((src.at[sl], dst.at[sl]))
    return out


HBM = pl.BlockSpec(memory_space=pltpu.HBM)
SEM = pl.BlockSpec(memory_space=pltpu.SEMAPHORE)
EFFECT = pltpu.SideEffectType.DATAFLOW_SIDE_EFFECTING


def _split_start(name, bufs, ncopy, plan, deps=()):
    nb, nd = len(bufs), len(deps)

    def body(*refs):
        ins, send, recv, token = refs[:nb], refs[nb + nd], refs[nb + nd + 1], refs[2 * nb + nd + 2]
        for k, (src, dst, _, to, bands) in enumerate(plan(ins)):
            for s, d in _pieces(src, dst, bands):
                _remote(s, d, send.at[k], recv.at[k], to).start()
        token[...] = jnp.zeros_like(token)

    res = pl.pallas_call(
        body, name=name, in_specs=[HBM] * nb + [ANY] * nd,
        out_specs=[SEM, SEM] + [HBM] * nb + [pl.BlockSpec(memory_space=pltpu.VMEM)],
        out_shape=[pltpu.SemaphoreType.DMA((ncopy,)), pltpu.SemaphoreType.DMA((ncopy,))]
        + [pltpu.HBM(b.shape, b.dtype) for b in bufs] + [jax.ShapeDtypeStruct((8, LANES), F32)],
        input_output_aliases={a: a + 2 for a in range(nb)},
        compiler_params=pltpu.CompilerParams(has_side_effects=EFFECT))(
        *[pltpu.with_memory_space_constraint(b, pltpu.HBM) for b in bufs], *deps)
    return res[0], res[1], list(res[2:2 + nb]), res[2 + nb]


def _split_wait(name, send, recv, bufs, plan, after):
    nb = len(bufs)

    def body(*refs):
        ins, send_sem, recv_sem = refs[:nb], refs[nb], refs[nb + 1]
        for k, (src, dst, landing, to, _) in enumerate(plan(ins)):
            _remote(src, dst, send_sem.at[k], recv_sem.at[k], to).wait_send()
            _remote(src, landing, send_sem.at[k], recv_sem.at[k], to).wait_recv()

    return pl.pallas_call(
        body, name=name, in_specs=[HBM] * nb + [SEM, SEM, ANY], out_specs=[HBM] * nb,
        out_shape=[pltpu.HBM(b.shape, b.dtype) for b in bufs], input_output_aliases={a: a for a in range(nb)},
        compiler_params=pltpu.CompilerParams(has_side_effects=EFFECT))(*bufs, send, recv, after)


def _gather_plan(n):
    def plan(refs):
        x, y, c, chips = _place()
        kme = 2 * x + y
        return [(refs[a].at[kme, c], refs[a].at[kme, c], refs[a].at[2 * chip[0] + chip[1], c], (*chip, c), 2)
                for a in range(n) for chip in chips]
    return plan


def _scatter_plan(n):
    def plan(refs):
        x, y, c, chips = _place()
        return [(refs[a].at[2 * chip[0] + chip[1]], refs[n + a].at[r], refs[n + a].at[r], (*chip, c), 2)
                for a in range(n) for r, chip in enumerate(chips)]
    return plan


def _forward_plan(n):
    def plan(refs):
        x, y, c, chips = _place()
        sib = (x, y, 1 - c)
        return [(refs[a].at[2 * chip[0] + chip[1], c], refs[a].at[2 * chip[0] + chip[1], c],
                 refs[a].at[2 * chip[0] + chip[1], 1 - c], sib, 2) for a in range(n) for chip in chips]
    return plan


def _join_plan(n):
    def plan(refs):
        x, y, c, _ = _place()
        return [(refs[a].at[c], refs[a].at[c], refs[a].at[1 - c], (x, y, 1 - c), 4) for a in range(n)]
    return plan


def _allgather_small(shards):
    n = len(shards)

    def body(*refs):
        ins, outs = refs[:n], refs[n:2 * n]
        send, recv, loc = refs[2 * n:]
        x, y, c, chips = _place()
        kme = 2 * x + y
        local = [pltpu.make_async_copy(ins[a], outs[a].at[kme], loc.at[a]) for a in range(n)]
        for cp in local:
            cp.start()
        cps = [_remote(ins[a], outs[a].at[kme], send.at[3 * a + r], recv.at[3 * a + r], (*chip, c))
               for a in range(n) for r, chip in enumerate(chips)]
        for cp in cps:
            cp.start()
        for a in range(n):
            for r, chip in enumerate(chips):
                kp = 2 * chip[0] + chip[1]
                _remote(ins[a], outs[a].at[kp], send.at[3 * a + r], recv.at[3 * a + r], (*chip, c)).wait_recv()
        for cp in cps:
            cp.wait_send()
        for cp in local:
            cp.wait()

    return pl.pallas_call(
        body, name="allgather_small", in_specs=[ANY] * n, out_specs=[ANY] * n,
        out_shape=[jax.ShapeDtypeStruct((4, *s.shape), s.dtype) for s in shards],
        scratch_shapes=[pltpu.SemaphoreType.DMA((3 * n,)), pltpu.SemaphoreType.DMA((3 * n,)),
                        pltpu.SemaphoreType.DMA((n,))])(*shards)


def _swap_plan(n):
    def plan(refs):
        x, y, c, _ = _place()
        return [(refs[a].at[:, 1 - c], refs[n + a], refs[n + a], (x, y, 1 - c), 1) for a in range(n)]
    return plan


def _join_halves(totals):
    n = len(totals)

    def body(*refs):
        outs = refs[n:2 * n]
        send, recv = refs[2 * n:]
        x, y, c, _ = _place()
        sib = (x, y, 1 - c)
        for a in range(n):
            for s, d in _pieces(outs[a].at[c], outs[a].at[c], 4):
                _remote(s, d, send.at[a], recv.at[a], sib).start()
        for a in range(n):
            _remote(outs[a].at[1 - c], outs[a].at[1 - c], send.at[a], recv.at[a], sib).wait_recv()
            _remote(outs[a].at[c], outs[a].at[c], send.at[a], recv.at[a], sib).wait_send()

    return pl.pallas_call(
        body, name="grad_join_halves", in_specs=[ANY] * n, out_specs=[ANY] * n,
        out_shape=[jax.ShapeDtypeStruct(t.shape, t.dtype) for t in totals],
        input_output_aliases={a: a for a in range(n)},
        scratch_shapes=[pltpu.SemaphoreType.DMA((n,)), pltpu.SemaphoreType.DMA((n,))])(*totals)


def _allreduce_small(packed):
    rows = packed.shape[0]
    half = rows // 2

    def body(in_ref, out_ref, q_ref, s_ref, t_ref, send, recv):
        x, y, c, chips = _place()
        sib = (x, y, 1 - c)
        mine = pl.ds(pl.multiple_of(c * half, 8), half)
        theirs = pl.ds(pl.multiple_of((1 - c) * half, 8), half)
        first = _remote(in_ref.at[theirs], q_ref, send.at[0], recv.at[0], sib)
        first.start()
        first.wait()
        s_ref[...] = in_ref[mine, :] + q_ref[...]
        cps = [_remote(s_ref, t_ref.at[r], send.at[1 + r], recv.at[1 + r], (*chip, c)) for r, chip in enumerate(chips)]
        for cp in cps:
            cp.start()
        for cp in cps:
            cp.wait()
        out_ref[mine, :] = (s_ref[...] + t_ref[2]) + (t_ref[0] + t_ref[1])
        last = _remote(out_ref.at[mine], out_ref.at[mine], send.at[4], recv.at[4], sib)
        last.start()
        _remote(out_ref.at[theirs], out_ref.at[theirs], send.at[4], recv.at[4], sib).wait_recv()
        last.wait_send()

    vm = pl.BlockSpec(memory_space=pltpu.VMEM)
    return pl.pallas_call(
        body, name="allreduce_small", in_specs=[vm], out_specs=vm, out_shape=jax.ShapeDtypeStruct(packed.shape, F32),
        scratch_shapes=[pltpu.VMEM((half, LANES), F32), pltpu.VMEM((half, LANES), F32),
                        pltpu.VMEM((3, half, LANES), F32), pltpu.SemaphoreType.DMA((5,)), pltpu.SemaphoreType.DMA((5,))],
        compiler_params=pltpu.CompilerParams(vmem_limit_bytes=VMEM_LIMIT))(packed)


PACK_ROWS = 16
PACK_BLOCK = 512


def _pack(arrs):
    parts, total = [], 0
    for a in arrs:
        flat = a.reshape(-1)
        rows = -(-flat.shape[0] // (LANES * PACK_ROWS)) * PACK_ROWS
        parts.append(jnp.pad(flat, (0, rows * LANES - flat.shape[0])).reshape(rows, LANES))
        total += rows
    tail = -total % PACK_BLOCK
    if tail:
        parts.append(jnp.zeros((tail, LANES), parts[0].dtype))
    return jnp.concatenate(parts, axis=0)


def _unpack(packed, shapes):
    out, row = [], 0
    for shp in shapes:
        size = math.prod(shp)
        rows = -(-size // (LANES * PACK_ROWS)) * PACK_ROWS
        out.append(packed[row:row + rows].reshape(-1)[:size].reshape(shp))
        row += rows
    return out


def kernel(x, norm_mix_g, norm_ffn_g, a_w_in, a_g_v, a_w_s, a_b_s, a_w_out, b_w_in, b_a_re, b_a_im, b_log_dt, b_b_re, b_b_im, b_c_re, b_c_im, b_d, b_w_glu, f_w_up, f_conv_w, f_conv_b, f_w_down, final_g, loss_target, m_norm_mix_g, m_norm_ffn_g, m_a_w_in, m_a_g_v, m_a_w_s, m_a_b_s, m_a_w_out, m_b_w_in, m_b_a_re, m_b_a_im, m_b_log_dt, m_b_b_re, m_b_b_im, m_b_c_re, m_b_c_im, m_b_d, m_b_w_glu, m_f_w_up, m_f_conv_w, m_f_conv_b, m_f_w_down, m_final_g, v_norm_mix_g, v_norm_ffn_g, v_a_w_in, v_a_g_v, v_a_w_s, v_a_b_s, v_a_w_out, v_b_w_in, v_b_a_re, v_b_a_im, v_b_log_dt, v_b_b_re, v_b_b_im, v_b_c_re, v_b_c_im, v_b_d, v_b_w_glu, v_f_w_up, v_f_conv_w, v_f_conv_b, v_f_w_down, v_final_g):
    w = dict(norm_mix_g=norm_mix_g, norm_ffn_g=norm_ffn_g, a_w_in=a_w_in, a_g_v=a_g_v, a_w_s=a_w_s, a_b_s=a_b_s,
             a_w_out=a_w_out, b_w_in=b_w_in, b_a_re=b_a_re, b_a_im=b_a_im, b_log_dt=b_log_dt, b_b_re=b_b_re,
             b_b_im=b_b_im, b_c_re=b_c_re, b_c_im=b_c_im, b_d=b_d, b_w_glu=b_w_glu, f_w_up=f_w_up, f_conv_w=f_conv_w,
             f_conv_b=f_conv_b, f_w_down=f_w_down, final_g=final_g)
    mom = dict(norm_mix_g=m_norm_mix_g, norm_ffn_g=m_norm_ffn_g, a_w_in=m_a_w_in, a_g_v=m_a_g_v, a_w_s=m_a_w_s,
               a_b_s=m_a_b_s, a_w_out=m_a_w_out, b_w_in=m_b_w_in, b_a_re=m_b_a_re, b_a_im=m_b_a_im,
               b_log_dt=m_b_log_dt, b_b_re=m_b_b_re, b_b_im=m_b_b_im, b_c_re=m_b_c_re, b_c_im=m_b_c_im, b_d=m_b_d,
               b_w_glu=m_b_w_glu, f_w_up=m_f_w_up, f_conv_w=m_f_conv_w, f_conv_b=m_f_conv_b, f_w_down=m_f_w_down,
               final_g=m_final_g)
    var = dict(norm_mix_g=v_norm_mix_g, norm_ffn_g=v_norm_ffn_g, a_w_in=v_a_w_in, a_g_v=v_a_g_v, a_w_s=v_a_w_s,
               a_b_s=v_a_b_s, a_w_out=v_a_w_out, b_w_in=v_b_w_in, b_a_re=v_b_a_re, b_a_im=v_b_a_im,
               b_log_dt=v_b_log_dt, b_b_re=v_b_b_re, b_b_im=v_b_b_im, b_c_re=v_b_c_re, b_c_im=v_b_c_im, b_d=v_b_d,
               b_w_glu=v_b_w_glu, f_w_up=v_f_w_up, f_conv_w=v_f_conv_w, f_conv_b=v_f_conv_b, f_w_down=v_f_w_down,
               final_g=v_final_g)

    rows, d = x.shape[1], x.shape[2]
    depth = norm_mix_g.shape[0]
    kchip = 2 * lax.axis_index("x") + lax.axis_index("y")
    big_names = list(BIG)
    dims = {n: _full_dims(BIG[n], w[n].shape) for n in big_names}

    keys = [(n, l) for n in big_names for l in range(w[n].shape[0])]

    def group_keys(g):
        i = g // 2
        if g % 2 == 1:
            return [('f_w_up', i), ('f_w_down', i)]
        return [('a_w_in', i // 2), ('a_w_out', i // 2)] if i % 2 == 0 else [('b_w_in', i // 2), ('b_w_glu', i // 2)]

    bd_all, cw_all = _allgather_small([b_d, f_conv_w.reshape(-1, f_conv_w.shape[-1])])

    gathered, gather_waits, tokens = {}, [], [bd_all]
    for g in range(2 * depth):
        arrs = [_cast_halves("cast_" + n, w[n], BIG[n], l, deps=tuple(tokens[-2:-1])) for n, l in group_keys(g)]
        send, recv, thru, token = _split_start(f"allgather_start_{g}", arrs, 3 * len(arrs), _gather_plan(len(arrs)),
                                               deps=(tokens[-1],))
        tokens.append(token)
        gather_waits.append((send, recv, thru))
    gather_after = tokens[-1]

    forwards = {}

    def prefetch_group(g, after):
        send, recv, thru = gather_waits[g]
        landed = _split_wait(f"allgather_wait_{g}", send, recv, thru, _gather_plan(len(thru)), after)
        send, recv, thru, token = _split_start(f"allgather_pass_start_{g}", landed, 3 * len(landed),
                                               _forward_plan(len(landed)))
        forwards[g] = (send, recv, thru)
        return (token,)

    def ready_group(g, after):
        send, recv, thru = forwards.pop(g)
        arrs = _split_wait(f"allgather_pass_wait_{g}", send, recv, thru, _forward_plan(len(thru)), after)
        gathered.update(zip(group_keys(g), arrs))

    bd_full = jnp.swapaxes(bd_all, 0, 1).reshape(b_d.shape[0], -1)
    cw_full = jnp.transpose(cw_all.reshape(4, *f_conv_w.shape), (1, 2, 0, 3)).reshape(depth, f_conv_w.shape[1], -1)

    pgrad = {}
    sgrad = {}

    def mm(name, a, wn, layer, out_dtype, residual=None, split=False, deps=()):
        return _mm_x_w(name, a, gathered[wn, layer], BIG[wn], *dims[wn], out_dtype, residual, split, deps)

    def mm_t(name, dy, wn, layer, out_dtype, split=False, deps=()):
        return _mm_dy_wt(name, dy, gathered[wn, layer], BIG[wn], *dims[wn], out_dtype, split, deps)

    def mm_g(name, xa, dy, wn, layer, split=False):
        pgrad[wn, layer] = _mm_xt_dy(name, xa, dy, BIG[wn], *dims[wn], split)

    e = d
    nb = e // LANES
    heads = e // SGU_GROUP
    h = x[0]
    saved = []
    for i in range(depth):
        j = i // 2
        if i == 0:
            prefetch_group(0, gather_after)
        ready_group(2 * i, gather_after if i == 0 else h)
        gm = norm_mix_g[i:i + 1]
        hn = _rms_fwd("rms_mix_fwd", h, gm)
        if i % 2 == 0:
            pre = mm("sgu_in", hn, 'a_w_in', j, BF16)
            passing = prefetch_group(2 * i + 1, pre)
            bsx = jnp.broadcast_to(a_b_s[j][:, :, None], (heads, CHUNK, LANES))
            us = _sgu_mix_fwd("sgu_mix_fwd", pre, a_g_v[j:j + 1], a_w_s[j], bsx)
            h_mid = mm("sgu_out", us, 'a_w_out', j, F32, residual=h, deps=passing)
            mix = dict(h=h, hn=hn, pre=pre, us=us, bsx=bsx)
        else:
            groups = b_a_re.shape[1]
            rep = lambda t: jnp.repeat(t, SSM_GROUP, axis=1)
            lr, li = b_a_re[j], b_a_im[j]
            ldt = jnp.broadcast_to(b_log_dt[j][:, None], lr.shape)
            bflat = lambda t: t.reshape(groups, SSM_STATE * SSM_GROUP)
            disc_in = (lr, li, ldt, rep(lr), rep(li), rep(ldt), bflat(b_b_re[j]), bflat(b_b_im[j]))
            abr, abi, bbr, bbi = _s5_disc("s5_disc", *disc_in)
            shape_b = (groups, SSM_STATE, SSM_GROUP)
            bre = _to_blockdiag_b(bbr.reshape(shape_b), nb).astype(BF16)
            bim = _to_blockdiag_b(bbi.reshape(shape_b), nb).astype(BF16)
            crt = _to_blockdiag_ct(b_c_re[j], nb).astype(BF16)
            cit = _to_blockdiag_ct(b_c_im[j], nb).astype(BF16)
            ar_row, ai_row = abr.reshape(1, -1), abi.reshape(1, -1)
            dd = bd_full[j:j + 1]
            u = mm("s5_in", hn, 'b_w_in', j, F32)
            passing = prefetch_group(2 * i + 1, u)
            yv, gy = _s5_fwd("s5_fwd", u, bre, bim, crt, cit, ar_row, ai_row, dd)
            gg = mm("s5_glu", gy, 'b_w_glu', j, BF16, deps=passing)
            h_mid = _glu_fwd("glu_fwd", gg, h)
            mix = dict(h=h, hn=hn, u=u, y=yv, gy=gy, gg=gg, disc_in=disc_in, mats=(bre, bim, crt, cit, ar_row, ai_row, dd))
        ready_group(2 * i + 1, h_mid)
        gf = norm_ffn_g[i:i + 1]
        hn2 = _rms_fwd("rms_ffn_fwd", h_mid, gf)
        z = mm("ffn_up", hn2, 'f_w_up', i, BF16, split=True)
        passing = prefetch_group(2 * i + 2, z) if i + 1 < depth else ()
        cw = jnp.swapaxes(cw_full[i].reshape(cw_full.shape[1], 2, -1), 0, 1)
        cb = f_conv_b[i].reshape(2, 1, -1)
        act = _ffn_act_fwd("ffn_act_fwd", z, cw, cb)
        h_out = mm("ffn_down", act, 'f_w_down', i, F32, residual=h_mid, deps=passing)
        saved.append((mix, dict(h=h_mid, hn=hn2, z=z, act=act, cw=cw, cb=cb)))
        h = h_out

    dh, g_final, loss_vec = _loss_head(h, final_g.reshape(1, d), loss_target[0])
    loss = lax.psum(jnp.sum(loss_vec), ("x", "y", "c"))
    sgrad['final_g'] = g_final.reshape(d)

    g_mix, g_ffn = [None] * depth, [None] * depth
    g_cw, g_cb = [None] * depth, [None] * depth
    sg = {k: [None] * (depth // 2) for k in ('a_g_v', 'a_w_s', 'a_b_s')}
    bg = {k: [None] * (depth // 2) for k in ('b_a_re', 'b_a_im', 'b_log_dt', 'b_b_re', 'b_b_im', 'b_c_re', 'b_c_im', 'b_d')}
    scatters, flight = {}, dict(pending=None, token=())

    def swap_group(g):
        parts = [pgrad[k] for k in group_keys(g)]
        land = [lax.empty((4, *p.shape[2:]), BF16) for p in parts]
        send, recv, thru, token = _split_start(f"grad_swap_start_{g}", parts + land, len(parts), _swap_plan(len(parts)))
        flight['swap'] = (send, recv, thru)
        return (token,)

    def scatter_group(g, done):
        if flight['pending'] is not None:
            prev, send, recv, thru = flight['pending']
            scatters[prev] = _split_wait(f"grad_scatter_wait_{prev}", send, recv, thru, _scatter_plan(len(thru) // 2), done)
        send, recv, thru = flight.pop('swap')
        thru = _split_wait(f"grad_swap_wait_{g}", send, recv, thru, _swap_plan(len(thru) // 2), done)
        n = len(thru) // 2
        sums = [_add2("grad_chip_sum", p, q) for p, q in zip(thru[:n], thru[n:])]
        land = [lax.empty((3, *s.shape[1:]), BF16) for s in sums]
        send, recv, thru, token = _split_start(f"grad_scatter_start_{g}", sums + land, 3 * len(sums),
                                               _scatter_plan(len(sums)))
        flight['pending'], flight['token'] = (g, send, recv, thru), (token,)

    for i in reversed(range(depth)):
        j = i // 2
        mix, ffn = saved[i]
        d_act = mm_t("ffn_down_dx", dh, 'f_w_down', i, BF16, deps=flight['token'])
        mm_g("ffn_down_dw", ffn['act'], dh, 'f_w_down', i)
        dz, dcw, dcb = _ffn_act_bwd("ffn_act_bwd", ffn['z'], d_act, ffn['cw'], ffn['cb'])
        g_cw[i], g_cb[i] = jnp.swapaxes(dcw, 0, 1).reshape(dcw.shape[1], -1), dcb.reshape(1, -1)
        mm_g("ffn_up_dw", ffn['hn'], dz, 'f_w_up', i, split=True)
        dhn = mm_t("ffn_up_dx", dz, 'f_w_up', i, F32, split=True, deps=swap_group(2 * i + 1))
        dh, g_ffn[i] = _rms_bwd("rms_ffn_bwd", ffn['h'], norm_ffn_g[i:i + 1], dhn, dh)
        scatter_group(2 * i + 1, dh)
        if i % 2 == 0:
            dus = mm_t("sgu_out_dx", dh, 'a_w_out', j, BF16, deps=flight['token'])
            mm_g("sgu_out_dw", mix['us'], dh, 'a_w_out', j)
            dpre, dws, dbs, dgv = _sgu_mix_bwd("sgu_mix_bwd", mix['pre'], dus, a_g_v[j:j + 1], a_w_s[j], mix['bsx'])
            sg['a_w_s'][j], sg['a_b_s'][j], sg['a_g_v'][j] = dws, dbs[:, :, 0], dgv[0]
            mm_g("sgu_in_dw", mix['hn'], dpre, 'a_w_in', j)
            dhn = mm_t("sgu_in_dx", dpre, 'a_w_in', j, F32, deps=swap_group(2 * i))
        else:
            dgg = _glu_bwd("glu_bwd", mix['gg'], dh)
            mm_g("s5_glu_dw", mix['gy'], dgg, 'b_w_glu', j)
            dgy = mm_t("s5_glu_dx", dgg, 'b_w_glu', j, BF16, deps=flight['token'])
            du, dbr, dbi, dcr, dci, dar, dai, ddd = _s5_bwd("s5_bwd", mix['u'], mix['y'], dgy, *mix['mats'])
            groups = b_a_re.shape[1]
            flat = lambda t: _from_blockdiag_b(t, nb).reshape(groups, SSM_STATE * SSM_GROUP)
            sel = jnp.repeat(jnp.eye(SSM_STATE, dtype=F32), SSM_GROUP, axis=0)
            dlr, dli, dldt, dbre, dbim = _s5_disc_bwd(
                "s5_disc_bwd", *mix['disc_in'], dar.reshape(groups, SSM_STATE), dai.reshape(groups, SSM_STATE),
                flat(dbr), flat(dbi), sel)
            bg['b_a_re'][j], bg['b_a_im'][j], bg['b_log_dt'][j] = dlr, dli, dldt[:, 0]
            bg['b_b_re'][j] = dbre.reshape(groups, SSM_STATE, SSM_GROUP)
            bg['b_b_im'][j] = dbim.reshape(groups, SSM_STATE, SSM_GROUP)
            bg['b_c_re'][j], bg['b_c_im'][j] = _from_blockdiag_ct(dcr, nb), _from_blockdiag_ct(dci, nb)
            bg['b_d'][j] = ddd[0]
            mm_g("s5_in_dw", mix['hn'], du, 'b_w_in', j)
            dhn = mm_t("s5_in_dx", du, 'b_w_in', j, F32, deps=swap_group(2 * i))
        dh, g_mix[i] = _rms_bwd("rms_mix_bwd", mix['h'], norm_mix_g[i:i + 1], dhn, dh)
        scatter_group(2 * i, dh)
    grad_x = dh[None]

    sgrad['norm_mix_g'] = jnp.concatenate(g_mix, axis=0)
    sgrad['norm_ffn_g'] = jnp.concatenate(g_ffn, axis=0)
    sgrad['f_conv_w'] = jnp.stack(g_cw)
    sgrad['f_conv_b'] = jnp.concatenate(g_cb, axis=0)
    for k, v_ in list(sg.items()) + list(bg.items()):
        sgrad[k] = jnp.stack(v_)

    total = _allreduce_small(_pack([sgrad[n] for n in SMALL]))
    full_shapes = [sgrad[n].shape for n in SMALL]
    gsmall = dict(zip(SMALL, _unpack(total, full_shapes)))
    for n, axis in CHIP_SHARDED_SMALL.items():
        width = w[n].shape[axis]
        gsmall[n] = lax.dynamic_slice_in_dim(gsmall[n], kchip * width, width, axis=axis)
    pk = lambda t: _pack([t[n] for n in SMALL])
    gpacked = pk(gsmall)
    dpk, mpk, vpk = _adam_small(pk(w), pk(mom), pk(var), gpacked)
    shard_shapes = [w[n].shape for n in SMALL]
    out_g = dict(gsmall)
    out_d = dict(zip(SMALL, _unpack(dpk, shard_shapes)))
    out_m = dict(zip(SMALL, _unpack(mpk, shard_shapes)))
    out_v = dict(zip(SMALL, _unpack(vpk, shard_shapes)))

    def group_totals(g):
        n = len(scatters[g]) // 2
        return [_add4("grad_total", s, r) for s, r in zip(scatters[g][:n], scatters[g][n:])]

    late_keys = [k for g in range(1, 2 * depth) for k in group_keys(g)]
    late = [t for g in range(1, 2 * depth) for t in group_totals(g)]
    send, recv, late, _ = _split_start("grad_join_start", late, len(late), _join_plan(len(late)))
    _, s0, r0, thru0 = flight['pending']
    scatters[0] = _split_wait("grad_scatter_wait_0", s0, r0, thru0, _scatter_plan(len(thru0) // 2), dpk)
    first = _join_halves(group_totals(0))
    late = _split_wait("grad_join_wait", send, recv, late, _join_plan(len(late)), first[0])
    gfull = dict(zip(group_keys(0) + late_keys, list(first) + list(late)))
    stacked = {n: [lax.empty(w[n].shape, F32) for _ in range(4)] for n in big_names}
    for n, l in keys:
        stacked[n] = _adam_big("adam_" + n, w[n], mom[n], var[n], gfull[n, l], BIG[n], l, stacked[n])
    for n in big_names:
        out_g[n], out_d[n], out_m[n], out_v[n] = stacked[n]

    return (loss, grad_x, *[out_g[n] for n in W_NAMES], *[out_d[n] for n in W_NAMES],
            *[out_m[n] for n in W_NAMES], *[out_v[n] for n in W_NAMES])
```
